```python
import math
import jax, jax.numpy as jnp
from jax import lax
import numpy as np

D_MODEL = 1024
BATCH = 8
SEQ = 4096
DEPTH = 2

CHUNK = 64
N_META = 16
N_SB_HEADS = 8
SB_HEAD_DIM = 64
SB_WIDTH = N_SB_HEADS * SB_HEAD_DIM
CONV_CHANNELS = 512
CONV_KERNEL = 31
MIX_WIDTH = SB_WIDTH + CONV_CHANNELS
IN_COLS = 3 * SB_WIDTH + 2 * CONV_CHANNELS
D_FF = ((8 * D_MODEL // 3 + 255) // 256) * 256
QUERY_BLOCK = 128
EPS = 1e-6

kernel_name = "hybrid_stickbreak_conformer_conv_block"


def _rmsnorm(x, g):
    xf = x.astype(jnp.float32)
    y = xf * lax.rsqrt(jnp.mean(xf * xf, axis=-1, keepdims=True) + EPS)
    return (y * g.astype(jnp.float32)).astype(x.dtype)


def _layernorm(x, g, b):
    xf = x.astype(jnp.float32)
    mu = jnp.mean(xf, axis=-1, keepdims=True)
    var = jnp.mean(jnp.square(xf - mu), axis=-1, keepdims=True)
    y = (xf - mu) * lax.rsqrt(var + EPS)
    return (y * g.astype(jnp.float32) + b.astype(jnp.float32)).astype(x.dtype)


def _stick_breaking_attention(q, k, v):
    b, l, h, dh = q.shape
    n_blocks = -(-l // QUERY_BLOCK)
    lp = n_blocks * QUERY_BLOCK
    pad = [(0, 0), (0, lp - l), (0, 0), (0, 0)]
    q = jnp.pad(q, pad)
    k = jnp.pad(k, pad)
    v = jnp.pad(v, pad)
    scale = 1.0 / math.sqrt(dh)
    q_blocks = q.reshape(b, n_blocks, QUERY_BLOCK, h, dh).transpose(1, 0, 2, 3, 4)
    starts = jnp.arange(n_blocks, dtype=jnp.int32) * QUERY_BLOCK
    key_pos = jnp.arange(lp, dtype=jnp.int32)

    def one_block(args):
        qi, start = args
        z = jnp.einsum('bqhd,bkhd->bhqk', qi, k).astype(jnp.float32) * scale
        t = start + jnp.arange(QUERY_BLOCK, dtype=jnp.int32)
        mask = key_pos[None, :] < t[:, None]
        log_beta = jax.nn.log_sigmoid(z)
        log_1m_beta = jnp.where(mask, jax.nn.log_sigmoid(-z), 0.0)
        rev = lax.cumsum(log_1m_beta, axis=3, reverse=True)
        excl = jnp.concatenate([rev[..., 1:], jnp.zeros_like(rev[..., :1])], axis=-1)
        w = jnp.where(mask, jnp.exp(log_beta + excl), 0.0)
        return jnp.einsum('bhqk,bkhd->bqhd', w.astype(v.dtype), v)

    out = lax.map(one_block, (q_blocks, starts))
    out = out.transpose(1, 0, 2, 3, 4).reshape(b, lp, h, dh)
    return out[:, :l]


def _conformer_conv(a, gate, dw_w, dw_b, ln_g, ln_b):
    u = a * jax.nn.sigmoid(gate)
    c = u.shape[-1]
    u = lax.conv_general_dilated(
        u, dw_w.astype(u.dtype)[:, None, :], window_strides=(1,),
        padding=[(CONV_KERNEL - 1, 0)],
        dimension_numbers=('NWC', 'WIO', 'NWC'), feature_group_count=c)
    u = u + dw_b.astype(u.dtype)
    u = _layernorm(u, ln_g, ln_b)
    return jax.nn.silu(u)


def _fwd_setup_inputs(seed: int = 0) -> dict:
    key = jax.random.key(seed)
    ks = jax.random.split(key, 16)
    f32 = jnp.float32
    nrm = lambda k, shape, s: jax.random.normal(k, shape, f32) * s
    return {
        "x": jax.random.normal(ks[0], (BATCH, SEQ, D_MODEL), f32),
        "meta_tokens": nrm(ks[1], (N_META, D_MODEL), 1.0),
        "mix_norm_g": 1.0 + nrm(ks[2], (DEPTH, D_MODEL), 0.02),
        "w_in": nrm(ks[3], (DEPTH, D_MODEL, IN_COLS), D_MODEL ** -0.5),
        "conv_dw_w": nrm(ks[4], (DEPTH, CONV_KERNEL, CONV_CHANNELS), CONV_KERNEL ** -0.5),
        "conv_dw_b": nrm(ks[5], (DEPTH, CONV_CHANNELS), 0.02),
        "conv_ln_g": 1.0 + nrm(ks[6], (DEPTH, CONV_CHANNELS), 0.02),
        "conv_ln_b": nrm(ks[7], (DEPTH, CONV_CHANNELS), 0.02),
        "w_out": nrm(ks[8], (DEPTH, MIX_WIDTH, D_MODEL), MIX_WIDTH ** -0.5),
        "ffn_norm_g": 1.0 + nrm(ks[9], (DEPTH, D_MODEL), 0.02),
        "w_gate": nrm(ks[10], (DEPTH, D_MODEL, D_FF), D_MODEL ** -0.5),
        "w_up": nrm(ks[11], (DEPTH, D_MODEL, D_FF), D_MODEL ** -0.5),
        "w_down": nrm(ks[12], (DEPTH, D_FF, D_MODEL), D_FF ** -0.5),
        "final_norm_g": 1.0 + nrm(ks[13], (D_MODEL,), 0.02),
    }


def _fwd_reference(x, meta_tokens, mix_norm_g, w_in, conv_dw_w, conv_dw_b, conv_ln_g, conv_ln_b,
              w_out, ffn_norm_g, w_gate, w_up, w_down, final_norm_g):
    b = x.shape[0]
    meta = jnp.broadcast_to(meta_tokens.astype(x.dtype)[None], (b, N_META, D_MODEL))
    h = jnp.concatenate([meta, x], axis=1)
    l = h.shape[1]
    for i in range(DEPTH):
        hn = _rmsnorm(h, mix_norm_g[i])
        proj = jnp.einsum('bld,dc->blc', hn, w_in[i])
        q, k, v, ca, cg = jnp.split(
            proj, [SB_WIDTH, 2 * SB_WIDTH, 3 * SB_WIDTH, 3 * SB_WIDTH + CONV_CHANNELS], axis=-1)
        heads = lambda t: t.reshape(b, l, N_SB_HEADS, SB_HEAD_DIM)
        attn = _stick_breaking_attention(heads(q), heads(k), heads(v)).reshape(b, l, SB_WIDTH)
        conv = _conformer_conv(ca, cg, conv_dw_w[i], conv_dw_b[i], conv_ln_g[i], conv_ln_b[i])
        mixed = jnp.concatenate([attn, conv], axis=-1)
        h = h + jnp.einsum('blc,cd->bld', mixed, w_out[i])
        hn = _rmsnorm(h, ffn_norm_g[i])
        g = jnp.einsum('bld,df->blf', hn, w_gate[i])
        u = jnp.einsum('bld,df->blf', hn, w_up[i])
        h = h + jnp.einsum('blf,fd->bld', jax.nn.silu(g) * u, w_down[i])
    h = _rmsnorm(h, final_norm_g)
    return h[:, N_META:]


import jax as _jax
import jax.numpy as _jnp

TWIN_FORMAT = 'train_step'
FWD_PARAMS = ['x', 'meta_tokens', 'mix_norm_g', 'w_in', 'conv_dw_w', 'conv_dw_b', 'conv_ln_g', 'conv_ln_b', 'w_out', 'ffn_norm_g', 'w_gate', 'w_up', 'w_down', 'final_norm_g']
TWIN_WEIGHTS = ['meta_tokens', 'mix_norm_g', 'w_in', 'conv_dw_w', 'conv_dw_b', 'conv_ln_g', 'conv_ln_b', 'w_out', 'ffn_norm_g', 'w_gate', 'w_up', 'w_down', 'final_norm_g']
TWIN_DIFF_INPUT = 'x'
TWIN_INPUTS = ['x', 'meta_tokens', 'mix_norm_g', 'w_in', 'conv_dw_w', 'conv_dw_b', 'conv_ln_g', 'conv_ln_b', 'w_out', 'ffn_norm_g', 'w_gate', 'w_up', 'w_down', 'final_norm_g', 'loss_target', 'm_meta_tokens', 'm_mix_norm_g', 'm_w_in', 'm_conv_dw_w', 'm_conv_dw_b', 'm_conv_ln_g', 'm_conv_ln_b', 'm_w_out', 'm_ffn_norm_g', 'm_w_gate', 'm_w_up', 'm_w_down', 'm_final_norm_g', 'v_meta_tokens', 'v_mix_norm_g', 'v_w_in', 'v_conv_dw_w', 'v_conv_dw_b', 'v_conv_ln_g', 'v_conv_ln_b', 'v_w_out', 'v_ffn_norm_g', 'v_w_gate', 'v_w_up', 'v_w_down', 'v_final_norm_g']
TWIN_OUTPUTS = ['loss', 'grad_x', 'grad_meta_tokens', 'grad_mix_norm_g', 'grad_w_in', 'grad_conv_dw_w', 'grad_conv_dw_b', 'grad_conv_ln_g', 'grad_conv_ln_b', 'grad_w_out', 'grad_ffn_norm_g', 'grad_w_gate', 'grad_w_up', 'grad_w_down', 'grad_final_norm_g', 'delta_meta_tokens', 'delta_mix_norm_g', 'delta_w_in', 'delta_conv_dw_w', 'delta_conv_dw_b', 'delta_conv_ln_g', 'delta_conv_ln_b', 'delta_w_out', 'delta_ffn_norm_g', 'delta_w_gate', 'delta_w_up', 'delta_w_down', 'delta_final_norm_g', 'new_m_meta_tokens', 'new_m_mix_norm_g', 'new_m_w_in', 'new_m_conv_dw_w', 'new_m_conv_dw_b', 'new_m_conv_ln_g', 'new_m_conv_ln_b', 'new_m_w_out', 'new_m_ffn_norm_g', 'new_m_w_gate', 'new_m_w_up', 'new_m_w_down', 'new_m_final_norm_g', 'new_v_meta_tokens', 'new_v_mix_norm_g', 'new_v_w_in', 'new_v_conv_dw_w', 'new_v_conv_dw_b', 'new_v_conv_ln_g', 'new_v_conv_ln_b', 'new_v_w_out', 'new_v_ffn_norm_g', 'new_v_w_gate', 'new_v_w_up', 'new_v_w_down', 'new_v_final_norm_g']
TWIN_LEAF_KINDS = {'loss': 'loss', 'grad_x': 'grad_x', 'grad_meta_tokens': 'grad_w', 'grad_mix_norm_g': 'grad_w', 'grad_w_in': 'grad_w', 'grad_conv_dw_w': 'grad_w', 'grad_conv_dw_b': 'grad_w', 'grad_conv_ln_g': 'grad_w', 'grad_conv_ln_b': 'grad_w', 'grad_w_out': 'grad_w', 'grad_ffn_norm_g': 'grad_w', 'grad_w_gate': 'grad_w', 'grad_w_up': 'grad_w', 'grad_w_down': 'grad_w', 'grad_final_norm_g': 'grad_w', 'delta_meta_tokens': 'delta_w', 'delta_mix_norm_g': 'delta_w', 'delta_w_in': 'delta_w', 'delta_conv_dw_w': 'delta_w', 'delta_conv_dw_b': 'delta_w', 'delta_conv_ln_g': 'delta_w', 'delta_conv_ln_b': 'delta_w', 'delta_w_out': 'delta_w', 'delta_ffn_norm_g': 'delta_w', 'delta_w_gate': 'delta_w', 'delta_w_up': 'delta_w', 'delta_w_down': 'delta_w', 'delta_final_norm_g': 'delta_w', 'new_m_meta_tokens': 'new_m', 'new_m_mix_norm_g': 'new_m', 'new_m_w_in': 'new_m', 'new_m_conv_dw_w': 'new_m', 'new_m_conv_dw_b': 'new_m', 'new_m_conv_ln_g': 'new_m', 'new_m_conv_ln_b': 'new_m', 'new_m_w_out': 'new_m', 'new_m_ffn_norm_g': 'new_m', 'new_m_w_gate': 'new_m', 'new_m_w_up': 'new_m', 'new_m_w_down': 'new_m', 'new_m_final_norm_g': 'new_m', 'new_v_meta_tokens': 'new_v', 'new_v_mix_norm_g': 'new_v', 'new_v_w_in': 'new_v', 'new_v_conv_dw_w': 'new_v', 'new_v_conv_dw_b': 'new_v', 'new_v_conv_ln_g': 'new_v', 'new_v_conv_ln_b': 'new_v', 'new_v_w_out': 'new_v', 'new_v_ffn_norm_g': 'new_v', 'new_v_w_gate': 'new_v', 'new_v_w_up': 'new_v', 'new_v_w_down': 'new_v', 'new_v_final_norm_g': 'new_v'}


def _forward(args):
    return _fwd_reference(*[args[k] for k in FWD_PARAMS])


def _output_shape():
    out = _jax.eval_shape(lambda: _forward(_fwd_setup_inputs(0)))
    return out.shape, out.dtype

N_MICROBATCH = 1
ADAM_LR = 0.001
ADAM_B1 = 0.9
ADAM_B2 = 0.999
ADAM_EPS = 1e-08
ADAM_WD = 0.01
ADAM_STEP = 10
PER_EXAMPLE_BATCH_AXIS = {'x': 0, 'loss_target': 0}
SHARED_INPUTS = []
_WEIGHT_DTYPES = {'meta_tokens': _jnp.float32, 'mix_norm_g': _jnp.float32, 'w_in': _jnp.float32, 'conv_dw_w': _jnp.float32, 'conv_dw_b': _jnp.float32, 'conv_ln_g': _jnp.float32, 'conv_ln_b': _jnp.float32, 'w_out': _jnp.float32, 'ffn_norm_g': _jnp.float32, 'w_gate': _jnp.float32, 'w_up': _jnp.float32, 'w_down': _jnp.float32, 'final_norm_g': _jnp.float32}
MOMENT_SCALE = {'meta_tokens': 6.965905e-03, 'mix_norm_g': 1.201930e-01, 'w_in': 7.254007e-02, 'conv_dw_w': 9.908726e-02, 'conv_dw_b': 1.820229e-01, 'conv_ln_g': 1.125614e-01, 'conv_ln_b': 1.013209e-01, 'w_out': 1.005076e-01, 'ffn_norm_g': 1.157649e-01, 'w_gate': 4.884245e-02, 'w_up': 4.727900e-02, 'w_down': 7.833299e-02, 'final_norm_g': 3.197457e+01}


def _to_microbatches(a, axis):
    t = _jnp.moveaxis(a, axis, 0)
    t = t.reshape((N_MICROBATCH, t.shape[0] // N_MICROBATCH) + t.shape[1:])
    return _jnp.moveaxis(t, 1, axis + 1)


def setup_inputs(seed: int = 0) -> dict:
    inp = _fwd_setup_inputs(seed)
    key = _jax.random.fold_in(_jax.random.key(seed), 7919)
    shape, _ = _output_shape()
    out = dict(inp)
    out["loss_target"] = _jax.random.normal(_jax.random.fold_in(key, 0), shape, _jnp.float32)
    for i, name in enumerate(TWIN_WEIGHTS):
        w = inp[name].astype(_jnp.float32)
        if MOMENT_SCALE is None:
            s = _jnp.sqrt(_jnp.mean(_jnp.square(w)) + 1e-30)
        else:
            s = MOMENT_SCALE[name]
        km, kv = _jax.random.split(_jax.random.fold_in(key, i + 1))
        out[name] = w
        out["m_" + name] = s * _jax.random.normal(km, w.shape, _jnp.float32)
        out["v_" + name] = (s * s) * _jax.random.uniform(kv, w.shape, _jnp.float32, 0.5, 1.5)
    if N_MICROBATCH > 1:
        for name, axis in PER_EXAMPLE_BATCH_AXIS.items():
            out[name] = _to_microbatches(out[name], axis)
    return {'x': out['x'], 'meta_tokens': out['meta_tokens'], 'mix_norm_g': out['mix_norm_g'], 'w_in': out['w_in'], 'conv_dw_w': out['conv_dw_w'], 'conv_dw_b': out['conv_dw_b'], 'conv_ln_g': out['conv_ln_g'], 'conv_ln_b': out['conv_ln_b'], 'w_out': out['w_out'], 'ffn_norm_g': out['ffn_norm_g'], 'w_gate': out['w_gate'], 'w_up': out['w_up'], 'w_down': out['w_down'], 'final_norm_g': out['final_norm_g'], 'loss_target': out['loss_target'], 'm_meta_tokens': out['m_meta_tokens'], 'm_mix_norm_g': out['m_mix_norm_g'], 'm_w_in': out['m_w_in'], 'm_conv_dw_w': out['m_conv_dw_w'], 'm_conv_dw_b': out['m_conv_dw_b'], 'm_conv_ln_g': out['m_conv_ln_g'], 'm_conv_ln_b': out['m_conv_ln_b'], 'm_w_out': out['m_w_out'], 'm_ffn_norm_g': out['m_ffn_norm_g'], 'm_w_gate': out['m_w_gate'], 'm_w_up': out['m_w_up'], 'm_w_down': out['m_w_down'], 'm_final_norm_g': out['m_final_norm_g'], 'v_meta_tokens': out['v_meta_tokens'], 'v_mix_norm_g': out['v_mix_norm_g'], 'v_w_in': out['v_w_in'], 'v_conv_dw_w': out['v_conv_dw_w'], 'v_conv_dw_b': out['v_conv_dw_b'], 'v_conv_ln_g': out['v_conv_ln_g'], 'v_conv_ln_b': out['v_conv_ln_b'], 'v_w_out': out['v_w_out'], 'v_ffn_norm_g': out['v_ffn_norm_g'], 'v_w_gate': out['v_w_gate'], 'v_w_up': out['v_w_up'], 'v_w_down': out['v_w_down'], 'v_final_norm_g': out['v_final_norm_g']}


def _loss(weights, diff, rest, loss_target):
    with _jax.named_scope("forward"):
        args = {**rest, TWIN_DIFF_INPUT: diff, **{k: w.astype(_WEIGHT_DTYPES[k]) for k, w in weights.items()}}
        y = _forward(args)
    with _jax.named_scope("loss_head"):
        err = _jnp.square(y.astype(_jnp.float32) - loss_target)
        return 0.5 * _jnp.sum(_jnp.mean(err, axis=-1)) if err.ndim else 0.5 * err


def _adamw(w, g, m, v):
    m = ADAM_B1 * m + (1.0 - ADAM_B1) * g
    v = ADAM_B2 * v + (1.0 - ADAM_B2) * _jnp.square(g)
    m_hat = m / (1.0 - ADAM_B1 ** ADAM_STEP)
    v_hat = v / (1.0 - ADAM_B2 ** ADAM_STEP)
    delta = -ADAM_LR * (m_hat / (_jnp.sqrt(v_hat) + ADAM_EPS) + ADAM_WD * w)
    return delta, m, v


def reference(x, meta_tokens, mix_norm_g, w_in, conv_dw_w, conv_dw_b, conv_ln_g, conv_ln_b, w_out, ffn_norm_g, w_gate, w_up, w_down, final_norm_g, loss_target, m_meta_tokens, m_mix_norm_g, m_w_in, m_conv_dw_w, m_conv_dw_b, m_conv_ln_g, m_conv_ln_b, m_w_out, m_ffn_norm_g, m_w_gate, m_w_up, m_w_down, m_final_norm_g, v_meta_tokens, v_mix_norm_g, v_w_in, v_conv_dw_w, v_conv_dw_b, v_conv_ln_g, v_conv_ln_b, v_w_out, v_ffn_norm_g, v_w_gate, v_w_up, v_w_down, v_final_norm_g):
    given = dict(x=x, meta_tokens=meta_tokens, mix_norm_g=mix_norm_g, w_in=w_in, conv_dw_w=conv_dw_w, conv_dw_b=conv_dw_b, conv_ln_g=conv_ln_g, conv_ln_b=conv_ln_b, w_out=w_out, ffn_norm_g=ffn_norm_g, w_gate=w_gate, w_up=w_up, w_down=w_down, final_norm_g=final_norm_g, loss_target=loss_target, m_meta_tokens=m_meta_tokens, m_mix_norm_g=m_mix_norm_g, m_w_in=m_w_in, m_conv_dw_w=m_conv_dw_w, m_conv_dw_b=m_conv_dw_b, m_conv_ln_g=m_conv_ln_g, m_conv_ln_b=m_conv_ln_b, m_w_out=m_w_out, m_ffn_norm_g=m_ffn_norm_g, m_w_gate=m_w_gate, m_w_up=m_w_up, m_w_down=m_w_down, m_final_norm_g=m_final_norm_g, v_meta_tokens=v_meta_tokens, v_mix_norm_g=v_mix_norm_g, v_w_in=v_w_in, v_conv_dw_w=v_conv_dw_w, v_conv_dw_b=v_conv_dw_b, v_conv_ln_g=v_conv_ln_g, v_conv_ln_b=v_conv_ln_b, v_w_out=v_w_out, v_ffn_norm_g=v_ffn_norm_g, v_w_gate=v_w_gate, v_w_up=v_w_up, v_w_down=v_w_down, v_final_norm_g=v_final_norm_g)
    weights = {n: given[n] for n in TWIN_WEIGHTS}
    shared = {n: given[n] for n in SHARED_INPUTS}
    per_example = {n: given[n] for n in ['x']}
    grad_fn = _jax.value_and_grad(_loss, argnums=(0, 1))

    def one_microbatch(ex, loss_target):
        ex = dict(ex)
        diff = ex.pop(TWIN_DIFF_INPUT)
        return grad_fn(weights, diff, {**shared, **ex}, loss_target)

    if N_MICROBATCH == 1:
        loss, (grad_w, grad_x) = one_microbatch(per_example, given["loss_target"])
    else:
        def body(carry, xs):
            loss_sum, grad_sum = carry
            l_k, (gw_k, gx_k) = one_microbatch(xs[0], xs[1])
            with _jax.named_scope("update"):
                return (loss_sum + l_k, _jax.tree.map(_jnp.add, grad_sum, gw_k)), gx_k

        init = (_jnp.zeros((), _jnp.float32), _jax.tree.map(_jnp.zeros_like, weights))
        (loss, grad_w), grad_x = _jax.lax.scan(body, init, (per_example, given["loss_target"]))
    with _jax.named_scope("update"):
        delta_w, new_m, new_v = {}, {}, {}
        for n in TWIN_WEIGHTS:
            delta_w[n], new_m[n], new_v[n] = _adamw(weights[n], grad_w[n], given["m_" + n], given["v_" + n])
    return (loss, grad_x, *[grad_w[n] for n in TWIN_WEIGHTS], *[delta_w[n] for n in TWIN_WEIGHTS],
            *[new_m[n] for n in TWIN_WEIGHTS], *[new_v[n] for n in TWIN_WEIGHTS])
```

```python
import functools
import math

import jax
import jax.numpy as jnp
from jax import lax
from jax.experimental import pallas as pl
from jax.experimental.pallas import tpu as pltpu

F32 = jnp.float32
BF16 = jnp.bfloat16
SDS = jax.ShapeDtypeStruct

N_META = 16
N_HEADS = 8
HEAD_DIM = 64
CONV_K = 31
CONV_PAD = 32
ABLK = 128
FF_CHUNK = 256
EPS = 1e-6
N_DEV = 8
MESH_AXES = ("x", "y", "c")
ADAM_LR = 0.001
ADAM_B1 = 0.9
ADAM_B2 = 0.999
ADAM_EPS = 1e-08
ADAM_WD = 0.01
ADAM_STEP = 10
MIB = 1 << 20
VMEM_LIMIT_MIB = 48


def _cp():
    return pltpu.CompilerParams(vmem_limit_bytes=VMEM_LIMIT_MIB * MIB)


def _tile(n, cap, mult):
    best = None
    for t in range(mult, min(n, cap) + 1, mult):
        if n % t == 0:
            best = t
    assert best is not None, (n, cap, mult)
    return best


def _sum8(v):
    r, c = v.shape
    return v.reshape(r // 8, 8, c).sum(axis=0)


def _sigmoid(x):
    return 1.0 / (1.0 + jnp.exp(-x))


def _rms_mm(h, g, w, w_col0, n_cols, out_dtype, name, hn_out):
    lp, d = h.shape
    tm = _tile(lp, 528, 16)
    tn = _tile(math.gcd(n_cols, w_col0), 512, 128)
    off = w_col0 // tn

    def body(h_ref, g_ref, w_ref, *rest):
        if hn_out:
            o_ref, hn_ref, hn_s = rest
        else:
            o_ref, hn_s = rest

        @pl.when(pl.program_id(1) == 0)
        def _():
            x = h_ref[...]
            r = lax.rsqrt(jnp.mean(x * x, axis=-1, keepdims=True) + EPS)
            hn = ((x * r) * g_ref[...]).astype(BF16)
            hn_s[...] = hn
            if hn_out:
                hn_ref[...] = hn

        o_ref[...] = jnp.dot(hn_s[...], w_ref[...], preferred_element_type=F32).astype(out_dtype)

    out_shape = [SDS((lp, n_cols), out_dtype)]
    out_specs = [pl.BlockSpec((tm, tn), lambda i, j: (i, j))]
    if hn_out:
        out_shape.append(SDS((lp, d), BF16))
        out_specs.append(pl.BlockSpec((tm, d), lambda i, j: (i, 0)))
    return pl.pallas_call(
        body, out_shape=out_shape, grid=(lp // tm, n_cols // tn),
        in_specs=[pl.BlockSpec((tm, d), lambda i, j: (i, 0)),
                  pl.BlockSpec((1, d), lambda i, j: (0, 0)),
                  pl.BlockSpec((d, tn), lambda i, j: (0, j + off))],
        out_specs=out_specs, scratch_shapes=[pltpu.VMEM((tm, d), BF16)],
        name=name, compiler_params=_cp())(h, g, w)


def _mm(a, b, *, trans_b, out_dtype, name, b_row0=0, n_out=None, res=None):
    m, k = a.shape
    n = n_out if n_out is not None else (b.shape[0] if trans_b else b.shape[1])
    tm = _tile(m, 528, 16)
    tn = _tile(n, 512, 128)
    assert b_row0 % tn == 0
    off = b_row0 // tn
    dn = (((1,), (1,)), ((), ())) if trans_b else (((1,), (0,)), ((), ()))

    def body(a_ref, b_ref, *rest):
        o_ref = rest[-1]
        acc = lax.dot_general(a_ref[...].astype(BF16), b_ref[...], dn, preferred_element_type=F32)
        if res is not None:
            acc = acc + rest[0][...]
        o_ref[...] = acc.astype(out_dtype)

    b_spec = (pl.BlockSpec((tn, k), lambda i, j: (j + off, 0)) if trans_b
              else pl.BlockSpec((k, tn), lambda i, j: (0, j)))
    in_specs = [pl.BlockSpec((tm, k), lambda i, j: (i, 0)), b_spec]
    args = [a, b]
    if res is not None:
        in_specs.append(pl.BlockSpec((tm, tn), lambda i, j: (i, j)))
        args.append(res)
    return pl.pallas_call(
        body, out_shape=SDS((m, n), out_dtype), grid=(m // tm, n // tn),
        in_specs=in_specs, out_specs=pl.BlockSpec((tm, tn), lambda i, j: (i, j)),
        name=name, compiler_params=_cp())(*args)


def _mm_tn(a, b, name):
    l, m = a.shape
    n = b.shape[1]
    tm = _tile(m, 1408, 128)
    tn = _tile(n, 1408, 128)
    tl = _tile(l, 512, 128)

    def body(a_ref, b_ref, o_ref):
        @pl.when(pl.program_id(2) == 0)
        def _():
            o_ref[...] = jnp.zeros_like(o_ref)

        o_ref[...] += lax.dot_general(a_ref[...].astype(BF16), b_ref[...].astype(BF16),
                                      (((0,), (0,)), ((), ())), preferred_element_type=F32)

    return pl.pallas_call(
        body, out_shape=SDS((m, n), F32), grid=(m // tm, n // tn, l // tl),
        in_specs=[pl.BlockSpec((tl, tm), lambda i, j, s: (s, i)),
                  pl.BlockSpec((tl, tn), lambda i, j, s: (s, j))],
        out_specs=pl.BlockSpec((tm, tn), lambda i, j, s: (i, j)),
        name=name, compiler_params=_cp())(a, b)


def _nt_rmsbwd(dy, w, h, g, dres, name):
    lp, k = dy.shape
    d = h.shape[1]
    tm = _tile(lp, 528, 16)
    tk = _tile(k, 512, 128)
    nk = k // tk

    def body(dy_ref, w_ref, h_ref, g_ref, dres_ref, o_ref, dg_ref, acc):
        i, kk = pl.program_id(0), pl.program_id(1)

        @pl.when(kk == 0)
        def _():
            acc[...] = jnp.zeros_like(acc)

        @pl.when((kk == 0) & (i == 0))
        def _():
            dg_ref[...] = jnp.zeros_like(dg_ref)

        acc[...] += lax.dot_general(dy_ref[...], w_ref[...], (((1,), (1,)), ((), ())), preferred_element_type=F32)

        @pl.when(kk == nk - 1)
        def _():
            x = h_ref[...]
            r = lax.rsqrt(jnp.mean(x * x, axis=-1, keepdims=True) + EPS)
            xr = x * r
            dyv = acc[...]
            gy = dyv * g_ref[...]
            dx = r * (gy - xr * jnp.mean(gy * xr, axis=-1, keepdims=True))
            o_ref[...] = dres_ref[...] + dx
            dg_ref[...] += _sum8(dyv * xr)

    return pl.pallas_call(
        body, out_shape=[SDS((lp, d), F32), SDS((8, d), F32)], grid=(lp // tm, nk),
        in_specs=[pl.BlockSpec((tm, tk), lambda i, j: (i, j)),
                  pl.BlockSpec((d, tk), lambda i, j: (0, j)),
                  pl.BlockSpec((tm, d), lambda i, j: (i, 0)),
                  pl.BlockSpec((1, d), lambda i, j: (0, 0)),
                  pl.BlockSpec((tm, d), lambda i, j: (i, 0))],
        out_specs=[pl.BlockSpec((tm, d), lambda i, j: (i, 0)), pl.BlockSpec((8, d), lambda i, j: (0, 0))],
        scratch_shapes=[pltpu.VMEM((tm, d), F32)], name=name, compiler_params=_cp())(dy, w, h, g, dres)


def _ffn_fwd(h, g, w_gu, w_d, name):
    lp, d = h.shape
    f = w_d.shape[0]
    tm = _tile(lp, 528, 16)
    nc = f // FF_CHUNK

    def body(h_ref, g_ref, wgu_ref, wd_ref, o_ref, hn_ref, act_ref, hn_s, acc):
        c = pl.program_id(1)

        @pl.when(c == 0)
        def _():
            x = h_ref[...]
            r = lax.rsqrt(jnp.mean(x * x, axis=-1, keepdims=True) + EPS)
            hn = ((x * r) * g_ref[...]).astype(BF16)
            hn_s[...] = hn
            hn_ref[...] = hn
            acc[...] = jnp.zeros_like(acc)

        gu = jnp.dot(hn_s[...], wgu_ref[...], preferred_element_type=F32)
        gg, uu = gu[:, :FF_CHUNK], gu[:, FF_CHUNK:]
        act = ((gg * _sigmoid(gg)) * uu).astype(BF16)
        act_ref[...] = act
        acc[...] += jnp.dot(act, wd_ref[...], preferred_element_type=F32)

        @pl.when(c == nc - 1)
        def _():
            o_ref[...] = h_ref[...] + acc[...]

    return pl.pallas_call(
        body, out_shape=[SDS((lp, d), F32), SDS((lp, d), BF16), SDS((lp, f), BF16)], grid=(lp // tm, nc),
        in_specs=[pl.BlockSpec((tm, d), lambda i, j: (i, 0)),
                  pl.BlockSpec((1, d), lambda i, j: (0, 0)),
                  pl.BlockSpec((d, 2 * FF_CHUNK), lambda i, j: (0, j)),
                  pl.BlockSpec((FF_CHUNK, d), lambda i, j: (j, 0))],
        out_specs=[pl.BlockSpec((tm, d), lambda i, j: (i, 0)),
                   pl.BlockSpec((tm, d), lambda i, j: (i, 0)),
                   pl.BlockSpec((tm, FF_CHUNK), lambda i, j: (i, j))],
        scratch_shapes=[pltpu.VMEM((tm, d), BF16), pltpu.VMEM((tm, d), F32)],
        name=name, compiler_params=_cp())(h, g, w_gu, w_d)


def _ffn_bwd_act(dh, hn, w_gu, w_d, name):
    lp, d = dh.shape
    f = w_d.shape[0]
    tm = _tile(lp, 528, 16)
    nc = f // FF_CHUNK

    def body(dh_ref, hn_ref, wgu_ref, wd_ref, o_ref, dh_s):
        @pl.when(pl.program_id(1) == 0)
        def _():
            dh_s[...] = dh_ref[...].astype(BF16)

        gu = jnp.dot(hn_ref[...], wgu_ref[...], preferred_element_type=F32)
        gg, uu = gu[:, :FF_CHUNK], gu[:, FF_CHUNK:]
        dact = lax.dot_general(dh_s[...], wd_ref[...], (((1,), (1,)), ((), ())), preferred_element_type=F32)
        s = _sigmoid(gg)
        d_g = dact * uu * (s * (1.0 + gg * (1.0 - s)))
        d_u = dact * (gg * s)
        o_ref[...] = jnp.concatenate([d_g, d_u], axis=1).astype(BF16)

    return pl.pallas_call(
        body, out_shape=SDS((lp, 2 * f), BF16), grid=(lp // tm, nc),
        in_specs=[pl.BlockSpec((tm, d), lambda i, j: (i, 0)),
                  pl.BlockSpec((tm, d), lambda i, j: (i, 0)),
                  pl.BlockSpec((d, 2 * FF_CHUNK), lambda i, j: (0, j)),
                  pl.BlockSpec((FF_CHUNK, d), lambda i, j: (j, 0))],
        out_specs=pl.BlockSpec((tm, 2 * FF_CHUNK), lambda i, j: (i, j)),
        scratch_shapes=[pltpu.VMEM((tm, d), BF16)], name=name, compiler_params=_cp())(dh, hn, w_gu, w_d)


def _loss_head(h, tpad, g, n_real, name):
    lp, d = h.shape
    tm = _tile(lp, 528, 16)

    def body(h_ref, t_ref, g_ref, dh_ref, dg_ref, loss_ref):
        i = pl.program_id(0)

        @pl.when(i == 0)
        def _():
            dg_ref[...] = jnp.zeros_like(dg_ref)
            loss_ref[...] = jnp.zeros_like(loss_ref)

        x = h_ref[...]
        r = lax.rsqrt(jnp.mean(x * x, axis=-1, keepdims=True) + EPS)
        xr = x * r
        y = xr * g_ref[...]
        row = i * tm + lax.broadcasted_iota(jnp.int32, (tm, d), 0)
        valid = (row >= N_META) & (row < N_META + n_real)
        diff = jnp.where(valid, y - t_ref[...], 0.0)
        loss_ref[...] += jnp.sum(diff * diff) * (0.5 / d)
        dy = diff * (1.0 / d)
        gy = dy * g_ref[...]
        dh_ref[...] = r * (gy - xr * jnp.mean(gy * xr, axis=-1, keepdims=True))
        dg_ref[...] += _sum8(dy * xr)

    return pl.pallas_call(
        body, out_shape=[SDS((lp, d), F32), SDS((8, d), F32), SDS((8, 128), F32)], grid=(lp // tm,),
        in_specs=[pl.BlockSpec((tm, d), lambda i: (i, 0)),
                  pl.BlockSpec((tm, d), lambda i: (i, 0)),
                  pl.BlockSpec((1, d), lambda i: (0, 0))],
        out_specs=[pl.BlockSpec((tm, d), lambda i: (i, 0)),
                   pl.BlockSpec((8, d), lambda i: (0, 0)),
                   pl.BlockSpec((8, 128), lambda i: (0, 0))],
        name=name, compiler_params=_cp())(h, tpad, g)


def _tri_consts():
    j = lax.broadcasted_iota(jnp.int32, (ABLK, ABLK), 0)
    s = lax.broadcasted_iota(jnp.int32, (ABLK, ABLK), 1)
    after = (j > s).astype(BF16)
    before = (j < s).astype(BF16)
    ones = jnp.ones((ABLK, ABLK), BF16)
    two = lambda t: jnp.concatenate([t, t], axis=0)
    return (two(jnp.concatenate([after, ones], axis=1)),
            two(after),
            two(jnp.concatenate([before, ones], axis=1)))


def _split_hi_lo(m):
    hi = m.astype(BF16)
    lo = (m - hi.astype(F32)).astype(BF16)
    return jnp.concatenate([hi, lo], axis=1)


def _head_halves(t2, in_a):
    zero = jnp.zeros_like(t2)
    return jnp.concatenate([jnp.where(in_a, t2, zero), jnp.where(in_a, zero, t2)], axis=0)


def _attn_fwd(qkv, tri_fwd, name):
    lp = qkv.shape[0]
    n_pairs = (N_HEADS * HEAD_DIM) // 128
    nb = lp // ABLK
    assert nb <= 128 and 2 * HEAD_DIM == 128
    scale = 1.0 / math.sqrt(HEAD_DIM)
    nt = (((1,), (1,)), ((), ()))

    def body(q_ref, k_ref, v_ref, tri_ref, o_ref, rs_ref, r_s, acc_s, rs_s):
        lane = lax.broadcasted_iota(jnp.int32, (ABLK, 128), 1)
        row = lax.broadcasted_iota(jnp.int32, (ABLK, 128), 0)
        in_a = lane < HEAD_DIM
        causal = lane < row

        def qblock(i, carry):
            q0 = pl.multiple_of(i * ABLK, ABLK)
            q2 = q_ref[pl.ds(q0, ABLK), :]
            r_s[...] = jnp.zeros_like(r_s)
            acc_s[...] = jnp.zeros_like(acc_s)
            rs_s[...] = jnp.zeros_like(rs_s)

            def kstep(b, diag):
                k0 = pl.multiple_of(b * ABLK, ABLK)
                kbd = _head_halves(k_ref[pl.ds(k0, ABLK), :], in_a)
                vbd = _head_halves(v_ref[pl.ds(k0, ABLK), :], in_a)
                z2 = lax.dot_general(q2, kbd, nt, preferred_element_type=F32) * scale
                ws = []
                for hh in range(2):
                    z = z2[:, hh * 128:(hh + 1) * 128]
                    sp = jnp.log(1.0 + jnp.exp(-jnp.abs(z)))
                    lb = jnp.minimum(z, 0.0) - sp
                    m = jnp.minimum(-z, 0.0) - sp
                    if diag:
                        m = jnp.where(causal, m, 0.0)
                    ce = jnp.dot(_split_hi_lo(m), tri_ref[...], preferred_element_type=F32)
                    r_old = r_s[hh]
                    w = jnp.exp(lb + ce[:, :128] + r_old)
                    if diag:
                        w = jnp.where(causal, w, 0.0)
                    ws.append(w)
                    rs_s[:, hh * 128:(hh + 1) * 128] = jnp.where(lane == b, r_old, rs_s[:, hh * 128:(hh + 1) * 128])
                    r_s[hh] = r_old + ce[:, 128:]
                w2 = jnp.concatenate(ws, axis=1).astype(BF16)
                acc_s[...] += jnp.dot(w2, vbd, preferred_element_type=F32)

            kstep(i, True)

            def inner(jj, c2):
                kstep(i - 1 - jj, False)
                return c2

            lax.fori_loop(0, i, inner, 0)
            o_ref[pl.ds(q0, ABLK), :] = acc_s[...].astype(BF16)
            rs_ref[pl.ds(q0, ABLK), :] = rs_s[...]
            return carry

        lax.fori_loop(0, nb, qblock, 0)

    col = lambda o: (lambda p: (0, p + o))
    return pl.pallas_call(
        body, out_shape=[SDS((lp, n_pairs * 128), BF16), SDS((lp, n_pairs * 256), F32)], grid=(n_pairs,),
        in_specs=[pl.BlockSpec((lp, 128), col(0)), pl.BlockSpec((lp, 128), col(n_pairs)),
                  pl.BlockSpec((lp, 128), col(2 * n_pairs)), pl.BlockSpec((256, 256), lambda p: (0, 0))],
        out_specs=[pl.BlockSpec((lp, 128), col(0)), pl.BlockSpec((lp, 256), col(0))],
        scratch_shapes=[pltpu.VMEM((2, ABLK, 128), F32), pltpu.VMEM((ABLK, 128), F32), pltpu.VMEM((ABLK, 256), F32)],
        name=name, compiler_params=_cp())(qkv, qkv, qkv, tri_fwd)


def _attn_bwd(qkv, d_out, rsave, tri_after, tri_before, name):
    lp = qkv.shape[0]
    n_pairs = (N_HEADS * HEAD_DIM) // 128
    nb = lp // ABLK
    scale = 1.0 / math.sqrt(HEAD_DIM)
    nt = (((1,), (1,)), ((), ()))
    tn = (((0,), (0,)), ((), ()))

    def body(q_ref, k_ref, v_ref, do_ref, rs_ref, ta_ref, tb_ref, o_ref, dk_s, dv_s, dq_s, pc_s):
        lane = lax.broadcasted_iota(jnp.int32, (ABLK, 128), 1)
        row = lax.broadcasted_iota(jnp.int32, (ABLK, 128), 0)
        in_a = lane < HEAD_DIM
        causal = lane < row
        dk_s[...] = jnp.zeros_like(dk_s)
        dv_s[...] = jnp.zeros_like(dv_s)

        def qblock(i, carry):
            q0 = pl.multiple_of(i * ABLK, ABLK)
            q2 = q_ref[pl.ds(q0, ABLK), :]
            do2 = do_ref[pl.ds(q0, ABLK), :]
            q_st = _head_halves(q2, in_a)
            do_st = _head_halves(do2, in_a)
            dq_s[...] = jnp.zeros_like(dq_s)
            pc_s[...] = jnp.zeros_like(pc_s)

            def kstep(b, diag):
                k0 = pl.multiple_of(b * ABLK, ABLK)
                kbd = _head_halves(k_ref[pl.ds(k0, ABLK), :], in_a)
                vbd = _head_halves(v_ref[pl.ds(k0, ABLK), :], in_a)
                z2 = lax.dot_general(q2, kbd, nt, preferred_element_type=F32) * scale
                dw2 = lax.dot_general(do2, vbd, nt, preferred_element_type=F32)
                dzs, ws = [], []
                for hh in range(2):
                    z = z2[:, hh * 128:(hh + 1) * 128]
                    e = jnp.exp(-jnp.abs(z))
                    sp = jnp.log(1.0 + e)
                    lb = jnp.minimum(z, 0.0) - sp
                    m = jnp.minimum(-z, 0.0) - sp
                    beta = jnp.where(z >= 0.0, 1.0, e) / (1.0 + e)
                    if diag:
                        m = jnp.where(causal, m, 0.0)
                    r_saved = jnp.sum(jnp.where(lane == b, rs_ref[pl.ds(q0, ABLK), hh * 128:(hh + 1) * 128], 0.0),
                                      axis=1, keepdims=True)
                    ex = jnp.dot(_split_hi_lo(m), ta_ref[...], preferred_element_type=F32)
                    w = jnp.exp(lb + ex + r_saved)
                    if diag:
                        w = jnp.where(causal, w, 0.0)
                    dl = dw2[:, hh * 128:(hh + 1) * 128] * w
                    pe = jnp.dot(_split_hi_lo(dl), tb_ref[...], preferred_element_type=F32)
                    p = pe[:, :128] + pc_s[hh]
                    pc_s[hh] += pe[:, 128:]
                    dz = (dl * (1.0 - beta) - p * beta) * scale
                    if diag:
                        dz = jnp.where(causal, dz, 0.0)
                    dzs.append(dz.astype(BF16))
                    ws.append(w.astype(BF16))
                dq_s[...] += jnp.dot(jnp.concatenate(dzs, axis=1), kbd, preferred_element_type=F32)
                dk_s[pl.ds(k0, ABLK), :] += lax.dot_general(jnp.concatenate(dzs, axis=0), q_st, tn,
                                                            preferred_element_type=F32)
                dv_s[pl.ds(k0, ABLK), :] += lax.dot_general(jnp.concatenate(ws, axis=0), do_st, tn,
                                                            preferred_element_type=F32)

            def inner(b, c2):
                kstep(b, False)
                return c2

            lax.fori_loop(0, i, inner, 0)
            kstep(i, True)
            o_ref[0, pl.ds(q0, ABLK), :] = dq_s[...].astype(BF16)
            return carry

        lax.fori_loop(0, nb, qblock, 0)
        o_ref[1] = dk_s[...].astype(BF16)
        o_ref[2] = dv_s[...].astype(BF16)

    col = lambda o: (lambda p: (0, p + o))
    return pl.pallas_call(
        body, out_shape=SDS((3, lp, n_pairs * 128), BF16), grid=(n_pairs,),
        in_specs=[pl.BlockSpec((lp, 128), col(0)), pl.BlockSpec((lp, 128), col(n_pairs)),
                  pl.BlockSpec((lp, 128), col(2 * n_pairs)), pl.BlockSpec((lp, 128), col(0)),
                  pl.BlockSpec((lp, 256), col(0)),
                  pl.BlockSpec((256, 128), lambda p: (0, 0)), pl.BlockSpec((256, 256), lambda p: (0, 0))],
        out_specs=pl.BlockSpec((3, lp, 128), lambda p: (0, 0, p)),
        scratch_shapes=[pltpu.VMEM((lp, 128), F32), pltpu.VMEM((lp, 128), F32),
                        pltpu.VMEM((ABLK, 128), F32), pltpu.VMEM((2, ABLK, 128), F32)],
        name=name, compiler_params=_cp())(qkv, qkv, qkv, d_out, rsave, tri_after, tri_before)


def _conv_fwd_dw(cacg, w, b, name):
    lp = cacg.shape[0]
    c = cacg.shape[1] // 2
    ncb = c // 128
    nchunk = lp // ABLK
    off = CONV_PAD - (CONV_K - 1)

    def body(a_ref, g_ref, w_ref, b_ref, y_ref, upad):
        upad[0:CONV_PAD, :] = jnp.zeros((CONV_PAD, 128), F32)

        def fill(ch, carry):
            base = pl.multiple_of(ch * ABLK, ABLK)
            upad[pl.ds(base + CONV_PAD, ABLK), :] = a_ref[pl.ds(base, ABLK), :] * _sigmoid(g_ref[pl.ds(base, ABLK), :])
            return carry

        lax.fori_loop(0, nchunk, fill, 0)

        def comp(ch, carry):
            base = pl.multiple_of(ch * ABLK, ABLK)
            acc = jnp.zeros((ABLK, 128), F32)
            for k in range(CONV_K):
                acc = acc + upad[pl.ds(base + (off + k), ABLK), :] * w_ref[k:k + 1, :]
            y_ref[pl.ds(base, ABLK), :] = acc + b_ref[...]
            return carry

        lax.fori_loop(0, nchunk, comp, 0)

    return pl.pallas_call(
        body, out_shape=SDS((lp, c), F32), grid=(ncb,),
        in_specs=[pl.BlockSpec((lp, 128), lambda j: (0, j)), pl.BlockSpec((lp, 128), lambda j: (0, j + ncb)),
                  pl.BlockSpec((CONV_PAD, 128), lambda j: (0, j)), pl.BlockSpec((1, 128), lambda j: (0, j))],
        out_specs=pl.BlockSpec((lp, 128), lambda j: (0, j)),
        scratch_shapes=[pltpu.VMEM((lp + CONV_PAD, 128), F32)], name=name, compiler_params=_cp())(cacg, cacg, w, b)


def _ln_parts(x, g, b):
    mu = jnp.mean(x, axis=-1, keepdims=True)
    xc = x - mu
    rstd = lax.rsqrt(jnp.mean(xc * xc, axis=-1, keepdims=True) + EPS)
    xh = xc * rstd
    return xh, rstd, xh * g + b


def _conv_fwd_ln(yc, g, b, name):
    lp, c = yc.shape
    tm = _tile(lp, 528, 16)

    def body(y_ref, g_ref, b_ref, o_ref):
        _, _, ln = _ln_parts(y_ref[...], g_ref[...], b_ref[...])
        o_ref[...] = (ln * _sigmoid(ln)).astype(BF16)

    return pl.pallas_call(
        body, out_shape=SDS((lp, c), BF16), grid=(lp // tm,),
        in_specs=[pl.BlockSpec((tm, c), lambda i: (i, 0)), pl.BlockSpec((1, c), lambda i: (0, 0)),
                  pl.BlockSpec((1, c), lambda i: (0, 0))],
        out_specs=pl.BlockSpec((tm, c), lambda i: (i, 0)), name=name, compiler_params=_cp())(yc, g, b)


def _conv_bwd_ln(yc, dout, g, b, name):
    lp, c = yc.shape
    tm = _tile(lp, 528, 16)

    def body(y_ref, d_ref, g_ref, b_ref, o_ref, dg_ref, db_ref):
        @pl.when(pl.program_id(0) == 0)
        def _():
            dg_ref[...] = jnp.zeros_like(dg_ref)
            db_ref[...] = jnp.zeros_like(db_ref)

        xh, rstd, ln = _ln_parts(y_ref[...], g_ref[...], b_ref[...])
        s = _sigmoid(ln)
        dln = d_ref[...] * (s * (1.0 + ln * (1.0 - s)))
        dg_ref[...] += _sum8(dln * xh)
        db_ref[...] += _sum8(dln)
        dxh = dln * g_ref[...]
        o_ref[...] = rstd * (dxh - jnp.mean(dxh, axis=-1, keepdims=True)
                             - xh * jnp.mean(dxh * xh, axis=-1, keepdims=True))

    return pl.pallas_call(
        body, out_shape=[SDS((lp, c), F32), SDS((8, c), F32), SDS((8, c), F32)], grid=(lp // tm,),
        in_specs=[pl.BlockSpec((tm, c), lambda i: (i, 0)), pl.BlockSpec((tm, c), lambda i: (i, 0)),
                  pl.BlockSpec((1, c), lambda i: (0, 0)), pl.BlockSpec((1, c), lambda i: (0, 0))],
        out_specs=[pl.BlockSpec((tm, c), lambda i: (i, 0)), pl.BlockSpec((8, c), lambda i: (0, 0)),
                   pl.BlockSpec((8, c), lambda i: (0, 0))],
        name=name, compiler_params=_cp())(yc, dout, g, b)


def _conv_bwd_dw(dyc, cacg, w, name):
    lp, c = dyc.shape
    ncb = c // 128
    nchunk = lp // ABLK
    off = CONV_PAD - (CONV_K - 1)

    def body(dy_ref, a_ref, g_ref, w_ref, da_ref, dgate_ref, dw_ref, db_ref, upad, dypad, dwacc):
        upad[0:CONV_PAD, :] = jnp.zeros((CONV_PAD, 128), F32)
        dypad[lp:lp + CONV_PAD, :] = jnp.zeros((CONV_PAD, 128), F32)
        dwacc[...] = jnp.zeros_like(dwacc)
        db_ref[...] = jnp.zeros_like(db_ref)

        def fill(ch, carry):
            base = pl.multiple_of(ch * ABLK, ABLK)
            upad[pl.ds(base + CONV_PAD, ABLK), :] = a_ref[pl.ds(base, ABLK), :] * _sigmoid(g_ref[pl.ds(base, ABLK), :])
            dypad[pl.ds(base, ABLK), :] = dy_ref[pl.ds(base, ABLK), :]
            return carry

        lax.fori_loop(0, nchunk, fill, 0)

        def comp(ch, carry):
            base = pl.multiple_of(ch * ABLK, ABLK)
            dy = dy_ref[pl.ds(base, ABLK), :]
            du = jnp.zeros((ABLK, 128), F32)
            for k in range(CONV_K):
                du = du + dypad[pl.ds(base + (CONV_K - 1 - k), ABLK), :] * w_ref[k:k + 1, :]
                dwacc[k * 8:(k + 1) * 8, :] += _sum8(dy * upad[pl.ds(base + (off + k), ABLK), :])
            db_ref[...] += _sum8(dy)
            a = a_ref[pl.ds(base, ABLK), :]
            s = _sigmoid(g_ref[pl.ds(base, ABLK), :])
            da_ref[pl.ds(base, ABLK), :] = (du * s).astype(BF16)
            dgate_ref[pl.ds(base, ABLK), :] = (du * a * (s * (1.0 - s))).astype(BF16)
            return carry

        lax.fori_loop(0, nchunk, comp, 0)
        dw_ref[...] = dwacc[...].reshape(CONV_PAD, 8, 128).sum(axis=1)

    return pl.pallas_call(
        body, out_shape=[SDS((lp, c), BF16), SDS((lp, c), BF16), SDS((CONV_PAD, c), F32), SDS((8, c), F32)],
        grid=(ncb,),
        in_specs=[pl.BlockSpec((lp, 128), lambda j: (0, j)), pl.BlockSpec((lp, 128), lambda j: (0, j)),
                  pl.BlockSpec((lp, 128), lambda j: (0, j + ncb)), pl.BlockSpec((CONV_PAD, 128), lambda j: (0, j))],
        out_specs=[pl.BlockSpec((lp, 128), lambda j: (0, j)), pl.BlockSpec((lp, 128), lambda j: (0, j)),
                   pl.BlockSpec((CONV_PAD, 128), lambda j: (0, j)), pl.BlockSpec((8, 128), lambda j: (0, j))],
        scratch_shapes=[pltpu.VMEM((lp + CONV_PAD, 128), F32), pltpu.VMEM((lp + CONV_PAD, 128), F32),
                        pltpu.VMEM((CONV_PAD * 8, 128), F32)],
        name=name, compiler_params=_cp())(dyc, cacg, cacg, w)


def _mesh_pos():
    x, y, c = lax.axis_index("x"), lax.axis_index("y"), lax.axis_index("c")
    return x, y, c


def _peer(pos, r):
    x, y, c = pos
    px = (1 - x) if (r >> 2) & 1 else x
    py = (1 - y) if (r >> 1) & 1 else y
    pc = (1 - c) if r & 1 else c
    return (px, py, pc), 4 * px + 2 * py + pc


def _exchange(arrs, scatter, name):
    n = len(arrs)
    any_spec = pl.BlockSpec(memory_space=pl.ANY)

    def body(*refs):
        ins, outs = refs[:n], refs[n:2 * n]
        send, recv, loc = refs[2 * n:]
        pos = _mesh_pos()
        me = 4 * pos[0] + 2 * pos[1] + pos[2]
        started = []
        for a in range(n):
            src_own = ins[a].at[me] if scatter else ins[a]
            own = pltpu.make_async_copy(src_own, outs[a].at[me], loc.at[a])
            own.start()
            started.append(own)
            for r in range(1, N_DEV):
                peer, peer_idx = _peer(pos, r)
                src = ins[a].at[peer_idx] if scatter else ins[a]
                cp = pltpu.make_async_remote_copy(src_ref=src, dst_ref=outs[a].at[me], send_sem=send.at[a, r - 1],
                                                  recv_sem=recv.at[a, r - 1], device_id=peer,
                                                  device_id_type=pl.DeviceIdType.MESH)
                cp.start()
        for a in range(n):
            for r in range(1, N_DEV):
                peer, peer_idx = _peer(pos, r)
                src = ins[a].at[peer_idx] if scatter else ins[a]
                cp = pltpu.make_async_remote_copy(src_ref=src, dst_ref=outs[a].at[peer_idx], send_sem=send.at[a, r - 1],
                                                  recv_sem=recv.at[a, r - 1], device_id=peer,
                                                  device_id_type=pl.DeviceIdType.MESH)
                cp.wait_recv()
                cp.wait_send()
        for own in started:
            own.wait()

    out_shape = [SDS(a.shape if scatter else (N_DEV,) + a.shape, a.dtype) for a in arrs]
    return pl.pallas_call(
        body, out_shape=out_shape, in_specs=[any_spec] * n, out_specs=[any_spec] * n,
        scratch_shapes=[pltpu.SemaphoreType.DMA((n, N_DEV - 1)), pltpu.SemaphoreType.DMA((n, N_DEV - 1)),
                        pltpu.SemaphoreType.DMA((n,))],
        name=name, compiler_params=pltpu.CompilerParams(has_side_effects=True))(*arrs)


def _all_reduce_small(p8, q, name):
    ni, _, w = p8.shape
    nq = q.shape[0]
    rows = ni + nq

    def body(p_ref, q_ref, o_ref, buf, send, recv):
        pos = _mesh_pos()
        me = 4 * pos[0] + 2 * pos[1] + pos[2]
        buf[me] = jnp.concatenate([p_ref[...].sum(axis=1), q_ref[...]], axis=0)
        for r in range(1, N_DEV):
            peer, _ = _peer(pos, r)
            pltpu.make_async_remote_copy(src_ref=buf.at[me], dst_ref=buf.at[me], send_sem=send.at[r - 1],
                                         recv_sem=recv.at[r - 1], device_id=peer,
                                         device_id_type=pl.DeviceIdType.MESH).start()
        for r in range(1, N_DEV):
            peer, peer_idx = _peer(pos, r)
            cp = pltpu.make_async_remote_copy(src_ref=buf.at[me], dst_ref=buf.at[peer_idx], send_sem=send.at[r - 1],
                                              recv_sem=recv.at[r - 1], device_id=peer,
                                              device_id_type=pl.DeviceIdType.MESH)
            cp.wait_recv()
            cp.wait_send()
        acc = buf[0]
        for dev in range(1, N_DEV):
            acc = acc + buf[dev]
        o_ref[...] = acc

    vmem = pl.BlockSpec(memory_space=pltpu.VMEM)
    return pl.pallas_call(
        body, out_shape=SDS((rows, w), F32), in_specs=[vmem, vmem], out_specs=vmem,
        scratch_shapes=[pltpu.VMEM((N_DEV, rows, w), F32), pltpu.SemaphoreType.DMA((N_DEV - 1,)),
                        pltpu.SemaphoreType.DMA((N_DEV - 1,))],
        name=name, compiler_params=pltpu.CompilerParams(has_side_effects=True))(p8, q)


def _adamw_math(w, g, m, v):
    m = ADAM_B1 * m + (1.0 - ADAM_B1) * g
    v = ADAM_B2 * v + (1.0 - ADAM_B2) * (g * g)
    m_hat = m / (1.0 - ADAM_B1 ** ADAM_STEP)
    v_hat = v / (1.0 - ADAM_B2 ** ADAM_STEP)
    delta = -ADAM_LR * (m_hat / (jnp.sqrt(v_hat) + ADAM_EPS) + ADAM_WD * w)
    return delta, m, v


def _adamw_shard(parts, w, m, v, name):
    _, rr, cc = parts.shape
    tr = _tile(rr, 256, 8)

    def body(p_ref, w_ref, m_ref, v_ref, g_out, d_out, m_out, v_out):
        g = p_ref[0].astype(F32)
        for dev in range(1, N_DEV):
            g = g + p_ref[dev].astype(F32)
        delta, mm, vv = _adamw_math(w_ref[...], g, m_ref[...], v_ref[...])
        g_out[...] = g
        d_out[...] = delta
        m_out[...] = mm
        v_out[...] = vv

    blk = pl.BlockSpec((tr, cc), lambda i: (i, 0))
    return pl.pallas_call(
        body, out_shape=[SDS((rr, cc), F32)] * 4, grid=(rr // tr,),
        in_specs=[pl.BlockSpec((N_DEV, tr, cc), lambda i: (0, i, 0)), blk, blk, blk],
        out_specs=[blk] * 4, name=name, compiler_params=_cp())(parts, w, m, v)


def _adamw_flat(g, w, m, v, name):
    def body(g_ref, w_ref, m_ref, v_ref, d_out, m_out, v_out):
        delta, mm, vv = _adamw_math(w_ref[...], g_ref[...], m_ref[...], v_ref[...])
        d_out[...] = delta
        m_out[...] = mm
        v_out[...] = vv

    return pl.pallas_call(body, out_shape=[SDS(g.shape, F32)] * 3, name=name, compiler_params=_cp())(g, w, m, v)


def _from_cols(t):
    return jnp.transpose(t, (1, 0, 2)).reshape(t.shape[1], N_DEV * t.shape[2])


def _to_cols(t):
    k, n = t.shape
    return jnp.transpose(t.reshape(k, N_DEV, n // N_DEV), (1, 0, 2))


def _interleave(gate, up):
    d, f = gate.shape
    nc = f // FF_CHUNK
    return jnp.stack([gate.reshape(d, nc, FF_CHUNK), up.reshape(d, nc, FF_CHUNK)], axis=2).reshape(d, 2 * f)


def _deinterleave(t):
    d, f2 = t.shape
    nc = f2 // (2 * FF_CHUNK)
    t = t.reshape(d, nc, 2, FF_CHUNK)
    return t[:, :, 0].reshape(d, f2 // 2), t[:, :, 1].reshape(d, f2 // 2)


def kernel(x, meta_tokens, mix_norm_g, w_in, conv_dw_w, conv_dw_b, conv_ln_g, conv_ln_b, w_out, ffn_norm_g, w_gate, w_up, w_down, final_norm_g, loss_target, m_meta_tokens, m_mix_norm_g, m_w_in, m_conv_dw_w, m_conv_dw_b, m_conv_ln_g, m_conv_ln_b, m_w_out, m_ffn_norm_g, m_w_gate, m_w_up, m_w_down, m_final_norm_g, v_meta_tokens, v_mix_norm_g, v_w_in, v_conv_dw_w, v_conv_dw_b, v_conv_ln_g, v_conv_ln_b, v_w_out, v_ffn_norm_g, v_w_gate, v_w_up, v_w_down, v_final_norm_g):
    depth, d, in_shard = w_in.shape
    seq = x.shape[1]
    sb = N_HEADS * HEAD_DIM
    cc = conv_dw_w.shape[2] * N_DEV
    ff = w_gate.shape[2] * N_DEV
    assert in_shard * N_DEV == 3 * sb + 2 * cc and x.shape[0] == 1
    lr = N_META + seq
    lp = -(-lr // ABLK) * ABLK
    me = 4 * lax.axis_index("x") + 2 * lax.axis_index("y") + lax.axis_index("c")

    g_in, g_out, g_gate, g_up, g_down, g_meta, g_taps = _exchange(
        [w_in.astype(BF16), w_out.astype(BF16), w_gate.astype(BF16), w_up.astype(BF16), w_down.astype(BF16),
         meta_tokens, conv_dw_w], False, "gather_weights")
    meta_full = _from_cols(g_meta)
    taps = jnp.transpose(g_taps, (1, 2, 0, 3)).reshape(depth, CONV_K, cc)
    taps = jnp.pad(taps, ((0, 0), (0, CONV_PAD - CONV_K), (0, 0)))
    wl = []
    for i in range(depth):
        wl.append(dict(
            w_in=_from_cols(g_in[:, i]),
            w_out=g_out[:, i].reshape(sb + cc, d),
            w_gu=_interleave(_from_cols(g_gate[:, i]), _from_cols(g_up[:, i])),
            w_d=g_down[:, i].reshape(ff, d)))
    tri_fwd, tri_after, tri_before = _tri_consts()

    h = jnp.concatenate([meta_full, x[0], jnp.zeros((lp - lr, d), F32)], axis=0)
    saved = []
    for i in range(depth):
        p = wl[i]
        sv = dict(h_in=h)
        qkv, hn = _rms_mm(h, mix_norm_g[i:i + 1], p["w_in"], 0, 3 * sb, BF16, f"proj_qkv_{i}", True)
        cacg = _rms_mm(h, mix_norm_g[i:i + 1], p["w_in"], 3 * sb, 2 * cc, F32, f"proj_conv_{i}", False)[0]
        attn, rsave = _attn_fwd(qkv, tri_fwd, f"attn_fwd_{i}")
        yc = _conv_fwd_dw(cacg, taps[i], conv_dw_b[i:i + 1], f"conv_fwd_dw_{i}")
        conv = _conv_fwd_ln(yc, conv_ln_g[i:i + 1], conv_ln_b[i:i + 1], f"conv_fwd_ln_{i}")
        mixed = jnp.concatenate([attn, conv], axis=1)
        h = _mm(mixed, p["w_out"], trans_b=False, out_dtype=F32, name=f"mix_out_{i}", res=h)
        sv.update(qkv=qkv, hn=hn, cacg=cacg, rsave=rsave, yc=yc, mixed=mixed, h_mid=h)
        h, hn2, act = _ffn_fwd(h, ffn_norm_g[i:i + 1], p["w_gu"], p["w_d"], f"ffn_fwd_{i}")
        sv.update(hn2=hn2, act=act)
        saved.append(sv)

    tpad = jnp.pad(loss_target[0], ((N_META, lp - lr), (0, 0)))
    dh, dg_final, loss_part = _loss_head(h, tpad, final_norm_g.reshape(1, d), seq, "loss_head")
    loss = lax.psum(loss_part[0, 0], MESH_AXES)

    grads = [None] * depth
    for i in reversed(range(depth)):
        p, sv = wl[i], saved[i]
        d_gu = _ffn_bwd_act(dh, sv["hn2"], p["w_gu"], p["w_d"], f"ffn_bwd_act_{i}")
        gw_d = _mm_tn(sv["act"], dh, f"grad_w_down_{i}")
        gw_gu = _mm_tn(sv["hn2"], d_gu, f"grad_w_gu_{i}")
        dh, dg_ffn = _nt_rmsbwd(d_gu, p["w_gu"], sv["h_mid"], ffn_norm_g[i:i + 1], dh, f"ffn_bwd_in_{i}")
        gw_out = _mm_tn(sv["mixed"], dh, f"grad_w_out_{i}")
        d_attn = _mm(dh, p["w_out"], trans_b=True, out_dtype=BF16, name=f"mix_bwd_attn_{i}", b_row0=0, n_out=sb)
        d_conv = _mm(dh, p["w_out"], trans_b=True, out_dtype=F32, name=f"mix_bwd_conv_{i}", b_row0=sb, n_out=cc)
        dqkv = _attn_bwd(sv["qkv"], d_attn, sv["rsave"], tri_after, tri_before, f"attn_bwd_{i}")
        dyc, dg_ln, db_ln = _conv_bwd_ln(sv["yc"], d_conv, conv_ln_g[i:i + 1], conv_ln_b[i:i + 1], f"conv_bwd_ln_{i}")
        dca, dcg, g_taps_i, db_conv = _conv_bwd_dw(dyc, sv["cacg"], taps[i], f"conv_bwd_dw_{i}")
        dproj = jnp.concatenate([dqkv[0], dqkv[1], dqkv[2], dca, dcg], axis=1)
        gw_in = _mm_tn(sv["hn"], dproj, f"grad_w_in_{i}")
        dh, dg_mix = _nt_rmsbwd(dproj, p["w_in"], sv["h_in"], mix_norm_g[i:i + 1], dh, f"mix_bwd_in_{i}")
        gw_gate, gw_up = _deinterleave(gw_gu)
        grads[i] = dict(w_in=gw_in, w_out=gw_out, w_gate=gw_gate, w_up=gw_up, w_down=gw_d, taps=g_taps_i,
                        dg_mix=dg_mix, dg_ffn=dg_ffn, dg_ln=dg_ln, db_ln=db_ln, db_conv=db_conv)
    grad_x = dh[N_META:lr][None]

    wide = lambda key: jnp.concatenate([jnp.pad(grads[i][key], ((0, 0), (0, d // depth - cc))) for i in range(depth)], axis=1)
    assert depth * cc <= d and d % depth == 0
    p8 = jnp.stack([grads[i]["dg_mix"] for i in range(depth)] + [grads[i]["dg_ffn"] for i in range(depth)]
                   + [dg_final, wide("db_conv"), wide("dg_ln"), wide("db_ln")])
    small = _all_reduce_small(p8, jnp.concatenate([wide("taps"), dh[:N_META]], axis=0), "reduce_small")
    r0 = 2 * depth + 4
    g_mix = small[0:depth]
    g_ffn = small[depth:2 * depth]
    g_final = small[2 * depth]
    narrow = lambda row: jnp.stack([row[i * (d // depth):i * (d // depth) + cc] for i in range(depth)])
    g_cb, g_lg, g_lb = narrow(small[2 * depth + 1]), narrow(small[2 * depth + 2]), narrow(small[2 * depth + 3])
    g_taps_full = jnp.stack([small[r0:r0 + CONV_K, i * (d // depth):i * (d // depth) + cc] for i in range(depth)])
    csh = cc // N_DEV
    g_taps_own = lax.dynamic_slice_in_dim(g_taps_full, me * csh, csh, axis=2)
    g_meta_full = small[r0 + CONV_PAD:r0 + CONV_PAD + N_META]
    msh = d // N_DEV
    g_meta_own = lax.dynamic_slice_in_dim(g_meta_full, me * msh, msh, axis=1)

    small_g = [g_meta_own, g_mix, g_taps_own, g_cb, g_lg, g_lb, g_ffn, g_final]
    small_w = [meta_tokens, mix_norm_g, conv_dw_w, conv_dw_b, conv_ln_g, conv_ln_b, ffn_norm_g, final_norm_g]
    small_m = [m_meta_tokens, m_mix_norm_g, m_conv_dw_w, m_conv_dw_b, m_conv_ln_g, m_conv_ln_b, m_ffn_norm_g, m_final_norm_g]
    small_v = [v_meta_tokens, v_mix_norm_g, v_conv_dw_w, v_conv_dw_b, v_conv_ln_g, v_conv_ln_b, v_ffn_norm_g, v_final_norm_g]
    sizes = [int(math.prod(t.shape)) for t in small_w]
    total = sum(sizes)
    rows = -(-total // (8 * 128)) * 8

    def flat(ts):
        v = jnp.concatenate([t.reshape(-1) for t in ts])
        return jnp.pad(v, (0, rows * 128 - total)).reshape(rows, 128)

    def unflat(t):
        v, out, o = t.reshape(-1), [], 0
        for sz, ref in zip(sizes, small_w):
            out.append(v[o:o + sz].reshape(ref.shape))
            o += sz
        return out

    sd, sm, sv_ = _adamw_flat(flat(small_g), flat(small_w), flat(small_m), flat(small_v), "adamw_small")
    s_delta, s_m, s_v = unflat(sd), unflat(sm), unflat(sv_)

    stk = lambda key, f: jnp.stack([f(grads[i][key]) for i in range(depth)], axis=1).astype(BF16)
    contrib = [stk("w_in", _to_cols),
               stk("w_out", lambda t: t.reshape(N_DEV, (sb + cc) // N_DEV, d)),
               stk("w_gate", _to_cols), stk("w_up", _to_cols),
               stk("w_down", lambda t: t.reshape(N_DEV, ff // N_DEV, d))]
    recv = _exchange(contrib, True, "scatter_grads")
    big = []
    for parts, w, m, v, nm in zip(recv, (w_in, w_out, w_gate, w_up, w_down), (m_w_in, m_w_out, m_w_gate, m_w_up, m_w_down),
                                  (v_w_in, v_w_out, v_w_gate, v_w_up, v_w_down), ("w_in", "w_out", "w_gate", "w_up", "w_down")):
        r2 = w.shape[0] * w.shape[1]
        c2 = w.shape[2]
        res = _adamw_shard(parts.reshape(N_DEV, r2, c2), w.reshape(r2, c2), m.reshape(r2, c2), v.reshape(r2, c2),
                           f"adamw_{nm}")
        big.append([t.reshape(w.shape) for t in res])
    b_in, b_out, b_gate, b_up, b_down = big

    def ordered(k, smalls):
        s_meta, s_mix, s_taps, s_cb, s_lg, s_lb, s_ffn, s_final = smalls
        return [s_meta, s_mix, b_in[k], s_taps, s_cb, s_lg, s_lb, b_out[k], s_ffn, b_gate[k], b_up[k], b_down[k], s_final]

    return (loss, grad_x, *ordered(0, small_g), *ordered(1, s_delta), *ordered(2, s_m), *ordered(3, s_v))
```

```python
import functools
import math

import jax
import jax.numpy as jnp
from jax import lax
from jax.experimental import pallas as pl
from jax.experimental.pallas import tpu as pltpu

F32 = jnp.float32
BF16 = jnp.bfloat16
SDS = jax.ShapeDtypeStruct

N_META = 16
N_HEADS = 8
HEAD_DIM = 64
CONV_K = 31
CONV_PAD = 32
ABLK = 128
ATT_GROUP = 4
FF_CHUNK = 256
EPS = 1e-6
N_DEV = 8
MESH_AXES = ("x", "y", "c")
ADAM_LR = 0.001
ADAM_B1 = 0.9
ADAM_B2 = 0.999
ADAM_EPS = 1e-08
ADAM_WD = 0.01
ADAM_STEP = 10
MIB = 1 << 20
VMEM_LIMIT_MIB = 48


def _cp():
    return pltpu.CompilerParams(vmem_limit_bytes=VMEM_LIMIT_MIB * MIB)


def _tile(n, cap, mult):
    best = None
    for t in range(mult, min(n, cap) + 1, mult):
        if n % t == 0:
            best = t
    assert best is not None, (n, cap, mult)
    return best


def _sum8(v):
    r, c = v.shape
    return v.reshape(r // 8, 8, c).sum(axis=0)


def _sigmoid(x):
    return 1.0 / (1.0 + jnp.exp(-x))


def _rms_mm(h, g, w, w_col0, n_cols, out_dtype, name, hn_out):
    lp, d = h.shape
    tm = _tile(lp, 528, 16)
    tn = _tile(math.gcd(n_cols, w_col0), 512, 128)
    off = w_col0 // tn

    def body(h_ref, g_ref, w_ref, *rest):
        if hn_out:
            o_ref, hn_ref, hn_s = rest
        else:
            o_ref, hn_s = rest

        @pl.when(pl.program_id(1) == 0)
        def _():
            x = h_ref[...]
            r = lax.rsqrt(jnp.mean(x * x, axis=-1, keepdims=True) + EPS)
            hn = ((x * r) * g_ref[...]).astype(BF16)
            hn_s[...] = hn
            if hn_out:
                hn_ref[...] = hn

        o_ref[...] = jnp.dot(hn_s[...], w_ref[...], preferred_element_type=F32).astype(out_dtype)

    out_shape = [SDS((lp, n_cols), out_dtype)]
    out_specs = [pl.BlockSpec((tm, tn), lambda i, j: (i, j))]
    if hn_out:
        out_shape.append(SDS((lp, d), BF16))
        out_specs.append(pl.BlockSpec((tm, d), lambda i, j: (i, 0)))
    return pl.pallas_call(
        body, out_shape=out_shape, grid=(lp // tm, n_cols // tn),
        in_specs=[pl.BlockSpec((tm, d), lambda i, j: (i, 0)),
                  pl.BlockSpec((1, d), lambda i, j: (0, 0)),
                  pl.BlockSpec((d, tn), lambda i, j: (0, j + off))],
        out_specs=out_specs, scratch_shapes=[pltpu.VMEM((tm, d), BF16)],
        name=name, compiler_params=_cp())(h, g, w)


def _mm(a, b, *, trans_b, out_dtype, name, b_row0=0, n_out=None, res=None):
    m, k = a.shape
    n = n_out if n_out is not None else (b.shape[0] if trans_b else b.shape[1])
    tm = _tile(m, 528, 16)
    tn = _tile(n, 512, 128)
    assert b_row0 % tn == 0
    off = b_row0 // tn
    dn = (((1,), (1,)), ((), ())) if trans_b else (((1,), (0,)), ((), ()))

    def body(a_ref, b_ref, *rest):
        o_ref = rest[-1]
        acc = lax.dot_general(a_ref[...].astype(BF16), b_ref[...], dn, preferred_element_type=F32)
        if res is not None:
            acc = acc + rest[0][...]
        o_ref[...] = acc.astype(out_dtype)

    b_spec = (pl.BlockSpec((tn, k), lambda i, j: (j + off, 0)) if trans_b
              else pl.BlockSpec((k, tn), lambda i, j: (0, j)))
    in_specs = [pl.BlockSpec((tm, k), lambda i, j: (i, 0)), b_spec]
    args = [a, b]
    if res is not None:
        in_specs.append(pl.BlockSpec((tm, tn), lambda i, j: (i, j)))
        args.append(res)
    return pl.pallas_call(
        body, out_shape=SDS((m, n), out_dtype), grid=(m // tm, n // tn),
        in_specs=in_specs, out_specs=pl.BlockSpec((tm, tn), lambda i, j: (i, j)),
        name=name, compiler_params=_cp())(*args)


def _mm_tn(a, b, name):
    l, m = a.shape
    n = b.shape[1]
    tm = _tile(m, 1408, 128)
    tn = _tile(n, 1408, 128)
    tl = _tile(l, 512, 128)

    def body(a_ref, b_ref, o_ref):
        @pl.when(pl.program_id(2) == 0)
        def _():
            o_ref[...] = jnp.zeros_like(o_ref)

        o_ref[...] += lax.dot_general(a_ref[...].astype(BF16), b_ref[...].astype(BF16),
                                      (((0,), (0,)), ((), ())), preferred_element_type=F32)

    return pl.pallas_call(
        body, out_shape=SDS((m, n), F32), grid=(m // tm, n // tn, l // tl),
        in_specs=[pl.BlockSpec((tl, tm), lambda i, j, s: (s, i)),
                  pl.BlockSpec((tl, tn), lambda i, j, s: (s, j))],
        out_specs=pl.BlockSpec((tm, tn), lambda i, j, s: (i, j)),
        name=name, compiler_params=_cp())(a, b)


def _nt_rmsbwd(dy, w, h, g, dres, name):
    lp, k = dy.shape
    d = h.shape[1]
    tm = _tile(lp, 528, 16)
    tk = _tile(k, 512, 128)
    nk = k // tk

    def body(dy_ref, w_ref, h_ref, g_ref, dres_ref, o_ref, dg_ref, acc):
        i, kk = pl.program_id(0), pl.program_id(1)

        @pl.when(kk == 0)
        def _():
            acc[...] = jnp.zeros_like(acc)

        @pl.when((kk == 0) & (i == 0))
        def _():
            dg_ref[...] = jnp.zeros_like(dg_ref)

        acc[...] += lax.dot_general(dy_ref[...], w_ref[...], (((1,), (1,)), ((), ())), preferred_element_type=F32)

        @pl.when(kk == nk - 1)
        def _():
            x = h_ref[...]
            r = lax.rsqrt(jnp.mean(x * x, axis=-1, keepdims=True) + EPS)
            xr = x * r
            dyv = acc[...]
            gy = dyv * g_ref[...]
            dx = r * (gy - xr * jnp.mean(gy * xr, axis=-1, keepdims=True))
            o_ref[...] = dres_ref[...] + dx
            dg_ref[...] += _sum8(dyv * xr)

    return pl.pallas_call(
        body, out_shape=[SDS((lp, d), F32), SDS((8, d), F32)], grid=(lp // tm, nk),
        in_specs=[pl.BlockSpec((tm, tk), lambda i, j: (i, j)),
                  pl.BlockSpec((d, tk), lambda i, j: (0, j)),
                  pl.BlockSpec((tm, d), lambda i, j: (i, 0)),
                  pl.BlockSpec((1, d), lambda i, j: (0, 0)),
                  pl.BlockSpec((tm, d), lambda i, j: (i, 0))],
        out_specs=[pl.BlockSpec((tm, d), lambda i, j: (i, 0)), pl.BlockSpec((8, d), lambda i, j: (0, 0))],
        scratch_shapes=[pltpu.VMEM((tm, d), F32)], name=name, compiler_params=_cp())(dy, w, h, g, dres)


def _ffn_fwd(h, g, w_gu, w_d, name):
    lp, d = h.shape
    f = w_d.shape[0]
    tm = _tile(lp, 528, 16)
    nc = f // FF_CHUNK

    def body(h_ref, g_ref, wgu_ref, wd_ref, o_ref, hn_ref, act_ref, hn_s, acc):
        c = pl.program_id(1)

        @pl.when(c == 0)
        def _():
            x = h_ref[...]
            r = lax.rsqrt(jnp.mean(x * x, axis=-1, keepdims=True) + EPS)
            hn = ((x * r) * g_ref[...]).astype(BF16)
            hn_s[...] = hn
            hn_ref[...] = hn
            acc[...] = jnp.zeros_like(acc)

        gu = jnp.dot(hn_s[...], wgu_ref[...], preferred_element_type=F32)
        gg, uu = gu[:, :FF_CHUNK], gu[:, FF_CHUNK:]
        act = ((gg * _sigmoid(gg)) * uu).astype(BF16)
        act_ref[...] = act
        acc[...] += jnp.dot(act, wd_ref[...], preferred_element_type=F32)

        @pl.when(c == nc - 1)
        def _():
            o_ref[...] = h_ref[...] + acc[...]

    return pl.pallas_call(
        body, out_shape=[SDS((lp, d), F32), SDS((lp, d), BF16), SDS((lp, f), BF16)], grid=(lp // tm, nc),
        in_specs=[pl.BlockSpec((tm, d), lambda i, j: (i, 0)),
                  pl.BlockSpec((1, d), lambda i, j: (0, 0)),
                  pl.BlockSpec((d, 2 * FF_CHUNK), lambda i, j: (0, j)),
                  pl.BlockSpec((FF_CHUNK, d), lambda i, j: (j, 0))],
        out_specs=[pl.BlockSpec((tm, d), lambda i, j: (i, 0)),
                   pl.BlockSpec((tm, d), lambda i, j: (i, 0)),
                   pl.BlockSpec((tm, FF_CHUNK), lambda i, j: (i, j))],
        scratch_shapes=[pltpu.VMEM((tm, d), BF16), pltpu.VMEM((tm, d), F32)],
        name=name, compiler_params=_cp())(h, g, w_gu, w_d)


def _ffn_bwd_act(dh, hn, w_gu, w_d, name):
    lp, d = dh.shape
    f = w_d.shape[0]
    tm = _tile(lp, 528, 16)
    nc = f // FF_CHUNK

    def body(dh_ref, hn_ref, wgu_ref, wd_ref, o_ref, dh_s):
        @pl.when(pl.program_id(1) == 0)
        def _():
            dh_s[...] = dh_ref[...].astype(BF16)

        gu = jnp.dot(hn_ref[...], wgu_ref[...], preferred_element_type=F32)
        gg, uu = gu[:, :FF_CHUNK], gu[:, FF_CHUNK:]
        dact = lax.dot_general(dh_s[...], wd_ref[...], (((1,), (1,)), ((), ())), preferred_element_type=F32)
        s = _sigmoid(gg)
        d_g = dact * uu * (s * (1.0 + gg * (1.0 - s)))
        d_u = dact * (gg * s)
        o_ref[...] = jnp.concatenate([d_g, d_u], axis=1).astype(BF16)

    return pl.pallas_call(
        body, out_shape=SDS((lp, 2 * f), BF16), grid=(lp // tm, nc),
        in_specs=[pl.BlockSpec((tm, d), lambda i, j: (i, 0)),
                  pl.BlockSpec((tm, d), lambda i, j: (i, 0)),
                  pl.BlockSpec((d, 2 * FF_CHUNK), lambda i, j: (0, j)),
                  pl.BlockSpec((FF_CHUNK, d), lambda i, j: (j, 0))],
        out_specs=pl.BlockSpec((tm, 2 * FF_CHUNK), lambda i, j: (i, j)),
        scratch_shapes=[pltpu.VMEM((tm, d), BF16)], name=name, compiler_params=_cp())(dh, hn, w_gu, w_d)


def _loss_head(h, tpad, g, n_real, name):
    lp, d = h.shape
    tm = _tile(lp, 528, 16)

    def body(h_ref, t_ref, g_ref, dh_ref, dg_ref, loss_ref):
        i = pl.program_id(0)

        @pl.when(i == 0)
        def _():
            dg_ref[...] = jnp.zeros_like(dg_ref)
            loss_ref[...] = jnp.zeros_like(loss_ref)

        x = h_ref[...]
        r = lax.rsqrt(jnp.mean(x * x, axis=-1, keepdims=True) + EPS)
        xr = x * r
        y = xr * g_ref[...]
        row = i * tm + lax.broadcasted_iota(jnp.int32, (tm, d), 0)
        valid = (row >= N_META) & (row < N_META + n_real)
        diff = jnp.where(valid, y - t_ref[...], 0.0)
        loss_ref[...] += jnp.sum(diff * diff) * (0.5 / d)
        dy = diff * (1.0 / d)
        gy = dy * g_ref[...]
        dh_ref[...] = r * (gy - xr * jnp.mean(gy * xr, axis=-1, keepdims=True))
        dg_ref[...] += _sum8(dy * xr)

    return pl.pallas_call(
        body, out_shape=[SDS((lp, d), F32), SDS((8, d), F32), SDS((8, 128), F32)], grid=(lp // tm,),
        in_specs=[pl.BlockSpec((tm, d), lambda i: (i, 0)),
                  pl.BlockSpec((tm, d), lambda i: (i, 0)),
                  pl.BlockSpec((1, d), lambda i: (0, 0))],
        out_specs=[pl.BlockSpec((tm, d), lambda i: (i, 0)),
                   pl.BlockSpec((8, d), lambda i: (0, 0)),
                   pl.BlockSpec((8, 128), lambda i: (0, 0))],
        name=name, compiler_params=_cp())(h, tpad, g)


def _tri_consts():
    j = lax.broadcasted_iota(jnp.int32, (ABLK, ABLK), 0)
    s = lax.broadcasted_iota(jnp.int32, (ABLK, ABLK), 1)
    after = (j >= s).astype(BF16)
    before = (j < s).astype(BF16)
    ones = jnp.ones((ABLK, ABLK), BF16)
    two = lambda t: jnp.concatenate([t, t], axis=0)
    return (two(jnp.concatenate([after, ones], axis=1)),
            two(after),
            two(jnp.concatenate([before, ones], axis=1)))


def _split_hi_lo(m):
    hi = m.astype(BF16)
    lo = (m - hi.astype(F32)).astype(BF16)
    return jnp.concatenate([hi, lo], axis=1)


def _head_halves(t2, in_a):
    zero = jnp.zeros_like(t2)
    return jnp.concatenate([jnp.where(in_a, t2, zero), jnp.where(in_a, zero, t2)], axis=0)


def _stack_blocks(t, nblk, in_a):
    return jnp.concatenate([_head_halves(t[u * ABLK:(u + 1) * ABLK], in_a) for u in range(nblk)], axis=0)


def _attn_scale():
    scale = 1.0 / math.sqrt(HEAD_DIM)
    assert math.frexp(scale)[0] == 0.5, "a power of two, so that scaling q in bf16 is exact"
    return scale


def _pow2_below(n):
    assert n & (n - 1) == 0
    return [p for p in (64, 32, 16, 8, 4, 2, 1) if p < n]


def _attn_fwd(qkv, tri_fwd, name):
    lp = qkv.shape[0]
    n_pairs = (N_HEADS * HEAD_DIM) // 128
    nb = lp // ABLK
    assert nb <= 128 and 2 * HEAD_DIM == 128
    scale = _attn_scale()
    nt = (((1,), (1,)), ((), ()))

    def body(q_ref, k_ref, v_ref, tri_ref, o_ref, rs_ref, r_s, acc_s, rs_s):
        lane = lax.broadcasted_iota(jnp.int32, (ABLK, 128), 1)
        row = lax.broadcasted_iota(jnp.int32, (ABLK, 128), 0)
        in_a = lane < HEAD_DIM
        causal = lane < row

        def qblock(i, carry):
            q0 = pl.multiple_of(i * ABLK, ABLK)
            q2 = q_ref[pl.ds(q0, ABLK), :] * scale
            r_s[...] = jnp.zeros_like(r_s)
            acc_s[...] = jnp.zeros_like(acc_s)
            rs_s[...] = jnp.zeros_like(rs_s)

            def step(kb0, nblk, diag):
                k0 = pl.multiple_of(kb0 * ABLK, ABLK)
                kbd = _stack_blocks(k_ref[pl.ds(k0, nblk * ABLK), :], nblk, in_a)
                vbd = _stack_blocks(v_ref[pl.ds(k0, nblk * ABLK), :], nblk, in_a)
                z = lax.dot_general(q2, kbd, nt, preferred_element_type=F32)
                zt = [z[:, c * 128:(c + 1) * 128] for c in range(2 * nblk)]
                parts = []
                for c in range(2 * nblk):
                    m = jnp.minimum(-zt[c], 0.0) - jnp.log(1.0 + jnp.exp(-jnp.abs(zt[c])))
                    if diag:
                        m = jnp.where(causal, m, 0.0)
                    parts.append(_split_hi_lo(m))
                ce = jnp.dot(jnp.concatenate(parts, axis=0), tri_ref[...], preferred_element_type=F32)
                rr = [r_s[0], r_s[1]]
                rsv = [rs_s[:, :128], rs_s[:, 128:]]
                ws = [None] * (2 * nblk)
                for u in reversed(range(nblk)):
                    for hh in range(2):
                        c = 2 * u + hh
                        cec = ce[c * 128:(c + 1) * 128]
                        w = jnp.exp(zt[c] + cec[:, :128] + rr[hh])
                        if diag:
                            w = jnp.where(causal, w, 0.0)
                        ws[c] = w.astype(BF16)
                        rsv[hh] = jnp.where(lane == kb0 + u, rr[hh], rsv[hh])
                        rr[hh] = rr[hh] + cec[:, 128:]
                acc_s[...] += jnp.dot(jnp.concatenate(ws, axis=1), vbd, preferred_element_type=F32)
                r_s[0] = rr[0]
                r_s[1] = rr[1]
                rs_s[:, :128] = rsv[0]
                rs_s[:, 128:] = rsv[1]

            step(i, 1, True)
            n_grp = i // ATT_GROUP

            def inner(jj, c2):
                step(i - ATT_GROUP * (jj + 1), ATT_GROUP, False)
                return c2

            lax.fori_loop(0, n_grp, inner, 0)
            rem = i - ATT_GROUP * n_grp
            for p in _pow2_below(ATT_GROUP):
                @pl.when((rem & p) != 0)
                def _():
                    step(rem & (p - 1), p, False)

            o_ref[pl.ds(q0, ABLK), :] = acc_s[...].astype(BF16)
            rs_ref[pl.ds(q0, ABLK), :] = rs_s[...]
            return carry

        lax.fori_loop(0, nb, qblock, 0)

    col = lambda o: (lambda p: (0, p + o))
    return pl.pallas_call(
        body, out_shape=[SDS((lp, n_pairs * 128), BF16), SDS((lp, n_pairs * 256), F32)], grid=(n_pairs,),
        in_specs=[pl.BlockSpec((lp, 128), col(0)), pl.BlockSpec((lp, 128), col(n_pairs)),
                  pl.BlockSpec((lp, 128), col(2 * n_pairs)), pl.BlockSpec((256, 256), lambda p: (0, 0))],
        out_specs=[pl.BlockSpec((lp, 128), col(0)), pl.BlockSpec((lp, 256), col(0))],
        scratch_shapes=[pltpu.VMEM((2, ABLK, 128), F32), pltpu.VMEM((ABLK, 128), F32), pltpu.VMEM((ABLK, 256), F32)],
        name=name, compiler_params=_cp())(qkv, qkv, qkv, tri_fwd)


def _attn_bwd(qkv, d_out, rsave, tri_after, tri_before, name):
    lp = qkv.shape[0]
    n_pairs = (N_HEADS * HEAD_DIM) // 128
    nb = lp // ABLK
    scale = _attn_scale()
    nt = (((1,), (1,)), ((), ()))
    tn = (((0,), (0,)), ((), ()))

    def body(q_ref, k_ref, v_ref, do_ref, rs_ref, ta_ref, tb_ref, o_ref, dk_s, dv_s, dq_s, pc_s):
        lane = lax.broadcasted_iota(jnp.int32, (ABLK, 128), 1)
        row = lax.broadcasted_iota(jnp.int32, (ABLK, 128), 0)
        in_a = lane < HEAD_DIM
        causal = lane < row
        dk_s[...] = jnp.zeros_like(dk_s)
        dv_s[...] = jnp.zeros_like(dv_s)

        def qblock(i, carry):
            q0 = pl.multiple_of(i * ABLK, ABLK)
            q2 = q_ref[pl.ds(q0, ABLK), :] * scale
            do2 = do_ref[pl.ds(q0, ABLK), :]
            q_st = _head_halves(q2, in_a)
            do_st = _head_halves(do2, in_a)
            dq_s[...] = jnp.zeros_like(dq_s)
            pc_s[...] = jnp.zeros_like(pc_s)

            def step(kb0, nblk, diag):
                k0 = pl.multiple_of(kb0 * ABLK, ABLK)
                kbd = _stack_blocks(k_ref[pl.ds(k0, nblk * ABLK), :], nblk, in_a)
                vbd = _stack_blocks(v_ref[pl.ds(k0, nblk * ABLK), :], nblk, in_a)
                z = lax.dot_general(q2, kbd, nt, preferred_element_type=F32)
                dw = lax.dot_general(do2, vbd, nt, preferred_element_type=F32)
                ncol = 2 * nblk
                zt = [z[:, c * 128:(c + 1) * 128] for c in range(ncol)]
                parts, betas = [], []
                for c in range(ncol):
                    e = jnp.exp(-jnp.abs(zt[c]))
                    m = jnp.minimum(-zt[c], 0.0) - jnp.log(1.0 + e)
                    if diag:
                        m = jnp.where(causal, m, 0.0)
                    betas.append(jnp.where(zt[c] >= 0.0, 1.0, e) / (1.0 + e))
                    parts.append(_split_hi_lo(m))
                ex = jnp.dot(jnp.concatenate(parts, axis=0), ta_ref[...], preferred_element_type=F32)
                ws, dls = [], []
                for c in range(ncol):
                    u, hh = c // 2, c % 2
                    r_saved = jnp.sum(jnp.where(lane == kb0 + u, rs_ref[pl.ds(q0, ABLK), hh * 128:(hh + 1) * 128], 0.0),
                                      axis=1, keepdims=True)
                    w = jnp.exp(zt[c] + ex[c * 128:(c + 1) * 128] + r_saved)
                    if diag:
                        w = jnp.where(causal, w, 0.0)
                    ws.append(w.astype(BF16))
                    dls.append(dw[:, c * 128:(c + 1) * 128] * w)
                pe = jnp.dot(jnp.concatenate([_split_hi_lo(dl) for dl in dls], axis=0), tb_ref[...],
                             preferred_element_type=F32)
                pc = [pc_s[0], pc_s[1]]
                dzs = []
                for c in range(ncol):
                    hh = c % 2
                    pec = pe[c * 128:(c + 1) * 128]
                    dz = dls[c] * (1.0 - betas[c]) - (pec[:, :128] + pc[hh]) * betas[c]
                    pc[hh] = pc[hh] + pec[:, 128:]
                    if diag:
                        dz = jnp.where(causal, dz, 0.0)
                    dzs.append(dz.astype(BF16))
                pc_s[0] = pc[0]
                pc_s[1] = pc[1]
                dq_s[...] += jnp.dot(jnp.concatenate(dzs, axis=1), kbd, preferred_element_type=F32)
                by_head = lambda ts: jnp.concatenate([jnp.concatenate(ts[0::2], axis=1), jnp.concatenate(ts[1::2], axis=1)],
                                                     axis=0)
                rows = pl.ds(k0, nblk * ABLK)
                dk_s[rows, :] += lax.dot_general(by_head(dzs), q_st, tn, preferred_element_type=F32)
                dv_s[rows, :] += lax.dot_general(by_head(ws), do_st, tn, preferred_element_type=F32)

            n_grp = i // ATT_GROUP

            def inner(jj, c2):
                step(ATT_GROUP * jj, ATT_GROUP, False)
                return c2

            lax.fori_loop(0, n_grp, inner, 0)
            rem = i - ATT_GROUP * n_grp
            for p in _pow2_below(ATT_GROUP):
                @pl.when((rem & p) != 0)
                def _():
                    step(i - (rem & (2 * p - 1)), p, False)

            step(i, 1, True)
            o_ref[0, pl.ds(q0, ABLK), :] = (dq_s[...] * scale).astype(BF16)
            return carry

        lax.fori_loop(0, nb, qblock, 0)
        o_ref[1] = dk_s[...].astype(BF16)
        o_ref[2] = dv_s[...].astype(BF16)

    col = lambda o: (lambda p: (0, p + o))
    return pl.pallas_call(
        body, out_shape=SDS((3, lp, n_pairs * 128), BF16), grid=(n_pairs,),
        in_specs=[pl.BlockSpec((lp, 128), col(0)), pl.BlockSpec((lp, 128), col(n_pairs)),
                  pl.BlockSpec((lp, 128), col(2 * n_pairs)), pl.BlockSpec((lp, 128), col(0)),
                  pl.BlockSpec((lp, 256), col(0)),
                  pl.BlockSpec((256, 128), lambda p: (0, 0)), pl.BlockSpec((256, 256), lambda p: (0, 0))],
        out_specs=pl.BlockSpec((3, lp, 128), lambda p: (0, 0, p)),
        scratch_shapes=[pltpu.VMEM((lp, 128), F32), pltpu.VMEM((lp, 128), F32),
                        pltpu.VMEM((ABLK, 128), F32), pltpu.VMEM((2, ABLK, 128), F32)],
        name=name, compiler_params=_cp())(qkv, qkv, qkv, d_out, rsave, tri_after, tri_before)


def _conv_fwd_dw(cacg, w, b, name):
    lp = cacg.shape[0]
    c = cacg.shape[1] // 2
    ncb = c // 128
    nchunk = lp // ABLK
    off = CONV_PAD - (CONV_K - 1)

    def body(a_ref, g_ref, w_ref, b_ref, y_ref, upad):
        upad[0:CONV_PAD, :] = jnp.zeros((CONV_PAD, 128), F32)

        def fill(ch, carry):
            base = pl.multiple_of(ch * ABLK, ABLK)
            upad[pl.ds(base + CONV_PAD, ABLK), :] = a_ref[pl.ds(base, ABLK), :] * _sigmoid(g_ref[pl.ds(base, ABLK), :])
            return carry

        lax.fori_loop(0, nchunk, fill, 0)

        def comp(ch, carry):
            base = pl.multiple_of(ch * ABLK, ABLK)
            acc = jnp.zeros((ABLK, 128), F32)
            for k in range(CONV_K):
                acc = acc + upad[pl.ds(base + (off + k), ABLK), :] * w_ref[k:k + 1, :]
            y_ref[pl.ds(base, ABLK), :] = acc + b_ref[...]
            return carry

        lax.fori_loop(0, nchunk, comp, 0)

    return pl.pallas_call(
        body, out_shape=SDS((lp, c), F32), grid=(ncb,),
        in_specs=[pl.BlockSpec((lp, 128), lambda j: (0, j)), pl.BlockSpec((lp, 128), lambda j: (0, j + ncb)),
                  pl.BlockSpec((CONV_PAD, 128), lambda j: (0, j)), pl.BlockSpec((1, 128), lambda j: (0, j))],
        out_specs=pl.BlockSpec((lp, 128), lambda j: (0, j)),
        scratch_shapes=[pltpu.VMEM((lp + CONV_PAD, 128), F32)], name=name, compiler_params=_cp())(cacg, cacg, w, b)


def _ln_parts(x, g, b):
    mu = jnp.mean(x, axis=-1, keepdims=True)
    xc = x - mu
    rstd = lax.rsqrt(jnp.mean(xc * xc, axis=-1, keepdims=True) + EPS)
    xh = xc * rstd
    return xh, rstd, xh * g + b


def _conv_fwd_ln(yc, g, b, name):
    lp, c = yc.shape
    tm = _tile(lp, 528, 16)

    def body(y_ref, g_ref, b_ref, o_ref):
        _, _, ln = _ln_parts(y_ref[...], g_ref[...], b_ref[...])
        o_ref[...] = (ln * _sigmoid(ln)).astype(BF16)

    return pl.pallas_call(
        body, out_shape=SDS((lp, c), BF16), grid=(lp // tm,),
        in_specs=[pl.BlockSpec((tm, c), lambda i: (i, 0)), pl.BlockSpec((1, c), lambda i: (0, 0)),
                  pl.BlockSpec((1, c), lambda i: (0, 0))],
        out_specs=pl.BlockSpec((tm, c), lambda i: (i, 0)), name=name, compiler_params=_cp())(yc, g, b)


def _conv_bwd_ln(yc, dout, g, b, name):
    lp, c = yc.shape
    tm = _tile(lp, 528, 16)

    def body(y_ref, d_ref, g_ref, b_ref, o_ref, dg_ref, db_ref):
        @pl.when(pl.program_id(0) == 0)
        def _():
            dg_ref[...] = jnp.zeros_like(dg_ref)
            db_ref[...] = jnp.zeros_like(db_ref)

        xh, rstd, ln = _ln_parts(y_ref[...], g_ref[...], b_ref[...])
        s = _sigmoid(ln)
        dln = d_ref[...] * (s * (1.0 + ln * (1.0 - s)))
        dg_ref[...] += _sum8(dln * xh)
        db_ref[...] += _sum8(dln)
        dxh = dln * g_ref[...]
        o_ref[...] = rstd * (dxh - jnp.mean(dxh, axis=-1, keepdims=True)
                             - xh * jnp.mean(dxh * xh, axis=-1, keepdims=True))

    return pl.pallas_call(
        body, out_shape=[SDS((lp, c), F32), SDS((8, c), F32), SDS((8, c), F32)], grid=(lp // tm,),
        in_specs=[pl.BlockSpec((tm, c), lambda i: (i, 0)), pl.BlockSpec((tm, c), lambda i: (i, 0)),
                  pl.BlockSpec((1, c), lambda i: (0, 0)), pl.BlockSpec((1, c), lambda i: (0, 0))],
        out_specs=[pl.BlockSpec((tm, c), lambda i: (i, 0)), pl.BlockSpec((8, c), lambda i: (0, 0)),
                   pl.BlockSpec((8, c), lambda i: (0, 0))],
        name=name, compiler_params=_cp())(yc, dout, g, b)


def _conv_bwd_dw(dyc, cacg, w, name):
    lp, c = dyc.shape
    ncb = c // 128
    nchunk = lp // ABLK
    off = CONV_PAD - (CONV_K - 1)

    def body(dy_ref, a_ref, g_ref, w_ref, da_ref, dgate_ref, dw_ref, db_ref, upad, dypad, dwacc):
        upad[0:CONV_PAD, :] = jnp.zeros((CONV_PAD, 128), F32)
        dypad[lp:lp + CONV_PAD, :] = jnp.zeros((CONV_PAD, 128), F32)
        dwacc[...] = jnp.zeros_like(dwacc)
        db_ref[...] = jnp.zeros_like(db_ref)

        def fill(ch, carry):
            base = pl.multiple_of(ch * ABLK, ABLK)
            upad[pl.ds(base + CONV_PAD, ABLK), :] = a_ref[pl.ds(base, ABLK), :] * _sigmoid(g_ref[pl.ds(base, ABLK), :])
            dypad[pl.ds(base, ABLK), :] = dy_ref[pl.ds(base, ABLK), :]
            return carry

        lax.fori_loop(0, nchunk, fill, 0)

        def comp(ch, carry):
            base = pl.multiple_of(ch * ABLK, ABLK)
            dy = dy_ref[pl.ds(base, ABLK), :]
            du = jnp.zeros((ABLK, 128), F32)
            for k in range(CONV_K):
                du = du + dypad[pl.ds(base + (CONV_K - 1 - k), ABLK), :] * w_ref[k:k + 1, :]
                dwacc[k * 8:(k + 1) * 8, :] += _sum8(dy * upad[pl.ds(base + (off + k), ABLK), :])
            db_ref[...] += _sum8(dy)
            a = a_ref[pl.ds(base, ABLK), :]
            s = _sigmoid(g_ref[pl.ds(base, ABLK), :])
            da_ref[pl.ds(base, ABLK), :] = (du * s).astype(BF16)
            dgate_ref[pl.ds(base, ABLK), :] = (du * a * (s * (1.0 - s))).astype(BF16)
            return carry

        lax.fori_loop(0, nchunk, comp, 0)
        dw_ref[...] = dwacc[...].reshape(CONV_PAD, 8, 128).sum(axis=1)

    return pl.pallas_call(
        body, out_shape=[SDS((lp, c), BF16), SDS((lp, c), BF16), SDS((CONV_PAD, c), F32), SDS((8, c), F32)],
        grid=(ncb,),
        in_specs=[pl.BlockSpec((lp, 128), lambda j: (0, j)), pl.BlockSpec((lp, 128), lambda j: (0, j)),
                  pl.BlockSpec((lp, 128), lambda j: (0, j + ncb)), pl.BlockSpec((CONV_PAD, 128), lambda j: (0, j))],
        out_specs=[pl.BlockSpec((lp, 128), lambda j: (0, j)), pl.BlockSpec((lp, 128), lambda j: (0, j)),
                   pl.BlockSpec((CONV_PAD, 128), lambda j: (0, j)), pl.BlockSpec((8, 128), lambda j: (0, j))],
        scratch_shapes=[pltpu.VMEM((lp + CONV_PAD, 128), F32), pltpu.VMEM((lp + CONV_PAD, 128), F32),
                        pltpu.VMEM((CONV_PAD * 8, 128), F32)],
        name=name, compiler_params=_cp())(dyc, cacg, cacg, w)


def _mesh_pos():
    x, y, c = lax.axis_index("x"), lax.axis_index("y"), lax.axis_index("c")
    return x, y, c


def _peer(pos, r):
    x, y, c = pos
    px = (1 - x) if (r >> 2) & 1 else x
    py = (1 - y) if (r >> 1) & 1 else y
    pc = (1 - c) if r & 1 else c
    return (px, py, pc), 4 * px + 2 * py + pc


def _exchange(arrs, scatter, name):
    n = len(arrs)
    any_spec = pl.BlockSpec(memory_space=pl.ANY)

    def body(*refs):
        ins, outs = refs[:n], refs[n:2 * n]
        send, recv, loc = refs[2 * n:]
        pos = _mesh_pos()
        me = 4 * pos[0] + 2 * pos[1] + pos[2]
        started = []
        for a in range(n):
            src_own = ins[a].at[me] if scatter else ins[a]
            own = pltpu.make_async_copy(src_own, outs[a].at[me], loc.at[a])
            own.start()
            started.append(own)
            for r in range(1, N_DEV):
                peer, peer_idx = _peer(pos, r)
                src = ins[a].at[peer_idx] if scatter else ins[a]
                cp = pltpu.make_async_remote_copy(src_ref=src, dst_ref=outs[a].at[me], send_sem=send.at[a, r - 1],
                                                  recv_sem=recv.at[a, r - 1], device_id=peer,
                                                  device_id_type=pl.DeviceIdType.MESH)
                cp.start()
        for a in range(n):
            for r in range(1, N_DEV):
                peer, peer_idx = _peer(pos, r)
                src = ins[a].at[peer_idx] if scatter else ins[a]
                cp = pltpu.make_async_remote_copy(src_ref=src, dst_ref=outs[a].at[peer_idx], send_sem=send.at[a, r - 1],
                                                  recv_sem=recv.at[a, r - 1], device_id=peer,
                                                  device_id_type=pl.DeviceIdType.MESH)
                cp.wait_recv()
                cp.wait_send()
        for own in started:
            own.wait()

    out_shape = [SDS(a.shape if scatter else (N_DEV,) + a.shape, a.dtype) for a in arrs]
    return pl.pallas_call(
        body, out_shape=out_shape, in_specs=[any_spec] * n, out_specs=[any_spec] * n,
        scratch_shapes=[pltpu.SemaphoreType.DMA((n, N_DEV - 1)), pltpu.SemaphoreType.DMA((n, N_DEV - 1)),
                        pltpu.SemaphoreType.DMA((n,))],
        name=name, compiler_params=pltpu.CompilerParams(has_side_effects=True))(*arrs)


def _all_reduce_small(p8, q, name):
    ni, _, w = p8.shape
    nq = q.shape[0]
    rows = ni + nq

    def body(p_ref, q_ref, o_ref, buf, send, recv):
        pos = _mesh_pos()
        me = 4 * pos[0] + 2 * pos[1] + pos[2]
        buf[me] = jnp.concatenate([p_ref[...].sum(axis=1), q_ref[...]], axis=0)
        for r in range(1, N_DEV):
            peer, _ = _peer(pos, r)
            pltpu.make_async_remote_copy(src_ref=buf.at[me], dst_ref=buf.at[me], send_sem=send.at[r - 1],
                                         recv_sem=recv.at[r - 1], device_id=peer,
                                         device_id_type=pl.DeviceIdType.MESH).start()
        for r in range(1, N_DEV):
            peer, peer_idx = _peer(pos, r)
            cp = pltpu.make_async_remote_copy(src_ref=buf.at[me], dst_ref=buf.at[peer_idx], send_sem=send.at[r - 1],
                                              recv_sem=recv.at[r - 1], device_id=peer,
                                              device_id_type=pl.DeviceIdType.MESH)
            cp.wait_recv()
            cp.wait_send()
        acc = buf[0]
        for dev in range(1, N_DEV):
            acc = acc + buf[dev]
        o_ref[...] = acc

    vmem = pl.BlockSpec(memory_space=pltpu.VMEM)
    return pl.pallas_call(
        body, out_shape=SDS((rows, w), F32), in_specs=[vmem, vmem], out_specs=vmem,
        scratch_shapes=[pltpu.VMEM((N_DEV, rows, w), F32), pltpu.SemaphoreType.DMA((N_DEV - 1,)),
                        pltpu.SemaphoreType.DMA((N_DEV - 1,))],
        name=name, compiler_params=pltpu.CompilerParams(has_side_effects=True))(p8, q)


def _adamw_math(w, g, m, v):
    m = ADAM_B1 * m + (1.0 - ADAM_B1) * g
    v = ADAM_B2 * v + (1.0 - ADAM_B2) * (g * g)
    m_hat = m / (1.0 - ADAM_B1 ** ADAM_STEP)
    v_hat = v / (1.0 - ADAM_B2 ** ADAM_STEP)
    delta = -ADAM_LR * (m_hat / (jnp.sqrt(v_hat) + ADAM_EPS) + ADAM_WD * w)
    return delta, m, v


def _adamw_shard(parts, w, m, v, name):
    _, rr, cc = parts.shape
    tr = _tile(rr, 256, 8)

    def body(p_ref, w_ref, m_ref, v_ref, g_out, d_out, m_out, v_out):
        g = p_ref[0].astype(F32)
        for dev in range(1, N_DEV):
            g = g + p_ref[dev].astype(F32)
        delta, mm, vv = _adamw_math(w_ref[...], g, m_ref[...], v_ref[...])
        g_out[...] = g
        d_out[...] = delta
        m_out[...] = mm
        v_out[...] = vv

    blk = pl.BlockSpec((tr, cc), lambda i: (i, 0))
    return pl.pallas_call(
        body, out_shape=[SDS((rr, cc), F32)] * 4, grid=(rr // tr,),
        in_specs=[pl.BlockSpec((N_DEV, tr, cc), lambda i: (0, i, 0)), blk, blk, blk],
        out_specs=[blk] * 4, name=name, compiler_params=_cp())(parts, w, m, v)


def _adamw_flat(g, w, m, v, name):
    def body(g_ref, w_ref, m_ref, v_ref, d_out, m_out, v_out):
        delta, mm, vv = _adamw_math(w_ref[...], g_ref[...], m_ref[...], v_ref[...])
        d_out[...] = delta
        m_out[...] = mm
        v_out[...] = vv

    return pl.pallas_call(body, out_shape=[SDS(g.shape, F32)] * 3, name=name, compiler_params=_cp())(g, w, m, v)


def _from_cols(t):
    return jnp.transpose(t, (1, 0, 2)).reshape(t.shape[1], N_DEV * t.shape[2])


def _to_cols(t):
    k, n = t.shape
    return jnp.transpose(t.reshape(k, N_DEV, n // N_DEV), (1, 0, 2))


def _interleave(gate, up):
    d, f = gate.shape
    nc = f // FF_CHUNK
    return jnp.stack([gate.reshape(d, nc, FF_CHUNK), up.reshape(d, nc, FF_CHUNK)], axis=2).reshape(d, 2 * f)


def _deinterleave(t):
    d, f2 = t.shape
    nc = f2 // (2 * FF_CHUNK)
    t = t.reshape(d, nc, 2, FF_CHUNK)
    return t[:, :, 0].reshape(d, f2 // 2), t[:, :, 1].reshape(d, f2 // 2)


def kernel(x, meta_tokens, mix_norm_g, w_in, conv_dw_w, conv_dw_b, conv_ln_g, conv_ln_b, w_out, ffn_norm_g, w_gate, w_up, w_down, final_norm_g, loss_target, m_meta_tokens, m_mix_norm_g, m_w_in, m_conv_dw_w, m_conv_dw_b, m_conv_ln_g, m_conv_ln_b, m_w_out, m_ffn_norm_g, m_w_gate, m_w_up, m_w_down, m_final_norm_g, v_meta_tokens, v_mix_norm_g, v_w_in, v_conv_dw_w, v_conv_dw_b, v_conv_ln_g, v_conv_ln_b, v_w_out, v_ffn_norm_g, v_w_gate, v_w_up, v_w_down, v_final_norm_g):
    depth, d, in_shard = w_in.shape
    seq = x.shape[1]
    sb = N_HEADS * HEAD_DIM
    cc = conv_dw_w.shape[2] * N_DEV
    ff = w_gate.shape[2] * N_DEV
    assert in_shard * N_DEV == 3 * sb + 2 * cc and x.shape[0] == 1
    lr = N_META + seq
    lp = -(-lr // ABLK) * ABLK
    me = 4 * lax.axis_index("x") + 2 * lax.axis_index("y") + lax.axis_index("c")

    g_in, g_out, g_gate, g_up, g_down, g_meta, g_taps = _exchange(
        [w_in.astype(BF16), w_out.astype(BF16), w_gate.astype(BF16), w_up.astype(BF16), w_down.astype(BF16),
         meta_tokens, conv_dw_w], False, "gather_weights")
    meta_full = _from_cols(g_meta)
    taps = jnp.transpose(g_taps, (1, 2, 0, 3)).reshape(depth, CONV_K, cc)
    taps = jnp.pad(taps, ((0, 0), (0, CONV_PAD - CONV_K), (0, 0)))
    wl = []
    for i in range(depth):
        wl.append(dict(
            w_in=_from_cols(g_in[:, i]),
            w_out=g_out[:, i].reshape(sb + cc, d),
            w_gu=_interleave(_from_cols(g_gate[:, i]), _from_cols(g_up[:, i])),
            w_d=g_down[:, i].reshape(ff, d)))
    tri_fwd, tri_after, tri_before = _tri_consts()

    h = jnp.concatenate([meta_full, x[0], jnp.zeros((lp - lr, d), F32)], axis=0)
    saved = []
    for i in range(depth):
        p = wl[i]
        sv = dict(h_in=h)
        qkv, hn = _rms_mm(h, mix_norm_g[i:i + 1], p["w_in"], 0, 3 * sb, BF16, f"proj_qkv_{i}", True)
        cacg = _rms_mm(h, mix_norm_g[i:i + 1], p["w_in"], 3 * sb, 2 * cc, F32, f"proj_conv_{i}", False)[0]
        attn, rsave = _attn_fwd(qkv, tri_fwd, f"attn_fwd_{i}")
        yc = _conv_fwd_dw(cacg, taps[i], conv_dw_b[i:i + 1], f"conv_fwd_dw_{i}")
        conv = _conv_fwd_ln(yc, conv_ln_g[i:i + 1], conv_ln_b[i:i + 1], f"conv_fwd_ln_{i}")
        mixed = jnp.concatenate([attn, conv], axis=1)
        h = _mm(mixed, p["w_out"], trans_b=False, out_dtype=F32, name=f"mix_out_{i}", res=h)
        sv.update(qkv=qkv, hn=hn, cacg=cacg, rsave=rsave, yc=yc, mixed=mixed, h_mid=h)
        h, hn2, act = _ffn_fwd(h, ffn_norm_g[i:i + 1], p["w_gu"], p["w_d"], f"ffn_fwd_{i}")
        sv.update(hn2=hn2, act=act)
        saved.append(sv)

    tpad = jnp.pad(loss_target[0], ((N_META, lp - lr), (0, 0)))
    dh, dg_final, loss_part = _loss_head(h, tpad, final_norm_g.reshape(1, d), seq, "loss_head")
    loss = lax.psum(loss_part[0, 0], MESH_AXES)

    grads = [None] * depth
    for i in reversed(range(depth)):
        p, sv = wl[i], saved[i]
        d_gu = _ffn_bwd_act(dh, sv["hn2"], p["w_gu"], p["w_d"], f"ffn_bwd_act_{i}")
        gw_d = _mm_tn(sv["act"], dh, f"grad_w_down_{i}")
        gw_gu = _mm_tn(sv["hn2"], d_gu, f"grad_w_gu_{i}")
        dh, dg_ffn = _nt_rmsbwd(d_gu, p["w_gu"], sv["h_mid"], ffn_norm_g[i:i + 1], dh, f"ffn_bwd_in_{i}")
        gw_out = _mm_tn(sv["mixed"], dh, f"grad_w_out_{i}")
        d_attn = _mm(dh, p["w_out"], trans_b=True, out_dtype=BF16, name=f"mix_bwd_attn_{i}", b_row0=0, n_out=sb)
        d_conv = _mm(dh, p["w_out"], trans_b=True, out_dtype=F32, name=f"mix_bwd_conv_{i}", b_row0=sb, n_out=cc)
        dqkv = _attn_bwd(sv["qkv"], d_attn, sv["rsave"], tri_after, tri_before, f"attn_bwd_{i}")
        dyc, dg_ln, db_ln = _conv_bwd_ln(sv["yc"], d_conv, conv_ln_g[i:i + 1], conv_ln_b[i:i + 1], f"conv_bwd_ln_{i}")
        dca, dcg, g_taps_i, db_conv = _conv_bwd_dw(dyc, sv["cacg"], taps[i], f"conv_bwd_dw_{i}")
        dproj = jnp.concatenate([dqkv[0], dqkv[1], dqkv[2], dca, dcg], axis=1)
        gw_in = _mm_tn(sv["hn"], dproj, f"grad_w_in_{i}")
        dh, dg_mix = _nt_rmsbwd(dproj, p["w_in"], sv["h_in"], mix_norm_g[i:i + 1], dh, f"mix_bwd_in_{i}")
        gw_gate, gw_up = _deinterleave(gw_gu)
        grads[i] = dict(w_in=gw_in, w_out=gw_out, w_gate=gw_gate, w_up=gw_up, w_down=gw_d, taps=g_taps_i,
                        dg_mix=dg_mix, dg_ffn=dg_ffn, dg_ln=dg_ln, db_ln=db_ln, db_conv=db_conv)
    grad_x = dh[N_META:lr][None]

    wide = lambda key: jnp.concatenate([jnp.pad(grads[i][key], ((0, 0), (0, d // depth - cc))) for i in range(depth)], axis=1)
    assert depth * cc <= d and d % depth == 0
    p8 = jnp.stack([grads[i]["dg_mix"] for i in range(depth)] + [grads[i]["dg_ffn"] for i in range(depth)]
                   + [dg_final, wide("db_conv"), wide("dg_ln"), wide("db_ln")])
    small = _all_reduce_small(p8, jnp.concatenate([wide("taps"), dh[:N_META]], axis=0), "reduce_small")
    r0 = 2 * depth + 4
    g_mix = small[0:depth]
    g_ffn = small[depth:2 * depth]
    g_final = small[2 * depth]
    narrow = lambda row: jnp.stack([row[i * (d // depth):i * (d // depth) + cc] for i in range(depth)])
    g_cb, g_lg, g_lb = narrow(small[2 * depth + 1]), narrow(small[2 * depth + 2]), narrow(small[2 * depth + 3])
    g_taps_full = jnp.stack([small[r0:r0 + CONV_K, i * (d // depth):i * (d // depth) + cc] for i in range(depth)])
    csh = cc // N_DEV
    g_taps_own = lax.dynamic_slice_in_dim(g_taps_full, me * csh, csh, axis=2)
    g_meta_full = small[r0 + CONV_PAD:r0 + CONV_PAD + N_META]
    msh = d // N_DEV
    g_meta_own = lax.dynamic_slice_in_dim(g_meta_full, me * msh, msh, axis=1)

    small_g = [g_meta_own, g_mix, g_taps_own, g_cb, g_lg, g_lb, g_ffn, g_final]
    small_w = [meta_tokens, mix_norm_g, conv_dw_w, conv_dw_b, conv_ln_g, conv_ln_b, ffn_norm_g, final_norm_g]
    small_m = [m_meta_tokens, m_mix_norm_g, m_conv_dw_w, m_conv_dw_b, m_conv_ln_g, m_conv_ln_b, m_ffn_norm_g, m_final_norm_g]
    small_v = [v_meta_tokens, v_mix_norm_g, v_conv_dw_w, v_conv_dw_b, v_conv_ln_g, v_conv_ln_b, v_ffn_norm_g, v_final_norm_g]
    sizes = [int(math.prod(t.shape)) for t in small_w]
    total = sum(sizes)
    rows = -(-total // (8 * 128)) * 8

    def flat(ts):
        v = jnp.concatenate([t.reshape(-1) for t in ts])
        return jnp.pad(v, (0, rows * 128 - total)).reshape(rows, 128)

    def unflat(t):
        v, out, o = t.reshape(-1), [], 0
        for sz, ref in zip(sizes, small_w):
            out.append(v[o:o + sz].reshape(ref.shape))
            o += sz
        return out

    sd, sm, sv_ = _adamw_flat(flat(small_g), flat(small_w), flat(small_m), flat(small_v), "adamw_small")
    s_delta, s_m, s_v = unflat(sd), unflat(sm), unflat(sv_)

    stk = lambda key, f: jnp.stack([f(grads[i][key]) for i in range(depth)], axis=1).astype(BF16)
    contrib = [stk("w_in", _to_cols),
               stk("w_out", lambda t: t.reshape(N_DEV, (sb + cc) // N_DEV, d)),
               stk("w_gate", _to_cols), stk("w_up", _to_cols),
               stk("w_down", lambda t: t.reshape(N_DEV, ff // N_DEV, d))]
    recv = _exchange(contrib, True, "scatter_grads")
    big = []
    for parts, w, m, v, nm in zip(recv, (w_in, w_out, w_gate, w_up, w_down), (m_w_in, m_w_out, m_w_gate, m_w_up, m_w_down),
                                  (v_w_in, v_w_out, v_w_gate, v_w_up, v_w_down), ("w_in", "w_out", "w_gate", "w_up", "w_down")):
        r2 = w.shape[0] * w.shape[1]
        c2 = w.shape[2]
        res = _adamw_shard(parts.reshape(N_DEV, r2, c2), w.reshape(r2, c2), m.reshape(r2, c2), v.reshape(r2, c2),
                           f"adamw_{nm}")
        big.append([t.reshape(w.shape) for t in res])
    b_in, b_out, b_gate, b_up, b_down = big

    def ordered(k, smalls):
        s_meta, s_mix, s_taps, s_cb, s_lg, s_lb, s_ffn, s_final = smalls
        return [s_meta, s_mix, b_in[k], s_taps, s_cb, s_lg, s_lb, b_out[k], s_ffn, b_gate[k], b_up[k], b_down[k], s_final]

    return (loss, grad_x, *ordered(0, small_g), *ordered(1, s_delta), *ordered(2, s_m), *ordered(3, s_v))
```

```python
import functools
import math

import jax
import jax.numpy as jnp
from jax import lax
from jax.experimental import pallas as pl
from jax.experimental.pallas import tpu as pltpu

F32 = jnp.float32
BF16 = jnp.bfloat16
SDS = jax.ShapeDtypeStruct

N_META = 16
N_HEADS = 8
HEAD_DIM = 64
CONV_K = 31
CONV_PAD = 32
ABLK = 128
ATT_GROUP = 4
FF_CHUNK = 256
EPS = 1e-6
N_DEV = 8
MESH_AXES = ("x", "y", "c")
ADAM_LR = 0.001
ADAM_B1 = 0.9
ADAM_B2 = 0.999
ADAM_EPS = 1e-08
ADAM_WD = 0.01
ADAM_STEP = 10
MIB = 1 << 20
VMEM_LIMIT_MIB = 48


def _cp():
    return pltpu.CompilerParams(vmem_limit_bytes=VMEM_LIMIT_MIB * MIB)


def _tile(n, cap, mult):
    best = None
    for t in range(mult, min(n, cap) + 1, mult):
        if n % t == 0:
            best = t
    assert best is not None, (n, cap, mult)
    return best


def _sum8(v):
    r, c = v.shape
    return v.reshape(r // 8, 8, c).sum(axis=0)


def _sigmoid(x):
    return 1.0 / (1.0 + jnp.exp(-x))


NT = (((1,), (1,)), ((), ()))
TN = (((0,), (0,)), ((), ()))


def _rms_mm(h, g, wt, w_row0, n_cols, out_dtype, name, hn_out):
    lp, d = h.shape
    tm = _tile(lp, 528, 16)
    tn = _tile(math.gcd(n_cols, w_row0), 512, 128)
    off = w_row0 // tn

    def body(h_ref, g_ref, w_ref, *rest):
        if hn_out:
            o_ref, hn_ref, hn_s = rest
        else:
            o_ref, hn_s = rest

        @pl.when(pl.program_id(1) == 0)
        def _():
            x = h_ref[...]
            r = lax.rsqrt(jnp.mean(x * x, axis=-1, keepdims=True) + EPS)
            hn = ((x * r) * g_ref[...]).astype(BF16)
            hn_s[...] = hn
            if hn_out:
                hn_ref[...] = hn

        o_ref[...] = lax.dot_general(hn_s[...], w_ref[...], NT, preferred_element_type=F32).astype(out_dtype)

    out_shape = [SDS((lp, n_cols), out_dtype)]
    out_specs = [pl.BlockSpec((tm, tn), lambda i, j: (i, j))]
    if hn_out:
        out_shape.append(SDS((lp, d), BF16))
        out_specs.append(pl.BlockSpec((tm, d), lambda i, j: (i, 0)))
    return pl.pallas_call(
        body, out_shape=out_shape, grid=(lp // tm, n_cols // tn),
        in_specs=[pl.BlockSpec((tm, d), lambda i, j: (i, 0)),
                  pl.BlockSpec((1, d), lambda i, j: (0, 0)),
                  pl.BlockSpec((tn, d), lambda i, j: (j + off, 0))],
        out_specs=out_specs, scratch_shapes=[pltpu.VMEM((tm, d), BF16)],
        name=name, compiler_params=_cp())(h, g, wt)


def _mm_nt(a, b, b_row0, n_out, out_dtype, name):
    m, k = a.shape
    tm = _tile(m, 528, 16)
    tn = _tile(math.gcd(n_out, b_row0), 512, 128)
    off = b_row0 // tn

    def body(a_ref, b_ref, o_ref):
        o_ref[...] = lax.dot_general(a_ref[...].astype(BF16), b_ref[...], NT,
                                     preferred_element_type=F32).astype(out_dtype)

    return pl.pallas_call(
        body, out_shape=SDS((m, n_out), out_dtype), grid=(m // tm, n_out // tn),
        in_specs=[pl.BlockSpec((tm, k), lambda i, j: (i, 0)), pl.BlockSpec((tn, k), lambda i, j: (j + off, 0))],
        out_specs=pl.BlockSpec((tm, tn), lambda i, j: (i, j)), name=name, compiler_params=_cp())(a, b)


def _mix_out(attn, conv, w, res, name):
    m, ka = attn.shape
    kc = conv.shape[1]
    n = w.shape[1]
    assert ka == kc
    tm = _tile(m, 528, 16)
    tn = _tile(n, 512, 128)

    def body(a_ref, c_ref, wa_ref, wc_ref, r_ref, o_ref):
        o_ref[...] = (r_ref[...] + jnp.dot(a_ref[...], wa_ref[...], preferred_element_type=F32)
                      + jnp.dot(c_ref[...], wc_ref[...], preferred_element_type=F32))

    return pl.pallas_call(
        body, out_shape=SDS((m, n), F32), grid=(m // tm, n // tn),
        in_specs=[pl.BlockSpec((tm, ka), lambda i, j: (i, 0)), pl.BlockSpec((tm, kc), lambda i, j: (i, 0)),
                  pl.BlockSpec((ka, tn), lambda i, j: (0, j)), pl.BlockSpec((kc, tn), lambda i, j: (1, j)),
                  pl.BlockSpec((tm, tn), lambda i, j: (i, j))],
        out_specs=pl.BlockSpec((tm, tn), lambda i, j: (i, j)), name=name, compiler_params=_cp())(attn, conv, w, w, res)


def _mm_tn(a, b, name):
    l, m = a.shape
    n = b.shape[1]
    tm = _tile(m, 1408, 128)
    tn = _tile(n, 1024, 128)
    tl = _tile(l, 512, 128)
    nl = l // tl

    def body(a_ref, b_ref, o_ref, acc):
        @pl.when(pl.program_id(2) == 0)
        def _():
            acc[...] = jnp.zeros_like(acc)

        acc[...] += lax.dot_general(a_ref[...].astype(BF16), b_ref[...].astype(BF16), TN, preferred_element_type=F32)

        @pl.when(pl.program_id(2) == nl - 1)
        def _():
            o_ref[...] = acc[...].astype(BF16)

    return pl.pallas_call(
        body, out_shape=SDS((m, n), BF16), grid=(m // tm, n // tn, nl),
        in_specs=[pl.BlockSpec((tl, tm), lambda i, j, s: (s, i)),
                  pl.BlockSpec((tl, tn), lambda i, j, s: (s, j))],
        out_specs=pl.BlockSpec((tm, tn), lambda i, j, s: (i, j)),
        scratch_shapes=[pltpu.VMEM((tm, tn), F32)], name=name, compiler_params=_cp())(a, b)


def _grad_w_in_t(dqkv, dcc, hn, name):
    nq, l, w = dqkv.shape
    nc = dcc.shape[0]
    d = hn.shape[1]
    assert dcc.shape[2] == w
    tn = _tile(d, 1024, 128)
    tl = _tile(l, 512, 128)
    nl = l // tl

    def body(q_ref, c_ref, b_ref, o_ref, acc):
        p, s = pl.program_id(0), pl.program_id(2)

        @pl.when(s == 0)
        def _():
            acc[...] = jnp.zeros_like(acc)

        @pl.when(p < nq)
        def _():
            acc[...] += lax.dot_general(q_ref[...], b_ref[...], TN, preferred_element_type=F32)

        @pl.when(p >= nq)
        def _():
            acc[...] += lax.dot_general(c_ref[...], b_ref[...], TN, preferred_element_type=F32)

        @pl.when(s == nl - 1)
        def _():
            o_ref[...] = acc[...].astype(BF16)

    out = pl.pallas_call(
        body, out_shape=SDS((nq + nc, w, d), BF16), grid=(nq + nc, d // tn, nl),
        in_specs=[pl.BlockSpec((None, tl, w), lambda p, j, s: (jnp.minimum(p, nq - 1), s, 0)),
                  pl.BlockSpec((None, tl, w), lambda p, j, s: (jnp.maximum(p - nq, 0), s, 0)),
                  pl.BlockSpec((tl, tn), lambda p, j, s: (s, j))],
        out_specs=pl.BlockSpec((None, w, tn), lambda p, j, s: (p, 0, j)),
        scratch_shapes=[pltpu.VMEM((w, tn), F32)], name=name, compiler_params=_cp())(dqkv, dcc, hn)
    return out.reshape((nq + nc) * w, d)


def _rms_bwd_tail(acc, h_ref, g_ref, dres_ref, o_ref, dg_ref):
    x = h_ref[...]
    r = lax.rsqrt(jnp.mean(x * x, axis=-1, keepdims=True) + EPS)
    xr = x * r
    dyv = acc[...]
    gy = dyv * g_ref[...]
    o_ref[...] = dres_ref[...] + r * (gy - xr * jnp.mean(gy * xr, axis=-1, keepdims=True))
    dg_ref[...] += _sum8(dyv * xr)


def _mix_bwd_in(dqkv, dcc, wt, h, g, dres, name):
    nq, lp, w = dqkv.shape
    nc = dcc.shape[0]
    d = h.shape[1]
    tm = _tile(lp, 528, 16)
    npc = nq + nc

    def body(q_ref, c_ref, w_ref, h_ref, g_ref, dres_ref, o_ref, dg_ref, acc):
        i, p = pl.program_id(0), pl.program_id(1)

        @pl.when(p == 0)
        def _():
            acc[...] = jnp.zeros_like(acc)

        @pl.when((p == 0) & (i == 0))
        def _():
            dg_ref[...] = jnp.zeros_like(dg_ref)

        @pl.when(p < nq)
        def _():
            acc[...] += jnp.dot(q_ref[...], w_ref[...], preferred_element_type=F32)

        @pl.when(p >= nq)
        def _():
            acc[...] += jnp.dot(c_ref[...], w_ref[...], preferred_element_type=F32)

        @pl.when(p == npc - 1)
        def _():
            _rms_bwd_tail(acc, h_ref, g_ref, dres_ref, o_ref, dg_ref)

    return pl.pallas_call(
        body, out_shape=[SDS((lp, d), F32), SDS((8, d), F32)], grid=(lp // tm, npc),
        in_specs=[pl.BlockSpec((None, tm, w), lambda i, p: (jnp.minimum(p, nq - 1), i, 0)),
                  pl.BlockSpec((None, tm, w), lambda i, p: (jnp.maximum(p - nq, 0), i, 0)),
                  pl.BlockSpec((w, d), lambda i, p: (p, 0)),
                  pl.BlockSpec((tm, d), lambda i, p: (i, 0)),
                  pl.BlockSpec((1, d), lambda i, p: (0, 0)),
                  pl.BlockSpec((tm, d), lambda i, p: (i, 0))],
        out_specs=[pl.BlockSpec((tm, d), lambda i, p: (i, 0)), pl.BlockSpec((8, d), lambda i, p: (0, 0))],
        scratch_shapes=[pltpu.VMEM((tm, d), F32)], name=name, compiler_params=_cp())(dqkv, dcc, wt, h, g, dres)


def _ffn_bwd_in(d_g, d_u, wg_t, wu_t, h, g, dres, name):
    lp, f = d_g.shape
    d = h.shape[1]
    tm = _tile(lp, 528, 16)
    tk = _tile(f, 512, 128)
    nk = f // tk

    def body(dg_in, du_in, wg_ref, wu_ref, h_ref, g_ref, dres_ref, o_ref, dg_ref, acc):
        i, kk = pl.program_id(0), pl.program_id(1)

        @pl.when(kk == 0)
        def _():
            acc[...] = jnp.zeros_like(acc)

        @pl.when((kk == 0) & (i == 0))
        def _():
            dg_ref[...] = jnp.zeros_like(dg_ref)

        acc[...] += (jnp.dot(dg_in[...], wg_ref[...], preferred_element_type=F32)
                     + jnp.dot(du_in[...], wu_ref[...], preferred_element_type=F32))

        @pl.when(kk == nk - 1)
        def _():
            _rms_bwd_tail(acc, h_ref, g_ref, dres_ref, o_ref, dg_ref)

    return pl.pallas_call(
        body, out_shape=[SDS((lp, d), F32), SDS((8, d), F32)], grid=(lp // tm, nk),
        in_specs=[pl.BlockSpec((tm, tk), lambda i, j: (i, j)), pl.BlockSpec((tm, tk), lambda i, j: (i, j)),
                  pl.BlockSpec((tk, d), lambda i, j: (j, 0)), pl.BlockSpec((tk, d), lambda i, j: (j, 0)),
                  pl.BlockSpec((tm, d), lambda i, j: (i, 0)),
                  pl.BlockSpec((1, d), lambda i, j: (0, 0)),
                  pl.BlockSpec((tm, d), lambda i, j: (i, 0))],
        out_specs=[pl.BlockSpec((tm, d), lambda i, j: (i, 0)), pl.BlockSpec((8, d), lambda i, j: (0, 0))],
        scratch_shapes=[pltpu.VMEM((tm, d), F32)], name=name, compiler_params=_cp())(d_g, d_u, wg_t, wu_t, h, g, dres)


def _ffn_fwd(h, g, wg_t, wu_t, w_d, name):
    lp, d = h.shape
    f = w_d.shape[0]
    tm = _tile(lp, 528, 16)
    nc = f // FF_CHUNK

    def body(h_ref, g_ref, wg_ref, wu_ref, wd_ref, o_ref, hn_ref, act_ref, hn_s, acc):
        c = pl.program_id(1)

        @pl.when(c == 0)
        def _():
            x = h_ref[...]
            r = lax.rsqrt(jnp.mean(x * x, axis=-1, keepdims=True) + EPS)
            hn = ((x * r) * g_ref[...]).astype(BF16)
            hn_s[...] = hn
            hn_ref[...] = hn
            acc[...] = jnp.zeros_like(acc)

        gg = lax.dot_general(hn_s[...], wg_ref[...], NT, preferred_element_type=F32)
        uu = lax.dot_general(hn_s[...], wu_ref[...], NT, preferred_element_type=F32)
        act = ((gg * _sigmoid(gg)) * uu).astype(BF16)
        act_ref[...] = act
        acc[...] += jnp.dot(act, wd_ref[...], preferred_element_type=F32)

        @pl.when(c == nc - 1)
        def _():
            o_ref[...] = h_ref[...] + acc[...]

    chunk = pl.BlockSpec((FF_CHUNK, d), lambda i, j: (j, 0))
    return pl.pallas_call(
        body, out_shape=[SDS((lp, d), F32), SDS((lp, d), BF16), SDS((lp, f), BF16)], grid=(lp // tm, nc),
        in_specs=[pl.BlockSpec((tm, d), lambda i, j: (i, 0)), pl.BlockSpec((1, d), lambda i, j: (0, 0)),
                  chunk, chunk, chunk],
        out_specs=[pl.BlockSpec((tm, d), lambda i, j: (i, 0)),
                   pl.BlockSpec((tm, d), lambda i, j: (i, 0)),
                   pl.BlockSpec((tm, FF_CHUNK), lambda i, j: (i, j))],
        scratch_shapes=[pltpu.VMEM((tm, d), BF16), pltpu.VMEM((tm, d), F32)],
        name=name, compiler_params=_cp())(h, g, wg_t, wu_t, w_d)


def _ffn_bwd_act(dh, hn, wg_t, wu_t, w_d, name):
    lp, d = dh.shape
    f = w_d.shape[0]
    tm = _tile(lp, 528, 16)
    nc = f // FF_CHUNK

    def body(dh_ref, hn_ref, wg_ref, wu_ref, wd_ref, dg_out, du_out, dh_s):
        @pl.when(pl.program_id(1) == 0)
        def _():
            dh_s[...] = dh_ref[...].astype(BF16)

        gg = lax.dot_general(hn_ref[...], wg_ref[...], NT, preferred_element_type=F32)
        uu = lax.dot_general(hn_ref[...], wu_ref[...], NT, preferred_element_type=F32)
        dact = lax.dot_general(dh_s[...], wd_ref[...], NT, preferred_element_type=F32)
        s = _sigmoid(gg)
        dg_out[...] = (dact * uu * (s * (1.0 + gg * (1.0 - s)))).astype(BF16)
        du_out[...] = (dact * (gg * s)).astype(BF16)

    chunk = pl.BlockSpec((FF_CHUNK, d), lambda i, j: (j, 0))
    out = pl.BlockSpec((tm, FF_CHUNK), lambda i, j: (i, j))
    return pl.pallas_call(
        body, out_shape=[SDS((lp, f), BF16), SDS((lp, f), BF16)], grid=(lp // tm, nc),
        in_specs=[pl.BlockSpec((tm, d), lambda i, j: (i, 0)), pl.BlockSpec((tm, d), lambda i, j: (i, 0)),
                  chunk, chunk, chunk],
        out_specs=[out, out], scratch_shapes=[pltpu.VMEM((tm, d), BF16)],
        name=name, compiler_params=_cp())(dh, hn, wg_t, wu_t, w_d)


def _loss_head(h, tpad, g, n_real, name):
    lp, d = h.shape
    tm = _tile(lp, 528, 16)

    def body(h_ref, t_ref, g_ref, dh_ref, dg_ref, loss_ref):
        i = pl.program_id(0)

        @pl.when(i == 0)
        def _():
            dg_ref[...] = jnp.zeros_like(dg_ref)
            loss_ref[...] = jnp.zeros_like(loss_ref)

        x = h_ref[...]
        r = lax.rsqrt(jnp.mean(x * x, axis=-1, keepdims=True) + EPS)
        xr = x * r
        y = xr * g_ref[...]
        row = i * tm + lax.broadcasted_iota(jnp.int32, (tm, d), 0)
        valid = (row >= N_META) & (row < N_META + n_real)
        diff = jnp.where(valid, y - t_ref[...], 0.0)
        loss_ref[...] += jnp.sum(diff * diff) * (0.5 / d)
        dy = diff * (1.0 / d)
        gy = dy * g_ref[...]
        dh_ref[...] = r * (gy - xr * jnp.mean(gy * xr, axis=-1, keepdims=True))
        dg_ref[...] += _sum8(dy * xr)

    return pl.pallas_call(
        body, out_shape=[SDS((lp, d), F32), SDS((8, d), F32), SDS((8, 128), F32)], grid=(lp // tm,),
        in_specs=[pl.BlockSpec((tm, d), lambda i: (i, 0)),
                  pl.BlockSpec((tm, d), lambda i: (i, 0)),
                  pl.BlockSpec((1, d), lambda i: (0, 0))],
        out_specs=[pl.BlockSpec((tm, d), lambda i: (i, 0)),
                   pl.BlockSpec((8, d), lambda i: (0, 0)),
                   pl.BlockSpec((8, 128), lambda i: (0, 0))],
        name=name, compiler_params=_cp())(h, tpad, g)


def _tri_consts():
    j = lax.broadcasted_iota(jnp.int32, (ABLK, ABLK), 0)
    s = lax.broadcasted_iota(jnp.int32, (ABLK, ABLK), 1)
    after = (j >= s).astype(BF16)
    before = (j < s).astype(BF16)
    ones = jnp.ones((ABLK, ABLK), BF16)
    two = lambda t: jnp.concatenate([t, t], axis=0)
    return (two(jnp.concatenate([after, ones], axis=1)),
            two(after),
            two(jnp.concatenate([before, ones], axis=1)))


def _split_hi_lo(m):
    hi = m.astype(BF16)
    lo = (m - hi.astype(F32)).astype(BF16)
    return jnp.concatenate([hi, lo], axis=1)


def _head_halves(t2, in_a):
    zero = jnp.zeros_like(t2)
    return jnp.concatenate([jnp.where(in_a, t2, zero), jnp.where(in_a, zero, t2)], axis=0)


def _stack_blocks(t, nblk, in_a):
    return jnp.concatenate([_head_halves(t[u * ABLK:(u + 1) * ABLK], in_a) for u in range(nblk)], axis=0)


def _attn_scale():
    scale = 1.0 / math.sqrt(HEAD_DIM)
    assert math.frexp(scale)[0] == 0.5, "a power of two, so that scaling q in bf16 is exact"
    return scale


def _pow2_below(n):
    assert n & (n - 1) == 0
    return [p for p in (64, 32, 16, 8, 4, 2, 1) if p < n]


def _attn_fwd(qkv, tri_fwd, name):
    lp = qkv.shape[0]
    n_pairs = (N_HEADS * HEAD_DIM) // 128
    nb = lp // ABLK
    assert nb <= 128 and 2 * HEAD_DIM == 128
    scale = _attn_scale()
    nt = (((1,), (1,)), ((), ()))

    def body(q_ref, k_ref, v_ref, tri_ref, o_ref, rs_ref, r_s, acc_s, rs_s):
        lane = lax.broadcasted_iota(jnp.int32, (ABLK, 128), 1)
        row = lax.broadcasted_iota(jnp.int32, (ABLK, 128), 0)
        in_a = lane < HEAD_DIM
        causal = lane < row

        def qblock(i, carry):
            q0 = pl.multiple_of(i * ABLK, ABLK)
            q2 = q_ref[pl.ds(q0, ABLK), :] * scale
            r_s[...] = jnp.zeros_like(r_s)
            acc_s[...] = jnp.zeros_like(acc_s)
            rs_s[...] = jnp.zeros_like(rs_s)

            def step(kb0, nblk, diag):
                k0 = pl.multiple_of(kb0 * ABLK, ABLK)
                kbd = _stack_blocks(k_ref[pl.ds(k0, nblk * ABLK), :], nblk, in_a)
                vbd = _stack_blocks(v_ref[pl.ds(k0, nblk * ABLK), :], nblk, in_a)
                z = lax.dot_general(q2, kbd, nt, preferred_element_type=F32)
                zt = [z[:, c * 128:(c + 1) * 128] for c in range(2 * nblk)]
                parts = []
                for c in range(2 * nblk):
                    m = jnp.minimum(-zt[c], 0.0) - jnp.log(1.0 + jnp.exp(-jnp.abs(zt[c])))
                    if diag:
                        m = jnp.where(causal, m, 0.0)
                    parts.append(_split_hi_lo(m))
                ce = jnp.dot(jnp.concatenate(parts, axis=0), tri_ref[...], preferred_element_type=F32)
                rr = [r_s[0], r_s[1]]
                rsv = [rs_s[:, :128], rs_s[:, 128:]]
                ws = [None] * (2 * nblk)
                for u in reversed(range(nblk)):
                    for hh in range(2):
                        c = 2 * u + hh
                        cec = ce[c * 128:(c + 1) * 128]
                        w = jnp.exp(zt[c] + cec[:, :128] + rr[hh])
                        if diag:
                            w = jnp.where(causal, w, 0.0)
                        ws[c] = w.astype(BF16)
                        rsv[hh] = jnp.where(lane == kb0 + u, rr[hh], rsv[hh])
                        rr[hh] = rr[hh] + cec[:, 128:]
                acc_s[...] += jnp.dot(jnp.concatenate(ws, axis=1), vbd, preferred_element_type=F32)
                r_s[0] = rr[0]
                r_s[1] = rr[1]
                rs_s[:, :128] = rsv[0]
                rs_s[:, 128:] = rsv[1]

            step(i, 1, True)
            n_grp = i // ATT_GROUP

            def inner(jj, c2):
                step(i - ATT_GROUP * (jj + 1), ATT_GROUP, False)
                return c2

            lax.fori_loop(0, n_grp, inner, 0)
            rem = i - ATT_GROUP * n_grp
            for p in _pow2_below(ATT_GROUP):
                @pl.when((rem & p) != 0)
                def _():
                    step(rem & (p - 1), p, False)

            o_ref[pl.ds(q0, ABLK), :] = acc_s[...].astype(BF16)
            rs_ref[pl.ds(q0, ABLK), :] = rs_s[...]
            return carry

        lax.fori_loop(0, nb, qblock, 0)

    col = lambda o: (lambda p: (0, p + o))
    return pl.pallas_call(
        body, out_shape=[SDS((lp, n_pairs * 128), BF16), SDS((lp, n_pairs * 256), F32)], grid=(n_pairs,),
        in_specs=[pl.BlockSpec((lp, 128), col(0)), pl.BlockSpec((lp, 128), col(n_pairs)),
                  pl.BlockSpec((lp, 128), col(2 * n_pairs)), pl.BlockSpec((256, 256), lambda p: (0, 0))],
        out_specs=[pl.BlockSpec((lp, 128), col(0)), pl.BlockSpec((lp, 256), col(0))],
        scratch_shapes=[pltpu.VMEM((2, ABLK, 128), F32), pltpu.VMEM((ABLK, 128), F32), pltpu.VMEM((ABLK, 256), F32)],
        name=name, compiler_params=_cp())(qkv, qkv, qkv, tri_fwd)


def _attn_bwd(qkv, d_out, rsave, tri_after, tri_before, name):
    lp = qkv.shape[0]
    n_pairs = (N_HEADS * HEAD_DIM) // 128
    nb = lp // ABLK
    scale = _attn_scale()
    nt = (((1,), (1,)), ((), ()))
    tn = (((0,), (0,)), ((), ()))

    def body(q_ref, k_ref, v_ref, do_ref, rs_ref, ta_ref, tb_ref, o_ref, dk_s, dv_s, dq_s, pc_s):
        lane = lax.broadcasted_iota(jnp.int32, (ABLK, 128), 1)
        row = lax.broadcasted_iota(jnp.int32, (ABLK, 128), 0)
        in_a = lane < HEAD_DIM
        causal = lane < row
        dk_s[...] = jnp.zeros_like(dk_s)
        dv_s[...] = jnp.zeros_like(dv_s)

        def qblock(i, carry):
            q0 = pl.multiple_of(i * ABLK, ABLK)
            q2 = q_ref[pl.ds(q0, ABLK), :] * scale
            do2 = do_ref[pl.ds(q0, ABLK), :]
            q_st = _head_halves(q2, in_a)
            do_st = _head_halves(do2, in_a)
            dq_s[...] = jnp.zeros_like(dq_s)
            pc_s[...] = jnp.zeros_like(pc_s)

            def step(kb0, nblk, diag):
                k0 = pl.multiple_of(kb0 * ABLK, ABLK)
                kbd = _stack_blocks(k_ref[pl.ds(k0, nblk * ABLK), :], nblk, in_a)
                vbd = _stack_blocks(v_ref[pl.ds(k0, nblk * ABLK), :], nblk, in_a)
                z = lax.dot_general(q2, kbd, nt, preferred_element_type=F32)
                dw = lax.dot_general(do2, vbd, nt, preferred_element_type=F32)
                ncol = 2 * nblk
                zt = [z[:, c * 128:(c + 1) * 128] for c in range(ncol)]
                parts, betas = [], []
                for c in range(ncol):
                    e = jnp.exp(-jnp.abs(zt[c]))
                    m = jnp.minimum(-zt[c], 0.0) - jnp.log(1.0 + e)
                    if diag:
                        m = jnp.where(causal, m, 0.0)
                    betas.append(jnp.where(zt[c] >= 0.0, 1.0, e) / (1.0 + e))
                    parts.append(_split_hi_lo(m))
                ex = jnp.dot(jnp.concatenate(parts, axis=0), ta_ref[...], preferred_element_type=F32)
                ws, dls = [], []
                for c in range(ncol):
                    u, hh = c // 2, c % 2
                    r_saved = jnp.sum(jnp.where(lane == kb0 + u, rs_ref[pl.ds(q0, ABLK), hh * 128:(hh + 1) * 128], 0.0),
                                      axis=1, keepdims=True)
                    w = jnp.exp(zt[c] + ex[c * 128:(c + 1) * 128] + r_saved)
                    if diag:
                        w = jnp.where(causal, w, 0.0)
                    ws.append(w.astype(BF16))
                    dls.append(dw[:, c * 128:(c + 1) * 128] * w)
                pe = jnp.dot(jnp.concatenate([_split_hi_lo(dl) for dl in dls], axis=0), tb_ref[...],
                             preferred_element_type=F32)
                pc = [pc_s[0], pc_s[1]]
                dzs = []
                for c in range(ncol):
                    hh = c % 2
                    pec = pe[c * 128:(c + 1) * 128]
                    dz = dls[c] * (1.0 - betas[c]) - (pec[:, :128] + pc[hh]) * betas[c]
                    pc[hh] = pc[hh] + pec[:, 128:]
                    if diag:
                        dz = jnp.where(causal, dz, 0.0)
                    dzs.append(dz.astype(BF16))
                pc_s[0] = pc[0]
                pc_s[1] = pc[1]
                dq_s[...] += jnp.dot(jnp.concatenate(dzs, axis=1), kbd, preferred_element_type=F32)
                by_head = lambda ts: jnp.concatenate([jnp.concatenate(ts[0::2], axis=1), jnp.concatenate(ts[1::2], axis=1)],
                                                     axis=0)
                rows = pl.ds(k0, nblk * ABLK)
                dk_s[rows, :] += lax.dot_general(by_head(dzs), q_st, tn, preferred_element_type=F32)
                dv_s[rows, :] += lax.dot_general(by_head(ws), do_st, tn, preferred_element_type=F32)

            n_grp = i // ATT_GROUP

            def inner(jj, c2):
                step(ATT_GROUP * jj, ATT_GROUP, False)
                return c2

            lax.fori_loop(0, n_grp, inner, 0)
            rem = i - ATT_GROUP * n_grp
            for p in _pow2_below(ATT_GROUP):
                @pl.when((rem & p) != 0)
                def _():
                    step(i - (rem & (2 * p - 1)), p, False)

            step(i, 1, True)
            o_ref[0, pl.ds(q0, ABLK), :] = (dq_s[...] * scale).astype(BF16)
            return carry

        lax.fori_loop(0, nb, qblock, 0)
        o_ref[1] = dk_s[...].astype(BF16)
        o_ref[2] = dv_s[...].astype(BF16)

    col = lambda o: (lambda p: (0, p + o))
    return pl.pallas_call(
        body, out_shape=SDS((3, lp, n_pairs * 128), BF16), grid=(n_pairs,),
        in_specs=[pl.BlockSpec((lp, 128), col(0)), pl.BlockSpec((lp, 128), col(n_pairs)),
                  pl.BlockSpec((lp, 128), col(2 * n_pairs)), pl.BlockSpec((lp, 128), col(0)),
                  pl.BlockSpec((lp, 256), col(0)),
                  pl.BlockSpec((256, 128), lambda p: (0, 0)), pl.BlockSpec((256, 256), lambda p: (0, 0))],
        out_specs=pl.BlockSpec((3, lp, 128), lambda p: (0, 0, p)),
        scratch_shapes=[pltpu.VMEM((lp, 128), F32), pltpu.VMEM((lp, 128), F32),
                        pltpu.VMEM((ABLK, 128), F32), pltpu.VMEM((2, ABLK, 128), F32)],
        name=name, compiler_params=_cp())(qkv, qkv, qkv, d_out, rsave, tri_after, tri_before)


def _conv_fwd_dw(cacg, w, b, name):
    lp = cacg.shape[0]
    c = cacg.shape[1] // 2
    ncb = c // 128
    nchunk = lp // ABLK
    off = CONV_PAD - (CONV_K - 1)

    def body(a_ref, g_ref, w_ref, b_ref, y_ref, upad):
        upad[0:CONV_PAD, :] = jnp.zeros((CONV_PAD, 128), F32)

        def fill(ch, carry):
            base = pl.multiple_of(ch * ABLK, ABLK)
            upad[pl.ds(base + CONV_PAD, ABLK), :] = a_ref[pl.ds(base, ABLK), :] * _sigmoid(g_ref[pl.ds(base, ABLK), :])
            return carry

        lax.fori_loop(0, nchunk, fill, 0)

        def comp(ch, carry):
            base = pl.multiple_of(ch * ABLK, ABLK)
            acc = jnp.zeros((ABLK, 128), F32)
            for k in range(CONV_K):
                acc = acc + upad[pl.ds(base + (off + k), ABLK), :] * w_ref[k:k + 1, :]
            y_ref[pl.ds(base, ABLK), :] = acc + b_ref[...]
            return carry

        lax.fori_loop(0, nchunk, comp, 0)

    return pl.pallas_call(
        body, out_shape=SDS((lp, c), F32), grid=(ncb,),
        in_specs=[pl.BlockSpec((lp, 128), lambda j: (0, j)), pl.BlockSpec((lp, 128), lambda j: (0, j + ncb)),
                  pl.BlockSpec((CONV_PAD, 128), lambda j: (0, j)), pl.BlockSpec((1, 128), lambda j: (0, j))],
        out_specs=pl.BlockSpec((lp, 128), lambda j: (0, j)),
        scratch_shapes=[pltpu.VMEM((lp + CONV_PAD, 128), F32)], name=name, compiler_params=_cp())(cacg, cacg, w, b)


def _ln_parts(x, g, b):
    mu = jnp.mean(x, axis=-1, keepdims=True)
    xc = x - mu
    rstd = lax.rsqrt(jnp.mean(xc * xc, axis=-1, keepdims=True) + EPS)
    xh = xc * rstd
    return xh, rstd, xh * g + b


def _conv_fwd_ln(yc, g, b, name):
    lp, c = yc.shape
    tm = _tile(lp, 528, 16)

    def body(y_ref, g_ref, b_ref, o_ref):
        _, _, ln = _ln_parts(y_ref[...], g_ref[...], b_ref[...])
        o_ref[...] = (ln * _sigmoid(ln)).astype(BF16)

    return pl.pallas_call(
        body, out_shape=SDS((lp, c), BF16), grid=(lp // tm,),
        in_specs=[pl.BlockSpec((tm, c), lambda i: (i, 0)), pl.BlockSpec((1, c), lambda i: (0, 0)),
                  pl.BlockSpec((1, c), lambda i: (0, 0))],
        out_specs=pl.BlockSpec((tm, c), lambda i: (i, 0)), name=name, compiler_params=_cp())(yc, g, b)


def _conv_bwd_ln(yc, dout, g, b, name):
    lp, c = yc.shape
    tm = _tile(lp, 528, 16)

    def body(y_ref, d_ref, g_ref, b_ref, o_ref, dg_ref, db_ref):
        @pl.when(pl.program_id(0) == 0)
        def _():
            dg_ref[...] = jnp.zeros_like(dg_ref)
            db_ref[...] = jnp.zeros_like(db_ref)

        xh, rstd, ln = _ln_parts(y_ref[...], g_ref[...], b_ref[...])
        s = _sigmoid(ln)
        dln = d_ref[...] * (s * (1.0 + ln * (1.0 - s)))
        dg_ref[...] += _sum8(dln * xh)
        db_ref[...] += _sum8(dln)
        dxh = dln * g_ref[...]
        o_ref[...] = rstd * (dxh - jnp.mean(dxh, axis=-1, keepdims=True)
                             - xh * jnp.mean(dxh * xh, axis=-1, keepdims=True))

    return pl.pallas_call(
        body, out_shape=[SDS((lp, c), F32), SDS((8, c), F32), SDS((8, c), F32)], grid=(lp // tm,),
        in_specs=[pl.BlockSpec((tm, c), lambda i: (i, 0)), pl.BlockSpec((tm, c), lambda i: (i, 0)),
                  pl.BlockSpec((1, c), lambda i: (0, 0)), pl.BlockSpec((1, c), lambda i: (0, 0))],
        out_specs=[pl.BlockSpec((tm, c), lambda i: (i, 0)), pl.BlockSpec((8, c), lambda i: (0, 0)),
                   pl.BlockSpec((8, c), lambda i: (0, 0))],
        name=name, compiler_params=_cp())(yc, dout, g, b)


def _conv_bwd_dw(dyc, cacg, w, name):
    lp, c = dyc.shape
    ncb = c // 128
    nchunk = lp // ABLK
    off = CONV_PAD - (CONV_K - 1)

    def body(dy_ref, a_ref, g_ref, w_ref, dcc_ref, dw_ref, db_ref, upad, dypad, dwacc):
        upad[0:CONV_PAD, :] = jnp.zeros((CONV_PAD, 128), F32)
        dypad[lp:lp + CONV_PAD, :] = jnp.zeros((CONV_PAD, 128), F32)
        dwacc[...] = jnp.zeros_like(dwacc)
        db_ref[...] = jnp.zeros_like(db_ref)

        def fill(ch, carry):
            base = pl.multiple_of(ch * ABLK, ABLK)
            upad[pl.ds(base + CONV_PAD, ABLK), :] = a_ref[pl.ds(base, ABLK), :] * _sigmoid(g_ref[pl.ds(base, ABLK), :])
            dypad[pl.ds(base, ABLK), :] = dy_ref[pl.ds(base, ABLK), :]
            return carry

        lax.fori_loop(0, nchunk, fill, 0)

        def comp(ch, carry):
            base = pl.multiple_of(ch * ABLK, ABLK)
            dy = dy_ref[pl.ds(base, ABLK), :]
            du = jnp.zeros((ABLK, 128), F32)
            for k in range(CONV_K):
                du = du + dypad[pl.ds(base + (CONV_K - 1 - k), ABLK), :] * w_ref[k:k + 1, :]
                dwacc[k * 8:(k + 1) * 8, :] += _sum8(dy * upad[pl.ds(base + (off + k), ABLK), :])
            db_ref[...] += _sum8(dy)
            a = a_ref[pl.ds(base, ABLK), :]
            s = _sigmoid(g_ref[pl.ds(base, ABLK), :])
            dcc_ref[0, pl.ds(base, ABLK), :] = (du * s).astype(BF16)
            dcc_ref[1, pl.ds(base, ABLK), :] = (du * a * (s * (1.0 - s))).astype(BF16)
            return carry

        lax.fori_loop(0, nchunk, comp, 0)
        dw_ref[...] = dwacc[...].reshape(CONV_PAD, 8, 128).sum(axis=1)

    return pl.pallas_call(
        body, out_shape=[SDS((2, lp, c), BF16), SDS((CONV_PAD, c), F32), SDS((8, c), F32)],
        grid=(ncb,),
        in_specs=[pl.BlockSpec((lp, 128), lambda j: (0, j)), pl.BlockSpec((lp, 128), lambda j: (0, j)),
                  pl.BlockSpec((lp, 128), lambda j: (0, j + ncb)), pl.BlockSpec((CONV_PAD, 128), lambda j: (0, j))],
        out_specs=[pl.BlockSpec((2, lp, 128), lambda j: (0, 0, j)),
                   pl.BlockSpec((CONV_PAD, 128), lambda j: (0, j)), pl.BlockSpec((8, 128), lambda j: (0, j))],
        scratch_shapes=[pltpu.VMEM((lp + CONV_PAD, 128), F32), pltpu.VMEM((lp + CONV_PAD, 128), F32),
                        pltpu.VMEM((CONV_PAD * 8, 128), F32)],
        name=name, compiler_params=_cp())(dyc, cacg, cacg, w)


def _mesh_pos():
    x, y, c = lax.axis_index("x"), lax.axis_index("y"), lax.axis_index("c")
    return x, y, c


def _peer(pos, r):
    x, y, c = pos
    px = (1 - x) if (r >> 2) & 1 else x
    py = (1 - y) if (r >> 1) & 1 else y
    pc = (1 - c) if r & 1 else c
    return (px, py, pc), 4 * px + 2 * py + pc


class _Job:
    def __init__(self, src, dst, scatter, src_layer=None, dst_layer=None):
        self.src, self.dst, self.scatter, self.src_layer, self.dst_layer = src, dst, scatter, src_layer, dst_layer

    def src_view(self, ins, idx):
        v = ins[self.src] if self.src_layer is None else ins[self.src].at[self.src_layer]
        return v.at[idx] if self.scatter else v

    def dst_view(self, outs, slot):
        v = outs[self.dst] if self.dst_layer is None else outs[self.dst].at[self.dst_layer]
        return v.at[slot]


def _exchange_start(jobs, ins, outs, send, recv, loc):
    pos = _mesh_pos()
    me = 4 * pos[0] + 2 * pos[1] + pos[2]
    for j, job in enumerate(jobs):
        pltpu.make_async_copy(job.src_view(ins, me), job.dst_view(outs, me), loc.at[j]).start()
        for r in range(1, N_DEV):
            peer, peer_idx = _peer(pos, r)
            pltpu.make_async_remote_copy(src_ref=job.src_view(ins, peer_idx), dst_ref=job.dst_view(outs, me),
                                         send_sem=send.at[j, r - 1], recv_sem=recv.at[j, r - 1], device_id=peer,
                                         device_id_type=pl.DeviceIdType.MESH).start()


def _exchange_wait(jobs, ins, outs, send, recv, loc):
    pos = _mesh_pos()
    me = 4 * pos[0] + 2 * pos[1] + pos[2]
    for j, job in enumerate(jobs):
        for r in range(1, N_DEV):
            peer, peer_idx = _peer(pos, r)
            cp = pltpu.make_async_remote_copy(src_ref=job.src_view(ins, peer_idx), dst_ref=job.dst_view(outs, peer_idx),
                                              send_sem=send.at[j, r - 1], recv_sem=recv.at[j, r - 1], device_id=peer,
                                              device_id_type=pl.DeviceIdType.MESH)
            cp.wait_recv()
            cp.wait_send()
        pltpu.make_async_copy(job.src_view(ins, me), job.dst_view(outs, me), loc.at[j]).wait()


def _exchange_sems(n_jobs):
    return [pltpu.SemaphoreType.DMA((n_jobs, N_DEV - 1)), pltpu.SemaphoreType.DMA((n_jobs, N_DEV - 1)),
            pltpu.SemaphoreType.DMA((n_jobs,))]


def _exchange(jobs, arrs, out_shape, name):
    n_in, n_out = len(arrs), len(out_shape)
    any_spec = pl.BlockSpec(memory_space=pl.ANY)

    def body(*refs):
        ins, outs, sems = refs[:n_in], refs[n_in:n_in + n_out], refs[n_in + n_out:]
        _exchange_start(jobs, ins, outs, *sems)
        _exchange_wait(jobs, ins, outs, *sems)

    return pl.pallas_call(
        body, out_shape=out_shape, in_specs=[any_spec] * n_in, out_specs=[any_spec] * n_out,
        scratch_shapes=_exchange_sems(len(jobs)),
        name=name, compiler_params=pltpu.CompilerParams(has_side_effects=True))(*arrs)


def _all_reduce_small(p8, q, name):
    ni, _, w = p8.shape
    nq = q.shape[0]
    rows = ni + nq

    def body(p_ref, q_ref, o_ref, buf, send, recv):
        pos = _mesh_pos()
        me = 4 * pos[0] + 2 * pos[1] + pos[2]
        buf[me] = jnp.concatenate([p_ref[...].sum(axis=1), q_ref[...]], axis=0)
        for r in range(1, N_DEV):
            peer, _ = _peer(pos, r)
            pltpu.make_async_remote_copy(src_ref=buf.at[me], dst_ref=buf.at[me], send_sem=send.at[r - 1],
                                         recv_sem=recv.at[r - 1], device_id=peer,
                                         device_id_type=pl.DeviceIdType.MESH).start()
        for r in range(1, N_DEV):
            peer, peer_idx = _peer(pos, r)
            cp = pltpu.make_async_remote_copy(src_ref=buf.at[me], dst_ref=buf.at[peer_idx], send_sem=send.at[r - 1],
                                              recv_sem=recv.at[r - 1], device_id=peer,
                                              device_id_type=pl.DeviceIdType.MESH)
            cp.wait_recv()
            cp.wait_send()
        acc = buf[0]
        for dev in range(1, N_DEV):
            acc = acc + buf[dev]
        o_ref[...] = acc

    vmem = pl.BlockSpec(memory_space=pltpu.VMEM)
    return pl.pallas_call(
        body, out_shape=SDS((rows, w), F32), in_specs=[vmem, vmem], out_specs=vmem,
        scratch_shapes=[pltpu.VMEM((N_DEV, rows, w), F32), pltpu.SemaphoreType.DMA((N_DEV - 1,)),
                        pltpu.SemaphoreType.DMA((N_DEV - 1,))],
        name=name, compiler_params=pltpu.CompilerParams(has_side_effects=True))(p8, q)


def _adamw_math(w, g, m, v):
    m = ADAM_B1 * m + (1.0 - ADAM_B1) * g
    v = ADAM_B2 * v + (1.0 - ADAM_B2) * (g * g)
    m_hat = m / (1.0 - ADAM_B1 ** ADAM_STEP)
    v_hat = v / (1.0 - ADAM_B2 ** ADAM_STEP)
    delta = -ADAM_LR * (m_hat / (jnp.sqrt(v_hat) + ADAM_EPS) + ADAM_WD * w)
    return delta, m, v


def _adamw_shard(parts, w, m, v, name):
    depth, _, rr, cc = parts.shape
    tr = _tile(rr, 256, 8)

    def body(p_ref, w_ref, m_ref, v_ref, g_out, d_out, m_out, v_out):
        g = p_ref[0].astype(F32)
        for dev in range(1, N_DEV):
            g = g + p_ref[dev].astype(F32)
        delta, mm, vv = _adamw_math(w_ref[...], g, m_ref[...], v_ref[...])
        g_out[...] = g
        d_out[...] = delta
        m_out[...] = mm
        v_out[...] = vv

    blk = pl.BlockSpec((None, tr, cc), lambda l, i: (l, i, 0))
    return pl.pallas_call(
        body, out_shape=[SDS((depth, rr, cc), F32)] * 4, grid=(depth, rr // tr),
        in_specs=[pl.BlockSpec((None, N_DEV, tr, cc), lambda l, i: (l, 0, i, 0)), blk, blk, blk],
        out_specs=[blk] * 4, name=name, compiler_params=_cp())(parts, w, m, v)


def _adamw_flat(g, w, m, v, name):
    def body(g_ref, w_ref, m_ref, v_ref, d_out, m_out, v_out):
        delta, mm, vv = _adamw_math(w_ref[...], g_ref[...], m_ref[...], v_ref[...])
        d_out[...] = delta
        m_out[...] = mm
        v_out[...] = vv

    return pl.pallas_call(body, out_shape=[SDS(g.shape, F32)] * 3, name=name, compiler_params=_cp())(g, w, m, v)


def _from_cols(t):
    return jnp.transpose(t, (1, 0, 2)).reshape(t.shape[1], N_DEV * t.shape[2])


def _swap(t):
    return jnp.swapaxes(t, -1, -2)


def kernel(x, meta_tokens, mix_norm_g, w_in, conv_dw_w, conv_dw_b, conv_ln_g, conv_ln_b, w_out, ffn_norm_g, w_gate, w_up, w_down, final_norm_g, loss_target, m_meta_tokens, m_mix_norm_g, m_w_in, m_conv_dw_w, m_conv_dw_b, m_conv_ln_g, m_conv_ln_b, m_w_out, m_ffn_norm_g, m_w_gate, m_w_up, m_w_down, m_final_norm_g, v_meta_tokens, v_mix_norm_g, v_w_in, v_conv_dw_w, v_conv_dw_b, v_conv_ln_g, v_conv_ln_b, v_w_out, v_ffn_norm_g, v_w_gate, v_w_up, v_w_down, v_final_norm_g):
    depth, d, in_shard = w_in.shape
    seq = x.shape[1]
    sb = N_HEADS * HEAD_DIM
    cc = conv_dw_w.shape[2] * N_DEV
    ff = w_gate.shape[2] * N_DEV
    assert in_shard * N_DEV == 3 * sb + 2 * cc and x.shape[0] == 1
    lr = N_META + seq
    lp = -(-lr // ABLK) * ABLK
    me = 4 * lax.axis_index("x") + 2 * lax.axis_index("y") + lax.axis_index("c")

    big_names = ("w_in", "w_out", "w_gate", "w_up", "w_down")
    transposed = {"w_in": True, "w_out": False, "w_gate": True, "w_up": True, "w_down": False}
    shard = dict(w_in=_swap(w_in).astype(BF16), w_out=w_out.astype(BF16), w_gate=_swap(w_gate).astype(BF16),
                 w_up=_swap(w_up).astype(BF16), w_down=w_down.astype(BF16))
    srcs = [shard[n] for n in big_names] + [meta_tokens, conv_dw_w]
    jobs, out_shape = [], []
    for k, n in enumerate(big_names):
        for i in range(depth):
            jobs.append(_Job(k, len(out_shape), False, src_layer=i))
            out_shape.append(SDS((N_DEV,) + shard[n].shape[1:], BF16))
    for k in (len(big_names), len(big_names) + 1):
        jobs.append(_Job(k, len(out_shape), False))
        out_shape.append(SDS((N_DEV,) + srcs[k].shape, F32))
    gathered = _exchange(jobs, srcs, out_shape, "gather_weights")
    wl = [{n: gathered[k * depth + i].reshape(-1, d) for k, n in enumerate(big_names)} for i in range(depth)]
    meta_full = _from_cols(gathered[-2])
    taps = jnp.transpose(gathered[-1], (1, 2, 0, 3)).reshape(depth, CONV_K, cc)
    taps = jnp.pad(taps, ((0, 0), (0, CONV_PAD - CONV_K), (0, 0)))
    tri_fwd, tri_after, tri_before = _tri_consts()

    h = jnp.concatenate([meta_full, x[0], jnp.zeros((lp - lr, d), F32)], axis=0)
    saved = []
    for i in range(depth):
        p = wl[i]
        sv = dict(h_in=h)
        qkv, hn = _rms_mm(h, mix_norm_g[i:i + 1], p["w_in"], 0, 3 * sb, BF16, f"proj_qkv_{i}", True)
        cacg = _rms_mm(h, mix_norm_g[i:i + 1], p["w_in"], 3 * sb, 2 * cc, F32, f"proj_conv_{i}", False)[0]
        attn, rsave = _attn_fwd(qkv, tri_fwd, f"attn_fwd_{i}")
        yc = _conv_fwd_dw(cacg, taps[i], conv_dw_b[i:i + 1], f"conv_fwd_dw_{i}")
        conv = _conv_fwd_ln(yc, conv_ln_g[i:i + 1], conv_ln_b[i:i + 1], f"conv_fwd_ln_{i}")
        h = _mix_out(attn, conv, p["w_out"], h, f"mix_out_{i}")
        sv.update(qkv=qkv, hn=hn, cacg=cacg, rsave=rsave, yc=yc, attn=attn, conv=conv, h_mid=h)
        h, hn2, act = _ffn_fwd(h, ffn_norm_g[i:i + 1], p["w_gate"], p["w_up"], p["w_down"], f"ffn_fwd_{i}")
        sv.update(hn2=hn2, act=act)
        saved.append(sv)

    tpad = jnp.pad(loss_target[0], ((N_META, lp - lr), (0, 0)))
    dh, dg_final, loss_part = _loss_head(h, tpad, final_norm_g.reshape(1, d), seq, "loss_head")
    loss = lax.psum(loss_part[0, 0], MESH_AXES)

    grads = [None] * depth
    for i in reversed(range(depth)):
        p, sv = wl[i], saved[i]
        d_g, d_u = _ffn_bwd_act(dh, sv["hn2"], p["w_gate"], p["w_up"], p["w_down"], f"ffn_bwd_act_{i}")
        gw_down = _mm_tn(sv["act"], dh, f"grad_w_down_{i}")
        gw_gate = _mm_tn(d_g, sv["hn2"], f"grad_w_gate_{i}")
        gw_up = _mm_tn(d_u, sv["hn2"], f"grad_w_up_{i}")
        dh, dg_ffn = _ffn_bwd_in(d_g, d_u, p["w_gate"], p["w_up"], sv["h_mid"], ffn_norm_g[i:i + 1], dh, f"ffn_bwd_in_{i}")
        gw_out = jnp.concatenate([_mm_tn(sv["attn"], dh, f"grad_w_out_attn_{i}"),
                                  _mm_tn(sv["conv"], dh, f"grad_w_out_conv_{i}")], axis=0)
        d_attn = _mm_nt(dh, p["w_out"], 0, sb, BF16, f"mix_bwd_attn_{i}")
        d_conv = _mm_nt(dh, p["w_out"], sb, cc, F32, f"mix_bwd_conv_{i}")
        dqkv = _attn_bwd(sv["qkv"], d_attn, sv["rsave"], tri_after, tri_before, f"attn_bwd_{i}")
        dyc, dg_ln, db_ln = _conv_bwd_ln(sv["yc"], d_conv, conv_ln_g[i:i + 1], conv_ln_b[i:i + 1], f"conv_bwd_ln_{i}")
        dcc, g_taps_i, db_conv = _conv_bwd_dw(dyc, sv["cacg"], taps[i], f"conv_bwd_dw_{i}")
        gw_in = _grad_w_in_t(dqkv, dcc, sv["hn"], f"grad_w_in_{i}")
        dh, dg_mix = _mix_bwd_in(dqkv, dcc, p["w_in"], sv["h_in"], mix_norm_g[i:i + 1], dh, f"mix_bwd_in_{i}")
        grads[i] = dict(w_in=gw_in, w_out=gw_out, w_gate=gw_gate, w_up=gw_up, w_down=gw_down, taps=g_taps_i,
                        dg_mix=dg_mix, dg_ffn=dg_ffn, dg_ln=dg_ln, db_ln=db_ln, db_conv=db_conv)
    grad_x = dh[N_META:lr][None]

    wide = lambda key: jnp.concatenate([jnp.pad(grads[i][key], ((0, 0), (0, d // depth - cc))) for i in range(depth)], axis=1)
    assert depth * cc <= d and d % depth == 0
    p8 = jnp.stack([grads[i]["dg_mix"] for i in range(depth)] + [grads[i]["dg_ffn"] for i in range(depth)]
                   + [dg_final, wide("db_conv"), wide("dg_ln"), wide("db_ln")])
    small = _all_reduce_small(p8, jnp.concatenate([wide("taps"), dh[:N_META]], axis=0), "reduce_small")
    r0 = 2 * depth + 4
    g_mix = small[0:depth]
    g_ffn = small[depth:2 * depth]
    g_final = small[2 * depth]
    narrow = lambda row: jnp.stack([row[i * (d // depth):i * (d // depth) + cc] for i in range(depth)])
    g_cb, g_lg, g_lb = narrow(small[2 * depth + 1]), narrow(small[2 * depth + 2]), narrow(small[2 * depth + 3])
    g_taps_full = jnp.stack([small[r0:r0 + CONV_K, i * (d // depth):i * (d // depth) + cc] for i in range(depth)])
    csh = cc // N_DEV
    g_taps_own = lax.dynamic_slice_in_dim(g_taps_full, me * csh, csh, axis=2)
    g_meta_full = small[r0 + CONV_PAD:r0 + CONV_PAD + N_META]
    msh = d // N_DEV
    g_meta_own = lax.dynamic_slice_in_dim(g_meta_full, me * msh, msh, axis=1)

    small_g = [g_meta_own, g_mix, g_taps_own, g_cb, g_lg, g_lb, g_ffn, g_final]
    small_w = [meta_tokens, mix_norm_g, conv_dw_w, conv_dw_b, conv_ln_g, conv_ln_b, ffn_norm_g, final_norm_g]
    small_m = [m_meta_tokens, m_mix_norm_g, m_conv_dw_w, m_conv_dw_b, m_conv_ln_g, m_conv_ln_b, m_ffn_norm_g, m_final_norm_g]
    small_v = [v_meta_tokens, v_mix_norm_g, v_conv_dw_w, v_conv_dw_b, v_conv_ln_g, v_conv_ln_b, v_ffn_norm_g, v_final_norm_g]
    sizes = [int(math.prod(t.shape)) for t in small_w]
    total = sum(sizes)
    rows = -(-total // (8 * 128)) * 8

    def flat(ts):
        v = jnp.concatenate([t.reshape(-1) for t in ts])
        return jnp.pad(v, (0, rows * 128 - total)).reshape(rows, 128)

    def unflat(t):
        v, out, o = t.reshape(-1), [], 0
        for sz, ref in zip(sizes, small_w):
            out.append(v[o:o + sz].reshape(ref.shape))
            o += sz
        return out

    sd, sm, sv_ = _adamw_flat(flat(small_g), flat(small_w), flat(small_m), flat(small_v), "adamw_small")
    s_delta, s_m, s_v = unflat(sd), unflat(sm), unflat(sv_)

    srcs, jobs, out_shape = [], [], []
    for k, n in enumerate(big_names):
        rows_n = shard[n].shape[1]
        for i in range(depth):
            jobs.append(_Job(len(srcs), k, True, dst_layer=i))
            srcs.append(grads[i][n].reshape(N_DEV, rows_n, d))
        out_shape.append(SDS((depth, N_DEV, rows_n, d), BF16))
    recv = _exchange(jobs, srcs, out_shape, "scatter_grads")
    big = []
    for parts, n, w, m, v in zip(recv, big_names, (w_in, w_out, w_gate, w_up, w_down),
                                 (m_w_in, m_w_out, m_w_gate, m_w_up, m_w_down), (v_w_in, v_w_out, v_w_gate, v_w_up, v_w_down)):
        fix = _swap if transposed[n] else (lambda t: t)
        big.append([fix(t) for t in _adamw_shard(parts, fix(w), fix(m), fix(v), f"adamw_{n}")])
    b_in, b_out, b_gate, b_up, b_down = big

    def ordered(k, smalls):
        s_meta, s_mix, s_taps, s_cb, s_lg, s_lb, s_ffn, s_final = smalls
        return [s_meta, s_mix, b_in[k], s_taps, s_cb, s_lg, s_lb, b_out[k], s_ffn, b_gate[k], b_up[k], b_down[k], s_final]

    return (loss, grad_x, *ordered(0, small_g), *ordered(1, s_delta), *ordered(2, s_m), *ordered(3, s_v))
```

```python
import functools
import math

import jax
import jax.numpy as jnp
from jax import lax
from jax.experimental import pallas as pl
from jax.experimental.pallas import tpu as pltpu

F32 = jnp.float32
BF16 = jnp.bfloat16
SDS = jax.ShapeDtypeStruct

N_META = 16
N_HEADS = 8
HEAD_DIM = 64
CONV_K = 31
CONV_PAD = 32
ABLK = 128
ATT_GROUP = 4
FF_CHUNK = 256
EPS = 1e-6
N_DEV = 8
MESH_AXES = ("x", "y", "c")
ADAM_LR = 0.001
ADAM_B1 = 0.9
ADAM_B2 = 0.999
ADAM_EPS = 1e-08
ADAM_WD = 0.01
ADAM_STEP = 10
MIB = 1 << 20
VMEM_LIMIT_MIB = 48


def _cp():
    return pltpu.CompilerParams(vmem_limit_bytes=VMEM_LIMIT_MIB * MIB)


def _tile(n, cap, mult):
    best = None
    for t in range(mult, min(n, cap) + 1, mult):
        if n % t == 0:
            best = t
    assert best is not None, (n, cap, mult)
    return best


def _sum8(v):
    r, c = v.shape
    return v.reshape(r // 8, 8, c).sum(axis=0)


def _sigmoid(x):
    return 1.0 / (1.0 + jnp.exp(-x))


NT = (((1,), (1,)), ((), ()))
TN = (((0,), (0,)), ((), ()))


def _rms_mm(h, g, wt, w_row0, n_cols, out_dtype, name, hn_out):
    lp, d = h.shape
    tm = _tile(lp, 528, 16)
    tn = _tile(math.gcd(n_cols, w_row0), 512, 128)
    off = w_row0 // tn

    def body(h_ref, g_ref, w_ref, *rest):
        if hn_out:
            o_ref, hn_ref, hn_s = rest
        else:
            o_ref, hn_s = rest

        @pl.when(pl.program_id(1) == 0)
        def _():
            x = h_ref[...]
            r = lax.rsqrt(jnp.mean(x * x, axis=-1, keepdims=True) + EPS)
            hn = ((x * r) * g_ref[...]).astype(BF16)
            hn_s[...] = hn
            if hn_out:
                hn_ref[...] = hn

        o_ref[...] = lax.dot_general(hn_s[...], w_ref[...], NT, preferred_element_type=F32).astype(out_dtype)

    out_shape = [SDS((lp, n_cols), out_dtype)]
    out_specs = [pl.BlockSpec((tm, tn), lambda i, j: (i, j))]
    if hn_out:
        out_shape.append(SDS((lp, d), BF16))
        out_specs.append(pl.BlockSpec((tm, d), lambda i, j: (i, 0)))
    return pl.pallas_call(
        body, out_shape=out_shape, grid=(lp // tm, n_cols // tn),
        in_specs=[pl.BlockSpec((tm, d), lambda i, j: (i, 0)),
                  pl.BlockSpec((1, d), lambda i, j: (0, 0)),
                  pl.BlockSpec((tn, d), lambda i, j: (j + off, 0))],
        out_specs=out_specs, scratch_shapes=[pltpu.VMEM((tm, d), BF16)],
        name=name, compiler_params=_cp())(h, g, wt)


def _mm_nt(a, b, b_row0, n_out, out_dtype, name):
    m, k = a.shape
    tm = _tile(m, 528, 16)
    tn = _tile(math.gcd(n_out, b_row0), 512, 128)
    off = b_row0 // tn

    def body(a_ref, b_ref, o_ref):
        o_ref[...] = lax.dot_general(a_ref[...].astype(BF16), b_ref[...], NT,
                                     preferred_element_type=F32).astype(out_dtype)

    return pl.pallas_call(
        body, out_shape=SDS((m, n_out), out_dtype), grid=(m // tm, n_out // tn),
        in_specs=[pl.BlockSpec((tm, k), lambda i, j: (i, 0)), pl.BlockSpec((tn, k), lambda i, j: (j + off, 0))],
        out_specs=pl.BlockSpec((tm, tn), lambda i, j: (i, j)), name=name, compiler_params=_cp())(a, b)


def _mix_out(attn, conv, w, res, name):
    m, ka = attn.shape
    kc = conv.shape[1]
    n = w.shape[1]
    assert ka == kc
    tm = _tile(m, 528, 16)
    tn = _tile(n, 512, 128)

    def body(a_ref, c_ref, wa_ref, wc_ref, r_ref, o_ref):
        o_ref[...] = (r_ref[...] + jnp.dot(a_ref[...], wa_ref[...], preferred_element_type=F32)
                      + jnp.dot(c_ref[...], wc_ref[...], preferred_element_type=F32))

    return pl.pallas_call(
        body, out_shape=SDS((m, n), F32), grid=(m // tm, n // tn),
        in_specs=[pl.BlockSpec((tm, ka), lambda i, j: (i, 0)), pl.BlockSpec((tm, kc), lambda i, j: (i, 0)),
                  pl.BlockSpec((ka, tn), lambda i, j: (0, j)), pl.BlockSpec((kc, tn), lambda i, j: (1, j)),
                  pl.BlockSpec((tm, tn), lambda i, j: (i, j))],
        out_specs=pl.BlockSpec((tm, tn), lambda i, j: (i, j)), name=name, compiler_params=_cp())(attn, conv, w, w, res)


def _mm_tn(a, b, name):
    l, m = a.shape
    n = b.shape[1]
    tm = _tile(m, 1408, 128)
    tn = _tile(n, 1024, 128)
    tl = _tile(l, 512, 128)
    nl = l // tl

    def body(a_ref, b_ref, o_ref, acc):
        @pl.when(pl.program_id(2) == 0)
        def _():
            acc[...] = jnp.zeros_like(acc)

        acc[...] += lax.dot_general(a_ref[...].astype(BF16), b_ref[...].astype(BF16), TN, preferred_element_type=F32)

        @pl.when(pl.program_id(2) == nl - 1)
        def _():
            o_ref[...] = acc[...].astype(BF16)

    return pl.pallas_call(
        body, out_shape=SDS((m, n), BF16), grid=(m // tm, n // tn, nl),
        in_specs=[pl.BlockSpec((tl, tm), lambda i, j, s: (s, i)),
                  pl.BlockSpec((tl, tn), lambda i, j, s: (s, j))],
        out_specs=pl.BlockSpec((tm, tn), lambda i, j, s: (i, j)),
        scratch_shapes=[pltpu.VMEM((tm, tn), F32)], name=name, compiler_params=_cp())(a, b)


def _grad_w_in_t(dqkv, dcc, hn, name):
    nq, l, w = dqkv.shape
    nc = dcc.shape[0]
    d = hn.shape[1]
    assert dcc.shape[2] == w
    tn = _tile(d, 1024, 128)
    tl = _tile(l, 512, 128)
    nl = l // tl

    def body(q_ref, c_ref, b_ref, o_ref, acc):
        p, s = pl.program_id(0), pl.program_id(2)

        @pl.when(s == 0)
        def _():
            acc[...] = jnp.zeros_like(acc)

        @pl.when(p < nq)
        def _():
            acc[...] += lax.dot_general(q_ref[...], b_ref[...], TN, preferred_element_type=F32)

        @pl.when(p >= nq)
        def _():
            acc[...] += lax.dot_general(c_ref[...], b_ref[...], TN, preferred_element_type=F32)

        @pl.when(s == nl - 1)
        def _():
            o_ref[...] = acc[...].astype(BF16)

    out = pl.pallas_call(
        body, out_shape=SDS((nq + nc, w, d), BF16), grid=(nq + nc, d // tn, nl),
        in_specs=[pl.BlockSpec((None, tl, w), lambda p, j, s: (jnp.minimum(p, nq - 1), s, 0)),
                  pl.BlockSpec((None, tl, w), lambda p, j, s: (jnp.maximum(p - nq, 0), s, 0)),
                  pl.BlockSpec((tl, tn), lambda p, j, s: (s, j))],
        out_specs=pl.BlockSpec((None, w, tn), lambda p, j, s: (p, 0, j)),
        scratch_shapes=[pltpu.VMEM((w, tn), F32)], name=name, compiler_params=_cp())(dqkv, dcc, hn)
    return out.reshape((nq + nc) * w, d)


def _rms_bwd_tail(acc, h_ref, g_ref, dres_ref, o_ref, dg_ref):
    x = h_ref[...]
    r = lax.rsqrt(jnp.mean(x * x, axis=-1, keepdims=True) + EPS)
    xr = x * r
    dyv = acc[...]
    gy = dyv * g_ref[...]
    o_ref[...] = dres_ref[...] + r * (gy - xr * jnp.mean(gy * xr, axis=-1, keepdims=True))
    dg_ref[...] += _sum8(dyv * xr)


def _mix_bwd_in(dqkv, dcc, wt, h, g, dres, name):
    nq, lp, w = dqkv.shape
    nc = dcc.shape[0]
    d = h.shape[1]
    tm = _tile(lp, 528, 16)
    npc = nq + nc

    def body(q_ref, c_ref, w_ref, h_ref, g_ref, dres_ref, o_ref, dg_ref, acc):
        i, p = pl.program_id(0), pl.program_id(1)

        @pl.when(p == 0)
        def _():
            acc[...] = jnp.zeros_like(acc)

        @pl.when((p == 0) & (i == 0))
        def _():
            dg_ref[...] = jnp.zeros_like(dg_ref)

        @pl.when(p < nq)
        def _():
            acc[...] += jnp.dot(q_ref[...], w_ref[...], preferred_element_type=F32)

        @pl.when(p >= nq)
        def _():
            acc[...] += jnp.dot(c_ref[...], w_ref[...], preferred_element_type=F32)

        @pl.when(p == npc - 1)
        def _():
            _rms_bwd_tail(acc, h_ref, g_ref, dres_ref, o_ref, dg_ref)

    return pl.pallas_call(
        body, out_shape=[SDS((lp, d), F32), SDS((8, d), F32)], grid=(lp // tm, npc),
        in_specs=[pl.BlockSpec((None, tm, w), lambda i, p: (jnp.minimum(p, nq - 1), i, 0)),
                  pl.BlockSpec((None, tm, w), lambda i, p: (jnp.maximum(p - nq, 0), i, 0)),
                  pl.BlockSpec((w, d), lambda i, p: (p, 0)),
                  pl.BlockSpec((tm, d), lambda i, p: (i, 0)),
                  pl.BlockSpec((1, d), lambda i, p: (0, 0)),
                  pl.BlockSpec((tm, d), lambda i, p: (i, 0))],
        out_specs=[pl.BlockSpec((tm, d), lambda i, p: (i, 0)), pl.BlockSpec((8, d), lambda i, p: (0, 0))],
        scratch_shapes=[pltpu.VMEM((tm, d), F32)], name=name, compiler_params=_cp())(dqkv, dcc, wt, h, g, dres)


def _ffn_bwd_in(d_g, d_u, wg_t, wu_t, h, g, dres, name):
    lp, f = d_g.shape
    d = h.shape[1]
    tm = _tile(lp, 528, 16)
    tk = _tile(f, 512, 128)
    nk = f // tk

    def body(dg_in, du_in, wg_ref, wu_ref, h_ref, g_ref, dres_ref, o_ref, dg_ref, acc):
        i, kk = pl.program_id(0), pl.program_id(1)

        @pl.when(kk == 0)
        def _():
            acc[...] = jnp.zeros_like(acc)

        @pl.when((kk == 0) & (i == 0))
        def _():
            dg_ref[...] = jnp.zeros_like(dg_ref)

        acc[...] += (jnp.dot(dg_in[...], wg_ref[...], preferred_element_type=F32)
                     + jnp.dot(du_in[...], wu_ref[...], preferred_element_type=F32))

        @pl.when(kk == nk - 1)
        def _():
            _rms_bwd_tail(acc, h_ref, g_ref, dres_ref, o_ref, dg_ref)

    return pl.pallas_call(
        body, out_shape=[SDS((lp, d), F32), SDS((8, d), F32)], grid=(lp // tm, nk),
        in_specs=[pl.BlockSpec((tm, tk), lambda i, j: (i, j)), pl.BlockSpec((tm, tk), lambda i, j: (i, j)),
                  pl.BlockSpec((tk, d), lambda i, j: (j, 0)), pl.BlockSpec((tk, d), lambda i, j: (j, 0)),
                  pl.BlockSpec((tm, d), lambda i, j: (i, 0)),
                  pl.BlockSpec((1, d), lambda i, j: (0, 0)),
                  pl.BlockSpec((tm, d), lambda i, j: (i, 0))],
        out_specs=[pl.BlockSpec((tm, d), lambda i, j: (i, 0)), pl.BlockSpec((8, d), lambda i, j: (0, 0))],
        scratch_shapes=[pltpu.VMEM((tm, d), F32)], name=name, compiler_params=_cp())(d_g, d_u, wg_t, wu_t, h, g, dres)


def _ffn_fwd(h, g, wg_t, wu_t, w_d, name):
    lp, d = h.shape
    f = w_d.shape[0]
    tm = _tile(lp, 528, 16)
    nc = f // FF_CHUNK

    def body(h_ref, g_ref, wg_ref, wu_ref, wd_ref, o_ref, hn_ref, act_ref, hn_s, acc):
        c = pl.program_id(1)

        @pl.when(c == 0)
        def _():
            x = h_ref[...]
            r = lax.rsqrt(jnp.mean(x * x, axis=-1, keepdims=True) + EPS)
            hn = ((x * r) * g_ref[...]).astype(BF16)
            hn_s[...] = hn
            hn_ref[...] = hn
            acc[...] = jnp.zeros_like(acc)

        gg = lax.dot_general(hn_s[...], wg_ref[...], NT, preferred_element_type=F32)
        uu = lax.dot_general(hn_s[...], wu_ref[...], NT, preferred_element_type=F32)
        act = ((gg * _sigmoid(gg)) * uu).astype(BF16)
        act_ref[...] = act
        acc[...] += jnp.dot(act, wd_ref[...], preferred_element_type=F32)

        @pl.when(c == nc - 1)
        def _():
            o_ref[...] = h_ref[...] + acc[...]

    chunk = pl.BlockSpec((FF_CHUNK, d), lambda i, j: (j, 0))
    return pl.pallas_call(
        body, out_shape=[SDS((lp, d), F32), SDS((lp, d), BF16), SDS((lp, f), BF16)], grid=(lp // tm, nc),
        in_specs=[pl.BlockSpec((tm, d), lambda i, j: (i, 0)), pl.BlockSpec((1, d), lambda i, j: (0, 0)),
                  chunk, chunk, chunk],
        out_specs=[pl.BlockSpec((tm, d), lambda i, j: (i, 0)),
                   pl.BlockSpec((tm, d), lambda i, j: (i, 0)),
                   pl.BlockSpec((tm, FF_CHUNK), lambda i, j: (i, j))],
        scratch_shapes=[pltpu.VMEM((tm, d), BF16), pltpu.VMEM((tm, d), F32)],
        name=name, compiler_params=_cp())(h, g, wg_t, wu_t, w_d)


def _ffn_bwd_act(dh, hn, wg_t, wu_t, w_d, name):
    lp, d = dh.shape
    f = w_d.shape[0]
    tm = _tile(lp, 528, 16)
    nc = f // FF_CHUNK

    def body(dh_ref, hn_ref, wg_ref, wu_ref, wd_ref, dg_out, du_out, dh_s):
        @pl.when(pl.program_id(1) == 0)
        def _():
            dh_s[...] = dh_ref[...].astype(BF16)

        gg = lax.dot_general(hn_ref[...], wg_ref[...], NT, preferred_element_type=F32)
        uu = lax.dot_general(hn_ref[...], wu_ref[...], NT, preferred_element_type=F32)
        dact = lax.dot_general(dh_s[...], wd_ref[...], NT, preferred_element_type=F32)
        s = _sigmoid(gg)
        dg_out[...] = (dact * uu * (s * (1.0 + gg * (1.0 - s)))).astype(BF16)
        du_out[...] = (dact * (gg * s)).astype(BF16)

    chunk = pl.BlockSpec((FF_CHUNK, d), lambda i, j: (j, 0))
    out = pl.BlockSpec((tm, FF_CHUNK), lambda i, j: (i, j))
    return pl.pallas_call(
        body, out_shape=[SDS((lp, f), BF16), SDS((lp, f), BF16)], grid=(lp // tm, nc),
        in_specs=[pl.BlockSpec((tm, d), lambda i, j: (i, 0)), pl.BlockSpec((tm, d), lambda i, j: (i, 0)),
                  chunk, chunk, chunk],
        out_specs=[out, out], scratch_shapes=[pltpu.VMEM((tm, d), BF16)],
        name=name, compiler_params=_cp())(dh, hn, wg_t, wu_t, w_d)


def _loss_head(h, tpad, g, n_real, name):
    lp, d = h.shape
    tm = _tile(lp, 528, 16)

    def body(h_ref, t_ref, g_ref, dh_ref, dg_ref, loss_ref):
        i = pl.program_id(0)

        @pl.when(i == 0)
        def _():
            dg_ref[...] = jnp.zeros_like(dg_ref)
            loss_ref[...] = jnp.zeros_like(loss_ref)

        x = h_ref[...]
        r = lax.rsqrt(jnp.mean(x * x, axis=-1, keepdims=True) + EPS)
        xr = x * r
        y = xr * g_ref[...]
        row = i * tm + lax.broadcasted_iota(jnp.int32, (tm, d), 0)
        valid = (row >= N_META) & (row < N_META + n_real)
        diff = jnp.where(valid, y - t_ref[...], 0.0)
        loss_ref[...] += jnp.sum(diff * diff) * (0.5 / d)
        dy = diff * (1.0 / d)
        gy = dy * g_ref[...]
        dh_ref[...] = r * (gy - xr * jnp.mean(gy * xr, axis=-1, keepdims=True))
        dg_ref[...] += _sum8(dy * xr)

    return pl.pallas_call(
        body, out_shape=[SDS((lp, d), F32), SDS((8, d), F32), SDS((8, 128), F32)], grid=(lp // tm,),
        in_specs=[pl.BlockSpec((tm, d), lambda i: (i, 0)),
                  pl.BlockSpec((tm, d), lambda i: (i, 0)),
                  pl.BlockSpec((1, d), lambda i: (0, 0))],
        out_specs=[pl.BlockSpec((tm, d), lambda i: (i, 0)),
                   pl.BlockSpec((8, d), lambda i: (0, 0)),
                   pl.BlockSpec((8, 128), lambda i: (0, 0))],
        name=name, compiler_params=_cp())(h, tpad, g)


def _tri_consts():
    j = lax.broadcasted_iota(jnp.int32, (ABLK, ABLK), 0)
    s = lax.broadcasted_iota(jnp.int32, (ABLK, ABLK), 1)
    after = (j >= s).astype(BF16)
    before = (j < s).astype(BF16)
    ones = jnp.ones((ABLK, ABLK), BF16)
    two = lambda t: jnp.concatenate([t, t], axis=0)
    return (two(jnp.concatenate([after, ones], axis=1)),
            two(after),
            two(jnp.concatenate([before, ones], axis=1)))


def _split_hi_lo(m):
    hi = m.astype(BF16)
    lo = (m - hi.astype(F32)).astype(BF16)
    return jnp.concatenate([hi, lo], axis=1)


def _head_halves(t2, in_a):
    zero = jnp.zeros_like(t2)
    return jnp.concatenate([jnp.where(in_a, t2, zero), jnp.where(in_a, zero, t2)], axis=0)


def _stack_blocks(t, nblk, in_a):
    return jnp.concatenate([_head_halves(t[u * ABLK:(u + 1) * ABLK], in_a) for u in range(nblk)], axis=0)


def _attn_scale():
    scale = 1.0 / math.sqrt(HEAD_DIM)
    assert math.frexp(scale)[0] == 0.5, "a power of two, so that scaling q in bf16 is exact"
    return scale


def _pow2_below(n):
    assert n & (n - 1) == 0
    return [p for p in (64, 32, 16, 8, 4, 2, 1) if p < n]


class _Rider:
    def __init__(self, jobs=(), srcs=(), out_shape=()):
        self.jobs, self.srcs, self.out_shape = list(jobs), list(srcs), list(out_shape)
        self.any = [pl.BlockSpec(memory_space=pl.ANY)]

    def split(self, rest, n_out, n_scratch):
        ni, no = len(self.srcs), len(self.out_shape)
        self.ins, outs = rest[:ni], rest[ni:ni + n_out]
        self.outs = rest[ni + n_out:ni + n_out + no]
        scratch = rest[ni + n_out + no:ni + n_out + no + n_scratch]
        self.sems = rest[ni + n_out + no + n_scratch:]
        return outs, scratch

    def start(self, first):
        if self.jobs:
            @pl.when(first)
            def _():
                _exchange_start(self.jobs, self.ins, self.outs, *self.sems)

    def wait(self, last):
        if self.jobs:
            @pl.when(last)
            def _():
                _exchange_wait(self.jobs, self.ins, self.outs, *self.sems)

    def in_specs(self):
        return self.any * len(self.srcs)

    def out_specs(self):
        return self.any * len(self.out_shape)

    def scratch(self):
        return _exchange_sems(len(self.jobs)) if self.jobs else []


def _attn_fwd(qkv, tri_fwd, name, rider=None):
    lp = qkv.shape[0]
    n_pairs = (N_HEADS * HEAD_DIM) // 128
    nb = lp // ABLK
    assert nb <= 128 and 2 * HEAD_DIM == 128
    scale = _attn_scale()
    nt = (((1,), (1,)), ((), ()))
    rider = rider or _Rider()

    def body(q_ref, k_ref, v_ref, tri_ref, *rest):
        (o_ref, rs_ref), (r_s, acc_s, rs_s) = rider.split(rest, 2, 3)
        rider.start(pl.program_id(0) == 0)
        lane = lax.broadcasted_iota(jnp.int32, (ABLK, 128), 1)
        row = lax.broadcasted_iota(jnp.int32, (ABLK, 128), 0)
        in_a = lane < HEAD_DIM
        causal = lane < row

        def qblock(i, carry):
            q0 = pl.multiple_of(i * ABLK, ABLK)
            q2 = q_ref[pl.ds(q0, ABLK), :] * scale
            r_s[...] = jnp.zeros_like(r_s)
            acc_s[...] = jnp.zeros_like(acc_s)
            rs_s[...] = jnp.zeros_like(rs_s)

            def step(kb0, nblk, diag):
                k0 = pl.multiple_of(kb0 * ABLK, ABLK)
                kbd = _stack_blocks(k_ref[pl.ds(k0, nblk * ABLK), :], nblk, in_a)
                vbd = _stack_blocks(v_ref[pl.ds(k0, nblk * ABLK), :], nblk, in_a)
                z = lax.dot_general(q2, kbd, nt, preferred_element_type=F32)
                zt = [z[:, c * 128:(c + 1) * 128] for c in range(2 * nblk)]
                parts = []
                for c in range(2 * nblk):
                    m = jnp.minimum(-zt[c], 0.0) - jnp.log(1.0 + jnp.exp(-jnp.abs(zt[c])))
                    if diag:
                        m = jnp.where(causal, m, 0.0)
                    parts.append(_split_hi_lo(m))
                ce = jnp.dot(jnp.concatenate(parts, axis=0), tri_ref[...], preferred_element_type=F32)
                rr = [r_s[0], r_s[1]]
                rsv = [rs_s[:, :128], rs_s[:, 128:]]
                ws = [None] * (2 * nblk)
                for u in reversed(range(nblk)):
                    for hh in range(2):
                        c = 2 * u + hh
                        cec = ce[c * 128:(c + 1) * 128]
                        w = jnp.exp(zt[c] + cec[:, :128] + rr[hh])
                        if diag:
                            w = jnp.where(causal, w, 0.0)
                        ws[c] = w.astype(BF16)
                        rsv[hh] = jnp.where(lane == kb0 + u, rr[hh], rsv[hh])
                        rr[hh] = rr[hh] + cec[:, 128:]
                acc_s[...] += jnp.dot(jnp.concatenate(ws, axis=1), vbd, preferred_element_type=F32)
                r_s[0] = rr[0]
                r_s[1] = rr[1]
                rs_s[:, :128] = rsv[0]
                rs_s[:, 128:] = rsv[1]

            step(i, 1, True)
            n_grp = i // ATT_GROUP

            def inner(jj, c2):
                step(i - ATT_GROUP * (jj + 1), ATT_GROUP, False)
                return c2

            lax.fori_loop(0, n_grp, inner, 0)
            rem = i - ATT_GROUP * n_grp
            for p in _pow2_below(ATT_GROUP):
                @pl.when((rem & p) != 0)
                def _():
                    step(rem & (p - 1), p, False)

            o_ref[pl.ds(q0, ABLK), :] = acc_s[...].astype(BF16)
            rs_ref[pl.ds(q0, ABLK), :] = rs_s[...]
            return carry

        lax.fori_loop(0, nb, qblock, 0)
        rider.wait(pl.program_id(0) == n_pairs - 1)

    col = lambda o: (lambda p: (0, p + o))
    return pl.pallas_call(
        body, out_shape=[SDS((lp, n_pairs * 128), BF16), SDS((lp, n_pairs * 256), F32)] + rider.out_shape,
        grid=(n_pairs,),
        in_specs=[pl.BlockSpec((lp, 128), col(0)), pl.BlockSpec((lp, 128), col(n_pairs)),
                  pl.BlockSpec((lp, 128), col(2 * n_pairs)), pl.BlockSpec((256, 256), lambda p: (0, 0))]
        + rider.in_specs(),
        out_specs=[pl.BlockSpec((lp, 128), col(0)), pl.BlockSpec((lp, 256), col(0))] + rider.out_specs(),
        scratch_shapes=[pltpu.VMEM((2, ABLK, 128), F32), pltpu.VMEM((ABLK, 128), F32), pltpu.VMEM((ABLK, 256), F32)]
        + rider.scratch(),
        name=name, compiler_params=_cp())(qkv, qkv, qkv, tri_fwd, *rider.srcs)


def _attn_bwd(qkv, d_out, rsave, tri_after, tri_before, name, rider=None):
    lp = qkv.shape[0]
    n_pairs = (N_HEADS * HEAD_DIM) // 128
    nb = lp // ABLK
    scale = _attn_scale()
    nt = (((1,), (1,)), ((), ()))
    tn = (((0,), (0,)), ((), ()))
    rider = rider or _Rider()

    def body(q_ref, k_ref, v_ref, do_ref, rs_ref, ta_ref, tb_ref, *rest):
        (o_ref,), (dk_s, dv_s, dq_s, pc_s) = rider.split(rest, 1, 4)
        rider.start(pl.program_id(0) == 0)
        lane = lax.broadcasted_iota(jnp.int32, (ABLK, 128), 1)
        row = lax.broadcasted_iota(jnp.int32, (ABLK, 128), 0)
        in_a = lane < HEAD_DIM
        causal = lane < row
        dk_s[...] = jnp.zeros_like(dk_s)
        dv_s[...] = jnp.zeros_like(dv_s)

        def qblock(i, carry):
            q0 = pl.multiple_of(i * ABLK, ABLK)
            q2 = q_ref[pl.ds(q0, ABLK), :] * scale
            do2 = do_ref[pl.ds(q0, ABLK), :]
            q_st = _head_halves(q2, in_a)
            do_st = _head_halves(do2, in_a)
            dq_s[...] = jnp.zeros_like(dq_s)
            pc_s[...] = jnp.zeros_like(pc_s)

            def step(kb0, nblk, diag):
                k0 = pl.multiple_of(kb0 * ABLK, ABLK)
                kbd = _stack_blocks(k_ref[pl.ds(k0, nblk * ABLK), :], nblk, in_a)
                vbd = _stack_blocks(v_ref[pl.ds(k0, nblk * ABLK), :], nblk, in_a)
                z = lax.dot_general(q2, kbd, nt, preferred_element_type=F32)
                dw = lax.dot_general(do2, vbd, nt, preferred_element_type=F32)
                ncol = 2 * nblk
                zt = [z[:, c * 128:(c + 1) * 128] for c in range(ncol)]
                parts, betas = [], []
                for c in range(ncol):
                    e = jnp.exp(-jnp.abs(zt[c]))
                    m = jnp.minimum(-zt[c], 0.0) - jnp.log(1.0 + e)
                    if diag:
                        m = jnp.where(causal, m, 0.0)
                    betas.append(jnp.where(zt[c] >= 0.0, 1.0, e) / (1.0 + e))
                    parts.append(_split_hi_lo(m))
                ex = jnp.dot(jnp.concatenate(parts, axis=0), ta_ref[...], preferred_element_type=F32)
                ws, dls = [], []
                for c in range(ncol):
                    u, hh = c // 2, c % 2
                    r_saved = jnp.sum(jnp.where(lane == kb0 + u, rs_ref[pl.ds(q0, ABLK), hh * 128:(hh + 1) * 128], 0.0),
                                      axis=1, keepdims=True)
                    w = jnp.exp(zt[c] + ex[c * 128:(c + 1) * 128] + r_saved)
                    if diag:
                        w = jnp.where(causal, w, 0.0)
                    ws.append(w.astype(BF16))
                    dls.append(dw[:, c * 128:(c + 1) * 128] * w)
                pe = jnp.dot(jnp.concatenate([_split_hi_lo(dl) for dl in dls], axis=0), tb_ref[...],
                             preferred_element_type=F32)
                pc = [pc_s[0], pc_s[1]]
                dzs = []
                for c in range(ncol):
                    hh = c % 2
                    pec = pe[c * 128:(c + 1) * 128]
                    dz = dls[c] * (1.0 - betas[c]) - (pec[:, :128] + pc[hh]) * betas[c]
                    pc[hh] = pc[hh] + pec[:, 128:]
                    if diag:
                        dz = jnp.where(causal, dz, 0.0)
                    dzs.append(dz.astype(BF16))
                pc_s[0] = pc[0]
                pc_s[1] = pc[1]
                dq_s[...] += jnp.dot(jnp.concatenate(dzs, axis=1), kbd, preferred_element_type=F32)
                by_head = lambda ts: jnp.concatenate([jnp.concatenate(ts[0::2], axis=1), jnp.concatenate(ts[1::2], axis=1)],
                                                     axis=0)
                rows = pl.ds(k0, nblk * ABLK)
                dk_s[rows, :] += lax.dot_general(by_head(dzs), q_st, tn, preferred_element_type=F32)
                dv_s[rows, :] += lax.dot_general(by_head(ws), do_st, tn, preferred_element_type=F32)

            n_grp = i // ATT_GROUP

            def inner(jj, c2):
                step(ATT_GROUP * jj, ATT_GROUP, False)
                return c2

            lax.fori_loop(0, n_grp, inner, 0)
            rem = i - ATT_GROUP * n_grp
            for p in _pow2_below(ATT_GROUP):
                @pl.when((rem & p) != 0)
                def _():
                    step(i - (rem & (2 * p - 1)), p, False)

            step(i, 1, True)
            o_ref[0, pl.ds(q0, ABLK), :] = (dq_s[...] * scale).astype(BF16)
            return carry

        lax.fori_loop(0, nb, qblock, 0)
        o_ref[1] = dk_s[...].astype(BF16)
        o_ref[2] = dv_s[...].astype(BF16)
        rider.wait(pl.program_id(0) == n_pairs - 1)

    col = lambda o: (lambda p: (0, p + o))
    return pl.pallas_call(
        body, out_shape=[SDS((3, lp, n_pairs * 128), BF16)] + rider.out_shape, grid=(n_pairs,),
        in_specs=[pl.BlockSpec((lp, 128), col(0)), pl.BlockSpec((lp, 128), col(n_pairs)),
                  pl.BlockSpec((lp, 128), col(2 * n_pairs)), pl.BlockSpec((lp, 128), col(0)),
                  pl.BlockSpec((lp, 256), col(0)),
                  pl.BlockSpec((256, 128), lambda p: (0, 0)), pl.BlockSpec((256, 256), lambda p: (0, 0))]
        + rider.in_specs(),
        out_specs=[pl.BlockSpec((3, lp, 128), lambda p: (0, 0, p))] + rider.out_specs(),
        scratch_shapes=[pltpu.VMEM((lp, 128), F32), pltpu.VMEM((lp, 128), F32),
                        pltpu.VMEM((ABLK, 128), F32), pltpu.VMEM((2, ABLK, 128), F32)] + rider.scratch(),
        name=name, compiler_params=_cp())(qkv, qkv, qkv, d_out, rsave, tri_after, tri_before, *rider.srcs)


def _conv_fwd_dw(cacg, w, b, name):
    lp = cacg.shape[0]
    c = cacg.shape[1] // 2
    ncb = c // 128
    nchunk = lp // ABLK
    off = CONV_PAD - (CONV_K - 1)

    def body(a_ref, g_ref, w_ref, b_ref, y_ref, upad):
        upad[0:CONV_PAD, :] = jnp.zeros((CONV_PAD, 128), F32)

        def fill(ch, carry):
            base = pl.multiple_of(ch * ABLK, ABLK)
            upad[pl.ds(base + CONV_PAD, ABLK), :] = a_ref[pl.ds(base, ABLK), :] * _sigmoid(g_ref[pl.ds(base, ABLK), :])
            return carry

        lax.fori_loop(0, nchunk, fill, 0)

        def comp(ch, carry):
            base = pl.multiple_of(ch * ABLK, ABLK)
            acc = jnp.zeros((ABLK, 128), F32)
            for k in range(CONV_K):
                acc = acc + upad[pl.ds(base + (off + k), ABLK), :] * w_ref[k:k + 1, :]
            y_ref[pl.ds(base, ABLK), :] = acc + b_ref[...]
            return carry

        lax.fori_loop(0, nchunk, comp, 0)

    return pl.pallas_call(
        body, out_shape=SDS((lp, c), F32), grid=(ncb,),
        in_specs=[pl.BlockSpec((lp, 128), lambda j: (0, j)), pl.BlockSpec((lp, 128), lambda j: (0, j + ncb)),
                  pl.BlockSpec((CONV_PAD, 128), lambda j: (0, j)), pl.BlockSpec((1, 128), lambda j: (0, j))],
        out_specs=pl.BlockSpec((lp, 128), lambda j: (0, j)),
        scratch_shapes=[pltpu.VMEM((lp + CONV_PAD, 128), F32)], name=name, compiler_params=_cp())(cacg, cacg, w, b)


def _ln_parts(x, g, b):
    mu = jnp.mean(x, axis=-1, keepdims=True)
    xc = x - mu
    rstd = lax.rsqrt(jnp.mean(xc * xc, axis=-1, keepdims=True) + EPS)
    xh = xc * rstd
    return xh, rstd, xh * g + b


def _conv_fwd_ln(yc, g, b, name):
    lp, c = yc.shape
    tm = _tile(lp, 528, 16)

    def body(y_ref, g_ref, b_ref, o_ref):
        _, _, ln = _ln_parts(y_ref[...], g_ref[...], b_ref[...])
        o_ref[...] = (ln * _sigmoid(ln)).astype(BF16)

    return pl.pallas_call(
        body, out_shape=SDS((lp, c), BF16), grid=(lp // tm,),
        in_specs=[pl.BlockSpec((tm, c), lambda i: (i, 0)), pl.BlockSpec((1, c), lambda i: (0, 0)),
                  pl.BlockSpec((1, c), lambda i: (0, 0))],
        out_specs=pl.BlockSpec((tm, c), lambda i: (i, 0)), name=name, compiler_params=_cp())(yc, g, b)


def _conv_bwd_ln(yc, dout, g, b, name):
    lp, c = yc.shape
    tm = _tile(lp, 528, 16)

    def body(y_ref, d_ref, g_ref, b_ref, o_ref, dg_ref, db_ref):
        @pl.when(pl.program_id(0) == 0)
        def _():
            dg_ref[...] = jnp.zeros_like(dg_ref)
            db_ref[...] = jnp.zeros_like(db_ref)

        xh, rstd, ln = _ln_parts(y_ref[...], g_ref[...], b_ref[...])
        s = _sigmoid(ln)
        dln = d_ref[...] * (s * (1.0 + ln * (1.0 - s)))
        dg_ref[...] += _sum8(dln * xh)
        db_ref[...] += _sum8(dln)
        dxh = dln * g_ref[...]
        o_ref[...] = rstd * (dxh - jnp.mean(dxh, axis=-1, keepdims=True)
                             - xh * jnp.mean(dxh * xh, axis=-1, keepdims=True))

    return pl.pallas_call(
        body, out_shape=[SDS((lp, c), F32), SDS((8, c), F32), SDS((8, c), F32)], grid=(lp // tm,),
        in_specs=[pl.BlockSpec((tm, c), lambda i: (i, 0)), pl.BlockSpec((tm, c), lambda i: (i, 0)),
                  pl.BlockSpec((1, c), lambda i: (0, 0)), pl.BlockSpec((1, c), lambda i: (0, 0))],
        out_specs=[pl.BlockSpec((tm, c), lambda i: (i, 0)), pl.BlockSpec((8, c), lambda i: (0, 0)),
                   pl.BlockSpec((8, c), lambda i: (0, 0))],
        name=name, compiler_params=_cp())(yc, dout, g, b)


def _conv_bwd_dw(dyc, cacg, w, name):
    lp, c = dyc.shape
    ncb = c // 128
    nchunk = lp // ABLK
    off = CONV_PAD - (CONV_K - 1)

    def body(dy_ref, a_ref, g_ref, w_ref, dcc_ref, dw_ref, db_ref, upad, dypad, dwacc):
        upad[0:CONV_PAD, :] = jnp.zeros((CONV_PAD, 128), F32)
        dypad[lp:lp + CONV_PAD, :] = jnp.zeros((CONV_PAD, 128), F32)
        dwacc[...] = jnp.zeros_like(dwacc)
        db_ref[...] = jnp.zeros_like(db_ref)

        def fill(ch, carry):
            base = pl.multiple_of(ch * ABLK, ABLK)
            upad[pl.ds(base + CONV_PAD, ABLK), :] = a_ref[pl.ds(base, ABLK), :] * _sigmoid(g_ref[pl.ds(base, ABLK), :])
            dypad[pl.ds(base, ABLK), :] = dy_ref[pl.ds(base, ABLK), :]
            return carry

        lax.fori_loop(0, nchunk, fill, 0)

        def comp(ch, carry):
            base = pl.multiple_of(ch * ABLK, ABLK)
            dy = dy_ref[pl.ds(base, ABLK), :]
            du = jnp.zeros((ABLK, 128), F32)
            for k in range(CONV_K):
                du = du + dypad[pl.ds(base + (CONV_K - 1 - k), ABLK), :] * w_ref[k:k + 1, :]
                dwacc[k * 8:(k + 1) * 8, :] += _sum8(dy * upad[pl.ds(base + (off + k), ABLK), :])
            db_ref[...] += _sum8(dy)
            a = a_ref[pl.ds(base, ABLK), :]
            s = _sigmoid(g_ref[pl.ds(base, ABLK), :])
            dcc_ref[0, pl.ds(base, ABLK), :] = (du * s).astype(BF16)
            dcc_ref[1, pl.ds(base, ABLK), :] = (du * a * (s * (1.0 - s))).astype(BF16)
            return carry

        lax.fori_loop(0, nchunk, comp, 0)
        dw_ref[...] = dwacc[...].reshape(CONV_PAD, 8, 128).sum(axis=1)

    return pl.pallas_call(
        body, out_shape=[SDS((2, lp, c), BF16), SDS((CONV_PAD, c), F32), SDS((8, c), F32)],
        grid=(ncb,),
        in_specs=[pl.BlockSpec((lp, 128), lambda j: (0, j)), pl.BlockSpec((lp, 128), lambda j: (0, j)),
                  pl.BlockSpec((lp, 128), lambda j: (0, j + ncb)), pl.BlockSpec((CONV_PAD, 128), lambda j: (0, j))],
        out_specs=[pl.BlockSpec((2, lp, 128), lambda j: (0, 0, j)),
                   pl.BlockSpec((CONV_PAD, 128), lambda j: (0, j)), pl.BlockSpec((8, 128), lambda j: (0, j))],
        scratch_shapes=[pltpu.VMEM((lp + CONV_PAD, 128), F32), pltpu.VMEM((lp + CONV_PAD, 128), F32),
                        pltpu.VMEM((CONV_PAD * 8, 128), F32)],
        name=name, compiler_params=_cp())(dyc, cacg, cacg, w)


def _mesh_pos():
    x, y, c = lax.axis_index("x"), lax.axis_index("y"), lax.axis_index("c")
    return x, y, c


def _peer(pos, r):
    x, y, c = pos
    px = (1 - x) if (r >> 2) & 1 else x
    py = (1 - y) if (r >> 1) & 1 else y
    pc = (1 - c) if r & 1 else c
    return (px, py, pc), 4 * px + 2 * py + pc


class _Job:
    def __init__(self, src, dst, scatter, src_layer=None, dst_layer=None):
        self.src, self.dst, self.scatter, self.src_layer, self.dst_layer = src, dst, scatter, src_layer, dst_layer

    def src_view(self, ins, idx):
        v = ins[self.src] if self.src_layer is None else ins[self.src].at[self.src_layer]
        return v.at[idx] if self.scatter else v

    def dst_view(self, outs, slot):
        v = outs[self.dst] if self.dst_layer is None else outs[self.dst].at[self.dst_layer]
        return v.at[slot]


def _exchange_start(jobs, ins, outs, send, recv, loc):
    pos = _mesh_pos()
    me = 4 * pos[0] + 2 * pos[1] + pos[2]
    for j, job in enumerate(jobs):
        pltpu.make_async_copy(job.src_view(ins, me), job.dst_view(outs, me), loc.at[j]).start()
        for r in range(1, N_DEV):
            peer, peer_idx = _peer(pos, r)
            pltpu.make_async_remote_copy(src_ref=job.src_view(ins, peer_idx), dst_ref=job.dst_view(outs, me),
                                         send_sem=send.at[j, r - 1], recv_sem=recv.at[j, r - 1], device_id=peer,
                                         device_id_type=pl.DeviceIdType.MESH).start()


def _exchange_wait(jobs, ins, outs, send, recv, loc):
    pos = _mesh_pos()
    me = 4 * pos[0] + 2 * pos[1] + pos[2]
    for j, job in enumerate(jobs):
        for r in range(1, N_DEV):
            peer, peer_idx = _peer(pos, r)
            cp = pltpu.make_async_remote_copy(src_ref=job.src_view(ins, peer_idx), dst_ref=job.dst_view(outs, peer_idx),
                                              send_sem=send.at[j, r - 1], recv_sem=recv.at[j, r - 1], device_id=peer,
                                              device_id_type=pl.DeviceIdType.MESH)
            cp.wait_recv()
            cp.wait_send()
        pltpu.make_async_copy(job.src_view(ins, me), job.dst_view(outs, me), loc.at[j]).wait()


def _exchange_sems(n_jobs):
    return [pltpu.SemaphoreType.DMA((n_jobs, N_DEV - 1)), pltpu.SemaphoreType.DMA((n_jobs, N_DEV - 1)),
            pltpu.SemaphoreType.DMA((n_jobs,))]


def _exchange(jobs, arrs, out_shape, name):
    n_in, n_out = len(arrs), len(out_shape)
    any_spec = pl.BlockSpec(memory_space=pl.ANY)

    def body(*refs):
        ins, outs, sems = refs[:n_in], refs[n_in:n_in + n_out], refs[n_in + n_out:]
        _exchange_start(jobs, ins, outs, *sems)
        _exchange_wait(jobs, ins, outs, *sems)

    return pl.pallas_call(
        body, out_shape=out_shape, in_specs=[any_spec] * n_in, out_specs=[any_spec] * n_out,
        scratch_shapes=_exchange_sems(len(jobs)),
        name=name, compiler_params=pltpu.CompilerParams(has_side_effects=True))(*arrs)


def _all_reduce_small(p8, q, name):
    ni, _, w = p8.shape
    nq = q.shape[0]
    rows = ni + nq

    def body(p_ref, q_ref, o_ref, buf, send, recv):
        pos = _mesh_pos()
        me = 4 * pos[0] + 2 * pos[1] + pos[2]
        buf[me] = jnp.concatenate([p_ref[...].sum(axis=1), q_ref[...]], axis=0)
        for r in range(1, N_DEV):
            peer, _ = _peer(pos, r)
            pltpu.make_async_remote_copy(src_ref=buf.at[me], dst_ref=buf.at[me], send_sem=send.at[r - 1],
                                         recv_sem=recv.at[r - 1], device_id=peer,
                                         device_id_type=pl.DeviceIdType.MESH).start()
        for r in range(1, N_DEV):
            peer, peer_idx = _peer(pos, r)
            cp = pltpu.make_async_remote_copy(src_ref=buf.at[me], dst_ref=buf.at[peer_idx], send_sem=send.at[r - 1],
                                              recv_sem=recv.at[r - 1], device_id=peer,
                                              device_id_type=pl.DeviceIdType.MESH)
            cp.wait_recv()
            cp.wait_send()
        acc = buf[0]
        for dev in range(1, N_DEV):
            acc = acc + buf[dev]
        o_ref[...] = acc

    vmem = pl.BlockSpec(memory_space=pltpu.VMEM)
    return pl.pallas_call(
        body, out_shape=SDS((rows, w), F32), in_specs=[vmem, vmem], out_specs=vmem,
        scratch_shapes=[pltpu.VMEM((N_DEV, rows, w), F32), pltpu.SemaphoreType.DMA((N_DEV - 1,)),
                        pltpu.SemaphoreType.DMA((N_DEV - 1,))],
        name=name, compiler_params=pltpu.CompilerParams(has_side_effects=True))(p8, q)


def _adamw_math(w, g, m, v):
    m = ADAM_B1 * m + (1.0 - ADAM_B1) * g
    v = ADAM_B2 * v + (1.0 - ADAM_B2) * (g * g)
    m_hat = m / (1.0 - ADAM_B1 ** ADAM_STEP)
    v_hat = v / (1.0 - ADAM_B2 ** ADAM_STEP)
    delta = -ADAM_LR * (m_hat / (jnp.sqrt(v_hat) + ADAM_EPS) + ADAM_WD * w)
    return delta, m, v


def _adamw_shard(parts, w, m, v, name):
    depth = len(parts)
    _, rr, cc = parts[0].shape
    tr = _tile(rr, 256, 8)
    nt = rr // tr

    def body(*refs):
        p_refs = refs[:depth]
        w_ref, m_ref, v_ref, g_out, d_out, m_out, v_out = refs[depth:]
        for li in range(depth):
            @pl.when(pl.program_id(0) == li)
            def _(p_ref=p_refs[li]):
                g = p_ref[0].astype(F32)
                for dev in range(1, N_DEV):
                    g = g + p_ref[dev].astype(F32)
                delta, mm, vv = _adamw_math(w_ref[...], g, m_ref[...], v_ref[...])
                g_out[...] = g
                d_out[...] = delta
                m_out[...] = mm
                v_out[...] = vv

    def part_spec(li):
        return pl.BlockSpec((N_DEV, tr, cc), lambda l, i: (0, jnp.where(l == li, i, jnp.where(l < li, 0, nt - 1)), 0))

    blk = pl.BlockSpec((None, tr, cc), lambda l, i: (l, i, 0))
    return pl.pallas_call(
        body, out_shape=[SDS((depth, rr, cc), F32)] * 4, grid=(depth, nt),
        in_specs=[part_spec(li) for li in range(depth)] + [blk, blk, blk],
        out_specs=[blk] * 4, name=name, compiler_params=_cp())(*parts, w, m, v)


def _adamw_flat(g, w, m, v, name):
    def body(g_ref, w_ref, m_ref, v_ref, d_out, m_out, v_out):
        delta, mm, vv = _adamw_math(w_ref[...], g_ref[...], m_ref[...], v_ref[...])
        d_out[...] = delta
        m_out[...] = mm
        v_out[...] = vv

    return pl.pallas_call(body, out_shape=[SDS(g.shape, F32)] * 3, name=name, compiler_params=_cp())(g, w, m, v)


def _from_cols(t):
    return jnp.transpose(t, (1, 0, 2)).reshape(t.shape[1], N_DEV * t.shape[2])


def _swap(t):
    return jnp.swapaxes(t, -1, -2)


def kernel(x, meta_tokens, mix_norm_g, w_in, conv_dw_w, conv_dw_b, conv_ln_g, conv_ln_b, w_out, ffn_norm_g, w_gate, w_up, w_down, final_norm_g, loss_target, m_meta_tokens, m_mix_norm_g, m_w_in, m_conv_dw_w, m_conv_dw_b, m_conv_ln_g, m_conv_ln_b, m_w_out, m_ffn_norm_g, m_w_gate, m_w_up, m_w_down, m_final_norm_g, v_meta_tokens, v_mix_norm_g, v_w_in, v_conv_dw_w, v_conv_dw_b, v_conv_ln_g, v_conv_ln_b, v_w_out, v_ffn_norm_g, v_w_gate, v_w_up, v_w_down, v_final_norm_g):
    depth, d, in_shard = w_in.shape
    seq = x.shape[1]
    sb = N_HEADS * HEAD_DIM
    cc = conv_dw_w.shape[2] * N_DEV
    ff = w_gate.shape[2] * N_DEV
    assert in_shard * N_DEV == 3 * sb + 2 * cc and x.shape[0] == 1
    lr = N_META + seq
    lp = -(-lr // ABLK) * ABLK
    me = 4 * lax.axis_index("x") + 2 * lax.axis_index("y") + lax.axis_index("c")

    big_names = ("w_in", "w_out", "w_gate", "w_up", "w_down")
    transposed = {"w_in": True, "w_out": False, "w_gate": True, "w_up": True, "w_down": False}
    shard = dict(w_in=_swap(w_in).astype(BF16), w_out=w_out.astype(BF16), w_gate=_swap(w_gate).astype(BF16),
                 w_up=_swap(w_up).astype(BF16), w_down=w_down.astype(BF16))

    def gather_of(keys):
        names = sorted({n for n, _ in keys}, key=big_names.index)
        jobs = [_Job(names.index(n), j, False, src_layer=i) for j, (n, i) in enumerate(keys)]
        return jobs, [shard[n] for n in names], [SDS((N_DEV,) + shard[n].shape[1:], BF16) for n, _ in keys]

    first_keys = [("w_in", 0)]
    later_keys = [(n, i) for n in big_names for i in range(depth) if (n, i) not in first_keys]
    jobs, srcs, out_shape = gather_of(first_keys)
    for extra in (meta_tokens, conv_dw_w):
        jobs.append(_Job(len(srcs), len(out_shape), False))
        srcs.append(extra)
        out_shape.append(SDS((N_DEV,) + extra.shape, F32))
    gathered = _exchange(jobs, srcs, out_shape, "gather_first")
    wl = [dict() for _ in range(depth)]
    for (n, i), t in zip(first_keys, gathered):
        wl[i][n] = t.reshape(-1, d)
    meta_full = _from_cols(gathered[-2])
    taps = jnp.transpose(gathered[-1], (1, 2, 0, 3)).reshape(depth, CONV_K, cc)
    taps = jnp.pad(taps, ((0, 0), (0, CONV_PAD - CONV_K), (0, 0)))
    tri_fwd, tri_after, tri_before = _tri_consts()

    h = jnp.concatenate([meta_full, x[0], jnp.zeros((lp - lr, d), F32)], axis=0)
    saved = []
    for i in range(depth):
        p = wl[i]
        sv = dict(h_in=h)
        qkv, hn = _rms_mm(h, mix_norm_g[i:i + 1], p["w_in"], 0, 3 * sb, BF16, f"proj_qkv_{i}", True)
        cacg = _rms_mm(h, mix_norm_g[i:i + 1], p["w_in"], 3 * sb, 2 * cc, F32, f"proj_conv_{i}", False)[0]
        rider = _Rider(*gather_of(later_keys)) if i == 0 else None
        attn, rsave, *rest = _attn_fwd(qkv, tri_fwd, f"attn_fwd_{i}", rider)
        if i == 0:
            for (n, li), t in zip(later_keys, rest):
                wl[li][n] = t.reshape(-1, d)
        yc = _conv_fwd_dw(cacg, taps[i], conv_dw_b[i:i + 1], f"conv_fwd_dw_{i}")
        conv = _conv_fwd_ln(yc, conv_ln_g[i:i + 1], conv_ln_b[i:i + 1], f"conv_fwd_ln_{i}")
        h = _mix_out(attn, conv, p["w_out"], h, f"mix_out_{i}")
        sv.update(qkv=qkv, hn=hn, cacg=cacg, rsave=rsave, yc=yc, attn=attn, conv=conv, h_mid=h)
        h, hn2, act = _ffn_fwd(h, ffn_norm_g[i:i + 1], p["w_gate"], p["w_up"], p["w_down"], f"ffn_fwd_{i}")
        sv.update(hn2=hn2, act=act)
        saved.append(sv)

    tpad = jnp.pad(loss_target[0], ((N_META, lp - lr), (0, 0)))
    dh, dg_final, loss_part = _loss_head(h, tpad, final_norm_g.reshape(1, d), seq, "loss_head")
    loss = lax.psum(loss_part[0, 0], MESH_AXES)

    def scatter_of(items):
        jobs = [_Job(j, j, True) for j in range(len(items))]
        srcs = [t.reshape(N_DEV, t.shape[0] // N_DEV, d) for t in items]
        return jobs, srcs, [SDS(t.shape, BF16) for t in srcs]

    grads = [None] * depth
    pending, parts = [], {}
    for i in reversed(range(depth)):
        p, sv = wl[i], saved[i]
        d_g, d_u = _ffn_bwd_act(dh, sv["hn2"], p["w_gate"], p["w_up"], p["w_down"], f"ffn_bwd_act_{i}")
        gw_down = _mm_tn(sv["act"], dh, f"grad_w_down_{i}")
        gw_gate = _mm_tn(d_g, sv["hn2"], f"grad_w_gate_{i}")
        gw_up = _mm_tn(d_u, sv["hn2"], f"grad_w_up_{i}")
        pending += [(("w_gate", i), gw_gate), (("w_up", i), gw_up), (("w_down", i), gw_down)]
        dh, dg_ffn = _ffn_bwd_in(d_g, d_u, p["w_gate"], p["w_up"], sv["h_mid"], ffn_norm_g[i:i + 1], dh, f"ffn_bwd_in_{i}")
        gw_out = jnp.concatenate([_mm_tn(sv["attn"], dh, f"grad_w_out_attn_{i}"),
                                  _mm_tn(sv["conv"], dh, f"grad_w_out_conv_{i}")], axis=0)
        d_attn = _mm_nt(dh, p["w_out"], 0, sb, BF16, f"mix_bwd_attn_{i}")
        d_conv = _mm_nt(dh, p["w_out"], sb, cc, F32, f"mix_bwd_conv_{i}")
        dqkv, *arrived = _attn_bwd(sv["qkv"], d_attn, sv["rsave"], tri_after, tri_before, f"attn_bwd_{i}",
                                   _Rider(*scatter_of([t for _, t in pending])))
        parts.update({key: t for (key, _), t in zip(pending, arrived)})
        dyc, dg_ln, db_ln = _conv_bwd_ln(sv["yc"], d_conv, conv_ln_g[i:i + 1], conv_ln_b[i:i + 1], f"conv_bwd_ln_{i}")
        dcc, g_taps_i, db_conv = _conv_bwd_dw(dyc, sv["cacg"], taps[i], f"conv_bwd_dw_{i}")
        gw_in = _grad_w_in_t(dqkv, dcc, sv["hn"], f"grad_w_in_{i}")
        pending = [(("w_in", i), gw_in), (("w_out", i), gw_out)]
        dh, dg_mix = _mix_bwd_in(dqkv, dcc, p["w_in"], sv["h_in"], mix_norm_g[i:i + 1], dh, f"mix_bwd_in_{i}")
        grads[i] = dict(taps=g_taps_i, dg_mix=dg_mix, dg_ffn=dg_ffn, dg_ln=dg_ln, db_ln=db_ln, db_conv=db_conv)
    arrived = _exchange(*scatter_of([t for _, t in pending]), "scatter_last")
    parts.update({key: t for (key, _), t in zip(pending, arrived)})
    grad_x = dh[N_META:lr][None]

    wide = lambda key: jnp.concatenate([jnp.pad(grads[i][key], ((0, 0), (0, d // depth - cc))) for i in range(depth)], axis=1)
    assert depth * cc <= d and d % depth == 0
    p8 = jnp.stack([grads[i]["dg_mix"] for i in range(depth)] + [grads[i]["dg_ffn"] for i in range(depth)]
                   + [dg_final, wide("db_conv"), wide("dg_ln"), wide("db_ln")])
    small = _all_reduce_small(p8, jnp.concatenate([wide("taps"), dh[:N_META]], axis=0), "reduce_small")
    r0 = 2 * depth + 4
    g_mix = small[0:depth]
    g_ffn = small[depth:2 * depth]
    g_final = small[2 * depth]
    narrow = lambda row: jnp.stack([row[i * (d // depth):i * (d // depth) + cc] for i in range(depth)])
    g_cb, g_lg, g_lb = narrow(small[2 * depth + 1]), narrow(small[2 * depth + 2]), narrow(small[2 * depth + 3])
    g_taps_full = jnp.stack([small[r0:r0 + CONV_K, i * (d // depth):i * (d // depth) + cc] for i in range(depth)])
    csh = cc // N_DEV
    g_taps_own = lax.dynamic_slice_in_dim(g_taps_full, me * csh, csh, axis=2)
    g_meta_full = small[r0 + CONV_PAD:r0 + CONV_PAD + N_META]
    msh = d // N_DEV
    g_meta_own = lax.dynamic_slice_in_dim(g_meta_full, me * msh, msh, axis=1)

    small_g = [g_meta_own, g_mix, g_taps_own, g_cb, g_lg, g_lb, g_ffn, g_final]
    small_w = [meta_tokens, mix_norm_g, conv_dw_w, conv_dw_b, conv_ln_g, conv_ln_b, ffn_norm_g, final_norm_g]
    small_m = [m_meta_tokens, m_mix_norm_g, m_conv_dw_w, m_conv_dw_b, m_conv_ln_g, m_conv_ln_b, m_ffn_norm_g, m_final_norm_g]
    small_v = [v_meta_tokens, v_mix_norm_g, v_conv_dw_w, v_conv_dw_b, v_conv_ln_g, v_conv_ln_b, v_ffn_norm_g, v_final_norm_g]
    sizes = [int(math.prod(t.shape)) for t in small_w]
    total = sum(sizes)
    rows = -(-total // (8 * 128)) * 8

    def flat(ts):
        v = jnp.concatenate([t.reshape(-1) for t in ts])
        return jnp.pad(v, (0, rows * 128 - total)).reshape(rows, 128)

    def unflat(t):
        v, out, o = t.reshape(-1), [], 0
        for sz, ref in zip(sizes, small_w):
            out.append(v[o:o + sz].reshape(ref.shape))
            o += sz
        return out

    sd, sm, sv_ = _adamw_flat(flat(small_g), flat(small_w), flat(small_m), flat(small_v), "adamw_small")
    s_delta, s_m, s_v = unflat(sd), unflat(sm), unflat(sv_)

    big = []
    for n, w, m, v in zip(big_names, (w_in, w_out, w_gate, w_up, w_down), (m_w_in, m_w_out, m_w_gate, m_w_up, m_w_down),
                          (v_w_in, v_w_out, v_w_gate, v_w_up, v_w_down)):
        fix = _swap if transposed[n] else (lambda t: t)
        res = _adamw_shard([parts[(n, i)] for i in range(depth)], fix(w), fix(m), fix(v), f"adamw_{n}")
        big.append([fix(t) for t in res])
    b_in, b_out, b_gate, b_up, b_down = big

    def ordered(k, smalls):
        s_meta, s_mix, s_taps, s_cb, s_lg, s_lb, s_ffn, s_final = smalls
        return [s_meta, s_mix, b_in[k], s_taps, s_cb, s_lg, s_lb, b_out[k], s_ffn, b_gate[k], b_up[k], b_down[k], s_final]

    return (loss, grad_x, *ordered(0, small_g), *ordered(1, s_delta), *ordered(2, s_m), *ordered(3, s_v))
```

```python
import functools
import math

import jax
import jax.numpy as jnp
from jax import lax
from jax.experimental import pallas as pl
from jax.experimental.pallas import tpu as pltpu

F32 = jnp.float32
BF16 = jnp.bfloat16
SDS = jax.ShapeDtypeStruct

N_META = 16
N_HEADS = 8
HEAD_DIM = 64
CONV_K = 31
CONV_PAD = 32
ABLK = 128
ATT_GROUP = 4
ATT_SPLIT = 2
ATT_TOP = 2
EXP_ZERO_AT = 104.0
GONE = 1e30
MASKED = -1e30
FF_CHUNK = 256
EPS = 1e-6
N_DEV = 8
MESH_AXES = ("x", "y", "c")
ADAM_LR = 0.001
ADAM_B1 = 0.9
ADAM_B2 = 0.999
ADAM_EPS = 1e-08
ADAM_WD = 0.01
ADAM_STEP = 10
MIB = 1 << 20
VMEM_LIMIT_MIB = 48


def _cp():
    return pltpu.CompilerParams(vmem_limit_bytes=VMEM_LIMIT_MIB * MIB)


def _tile(n, cap, mult):
    best = None
    for t in range(mult, min(n, cap) + 1, mult):
        if n % t == 0:
            best = t
    assert best is not None, (n, cap, mult)
    return best


def _sum8(v):
    r, c = v.shape
    return v.reshape(r // 8, 8, c).sum(axis=0)


def _sigmoid(x):
    return 1.0 / (1.0 + jnp.exp(-x))


NT = (((1,), (1,)), ((), ()))
TN = (((0,), (0,)), ((), ()))


def _rms_mm(h, g, wt, w_row0, n_cols, out_dtype, name, hn_out):
    lp, d = h.shape
    tm = _tile(lp, 528, 16)
    tn = _tile(math.gcd(n_cols, w_row0), 512, 128)
    off = w_row0 // tn

    def body(h_ref, g_ref, w_ref, *rest):
        if hn_out:
            o_ref, hn_ref, hn_s = rest
        else:
            o_ref, hn_s = rest

        @pl.when(pl.program_id(1) == 0)
        def _():
            x = h_ref[...]
            r = lax.rsqrt(jnp.mean(x * x, axis=-1, keepdims=True) + EPS)
            hn = ((x * r) * g_ref[...]).astype(BF16)
            hn_s[...] = hn
            if hn_out:
                hn_ref[...] = hn

        o_ref[...] = lax.dot_general(hn_s[...], w_ref[...], NT, preferred_element_type=F32).astype(out_dtype)

    out_shape = [SDS((lp, n_cols), out_dtype)]
    out_specs = [pl.BlockSpec((tm, tn), lambda i, j: (i, j))]
    if hn_out:
        out_shape.append(SDS((lp, d), BF16))
        out_specs.append(pl.BlockSpec((tm, d), lambda i, j: (i, 0)))
    return pl.pallas_call(
        body, out_shape=out_shape, grid=(lp // tm, n_cols // tn),
        in_specs=[pl.BlockSpec((tm, d), lambda i, j: (i, 0)),
                  pl.BlockSpec((1, d), lambda i, j: (0, 0)),
                  pl.BlockSpec((tn, d), lambda i, j: (j + off, 0))],
        out_specs=out_specs, scratch_shapes=[pltpu.VMEM((tm, d), BF16)],
        name=name, compiler_params=_cp())(h, g, wt)


def _mm_nt(a, b, b_row0, n_out, out_dtype, name):
    m, k = a.shape
    tm = _tile(m, 528, 16)
    tn = _tile(math.gcd(n_out, b_row0), 512, 128)
    off = b_row0 // tn

    def body(a_ref, b_ref, o_ref):
        o_ref[...] = lax.dot_general(a_ref[...].astype(BF16), b_ref[...], NT,
                                     preferred_element_type=F32).astype(out_dtype)

    return pl.pallas_call(
        body, out_shape=SDS((m, n_out), out_dtype), grid=(m // tm, n_out // tn),
        in_specs=[pl.BlockSpec((tm, k), lambda i, j: (i, 0)), pl.BlockSpec((tn, k), lambda i, j: (j + off, 0))],
        out_specs=pl.BlockSpec((tm, tn), lambda i, j: (i, j)), name=name, compiler_params=_cp())(a, b)


def _mix_out(attn, conv, w, res, name):
    m, ka = attn.shape
    kc = conv.shape[1]
    n = w.shape[1]
    assert ka == kc
    tm = _tile(m, 528, 16)
    tn = _tile(n, 512, 128)

    def body(a_ref, c_ref, wa_ref, wc_ref, r_ref, o_ref):
        o_ref[...] = (r_ref[...] + jnp.dot(a_ref[...], wa_ref[...], preferred_element_type=F32)
                      + jnp.dot(c_ref[...], wc_ref[...], preferred_element_type=F32))

    return pl.pallas_call(
        body, out_shape=SDS((m, n), F32), grid=(m // tm, n // tn),
        in_specs=[pl.BlockSpec((tm, ka), lambda i, j: (i, 0)), pl.BlockSpec((tm, kc), lambda i, j: (i, 0)),
                  pl.BlockSpec((ka, tn), lambda i, j: (0, j)), pl.BlockSpec((kc, tn), lambda i, j: (1, j)),
                  pl.BlockSpec((tm, tn), lambda i, j: (i, j))],
        out_specs=pl.BlockSpec((tm, tn), lambda i, j: (i, j)), name=name, compiler_params=_cp())(attn, conv, w, w, res)


def _mm_tn(a, b, name):
    l, m = a.shape
    n = b.shape[1]
    tm = _tile(m, 1408, 128)
    tn = _tile(n, 1024, 128)
    tl = _tile(l, 512, 128)
    nl = l // tl

    def body(a_ref, b_ref, o_ref, acc):
        @pl.when(pl.program_id(2) == 0)
        def _():
            acc[...] = jnp.zeros_like(acc)

        acc[...] += lax.dot_general(a_ref[...].astype(BF16), b_ref[...].astype(BF16), TN, preferred_element_type=F32)

        @pl.when(pl.program_id(2) == nl - 1)
        def _():
            o_ref[...] = acc[...].astype(BF16)

    return pl.pallas_call(
        body, out_shape=SDS((m, n), BF16), grid=(m // tm, n // tn, nl),
        in_specs=[pl.BlockSpec((tl, tm), lambda i, j, s: (s, i)),
                  pl.BlockSpec((tl, tn), lambda i, j, s: (s, j))],
        out_specs=pl.BlockSpec((tm, tn), lambda i, j, s: (i, j)),
        scratch_shapes=[pltpu.VMEM((tm, tn), F32)], name=name, compiler_params=_cp())(a, b)


def _grad_w_in_t(dqkv, dcc, hn, name):
    nq, l, w = dqkv.shape
    nc = dcc.shape[0]
    d = hn.shape[1]
    assert dcc.shape[2] == w
    tn = _tile(d, 1024, 128)
    tl = _tile(l, 512, 128)
    nl = l // tl

    def body(q_ref, c_ref, b_ref, o_ref, acc):
        p, s = pl.program_id(0), pl.program_id(2)

        @pl.when(s == 0)
        def _():
            acc[...] = jnp.zeros_like(acc)

        @pl.when(p < nq)
        def _():
            acc[...] += lax.dot_general(q_ref[...], b_ref[...], TN, preferred_element_type=F32)

        @pl.when(p >= nq)
        def _():
            acc[...] += lax.dot_general(c_ref[...], b_ref[...], TN, preferred_element_type=F32)

        @pl.when(s == nl - 1)
        def _():
            o_ref[...] = acc[...].astype(BF16)

    out = pl.pallas_call(
        body, out_shape=SDS((nq + nc, w, d), BF16), grid=(nq + nc, d // tn, nl),
        in_specs=[pl.BlockSpec((None, tl, w), lambda p, j, s: (jnp.minimum(p, nq - 1), s, 0)),
                  pl.BlockSpec((None, tl, w), lambda p, j, s: (jnp.maximum(p - nq, 0), s, 0)),
                  pl.BlockSpec((tl, tn), lambda p, j, s: (s, j))],
        out_specs=pl.BlockSpec((None, w, tn), lambda p, j, s: (p, 0, j)),
        scratch_shapes=[pltpu.VMEM((w, tn), F32)], name=name, compiler_params=_cp())(dqkv, dcc, hn)
    return out.reshape((nq + nc) * w, d)


def _rms_bwd_tail(acc, h_ref, g_ref, dres_ref, o_ref, dg_ref):
    x = h_ref[...]
    r = lax.rsqrt(jnp.mean(x * x, axis=-1, keepdims=True) + EPS)
    xr = x * r
    dyv = acc[...]
    gy = dyv * g_ref[...]
    o_ref[...] = dres_ref[...] + r * (gy - xr * jnp.mean(gy * xr, axis=-1, keepdims=True))
    dg_ref[...] += _sum8(dyv * xr)


def _mix_bwd_in(dqkv, dcc, wt, h, g, dres, name):
    nq, lp, w = dqkv.shape
    nc = dcc.shape[0]
    d = h.shape[1]
    tm = _tile(lp, 528, 16)
    npc = nq + nc

    def body(q_ref, c_ref, w_ref, h_ref, g_ref, dres_ref, o_ref, dg_ref, acc):
        i, p = pl.program_id(0), pl.program_id(1)

        @pl.when(p == 0)
        def _():
            acc[...] = jnp.zeros_like(acc)

        @pl.when((p == 0) & (i == 0))
        def _():
            dg_ref[...] = jnp.zeros_like(dg_ref)

        @pl.when(p < nq)
        def _():
            acc[...] += jnp.dot(q_ref[...], w_ref[...], preferred_element_type=F32)

        @pl.when(p >= nq)
        def _():
            acc[...] += jnp.dot(c_ref[...], w_ref[...], preferred_element_type=F32)

        @pl.when(p == npc - 1)
        def _():
            _rms_bwd_tail(acc, h_ref, g_ref, dres_ref, o_ref, dg_ref)

    return pl.pallas_call(
        body, out_shape=[SDS((lp, d), F32), SDS((8, d), F32)], grid=(lp // tm, npc),
        in_specs=[pl.BlockSpec((None, tm, w), lambda i, p: (jnp.minimum(p, nq - 1), i, 0)),
                  pl.BlockSpec((None, tm, w), lambda i, p: (jnp.maximum(p - nq, 0), i, 0)),
                  pl.BlockSpec((w, d), lambda i, p: (p, 0)),
                  pl.BlockSpec((tm, d), lambda i, p: (i, 0)),
                  pl.BlockSpec((1, d), lambda i, p: (0, 0)),
                  pl.BlockSpec((tm, d), lambda i, p: (i, 0))],
        out_specs=[pl.BlockSpec((tm, d), lambda i, p: (i, 0)), pl.BlockSpec((8, d), lambda i, p: (0, 0))],
        scratch_shapes=[pltpu.VMEM((tm, d), F32)], name=name, compiler_params=_cp())(dqkv, dcc, wt, h, g, dres)


def _ffn_bwd_in(d_g, d_u, wg_t, wu_t, h, g, dres, name):
    lp, f = d_g.shape
    d = h.shape[1]
    tm = _tile(lp, 528, 16)
    tk = _tile(f, 512, 128)
    nk = f // tk

    def body(dg_in, du_in, wg_ref, wu_ref, h_ref, g_ref, dres_ref, o_ref, dg_ref, acc):
        i, kk = pl.program_id(0), pl.program_id(1)

        @pl.when(kk == 0)
        def _():
            acc[...] = jnp.zeros_like(acc)

        @pl.when((kk == 0) & (i == 0))
        def _():
            dg_ref[...] = jnp.zeros_like(dg_ref)

        acc[...] += (jnp.dot(dg_in[...], wg_ref[...], preferred_element_type=F32)
                     + jnp.dot(du_in[...], wu_ref[...], preferred_element_type=F32))

        @pl.when(kk == nk - 1)
        def _():
            _rms_bwd_tail(acc, h_ref, g_ref, dres_ref, o_ref, dg_ref)

    return pl.pallas_call(
        body, out_shape=[SDS((lp, d), F32), SDS((8, d), F32)], grid=(lp // tm, nk),
        in_specs=[pl.BlockSpec((tm, tk), lambda i, j: (i, j)), pl.BlockSpec((tm, tk), lambda i, j: (i, j)),
                  pl.BlockSpec((tk, d), lambda i, j: (j, 0)), pl.BlockSpec((tk, d), lambda i, j: (j, 0)),
                  pl.BlockSpec((tm, d), lambda i, j: (i, 0)),
                  pl.BlockSpec((1, d), lambda i, j: (0, 0)),
                  pl.BlockSpec((tm, d), lambda i, j: (i, 0))],
        out_specs=[pl.BlockSpec((tm, d), lambda i, j: (i, 0)), pl.BlockSpec((8, d), lambda i, j: (0, 0))],
        scratch_shapes=[pltpu.VMEM((tm, d), F32)], name=name, compiler_params=_cp())(d_g, d_u, wg_t, wu_t, h, g, dres)


def _ffn_fwd(h, g, wg_t, wu_t, w_d, name):
    lp, d = h.shape
    f = w_d.shape[0]
    tm = _tile(lp, 528, 16)
    nc = f // FF_CHUNK

    def body(h_ref, g_ref, wg_ref, wu_ref, wd_ref, o_ref, hn_ref, act_ref, hn_s, acc):
        c = pl.program_id(1)

        @pl.when(c == 0)
        def _():
            x = h_ref[...]
            r = lax.rsqrt(jnp.mean(x * x, axis=-1, keepdims=True) + EPS)
            hn = ((x * r) * g_ref[...]).astype(BF16)
            hn_s[...] = hn
            hn_ref[...] = hn
            acc[...] = jnp.zeros_like(acc)

        gg = lax.dot_general(hn_s[...], wg_ref[...], NT, preferred_element_type=F32)
        uu = lax.dot_general(hn_s[...], wu_ref[...], NT, preferred_element_type=F32)
        act = ((gg * _sigmoid(gg)) * uu).astype(BF16)
        act_ref[...] = act
        acc[...] += jnp.dot(act, wd_ref[...], preferred_element_type=F32)

        @pl.when(c == nc - 1)
        def _():
            o_ref[...] = h_ref[...] + acc[...]

    chunk = pl.BlockSpec((FF_CHUNK, d), lambda i, j: (j, 0))
    return pl.pallas_call(
        body, out_shape=[SDS((lp, d), F32), SDS((lp, d), BF16), SDS((lp, f), BF16)], grid=(lp // tm, nc),
        in_specs=[pl.BlockSpec((tm, d), lambda i, j: (i, 0)), pl.BlockSpec((1, d), lambda i, j: (0, 0)),
                  chunk, chunk, chunk],
        out_specs=[pl.BlockSpec((tm, d), lambda i, j: (i, 0)),
                   pl.BlockSpec((tm, d), lambda i, j: (i, 0)),
                   pl.BlockSpec((tm, FF_CHUNK), lambda i, j: (i, j))],
        scratch_shapes=[pltpu.VMEM((tm, d), BF16), pltpu.VMEM((tm, d), F32)],
        name=name, compiler_params=_cp())(h, g, wg_t, wu_t, w_d)


def _ffn_bwd_act(dh, hn, wg_t, wu_t, w_d, name):
    lp, d = dh.shape
    f = w_d.shape[0]
    tm = _tile(lp, 528, 16)
    nc = f // FF_CHUNK

    def body(dh_ref, hn_ref, wg_ref, wu_ref, wd_ref, dg_out, du_out, dh_s):
        @pl.when(pl.program_id(1) == 0)
        def _():
            dh_s[...] = dh_ref[...].astype(BF16)

        gg = lax.dot_general(hn_ref[...], wg_ref[...], NT, preferred_element_type=F32)
        uu = lax.dot_general(hn_ref[...], wu_ref[...], NT, preferred_element_type=F32)
        dact = lax.dot_general(dh_s[...], wd_ref[...], NT, preferred_element_type=F32)
        s = _sigmoid(gg)
        dg_out[...] = (dact * uu * (s * (1.0 + gg * (1.0 - s)))).astype(BF16)
        du_out[...] = (dact * (gg * s)).astype(BF16)

    chunk = pl.BlockSpec((FF_CHUNK, d), lambda i, j: (j, 0))
    out = pl.BlockSpec((tm, FF_CHUNK), lambda i, j: (i, j))
    return pl.pallas_call(
        body, out_shape=[SDS((lp, f), BF16), SDS((lp, f), BF16)], grid=(lp // tm, nc),
        in_specs=[pl.BlockSpec((tm, d), lambda i, j: (i, 0)), pl.BlockSpec((tm, d), lambda i, j: (i, 0)),
                  chunk, chunk, chunk],
        out_specs=[out, out], scratch_shapes=[pltpu.VMEM((tm, d), BF16)],
        name=name, compiler_params=_cp())(dh, hn, wg_t, wu_t, w_d)


def _loss_head(h, tpad, g, n_real, name):
    lp, d = h.shape
    tm = _tile(lp, 528, 16)

    def body(h_ref, t_ref, g_ref, dh_ref, dg_ref, loss_ref):
        i = pl.program_id(0)

        @pl.when(i == 0)
        def _():
            dg_ref[...] = jnp.zeros_like(dg_ref)
            loss_ref[...] = jnp.zeros_like(loss_ref)

        x = h_ref[...]
        r = lax.rsqrt(jnp.mean(x * x, axis=-1, keepdims=True) + EPS)
        xr = x * r
        y = xr * g_ref[...]
        row = i * tm + lax.broadcasted_iota(jnp.int32, (tm, d), 0)
        valid = (row >= N_META) & (row < N_META + n_real)
        diff = jnp.where(valid, y - t_ref[...], 0.0)
        loss_ref[...] += jnp.sum(diff * diff) * (0.5 / d)
        dy = diff * (1.0 / d)
        gy = dy * g_ref[...]
        dh_ref[...] = r * (gy - xr * jnp.mean(gy * xr, axis=-1, keepdims=True))
        dg_ref[...] += _sum8(dy * xr)

    return pl.pallas_call(
        body, out_shape=[SDS((lp, d), F32), SDS((8, d), F32), SDS((8, 128), F32)], grid=(lp // tm,),
        in_specs=[pl.BlockSpec((tm, d), lambda i: (i, 0)),
                  pl.BlockSpec((tm, d), lambda i: (i, 0)),
                  pl.BlockSpec((1, d), lambda i: (0, 0))],
        out_specs=[pl.BlockSpec((tm, d), lambda i: (i, 0)),
                   pl.BlockSpec((8, d), lambda i: (0, 0)),
                   pl.BlockSpec((8, 128), lambda i: (0, 0))],
        name=name, compiler_params=_cp())(h, tpad, g)


def _tri_consts():
    j = lax.broadcasted_iota(jnp.int32, (ABLK, ABLK), 0)
    s = lax.broadcasted_iota(jnp.int32, (ABLK, ABLK), 1)
    after = (j >= s).astype(BF16)
    before = (j < s).astype(BF16)
    ones = jnp.ones((ABLK, ABLK), BF16)
    two = lambda t: jnp.concatenate([t, t], axis=0)
    return (two(jnp.concatenate([after, ones], axis=1)),
            two(after),
            two(jnp.concatenate([before, ones], axis=1)))


def _softplus(z):
    neg_abs = lax.bitcast_convert_type(lax.bitcast_convert_type(z, jnp.uint32) | jnp.uint32(0x80000000), F32)
    return jnp.log(1.0 + jnp.exp(neg_abs)) + jnp.maximum(z, 0.0)


def _split_hi_lo(m):
    hi = m.astype(BF16)
    lo = (m - hi.astype(F32)).astype(BF16)
    return jnp.concatenate([hi, lo], axis=1)


def _head_halves(t2, in_a):
    zero = jnp.zeros_like(t2)
    return jnp.concatenate([jnp.where(in_a, t2, zero), jnp.where(in_a, zero, t2)], axis=0)


def _stack_blocks(t, nblk, in_a):
    return jnp.concatenate([_head_halves(t[u * ABLK:(u + 1) * ABLK], in_a) for u in range(nblk)], axis=0)


def _attn_scale():
    scale = 1.0 / math.sqrt(HEAD_DIM)
    assert math.frexp(scale)[0] == 0.5, "a power of two, so that scaling q in bf16 is exact"
    return scale


def _pow2_below(n):
    assert n & (n - 1) == 0
    return [p for p in (64, 32, 16, 8, 4, 2, 1) if p < n]


class _Rider:
    def __init__(self, jobs=(), srcs=(), out_shape=()):
        self.jobs, self.srcs, self.out_shape = list(jobs), list(srcs), list(out_shape)
        self.any = [pl.BlockSpec(memory_space=pl.ANY)]

    def split(self, rest, n_out, n_scratch):
        ni, no = len(self.srcs), len(self.out_shape)
        self.ins, outs = rest[:ni], rest[ni:ni + n_out]
        self.outs = rest[ni + n_out:ni + n_out + no]
        scratch = rest[ni + n_out + no:ni + n_out + no + n_scratch]
        self.sems = rest[ni + n_out + no + n_scratch:]
        return outs, scratch

    def start(self, first):
        if self.jobs:
            @pl.when(first)
            def _():
                _exchange_start(self.jobs, self.ins, self.outs, *self.sems)

    def wait(self, last):
        if self.jobs:
            @pl.when(last)
            def _():
                _exchange_wait(self.jobs, self.ins, self.outs, *self.sems)

    def in_specs(self):
        return self.any * len(self.srcs)

    def out_specs(self):
        return self.any * len(self.out_shape)

    def scratch(self):
        return _exchange_sems(len(self.jobs)) if self.jobs else []


def _attn_fwd(qkv, tri_fwd, name, rider=None):
    lp = qkv.shape[0]
    n_pairs = (N_HEADS * HEAD_DIM) // 128
    nb = lp // ABLK
    assert nb <= 128 and 2 * HEAD_DIM == 128
    scale = _attn_scale()
    nt = (((1,), (1,)), ((), ()))
    rider = rider or _Rider()

    def body(q_ref, k_ref, v_ref, tri_ref, *rest):
        (o_ref, rs_ref), (r_s, acc_s, rs_s) = rider.split(rest, 2, 3)
        rider.start(pl.program_id(0) == 0)
        lane = lax.broadcasted_iota(jnp.int32, (ABLK, 128), 1)
        row = lax.broadcasted_iota(jnp.int32, (ABLK, 128), 0)
        in_a = lane < HEAD_DIM
        causal = lane < row

        def qblock(i, carry):
            q0 = pl.multiple_of(i * ABLK, ABLK)
            q2 = q_ref[pl.ds(q0, ABLK), :] * scale
            r_s[...] = jnp.zeros_like(r_s)
            acc_s[...] = jnp.zeros_like(acc_s)
            rs_s[...] = jnp.full_like(rs_s, GONE)

            def live():
                return (jnp.min(jnp.minimum(r_s[0], r_s[1])) < EXP_ZERO_AT).astype(jnp.int32)

            def step(kb0, nblk, diag):
                k0 = pl.multiple_of(kb0 * ABLK, ABLK)
                kbd = _stack_blocks(k_ref[pl.ds(k0, nblk * ABLK), :], nblk, in_a)
                vbd = _stack_blocks(v_ref[pl.ds(k0, nblk * ABLK), :], nblk, in_a)
                z = lax.dot_general(q2, kbd, nt, preferred_element_type=F32)
                ncol = 2 * nblk
                zt = [z[:, c * 128:(c + 1) * 128] for c in range(ncol)]
                if diag:
                    zt = [jnp.where(causal, t, MASKED) for t in zt]
                per = max(1, ncol // ATT_SPLIT)
                ce = [None] * ncol
                for c1 in range(ncol, 0, -per):
                    parts = []
                    for c in range(c1 - per, c1):
                        parts.append(_split_hi_lo(_softplus(zt[c])))
                    got = jnp.dot(jnp.concatenate(parts, axis=0), tri_ref[...], preferred_element_type=F32)
                    for c in range(c1 - per, c1):
                        ce[c] = got[(c - c1 + per) * 128:(c - c1 + per + 1) * 128]
                rr = [r_s[0], r_s[1]]
                rsv = [rs_s[:, :128], rs_s[:, 128:]]
                ws = [None] * ncol
                for u in reversed(range(nblk)):
                    for hh in range(2):
                        c = 2 * u + hh
                        ws[c] = jnp.exp(zt[c] - ce[c][:, :128] - rr[hh]).astype(BF16)
                        rsv[hh] = jnp.where(lane == kb0 + u, rr[hh], rsv[hh])
                        rr[hh] = rr[hh] + ce[c][:, 128:]
                acc_s[...] += jnp.dot(jnp.concatenate(ws, axis=1), vbd, preferred_element_type=F32)
                r_s[0] = rr[0]
                r_s[1] = rr[1]
                rs_s[:, :128] = rsv[0]
                rs_s[:, 128:] = rsv[1]

            step(i, 1, True)

            @pl.when(i >= ATT_TOP)
            def _():
                step(i - ATT_TOP, ATT_TOP, False)

            i_low = jnp.where(i >= ATT_TOP, i - ATT_TOP, i)
            n_grp = i_low // ATT_GROUP

            def more(c):
                return (c[0] < n_grp) & (c[1] > 0)

            def inner(c):
                step(i_low - ATT_GROUP * (c[0] + 1), ATT_GROUP, False)
                return c[0] + 1, live()

            lax.while_loop(more, inner, (jnp.int32(0), live()))
            rem = i_low - ATT_GROUP * n_grp
            for p in _pow2_below(ATT_GROUP):
                @pl.when(((rem & p) != 0) & (live() > 0))
                def _():
                    step(rem & (p - 1), p, False)

            o_ref[pl.ds(q0, ABLK), :] = acc_s[...].astype(BF16)
            rs_ref[pl.ds(q0, ABLK), :] = rs_s[...]
            return carry

        lax.fori_loop(0, nb, qblock, 0)
        rider.wait(pl.program_id(0) == n_pairs - 1)

    col = lambda o: (lambda p: (0, p + o))
    return pl.pallas_call(
        body, out_shape=[SDS((lp, n_pairs * 128), BF16), SDS((lp, n_pairs * 256), F32)] + rider.out_shape,
        grid=(n_pairs,),
        in_specs=[pl.BlockSpec((lp, 128), col(0)), pl.BlockSpec((lp, 128), col(n_pairs)),
                  pl.BlockSpec((lp, 128), col(2 * n_pairs)), pl.BlockSpec((256, 256), lambda p: (0, 0))]
        + rider.in_specs(),
        out_specs=[pl.BlockSpec((lp, 128), col(0)), pl.BlockSpec((lp, 256), col(0))] + rider.out_specs(),
        scratch_shapes=[pltpu.VMEM((2, ABLK, 128), F32), pltpu.VMEM((ABLK, 128), F32), pltpu.VMEM((ABLK, 256), F32)]
        + rider.scratch(),
        name=name, compiler_params=_cp())(qkv, qkv, qkv, tri_fwd, *rider.srcs)


def _attn_bwd(qkv, d_out, rsave, tri_after, tri_before, name, rider=None):
    lp = qkv.shape[0]
    n_pairs = (N_HEADS * HEAD_DIM) // 128
    nb = lp // ABLK
    scale = _attn_scale()
    nt = (((1,), (1,)), ((), ()))
    tn = (((0,), (0,)), ((), ()))
    rider = rider or _Rider()

    def body(q_ref, k_ref, v_ref, do_ref, rs_ref, ta_ref, tb_ref, *rest):
        (o_ref,), (dk_s, dv_s, dq_s, pc_s) = rider.split(rest, 1, 4)
        rider.start(pl.program_id(0) == 0)
        lane = lax.broadcasted_iota(jnp.int32, (ABLK, 128), 1)
        row = lax.broadcasted_iota(jnp.int32, (ABLK, 128), 0)
        in_a = lane < HEAD_DIM
        causal = lane < row
        dk_s[...] = jnp.zeros_like(dk_s)
        dv_s[...] = jnp.zeros_like(dv_s)

        def qblock(i, carry):
            q0 = pl.multiple_of(i * ABLK, ABLK)
            q2 = q_ref[pl.ds(q0, ABLK), :] * scale
            do2 = do_ref[pl.ds(q0, ABLK), :]
            q_st = _head_halves(q2, in_a)
            do_st = _head_halves(do2, in_a)
            dq_s[...] = jnp.zeros_like(dq_s)
            pc_s[...] = jnp.zeros_like(pc_s)

            def step(kb0, nblk, diag):
                k0 = pl.multiple_of(kb0 * ABLK, ABLK)
                kbd = _stack_blocks(k_ref[pl.ds(k0, nblk * ABLK), :], nblk, in_a)
                vbd = _stack_blocks(v_ref[pl.ds(k0, nblk * ABLK), :], nblk, in_a)
                z = lax.dot_general(q2, kbd, nt, preferred_element_type=F32)
                dw = lax.dot_general(do2, vbd, nt, preferred_element_type=F32)
                ncol = 2 * nblk
                zt = [z[:, c * 128:(c + 1) * 128] for c in range(ncol)]
                if diag:
                    zt = [jnp.where(causal, t, MASKED) for t in zt]
                per = max(1, ncol // ATT_SPLIT)
                batches = [range(c0, c0 + per) for c0 in range(0, ncol, per)]
                sps, ex, ws, dls, pe = [None] * ncol, [None] * ncol, [None] * ncol, [None] * ncol, [None] * ncol

                def mass(cols):
                    for c in cols:
                        sps[c] = _softplus(zt[c])
                    got = jnp.dot(jnp.concatenate([_split_hi_lo(sps[c]) for c in cols], axis=0), ta_ref[...],
                                  preferred_element_type=F32)
                    for j, c in enumerate(cols):
                        ex[c] = got[j * 128:(j + 1) * 128]

                def weights(cols):
                    for c in cols:
                        u, hh = c // 2, c % 2
                        r_saved = jnp.sum(jnp.where(lane == kb0 + u, rs_ref[pl.ds(q0, ABLK), hh * 128:(hh + 1) * 128],
                                                    0.0), axis=1, keepdims=True)
                        w = jnp.exp(zt[c] - ex[c] - r_saved)
                        ws[c] = w.astype(BF16)
                        dls[c] = dw[:, c * 128:(c + 1) * 128] * w
                    got = jnp.dot(jnp.concatenate([_split_hi_lo(dls[c]) for c in cols], axis=0), tb_ref[...],
                                  preferred_element_type=F32)
                    for j, c in enumerate(cols):
                        pe[c] = got[j * 128:(j + 1) * 128]

                mass(batches[0])
                for j in range(len(batches)):
                    if j + 1 < len(batches):
                        mass(batches[j + 1])
                    weights(batches[j])
                pc = [pc_s[0], pc_s[1]]
                dzs = []
                for c in range(ncol):
                    hh = c % 2
                    one_minus_beta = jnp.exp(-sps[c])
                    dz = dls[c] * one_minus_beta - (pe[c][:, :128] + pc[hh]) * (1.0 - one_minus_beta)
                    pc[hh] = pc[hh] + pe[c][:, 128:]
                    dzs.append(dz.astype(BF16))
                pc_s[0] = pc[0]
                pc_s[1] = pc[1]
                dq_s[...] += jnp.dot(jnp.concatenate(dzs, axis=1), kbd, preferred_element_type=F32)
                by_head = lambda ts: jnp.concatenate([jnp.concatenate(ts[0::2], axis=1), jnp.concatenate(ts[1::2], axis=1)],
                                                     axis=0)
                rows = pl.ds(k0, nblk * ABLK)
                dk_s[rows, :] += lax.dot_general(by_head(dzs), q_st, tn, preferred_element_type=F32)
                dv_s[rows, :] += lax.dot_general(by_head(ws), do_st, tn, preferred_element_type=F32)

            gone = jnp.min(rs_ref[pl.ds(q0, ABLK), :], axis=0, keepdims=True) >= EXP_ZERO_AT
            lane1 = lax.broadcasted_iota(jnp.int32, (1, 128), 1)
            first = jnp.sum(jnp.where(gone[:, :128] & gone[:, 128:] & (lane1 < i), 1.0, 0.0)).astype(jnp.int32)
            i_low = jnp.where(i >= ATT_TOP, i - ATT_TOP, i)
            n_grp = i_low // ATT_GROUP
            rem = i_low - ATT_GROUP * n_grp
            for p in reversed(_pow2_below(ATT_GROUP)):
                @pl.when(((rem & p) != 0) & ((rem & (p - 1)) + p > first))
                def _():
                    step(rem & (p - 1), p, False)

            def inner(g, c2):
                step(rem + ATT_GROUP * g, ATT_GROUP, False)
                return c2

            lax.fori_loop(jnp.maximum(first - rem, 0) // ATT_GROUP, n_grp, inner, 0)

            @pl.when((i >= ATT_TOP) & (first < i))
            def _():
                step(i - ATT_TOP, ATT_TOP, False)

            step(i, 1, True)
            o_ref[0, pl.ds(q0, ABLK), :] = (dq_s[...] * scale).astype(BF16)
            return carry

        lax.fori_loop(0, nb, qblock, 0)
        o_ref[1] = dk_s[...].astype(BF16)
        o_ref[2] = dv_s[...].astype(BF16)
        rider.wait(pl.program_id(0) == n_pairs - 1)

    col = lambda o: (lambda p: (0, p + o))
    return pl.pallas_call(
        body, out_shape=[SDS((3, lp, n_pairs * 128), BF16)] + rider.out_shape, grid=(n_pairs,),
        in_specs=[pl.BlockSpec((lp, 128), col(0)), pl.BlockSpec((lp, 128), col(n_pairs)),
                  pl.BlockSpec((lp, 128), col(2 * n_pairs)), pl.BlockSpec((lp, 128), col(0)),
                  pl.BlockSpec((lp, 256), col(0)),
                  pl.BlockSpec((256, 128), lambda p: (0, 0)), pl.BlockSpec((256, 256), lambda p: (0, 0))]
        + rider.in_specs(),
        out_specs=[pl.BlockSpec((3, lp, 128), lambda p: (0, 0, p))] + rider.out_specs(),
        scratch_shapes=[pltpu.VMEM((lp, 128), F32), pltpu.VMEM((lp, 128), F32),
                        pltpu.VMEM((ABLK, 128), F32), pltpu.VMEM((2, ABLK, 128), F32)] + rider.scratch(),
        name=name, compiler_params=_cp())(qkv, qkv, qkv, d_out, rsave, tri_after, tri_before, *rider.srcs)


def _conv_fwd_dw(cacg, w, b, name):
    lp = cacg.shape[0]
    c = cacg.shape[1] // 2
    ncb = c // 128
    nchunk = lp // ABLK
    off = CONV_PAD - (CONV_K - 1)

    def body(a_ref, g_ref, w_ref, b_ref, y_ref, upad):
        upad[0:CONV_PAD, :] = jnp.zeros((CONV_PAD, 128), F32)

        def fill(ch, carry):
            base = pl.multiple_of(ch * ABLK, ABLK)
            upad[pl.ds(base + CONV_PAD, ABLK), :] = a_ref[pl.ds(base, ABLK), :] * _sigmoid(g_ref[pl.ds(base, ABLK), :])
            return carry

        lax.fori_loop(0, nchunk, fill, 0)

        def comp(ch, carry):
            base = pl.multiple_of(ch * ABLK, ABLK)
            acc = jnp.zeros((ABLK, 128), F32)
            for k in range(CONV_K):
                acc = acc + upad[pl.ds(base + (off + k), ABLK), :] * w_ref[k:k + 1, :]
            y_ref[pl.ds(base, ABLK), :] = acc + b_ref[...]
            return carry

        lax.fori_loop(0, nchunk, comp, 0)

    return pl.pallas_call(
        body, out_shape=SDS((lp, c), F32), grid=(ncb,),
        in_specs=[pl.BlockSpec((lp, 128), lambda j: (0, j)), pl.BlockSpec((lp, 128), lambda j: (0, j + ncb)),
                  pl.BlockSpec((CONV_PAD, 128), lambda j: (0, j)), pl.BlockSpec((1, 128), lambda j: (0, j))],
        out_specs=pl.BlockSpec((lp, 128), lambda j: (0, j)),
        scratch_shapes=[pltpu.VMEM((lp + CONV_PAD, 128), F32)], name=name, compiler_params=_cp())(cacg, cacg, w, b)


def _ln_parts(x, g, b):
    mu = jnp.mean(x, axis=-1, keepdims=True)
    xc = x - mu
    rstd = lax.rsqrt(jnp.mean(xc * xc, axis=-1, keepdims=True) + EPS)
    xh = xc * rstd
    return xh, rstd, xh * g + b


def _conv_fwd_ln(yc, g, b, name):
    lp, c = yc.shape
    tm = _tile(lp, 528, 16)

    def body(y_ref, g_ref, b_ref, o_ref):
        _, _, ln = _ln_parts(y_ref[...], g_ref[...], b_ref[...])
        o_ref[...] = (ln * _sigmoid(ln)).astype(BF16)

    return pl.pallas_call(
        body, out_shape=SDS((lp, c), BF16), grid=(lp // tm,),
        in_specs=[pl.BlockSpec((tm, c), lambda i: (i, 0)), pl.BlockSpec((1, c), lambda i: (0, 0)),
                  pl.BlockSpec((1, c), lambda i: (0, 0))],
        out_specs=pl.BlockSpec((tm, c), lambda i: (i, 0)), name=name, compiler_params=_cp())(yc, g, b)


def _conv_bwd_ln(yc, dout, g, b, name):
    lp, c = yc.shape
    tm = _tile(lp, 528, 16)

    def body(y_ref, d_ref, g_ref, b_ref, o_ref, dg_ref, db_ref):
        @pl.when(pl.program_id(0) == 0)
        def _():
            dg_ref[...] = jnp.zeros_like(dg_ref)
            db_ref[...] = jnp.zeros_like(db_ref)

        xh, rstd, ln = _ln_parts(y_ref[...], g_ref[...], b_ref[...])
        s = _sigmoid(ln)
        dln = d_ref[...] * (s * (1.0 + ln * (1.0 - s)))
        dg_ref[...] += _sum8(dln * xh)
        db_ref[...] += _sum8(dln)
        dxh = dln * g_ref[...]
        o_ref[...] = rstd * (dxh - jnp.mean(dxh, axis=-1, keepdims=True)
                             - xh * jnp.mean(dxh * xh, axis=-1, keepdims=True))

    return pl.pallas_call(
        body, out_shape=[SDS((lp, c), F32), SDS((8, c), F32), SDS((8, c), F32)], grid=(lp // tm,),
        in_specs=[pl.BlockSpec((tm, c), lambda i: (i, 0)), pl.BlockSpec((tm, c), lambda i: (i, 0)),
                  pl.BlockSpec((1, c), lambda i: (0, 0)), pl.BlockSpec((1, c), lambda i: (0, 0))],
        out_specs=[pl.BlockSpec((tm, c), lambda i: (i, 0)), pl.BlockSpec((8, c), lambda i: (0, 0)),
                   pl.BlockSpec((8, c), lambda i: (0, 0))],
        name=name, compiler_params=_cp())(yc, dout, g, b)


def _conv_bwd_dw(dyc, cacg, w, name):
    lp, c = dyc.shape
    ncb = c // 128
    nchunk = lp // ABLK
    off = CONV_PAD - (CONV_K - 1)

    def body(dy_ref, a_ref, g_ref, w_ref, dcc_ref, dw_ref, db_ref, upad, dypad, dwacc):
        upad[0:CONV_PAD, :] = jnp.zeros((CONV_PAD, 128), F32)
        dypad[lp:lp + CONV_PAD, :] = jnp.zeros((CONV_PAD, 128), F32)
        dwacc[...] = jnp.zeros_like(dwacc)
        db_ref[...] = jnp.zeros_like(db_ref)

        def fill(ch, carry):
            base = pl.multiple_of(ch * ABLK, ABLK)
            upad[pl.ds(base + CONV_PAD, ABLK), :] = a_ref[pl.ds(base, ABLK), :] * _sigmoid(g_ref[pl.ds(base, ABLK), :])
            dypad[pl.ds(base, ABLK), :] = dy_ref[pl.ds(base, ABLK), :]
            return carry

        lax.fori_loop(0, nchunk, fill, 0)

        def comp(ch, carry):
            base = pl.multiple_of(ch * ABLK, ABLK)
            dy = dy_ref[pl.ds(base, ABLK), :]
            du = jnp.zeros((ABLK, 128), F32)
            for k in range(CONV_K):
                du = du + dypad[pl.ds(base + (CONV_K - 1 - k), ABLK), :] * w_ref[k:k + 1, :]
                dwacc[k * 8:(k + 1) * 8, :] += _sum8(dy * upad[pl.ds(base + (off + k), ABLK), :])
            db_ref[...] += _sum8(dy)
            a = a_ref[pl.ds(base, ABLK), :]
            s = _sigmoid(g_ref[pl.ds(base, ABLK), :])
            dcc_ref[0, pl.ds(base, ABLK), :] = (du * s).astype(BF16)
            dcc_ref[1, pl.ds(base, ABLK), :] = (du * a * (s * (1.0 - s))).astype(BF16)
            return carry

        lax.fori_loop(0, nchunk, comp, 0)
        dw_ref[...] = dwacc[...].reshape(CONV_PAD, 8, 128).sum(axis=1)

    return pl.pallas_call(
        body, out_shape=[SDS((2, lp, c), BF16), SDS((CONV_PAD, c), F32), SDS((8, c), F32)],
        grid=(ncb,),
        in_specs=[pl.BlockSpec((lp, 128), lambda j: (0, j)), pl.BlockSpec((lp, 128), lambda j: (0, j)),
                  pl.BlockSpec((lp, 128), lambda j: (0, j + ncb)), pl.BlockSpec((CONV_PAD, 128), lambda j: (0, j))],
        out_specs=[pl.BlockSpec((2, lp, 128), lambda j: (0, 0, j)),
                   pl.BlockSpec((CONV_PAD, 128), lambda j: (0, j)), pl.BlockSpec((8, 128), lambda j: (0, j))],
        scratch_shapes=[pltpu.VMEM((lp + CONV_PAD, 128), F32), pltpu.VMEM((lp + CONV_PAD, 128), F32),
                        pltpu.VMEM((CONV_PAD * 8, 128), F32)],
        name=name, compiler_params=_cp())(dyc, cacg, cacg, w)


def _mesh_pos():
    x, y, c = lax.axis_index("x"), lax.axis_index("y"), lax.axis_index("c")
    return x, y, c


def _peer(pos, r):
    x, y, c = pos
    px = (1 - x) if (r >> 2) & 1 else x
    py = (1 - y) if (r >> 1) & 1 else y
    pc = (1 - c) if r & 1 else c
    return (px, py, pc), 4 * px + 2 * py + pc


class _Job:
    def __init__(self, src, dst, scatter, src_layer=None, dst_layer=None):
        self.src, self.dst, self.scatter, self.src_layer, self.dst_layer = src, dst, scatter, src_layer, dst_layer

    def src_view(self, ins, idx):
        v = ins[self.src] if self.src_layer is None else ins[self.src].at[self.src_layer]
        return v.at[idx] if self.scatter else v

    def dst_view(self, outs, slot):
        v = outs[self.dst] if self.dst_layer is None else outs[self.dst].at[self.dst_layer]
        return v.at[slot]


def _exchange_start(jobs, ins, outs, send, recv, loc):
    pos = _mesh_pos()
    me = 4 * pos[0] + 2 * pos[1] + pos[2]
    for j, job in enumerate(jobs):
        pltpu.make_async_copy(job.src_view(ins, me), job.dst_view(outs, me), loc.at[j]).start()
        for r in range(1, N_DEV):
            peer, peer_idx = _peer(pos, r)
            pltpu.make_async_remote_copy(src_ref=job.src_view(ins, peer_idx), dst_ref=job.dst_view(outs, me),
                                         send_sem=send.at[j, r - 1], recv_sem=recv.at[j, r - 1], device_id=peer,
                                         device_id_type=pl.DeviceIdType.MESH).start()


def _exchange_wait(jobs, ins, outs, send, recv, loc):
    pos = _mesh_pos()
    me = 4 * pos[0] + 2 * pos[1] + pos[2]
    for j, job in enumerate(jobs):
        for r in range(1, N_DEV):
            peer, peer_idx = _peer(pos, r)
            cp = pltpu.make_async_remote_copy(src_ref=job.src_view(ins, peer_idx), dst_ref=job.dst_view(outs, peer_idx),
                                              send_sem=send.at[j, r - 1], recv_sem=recv.at[j, r - 1], device_id=peer,
                                              device_id_type=pl.DeviceIdType.MESH)
            cp.wait_recv()
            cp.wait_send()
        pltpu.make_async_copy(job.src_view(ins, me), job.dst_view(outs, me), loc.at[j]).wait()


def _exchange_sems(n_jobs):
    return [pltpu.SemaphoreType.DMA((n_jobs, N_DEV - 1)), pltpu.SemaphoreType.DMA((n_jobs, N_DEV - 1)),
            pltpu.SemaphoreType.DMA((n_jobs,))]


def _exchange(jobs, arrs, out_shape, name):
    n_in, n_out = len(arrs), len(out_shape)
    any_spec = pl.BlockSpec(memory_space=pl.ANY)

    def body(*refs):
        ins, outs, sems = refs[:n_in], refs[n_in:n_in + n_out], refs[n_in + n_out:]
        _exchange_start(jobs, ins, outs, *sems)
        _exchange_wait(jobs, ins, outs, *sems)

    return pl.pallas_call(
        body, out_shape=out_shape, in_specs=[any_spec] * n_in, out_specs=[any_spec] * n_out,
        scratch_shapes=_exchange_sems(len(jobs)),
        name=name, compiler_params=pltpu.CompilerParams(has_side_effects=True))(*arrs)


def _all_reduce_small(p8, q, name):
    ni, _, w = p8.shape
    nq = q.shape[0]
    rows = ni + nq

    def body(p_ref, q_ref, o_ref, buf, send, recv):
        pos = _mesh_pos()
        me = 4 * pos[0] + 2 * pos[1] + pos[2]
        buf[me] = jnp.concatenate([p_ref[...].sum(axis=1), q_ref[...]], axis=0)
        for r in range(1, N_DEV):
            peer, _ = _peer(pos, r)
            pltpu.make_async_remote_copy(src_ref=buf.at[me], dst_ref=buf.at[me], send_sem=send.at[r - 1],
                                         recv_sem=recv.at[r - 1], device_id=peer,
                                         device_id_type=pl.DeviceIdType.MESH).start()
        for r in range(1, N_DEV):
            peer, peer_idx = _peer(pos, r)
            cp = pltpu.make_async_remote_copy(src_ref=buf.at[me], dst_ref=buf.at[peer_idx], send_sem=send.at[r - 1],
                                              recv_sem=recv.at[r - 1], device_id=peer,
                                              device_id_type=pl.DeviceIdType.MESH)
            cp.wait_recv()
            cp.wait_send()
        acc = buf[0]
        for dev in range(1, N_DEV):
            acc = acc + buf[dev]
        o_ref[...] = acc

    vmem = pl.BlockSpec(memory_space=pltpu.VMEM)
    return pl.pallas_call(
        body, out_shape=SDS((rows, w), F32), in_specs=[vmem, vmem], out_specs=vmem,
        scratch_shapes=[pltpu.VMEM((N_DEV, rows, w), F32), pltpu.SemaphoreType.DMA((N_DEV - 1,)),
                        pltpu.SemaphoreType.DMA((N_DEV - 1,))],
        name=name, compiler_params=pltpu.CompilerParams(has_side_effects=True))(p8, q)


def _adamw_math(w, g, m, v):
    m = ADAM_B1 * m + (1.0 - ADAM_B1) * g
    v = ADAM_B2 * v + (1.0 - ADAM_B2) * (g * g)
    m_hat = m / (1.0 - ADAM_B1 ** ADAM_STEP)
    v_hat = v / (1.0 - ADAM_B2 ** ADAM_STEP)
    delta = -ADAM_LR * (m_hat / (jnp.sqrt(v_hat) + ADAM_EPS) + ADAM_WD * w)
    return delta, m, v


def _adamw_shard(parts, w, m, v, name):
    depth = len(parts)
    _, rr, cc = parts[0].shape
    tr = _tile(rr, 256, 8)
    nt = rr // tr

    def body(*refs):
        p_refs = refs[:depth]
        w_ref, m_ref, v_ref, g_out, d_out, m_out, v_out = refs[depth:]
        for li in range(depth):
            @pl.when(pl.program_id(0) == li)
            def _(p_ref=p_refs[li]):
                g = p_ref[0].astype(F32)
                for dev in range(1, N_DEV):
                    g = g + p_ref[dev].astype(F32)
                delta, mm, vv = _adamw_math(w_ref[...], g, m_ref[...], v_ref[...])
                g_out[...] = g
                d_out[...] = delta
                m_out[...] = mm
                v_out[...] = vv

    def part_spec(li):
        return pl.BlockSpec((N_DEV, tr, cc), lambda l, i: (0, jnp.where(l == li, i, jnp.where(l < li, 0, nt - 1)), 0))

    blk = pl.BlockSpec((None, tr, cc), lambda l, i: (l, i, 0))
    return pl.pallas_call(
        body, out_shape=[SDS((depth, rr, cc), F32)] * 4, grid=(depth, nt),
        in_specs=[part_spec(li) for li in range(depth)] + [blk, blk, blk],
        out_specs=[blk] * 4, name=name, compiler_params=_cp())(*parts, w, m, v)


def _adamw_flat(g, w, m, v, name):
    def body(g_ref, w_ref, m_ref, v_ref, d_out, m_out, v_out):
        delta, mm, vv = _adamw_math(w_ref[...], g_ref[...], m_ref[...], v_ref[...])
        d_out[...] = delta
        m_out[...] = mm
        v_out[...] = vv

    return pl.pallas_call(body, out_shape=[SDS(g.shape, F32)] * 3, name=name, compiler_params=_cp())(g, w, m, v)


def _from_cols(t):
    return jnp.transpose(t, (1, 0, 2)).reshape(t.shape[1], N_DEV * t.shape[2])


def _swap(t):
    return jnp.swapaxes(t, -1, -2)


def kernel(x, meta_tokens, mix_norm_g, w_in, conv_dw_w, conv_dw_b, conv_ln_g, conv_ln_b, w_out, ffn_norm_g, w_gate, w_up, w_down, final_norm_g, loss_target, m_meta_tokens, m_mix_norm_g, m_w_in, m_conv_dw_w, m_conv_dw_b, m_conv_ln_g, m_conv_ln_b, m_w_out, m_ffn_norm_g, m_w_gate, m_w_up, m_w_down, m_final_norm_g, v_meta_tokens, v_mix_norm_g, v_w_in, v_conv_dw_w, v_conv_dw_b, v_conv_ln_g, v_conv_ln_b, v_w_out, v_ffn_norm_g, v_w_gate, v_w_up, v_w_down, v_final_norm_g):
    depth, d, in_shard = w_in.shape
    seq = x.shape[1]
    sb = N_HEADS * HEAD_DIM
    cc = conv_dw_w.shape[2] * N_DEV
    ff = w_gate.shape[2] * N_DEV
    assert in_shard * N_DEV == 3 * sb + 2 * cc and x.shape[0] == 1
    lr = N_META + seq
    lp = -(-lr // ABLK) * ABLK
    me = 4 * lax.axis_index("x") + 2 * lax.axis_index("y") + lax.axis_index("c")

    big_names = ("w_in", "w_out", "w_gate", "w_up", "w_down")
    transposed = {"w_in": True, "w_out": False, "w_gate": True, "w_up": True, "w_down": False}
    shard = dict(w_in=_swap(w_in).astype(BF16), w_out=w_out.astype(BF16), w_gate=_swap(w_gate).astype(BF16),
                 w_up=_swap(w_up).astype(BF16), w_down=w_down.astype(BF16))

    def gather_of(keys):
        names = sorted({n for n, _ in keys}, key=big_names.index)
        jobs = [_Job(names.index(n), j, False, src_layer=i) for j, (n, i) in enumerate(keys)]
        return jobs, [shard[n] for n in names], [SDS((N_DEV,) + shard[n].shape[1:], BF16) for n, _ in keys]

    first_keys = [("w_in", 0)]
    later_keys = [(n, i) for n in big_names for i in range(depth) if (n, i) not in first_keys]
    jobs, srcs, out_shape = gather_of(first_keys)
    for extra in (meta_tokens, conv_dw_w):
        jobs.append(_Job(len(srcs), len(out_shape), False))
        srcs.append(extra)
        out_shape.append(SDS((N_DEV,) + extra.shape, F32))
    gathered = _exchange(jobs, srcs, out_shape, "gather_first")
    wl = [dict() for _ in range(depth)]
    for (n, i), t in zip(first_keys, gathered):
        wl[i][n] = t.reshape(-1, d)
    meta_full = _from_cols(gathered[-2])
    taps = jnp.transpose(gathered[-1], (1, 2, 0, 3)).reshape(depth, CONV_K, cc)
    taps = jnp.pad(taps, ((0, 0), (0, CONV_PAD - CONV_K), (0, 0)))
    tri_fwd, tri_after, tri_before = _tri_consts()

    h = jnp.concatenate([meta_full, x[0], jnp.zeros((lp - lr, d), F32)], axis=0)
    saved = []
    for i in range(depth):
        p = wl[i]
        sv = dict(h_in=h)
        qkv, hn = _rms_mm(h, mix_norm_g[i:i + 1], p["w_in"], 0, 3 * sb, BF16, f"proj_qkv_{i}", True)
        cacg = _rms_mm(h, mix_norm_g[i:i + 1], p["w_in"], 3 * sb, 2 * cc, F32, f"proj_conv_{i}", False)[0]
        rider = _Rider(*gather_of(later_keys)) if i == 0 else None
        attn, rsave, *rest = _attn_fwd(qkv, tri_fwd, f"attn_fwd_{i}", rider)
        if i == 0:
            for (n, li), t in zip(later_keys, rest):
                wl[li][n] = t.reshape(-1, d)
        yc = _conv_fwd_dw(cacg, taps[i], conv_dw_b[i:i + 1], f"conv_fwd_dw_{i}")
        conv = _conv_fwd_ln(yc, conv_ln_g[i:i + 1], conv_ln_b[i:i + 1], f"conv_fwd_ln_{i}")
        h = _mix_out(attn, conv, p["w_out"], h, f"mix_out_{i}")
        sv.update(qkv=qkv, hn=hn, cacg=cacg, rsave=rsave, yc=yc, attn=attn, conv=conv, h_mid=h)
        h, hn2, act = _ffn_fwd(h, ffn_norm_g[i:i + 1], p["w_gate"], p["w_up"], p["w_down"], f"ffn_fwd_{i}")
        sv.update(hn2=hn2, act=act)
        saved.append(sv)

    tpad = jnp.pad(loss_target[0], ((N_META, lp - lr), (0, 0)))
    dh, dg_final, loss_part = _loss_head(h, tpad, final_norm_g.reshape(1, d), seq, "loss_head")
    loss = lax.psum(loss_part[0, 0], MESH_AXES)

    def scatter_of(items):
        jobs = [_Job(j, j, True) for j in range(len(items))]
        srcs = [t.reshape(N_DEV, t.shape[0] // N_DEV, d) for t in items]
        return jobs, srcs, [SDS(t.shape, BF16) for t in srcs]

    grads = [None] * depth
    pending, parts = [], {}
    for i in reversed(range(depth)):
        p, sv = wl[i], saved[i]
        d_g, d_u = _ffn_bwd_act(dh, sv["hn2"], p["w_gate"], p["w_up"], p["w_down"], f"ffn_bwd_act_{i}")
        gw_down = _mm_tn(sv["act"], dh, f"grad_w_down_{i}")
        gw_gate = _mm_tn(d_g, sv["hn2"], f"grad_w_gate_{i}")
        gw_up = _mm_tn(d_u, sv["hn2"], f"grad_w_up_{i}")
        pending += [(("w_gate", i), gw_gate), (("w_up", i), gw_up), (("w_down", i), gw_down)]
        dh, dg_ffn = _ffn_bwd_in(d_g, d_u, p["w_gate"], p["w_up"], sv["h_mid"], ffn_norm_g[i:i + 1], dh, f"ffn_bwd_in_{i}")
        gw_out = jnp.concatenate([_mm_tn(sv["attn"], dh, f"grad_w_out_attn_{i}"),
                                  _mm_tn(sv["conv"], dh, f"grad_w_out_conv_{i}")], axis=0)
        d_attn = _mm_nt(dh, p["w_out"], 0, sb, BF16, f"mix_bwd_attn_{i}")
        d_conv = _mm_nt(dh, p["w_out"], sb, cc, F32, f"mix_bwd_conv_{i}")
        dqkv, *arrived = _attn_bwd(sv["qkv"], d_attn, sv["rsave"], tri_after, tri_before, f"attn_bwd_{i}",
                                   _Rider(*scatter_of([t for _, t in pending])))
        parts.update({key: t for (key, _), t in zip(pending, arrived)})
        dyc, dg_ln, db_ln = _conv_bwd_ln(sv["yc"], d_conv, conv_ln_g[i:i + 1], conv_ln_b[i:i + 1], f"conv_bwd_ln_{i}")
        dcc, g_taps_i, db_conv = _conv_bwd_dw(dyc, sv["cacg"], taps[i], f"conv_bwd_dw_{i}")
        gw_in = _grad_w_in_t(dqkv, dcc, sv["hn"], f"grad_w_in_{i}")
        pending = [(("w_in", i), gw_in), (("w_out", i), gw_out)]
        dh, dg_mix = _mix_bwd_in(dqkv, dcc, p["w_in"], sv["h_in"], mix_norm_g[i:i + 1], dh, f"mix_bwd_in_{i}")
        grads[i] = dict(taps=g_taps_i, dg_mix=dg_mix, dg_ffn=dg_ffn, dg_ln=dg_ln, db_ln=db_ln, db_conv=db_conv)
    arrived = _exchange(*scatter_of([t for _, t in pending]), "scatter_last")
    parts.update({key: t for (key, _), t in zip(pending, arrived)})
    grad_x = dh[N_META:lr][None]

    wide = lambda key: jnp.concatenate([jnp.pad(grads[i][key], ((0, 0), (0, d // depth - cc))) for i in range(depth)], axis=1)
    assert depth * cc <= d and d % depth == 0
    p8 = jnp.stack([grads[i]["dg_mix"] for i in range(depth)] + [grads[i]["dg_ffn"] for i in range(depth)]
                   + [dg_final, wide("db_conv"), wide("dg_ln"), wide("db_ln")])
    small = _all_reduce_small(p8, jnp.concatenate([wide("taps"), dh[:N_META]], axis=0), "reduce_small")
    r0 = 2 * depth + 4
    g_mix = small[0:depth]
    g_ffn = small[depth:2 * depth]
    g_final = small[2 * depth]
    narrow = lambda row: jnp.stack([row[i * (d // depth):i * (d // depth) + cc] for i in range(depth)])
    g_cb, g_lg, g_lb = narrow(small[2 * depth + 1]), narrow(small[2 * depth + 2]), narrow(small[2 * depth + 3])
    g_taps_full = jnp.stack([small[r0:r0 + CONV_K, i * (d // depth):i * (d // depth) + cc] for i in range(depth)])
    csh = cc // N_DEV
    g_taps_own = lax.dynamic_slice_in_dim(g_taps_full, me * csh, csh, axis=2)
    g_meta_full = small[r0 + CONV_PAD:r0 + CONV_PAD + N_META]
    msh = d // N_DEV
    g_meta_own = lax.dynamic_slice_in_dim(g_meta_full, me * msh, msh, axis=1)

    small_g = [g_meta_own, g_mix, g_taps_own, g_cb, g_lg, g_lb, g_ffn, g_final]
    small_w = [meta_tokens, mix_norm_g, conv_dw_w, conv_dw_b, conv_ln_g, conv_ln_b, ffn_norm_g, final_norm_g]
    small_m = [m_meta_tokens, m_mix_norm_g, m_conv_dw_w, m_conv_dw_b, m_conv_ln_g, m_conv_ln_b, m_ffn_norm_g, m_final_norm_g]
    small_v = [v_meta_tokens, v_mix_norm_g, v_conv_dw_w, v_conv_dw_b, v_conv_ln_g, v_conv_ln_b, v_ffn_norm_g, v_final_norm_g]
    sizes = [int(math.prod(t.shape)) for t in small_w]
    total = sum(sizes)
    rows = -(-total // (8 * 128)) * 8

    def flat(ts):
        v = jnp.concatenate([t.reshape(-1) for t in ts])
        return jnp.pad(v, (0, rows * 128 - total)).reshape(rows, 128)

    def unflat(t):
        v, out, o = t.reshape(-1), [], 0
        for sz, ref in zip(sizes, small_w):
            out.append(v[o:o + sz].reshape(ref.shape))
            o += sz
        return out

    sd, sm, sv_ = _adamw_flat(flat(small_g), flat(small_w), flat(small_m), flat(small_v), "adamw_small")
    s_delta, s_m, s_v = unflat(sd), unflat(sm), unflat(sv_)

    big = []
    for n, w, m, v in zip(big_names, (w_in, w_out, w_gate, w_up, w_down), (m_w_in, m_w_out, m_w_gate, m_w_up, m_w_down),
                          (v_w_in, v_w_out, v_w_gate, v_w_up, v_w_down)):
        fix = _swap if transposed[n] else (lambda t: t)
        res = _adamw_shard([parts[(n, i)] for i in range(depth)], fix(w), fix(m), fix(v), f"adamw_{n}")
        big.append([fix(t) for t in res])
    b_in, b_out, b_gate, b_up, b_down = big

    def ordered(k, smalls):
        s_meta, s_mix, s_taps, s_cb, s_lg, s_lb, s_ffn, s_final = smalls
        return [s_meta, s_mix, b_in[k], s_taps, s_cb, s_lg, s_lb, b_out[k], s_ffn, b_gate[k], b_up[k], b_down[k], s_final]

    return (loss, grad_x, *ordered(0, small_g), *ordered(1, s_delta), *ordered(2, s_m), *ordered(3, s_v))
```

```python
import functools
import math

import jax
import jax.numpy as jnp
from jax import lax
from jax.experimental import pallas as pl
from jax.experimental.pallas import tpu as pltpu

F32 = jnp.float32
BF16 = jnp.bfloat16
SDS = jax.ShapeDtypeStruct

N_META = 16
N_HEADS = 8
HEAD_DIM = 64
CONV_K = 31
CONV_PAD = 32
ABLK = 128
ATT_GROUP = 4
ATT_SPLIT = 2
ATT_TOP = 2
EXP_ZERO_AT = 104.0
GONE = 1e30
MASKED = -1e30
FF_CHUNK = 256
EPS = 1e-6
N_DEV = 8
MESH_AXES = ("x", "y", "c")
ADAM_LR = 0.001
ADAM_B1 = 0.9
ADAM_B2 = 0.999
ADAM_EPS = 1e-08
ADAM_WD = 0.01
ADAM_STEP = 10
MIB = 1 << 20
VMEM_LIMIT_MIB = 48


def _cp():
    return pltpu.CompilerParams(vmem_limit_bytes=VMEM_LIMIT_MIB * MIB)


def _tile(n, cap, mult):
    best = None
    for t in range(mult, min(n, cap) + 1, mult):
        if n % t == 0:
            best = t
    assert best is not None, (n, cap, mult)
    return best


def _sum8(v):
    r, c = v.shape
    return v.reshape(r // 8, 8, c).sum(axis=0)


def _sigmoid(x):
    return 1.0 / (1.0 + jnp.exp(-x))


NT = (((1,), (1,)), ((), ()))
TN = (((0,), (0,)), ((), ()))


def _rms_mm(h, g, wt, w_row0, n_cols, out_dtype, name, hn_out):
    lp, d = h.shape
    tm = _tile(lp, 528, 16)
    tn = _tile(math.gcd(n_cols, w_row0), 512, 128)
    off = w_row0 // tn

    def body(h_ref, g_ref, w_ref, *rest):
        if hn_out:
            o_ref, hn_ref, hn_s = rest
        else:
            o_ref, hn_s = rest

        @pl.when(pl.program_id(1) == 0)
        def _():
            x = h_ref[...]
            r = lax.rsqrt(jnp.mean(x * x, axis=-1, keepdims=True) + EPS)
            hn = ((x * r) * g_ref[...]).astype(BF16)
            hn_s[...] = hn
            if hn_out:
                hn_ref[...] = hn

        o_ref[...] = lax.dot_general(hn_s[...], w_ref[...], NT, preferred_element_type=F32).astype(out_dtype)

    out_shape = [SDS((lp, n_cols), out_dtype)]
    out_specs = [pl.BlockSpec((tm, tn), lambda i, j: (i, j))]
    if hn_out:
        out_shape.append(SDS((lp, d), BF16))
        out_specs.append(pl.BlockSpec((tm, d), lambda i, j: (i, 0)))
    return pl.pallas_call(
        body, out_shape=out_shape, grid=(lp // tm, n_cols // tn),
        in_specs=[pl.BlockSpec((tm, d), lambda i, j: (i, 0)),
                  pl.BlockSpec((1, d), lambda i, j: (0, 0)),
                  pl.BlockSpec((tn, d), lambda i, j: (j + off, 0))],
        out_specs=out_specs, scratch_shapes=[pltpu.VMEM((tm, d), BF16)],
        name=name, compiler_params=_cp())(h, g, wt)


def _mm_nt(a, b, b_row0, n_out, out_dtype, name):
    m, k = a.shape
    tm = _tile(m, 528, 16)
    tn = _tile(math.gcd(n_out, b_row0), 512, 128)
    off = b_row0 // tn

    def body(a_ref, b_ref, o_ref):
        o_ref[...] = lax.dot_general(a_ref[...].astype(BF16), b_ref[...], NT,
                                     preferred_element_type=F32).astype(out_dtype)

    return pl.pallas_call(
        body, out_shape=SDS((m, n_out), out_dtype), grid=(m // tm, n_out // tn),
        in_specs=[pl.BlockSpec((tm, k), lambda i, j: (i, 0)), pl.BlockSpec((tn, k), lambda i, j: (j + off, 0))],
        out_specs=pl.BlockSpec((tm, tn), lambda i, j: (i, j)), name=name, compiler_params=_cp())(a, b)


def _mix_out(attn, conv, w, res, name):
    m, ka = attn.shape
    kc = conv.shape[1]
    n = w.shape[1]
    assert ka == kc
    tm = _tile(m, 528, 16)
    tn = _tile(n, 512, 128)

    def body(a_ref, c_ref, wa_ref, wc_ref, r_ref, o_ref):
        o_ref[...] = (r_ref[...] + jnp.dot(a_ref[...], wa_ref[...], preferred_element_type=F32)
                      + jnp.dot(c_ref[...], wc_ref[...], preferred_element_type=F32))

    return pl.pallas_call(
        body, out_shape=SDS((m, n), F32), grid=(m // tm, n // tn),
        in_specs=[pl.BlockSpec((tm, ka), lambda i, j: (i, 0)), pl.BlockSpec((tm, kc), lambda i, j: (i, 0)),
                  pl.BlockSpec((ka, tn), lambda i, j: (0, j)), pl.BlockSpec((kc, tn), lambda i, j: (1, j)),
                  pl.BlockSpec((tm, tn), lambda i, j: (i, j))],
        out_specs=pl.BlockSpec((tm, tn), lambda i, j: (i, j)), name=name, compiler_params=_cp())(attn, conv, w, w, res)


def _mm_tn(a, b, name):
    l, m = a.shape
    n = b.shape[1]
    tm = _tile(m, 1408, 128)
    tn = _tile(n, 1024, 128)
    tl = _tile(l, 512, 128)
    nl = l // tl

    def body(a_ref, b_ref, o_ref, acc):
        @pl.when(pl.program_id(2) == 0)
        def _():
            acc[...] = jnp.zeros_like(acc)

        acc[...] += lax.dot_general(a_ref[...].astype(BF16), b_ref[...].astype(BF16), TN, preferred_element_type=F32)

        @pl.when(pl.program_id(2) == nl - 1)
        def _():
            o_ref[...] = acc[...].astype(BF16)

    return pl.pallas_call(
        body, out_shape=SDS((m, n), BF16), grid=(m // tm, n // tn, nl),
        in_specs=[pl.BlockSpec((tl, tm), lambda i, j, s: (s, i)),
                  pl.BlockSpec((tl, tn), lambda i, j, s: (s, j))],
        out_specs=pl.BlockSpec((tm, tn), lambda i, j, s: (i, j)),
        scratch_shapes=[pltpu.VMEM((tm, tn), F32)], name=name, compiler_params=_cp())(a, b)


def _grad_w_in_t(dqkv, dcc, hn, name):
    nq, l, w = dqkv.shape
    nc = dcc.shape[0]
    d = hn.shape[1]
    assert dcc.shape[2] == w
    tn = _tile(d, 1024, 128)
    tl = _tile(l, 512, 128)
    nl = l // tl

    def body(q_ref, c_ref, b_ref, o_ref, acc):
        p, s = pl.program_id(0), pl.program_id(2)

        @pl.when(s == 0)
        def _():
            acc[...] = jnp.zeros_like(acc)

        @pl.when(p < nq)
        def _():
            acc[...] += lax.dot_general(q_ref[...], b_ref[...], TN, preferred_element_type=F32)

        @pl.when(p >= nq)
        def _():
            acc[...] += lax.dot_general(c_ref[...], b_ref[...], TN, preferred_element_type=F32)

        @pl.when(s == nl - 1)
        def _():
            o_ref[...] = acc[...].astype(BF16)

    out = pl.pallas_call(
        body, out_shape=SDS((nq + nc, w, d), BF16), grid=(nq + nc, d // tn, nl),
        in_specs=[pl.BlockSpec((None, tl, w), lambda p, j, s: (jnp.minimum(p, nq - 1), s, 0)),
                  pl.BlockSpec((None, tl, w), lambda p, j, s: (jnp.maximum(p - nq, 0), s, 0)),
                  pl.BlockSpec((tl, tn), lambda p, j, s: (s, j))],
        out_specs=pl.BlockSpec((None, w, tn), lambda p, j, s: (p, 0, j)),
        scratch_shapes=[pltpu.VMEM((w, tn), F32)], name=name, compiler_params=_cp())(dqkv, dcc, hn)
    return out.reshape((nq + nc) * w, d)


def _rms_bwd_tail(acc, h_ref, g_ref, dres_ref, o_ref, dg_ref):
    x = h_ref[...]
    r = lax.rsqrt(jnp.mean(x * x, axis=-1, keepdims=True) + EPS)
    xr = x * r
    dyv = acc[...]
    gy = dyv * g_ref[...]
    o_ref[...] = dres_ref[...] + r * (gy - xr * jnp.mean(gy * xr, axis=-1, keepdims=True))
    dg_ref[...] += _sum8(dyv * xr)


def _mix_bwd_in(dqkv, dcc, wt, h, g, dres, name, rider=None):
    nq, lp, w = dqkv.shape
    nc = dcc.shape[0]
    d = h.shape[1]
    tm = _tile(lp, 528, 16)
    npc = nq + nc
    ni = lp // tm
    rider = rider or _Rider()

    def body(q_ref, c_ref, w_ref, h_ref, g_ref, dres_ref, *rest):
        (o_ref, dg_ref), (acc,) = rider.split(rest, 2, 1)
        i, p = pl.program_id(0), pl.program_id(1)
        rider.start((i == 0) & (p == 0))

        @pl.when(p == 0)
        def _():
            acc[...] = jnp.zeros_like(acc)

        @pl.when((p == 0) & (i == 0))
        def _():
            dg_ref[...] = jnp.zeros_like(dg_ref)

        @pl.when(p < nq)
        def _():
            acc[...] += jnp.dot(q_ref[...], w_ref[...], preferred_element_type=F32)

        @pl.when(p >= nq)
        def _():
            acc[...] += jnp.dot(c_ref[...], w_ref[...], preferred_element_type=F32)

        @pl.when(p == npc - 1)
        def _():
            _rms_bwd_tail(acc, h_ref, g_ref, dres_ref, o_ref, dg_ref)

        rider.wait((i == ni - 1) & (p == npc - 1))

    return pl.pallas_call(
        body, out_shape=[SDS((lp, d), F32), SDS((8, d), F32)] + rider.out_shape, grid=(ni, npc),
        in_specs=[pl.BlockSpec((None, tm, w), lambda i, p: (jnp.minimum(p, nq - 1), i, 0)),
                  pl.BlockSpec((None, tm, w), lambda i, p: (jnp.maximum(p - nq, 0), i, 0)),
                  pl.BlockSpec((w, d), lambda i, p: (p, 0)),
                  pl.BlockSpec((tm, d), lambda i, p: (i, 0)),
                  pl.BlockSpec((1, d), lambda i, p: (0, 0)),
                  pl.BlockSpec((tm, d), lambda i, p: (i, 0))] + rider.in_specs(),
        out_specs=[pl.BlockSpec((tm, d), lambda i, p: (i, 0)), pl.BlockSpec((8, d), lambda i, p: (0, 0))]
        + rider.out_specs(),
        scratch_shapes=[pltpu.VMEM((tm, d), F32)] + rider.scratch(), name=name,
        compiler_params=_cp())(dqkv, dcc, wt, h, g, dres, *rider.srcs)


def _ffn_bwd_in(d_g, d_u, wg_t, wu_t, h, g, dres, name, rider=None):
    lp, f = d_g.shape
    d = h.shape[1]
    tm = _tile(lp, 528, 16)
    tk = _tile(f, 512, 128)
    nk = f // tk
    ni = lp // tm
    rider = rider or _Rider()

    def body(dg_in, du_in, wg_ref, wu_ref, h_ref, g_ref, dres_ref, *rest):
        (o_ref, dg_ref), (acc,) = rider.split(rest, 2, 1)
        i, kk = pl.program_id(0), pl.program_id(1)
        rider.start((i == 0) & (kk == 0))

        @pl.when(kk == 0)
        def _():
            acc[...] = jnp.zeros_like(acc)

        @pl.when((kk == 0) & (i == 0))
        def _():
            dg_ref[...] = jnp.zeros_like(dg_ref)

        acc[...] += (jnp.dot(dg_in[...], wg_ref[...], preferred_element_type=F32)
                     + jnp.dot(du_in[...], wu_ref[...], preferred_element_type=F32))

        @pl.when(kk == nk - 1)
        def _():
            _rms_bwd_tail(acc, h_ref, g_ref, dres_ref, o_ref, dg_ref)

        rider.wait((i == ni - 1) & (kk == nk - 1))

    return pl.pallas_call(
        body, out_shape=[SDS((lp, d), F32), SDS((8, d), F32)] + rider.out_shape, grid=(ni, nk),
        in_specs=[pl.BlockSpec((tm, tk), lambda i, j: (i, j)), pl.BlockSpec((tm, tk), lambda i, j: (i, j)),
                  pl.BlockSpec((tk, d), lambda i, j: (j, 0)), pl.BlockSpec((tk, d), lambda i, j: (j, 0)),
                  pl.BlockSpec((tm, d), lambda i, j: (i, 0)),
                  pl.BlockSpec((1, d), lambda i, j: (0, 0)),
                  pl.BlockSpec((tm, d), lambda i, j: (i, 0))] + rider.in_specs(),
        out_specs=[pl.BlockSpec((tm, d), lambda i, j: (i, 0)), pl.BlockSpec((8, d), lambda i, j: (0, 0))]
        + rider.out_specs(),
        scratch_shapes=[pltpu.VMEM((tm, d), F32)] + rider.scratch(), name=name,
        compiler_params=_cp())(d_g, d_u, wg_t, wu_t, h, g, dres, *rider.srcs)


def _ffn_fwd(h, g, wg_t, wu_t, w_d, name, rider=None):
    lp, d = h.shape
    f = w_d.shape[0]
    tm = _tile(lp, 528, 16)
    nc = f // FF_CHUNK
    ni = lp // tm
    rider = rider or _Rider()

    def body(h_ref, g_ref, wg_ref, wu_ref, wd_ref, *rest):
        (o_ref, hn_ref, act_ref), (hn_s, acc) = rider.split(rest, 3, 2)
        c = pl.program_id(1)
        rider.start((pl.program_id(0) == 0) & (c == 0))

        @pl.when(c == 0)
        def _():
            x = h_ref[...]
            r = lax.rsqrt(jnp.mean(x * x, axis=-1, keepdims=True) + EPS)
            hn = ((x * r) * g_ref[...]).astype(BF16)
            hn_s[...] = hn
            hn_ref[...] = hn
            acc[...] = jnp.zeros_like(acc)

        gg = lax.dot_general(hn_s[...], wg_ref[...], NT, preferred_element_type=F32)
        uu = lax.dot_general(hn_s[...], wu_ref[...], NT, preferred_element_type=F32)
        act = ((gg * _sigmoid(gg)) * uu).astype(BF16)
        act_ref[...] = act
        acc[...] += jnp.dot(act, wd_ref[...], preferred_element_type=F32)

        @pl.when(c == nc - 1)
        def _():
            o_ref[...] = h_ref[...] + acc[...]

        rider.wait((pl.program_id(0) == ni - 1) & (c == nc - 1))

    chunk = pl.BlockSpec((FF_CHUNK, d), lambda i, j: (j, 0))
    return pl.pallas_call(
        body, out_shape=[SDS((lp, d), F32), SDS((lp, d), BF16), SDS((lp, f), BF16)] + rider.out_shape, grid=(ni, nc),
        in_specs=[pl.BlockSpec((tm, d), lambda i, j: (i, 0)), pl.BlockSpec((1, d), lambda i, j: (0, 0)),
                  chunk, chunk, chunk] + rider.in_specs(),
        out_specs=[pl.BlockSpec((tm, d), lambda i, j: (i, 0)),
                   pl.BlockSpec((tm, d), lambda i, j: (i, 0)),
                   pl.BlockSpec((tm, FF_CHUNK), lambda i, j: (i, j))] + rider.out_specs(),
        scratch_shapes=[pltpu.VMEM((tm, d), BF16), pltpu.VMEM((tm, d), F32)] + rider.scratch(),
        name=name, compiler_params=_cp())(h, g, wg_t, wu_t, w_d, *rider.srcs)


def _ffn_bwd_act(dh, hn, wg_t, wu_t, w_d, name, rider=None):
    lp, d = dh.shape
    f = w_d.shape[0]
    tm = _tile(lp, 528, 16)
    nc = f // FF_CHUNK
    ni = lp // tm
    rider = rider or _Rider()

    def body(dh_ref, hn_ref, wg_ref, wu_ref, wd_ref, *rest):
        (dg_out, du_out), (dh_s,) = rider.split(rest, 2, 1)
        rider.start((pl.program_id(0) == 0) & (pl.program_id(1) == 0))

        @pl.when(pl.program_id(1) == 0)
        def _():
            dh_s[...] = dh_ref[...].astype(BF16)

        gg = lax.dot_general(hn_ref[...], wg_ref[...], NT, preferred_element_type=F32)
        uu = lax.dot_general(hn_ref[...], wu_ref[...], NT, preferred_element_type=F32)
        dact = lax.dot_general(dh_s[...], wd_ref[...], NT, preferred_element_type=F32)
        s = _sigmoid(gg)
        dg_out[...] = (dact * uu * (s * (1.0 + gg * (1.0 - s)))).astype(BF16)
        du_out[...] = (dact * (gg * s)).astype(BF16)
        rider.wait((pl.program_id(0) == ni - 1) & (pl.program_id(1) == nc - 1))

    chunk = pl.BlockSpec((FF_CHUNK, d), lambda i, j: (j, 0))
    out = pl.BlockSpec((tm, FF_CHUNK), lambda i, j: (i, j))
    return pl.pallas_call(
        body, out_shape=[SDS((lp, f), BF16), SDS((lp, f), BF16)] + rider.out_shape, grid=(ni, nc),
        in_specs=[pl.BlockSpec((tm, d), lambda i, j: (i, 0)), pl.BlockSpec((tm, d), lambda i, j: (i, 0)),
                  chunk, chunk, chunk] + rider.in_specs(),
        out_specs=[out, out] + rider.out_specs(), scratch_shapes=[pltpu.VMEM((tm, d), BF16)] + rider.scratch(),
        name=name, compiler_params=_cp())(dh, hn, wg_t, wu_t, w_d, *rider.srcs)


def _loss_head(h, tpad, g, n_real, name):
    lp, d = h.shape
    tm = _tile(lp, 528, 16)

    def body(h_ref, t_ref, g_ref, dh_ref, dg_ref, loss_ref):
        i = pl.program_id(0)

        @pl.when(i == 0)
        def _():
            dg_ref[...] = jnp.zeros_like(dg_ref)
            loss_ref[...] = jnp.zeros_like(loss_ref)

        x = h_ref[...]
        r = lax.rsqrt(jnp.mean(x * x, axis=-1, keepdims=True) + EPS)
        xr = x * r
        y = xr * g_ref[...]
        row = i * tm + lax.broadcasted_iota(jnp.int32, (tm, d), 0)
        valid = (row >= N_META) & (row < N_META + n_real)
        diff = jnp.where(valid, y - t_ref[...], 0.0)
        loss_ref[...] += jnp.sum(diff * diff) * (0.5 / d)
        dy = diff * (1.0 / d)
        gy = dy * g_ref[...]
        dh_ref[...] = r * (gy - xr * jnp.mean(gy * xr, axis=-1, keepdims=True))
        dg_ref[...] += _sum8(dy * xr)

    return pl.pallas_call(
        body, out_shape=[SDS((lp, d), F32), SDS((8, d), F32), SDS((8, 128), F32)], grid=(lp // tm,),
        in_specs=[pl.BlockSpec((tm, d), lambda i: (i, 0)),
                  pl.BlockSpec((tm, d), lambda i: (i, 0)),
                  pl.BlockSpec((1, d), lambda i: (0, 0))],
        out_specs=[pl.BlockSpec((tm, d), lambda i: (i, 0)),
                   pl.BlockSpec((8, d), lambda i: (0, 0)),
                   pl.BlockSpec((8, 128), lambda i: (0, 0))],
        name=name, compiler_params=_cp())(h, tpad, g)


def _tri_consts():
    j = lax.broadcasted_iota(jnp.int32, (ABLK, ABLK), 0)
    s = lax.broadcasted_iota(jnp.int32, (ABLK, ABLK), 1)
    after = (j >= s).astype(BF16)
    before = (j < s).astype(BF16)
    ones = jnp.ones((ABLK, ABLK), BF16)
    two = lambda t: jnp.concatenate([t, t], axis=0)
    return (two(jnp.concatenate([after, ones], axis=1)),
            two(after),
            two(jnp.concatenate([before, ones], axis=1)))


def _softplus(z):
    neg_abs = lax.bitcast_convert_type(lax.bitcast_convert_type(z, jnp.uint32) | jnp.uint32(0x80000000), F32)
    return jnp.log(1.0 + jnp.exp(neg_abs)) + jnp.maximum(z, 0.0)


def _split_hi_lo(m):
    hi = m.astype(BF16)
    lo = (m - hi.astype(F32)).astype(BF16)
    return jnp.concatenate([hi, lo], axis=1)


def _head_halves(t2, in_a):
    zero = jnp.zeros_like(t2)
    return jnp.concatenate([jnp.where(in_a, t2, zero), jnp.where(in_a, zero, t2)], axis=0)


def _stack_blocks(t, nblk, in_a):
    return jnp.concatenate([_head_halves(t[u * ABLK:(u + 1) * ABLK], in_a) for u in range(nblk)], axis=0)


def _attn_scale():
    scale = 1.0 / math.sqrt(HEAD_DIM)
    assert math.frexp(scale)[0] == 0.5, "a power of two, so that scaling q in bf16 is exact"
    return scale


def _pow2_below(n):
    assert n & (n - 1) == 0
    return [p for p in (64, 32, 16, 8, 4, 2, 1) if p < n]


class _Rider:
    def __init__(self, jobs=(), srcs=(), out_shape=()):
        self.jobs, self.srcs, self.out_shape = list(jobs), list(srcs), list(out_shape)
        self.any = [pl.BlockSpec(memory_space=pl.ANY)]

    def split(self, rest, n_out, n_scratch):
        ni, no = len(self.srcs), len(self.out_shape)
        self.ins, outs = rest[:ni], rest[ni:ni + n_out]
        self.outs = rest[ni + n_out:ni + n_out + no]
        scratch = rest[ni + n_out + no:ni + n_out + no + n_scratch]
        self.sems = rest[ni + n_out + no + n_scratch:]
        return outs, scratch

    def start(self, first):
        if self.jobs:
            @pl.when(first)
            def _():
                _exchange_start(self.jobs, self.ins, self.outs, *self.sems)

    def wait(self, last):
        if self.jobs:
            @pl.when(last)
            def _():
                _exchange_wait(self.jobs, self.ins, self.outs, *self.sems)

    def in_specs(self):
        return self.any * len(self.srcs)

    def out_specs(self):
        return self.any * len(self.out_shape)

    def scratch(self):
        return _exchange_sems(len(self.jobs)) if self.jobs else []


def _attn_fwd(qkv, tri_fwd, name, rider=None):
    lp = qkv.shape[0]
    n_pairs = (N_HEADS * HEAD_DIM) // 128
    nb = lp // ABLK
    assert nb <= 128 and 2 * HEAD_DIM == 128
    scale = _attn_scale()
    nt = (((1,), (1,)), ((), ()))
    rider = rider or _Rider()

    def body(q_ref, k_ref, v_ref, tri_ref, *rest):
        (o_ref, rs_ref), (r_s, acc_s, rs_s) = rider.split(rest, 2, 3)
        rider.start(pl.program_id(0) == 0)
        lane = lax.broadcasted_iota(jnp.int32, (ABLK, 128), 1)
        row = lax.broadcasted_iota(jnp.int32, (ABLK, 128), 0)
        in_a = lane < HEAD_DIM
        causal = lane < row

        def qblock(i, carry):
            q0 = pl.multiple_of(i * ABLK, ABLK)
            q2 = q_ref[pl.ds(q0, ABLK), :] * scale
            r_s[...] = jnp.zeros_like(r_s)
            acc_s[...] = jnp.zeros_like(acc_s)
            rs_s[...] = jnp.full_like(rs_s, GONE)

            def live():
                return (jnp.min(jnp.minimum(r_s[0], r_s[1])) < EXP_ZERO_AT).astype(jnp.int32)

            def step(kb0, nblk, diag):
                k0 = pl.multiple_of(kb0 * ABLK, ABLK)
                kbd = _stack_blocks(k_ref[pl.ds(k0, nblk * ABLK), :], nblk, in_a)
                vbd = _stack_blocks(v_ref[pl.ds(k0, nblk * ABLK), :], nblk, in_a)
                z = lax.dot_general(q2, kbd, nt, preferred_element_type=F32)
                ncol = 2 * nblk
                zt = [z[:, c * 128:(c + 1) * 128] for c in range(ncol)]
                if diag:
                    zt = [jnp.where(causal, t, MASKED) for t in zt]
                per = max(1, ncol // ATT_SPLIT)
                ce = [None] * ncol
                for c1 in range(ncol, 0, -per):
                    parts = []
                    for c in range(c1 - per, c1):
                        parts.append(_split_hi_lo(_softplus(zt[c])))
                    got = jnp.dot(jnp.concatenate(parts, axis=0), tri_ref[...], preferred_element_type=F32)
                    for c in range(c1 - per, c1):
                        ce[c] = got[(c - c1 + per) * 128:(c - c1 + per + 1) * 128]
                rr = [r_s[0], r_s[1]]
                rsv = [rs_s[:, :128], rs_s[:, 128:]]
                ws = [None] * ncol
                for u in reversed(range(nblk)):
                    for hh in range(2):
                        c = 2 * u + hh
                        ws[c] = jnp.exp(zt[c] - ce[c][:, :128] - rr[hh]).astype(BF16)
                        rsv[hh] = jnp.where(lane == kb0 + u, rr[hh], rsv[hh])
                        rr[hh] = rr[hh] + ce[c][:, 128:]
                acc_s[...] += jnp.dot(jnp.concatenate(ws, axis=1), vbd, preferred_element_type=F32)
                r_s[0] = rr[0]
                r_s[1] = rr[1]
                rs_s[:, :128] = rsv[0]
                rs_s[:, 128:] = rsv[1]

            step(i, 1, True)

            @pl.when(i >= ATT_TOP)
            def _():
                step(i - ATT_TOP, ATT_TOP, False)

            i_low = jnp.where(i >= ATT_TOP, i - ATT_TOP, i)
            n_grp = i_low // ATT_GROUP

            def more(c):
                return (c[0] < n_grp) & (c[1] > 0)

            def inner(c):
                step(i_low - ATT_GROUP * (c[0] + 1), ATT_GROUP, False)
                return c[0] + 1, live()

            lax.while_loop(more, inner, (jnp.int32(0), live()))
            rem = i_low - ATT_GROUP * n_grp
            for p in _pow2_below(ATT_GROUP):
                @pl.when(((rem & p) != 0) & (live() > 0))
                def _():
                    step(rem & (p - 1), p, False)

            o_ref[pl.ds(q0, ABLK), :] = acc_s[...].astype(BF16)
            rs_ref[pl.ds(q0, ABLK), :] = rs_s[...]
            return carry

        lax.fori_loop(0, nb, qblock, 0)
        rider.wait(pl.program_id(0) == n_pairs - 1)

    col = lambda o: (lambda p: (0, p + o))
    return pl.pallas_call(
        body, out_shape=[SDS((lp, n_pairs * 128), BF16), SDS((lp, n_pairs * 256), F32)] + rider.out_shape,
        grid=(n_pairs,),
        in_specs=[pl.BlockSpec((lp, 128), col(0)), pl.BlockSpec((lp, 128), col(n_pairs)),
                  pl.BlockSpec((lp, 128), col(2 * n_pairs)), pl.BlockSpec((256, 256), lambda p: (0, 0))]
        + rider.in_specs(),
        out_specs=[pl.BlockSpec((lp, 128), col(0)), pl.BlockSpec((lp, 256), col(0))] + rider.out_specs(),
        scratch_shapes=[pltpu.VMEM((2, ABLK, 128), F32), pltpu.VMEM((ABLK, 128), F32), pltpu.VMEM((ABLK, 256), F32)]
        + rider.scratch(),
        name=name, compiler_params=_cp())(qkv, qkv, qkv, tri_fwd, *rider.srcs)


def _attn_bwd(qkv, d_out, rsave, tri_after, tri_before, name, rider=None):
    lp = qkv.shape[0]
    n_pairs = (N_HEADS * HEAD_DIM) // 128
    nb = lp // ABLK
    scale = _attn_scale()
    nt = (((1,), (1,)), ((), ()))
    tn = (((0,), (0,)), ((), ()))
    rider = rider or _Rider()

    def body(q_ref, k_ref, v_ref, do_ref, rs_ref, ta_ref, tb_ref, *rest):
        (o_ref,), (dk_s, dv_s, dq_s, pc_s) = rider.split(rest, 1, 4)
        rider.start(pl.program_id(0) == 0)
        lane = lax.broadcasted_iota(jnp.int32, (ABLK, 128), 1)
        row = lax.broadcasted_iota(jnp.int32, (ABLK, 128), 0)
        in_a = lane < HEAD_DIM
        causal = lane < row
        dk_s[...] = jnp.zeros_like(dk_s)
        dv_s[...] = jnp.zeros_like(dv_s)

        def qblock(i, carry):
            q0 = pl.multiple_of(i * ABLK, ABLK)
            q2 = q_ref[pl.ds(q0, ABLK), :] * scale
            do2 = do_ref[pl.ds(q0, ABLK), :]
            q_st = _head_halves(q2, in_a)
            do_st = _head_halves(do2, in_a)
            dq_s[...] = jnp.zeros_like(dq_s)
            pc_s[...] = jnp.zeros_like(pc_s)

            def step(kb0, nblk, diag):
                k0 = pl.multiple_of(kb0 * ABLK, ABLK)
                kbd = _stack_blocks(k_ref[pl.ds(k0, nblk * ABLK), :], nblk, in_a)
                vbd = _stack_blocks(v_ref[pl.ds(k0, nblk * ABLK), :], nblk, in_a)
                z = lax.dot_general(q2, kbd, nt, preferred_element_type=F32)
                dw = lax.dot_general(do2, vbd, nt, preferred_element_type=F32)
                ncol = 2 * nblk
                zt = [z[:, c * 128:(c + 1) * 128] for c in range(ncol)]
                if diag:
                    zt = [jnp.where(causal, t, MASKED) for t in zt]
                per = max(1, ncol // ATT_SPLIT)
                batches = [range(c0, c0 + per) for c0 in range(0, ncol, per)]
                sps, ex, ws, dls, pe = [None] * ncol, [None] * ncol, [None] * ncol, [None] * ncol, [None] * ncol

                def mass(cols):
                    for c in cols:
                        sps[c] = _softplus(zt[c])
                    got = jnp.dot(jnp.concatenate([_split_hi_lo(sps[c]) for c in cols], axis=0), ta_ref[...],
                                  preferred_element_type=F32)
                    for j, c in enumerate(cols):
                        ex[c] = got[j * 128:(j + 1) * 128]

                def weights(cols):
                    for c in cols:
                        u, hh = c // 2, c % 2
                        r_saved = jnp.sum(jnp.where(lane == kb0 + u, rs_ref[pl.ds(q0, ABLK), hh * 128:(hh + 1) * 128],
                                                    0.0), axis=1, keepdims=True)
                        w = jnp.exp(zt[c] - ex[c] - r_saved)
                        ws[c] = w.astype(BF16)
                        dls[c] = dw[:, c * 128:(c + 1) * 128] * w
                    got = jnp.dot(jnp.concatenate([_split_hi_lo(dls[c]) for c in cols], axis=0), tb_ref[...],
                                  preferred_element_type=F32)
                    for j, c in enumerate(cols):
                        pe[c] = got[j * 128:(j + 1) * 128]

                mass(batches[0])
                for j in range(len(batches)):
                    if j + 1 < len(batches):
                        mass(batches[j + 1])
                    weights(batches[j])
                pc = [pc_s[0], pc_s[1]]
                dzs = []
                for c in range(ncol):
                    hh = c % 2
                    one_minus_beta = jnp.exp(-sps[c])
                    dz = dls[c] * one_minus_beta - (pe[c][:, :128] + pc[hh]) * (1.0 - one_minus_beta)
                    pc[hh] = pc[hh] + pe[c][:, 128:]
                    dzs.append(dz.astype(BF16))
                pc_s[0] = pc[0]
                pc_s[1] = pc[1]
                dq_s[...] += jnp.dot(jnp.concatenate(dzs, axis=1), kbd, preferred_element_type=F32)
                by_head = lambda ts: jnp.concatenate([jnp.concatenate(ts[0::2], axis=1), jnp.concatenate(ts[1::2], axis=1)],
                                                     axis=0)
                rows = pl.ds(k0, nblk * ABLK)
                dk_s[rows, :] += lax.dot_general(by_head(dzs), q_st, tn, preferred_element_type=F32)
                dv_s[rows, :] += lax.dot_general(by_head(ws), do_st, tn, preferred_element_type=F32)

            gone = jnp.min(rs_ref[pl.ds(q0, ABLK), :], axis=0, keepdims=True) >= EXP_ZERO_AT
            lane1 = lax.broadcasted_iota(jnp.int32, (1, 128), 1)
            first = jnp.sum(jnp.where(gone[:, :128] & gone[:, 128:] & (lane1 < i), 1.0, 0.0)).astype(jnp.int32)
            i_low = jnp.where(i >= ATT_TOP, i - ATT_TOP, i)
            n_grp = i_low // ATT_GROUP
            rem = i_low - ATT_GROUP * n_grp
            for p in reversed(_pow2_below(ATT_GROUP)):
                @pl.when(((rem & p) != 0) & ((rem & (p - 1)) + p > first))
                def _():
                    step(rem & (p - 1), p, False)

            def inner(g, c2):
                step(rem + ATT_GROUP * g, ATT_GROUP, False)
                return c2

            lax.fori_loop(jnp.maximum(first - rem, 0) // ATT_GROUP, n_grp, inner, 0)

            @pl.when((i >= ATT_TOP) & (first < i))
            def _():
                step(i - ATT_TOP, ATT_TOP, False)

            step(i, 1, True)
            o_ref[0, pl.ds(q0, ABLK), :] = (dq_s[...] * scale).astype(BF16)
            return carry

        lax.fori_loop(0, nb, qblock, 0)
        o_ref[1] = dk_s[...].astype(BF16)
        o_ref[2] = dv_s[...].astype(BF16)
        rider.wait(pl.program_id(0) == n_pairs - 1)

    col = lambda o: (lambda p: (0, p + o))
    return pl.pallas_call(
        body, out_shape=[SDS((3, lp, n_pairs * 128), BF16)] + rider.out_shape, grid=(n_pairs,),
        in_specs=[pl.BlockSpec((lp, 128), col(0)), pl.BlockSpec((lp, 128), col(n_pairs)),
                  pl.BlockSpec((lp, 128), col(2 * n_pairs)), pl.BlockSpec((lp, 128), col(0)),
                  pl.BlockSpec((lp, 256), col(0)),
                  pl.BlockSpec((256, 128), lambda p: (0, 0)), pl.BlockSpec((256, 256), lambda p: (0, 0))]
        + rider.in_specs(),
        out_specs=[pl.BlockSpec((3, lp, 128), lambda p: (0, 0, p))] + rider.out_specs(),
        scratch_shapes=[pltpu.VMEM((lp, 128), F32), pltpu.VMEM((lp, 128), F32),
                        pltpu.VMEM((ABLK, 128), F32), pltpu.VMEM((2, ABLK, 128), F32)] + rider.scratch(),
        name=name, compiler_params=_cp())(qkv, qkv, qkv, d_out, rsave, tri_after, tri_before, *rider.srcs)


def _conv_fwd_dw(cacg, w, b, name):
    lp = cacg.shape[0]
    c = cacg.shape[1] // 2
    ncb = c // 128
    nchunk = lp // ABLK
    off = CONV_PAD - (CONV_K - 1)

    def body(a_ref, g_ref, w_ref, b_ref, y_ref, upad):
        upad[0:CONV_PAD, :] = jnp.zeros((CONV_PAD, 128), F32)

        def fill(ch, carry):
            base = pl.multiple_of(ch * ABLK, ABLK)
            upad[pl.ds(base + CONV_PAD, ABLK), :] = a_ref[pl.ds(base, ABLK), :] * _sigmoid(g_ref[pl.ds(base, ABLK), :])
            return carry

        lax.fori_loop(0, nchunk, fill, 0)

        def comp(ch, carry):
            base = pl.multiple_of(ch * ABLK, ABLK)
            acc = jnp.zeros((ABLK, 128), F32)
            for k in range(CONV_K):
                acc = acc + upad[pl.ds(base + (off + k), ABLK), :] * w_ref[k:k + 1, :]
            y_ref[pl.ds(base, ABLK), :] = acc + b_ref[...]
            return carry

        lax.fori_loop(0, nchunk, comp, 0)

    return pl.pallas_call(
        body, out_shape=SDS((lp, c), F32), grid=(ncb,),
        in_specs=[pl.BlockSpec((lp, 128), lambda j: (0, j)), pl.BlockSpec((lp, 128), lambda j: (0, j + ncb)),
                  pl.BlockSpec((CONV_PAD, 128), lambda j: (0, j)), pl.BlockSpec((1, 128), lambda j: (0, j))],
        out_specs=pl.BlockSpec((lp, 128), lambda j: (0, j)),
        scratch_shapes=[pltpu.VMEM((lp + CONV_PAD, 128), F32)], name=name, compiler_params=_cp())(cacg, cacg, w, b)


def _ln_parts(x, g, b):
    mu = jnp.mean(x, axis=-1, keepdims=True)
    xc = x - mu
    rstd = lax.rsqrt(jnp.mean(xc * xc, axis=-1, keepdims=True) + EPS)
    xh = xc * rstd
    return xh, rstd, xh * g + b


def _conv_fwd_ln(yc, g, b, name):
    lp, c = yc.shape
    tm = _tile(lp, 528, 16)

    def body(y_ref, g_ref, b_ref, o_ref):
        _, _, ln = _ln_parts(y_ref[...], g_ref[...], b_ref[...])
        o_ref[...] = (ln * _sigmoid(ln)).astype(BF16)

    return pl.pallas_call(
        body, out_shape=SDS((lp, c), BF16), grid=(lp // tm,),
        in_specs=[pl.BlockSpec((tm, c), lambda i: (i, 0)), pl.BlockSpec((1, c), lambda i: (0, 0)),
                  pl.BlockSpec((1, c), lambda i: (0, 0))],
        out_specs=pl.BlockSpec((tm, c), lambda i: (i, 0)), name=name, compiler_params=_cp())(yc, g, b)


def _conv_bwd_ln(yc, dout, g, b, name):
    lp, c = yc.shape
    tm = _tile(lp, 528, 16)

    def body(y_ref, d_ref, g_ref, b_ref, o_ref, dg_ref, db_ref):
        @pl.when(pl.program_id(0) == 0)
        def _():
            dg_ref[...] = jnp.zeros_like(dg_ref)
            db_ref[...] = jnp.zeros_like(db_ref)

        xh, rstd, ln = _ln_parts(y_ref[...], g_ref[...], b_ref[...])
        s = _sigmoid(ln)
        dln = d_ref[...] * (s * (1.0 + ln * (1.0 - s)))
        dg_ref[...] += _sum8(dln * xh)
        db_ref[...] += _sum8(dln)
        dxh = dln * g_ref[...]
        o_ref[...] = rstd * (dxh - jnp.mean(dxh, axis=-1, keepdims=True)
                             - xh * jnp.mean(dxh * xh, axis=-1, keepdims=True))

    return pl.pallas_call(
        body, out_shape=[SDS((lp, c), F32), SDS((8, c), F32), SDS((8, c), F32)], grid=(lp // tm,),
        in_specs=[pl.BlockSpec((tm, c), lambda i: (i, 0)), pl.BlockSpec((tm, c), lambda i: (i, 0)),
                  pl.BlockSpec((1, c), lambda i: (0, 0)), pl.BlockSpec((1, c), lambda i: (0, 0))],
        out_specs=[pl.BlockSpec((tm, c), lambda i: (i, 0)), pl.BlockSpec((8, c), lambda i: (0, 0)),
                   pl.BlockSpec((8, c), lambda i: (0, 0))],
        name=name, compiler_params=_cp())(yc, dout, g, b)


def _conv_bwd_dw(dyc, cacg, w, name):
    lp, c = dyc.shape
    ncb = c // 128
    nchunk = lp // ABLK
    off = CONV_PAD - (CONV_K - 1)

    def body(dy_ref, a_ref, g_ref, w_ref, dcc_ref, dw_ref, db_ref, upad, dypad, dwacc):
        upad[0:CONV_PAD, :] = jnp.zeros((CONV_PAD, 128), F32)
        dypad[lp:lp + CONV_PAD, :] = jnp.zeros((CONV_PAD, 128), F32)
        dwacc[...] = jnp.zeros_like(dwacc)
        db_ref[...] = jnp.zeros_like(db_ref)

        def fill(ch, carry):
            base = pl.multiple_of(ch * ABLK, ABLK)
            upad[pl.ds(base + CONV_PAD, ABLK), :] = a_ref[pl.ds(base, ABLK), :] * _sigmoid(g_ref[pl.ds(base, ABLK), :])
            dypad[pl.ds(base, ABLK), :] = dy_ref[pl.ds(base, ABLK), :]
            return carry

        lax.fori_loop(0, nchunk, fill, 0)

        def comp(ch, carry):
            base = pl.multiple_of(ch * ABLK, ABLK)
            dy = dy_ref[pl.ds(base, ABLK), :]
            du = jnp.zeros((ABLK, 128), F32)
            for k in range(CONV_K):
                du = du + dypad[pl.ds(base + (CONV_K - 1 - k), ABLK), :] * w_ref[k:k + 1, :]
                dwacc[k * 8:(k + 1) * 8, :] += _sum8(dy * upad[pl.ds(base + (off + k), ABLK), :])
            db_ref[...] += _sum8(dy)
            a = a_ref[pl.ds(base, ABLK), :]
            s = _sigmoid(g_ref[pl.ds(base, ABLK), :])
            dcc_ref[0, pl.ds(base, ABLK), :] = (du * s).astype(BF16)
            dcc_ref[1, pl.ds(base, ABLK), :] = (du * a * (s * (1.0 - s))).astype(BF16)
            return carry

        lax.fori_loop(0, nchunk, comp, 0)
        dw_ref[...] = dwacc[...].reshape(CONV_PAD, 8, 128).sum(axis=1)

    return pl.pallas_call(
        body, out_shape=[SDS((2, lp, c), BF16), SDS((CONV_PAD, c), F32), SDS((8, c), F32)],
        grid=(ncb,),
        in_specs=[pl.BlockSpec((lp, 128), lambda j: (0, j)), pl.BlockSpec((lp, 128), lambda j: (0, j)),
                  pl.BlockSpec((lp, 128), lambda j: (0, j + ncb)), pl.BlockSpec((CONV_PAD, 128), lambda j: (0, j))],
        out_specs=[pl.BlockSpec((2, lp, 128), lambda j: (0, 0, j)),
                   pl.BlockSpec((CONV_PAD, 128), lambda j: (0, j)), pl.BlockSpec((8, 128), lambda j: (0, j))],
        scratch_shapes=[pltpu.VMEM((lp + CONV_PAD, 128), F32), pltpu.VMEM((lp + CONV_PAD, 128), F32),
                        pltpu.VMEM((CONV_PAD * 8, 128), F32)],
        name=name, compiler_params=_cp())(dyc, cacg, cacg, w)


def _mesh_pos():
    x, y, c = lax.axis_index("x"), lax.axis_index("y"), lax.axis_index("c")
    return x, y, c


def _peer(pos, r):
    x, y, c = pos
    px = (1 - x) if (r >> 2) & 1 else x
    py = (1 - y) if (r >> 1) & 1 else y
    pc = (1 - c) if r & 1 else c
    return (px, py, pc), 4 * px + 2 * py + pc


class _Job:
    def __init__(self, src, dst, scatter, src_layer=None, dst_layer=None):
        self.src, self.dst, self.scatter, self.src_layer, self.dst_layer = src, dst, scatter, src_layer, dst_layer

    def src_view(self, ins, idx):
        v = ins[self.src] if self.src_layer is None else ins[self.src].at[self.src_layer]
        return v.at[idx] if self.scatter else v

    def dst_view(self, outs, slot):
        v = outs[self.dst] if self.dst_layer is None else outs[self.dst].at[self.dst_layer]
        return v.at[slot]


def _exchange_start(jobs, ins, outs, send, recv, loc):
    pos = _mesh_pos()
    me = 4 * pos[0] + 2 * pos[1] + pos[2]
    for j, job in enumerate(jobs):
        pltpu.make_async_copy(job.src_view(ins, me), job.dst_view(outs, me), loc.at[j]).start()
        for r in range(1, N_DEV):
            peer, peer_idx = _peer(pos, r)
            pltpu.make_async_remote_copy(src_ref=job.src_view(ins, peer_idx), dst_ref=job.dst_view(outs, me),
                                         send_sem=send.at[j, r - 1], recv_sem=recv.at[j, r - 1], device_id=peer,
                                         device_id_type=pl.DeviceIdType.MESH).start()


def _exchange_wait(jobs, ins, outs, send, recv, loc):
    pos = _mesh_pos()
    me = 4 * pos[0] + 2 * pos[1] + pos[2]
    for j, job in enumerate(jobs):
        for r in range(1, N_DEV):
            peer, peer_idx = _peer(pos, r)
            cp = pltpu.make_async_remote_copy(src_ref=job.src_view(ins, peer_idx), dst_ref=job.dst_view(outs, peer_idx),
                                              send_sem=send.at[j, r - 1], recv_sem=recv.at[j, r - 1], device_id=peer,
                                              device_id_type=pl.DeviceIdType.MESH)
            cp.wait_recv()
            cp.wait_send()
        pltpu.make_async_copy(job.src_view(ins, me), job.dst_view(outs, me), loc.at[j]).wait()


def _exchange_sems(n_jobs):
    return [pltpu.SemaphoreType.DMA((n_jobs, N_DEV - 1)), pltpu.SemaphoreType.DMA((n_jobs, N_DEV - 1)),
            pltpu.SemaphoreType.DMA((n_jobs,))]


def _exchange(jobs, arrs, out_shape, name):
    n_in, n_out = len(arrs), len(out_shape)
    any_spec = pl.BlockSpec(memory_space=pl.ANY)

    def body(*refs):
        ins, outs, sems = refs[:n_in], refs[n_in:n_in + n_out], refs[n_in + n_out:]
        _exchange_start(jobs, ins, outs, *sems)
        _exchange_wait(jobs, ins, outs, *sems)

    return pl.pallas_call(
        body, out_shape=out_shape, in_specs=[any_spec] * n_in, out_specs=[any_spec] * n_out,
        scratch_shapes=_exchange_sems(len(jobs)),
        name=name, compiler_params=pltpu.CompilerParams(has_side_effects=True))(*arrs)


def _all_reduce_small(p8, q, name):
    ni, _, w = p8.shape
    nq = q.shape[0]
    rows = ni + nq

    def body(p_ref, q_ref, o_ref, buf, send, recv):
        pos = _mesh_pos()
        me = 4 * pos[0] + 2 * pos[1] + pos[2]
        buf[me] = jnp.concatenate([p_ref[...].sum(axis=1), q_ref[...]], axis=0)
        for r in range(1, N_DEV):
            peer, _ = _peer(pos, r)
            pltpu.make_async_remote_copy(src_ref=buf.at[me], dst_ref=buf.at[me], send_sem=send.at[r - 1],
                                         recv_sem=recv.at[r - 1], device_id=peer,
                                         device_id_type=pl.DeviceIdType.MESH).start()
        for r in range(1, N_DEV):
            peer, peer_idx = _peer(pos, r)
            cp = pltpu.make_async_remote_copy(src_ref=buf.at[me], dst_ref=buf.at[peer_idx], send_sem=send.at[r - 1],
                                              recv_sem=recv.at[r - 1], device_id=peer,
                                              device_id_type=pl.DeviceIdType.MESH)
            cp.wait_recv()
            cp.wait_send()
        acc = buf[0]
        for dev in range(1, N_DEV):
            acc = acc + buf[dev]
        o_ref[...] = acc

    vmem = pl.BlockSpec(memory_space=pltpu.VMEM)
    return pl.pallas_call(
        body, out_shape=SDS((rows, w), F32), in_specs=[vmem, vmem], out_specs=vmem,
        scratch_shapes=[pltpu.VMEM((N_DEV, rows, w), F32), pltpu.SemaphoreType.DMA((N_DEV - 1,)),
                        pltpu.SemaphoreType.DMA((N_DEV - 1,))],
        name=name, compiler_params=pltpu.CompilerParams(has_side_effects=True))(p8, q)


def _adamw_math(w, g, m, v):
    m = ADAM_B1 * m + (1.0 - ADAM_B1) * g
    v = ADAM_B2 * v + (1.0 - ADAM_B2) * (g * g)
    m_hat = m / (1.0 - ADAM_B1 ** ADAM_STEP)
    v_hat = v / (1.0 - ADAM_B2 ** ADAM_STEP)
    delta = -ADAM_LR * (m_hat / (jnp.sqrt(v_hat) + ADAM_EPS) + ADAM_WD * w)
    return delta, m, v


def _adamw_shard(parts, w, m, v, name):
    depth = len(parts)
    _, rr, cc = parts[0].shape
    tr = _tile(rr, 256, 8)
    nt = rr // tr

    def body(*refs):
        p_refs = refs[:depth]
        w_ref, m_ref, v_ref, g_out, d_out, m_out, v_out = refs[depth:]
        for li in range(depth):
            @pl.when(pl.program_id(0) == li)
            def _(p_ref=p_refs[li]):
                g = p_ref[0].astype(F32)
                for dev in range(1, N_DEV):
                    g = g + p_ref[dev].astype(F32)
                delta, mm, vv = _adamw_math(w_ref[...], g, m_ref[...], v_ref[...])
                g_out[...] = g
                d_out[...] = delta
                m_out[...] = mm
                v_out[...] = vv

    def part_spec(li):
        return pl.BlockSpec((N_DEV, tr, cc), lambda l, i: (0, jnp.where(l == li, i, jnp.where(l < li, 0, nt - 1)), 0))

    blk = pl.BlockSpec((None, tr, cc), lambda l, i: (l, i, 0))
    return pl.pallas_call(
        body, out_shape=[SDS((depth, rr, cc), F32)] * 4, grid=(depth, nt),
        in_specs=[part_spec(li) for li in range(depth)] + [blk, blk, blk],
        out_specs=[blk] * 4, name=name, compiler_params=_cp())(*parts, w, m, v)


def _adamw_flat(g, w, m, v, name):
    def body(g_ref, w_ref, m_ref, v_ref, d_out, m_out, v_out):
        delta, mm, vv = _adamw_math(w_ref[...], g_ref[...], m_ref[...], v_ref[...])
        d_out[...] = delta
        m_out[...] = mm
        v_out[...] = vv

    return pl.pallas_call(body, out_shape=[SDS(g.shape, F32)] * 3, name=name, compiler_params=_cp())(g, w, m, v)


def _from_cols(t):
    return jnp.transpose(t, (1, 0, 2)).reshape(t.shape[1], N_DEV * t.shape[2])


def _swap(t):
    return jnp.swapaxes(t, -1, -2)


def kernel(x, meta_tokens, mix_norm_g, w_in, conv_dw_w, conv_dw_b, conv_ln_g, conv_ln_b, w_out, ffn_norm_g, w_gate, w_up, w_down, final_norm_g, loss_target, m_meta_tokens, m_mix_norm_g, m_w_in, m_conv_dw_w, m_conv_dw_b, m_conv_ln_g, m_conv_ln_b, m_w_out, m_ffn_norm_g, m_w_gate, m_w_up, m_w_down, m_final_norm_g, v_meta_tokens, v_mix_norm_g, v_w_in, v_conv_dw_w, v_conv_dw_b, v_conv_ln_g, v_conv_ln_b, v_w_out, v_ffn_norm_g, v_w_gate, v_w_up, v_w_down, v_final_norm_g):
    depth, d, in_shard = w_in.shape
    seq = x.shape[1]
    sb = N_HEADS * HEAD_DIM
    cc = conv_dw_w.shape[2] * N_DEV
    ff = w_gate.shape[2] * N_DEV
    assert in_shard * N_DEV == 3 * sb + 2 * cc and x.shape[0] == 1
    lr = N_META + seq
    lp = -(-lr // ABLK) * ABLK
    me = 4 * lax.axis_index("x") + 2 * lax.axis_index("y") + lax.axis_index("c")

    big_names = ("w_in", "w_out", "w_gate", "w_up", "w_down")
    transposed = {"w_in": True, "w_out": False, "w_gate": True, "w_up": True, "w_down": False}
    shard = dict(w_in=_swap(w_in).astype(BF16), w_out=w_out.astype(BF16), w_gate=_swap(w_gate).astype(BF16),
                 w_up=_swap(w_up).astype(BF16), w_down=w_down.astype(BF16))

    def gather_of(keys):
        names = sorted({n for n, _ in keys}, key=big_names.index)
        jobs = [_Job(names.index(n), j, False, src_layer=i) for j, (n, i) in enumerate(keys)]
        return jobs, [shard[n] for n in names], [SDS((N_DEV,) + shard[n].shape[1:], BF16) for n, _ in keys]

    first_keys = [("w_in", 0)]
    ffn_keys = lambda i: [("w_gate", i), ("w_up", i), ("w_down", i)]
    mixer_keys = lambda i: [("w_in", i), ("w_out", i)]

    def receive(keys, arrays):
        for (n, li), t in zip(keys, arrays):
            wl[li][n] = t.reshape(-1, d)

    jobs, srcs, out_shape = gather_of(first_keys)
    for extra in (meta_tokens, conv_dw_w):
        jobs.append(_Job(len(srcs), len(out_shape), False))
        srcs.append(extra)
        out_shape.append(SDS((N_DEV,) + extra.shape, F32))
    gathered = _exchange(jobs, srcs, out_shape, "gather_first")
    wl = [dict() for _ in range(depth)]
    receive(first_keys, gathered)
    meta_full = _from_cols(gathered[-2])
    taps = jnp.transpose(gathered[-1], (1, 2, 0, 3)).reshape(depth, CONV_K, cc)
    taps = jnp.pad(taps, ((0, 0), (0, CONV_PAD - CONV_K), (0, 0)))
    tri_fwd, tri_after, tri_before = _tri_consts()

    h = jnp.concatenate([meta_full, x[0], jnp.zeros((lp - lr, d), F32)], axis=0)
    saved = []
    for i in range(depth):
        p = wl[i]
        sv = dict(h_in=h)
        qkv, hn = _rms_mm(h, mix_norm_g[i:i + 1], p["w_in"], 0, 3 * sb, BF16, f"proj_qkv_{i}", True)
        cacg = _rms_mm(h, mix_norm_g[i:i + 1], p["w_in"], 3 * sb, 2 * cc, F32, f"proj_conv_{i}", False)[0]
        keys = ([("w_out", 0)] if i == 0 else []) + ffn_keys(i)
        attn, rsave, *arrived = _attn_fwd(qkv, tri_fwd, f"attn_fwd_{i}", _Rider(*gather_of(keys)))
        receive(keys, arrived)
        yc = _conv_fwd_dw(cacg, taps[i], conv_dw_b[i:i + 1], f"conv_fwd_dw_{i}")
        conv = _conv_fwd_ln(yc, conv_ln_g[i:i + 1], conv_ln_b[i:i + 1], f"conv_fwd_ln_{i}")
        h = _mix_out(attn, conv, p["w_out"], h, f"mix_out_{i}")
        sv.update(qkv=qkv, hn=hn, cacg=cacg, rsave=rsave, yc=yc, attn=attn, conv=conv, h_mid=h)
        keys = mixer_keys(i + 1) if i + 1 < depth else []
        h, hn2, act, *arrived = _ffn_fwd(h, ffn_norm_g[i:i + 1], p["w_gate"], p["w_up"], p["w_down"], f"ffn_fwd_{i}",
                                         _Rider(*gather_of(keys)) if keys else None)
        receive(keys, arrived)
        sv.update(hn2=hn2, act=act)
        saved.append(sv)

    tpad = jnp.pad(loss_target[0], ((N_META, lp - lr), (0, 0)))
    dh, dg_final, loss_part = _loss_head(h, tpad, final_norm_g.reshape(1, d), seq, "loss_head")
    loss = lax.psum(loss_part[0, 0], MESH_AXES)

    parts = {}

    def sending(items):
        srcs = [t.reshape(N_DEV, t.shape[0] // N_DEV, d) for _, t in items]
        return _Rider([_Job(j, j, True) for j in range(len(items))], srcs, [SDS(t.shape, BF16) for t in srcs])

    def arrive(items, arrays):
        parts.update({key: t for (key, _), t in zip(items, arrays)})

    grads = [None] * depth
    from_above = []
    for i in reversed(range(depth)):
        p, sv = wl[i], saved[i]
        d_g, d_u, *got = _ffn_bwd_act(dh, sv["hn2"], p["w_gate"], p["w_up"], p["w_down"], f"ffn_bwd_act_{i}",
                                      sending(from_above))
        arrive(from_above, got)
        gw_down = _mm_tn(sv["act"], dh, f"grad_w_down_{i}")
        gw_gate = _mm_tn(d_g, sv["hn2"], f"grad_w_gate_{i}")
        gw_up = _mm_tn(d_u, sv["hn2"], f"grad_w_up_{i}")
        items = [(("w_down", i), gw_down)]
        dh, dg_ffn, *got = _ffn_bwd_in(d_g, d_u, p["w_gate"], p["w_up"], sv["h_mid"], ffn_norm_g[i:i + 1], dh,
                                       f"ffn_bwd_in_{i}", sending(items))
        arrive(items, got)
        gw_out = jnp.concatenate([_mm_tn(sv["attn"], dh, f"grad_w_out_attn_{i}"),
                                  _mm_tn(sv["conv"], dh, f"grad_w_out_conv_{i}")], axis=0)
        d_attn = _mm_nt(dh, p["w_out"], 0, sb, BF16, f"mix_bwd_attn_{i}")
        d_conv = _mm_nt(dh, p["w_out"], sb, cc, F32, f"mix_bwd_conv_{i}")
        items = [(("w_gate", i), gw_gate), (("w_up", i), gw_up), (("w_out", i), gw_out)]
        dqkv, *got = _attn_bwd(sv["qkv"], d_attn, sv["rsave"], tri_after, tri_before, f"attn_bwd_{i}", sending(items))
        arrive(items, got)
        dyc, dg_ln, db_ln = _conv_bwd_ln(sv["yc"], d_conv, conv_ln_g[i:i + 1], conv_ln_b[i:i + 1], f"conv_bwd_ln_{i}")
        dcc, g_taps_i, db_conv = _conv_bwd_dw(dyc, sv["cacg"], taps[i], f"conv_bwd_dw_{i}")
        gw_in = _grad_w_in_t(dqkv, dcc, sv["hn"], f"grad_w_in_{i}")
        from_above = [(("w_in", i), gw_in)]
        items = from_above if i == 0 else []
        dh, dg_mix, *got = _mix_bwd_in(dqkv, dcc, p["w_in"], sv["h_in"], mix_norm_g[i:i + 1], dh, f"mix_bwd_in_{i}",
                                       sending(items))
        arrive(items, got)
        grads[i] = dict(taps=g_taps_i, dg_mix=dg_mix, dg_ffn=dg_ffn, dg_ln=dg_ln, db_ln=db_ln, db_conv=db_conv)
    grad_x = dh[N_META:lr][None]

    wide = lambda key: jnp.concatenate([jnp.pad(grads[i][key], ((0, 0), (0, d // depth - cc))) for i in range(depth)], axis=1)
    assert depth * cc <= d and d % depth == 0
    p8 = jnp.stack([grads[i]["dg_mix"] for i in range(depth)] + [grads[i]["dg_ffn"] for i in range(depth)]
                   + [dg_final, wide("db_conv"), wide("dg_ln"), wide("db_ln")])
    small = _all_reduce_small(p8, jnp.concatenate([wide("taps"), dh[:N_META]], axis=0), "reduce_small")
    r0 = 2 * depth + 4
    g_mix = small[0:depth]
    g_ffn = small[depth:2 * depth]
    g_final = small[2 * depth]
    narrow = lambda row: jnp.stack([row[i * (d // depth):i * (d // depth) + cc] for i in range(depth)])
    g_cb, g_lg, g_lb = narrow(small[2 * depth + 1]), narrow(small[2 * depth + 2]), narrow(small[2 * depth + 3])
    g_taps_full = jnp.stack([small[r0:r0 + CONV_K, i * (d // depth):i * (d // depth) + cc] for i in range(depth)])
    csh = cc // N_DEV
    g_taps_own = lax.dynamic_slice_in_dim(g_taps_full, me * csh, csh, axis=2)
    g_meta_full = small[r0 + CONV_PAD:r0 + CONV_PAD + N_META]
    msh = d // N_DEV
    g_meta_own = lax.dynamic_slice_in_dim(g_meta_full, me * msh, msh, axis=1)

    small_g = [g_meta_own, g_mix, g_taps_own, g_cb, g_lg, g_lb, g_ffn, g_final]
    small_w = [meta_tokens, mix_norm_g, conv_dw_w, conv_dw_b, conv_ln_g, conv_ln_b, ffn_norm_g, final_norm_g]
    small_m = [m_meta_tokens, m_mix_norm_g, m_conv_dw_w, m_conv_dw_b, m_conv_ln_g, m_conv_ln_b, m_ffn_norm_g, m_final_norm_g]
    small_v = [v_meta_tokens, v_mix_norm_g, v_conv_dw_w, v_conv_dw_b, v_conv_ln_g, v_conv_ln_b, v_ffn_norm_g, v_final_norm_g]
    sizes = [int(math.prod(t.shape)) for t in small_w]
    total = sum(sizes)
    rows = -(-total // (8 * 128)) * 8

    def flat(ts):
        v = jnp.concatenate([t.reshape(-1) for t in ts])
        return jnp.pad(v, (0, rows * 128 - total)).reshape(rows, 128)

    def unflat(t):
        v, out, o = t.reshape(-1), [], 0
        for sz, ref in zip(sizes, small_w):
            out.append(v[o:o + sz].reshape(ref.shape))
            o += sz
        return out

    sd, sm, sv_ = _adamw_flat(flat(small_g), flat(small_w), flat(small_m), flat(small_v), "adamw_small")
    s_delta, s_m, s_v = unflat(sd), unflat(sm), unflat(sv_)

    big = []
    for n, w, m, v in zip(big_names, (w_in, w_out, w_gate, w_up, w_down), (m_w_in, m_w_out, m_w_gate, m_w_up, m_w_down),
                          (v_w_in, v_w_out, v_w_gate, v_w_up, v_w_down)):
        fix = _swap if transposed[n] else (lambda t: t)
        res = _adamw_shard([parts[(n, i)] for i in range(depth)], fix(w), fix(m), fix(v), f"adamw_{n}")
        big.append([fix(t) for t in res])
    b_in, b_out, b_gate, b_up, b_down = big

    def ordered(k, smalls):
        s_meta, s_mix, s_taps, s_cb, s_lg, s_lb, s_ffn, s_final = smalls
        return [s_meta, s_mix, b_in[k], s_taps, s_cb, s_lg, s_lb, b_out[k], s_ffn, b_gate[k], b_up[k], b_down[k], s_final]

    return (loss, grad_x, *ordered(0, small_g), *ordered(1, s_delta), *ordered(2, s_m), *ordered(3, s_v))
```

```python
import functools
import math

import jax
import jax.numpy as jnp
from jax import lax
from jax.experimental import pallas as pl
from jax.experimental.pallas import tpu as pltpu

F32 = jnp.float32
BF16 = jnp.bfloat16
SDS = jax.ShapeDtypeStruct

N_META = 16
N_HEADS = 8
HEAD_DIM = 64
CONV_K = 31
CONV_PAD = 32
ABLK = 128
ATT_GROUP = 4
ATT_SPLIT = 2
ATT_TOP = 2
EXP_ZERO_AT = 104.0
GONE = 1e30
MASKED = -1e30
FF_CHUNK = 256
EPS = 1e-6
N_DEV = 8
MESH_AXES = ("x", "y", "c")
ADAM_LR = 0.001
ADAM_B1 = 0.9
ADAM_B2 = 0.999
ADAM_EPS = 1e-08
ADAM_WD = 0.01
ADAM_STEP = 10
MIB = 1 << 20
VMEM_LIMIT_MIB = 48
MM_ROWS = 1056
EW_ROWS = 528


def _cp():
    return pltpu.CompilerParams(vmem_limit_bytes=VMEM_LIMIT_MIB * MIB)


def _tile(n, cap, mult):
    best = None
    for t in range(mult, min(n, cap) + 1, mult):
        if n % t == 0:
            best = t
    assert best is not None, (n, cap, mult)
    return best


def _sum8(v):
    r, c = v.shape
    return v.reshape(r // 8, 8, c).sum(axis=0)


def _sigmoid(x):
    return 1.0 / (1.0 + jnp.exp(-x))


NT = (((1,), (1,)), ((), ()))
TN = (((0,), (0,)), ((), ()))


def _rms_mm(h, g, wt, w_row0, n_cols, out_dtype, name, hn_out):
    lp, d = h.shape
    tm = _tile(lp, MM_ROWS, 16)
    tn = _tile(math.gcd(n_cols, w_row0), 768, 128)
    off = w_row0 // tn

    def body(h_ref, g_ref, w_ref, *rest):
        if hn_out:
            o_ref, hn_ref, hn_s = rest
        else:
            o_ref, hn_s = rest

        @pl.when(pl.program_id(1) == 0)
        def _():
            x = h_ref[...]
            r = lax.rsqrt(jnp.mean(x * x, axis=-1, keepdims=True) + EPS)
            hn = ((x * r) * g_ref[...]).astype(BF16)
            hn_s[...] = hn
            if hn_out:
                hn_ref[...] = hn

        o_ref[...] = lax.dot_general(hn_s[...], w_ref[...], NT, preferred_element_type=F32).astype(out_dtype)

    out_shape = [SDS((lp, n_cols), out_dtype)]
    out_specs = [pl.BlockSpec((tm, tn), lambda i, j: (i, j))]
    if hn_out:
        out_shape.append(SDS((lp, d), BF16))
        out_specs.append(pl.BlockSpec((tm, d), lambda i, j: (i, 0)))
    return pl.pallas_call(
        body, out_shape=out_shape, grid=(lp // tm, n_cols // tn),
        in_specs=[pl.BlockSpec((tm, d), lambda i, j: (i, 0)),
                  pl.BlockSpec((1, d), lambda i, j: (0, 0)),
                  pl.BlockSpec((tn, d), lambda i, j: (j + off, 0))],
        out_specs=out_specs, scratch_shapes=[pltpu.VMEM((tm, d), BF16)],
        name=name, compiler_params=_cp())(h, g, wt)


def _mm_nt(a, b, b_row0, n_out, out_dtype, name):
    m, k = a.shape
    tm = _tile(m, MM_ROWS, 16)
    tn = _tile(math.gcd(n_out, b_row0), 512, 128)
    off = b_row0 // tn

    def body(a_ref, b_ref, o_ref):
        o_ref[...] = lax.dot_general(a_ref[...].astype(BF16), b_ref[...], NT,
                                     preferred_element_type=F32).astype(out_dtype)

    return pl.pallas_call(
        body, out_shape=SDS((m, n_out), out_dtype), grid=(m // tm, n_out // tn),
        in_specs=[pl.BlockSpec((tm, k), lambda i, j: (i, 0)), pl.BlockSpec((tn, k), lambda i, j: (j + off, 0))],
        out_specs=pl.BlockSpec((tm, tn), lambda i, j: (i, j)), name=name, compiler_params=_cp())(a, b)


def _mix_out(attn, conv, w, res, name):
    m, ka = attn.shape
    kc = conv.shape[1]
    n = w.shape[1]
    assert ka == kc
    tm = _tile(m, MM_ROWS, 16)
    tn = _tile(n, 512, 128)

    def body(a_ref, c_ref, wa_ref, wc_ref, r_ref, o_ref):
        o_ref[...] = (r_ref[...] + jnp.dot(a_ref[...], wa_ref[...], preferred_element_type=F32)
                      + jnp.dot(c_ref[...], wc_ref[...], preferred_element_type=F32))

    return pl.pallas_call(
        body, out_shape=SDS((m, n), F32), grid=(m // tm, n // tn),
        in_specs=[pl.BlockSpec((tm, ka), lambda i, j: (i, 0)), pl.BlockSpec((tm, kc), lambda i, j: (i, 0)),
                  pl.BlockSpec((ka, tn), lambda i, j: (0, j)), pl.BlockSpec((kc, tn), lambda i, j: (1, j)),
                  pl.BlockSpec((tm, tn), lambda i, j: (i, j))],
        out_specs=pl.BlockSpec((tm, tn), lambda i, j: (i, j)), name=name, compiler_params=_cp())(attn, conv, w, w, res)


def _mm_tn(a, b, name):
    l, m = a.shape
    n = b.shape[1]
    tm = _tile(m, 1408, 128)
    tn = _tile(n, 1024, 128)
    tl = _tile(l, 512, 128)
    nl = l // tl

    def body(a_ref, b_ref, o_ref, acc):
        @pl.when(pl.program_id(2) == 0)
        def _():
            acc[...] = jnp.zeros_like(acc)

        acc[...] += lax.dot_general(a_ref[...].astype(BF16), b_ref[...].astype(BF16), TN, preferred_element_type=F32)

        @pl.when(pl.program_id(2) == nl - 1)
        def _():
            o_ref[...] = acc[...].astype(BF16)

    return pl.pallas_call(
        body, out_shape=SDS((m, n), BF16), grid=(m // tm, n // tn, nl),
        in_specs=[pl.BlockSpec((tl, tm), lambda i, j, s: (s, i)),
                  pl.BlockSpec((tl, tn), lambda i, j, s: (s, j))],
        out_specs=pl.BlockSpec((tm, tn), lambda i, j, s: (i, j)),
        scratch_shapes=[pltpu.VMEM((tm, tn), F32)], name=name, compiler_params=_cp())(a, b)


def _grad_w_in_t(dqkv, dcc, hn, name):
    nq, l, w = dqkv.shape
    nc = dcc.shape[0]
    d = hn.shape[1]
    assert dcc.shape[2] == w
    tn = _tile(d, 1024, 128)
    tl = _tile(l, 512, 128)
    nl = l // tl

    def body(q_ref, c_ref, b_ref, o_ref, acc):
        p, s = pl.program_id(0), pl.program_id(2)

        @pl.when(s == 0)
        def _():
            acc[...] = jnp.zeros_like(acc)

        @pl.when(p < nq)
        def _():
            acc[...] += lax.dot_general(q_ref[...], b_ref[...], TN, preferred_element_type=F32)

        @pl.when(p >= nq)
        def _():
            acc[...] += lax.dot_general(c_ref[...], b_ref[...], TN, preferred_element_type=F32)

        @pl.when(s == nl - 1)
        def _():
            o_ref[...] = acc[...].astype(BF16)

    out = pl.pallas_call(
        body, out_shape=SDS((nq + nc, w, d), BF16), grid=(nq + nc, d // tn, nl),
        in_specs=[pl.BlockSpec((None, tl, w), lambda p, j, s: (jnp.minimum(p, nq - 1), s, 0)),
                  pl.BlockSpec((None, tl, w), lambda p, j, s: (jnp.maximum(p - nq, 0), s, 0)),
                  pl.BlockSpec((tl, tn), lambda p, j, s: (s, j))],
        out_specs=pl.BlockSpec((None, w, tn), lambda p, j, s: (p, 0, j)),
        scratch_shapes=[pltpu.VMEM((w, tn), F32)], name=name, compiler_params=_cp())(dqkv, dcc, hn)
    return out.reshape((nq + nc) * w, d)


def _rms_bwd_tail(acc, h_ref, g_ref, dres_ref, o_ref, dg_ref):
    x = h_ref[...]
    r = lax.rsqrt(jnp.mean(x * x, axis=-1, keepdims=True) + EPS)
    xr = x * r
    dyv = acc[...]
    gy = dyv * g_ref[...]
    o_ref[...] = dres_ref[...] + r * (gy - xr * jnp.mean(gy * xr, axis=-1, keepdims=True))
    dg_ref[...] += _sum8(dyv * xr)


def _mix_bwd_in(dqkv, dcc, wt, h, g, dres, name, rider=None):
    nq, lp, w = dqkv.shape
    nc = dcc.shape[0]
    d = h.shape[1]
    tm = _tile(lp, MM_ROWS, 16)
    npc = nq + nc
    ni = lp // tm
    rider = rider or _Rider()

    def body(q_ref, c_ref, w_ref, h_ref, g_ref, dres_ref, *rest):
        (o_ref, dg_ref), (acc,) = rider.split(rest, 2, 1)
        i, p = pl.program_id(0), pl.program_id(1)
        rider.start((i == 0) & (p == 0))

        @pl.when(p == 0)
        def _():
            acc[...] = jnp.zeros_like(acc)

        @pl.when((p == 0) & (i == 0))
        def _():
            dg_ref[...] = jnp.zeros_like(dg_ref)

        @pl.when(p < nq)
        def _():
            acc[...] += jnp.dot(q_ref[...], w_ref[...], preferred_element_type=F32)

        @pl.when(p >= nq)
        def _():
            acc[...] += jnp.dot(c_ref[...], w_ref[...], preferred_element_type=F32)

        @pl.when(p == npc - 1)
        def _():
            _rms_bwd_tail(acc, h_ref, g_ref, dres_ref, o_ref, dg_ref)

        rider.wait((i == ni - 1) & (p == npc - 1))

    return pl.pallas_call(
        body, out_shape=[SDS((lp, d), F32), SDS((8, d), F32)] + rider.out_shape, grid=(ni, npc),
        in_specs=[pl.BlockSpec((None, tm, w), lambda i, p: (jnp.minimum(p, nq - 1), i, 0)),
                  pl.BlockSpec((None, tm, w), lambda i, p: (jnp.maximum(p - nq, 0), i, 0)),
                  pl.BlockSpec((w, d), lambda i, p: (p, 0)),
                  pl.BlockSpec((tm, d), lambda i, p: (i, 0)),
                  pl.BlockSpec((1, d), lambda i, p: (0, 0)),
                  pl.BlockSpec((tm, d), lambda i, p: (i, 0))] + rider.in_specs(),
        out_specs=[pl.BlockSpec((tm, d), lambda i, p: (i, 0)), pl.BlockSpec((8, d), lambda i, p: (0, 0))]
        + rider.out_specs(),
        scratch_shapes=[pltpu.VMEM((tm, d), F32)] + rider.scratch(), name=name,
        compiler_params=_cp())(dqkv, dcc, wt, h, g, dres, *rider.srcs)


def _ffn_bwd_in(d_g, d_u, wg_t, wu_t, h, g, dres, name, rider=None):
    lp, f = d_g.shape
    d = h.shape[1]
    tm = _tile(lp, MM_ROWS, 16)
    tk = _tile(f, 512, 128)
    nk = f // tk
    ni = lp // tm
    rider = rider or _Rider()

    def body(dg_in, du_in, wg_ref, wu_ref, h_ref, g_ref, dres_ref, *rest):
        (o_ref, dg_ref), (acc,) = rider.split(rest, 2, 1)
        i, kk = pl.program_id(0), pl.program_id(1)
        rider.start((i == 0) & (kk == 0))

        @pl.when(kk == 0)
        def _():
            acc[...] = jnp.zeros_like(acc)

        @pl.when((kk == 0) & (i == 0))
        def _():
            dg_ref[...] = jnp.zeros_like(dg_ref)

        acc[...] += (jnp.dot(dg_in[...], wg_ref[...], preferred_element_type=F32)
                     + jnp.dot(du_in[...], wu_ref[...], preferred_element_type=F32))

        @pl.when(kk == nk - 1)
        def _():
            _rms_bwd_tail(acc, h_ref, g_ref, dres_ref, o_ref, dg_ref)

        rider.wait((i == ni - 1) & (kk == nk - 1))

    return pl.pallas_call(
        body, out_shape=[SDS((lp, d), F32), SDS((8, d), F32)] + rider.out_shape, grid=(ni, nk),
        in_specs=[pl.BlockSpec((tm, tk), lambda i, j: (i, j)), pl.BlockSpec((tm, tk), lambda i, j: (i, j)),
                  pl.BlockSpec((tk, d), lambda i, j: (j, 0)), pl.BlockSpec((tk, d), lambda i, j: (j, 0)),
                  pl.BlockSpec((tm, d), lambda i, j: (i, 0)),
                  pl.BlockSpec((1, d), lambda i, j: (0, 0)),
                  pl.BlockSpec((tm, d), lambda i, j: (i, 0))] + rider.in_specs(),
        out_specs=[pl.BlockSpec((tm, d), lambda i, j: (i, 0)), pl.BlockSpec((8, d), lambda i, j: (0, 0))]
        + rider.out_specs(),
        scratch_shapes=[pltpu.VMEM((tm, d), F32)] + rider.scratch(), name=name,
        compiler_params=_cp())(d_g, d_u, wg_t, wu_t, h, g, dres, *rider.srcs)


def _ffn_fwd(h, g, wg_t, wu_t, w_d, name, rider=None):
    lp, d = h.shape
    f = w_d.shape[0]
    tm = _tile(lp, MM_ROWS, 16)
    nc = f // FF_CHUNK
    ni = lp // tm
    rider = rider or _Rider()

    def body(h_ref, g_ref, wg_ref, wu_ref, wd_ref, *rest):
        (o_ref, hn_ref, act_ref), (hn_s, acc) = rider.split(rest, 3, 2)
        c = pl.program_id(1)
        rider.start((pl.program_id(0) == 0) & (c == 0))

        @pl.when(c == 0)
        def _():
            x = h_ref[...]
            r = lax.rsqrt(jnp.mean(x * x, axis=-1, keepdims=True) + EPS)
            hn = ((x * r) * g_ref[...]).astype(BF16)
            hn_s[...] = hn
            hn_ref[...] = hn
            acc[...] = jnp.zeros_like(acc)

        gg = lax.dot_general(hn_s[...], wg_ref[...], NT, preferred_element_type=F32)
        uu = lax.dot_general(hn_s[...], wu_ref[...], NT, preferred_element_type=F32)
        act = ((gg * _sigmoid(gg)) * uu).astype(BF16)
        act_ref[...] = act
        acc[...] += jnp.dot(act, wd_ref[...], preferred_element_type=F32)

        @pl.when(c == nc - 1)
        def _():
            o_ref[...] = h_ref[...] + acc[...]

        rider.wait((pl.program_id(0) == ni - 1) & (c == nc - 1))

    chunk = pl.BlockSpec((FF_CHUNK, d), lambda i, j: (j, 0))
    return pl.pallas_call(
        body, out_shape=[SDS((lp, d), F32), SDS((lp, d), BF16), SDS((lp, f), BF16)] + rider.out_shape, grid=(ni, nc),
        in_specs=[pl.BlockSpec((tm, d), lambda i, j: (i, 0)), pl.BlockSpec((1, d), lambda i, j: (0, 0)),
                  chunk, chunk, chunk] + rider.in_specs(),
        out_specs=[pl.BlockSpec((tm, d), lambda i, j: (i, 0)),
                   pl.BlockSpec((tm, d), lambda i, j: (i, 0)),
                   pl.BlockSpec((tm, FF_CHUNK), lambda i, j: (i, j))] + rider.out_specs(),
        scratch_shapes=[pltpu.VMEM((tm, d), BF16), pltpu.VMEM((tm, d), F32)] + rider.scratch(),
        name=name, compiler_params=_cp())(h, g, wg_t, wu_t, w_d, *rider.srcs)


def _ffn_bwd_act(dh, hn, wg_t, wu_t, w_d, name, rider=None):
    lp, d = dh.shape
    f = w_d.shape[0]
    tm = _tile(lp, MM_ROWS, 16)
    nc = f // FF_CHUNK
    ni = lp // tm
    rider = rider or _Rider()

    def body(dh_ref, hn_ref, wg_ref, wu_ref, wd_ref, *rest):
        (dg_out, du_out), (dh_s,) = rider.split(rest, 2, 1)
        rider.start((pl.program_id(0) == 0) & (pl.program_id(1) == 0))

        @pl.when(pl.program_id(1) == 0)
        def _():
            dh_s[...] = dh_ref[...].astype(BF16)

        gg = lax.dot_general(hn_ref[...], wg_ref[...], NT, preferred_element_type=F32)
        uu = lax.dot_general(hn_ref[...], wu_ref[...], NT, preferred_element_type=F32)
        dact = lax.dot_general(dh_s[...], wd_ref[...], NT, preferred_element_type=F32)
        s = _sigmoid(gg)
        dg_out[...] = (dact * uu * (s * (1.0 + gg * (1.0 - s)))).astype(BF16)
        du_out[...] = (dact * (gg * s)).astype(BF16)
        rider.wait((pl.program_id(0) == ni - 1) & (pl.program_id(1) == nc - 1))

    chunk = pl.BlockSpec((FF_CHUNK, d), lambda i, j: (j, 0))
    out = pl.BlockSpec((tm, FF_CHUNK), lambda i, j: (i, j))
    return pl.pallas_call(
        body, out_shape=[SDS((lp, f), BF16), SDS((lp, f), BF16)] + rider.out_shape, grid=(ni, nc),
        in_specs=[pl.BlockSpec((tm, d), lambda i, j: (i, 0)), pl.BlockSpec((tm, d), lambda i, j: (i, 0)),
                  chunk, chunk, chunk] + rider.in_specs(),
        out_specs=[out, out] + rider.out_specs(), scratch_shapes=[pltpu.VMEM((tm, d), BF16)] + rider.scratch(),
        name=name, compiler_params=_cp())(dh, hn, wg_t, wu_t, w_d, *rider.srcs)


def _loss_head(h, tpad, g, n_real, name):
    lp, d = h.shape
    tm = _tile(lp, EW_ROWS, 16)

    def body(h_ref, t_ref, g_ref, dh_ref, dg_ref, loss_ref):
        i = pl.program_id(0)

        @pl.when(i == 0)
        def _():
            dg_ref[...] = jnp.zeros_like(dg_ref)
            loss_ref[...] = jnp.zeros_like(loss_ref)

        x = h_ref[...]
        r = lax.rsqrt(jnp.mean(x * x, axis=-1, keepdims=True) + EPS)
        xr = x * r
        y = xr * g_ref[...]
        row = i * tm + lax.broadcasted_iota(jnp.int32, (tm, d), 0)
        valid = (row >= N_META) & (row < N_META + n_real)
        diff = jnp.where(valid, y - t_ref[...], 0.0)
        loss_ref[...] += jnp.sum(diff * diff) * (0.5 / d)
        dy = diff * (1.0 / d)
        gy = dy * g_ref[...]
        dh_ref[...] = r * (gy - xr * jnp.mean(gy * xr, axis=-1, keepdims=True))
        dg_ref[...] += _sum8(dy * xr)

    return pl.pallas_call(
        body, out_shape=[SDS((lp, d), F32), SDS((8, d), F32), SDS((8, 128), F32)], grid=(lp // tm,),
        in_specs=[pl.BlockSpec((tm, d), lambda i: (i, 0)),
                  pl.BlockSpec((tm, d), lambda i: (i, 0)),
                  pl.BlockSpec((1, d), lambda i: (0, 0))],
        out_specs=[pl.BlockSpec((tm, d), lambda i: (i, 0)),
                   pl.BlockSpec((8, d), lambda i: (0, 0)),
                   pl.BlockSpec((8, 128), lambda i: (0, 0))],
        name=name, compiler_params=_cp())(h, tpad, g)


def _tri_consts():
    j = lax.broadcasted_iota(jnp.int32, (ABLK, ABLK), 0)
    s = lax.broadcasted_iota(jnp.int32, (ABLK, ABLK), 1)
    after = (j >= s).astype(BF16)
    before = (j < s).astype(BF16)
    ones = jnp.ones((ABLK, ABLK), BF16)
    two = lambda t: jnp.concatenate([t, t], axis=0)
    return (two(jnp.concatenate([after, ones], axis=1)),
            two(after),
            two(jnp.concatenate([before, ones], axis=1)))


def _softplus(z):
    neg_abs = lax.bitcast_convert_type(lax.bitcast_convert_type(z, jnp.uint32) | jnp.uint32(0x80000000), F32)
    return jnp.log(1.0 + jnp.exp(neg_abs)) + jnp.maximum(z, 0.0)


def _split_hi_lo(m):
    hi = m.astype(BF16)
    lo = (m - hi.astype(F32)).astype(BF16)
    return jnp.concatenate([hi, lo], axis=1)


def _head_halves(t2, in_a):
    zero = jnp.zeros_like(t2)
    return jnp.concatenate([jnp.where(in_a, t2, zero), jnp.where(in_a, zero, t2)], axis=0)


def _stack_blocks(t, nblk, in_a):
    return jnp.concatenate([_head_halves(t[u * ABLK:(u + 1) * ABLK], in_a) for u in range(nblk)], axis=0)


def _attn_scale():
    scale = 1.0 / math.sqrt(HEAD_DIM)
    assert math.frexp(scale)[0] == 0.5, "a power of two, so that scaling q in bf16 is exact"
    return scale


def _pow2_below(n):
    assert n & (n - 1) == 0
    return [p for p in (64, 32, 16, 8, 4, 2, 1) if p < n]


class _Rider:
    def __init__(self, jobs=(), srcs=(), out_shape=()):
        self.jobs, self.srcs, self.out_shape = list(jobs), list(srcs), list(out_shape)
        self.any = [pl.BlockSpec(memory_space=pl.ANY)]

    def split(self, rest, n_out, n_scratch):
        ni, no = len(self.srcs), len(self.out_shape)
        self.ins, outs = rest[:ni], rest[ni:ni + n_out]
        self.outs = rest[ni + n_out:ni + n_out + no]
        scratch = rest[ni + n_out + no:ni + n_out + no + n_scratch]
        self.sems = rest[ni + n_out + no + n_scratch:]
        return outs, scratch

    def start(self, first):
        if self.jobs:
            @pl.when(first)
            def _():
                _exchange_start(self.jobs, self.ins, self.outs, *self.sems)

    def wait(self, last):
        if self.jobs:
            @pl.when(last)
            def _():
                _exchange_wait(self.jobs, self.ins, self.outs, *self.sems)

    def in_specs(self):
        return self.any * len(self.srcs)

    def out_specs(self):
        return self.any * len(self.out_shape)

    def scratch(self):
        return _exchange_sems(len(self.jobs)) if self.jobs else []


def _attn_fwd(qkv, tri_fwd, name, rider=None):
    lp = qkv.shape[0]
    n_pairs = (N_HEADS * HEAD_DIM) // 128
    nb = lp // ABLK
    assert nb <= 128 and 2 * HEAD_DIM == 128
    scale = _attn_scale()
    nt = (((1,), (1,)), ((), ()))
    rider = rider or _Rider()

    def body(q_ref, k_ref, v_ref, tri_ref, *rest):
        (o_ref, rs_ref), (r_s, acc_s, rs_s) = rider.split(rest, 2, 3)
        rider.start(pl.program_id(0) == 0)
        lane = lax.broadcasted_iota(jnp.int32, (ABLK, 128), 1)
        row = lax.broadcasted_iota(jnp.int32, (ABLK, 128), 0)
        in_a = lane < HEAD_DIM
        causal = lane < row

        def qblock(i, carry):
            q0 = pl.multiple_of(i * ABLK, ABLK)
            q2 = q_ref[pl.ds(q0, ABLK), :] * scale
            r_s[...] = jnp.zeros_like(r_s)
            acc_s[...] = jnp.zeros_like(acc_s)
            rs_s[...] = jnp.full_like(rs_s, GONE)

            def live():
                return (jnp.min(jnp.minimum(r_s[0], r_s[1])) < EXP_ZERO_AT).astype(jnp.int32)

            def step(kb0, nblk, diag):
                k0 = pl.multiple_of(kb0 * ABLK, ABLK)
                kbd = _stack_blocks(k_ref[pl.ds(k0, nblk * ABLK), :], nblk, in_a)
                vbd = _stack_blocks(v_ref[pl.ds(k0, nblk * ABLK), :], nblk, in_a)
                z = lax.dot_general(q2, kbd, nt, preferred_element_type=F32)
                ncol = 2 * nblk
                zt = [z[:, c * 128:(c + 1) * 128] for c in range(ncol)]
                if diag:
                    zt = [jnp.where(causal, t, MASKED) if c >= ncol - 2 else t for c, t in enumerate(zt)]
                bounds = [ncol * j // ATT_SPLIT for j in range(ATT_SPLIT + 1)]
                ce = [None] * ncol
                for c0, c1 in reversed([b for b in zip(bounds[:-1], bounds[1:]) if b[0] < b[1]]):
                    parts = [_split_hi_lo(_softplus(zt[c])) for c in range(c0, c1)]
                    got = jnp.dot(jnp.concatenate(parts, axis=0), tri_ref[...], preferred_element_type=F32)
                    for c in range(c0, c1):
                        ce[c] = got[(c - c0) * 128:(c - c0 + 1) * 128]
                rr = [r_s[0], r_s[1]]
                rsv = [rs_s[:, :128], rs_s[:, 128:]]
                ws = [None] * ncol
                for u in reversed(range(nblk)):
                    for hh in range(2):
                        c = 2 * u + hh
                        ws[c] = jnp.exp(zt[c] - ce[c][:, :128] - rr[hh]).astype(BF16)
                        rsv[hh] = jnp.where(lane == kb0 + u, rr[hh], rsv[hh])
                        rr[hh] = rr[hh] + ce[c][:, 128:]
                acc_s[...] += jnp.dot(jnp.concatenate(ws, axis=1), vbd, preferred_element_type=F32)
                r_s[0] = rr[0]
                r_s[1] = rr[1]
                rs_s[:, :128] = rsv[0]
                rs_s[:, 128:] = rsv[1]

            n_top = jnp.minimum(i, ATT_TOP)
            for t in range(ATT_TOP + 1):
                @pl.when(n_top == t)
                def _():
                    step(i - t, t + 1, True)

            i_low = i - n_top
            n_grp = i_low // ATT_GROUP

            def more(c):
                return (c[0] < n_grp) & (c[1] > 0)

            def inner(c):
                step(i_low - ATT_GROUP * (c[0] + 1), ATT_GROUP, False)
                return c[0] + 1, live()

            lax.while_loop(more, inner, (jnp.int32(0), live()))
            rem = i_low - ATT_GROUP * n_grp
            for p in _pow2_below(ATT_GROUP):
                @pl.when(((rem & p) != 0) & (live() > 0))
                def _():
                    step(rem & (p - 1), p, False)

            o_ref[pl.ds(q0, ABLK), :] = acc_s[...].astype(BF16)
            rs_ref[pl.ds(q0, ABLK), :] = rs_s[...]
            return carry

        lax.fori_loop(0, nb, qblock, 0)
        rider.wait(pl.program_id(0) == n_pairs - 1)

    col = lambda o: (lambda p: (0, p + o))
    return pl.pallas_call(
        body, out_shape=[SDS((lp, n_pairs * 128), BF16), SDS((lp, n_pairs * 256), F32)] + rider.out_shape,
        grid=(n_pairs,),
        in_specs=[pl.BlockSpec((lp, 128), col(0)), pl.BlockSpec((lp, 128), col(n_pairs)),
                  pl.BlockSpec((lp, 128), col(2 * n_pairs)), pl.BlockSpec((256, 256), lambda p: (0, 0))]
        + rider.in_specs(),
        out_specs=[pl.BlockSpec((lp, 128), col(0)), pl.BlockSpec((lp, 256), col(0))] + rider.out_specs(),
        scratch_shapes=[pltpu.VMEM((2, ABLK, 128), F32), pltpu.VMEM((ABLK, 128), F32), pltpu.VMEM((ABLK, 256), F32)]
        + rider.scratch(),
        name=name, compiler_params=_cp())(qkv, qkv, qkv, tri_fwd, *rider.srcs)


def _attn_bwd(qkv, d_out, rsave, tri_after, tri_before, name, rider=None):
    lp = qkv.shape[0]
    n_pairs = (N_HEADS * HEAD_DIM) // 128
    nb = lp // ABLK
    scale = _attn_scale()
    nt = (((1,), (1,)), ((), ()))
    tn = (((0,), (0,)), ((), ()))
    rider = rider or _Rider()

    def body(q_ref, k_ref, v_ref, do_ref, rs_ref, ta_ref, tb_ref, *rest):
        (o_ref,), (dk_s, dv_s, dq_s, pc_s) = rider.split(rest, 1, 4)
        rider.start(pl.program_id(0) == 0)
        lane = lax.broadcasted_iota(jnp.int32, (ABLK, 128), 1)
        row = lax.broadcasted_iota(jnp.int32, (ABLK, 128), 0)
        in_a = lane < HEAD_DIM
        causal = lane < row
        dk_s[...] = jnp.zeros_like(dk_s)
        dv_s[...] = jnp.zeros_like(dv_s)

        def qblock(i, carry):
            q0 = pl.multiple_of(i * ABLK, ABLK)
            q2 = q_ref[pl.ds(q0, ABLK), :] * scale
            do2 = do_ref[pl.ds(q0, ABLK), :]
            q_st = _head_halves(q2, in_a)
            do_st = _head_halves(do2, in_a)
            dq_s[...] = jnp.zeros_like(dq_s)
            pc_s[...] = jnp.zeros_like(pc_s)

            def step(kb0, nblk, diag):
                k0 = pl.multiple_of(kb0 * ABLK, ABLK)
                kbd = _stack_blocks(k_ref[pl.ds(k0, nblk * ABLK), :], nblk, in_a)
                vbd = _stack_blocks(v_ref[pl.ds(k0, nblk * ABLK), :], nblk, in_a)
                z = lax.dot_general(q2, kbd, nt, preferred_element_type=F32)
                dw = lax.dot_general(do2, vbd, nt, preferred_element_type=F32)
                ncol = 2 * nblk
                zt = [z[:, c * 128:(c + 1) * 128] for c in range(ncol)]
                if diag:
                    zt = [jnp.where(causal, t, MASKED) if c >= ncol - 2 else t for c, t in enumerate(zt)]
                bounds = [ncol * j // ATT_SPLIT for j in range(ATT_SPLIT + 1)]
                batches = [range(c0, c1) for c0, c1 in zip(bounds[:-1], bounds[1:]) if c0 < c1]
                sps, ex, ws, dls, pe = [None] * ncol, [None] * ncol, [None] * ncol, [None] * ncol, [None] * ncol

                def mass(cols):
                    for c in cols:
                        sps[c] = _softplus(zt[c])
                    got = jnp.dot(jnp.concatenate([_split_hi_lo(sps[c]) for c in cols], axis=0), ta_ref[...],
                                  preferred_element_type=F32)
                    for j, c in enumerate(cols):
                        ex[c] = got[j * 128:(j + 1) * 128]

                def weights(cols):
                    for c in cols:
                        u, hh = c // 2, c % 2
                        r_saved = jnp.sum(jnp.where(lane == kb0 + u, rs_ref[pl.ds(q0, ABLK), hh * 128:(hh + 1) * 128],
                                                    0.0), axis=1, keepdims=True)
                        w = jnp.exp(zt[c] - ex[c] - r_saved)
                        ws[c] = w.astype(BF16)
                        dls[c] = dw[:, c * 128:(c + 1) * 128] * w
                    got = jnp.dot(jnp.concatenate([_split_hi_lo(dls[c]) for c in cols], axis=0), tb_ref[...],
                                  preferred_element_type=F32)
                    for j, c in enumerate(cols):
                        pe[c] = got[j * 128:(j + 1) * 128]

                mass(batches[0])
                for j in range(len(batches)):
                    if j + 1 < len(batches):
                        mass(batches[j + 1])
                    weights(batches[j])
                pc = [pc_s[0], pc_s[1]]
                dzs = []
                for c in range(ncol):
                    hh = c % 2
                    one_minus_beta = jnp.exp(-sps[c])
                    dz = dls[c] * one_minus_beta - (pe[c][:, :128] + pc[hh]) * (1.0 - one_minus_beta)
                    pc[hh] = pc[hh] + pe[c][:, 128:]
                    dzs.append(dz.astype(BF16))
                pc_s[0] = pc[0]
                pc_s[1] = pc[1]
                dq_s[...] += jnp.dot(jnp.concatenate(dzs, axis=1), kbd, preferred_element_type=F32)
                by_head = lambda ts: jnp.concatenate([jnp.concatenate(ts[0::2], axis=1), jnp.concatenate(ts[1::2], axis=1)],
                                                     axis=0)
                rows = pl.ds(k0, nblk * ABLK)
                dk_s[rows, :] += lax.dot_general(by_head(dzs), q_st, tn, preferred_element_type=F32)
                dv_s[rows, :] += lax.dot_general(by_head(ws), do_st, tn, preferred_element_type=F32)

            gone = jnp.min(rs_ref[pl.ds(q0, ABLK), :], axis=0, keepdims=True) >= EXP_ZERO_AT
            lane1 = lax.broadcasted_iota(jnp.int32, (1, 128), 1)
            first = jnp.sum(jnp.where(gone[:, :128] & gone[:, 128:] & (lane1 < i), 1.0, 0.0)).astype(jnp.int32)
            n_top = jnp.minimum(i, ATT_TOP)
            i_low = i - n_top
            n_grp = i_low // ATT_GROUP
            rem = i_low - ATT_GROUP * n_grp
            for p in reversed(_pow2_below(ATT_GROUP)):
                @pl.when(((rem & p) != 0) & ((rem & (p - 1)) + p > first))
                def _():
                    step(rem & (p - 1), p, False)

            def inner(g, c2):
                step(rem + ATT_GROUP * g, ATT_GROUP, False)
                return c2

            lax.fori_loop(jnp.maximum(first - rem, 0) // ATT_GROUP, n_grp, inner, 0)
            for t in range(ATT_TOP + 1):
                @pl.when(n_top == t)
                def _():
                    step(i - t, t + 1, True)

            o_ref[0, pl.ds(q0, ABLK), :] = (dq_s[...] * scale).astype(BF16)
            o_ref[0, pl.ds(q0, ABLK), :] = (dq_s[...] * scale).astype(BF16)
            return carry

        lax.fori_loop(0, nb, qblock, 0)
        o_ref[1] = dk_s[...].astype(BF16)
        o_ref[2] = dv_s[...].astype(BF16)
        rider.wait(pl.program_id(0) == n_pairs - 1)

    col = lambda o: (lambda p: (0, p + o))
    return pl.pallas_call(
        body, out_shape=[SDS((3, lp, n_pairs * 128), BF16)] + rider.out_shape, grid=(n_pairs,),
        in_specs=[pl.BlockSpec((lp, 128), col(0)), pl.BlockSpec((lp, 128), col(n_pairs)),
                  pl.BlockSpec((lp, 128), col(2 * n_pairs)), pl.BlockSpec((lp, 128), col(0)),
                  pl.BlockSpec((lp, 256), col(0)),
                  pl.BlockSpec((256, 128), lambda p: (0, 0)), pl.BlockSpec((256, 256), lambda p: (0, 0))]
        + rider.in_specs(),
        out_specs=[pl.BlockSpec((3, lp, 128), lambda p: (0, 0, p))] + rider.out_specs(),
        scratch_shapes=[pltpu.VMEM((lp, 128), F32), pltpu.VMEM((lp, 128), F32),
                        pltpu.VMEM((ABLK, 128), F32), pltpu.VMEM((2, ABLK, 128), F32)] + rider.scratch(),
        name=name, compiler_params=_cp())(qkv, qkv, qkv, d_out, rsave, tri_after, tri_before, *rider.srcs)


def _conv_fwd_dw(cacg, w, b, name):
    lp = cacg.shape[0]
    c = cacg.shape[1] // 2
    ncb = c // 128
    nchunk = lp // ABLK
    off = CONV_PAD - (CONV_K - 1)

    def body(a_ref, g_ref, w_ref, b_ref, y_ref, upad):
        upad[0:CONV_PAD, :] = jnp.zeros((CONV_PAD, 128), F32)

        def fill(ch, carry):
            base = pl.multiple_of(ch * ABLK, ABLK)
            upad[pl.ds(base + CONV_PAD, ABLK), :] = a_ref[pl.ds(base, ABLK), :] * _sigmoid(g_ref[pl.ds(base, ABLK), :])
            return carry

        lax.fori_loop(0, nchunk, fill, 0)

        def comp(ch, carry):
            base = pl.multiple_of(ch * ABLK, ABLK)
            acc = jnp.zeros((ABLK, 128), F32)
            for k in range(CONV_K):
                acc = acc + upad[pl.ds(base + (off + k), ABLK), :] * w_ref[k:k + 1, :]
            y_ref[pl.ds(base, ABLK), :] = acc + b_ref[...]
            return carry

        lax.fori_loop(0, nchunk, comp, 0)

    return pl.pallas_call(
        body, out_shape=SDS((lp, c), F32), grid=(ncb,),
        in_specs=[pl.BlockSpec((lp, 128), lambda j: (0, j)), pl.BlockSpec((lp, 128), lambda j: (0, j + ncb)),
                  pl.BlockSpec((CONV_PAD, 128), lambda j: (0, j)), pl.BlockSpec((1, 128), lambda j: (0, j))],
        out_specs=pl.BlockSpec((lp, 128), lambda j: (0, j)),
        scratch_shapes=[pltpu.VMEM((lp + CONV_PAD, 128), F32)], name=name, compiler_params=_cp())(cacg, cacg, w, b)


def _ln_parts(x, g, b):
    mu = jnp.mean(x, axis=-1, keepdims=True)
    xc = x - mu
    rstd = lax.rsqrt(jnp.mean(xc * xc, axis=-1, keepdims=True) + EPS)
    xh = xc * rstd
    return xh, rstd, xh * g + b


def _conv_fwd_ln(yc, g, b, name):
    lp, c = yc.shape
    tm = _tile(lp, EW_ROWS, 16)

    def body(y_ref, g_ref, b_ref, o_ref):
        _, _, ln = _ln_parts(y_ref[...], g_ref[...], b_ref[...])
        o_ref[...] = (ln * _sigmoid(ln)).astype(BF16)

    return pl.pallas_call(
        body, out_shape=SDS((lp, c), BF16), grid=(lp // tm,),
        in_specs=[pl.BlockSpec((tm, c), lambda i: (i, 0)), pl.BlockSpec((1, c), lambda i: (0, 0)),
                  pl.BlockSpec((1, c), lambda i: (0, 0))],
        out_specs=pl.BlockSpec((tm, c), lambda i: (i, 0)), name=name, compiler_params=_cp())(yc, g, b)


def _conv_bwd_ln(yc, dout, g, b, name):
    lp, c = yc.shape
    tm = _tile(lp, EW_ROWS, 16)

    def body(y_ref, d_ref, g_ref, b_ref, o_ref, dg_ref, db_ref):
        @pl.when(pl.program_id(0) == 0)
        def _():
            dg_ref[...] = jnp.zeros_like(dg_ref)
            db_ref[...] = jnp.zeros_like(db_ref)

        xh, rstd, ln = _ln_parts(y_ref[...], g_ref[...], b_ref[...])
        s = _sigmoid(ln)
        dln = d_ref[...] * (s * (1.0 + ln * (1.0 - s)))
        dg_ref[...] += _sum8(dln * xh)
        db_ref[...] += _sum8(dln)
        dxh = dln * g_ref[...]
        o_ref[...] = rstd * (dxh - jnp.mean(dxh, axis=-1, keepdims=True)
                             - xh * jnp.mean(dxh * xh, axis=-1, keepdims=True))

    return pl.pallas_call(
        body, out_shape=[SDS((lp, c), F32), SDS((8, c), F32), SDS((8, c), F32)], grid=(lp // tm,),
        in_specs=[pl.BlockSpec((tm, c), lambda i: (i, 0)), pl.BlockSpec((tm, c), lambda i: (i, 0)),
                  pl.BlockSpec((1, c), lambda i: (0, 0)), pl.BlockSpec((1, c), lambda i: (0, 0))],
        out_specs=[pl.BlockSpec((tm, c), lambda i: (i, 0)), pl.BlockSpec((8, c), lambda i: (0, 0)),
                   pl.BlockSpec((8, c), lambda i: (0, 0))],
        name=name, compiler_params=_cp())(yc, dout, g, b)


def _conv_bwd_dw(dyc, cacg, w, name):
    lp, c = dyc.shape
    ncb = c // 128
    nchunk = lp // ABLK
    off = CONV_PAD - (CONV_K - 1)

    def body(dy_ref, a_ref, g_ref, w_ref, dcc_ref, dw_ref, db_ref, upad, dypad, dwacc):
        upad[0:CONV_PAD, :] = jnp.zeros((CONV_PAD, 128), F32)
        dypad[lp:lp + CONV_PAD, :] = jnp.zeros((CONV_PAD, 128), F32)
        dwacc[...] = jnp.zeros_like(dwacc)
        db_ref[...] = jnp.zeros_like(db_ref)

        def fill(ch, carry):
            base = pl.multiple_of(ch * ABLK, ABLK)
            upad[pl.ds(base + CONV_PAD, ABLK), :] = a_ref[pl.ds(base, ABLK), :] * _sigmoid(g_ref[pl.ds(base, ABLK), :])
            dypad[pl.ds(base, ABLK), :] = dy_ref[pl.ds(base, ABLK), :]
            return carry

        lax.fori_loop(0, nchunk, fill, 0)

        def comp(ch, carry):
            base = pl.multiple_of(ch * ABLK, ABLK)
            dy = dy_ref[pl.ds(base, ABLK), :]
            du = jnp.zeros((ABLK, 128), F32)
            for k in range(CONV_K):
                du = du + dypad[pl.ds(base + (CONV_K - 1 - k), ABLK), :] * w_ref[k:k + 1, :]
                dwacc[k * 8:(k + 1) * 8, :] += _sum8(dy * upad[pl.ds(base + (off + k), ABLK), :])
            db_ref[...] += _sum8(dy)
            a = a_ref[pl.ds(base, ABLK), :]
            s = _sigmoid(g_ref[pl.ds(base, ABLK), :])
            dcc_ref[0, pl.ds(base, ABLK), :] = (du * s).astype(BF16)
            dcc_ref[1, pl.ds(base, ABLK), :] = (du * a * (s * (1.0 - s))).astype(BF16)
            return carry

        lax.fori_loop(0, nchunk, comp, 0)
        dw_ref[...] = dwacc[...].reshape(CONV_PAD, 8, 128).sum(axis=1)

    return pl.pallas_call(
        body, out_shape=[SDS((2, lp, c), BF16), SDS((CONV_PAD, c), F32), SDS((8, c), F32)],
        grid=(ncb,),
        in_specs=[pl.BlockSpec((lp, 128), lambda j: (0, j)), pl.BlockSpec((lp, 128), lambda j: (0, j)),
                  pl.BlockSpec((lp, 128), lambda j: (0, j + ncb)), pl.BlockSpec((CONV_PAD, 128), lambda j: (0, j))],
        out_specs=[pl.BlockSpec((2, lp, 128), lambda j: (0, 0, j)),
                   pl.BlockSpec((CONV_PAD, 128), lambda j: (0, j)), pl.BlockSpec((8, 128), lambda j: (0, j))],
        scratch_shapes=[pltpu.VMEM((lp + CONV_PAD, 128), F32), pltpu.VMEM((lp + CONV_PAD, 128), F32),
                        pltpu.VMEM((CONV_PAD * 8, 128), F32)],
        name=name, compiler_params=_cp())(dyc, cacg, cacg, w)


def _mesh_pos():
    x, y, c = lax.axis_index("x"), lax.axis_index("y"), lax.axis_index("c")
    return x, y, c


def _peer(pos, r):
    x, y, c = pos
    px = (1 - x) if (r >> 2) & 1 else x
    py = (1 - y) if (r >> 1) & 1 else y
    pc = (1 - c) if r & 1 else c
    return (px, py, pc), 4 * px + 2 * py + pc


class _Job:
    def __init__(self, src, dst, scatter, src_layer=None, dst_layer=None):
        self.src, self.dst, self.scatter, self.src_layer, self.dst_layer = src, dst, scatter, src_layer, dst_layer

    def src_view(self, ins, idx):
        v = ins[self.src] if self.src_layer is None else ins[self.src].at[self.src_layer]
        return v.at[idx] if self.scatter else v

    def dst_view(self, outs, slot):
        v = outs[self.dst] if self.dst_layer is None else outs[self.dst].at[self.dst_layer]
        return v.at[slot]


def _exchange_start(jobs, ins, outs, send, recv, loc):
    pos = _mesh_pos()
    me = 4 * pos[0] + 2 * pos[1] + pos[2]
    for j, job in enumerate(jobs):
        pltpu.make_async_copy(job.src_view(ins, me), job.dst_view(outs, me), loc.at[j]).start()
        for r in range(1, N_DEV):
            peer, peer_idx = _peer(pos, r)
            pltpu.make_async_remote_copy(src_ref=job.src_view(ins, peer_idx), dst_ref=job.dst_view(outs, me),
                                         send_sem=send.at[j, r - 1], recv_sem=recv.at[j, r - 1], device_id=peer,
                                         device_id_type=pl.DeviceIdType.MESH).start()


def _exchange_wait(jobs, ins, outs, send, recv, loc):
    pos = _mesh_pos()
    me = 4 * pos[0] + 2 * pos[1] + pos[2]
    for j, job in enumerate(jobs):
        for r in range(1, N_DEV):
            peer, peer_idx = _peer(pos, r)
            cp = pltpu.make_async_remote_copy(src_ref=job.src_view(ins, peer_idx), dst_ref=job.dst_view(outs, peer_idx),
                                              send_sem=send.at[j, r - 1], recv_sem=recv.at[j, r - 1], device_id=peer,
                                              device_id_type=pl.DeviceIdType.MESH)
            cp.wait_recv()
            cp.wait_send()
        pltpu.make_async_copy(job.src_view(ins, me), job.dst_view(outs, me), loc.at[j]).wait()


def _exchange_sems(n_jobs):
    return [pltpu.SemaphoreType.DMA((n_jobs, N_DEV - 1)), pltpu.SemaphoreType.DMA((n_jobs, N_DEV - 1)),
            pltpu.SemaphoreType.DMA((n_jobs,))]


def _exchange(jobs, arrs, out_shape, name):
    n_in, n_out = len(arrs), len(out_shape)
    any_spec = pl.BlockSpec(memory_space=pl.ANY)

    def body(*refs):
        ins, outs, sems = refs[:n_in], refs[n_in:n_in + n_out], refs[n_in + n_out:]
        _exchange_start(jobs, ins, outs, *sems)
        _exchange_wait(jobs, ins, outs, *sems)

    return pl.pallas_call(
        body, out_shape=out_shape, in_specs=[any_spec] * n_in, out_specs=[any_spec] * n_out,
        scratch_shapes=_exchange_sems(len(jobs)),
        name=name, compiler_params=pltpu.CompilerParams(has_side_effects=True))(*arrs)


def _all_reduce_small(p8, q, name):
    ni, _, w = p8.shape
    nq = q.shape[0]
    rows = ni + nq

    def body(p_ref, q_ref, o_ref, buf, send, recv):
        pos = _mesh_pos()
        me = 4 * pos[0] + 2 * pos[1] + pos[2]
        buf[me] = jnp.concatenate([p_ref[...].sum(axis=1), q_ref[...]], axis=0)
        for r in range(1, N_DEV):
            peer, _ = _peer(pos, r)
            pltpu.make_async_remote_copy(src_ref=buf.at[me], dst_ref=buf.at[me], send_sem=send.at[r - 1],
                                         recv_sem=recv.at[r - 1], device_id=peer,
                                         device_id_type=pl.DeviceIdType.MESH).start()
        for r in range(1, N_DEV):
            peer, peer_idx = _peer(pos, r)
            cp = pltpu.make_async_remote_copy(src_ref=buf.at[me], dst_ref=buf.at[peer_idx], send_sem=send.at[r - 1],
                                              recv_sem=recv.at[r - 1], device_id=peer,
                                              device_id_type=pl.DeviceIdType.MESH)
            cp.wait_recv()
            cp.wait_send()
        acc = buf[0]
        for dev in range(1, N_DEV):
            acc = acc + buf[dev]
        o_ref[...] = acc

    vmem = pl.BlockSpec(memory_space=pltpu.VMEM)
    return pl.pallas_call(
        body, out_shape=SDS((rows, w), F32), in_specs=[vmem, vmem], out_specs=vmem,
        scratch_shapes=[pltpu.VMEM((N_DEV, rows, w), F32), pltpu.SemaphoreType.DMA((N_DEV - 1,)),
                        pltpu.SemaphoreType.DMA((N_DEV - 1,))],
        name=name, compiler_params=pltpu.CompilerParams(has_side_effects=True))(p8, q)


def _adamw_math(w, g, m, v):
    m = ADAM_B1 * m + (1.0 - ADAM_B1) * g
    v = ADAM_B2 * v + (1.0 - ADAM_B2) * (g * g)
    m_hat = m / (1.0 - ADAM_B1 ** ADAM_STEP)
    v_hat = v / (1.0 - ADAM_B2 ** ADAM_STEP)
    delta = -ADAM_LR * (m_hat / (jnp.sqrt(v_hat) + ADAM_EPS) + ADAM_WD * w)
    return delta, m, v


def _adamw_shard(parts, w, m, v, name):
    depth = len(parts)
    _, rr, cc = parts[0].shape
    tr = _tile(rr, 256, 8)
    nt = rr // tr

    def body(*refs):
        p_refs = refs[:depth]
        w_ref, m_ref, v_ref, g_out, d_out, m_out, v_out = refs[depth:]
        for li in range(depth):
            @pl.when(pl.program_id(0) == li)
            def _(p_ref=p_refs[li]):
                g = p_ref[0].astype(F32)
                for dev in range(1, N_DEV):
                    g = g + p_ref[dev].astype(F32)
                delta, mm, vv = _adamw_math(w_ref[...], g, m_ref[...], v_ref[...])
                g_out[...] = g
                d_out[...] = delta
                m_out[...] = mm
                v_out[...] = vv

    def part_spec(li):
        return pl.BlockSpec((N_DEV, tr, cc), lambda l, i: (0, jnp.where(l == li, i, jnp.where(l < li, 0, nt - 1)), 0))

    blk = pl.BlockSpec((None, tr, cc), lambda l, i: (l, i, 0))
    return pl.pallas_call(
        body, out_shape=[SDS((depth, rr, cc), F32)] * 4, grid=(depth, nt),
        in_specs=[part_spec(li) for li in range(depth)] + [blk, blk, blk],
        out_specs=[blk] * 4, name=name, compiler_params=_cp())(*parts, w, m, v)


def _adamw_flat(g, w, m, v, name):
    def body(g_ref, w_ref, m_ref, v_ref, d_out, m_out, v_out):
        delta, mm, vv = _adamw_math(w_ref[...], g_ref[...], m_ref[...], v_ref[...])
        d_out[...] = delta
        m_out[...] = mm
        v_out[...] = vv

    return pl.pallas_call(body, out_shape=[SDS(g.shape, F32)] * 3, name=name, compiler_params=_cp())(g, w, m, v)


def _from_cols(t):
    return jnp.transpose(t, (1, 0, 2)).reshape(t.shape[1], N_DEV * t.shape[2])


def _swap(t):
    return jnp.swapaxes(t, -1, -2)


def kernel(x, meta_tokens, mix_norm_g, w_in, conv_dw_w, conv_dw_b, conv_ln_g, conv_ln_b, w_out, ffn_norm_g, w_gate, w_up, w_down, final_norm_g, loss_target, m_meta_tokens, m_mix_norm_g, m_w_in, m_conv_dw_w, m_conv_dw_b, m_conv_ln_g, m_conv_ln_b, m_w_out, m_ffn_norm_g, m_w_gate, m_w_up, m_w_down, m_final_norm_g, v_meta_tokens, v_mix_norm_g, v_w_in, v_conv_dw_w, v_conv_dw_b, v_conv_ln_g, v_conv_ln_b, v_w_out, v_ffn_norm_g, v_w_gate, v_w_up, v_w_down, v_final_norm_g):
    depth, d, in_shard = w_in.shape
    seq = x.shape[1]
    sb = N_HEADS * HEAD_DIM
    cc = conv_dw_w.shape[2] * N_DEV
    ff = w_gate.shape[2] * N_DEV
    assert in_shard * N_DEV == 3 * sb + 2 * cc and x.shape[0] == 1
    lr = N_META + seq
    lp = -(-lr // ABLK) * ABLK
    me = 4 * lax.axis_index("x") + 2 * lax.axis_index("y") + lax.axis_index("c")

    big_names = ("w_in", "w_out", "w_gate", "w_up", "w_down")
    transposed = {"w_in": True, "w_out": False, "w_gate": True, "w_up": True, "w_down": False}
    shard = dict(w_in=_swap(w_in).astype(BF16), w_out=w_out.astype(BF16), w_gate=_swap(w_gate).astype(BF16),
                 w_up=_swap(w_up).astype(BF16), w_down=w_down.astype(BF16))

    def gather_of(keys):
        names = sorted({n for n, _ in keys}, key=big_names.index)
        jobs = [_Job(names.index(n), j, False, src_layer=i) for j, (n, i) in enumerate(keys)]
        return jobs, [shard[n] for n in names], [SDS((N_DEV,) + shard[n].shape[1:], BF16) for n, _ in keys]

    first_keys = [("w_in", 0)]
    ffn_keys = lambda i: [("w_gate", i), ("w_up", i), ("w_down", i)]
    mixer_keys = lambda i: [("w_in", i), ("w_out", i)]

    def receive(keys, arrays):
        for (n, li), t in zip(keys, arrays):
            wl[li][n] = t.reshape(-1, d)

    jobs, srcs, out_shape = gather_of(first_keys)
    for extra in (meta_tokens, conv_dw_w):
        jobs.append(_Job(len(srcs), len(out_shape), False))
        srcs.append(extra)
        out_shape.append(SDS((N_DEV,) + extra.shape, F32))
    gathered = _exchange(jobs, srcs, out_shape, "gather_first")
    wl = [dict() for _ in range(depth)]
    receive(first_keys, gathered)
    meta_full = _from_cols(gathered[-2])
    taps = jnp.transpose(gathered[-1], (1, 2, 0, 3)).reshape(depth, CONV_K, cc)
    taps = jnp.pad(taps, ((0, 0), (0, CONV_PAD - CONV_K), (0, 0)))
    tri_fwd, tri_after, tri_before = _tri_consts()

    h = jnp.concatenate([meta_full, x[0], jnp.zeros((lp - lr, d), F32)], axis=0)
    saved = []
    for i in range(depth):
        p = wl[i]
        sv = dict(h_in=h)
        qkv, hn = _rms_mm(h, mix_norm_g[i:i + 1], p["w_in"], 0, 3 * sb, BF16, f"proj_qkv_{i}", True)
        cacg = _rms_mm(h, mix_norm_g[i:i + 1], p["w_in"], 3 * sb, 2 * cc, F32, f"proj_conv_{i}", False)[0]
        keys = ([("w_out", 0)] if i == 0 else []) + ffn_keys(i)
        attn, rsave, *arrived = _attn_fwd(qkv, tri_fwd, f"attn_fwd_{i}", _Rider(*gather_of(keys)))
        receive(keys, arrived)
        yc = _conv_fwd_dw(cacg, taps[i], conv_dw_b[i:i + 1], f"conv_fwd_dw_{i}")
        conv = _conv_fwd_ln(yc, conv_ln_g[i:i + 1], conv_ln_b[i:i + 1], f"conv_fwd_ln_{i}")
        h = _mix_out(attn, conv, p["w_out"], h, f"mix_out_{i}")
        sv.update(qkv=qkv, hn=hn, cacg=cacg, rsave=rsave, yc=yc, attn=attn, conv=conv, h_mid=h)
        keys = mixer_keys(i + 1) if i + 1 < depth else []
        h, hn2, act, *arrived = _ffn_fwd(h, ffn_norm_g[i:i + 1], p["w_gate"], p["w_up"], p["w_down"], f"ffn_fwd_{i}",
                                         _Rider(*gather_of(keys)) if keys else None)
        receive(keys, arrived)
        sv.update(hn2=hn2, act=act)
        saved.append(sv)

    tpad = jnp.pad(loss_target[0], ((N_META, lp - lr), (0, 0)))
    dh, dg_final, loss_part = _loss_head(h, tpad, final_norm_g.reshape(1, d), seq, "loss_head")
    loss = lax.psum(loss_part[0, 0], MESH_AXES)

    parts = {}

    def sending(items):
        srcs = [t.reshape(N_DEV, t.shape[0] // N_DEV, d) for _, t in items]
        return _Rider([_Job(j, j, True) for j in range(len(items))], srcs, [SDS(t.shape, BF16) for t in srcs])

    def arrive(items, arrays):
        parts.update({key: t for (key, _), t in zip(items, arrays)})

    grads = [None] * depth
    from_above = []
    for i in reversed(range(depth)):
        p, sv = wl[i], saved[i]
        d_g, d_u, *got = _ffn_bwd_act(dh, sv["hn2"], p["w_gate"], p["w_up"], p["w_down"], f"ffn_bwd_act_{i}",
                                      sending(from_above))
        arrive(from_above, got)
        gw_down = _mm_tn(sv["act"], dh, f"grad_w_down_{i}")
        gw_gate = _mm_tn(d_g, sv["hn2"], f"grad_w_gate_{i}")
        gw_up = _mm_tn(d_u, sv["hn2"], f"grad_w_up_{i}")
        items = [(("w_down", i), gw_down)]
        dh, dg_ffn, *got = _ffn_bwd_in(d_g, d_u, p["w_gate"], p["w_up"], sv["h_mid"], ffn_norm_g[i:i + 1], dh,
                                       f"ffn_bwd_in_{i}", sending(items))
        arrive(items, got)
        gw_out = jnp.concatenate([_mm_tn(sv["attn"], dh, f"grad_w_out_attn_{i}"),
                                  _mm_tn(sv["conv"], dh, f"grad_w_out_conv_{i}")], axis=0)
        d_attn = _mm_nt(dh, p["w_out"], 0, sb, BF16, f"mix_bwd_attn_{i}")
        d_conv = _mm_nt(dh, p["w_out"], sb, cc, F32, f"mix_bwd_conv_{i}")
        items = [(("w_gate", i), gw_gate), (("w_up", i), gw_up), (("w_out", i), gw_out)]
        dqkv, *got = _attn_bwd(sv["qkv"], d_attn, sv["rsave"], tri_after, tri_before, f"attn_bwd_{i}", sending(items))
        arrive(items, got)
        dyc, dg_ln, db_ln = _conv_bwd_ln(sv["yc"], d_conv, conv_ln_g[i:i + 1], conv_ln_b[i:i + 1], f"conv_bwd_ln_{i}")
        dcc, g_taps_i, db_conv = _conv_bwd_dw(dyc, sv["cacg"], taps[i], f"conv_bwd_dw_{i}")
        gw_in = _grad_w_in_t(dqkv, dcc, sv["hn"], f"grad_w_in_{i}")
        from_above = [(("w_in", i), gw_in)]
        items = from_above if i == 0 else []
        dh, dg_mix, *got = _mix_bwd_in(dqkv, dcc, p["w_in"], sv["h_in"], mix_norm_g[i:i + 1], dh, f"mix_bwd_in_{i}",
                                       sending(items))
        arrive(items, got)
        grads[i] = dict(taps=g_taps_i, dg_mix=dg_mix, dg_ffn=dg_ffn, dg_ln=dg_ln, db_ln=db_ln, db_conv=db_conv)
    grad_x = dh[N_META:lr][None]

    wide = lambda key: jnp.concatenate([jnp.pad(grads[i][key], ((0, 0), (0, d // depth - cc))) for i in range(depth)], axis=1)
    assert depth * cc <= d and d % depth == 0
    p8 = jnp.stack([grads[i]["dg_mix"] for i in range(depth)] + [grads[i]["dg_ffn"] for i in range(depth)]
                   + [dg_final, wide("db_conv"), wide("dg_ln"), wide("db_ln")])
    small = _all_reduce_small(p8, jnp.concatenate([wide("taps"), dh[:N_META]], axis=0), "reduce_small")
    r0 = 2 * depth + 4
    g_mix = small[0:depth]
    g_ffn = small[depth:2 * depth]
    g_final = small[2 * depth]
    narrow = lambda row: jnp.stack([row[i * (d // depth):i * (d // depth) + cc] for i in range(depth)])
    g_cb, g_lg, g_lb = narrow(small[2 * depth + 1]), narrow(small[2 * depth + 2]), narrow(small[2 * depth + 3])
    g_taps_full = jnp.stack([small[r0:r0 + CONV_K, i * (d // depth):i * (d // depth) + cc] for i in range(depth)])
    csh = cc // N_DEV
    g_taps_own = lax.dynamic_slice_in_dim(g_taps_full, me * csh, csh, axis=2)
    g_meta_full = small[r0 + CONV_PAD:r0 + CONV_PAD + N_META]
    msh = d // N_DEV
    g_meta_own = lax.dynamic_slice_in_dim(g_meta_full, me * msh, msh, axis=1)

    small_g = [g_meta_own, g_mix, g_taps_own, g_cb, g_lg, g_lb, g_ffn, g_final]
    small_w = [meta_tokens, mix_norm_g, conv_dw_w, conv_dw_b, conv_ln_g, conv_ln_b, ffn_norm_g, final_norm_g]
    small_m = [m_meta_tokens, m_mix_norm_g, m_conv_dw_w, m_conv_dw_b, m_conv_ln_g, m_conv_ln_b, m_ffn_norm_g, m_final_norm_g]
    small_v = [v_meta_tokens, v_mix_norm_g, v_conv_dw_w, v_conv_dw_b, v_conv_ln_g, v_conv_ln_b, v_ffn_norm_g, v_final_norm_g]
    sizes = [int(math.prod(t.shape)) for t in small_w]
    total = sum(sizes)
    rows = -(-total // (8 * 128)) * 8

    def flat(ts):
        v = jnp.concatenate([t.reshape(-1) for t in ts])
        return jnp.pad(v, (0, rows * 128 - total)).reshape(rows, 128)

    def unflat(t):
        v, out, o = t.reshape(-1), [], 0
        for sz, ref in zip(sizes, small_w):
            out.append(v[o:o + sz].reshape(ref.shape))
            o += sz
        return out

    sd, sm, sv_ = _adamw_flat(flat(small_g), flat(small_w), flat(small_m), flat(small_v), "adamw_small")
    s_delta, s_m, s_v = unflat(sd), unflat(sm), unflat(sv_)

    big = []
    for n, w, m, v in zip(big_names, (w_in, w_out, w_gate, w_up, w_down), (m_w_in, m_w_out, m_w_gate, m_w_up, m_w_down),
                          (v_w_in, v_w_out, v_w_gate, v_w_up, v_w_down)):
        fix = _swap if transposed[n] else (lambda t: t)
        res = _adamw_shard([parts[(n, i)] for i in range(depth)], fix(w), fix(m), fix(v), f"adamw_{n}")
        big.append([fix(t) for t in res])
    b_in, b_out, b_gate, b_up, b_down = big

    def ordered(k, smalls):
        s_meta, s_mix, s_taps, s_cb, s_lg, s_lb, s_ffn, s_final = smalls
        return [s_meta, s_mix, b_in[k], s_taps, s_cb, s_lg, s_lb, b_out[k], s_ffn, b_gate[k], b_up[k], b_down[k], s_final]

    return (loss, grad_x, *ordered(0, small_g), *ordered(1, s_delta), *ordered(2, s_m), *ordered(3, s_v))
```

```python
import functools
import math

import jax
import jax.numpy as jnp
from jax import lax
from jax.experimental import pallas as pl
from jax.experimental.pallas import tpu as pltpu

F32 = jnp.float32
BF16 = jnp.bfloat16
SDS = jax.ShapeDtypeStruct

N_META = 16
N_HEADS = 8
HEAD_DIM = 64
CONV_K = 31
CONV_PAD = 32
ABLK = 128
ATT_GROUP = 4
ATT_SPLIT = 2
ATT_TOP = 2
EXP_ZERO_AT = 104.0
GONE = 1e30
MASKED = -1e30
FF_CHUNK = 256
EPS = 1e-6
N_DEV = 8
MESH_AXES = ("x", "y", "c")
ADAM_LR = 0.001
ADAM_B1 = 0.9
ADAM_B2 = 0.999
ADAM_EPS = 1e-08
ADAM_WD = 0.01
ADAM_STEP = 10
MIB = 1 << 20
VMEM_LIMIT_MIB = 48
MM_ROWS = 1056
EW_ROWS = 528


def _cp():
    return pltpu.CompilerParams(vmem_limit_bytes=VMEM_LIMIT_MIB * MIB)


def _tile(n, cap, mult):
    best = None
    for t in range(mult, min(n, cap) + 1, mult):
        if n % t == 0:
            best = t
    assert best is not None, (n, cap, mult)
    return best


def _sum8(v):
    r, c = v.shape
    return v.reshape(r // 8, 8, c).sum(axis=0)


def _sigmoid(x):
    return 1.0 / (1.0 + jnp.exp(-x))


NT = (((1,), (1,)), ((), ()))
TN = (((0,), (0,)), ((), ()))


def _rms_mm(h, g, wt, w_row0, n_cols, out_dtype, name, hn_out, rider=None):
    lp, d = h.shape
    tm = _tile(lp, MM_ROWS, 16)
    tn = _tile(math.gcd(n_cols, w_row0), 768, 128)
    off = w_row0 // tn
    ni, nj = lp // tm, n_cols // tn
    rider = rider or _Rider()

    def body(h_ref, g_ref, w_ref, *rest):
        outs, (hn_s,) = rider.split(rest, 2 if hn_out else 1, 1)
        o_ref = outs[0]
        hn_ref = outs[1] if hn_out else None
        rider.start((pl.program_id(0) == 0) & (pl.program_id(1) == 0))

        @pl.when(pl.program_id(1) == 0)
        def _():
            x = h_ref[...]
            r = lax.rsqrt(jnp.mean(x * x, axis=-1, keepdims=True) + EPS)
            hn = ((x * r) * g_ref[...]).astype(BF16)
            hn_s[...] = hn
            if hn_out:
                hn_ref[...] = hn

        o_ref[...] = lax.dot_general(hn_s[...], w_ref[...], NT, preferred_element_type=F32).astype(out_dtype)
        rider.wait((pl.program_id(0) == ni - 1) & (pl.program_id(1) == nj - 1))

    out_shape = [SDS((lp, n_cols), out_dtype)]
    out_specs = [pl.BlockSpec((tm, tn), lambda i, j: (i, j))]
    if hn_out:
        out_shape.append(SDS((lp, d), BF16))
        out_specs.append(pl.BlockSpec((tm, d), lambda i, j: (i, 0)))
    return pl.pallas_call(
        body, out_shape=out_shape + rider.out_shape, grid=(ni, nj),
        in_specs=[pl.BlockSpec((tm, d), lambda i, j: (i, 0)),
                  pl.BlockSpec((1, d), lambda i, j: (0, 0)),
                  pl.BlockSpec((tn, d), lambda i, j: (j + off, 0))] + rider.in_specs(),
        out_specs=out_specs + rider.out_specs(), scratch_shapes=[pltpu.VMEM((tm, d), BF16)] + rider.scratch(),
        name=name, compiler_params=_cp())(h, g, wt, *rider.srcs)


def _mm_nt(a, b, b_row0, n_out, out_dtype, name):
    m, k = a.shape
    tm = _tile(m, MM_ROWS, 16)
    tn = _tile(math.gcd(n_out, b_row0), 512, 128)
    off = b_row0 // tn

    def body(a_ref, b_ref, o_ref):
        o_ref[...] = lax.dot_general(a_ref[...].astype(BF16), b_ref[...], NT,
                                     preferred_element_type=F32).astype(out_dtype)

    return pl.pallas_call(
        body, out_shape=SDS((m, n_out), out_dtype), grid=(m // tm, n_out // tn),
        in_specs=[pl.BlockSpec((tm, k), lambda i, j: (i, 0)), pl.BlockSpec((tn, k), lambda i, j: (j + off, 0))],
        out_specs=pl.BlockSpec((tm, tn), lambda i, j: (i, j)), name=name, compiler_params=_cp())(a, b)


def _mix_out(attn, conv, w, res, name):
    m, ka = attn.shape
    kc = conv.shape[1]
    n = w.shape[1]
    assert ka == kc
    tm = _tile(m, MM_ROWS, 16)
    tn = _tile(n, 512, 128)

    def body(a_ref, c_ref, wa_ref, wc_ref, r_ref, o_ref):
        o_ref[...] = (r_ref[...] + jnp.dot(a_ref[...], wa_ref[...], preferred_element_type=F32)
                      + jnp.dot(c_ref[...], wc_ref[...], preferred_element_type=F32))

    return pl.pallas_call(
        body, out_shape=SDS((m, n), F32), grid=(m // tm, n // tn),
        in_specs=[pl.BlockSpec((tm, ka), lambda i, j: (i, 0)), pl.BlockSpec((tm, kc), lambda i, j: (i, 0)),
                  pl.BlockSpec((ka, tn), lambda i, j: (0, j)), pl.BlockSpec((kc, tn), lambda i, j: (1, j)),
                  pl.BlockSpec((tm, tn), lambda i, j: (i, j))],
        out_specs=pl.BlockSpec((tm, tn), lambda i, j: (i, j)), name=name, compiler_params=_cp())(attn, conv, w, w, res)


def _mm_tn(a, b, name):
    l, m = a.shape
    n = b.shape[1]
    tm = _tile(m, 1408, 128)
    tn = _tile(n, 1024, 128)
    tl = _tile(l, 1408, 128)
    nl = l // tl

    def body(a_ref, b_ref, o_ref, acc):
        @pl.when(pl.program_id(2) == 0)
        def _():
            acc[...] = jnp.zeros_like(acc)

        acc[...] += lax.dot_general(a_ref[...].astype(BF16), b_ref[...].astype(BF16), TN, preferred_element_type=F32)

        @pl.when(pl.program_id(2) == nl - 1)
        def _():
            o_ref[...] = acc[...].astype(BF16)

    return pl.pallas_call(
        body, out_shape=SDS((m, n), BF16), grid=(m // tm, n // tn, nl),
        in_specs=[pl.BlockSpec((tl, tm), lambda i, j, s: (s, i)),
                  pl.BlockSpec((tl, tn), lambda i, j, s: (s, j))],
        out_specs=pl.BlockSpec((tm, tn), lambda i, j, s: (i, j)),
        scratch_shapes=[pltpu.VMEM((tm, tn), F32)], name=name, compiler_params=_cp())(a, b)


def _grad_w_in_t(dqkv, dcc, hn, name):
    nq, l, w = dqkv.shape
    nc = dcc.shape[0]
    d = hn.shape[1]
    assert dcc.shape[2] == w
    tn = _tile(d, 1024, 128)
    tl = _tile(l, 1408, 128)
    nl = l // tl

    def body(q_ref, c_ref, b_ref, o_ref, acc):
        p, s = pl.program_id(0), pl.program_id(2)

        @pl.when(s == 0)
        def _():
            acc[...] = jnp.zeros_like(acc)

        @pl.when(p < nq)
        def _():
            acc[...] += lax.dot_general(q_ref[...], b_ref[...], TN, preferred_element_type=F32)

        @pl.when(p >= nq)
        def _():
            acc[...] += lax.dot_general(c_ref[...], b_ref[...], TN, preferred_element_type=F32)

        @pl.when(s == nl - 1)
        def _():
            o_ref[...] = acc[...].astype(BF16)

    out = pl.pallas_call(
        body, out_shape=SDS((nq + nc, w, d), BF16), grid=(nq + nc, d // tn, nl),
        in_specs=[pl.BlockSpec((None, tl, w), lambda p, j, s: (jnp.minimum(p, nq - 1), s, 0)),
                  pl.BlockSpec((None, tl, w), lambda p, j, s: (jnp.maximum(p - nq, 0), s, 0)),
                  pl.BlockSpec((tl, tn), lambda p, j, s: (s, j))],
        out_specs=pl.BlockSpec((None, w, tn), lambda p, j, s: (p, 0, j)),
        scratch_shapes=[pltpu.VMEM((w, tn), F32)], name=name, compiler_params=_cp())(dqkv, dcc, hn)
    return out.reshape((nq + nc) * w, d)


def _rms_bwd_tail(acc, h_ref, g_ref, dres_ref, o_ref, dg_ref):
    x = h_ref[...]
    r = lax.rsqrt(jnp.mean(x * x, axis=-1, keepdims=True) + EPS)
    xr = x * r
    dyv = acc[...]
    gy = dyv * g_ref[...]
    o_ref[...] = dres_ref[...] + r * (gy - xr * jnp.mean(gy * xr, axis=-1, keepdims=True))
    dg_ref[...] += _sum8(dyv * xr)


def _mix_bwd_in(dqkv, dcc, wt, h, g, dres, name, rider=None):
    nq, lp, w = dqkv.shape
    nc = dcc.shape[0]
    d = h.shape[1]
    tm = _tile(lp, MM_ROWS, 16)
    npc = nq + nc
    ni = lp // tm
    rider = rider or _Rider()

    def body(q_ref, c_ref, w_ref, h_ref, g_ref, dres_ref, *rest):
        (o_ref, dg_ref), (acc,) = rider.split(rest, 2, 1)
        i, p = pl.program_id(0), pl.program_id(1)
        rider.start((i == 0) & (p == 0))

        @pl.when(p == 0)
        def _():
            acc[...] = jnp.zeros_like(acc)

        @pl.when((p == 0) & (i == 0))
        def _():
            dg_ref[...] = jnp.zeros_like(dg_ref)

        @pl.when(p < nq)
        def _():
            acc[...] += jnp.dot(q_ref[...], w_ref[...], preferred_element_type=F32)

        @pl.when(p >= nq)
        def _():
            acc[...] += jnp.dot(c_ref[...], w_ref[...], preferred_element_type=F32)

        @pl.when(p == npc - 1)
        def _():
            _rms_bwd_tail(acc, h_ref, g_ref, dres_ref, o_ref, dg_ref)

        rider.wait((i == ni - 1) & (p == npc - 1))

    return pl.pallas_call(
        body, out_shape=[SDS((lp, d), F32), SDS((8, d), F32)] + rider.out_shape, grid=(ni, npc),
        in_specs=[pl.BlockSpec((None, tm, w), lambda i, p: (jnp.minimum(p, nq - 1), i, 0)),
                  pl.BlockSpec((None, tm, w), lambda i, p: (jnp.maximum(p - nq, 0), i, 0)),
                  pl.BlockSpec((w, d), lambda i, p: (p, 0)),
                  pl.BlockSpec((tm, d), lambda i, p: (i, 0)),
                  pl.BlockSpec((1, d), lambda i, p: (0, 0)),
                  pl.BlockSpec((tm, d), lambda i, p: (i, 0))] + rider.in_specs(),
        out_specs=[pl.BlockSpec((tm, d), lambda i, p: (i, 0)), pl.BlockSpec((8, d), lambda i, p: (0, 0))]
        + rider.out_specs(),
        scratch_shapes=[pltpu.VMEM((tm, d), F32)] + rider.scratch(), name=name,
        compiler_params=_cp())(dqkv, dcc, wt, h, g, dres, *rider.srcs)


def _ffn_bwd_in(d_g, d_u, wg_t, wu_t, h, g, dres, name, rider=None):
    lp, f = d_g.shape
    d = h.shape[1]
    tm = _tile(lp, MM_ROWS, 16)
    tk = _tile(f, 512, 128)
    nk = f // tk
    ni = lp // tm
    rider = rider or _Rider()

    def body(dg_in, du_in, wg_ref, wu_ref, h_ref, g_ref, dres_ref, *rest):
        (o_ref, dg_ref), (acc,) = rider.split(rest, 2, 1)
        i, kk = pl.program_id(0), pl.program_id(1)
        rider.start((i == 0) & (kk == 0))

        @pl.when(kk == 0)
        def _():
            acc[...] = jnp.zeros_like(acc)

        @pl.when((kk == 0) & (i == 0))
        def _():
            dg_ref[...] = jnp.zeros_like(dg_ref)

        acc[...] += (jnp.dot(dg_in[...], wg_ref[...], preferred_element_type=F32)
                     + jnp.dot(du_in[...], wu_ref[...], preferred_element_type=F32))

        @pl.when(kk == nk - 1)
        def _():
            _rms_bwd_tail(acc, h_ref, g_ref, dres_ref, o_ref, dg_ref)

        rider.wait((i == ni - 1) & (kk == nk - 1))

    return pl.pallas_call(
        body, out_shape=[SDS((lp, d), F32), SDS((8, d), F32)] + rider.out_shape, grid=(ni, nk),
        in_specs=[pl.BlockSpec((tm, tk), lambda i, j: (i, j)), pl.BlockSpec((tm, tk), lambda i, j: (i, j)),
                  pl.BlockSpec((tk, d), lambda i, j: (j, 0)), pl.BlockSpec((tk, d), lambda i, j: (j, 0)),
                  pl.BlockSpec((tm, d), lambda i, j: (i, 0)),
                  pl.BlockSpec((1, d), lambda i, j: (0, 0)),
                  pl.BlockSpec((tm, d), lambda i, j: (i, 0))] + rider.in_specs(),
        out_specs=[pl.BlockSpec((tm, d), lambda i, j: (i, 0)), pl.BlockSpec((8, d), lambda i, j: (0, 0))]
        + rider.out_specs(),
        scratch_shapes=[pltpu.VMEM((tm, d), F32)] + rider.scratch(), name=name,
        compiler_params=_cp())(d_g, d_u, wg_t, wu_t, h, g, dres, *rider.srcs)


def _ffn_fwd(h, g, wg_t, wu_t, w_d, name, rider=None):
    lp, d = h.shape
    f = w_d.shape[0]
    tm = _tile(lp, MM_ROWS, 16)
    nc = f // FF_CHUNK
    ni = lp // tm
    rider = rider or _Rider()

    def body(h_ref, g_ref, wg_ref, wu_ref, wd_ref, *rest):
        (o_ref, hn_ref, act_ref), (hn_s, acc) = rider.split(rest, 3, 2)
        c = pl.program_id(1)
        rider.start((pl.program_id(0) == 0) & (c == 0))

        @pl.when(c == 0)
        def _():
            x = h_ref[...]
            r = lax.rsqrt(jnp.mean(x * x, axis=-1, keepdims=True) + EPS)
            hn = ((x * r) * g_ref[...]).astype(BF16)
            hn_s[...] = hn
            hn_ref[...] = hn
            acc[...] = jnp.zeros_like(acc)

        gg = lax.dot_general(hn_s[...], wg_ref[...], NT, preferred_element_type=F32)
        uu = lax.dot_general(hn_s[...], wu_ref[...], NT, preferred_element_type=F32)
        act = ((gg * _sigmoid(gg)) * uu).astype(BF16)
        act_ref[...] = act
        acc[...] += jnp.dot(act, wd_ref[...], preferred_element_type=F32)

        @pl.when(c == nc - 1)
        def _():
            o_ref[...] = h_ref[...] + acc[...]

        rider.wait((pl.program_id(0) == ni - 1) & (c == nc - 1))

    chunk = pl.BlockSpec((FF_CHUNK, d), lambda i, j: (j, 0))
    return pl.pallas_call(
        body, out_shape=[SDS((lp, d), F32), SDS((lp, d), BF16), SDS((lp, f), BF16)] + rider.out_shape, grid=(ni, nc),
        in_specs=[pl.BlockSpec((tm, d), lambda i, j: (i, 0)), pl.BlockSpec((1, d), lambda i, j: (0, 0)),
                  chunk, chunk, chunk] + rider.in_specs(),
        out_specs=[pl.BlockSpec((tm, d), lambda i, j: (i, 0)),
                   pl.BlockSpec((tm, d), lambda i, j: (i, 0)),
                   pl.BlockSpec((tm, FF_CHUNK), lambda i, j: (i, j))] + rider.out_specs(),
        scratch_shapes=[pltpu.VMEM((tm, d), BF16), pltpu.VMEM((tm, d), F32)] + rider.scratch(),
        name=name, compiler_params=_cp())(h, g, wg_t, wu_t, w_d, *rider.srcs)


def _ffn_bwd_act(dh, hn, wg_t, wu_t, w_d, name, rider=None):
    lp, d = dh.shape
    f = w_d.shape[0]
    tm = _tile(lp, MM_ROWS, 16)
    nc = f // FF_CHUNK
    ni = lp // tm
    rider = rider or _Rider()

    def body(dh_ref, hn_ref, wg_ref, wu_ref, wd_ref, *rest):
        (dg_out, du_out), (dh_s,) = rider.split(rest, 2, 1)
        rider.start((pl.program_id(0) == 0) & (pl.program_id(1) == 0))

        @pl.when(pl.program_id(1) == 0)
        def _():
            dh_s[...] = dh_ref[...].astype(BF16)

        gg = lax.dot_general(hn_ref[...], wg_ref[...], NT, preferred_element_type=F32)
        uu = lax.dot_general(hn_ref[...], wu_ref[...], NT, preferred_element_type=F32)
        dact = lax.dot_general(dh_s[...], wd_ref[...], NT, preferred_element_type=F32)
        s = _sigmoid(gg)
        dg_out[...] = (dact * uu * (s * (1.0 + gg * (1.0 - s)))).astype(BF16)
        du_out[...] = (dact * (gg * s)).astype(BF16)
        rider.wait((pl.program_id(0) == ni - 1) & (pl.program_id(1) == nc - 1))

    chunk = pl.BlockSpec((FF_CHUNK, d), lambda i, j: (j, 0))
    out = pl.BlockSpec((tm, FF_CHUNK), lambda i, j: (i, j))
    return pl.pallas_call(
        body, out_shape=[SDS((lp, f), BF16), SDS((lp, f), BF16)] + rider.out_shape, grid=(ni, nc),
        in_specs=[pl.BlockSpec((tm, d), lambda i, j: (i, 0)), pl.BlockSpec((tm, d), lambda i, j: (i, 0)),
                  chunk, chunk, chunk] + rider.in_specs(),
        out_specs=[out, out] + rider.out_specs(), scratch_shapes=[pltpu.VMEM((tm, d), BF16)] + rider.scratch(),
        name=name, compiler_params=_cp())(dh, hn, wg_t, wu_t, w_d, *rider.srcs)


def _loss_head(h, tpad, g, n_real, name):
    lp, d = h.shape
    tm = _tile(lp, EW_ROWS, 16)

    def body(h_ref, t_ref, g_ref, dh_ref, dg_ref, loss_ref):
        i = pl.program_id(0)

        @pl.when(i == 0)
        def _():
            dg_ref[...] = jnp.zeros_like(dg_ref)
            loss_ref[...] = jnp.zeros_like(loss_ref)

        x = h_ref[...]
        r = lax.rsqrt(jnp.mean(x * x, axis=-1, keepdims=True) + EPS)
        xr = x * r
        y = xr * g_ref[...]
        row = i * tm + lax.broadcasted_iota(jnp.int32, (tm, d), 0)
        valid = (row >= N_META) & (row < N_META + n_real)
        diff = jnp.where(valid, y - t_ref[...], 0.0)
        loss_ref[...] += jnp.sum(diff * diff) * (0.5 / d)
        dy = diff * (1.0 / d)
        gy = dy * g_ref[...]
        dh_ref[...] = r * (gy - xr * jnp.mean(gy * xr, axis=-1, keepdims=True))
        dg_ref[...] += _sum8(dy * xr)

    return pl.pallas_call(
        body, out_shape=[SDS((lp, d), F32), SDS((8, d), F32), SDS((8, 128), F32)], grid=(lp // tm,),
        in_specs=[pl.BlockSpec((tm, d), lambda i: (i, 0)),
                  pl.BlockSpec((tm, d), lambda i: (i, 0)),
                  pl.BlockSpec((1, d), lambda i: (0, 0))],
        out_specs=[pl.BlockSpec((tm, d), lambda i: (i, 0)),
                   pl.BlockSpec((8, d), lambda i: (0, 0)),
                   pl.BlockSpec((8, 128), lambda i: (0, 0))],
        name=name, compiler_params=_cp())(h, tpad, g)


def _tri_consts():
    j = lax.broadcasted_iota(jnp.int32, (ABLK, ABLK), 0)
    s = lax.broadcasted_iota(jnp.int32, (ABLK, ABLK), 1)
    after = (j >= s).astype(BF16)
    before = (j < s).astype(BF16)
    ones = jnp.ones((ABLK, ABLK), BF16)
    two = lambda t: jnp.concatenate([t, t], axis=0)
    return (two(jnp.concatenate([after, ones], axis=1)),
            two(after),
            two(jnp.concatenate([before, ones], axis=1)))


def _softplus(z):
    neg_abs = lax.bitcast_convert_type(lax.bitcast_convert_type(z, jnp.uint32) | jnp.uint32(0x80000000), F32)
    return jnp.log(1.0 + jnp.exp(neg_abs)) + jnp.maximum(z, 0.0)


def _split_hi_lo(m):
    hi = m.astype(BF16)
    lo = (m - hi.astype(F32)).astype(BF16)
    return jnp.concatenate([hi, lo], axis=1)


def _head_halves(t2, in_a):
    zero = jnp.zeros_like(t2)
    return jnp.concatenate([jnp.where(in_a, t2, zero), jnp.where(in_a, zero, t2)], axis=0)


def _stack_blocks(t, nblk, in_a):
    return jnp.concatenate([_head_halves(t[u * ABLK:(u + 1) * ABLK], in_a) for u in range(nblk)], axis=0)


def _attn_scale():
    scale = 1.0 / math.sqrt(HEAD_DIM)
    assert math.frexp(scale)[0] == 0.5, "a power of two, so that scaling q in bf16 is exact"
    return scale


def _pow2_below(n):
    assert n & (n - 1) == 0
    return [p for p in (64, 32, 16, 8, 4, 2, 1) if p < n]


class _Rider:
    def __init__(self, jobs=(), srcs=(), out_shape=()):
        self.jobs, self.srcs, self.out_shape = list(jobs), list(srcs), list(out_shape)
        self.any = [pl.BlockSpec(memory_space=pl.ANY)]

    def split(self, rest, n_out, n_scratch):
        ni, no = len(self.srcs), len(self.out_shape)
        self.ins, outs = rest[:ni], rest[ni:ni + n_out]
        self.outs = rest[ni + n_out:ni + n_out + no]
        scratch = rest[ni + n_out + no:ni + n_out + no + n_scratch]
        self.sems = rest[ni + n_out + no + n_scratch:]
        return outs, scratch

    def start(self, first):
        if self.jobs:
            @pl.when(first)
            def _():
                _exchange_start(self.jobs, self.ins, self.outs, *self.sems)

    def wait(self, last):
        if self.jobs:
            @pl.when(last)
            def _():
                _exchange_wait(self.jobs, self.ins, self.outs, *self.sems)

    def in_specs(self):
        return self.any * len(self.srcs)

    def out_specs(self):
        return self.any * len(self.out_shape)

    def scratch(self):
        return _exchange_sems(len(self.jobs)) if self.jobs else []


def _attn_fwd(qkv, tri_fwd, name, rider=None):
    lp = qkv.shape[0]
    n_pairs = (N_HEADS * HEAD_DIM) // 128
    nb = lp // ABLK
    assert nb <= 128 and 2 * HEAD_DIM == 128
    scale = _attn_scale()
    nt = (((1,), (1,)), ((), ()))
    rider = rider or _Rider()

    def body(q_ref, k_ref, v_ref, tri_ref, *rest):
        (o_ref, rs_ref), (r_s, acc_s, rs_s) = rider.split(rest, 2, 3)
        rider.start(pl.program_id(0) == 0)
        lane = lax.broadcasted_iota(jnp.int32, (ABLK, 128), 1)
        row = lax.broadcasted_iota(jnp.int32, (ABLK, 128), 0)
        in_a = lane < HEAD_DIM
        causal = lane < row

        def qblock(i, carry):
            q0 = pl.multiple_of(i * ABLK, ABLK)
            q2 = q_ref[pl.ds(q0, ABLK), :] * scale
            r_s[...] = jnp.zeros_like(r_s)
            acc_s[...] = jnp.zeros_like(acc_s)
            rs_s[...] = jnp.full_like(rs_s, GONE)

            def live():
                least = jnp.min(jnp.minimum(r_s[0], r_s[1]), axis=0, keepdims=True)
                return (least[0, 0] < EXP_ZERO_AT).astype(jnp.int32)

            def step(kb0, nblk, diag):
                k0 = pl.multiple_of(kb0 * ABLK, ABLK)
                kbd = _stack_blocks(k_ref[pl.ds(k0, nblk * ABLK), :], nblk, in_a)
                vbd = _stack_blocks(v_ref[pl.ds(k0, nblk * ABLK), :], nblk, in_a)
                z = lax.dot_general(q2, kbd, nt, preferred_element_type=F32)
                ncol = 2 * nblk
                zt = [z[:, c * 128:(c + 1) * 128] for c in range(ncol)]
                if diag:
                    zt = [jnp.where(causal, t, MASKED) if c >= ncol - 2 else t for c, t in enumerate(zt)]
                bounds = [ncol * j // ATT_SPLIT for j in range(ATT_SPLIT + 1)]
                ce = [None] * ncol
                for c0, c1 in reversed([b for b in zip(bounds[:-1], bounds[1:]) if b[0] < b[1]]):
                    parts = [_split_hi_lo(_softplus(zt[c])) for c in range(c0, c1)]
                    got = jnp.dot(jnp.concatenate(parts, axis=0), tri_ref[...], preferred_element_type=F32)
                    for c in range(c0, c1):
                        ce[c] = got[(c - c0) * 128:(c - c0 + 1) * 128]
                rr = [r_s[0], r_s[1]]
                rsv = [rs_s[:, :128], rs_s[:, 128:]]
                ws = [None] * ncol
                for u in reversed(range(nblk)):
                    for hh in range(2):
                        c = 2 * u + hh
                        ws[c] = jnp.exp(zt[c] - ce[c][:, :128] - rr[hh]).astype(BF16)
                        rsv[hh] = jnp.where(lane == kb0 + u, rr[hh], rsv[hh])
                        rr[hh] = rr[hh] + ce[c][:, 128:]
                acc_s[...] += jnp.dot(jnp.concatenate(ws, axis=1), vbd, preferred_element_type=F32)
                r_s[0] = rr[0]
                r_s[1] = rr[1]
                rs_s[:, :128] = rsv[0]
                rs_s[:, 128:] = rsv[1]

            n_top = jnp.minimum(i, ATT_TOP)
            for t in range(ATT_TOP + 1):
                @pl.when(n_top == t)
                def _():
                    step(i - t, t + 1, True)

            i_low = i - n_top
            n_grp = i_low // ATT_GROUP

            def more(c):
                return (c[0] < n_grp) & (c[1] > 0)

            def inner(c):
                step(i_low - ATT_GROUP * (c[0] + 1), ATT_GROUP, False)
                return c[0] + 1, live()

            _, alive = lax.while_loop(more, inner, (jnp.int32(0), live()))
            rem = i_low - ATT_GROUP * n_grp
            for p in _pow2_below(ATT_GROUP):
                def last_steps(p=p):
                    step(rem & (p - 1), p, False)
                    return live()

                alive = lax.cond(((rem & p) != 0) & (alive > 0), last_steps, lambda alive=alive: alive)

            o_ref[pl.ds(q0, ABLK), :] = acc_s[...].astype(BF16)
            rs_ref[pl.ds(q0, ABLK), :] = rs_s[...]
            return carry

        lax.fori_loop(0, nb, qblock, 0)
        rider.wait(pl.program_id(0) == n_pairs - 1)

    col = lambda o: (lambda p: (0, p + o))
    return pl.pallas_call(
        body, out_shape=[SDS((lp, n_pairs * 128), BF16), SDS((lp, n_pairs * 256), F32)] + rider.out_shape,
        grid=(n_pairs,),
        in_specs=[pl.BlockSpec((lp, 128), col(0)), pl.BlockSpec((lp, 128), col(n_pairs)),
                  pl.BlockSpec((lp, 128), col(2 * n_pairs)), pl.BlockSpec((256, 256), lambda p: (0, 0))]
        + rider.in_specs(),
        out_specs=[pl.BlockSpec((lp, 128), col(0)), pl.BlockSpec((lp, 256), col(0))] + rider.out_specs(),
        scratch_shapes=[pltpu.VMEM((2, ABLK, 128), F32), pltpu.VMEM((ABLK, 128), F32), pltpu.VMEM((ABLK, 256), F32)]
        + rider.scratch(),
        name=name, compiler_params=_cp())(qkv, qkv, qkv, tri_fwd, *rider.srcs)


def _attn_bwd(qkv, d_out, rsave, tri_after, tri_before, name, rider=None):
    lp = qkv.shape[0]
    n_pairs = (N_HEADS * HEAD_DIM) // 128
    nb = lp // ABLK
    scale = _attn_scale()
    nt = (((1,), (1,)), ((), ()))
    tn = (((0,), (0,)), ((), ()))
    rider = rider or _Rider()

    def body(q_ref, k_ref, v_ref, do_ref, rs_ref, ta_ref, tb_ref, *rest):
        (o_ref,), (dk_s, dv_s, dq_s, pc_s) = rider.split(rest, 1, 4)
        rider.start(pl.program_id(0) == 0)
        lane = lax.broadcasted_iota(jnp.int32, (ABLK, 128), 1)
        row = lax.broadcasted_iota(jnp.int32, (ABLK, 128), 0)
        in_a = lane < HEAD_DIM
        causal = lane < row
        dk_s[...] = jnp.zeros_like(dk_s)
        dv_s[...] = jnp.zeros_like(dv_s)

        def qblock(i, carry):
            q0 = pl.multiple_of(i * ABLK, ABLK)
            q2 = q_ref[pl.ds(q0, ABLK), :] * scale
            do2 = do_ref[pl.ds(q0, ABLK), :]
            q_st = _head_halves(q2, in_a)
            do_st = _head_halves(do2, in_a)
            dq_s[...] = jnp.zeros_like(dq_s)
            pc_s[...] = jnp.zeros_like(pc_s)

            def step(kb0, nblk, diag):
                k0 = pl.multiple_of(kb0 * ABLK, ABLK)
                kbd = _stack_blocks(k_ref[pl.ds(k0, nblk * ABLK), :], nblk, in_a)
                vbd = _stack_blocks(v_ref[pl.ds(k0, nblk * ABLK), :], nblk, in_a)
                z = lax.dot_general(q2, kbd, nt, preferred_element_type=F32)
                dw = lax.dot_general(do2, vbd, nt, preferred_element_type=F32)
                ncol = 2 * nblk
                zt = [z[:, c * 128:(c + 1) * 128] for c in range(ncol)]
                if diag:
                    zt = [jnp.where(causal, t, MASKED) if c >= ncol - 2 else t for c, t in enumerate(zt)]
                bounds = [ncol * j // ATT_SPLIT for j in range(ATT_SPLIT + 1)]
                batches = [range(c0, c1) for c0, c1 in zip(bounds[:-1], bounds[1:]) if c0 < c1]
                sps, ex, ws, dls, pe = [None] * ncol, [None] * ncol, [None] * ncol, [None] * ncol, [None] * ncol

                def mass(cols):
                    for c in cols:
                        sps[c] = _softplus(zt[c])
                    got = jnp.dot(jnp.concatenate([_split_hi_lo(sps[c]) for c in cols], axis=0), ta_ref[...],
                                  preferred_element_type=F32)
                    for j, c in enumerate(cols):
                        ex[c] = got[j * 128:(j + 1) * 128]

                def weights(cols):
                    for c in cols:
                        u, hh = c // 2, c % 2
                        r_saved = jnp.sum(jnp.where(lane == kb0 + u, rs_ref[pl.ds(q0, ABLK), hh * 128:(hh + 1) * 128],
                                                    0.0), axis=1, keepdims=True)
                        w = jnp.exp(zt[c] - ex[c] - r_saved)
                        ws[c] = w.astype(BF16)
                        dls[c] = dw[:, c * 128:(c + 1) * 128] * w
                    got = jnp.dot(jnp.concatenate([_split_hi_lo(dls[c]) for c in cols], axis=0), tb_ref[...],
                                  preferred_element_type=F32)
                    for j, c in enumerate(cols):
                        pe[c] = got[j * 128:(j + 1) * 128]

                mass(batches[0])
                for j in range(len(batches)):
                    if j + 1 < len(batches):
                        mass(batches[j + 1])
                    weights(batches[j])
                pc = [pc_s[0], pc_s[1]]
                dzs = []
                for c in range(ncol):
                    hh = c % 2
                    one_minus_beta = jnp.exp(-sps[c])
                    dz = dls[c] * one_minus_beta - (pe[c][:, :128] + pc[hh]) * (1.0 - one_minus_beta)
                    pc[hh] = pc[hh] + pe[c][:, 128:]
                    dzs.append(dz.astype(BF16))
                pc_s[0] = pc[0]
                pc_s[1] = pc[1]
                dq_s[...] += jnp.dot(jnp.concatenate(dzs, axis=1), kbd, preferred_element_type=F32)
                by_head = lambda ts: jnp.concatenate([jnp.concatenate(ts[0::2], axis=1), jnp.concatenate(ts[1::2], axis=1)],
                                                     axis=0)
                rows = pl.ds(k0, nblk * ABLK)
                dk_s[rows, :] += lax.dot_general(by_head(dzs), q_st, tn, preferred_element_type=F32)
                dv_s[rows, :] += lax.dot_general(by_head(ws), do_st, tn, preferred_element_type=F32)

            gone = jnp.min(rs_ref[pl.ds(q0, ABLK), :], axis=0, keepdims=True) >= EXP_ZERO_AT
            lane1 = lax.broadcasted_iota(jnp.int32, (1, 128), 1)
            first = jnp.sum(jnp.where(gone[:, :128] & gone[:, 128:] & (lane1 < i), 1.0, 0.0)).astype(jnp.int32)
            n_top = jnp.minimum(i, ATT_TOP)
            i_low = i - n_top
            n_grp = i_low // ATT_GROUP
            rem = i_low - ATT_GROUP * n_grp
            for p in reversed(_pow2_below(ATT_GROUP)):
                @pl.when(((rem & p) != 0) & ((rem & (p - 1)) + p > first))
                def _():
                    step(rem & (p - 1), p, False)

            def inner(g, c2):
                step(rem + ATT_GROUP * g, ATT_GROUP, False)
                return c2

            lax.fori_loop(jnp.maximum(first - rem, 0) // ATT_GROUP, n_grp, inner, 0)
            for t in range(ATT_TOP + 1):
                @pl.when(n_top == t)
                def _():
                    step(i - t, t + 1, True)

            o_ref[0, pl.ds(q0, ABLK), :] = (dq_s[...] * scale).astype(BF16)
            o_ref[0, pl.ds(q0, ABLK), :] = (dq_s[...] * scale).astype(BF16)
            return carry

        lax.fori_loop(0, nb, qblock, 0)
        o_ref[1] = dk_s[...].astype(BF16)
        o_ref[2] = dv_s[...].astype(BF16)
        rider.wait(pl.program_id(0) == n_pairs - 1)

    col = lambda o: (lambda p: (0, p + o))
    return pl.pallas_call(
        body, out_shape=[SDS((3, lp, n_pairs * 128), BF16)] + rider.out_shape, grid=(n_pairs,),
        in_specs=[pl.BlockSpec((lp, 128), col(0)), pl.BlockSpec((lp, 128), col(n_pairs)),
                  pl.BlockSpec((lp, 128), col(2 * n_pairs)), pl.BlockSpec((lp, 128), col(0)),
                  pl.BlockSpec((lp, 256), col(0)),
                  pl.BlockSpec((256, 128), lambda p: (0, 0)), pl.BlockSpec((256, 256), lambda p: (0, 0))]
        + rider.in_specs(),
        out_specs=[pl.BlockSpec((3, lp, 128), lambda p: (0, 0, p))] + rider.out_specs(),
        scratch_shapes=[pltpu.VMEM((lp, 128), F32), pltpu.VMEM((lp, 128), F32),
                        pltpu.VMEM((ABLK, 128), F32), pltpu.VMEM((2, ABLK, 128), F32)] + rider.scratch(),
        name=name, compiler_params=_cp())(qkv, qkv, qkv, d_out, rsave, tri_after, tri_before, *rider.srcs)


def _conv_fwd_dw(cacg, w, b, name):
    lp = cacg.shape[0]
    c = cacg.shape[1] // 2
    ncb = c // 128
    nchunk = lp // ABLK
    off = CONV_PAD - (CONV_K - 1)

    def body(a_ref, g_ref, w_ref, b_ref, y_ref, upad):
        upad[0:CONV_PAD, :] = jnp.zeros((CONV_PAD, 128), F32)

        def fill(ch, carry):
            base = pl.multiple_of(ch * ABLK, ABLK)
            upad[pl.ds(base + CONV_PAD, ABLK), :] = a_ref[pl.ds(base, ABLK), :] * _sigmoid(g_ref[pl.ds(base, ABLK), :])
            return carry

        lax.fori_loop(0, nchunk, fill, 0)

        def comp(ch, carry):
            base = pl.multiple_of(ch * ABLK, ABLK)
            acc = jnp.zeros((ABLK, 128), F32)
            for k in range(CONV_K):
                acc = acc + upad[pl.ds(base + (off + k), ABLK), :] * w_ref[k:k + 1, :]
            y_ref[pl.ds(base, ABLK), :] = acc + b_ref[...]
            return carry

        lax.fori_loop(0, nchunk, comp, 0)

    return pl.pallas_call(
        body, out_shape=SDS((lp, c), F32), grid=(ncb,),
        in_specs=[pl.BlockSpec((lp, 128), lambda j: (0, j)), pl.BlockSpec((lp, 128), lambda j: (0, j + ncb)),
                  pl.BlockSpec((CONV_PAD, 128), lambda j: (0, j)), pl.BlockSpec((1, 128), lambda j: (0, j))],
        out_specs=pl.BlockSpec((lp, 128), lambda j: (0, j)),
        scratch_shapes=[pltpu.VMEM((lp + CONV_PAD, 128), F32)], name=name, compiler_params=_cp())(cacg, cacg, w, b)


def _ln_parts(x, g, b):
    mu = jnp.mean(x, axis=-1, keepdims=True)
    xc = x - mu
    rstd = lax.rsqrt(jnp.mean(xc * xc, axis=-1, keepdims=True) + EPS)
    xh = xc * rstd
    return xh, rstd, xh * g + b


def _conv_fwd_ln(yc, g, b, name):
    lp, c = yc.shape
    tm = _tile(lp, EW_ROWS, 16)

    def body(y_ref, g_ref, b_ref, o_ref):
        _, _, ln = _ln_parts(y_ref[...], g_ref[...], b_ref[...])
        o_ref[...] = (ln * _sigmoid(ln)).astype(BF16)

    return pl.pallas_call(
        body, out_shape=SDS((lp, c), BF16), grid=(lp // tm,),
        in_specs=[pl.BlockSpec((tm, c), lambda i: (i, 0)), pl.BlockSpec((1, c), lambda i: (0, 0)),
                  pl.BlockSpec((1, c), lambda i: (0, 0))],
        out_specs=pl.BlockSpec((tm, c), lambda i: (i, 0)), name=name, compiler_params=_cp())(yc, g, b)


def _conv_bwd_ln(yc, dout, g, b, name):
    lp, c = yc.shape
    tm = _tile(lp, EW_ROWS, 16)

    def body(y_ref, d_ref, g_ref, b_ref, o_ref, dg_ref, db_ref):
        @pl.when(pl.program_id(0) == 0)
        def _():
            dg_ref[...] = jnp.zeros_like(dg_ref)
            db_ref[...] = jnp.zeros_like(db_ref)

        xh, rstd, ln = _ln_parts(y_ref[...], g_ref[...], b_ref[...])
        s = _sigmoid(ln)
        dln = d_ref[...] * (s * (1.0 + ln * (1.0 - s)))
        dg_ref[...] += _sum8(dln * xh)
        db_ref[...] += _sum8(dln)
        dxh = dln * g_ref[...]
        o_ref[...] = rstd * (dxh - jnp.mean(dxh, axis=-1, keepdims=True)
                             - xh * jnp.mean(dxh * xh, axis=-1, keepdims=True))

    return pl.pallas_call(
        body, out_shape=[SDS((lp, c), F32), SDS((8, c), F32), SDS((8, c), F32)], grid=(lp // tm,),
        in_specs=[pl.BlockSpec((tm, c), lambda i: (i, 0)), pl.BlockSpec((tm, c), lambda i: (i, 0)),
                  pl.BlockSpec((1, c), lambda i: (0, 0)), pl.BlockSpec((1, c), lambda i: (0, 0))],
        out_specs=[pl.BlockSpec((tm, c), lambda i: (i, 0)), pl.BlockSpec((8, c), lambda i: (0, 0)),
                   pl.BlockSpec((8, c), lambda i: (0, 0))],
        name=name, compiler_params=_cp())(yc, dout, g, b)


def _conv_bwd_dw(dyc, cacg, w, name):
    lp, c = dyc.shape
    ncb = c // 128
    nchunk = lp // ABLK
    off = CONV_PAD - (CONV_K - 1)

    def body(dy_ref, a_ref, g_ref, w_ref, dcc_ref, dw_ref, db_ref, upad, dypad, dwacc):
        upad[0:CONV_PAD, :] = jnp.zeros((CONV_PAD, 128), F32)
        dypad[lp:lp + CONV_PAD, :] = jnp.zeros((CONV_PAD, 128), F32)
        dwacc[...] = jnp.zeros_like(dwacc)
        db_ref[...] = jnp.zeros_like(db_ref)

        def fill(ch, carry):
            base = pl.multiple_of(ch * ABLK, ABLK)
            upad[pl.ds(base + CONV_PAD, ABLK), :] = a_ref[pl.ds(base, ABLK), :] * _sigmoid(g_ref[pl.ds(base, ABLK), :])
            dypad[pl.ds(base, ABLK), :] = dy_ref[pl.ds(base, ABLK), :]
            return carry

        lax.fori_loop(0, nchunk, fill, 0)

        def comp(ch, carry):
            base = pl.multiple_of(ch * ABLK, ABLK)
            dy = dy_ref[pl.ds(base, ABLK), :]
            du = jnp.zeros((ABLK, 128), F32)
            for k in range(CONV_K):
                du = du + dypad[pl.ds(base + (CONV_K - 1 - k), ABLK), :] * w_ref[k:k + 1, :]
                dwacc[k * 8:(k + 1) * 8, :] += _sum8(dy * upad[pl.ds(base + (off + k), ABLK), :])
            db_ref[...] += _sum8(dy)
            a = a_ref[pl.ds(base, ABLK), :]
            s = _sigmoid(g_ref[pl.ds(base, ABLK), :])
            dcc_ref[0, pl.ds(base, ABLK), :] = (du * s).astype(BF16)
            dcc_ref[1, pl.ds(base, ABLK), :] = (du * a * (s * (1.0 - s))).astype(BF16)
            return carry

        lax.fori_loop(0, nchunk, comp, 0)
        dw_ref[...] = dwacc[...].reshape(CONV_PAD, 8, 128).sum(axis=1)

    return pl.pallas_call(
        body, out_shape=[SDS((2, lp, c), BF16), SDS((CONV_PAD, c), F32), SDS((8, c), F32)],
        grid=(ncb,),
        in_specs=[pl.BlockSpec((lp, 128), lambda j: (0, j)), pl.BlockSpec((lp, 128), lambda j: (0, j)),
                  pl.BlockSpec((lp, 128), lambda j: (0, j + ncb)), pl.BlockSpec((CONV_PAD, 128), lambda j: (0, j))],
        out_specs=[pl.BlockSpec((2, lp, 128), lambda j: (0, 0, j)),
                   pl.BlockSpec((CONV_PAD, 128), lambda j: (0, j)), pl.BlockSpec((8, 128), lambda j: (0, j))],
        scratch_shapes=[pltpu.VMEM((lp + CONV_PAD, 128), F32), pltpu.VMEM((lp + CONV_PAD, 128), F32),
                        pltpu.VMEM((CONV_PAD * 8, 128), F32)],
        name=name, compiler_params=_cp())(dyc, cacg, cacg, w)


def _mesh_pos():
    x, y, c = lax.axis_index("x"), lax.axis_index("y"), lax.axis_index("c")
    return x, y, c


def _peer(pos, r):
    x, y, c = pos
    px = (1 - x) if (r >> 2) & 1 else x
    py = (1 - y) if (r >> 1) & 1 else y
    pc = (1 - c) if r & 1 else c
    return (px, py, pc), 4 * px + 2 * py + pc


class _Job:
    def __init__(self, src, dst, scatter, src_layer=None, dst_layer=None):
        self.src, self.dst, self.scatter, self.src_layer, self.dst_layer = src, dst, scatter, src_layer, dst_layer

    def src_view(self, ins, idx):
        v = ins[self.src] if self.src_layer is None else ins[self.src].at[self.src_layer]
        return v.at[idx] if self.scatter else v

    def dst_view(self, outs, slot):
        v = outs[self.dst] if self.dst_layer is None else outs[self.dst].at[self.dst_layer]
        return v.at[slot]


def _exchange_start(jobs, ins, outs, send, recv, loc):
    pos = _mesh_pos()
    me = 4 * pos[0] + 2 * pos[1] + pos[2]
    for j, job in enumerate(jobs):
        pltpu.make_async_copy(job.src_view(ins, me), job.dst_view(outs, me), loc.at[j]).start()
        for r in range(1, N_DEV):
            peer, peer_idx = _peer(pos, r)
            pltpu.make_async_remote_copy(src_ref=job.src_view(ins, peer_idx), dst_ref=job.dst_view(outs, me),
                                         send_sem=send.at[j, r - 1], recv_sem=recv.at[j, r - 1], device_id=peer,
                                         device_id_type=pl.DeviceIdType.MESH).start()


def _exchange_wait(jobs, ins, outs, send, recv, loc):
    pos = _mesh_pos()
    me = 4 * pos[0] + 2 * pos[1] + pos[2]
    for j, job in enumerate(jobs):
        for r in range(1, N_DEV):
            peer, peer_idx = _peer(pos, r)
            cp = pltpu.make_async_remote_copy(src_ref=job.src_view(ins, peer_idx), dst_ref=job.dst_view(outs, peer_idx),
                                              send_sem=send.at[j, r - 1], recv_sem=recv.at[j, r - 1], device_id=peer,
                                              device_id_type=pl.DeviceIdType.MESH)
            cp.wait_recv()
            cp.wait_send()
        pltpu.make_async_copy(job.src_view(ins, me), job.dst_view(outs, me), loc.at[j]).wait()


def _exchange_sems(n_jobs):
    return [pltpu.SemaphoreType.DMA((n_jobs, N_DEV - 1)), pltpu.SemaphoreType.DMA((n_jobs, N_DEV - 1)),
            pltpu.SemaphoreType.DMA((n_jobs,))]


def _exchange(jobs, arrs, out_shape, name):
    n_in, n_out = len(arrs), len(out_shape)
    any_spec = pl.BlockSpec(memory_space=pl.ANY)

    def body(*refs):
        ins, outs, sems = refs[:n_in], refs[n_in:n_in + n_out], refs[n_in + n_out:]
        _exchange_start(jobs, ins, outs, *sems)
        _exchange_wait(jobs, ins, outs, *sems)

    return pl.pallas_call(
        body, out_shape=out_shape, in_specs=[any_spec] * n_in, out_specs=[any_spec] * n_out,
        scratch_shapes=_exchange_sems(len(jobs)),
        name=name, compiler_params=pltpu.CompilerParams(has_side_effects=True))(*arrs)


def _all_reduce_small(p8, q, name):
    ni, _, w = p8.shape
    nq = q.shape[0]
    rows = ni + nq

    def body(p_ref, q_ref, o_ref, buf, send, recv):
        pos = _mesh_pos()
        me = 4 * pos[0] + 2 * pos[1] + pos[2]
        buf[me] = jnp.concatenate([p_ref[...].sum(axis=1), q_ref[...]], axis=0)
        for r in range(1, N_DEV):
            peer, _ = _peer(pos, r)
            pltpu.make_async_remote_copy(src_ref=buf.at[me], dst_ref=buf.at[me], send_sem=send.at[r - 1],
                                         recv_sem=recv.at[r - 1], device_id=peer,
                                         device_id_type=pl.DeviceIdType.MESH).start()
        for r in range(1, N_DEV):
            peer, peer_idx = _peer(pos, r)
            cp = pltpu.make_async_remote_copy(src_ref=buf.at[me], dst_ref=buf.at[peer_idx], send_sem=send.at[r - 1],
                                              recv_sem=recv.at[r - 1], device_id=peer,
                                              device_id_type=pl.DeviceIdType.MESH)
            cp.wait_recv()
            cp.wait_send()
        acc = buf[0]
        for dev in range(1, N_DEV):
            acc = acc + buf[dev]
        o_ref[...] = acc

    vmem = pl.BlockSpec(memory_space=pltpu.VMEM)
    return pl.pallas_call(
        body, out_shape=SDS((rows, w), F32), in_specs=[vmem, vmem], out_specs=vmem,
        scratch_shapes=[pltpu.VMEM((N_DEV, rows, w), F32), pltpu.SemaphoreType.DMA((N_DEV - 1,)),
                        pltpu.SemaphoreType.DMA((N_DEV - 1,))],
        name=name, compiler_params=pltpu.CompilerParams(has_side_effects=True))(p8, q)


def _adamw_math(w, g, m, v):
    m = ADAM_B1 * m + (1.0 - ADAM_B1) * g
    v = ADAM_B2 * v + (1.0 - ADAM_B2) * (g * g)
    m_hat = m / (1.0 - ADAM_B1 ** ADAM_STEP)
    v_hat = v / (1.0 - ADAM_B2 ** ADAM_STEP)
    delta = -ADAM_LR * (m_hat / (jnp.sqrt(v_hat) + ADAM_EPS) + ADAM_WD * w)
    return delta, m, v


def _adamw_shard(parts, w, m, v, transposed, name):
    depth = len(parts)
    _, rr, cc = parts[0].shape
    if transposed:
        tc = _tile(cc, 256, 128)
        nt = cc // tc
        part_block, blk = (N_DEV, rr, tc), pl.BlockSpec((None, tc, rr), lambda l, i: (l, i, 0))
    else:
        tr = _tile(rr, 256, 8)
        nt = rr // tr
        part_block, blk = (N_DEV, tr, cc), pl.BlockSpec((None, tr, cc), lambda l, i: (l, i, 0))

    def body(*refs):
        p_refs = refs[:depth]
        w_ref, m_ref, v_ref, g_out, d_out, m_out, v_out = refs[depth:]
        for li in range(depth):
            @pl.when(pl.program_id(0) == li)
            def _(p_ref=p_refs[li]):
                g = p_ref[0].astype(F32)
                for dev in range(1, N_DEV):
                    g = g + p_ref[dev].astype(F32)
                if transposed:
                    g = g.T
                delta, mm, vv = _adamw_math(w_ref[...], g, m_ref[...], v_ref[...])
                g_out[...] = g
                d_out[...] = delta
                m_out[...] = mm
                v_out[...] = vv

    def part_spec(li):
        at = lambda l, i: jnp.where(l == li, i, jnp.where(l < li, 0, nt - 1))
        return pl.BlockSpec(part_block, (lambda l, i: (0, 0, at(l, i))) if transposed else (lambda l, i: (0, at(l, i), 0)))

    return pl.pallas_call(
        body, out_shape=[SDS(w.shape, F32)] * 4, grid=(depth, nt),
        in_specs=[part_spec(li) for li in range(depth)] + [blk, blk, blk],
        out_specs=[blk] * 4, name=name, compiler_params=_cp())(*parts, w, m, v)


def _adamw_flat(g, w, m, v, name):
    def body(g_ref, w_ref, m_ref, v_ref, d_out, m_out, v_out):
        delta, mm, vv = _adamw_math(w_ref[...], g_ref[...], m_ref[...], v_ref[...])
        d_out[...] = delta
        m_out[...] = mm
        v_out[...] = vv

    return pl.pallas_call(body, out_shape=[SDS(g.shape, F32)] * 3, name=name, compiler_params=_cp())(g, w, m, v)


def _from_cols(t):
    return jnp.transpose(t, (1, 0, 2)).reshape(t.shape[1], N_DEV * t.shape[2])


def _swap(t):
    return jnp.swapaxes(t, -1, -2)


def kernel(x, meta_tokens, mix_norm_g, w_in, conv_dw_w, conv_dw_b, conv_ln_g, conv_ln_b, w_out, ffn_norm_g, w_gate, w_up, w_down, final_norm_g, loss_target, m_meta_tokens, m_mix_norm_g, m_w_in, m_conv_dw_w, m_conv_dw_b, m_conv_ln_g, m_conv_ln_b, m_w_out, m_ffn_norm_g, m_w_gate, m_w_up, m_w_down, m_final_norm_g, v_meta_tokens, v_mix_norm_g, v_w_in, v_conv_dw_w, v_conv_dw_b, v_conv_ln_g, v_conv_ln_b, v_w_out, v_ffn_norm_g, v_w_gate, v_w_up, v_w_down, v_final_norm_g):
    depth, d, in_shard = w_in.shape
    seq = x.shape[1]
    sb = N_HEADS * HEAD_DIM
    cc = conv_dw_w.shape[2] * N_DEV
    ff = w_gate.shape[2] * N_DEV
    assert in_shard * N_DEV == 3 * sb + 2 * cc and x.shape[0] == 1
    lr = N_META + seq
    lp = -(-lr // ABLK) * ABLK
    me = 4 * lax.axis_index("x") + 2 * lax.axis_index("y") + lax.axis_index("c")

    big_names = ("w_in", "w_out", "w_gate", "w_up", "w_down")
    transposed = {"w_in": True, "w_out": False, "w_gate": True, "w_up": True, "w_down": False}
    shard = dict(w_in=_swap(w_in).astype(BF16), w_out=w_out.astype(BF16), w_gate=_swap(w_gate).astype(BF16),
                 w_up=_swap(w_up).astype(BF16), w_down=w_down.astype(BF16))

    def gather_of(keys):
        names = sorted({n for n, _ in keys}, key=big_names.index)
        jobs = [_Job(names.index(n), j, False, src_layer=i) for j, (n, i) in enumerate(keys)]
        return jobs, [shard[n] for n in names], [SDS((N_DEV,) + shard[n].shape[1:], BF16) for n, _ in keys]

    first_keys = [("w_in", 0)]
    ffn_keys = lambda i: [("w_gate", i), ("w_up", i), ("w_down", i)]
    mixer_keys = lambda i: [("w_in", i), ("w_out", i)]

    def receive(keys, arrays):
        for (n, li), t in zip(keys, arrays):
            wl[li][n] = t.reshape(-1, d)

    jobs, srcs, out_shape = gather_of(first_keys)
    for extra in (meta_tokens, conv_dw_w):
        jobs.append(_Job(len(srcs), len(out_shape), False))
        srcs.append(extra)
        out_shape.append(SDS((N_DEV,) + extra.shape, F32))
    gathered = _exchange(jobs, srcs, out_shape, "gather_first")
    wl = [dict() for _ in range(depth)]
    receive(first_keys, gathered)
    meta_full = _from_cols(gathered[-2])
    taps = jnp.transpose(gathered[-1], (1, 2, 0, 3)).reshape(depth, CONV_K, cc)
    taps = jnp.pad(taps, ((0, 0), (0, CONV_PAD - CONV_K), (0, 0)))
    tri_fwd, tri_after, tri_before = _tri_consts()

    h = jnp.concatenate([meta_full, x[0], jnp.zeros((lp - lr, d), F32)], axis=0)
    saved = []
    for i in range(depth):
        p = wl[i]
        sv = dict(h_in=h)
        qkv, hn = _rms_mm(h, mix_norm_g[i:i + 1], p["w_in"], 0, 3 * sb, BF16, f"proj_qkv_{i}", True)
        keys = [("w_out", 0)] if i == 0 else []
        cacg, *arrived = _rms_mm(h, mix_norm_g[i:i + 1], p["w_in"], 3 * sb, 2 * cc, F32, f"proj_conv_{i}", False,
                                 _Rider(*gather_of(keys)) if keys else None)
        receive(keys, arrived)
        keys = ffn_keys(i)
        attn, rsave, *arrived = _attn_fwd(qkv, tri_fwd, f"attn_fwd_{i}", _Rider(*gather_of(keys)))
        receive(keys, arrived)
        yc = _conv_fwd_dw(cacg, taps[i], conv_dw_b[i:i + 1], f"conv_fwd_dw_{i}")
        conv = _conv_fwd_ln(yc, conv_ln_g[i:i + 1], conv_ln_b[i:i + 1], f"conv_fwd_ln_{i}")
        h = _mix_out(attn, conv, p["w_out"], h, f"mix_out_{i}")
        sv.update(qkv=qkv, hn=hn, cacg=cacg, rsave=rsave, yc=yc, attn=attn, conv=conv, h_mid=h)
        keys = mixer_keys(i + 1) if i + 1 < depth else []
        h, hn2, act, *arrived = _ffn_fwd(h, ffn_norm_g[i:i + 1], p["w_gate"], p["w_up"], p["w_down"], f"ffn_fwd_{i}",
                                         _Rider(*gather_of(keys)) if keys else None)
        receive(keys, arrived)
        sv.update(hn2=hn2, act=act)
        saved.append(sv)

    tpad = jnp.pad(loss_target[0], ((N_META, lp - lr), (0, 0)))
    dh, dg_final, loss_part = _loss_head(h, tpad, final_norm_g.reshape(1, d), seq, "loss_head")
    loss = lax.psum(loss_part[0, 0], MESH_AXES)

    parts = {}

    def sending(items):
        srcs = [t.reshape(N_DEV, t.shape[0] // N_DEV, d) for _, t in items]
        return _Rider([_Job(j, j, True) for j in range(len(items))], srcs, [SDS(t.shape, BF16) for t in srcs])

    def arrive(items, arrays):
        parts.update({key: t for (key, _), t in zip(items, arrays)})

    grads = [None] * depth
    from_above = []
    for i in reversed(range(depth)):
        p, sv = wl[i], saved[i]
        d_g, d_u, *got = _ffn_bwd_act(dh, sv["hn2"], p["w_gate"], p["w_up"], p["w_down"], f"ffn_bwd_act_{i}",
                                      sending(from_above))
        arrive(from_above, got)
        gw_down = _mm_tn(sv["act"], dh, f"grad_w_down_{i}")
        gw_gate = _mm_tn(d_g, sv["hn2"], f"grad_w_gate_{i}")
        gw_up = _mm_tn(d_u, sv["hn2"], f"grad_w_up_{i}")
        items = [(("w_down", i), gw_down)]
        dh, dg_ffn, *got = _ffn_bwd_in(d_g, d_u, p["w_gate"], p["w_up"], sv["h_mid"], ffn_norm_g[i:i + 1], dh,
                                       f"ffn_bwd_in_{i}", sending(items))
        arrive(items, got)
        gw_out = jnp.concatenate([_mm_tn(sv["attn"], dh, f"grad_w_out_attn_{i}"),
                                  _mm_tn(sv["conv"], dh, f"grad_w_out_conv_{i}")], axis=0)
        d_attn = _mm_nt(dh, p["w_out"], 0, sb, BF16, f"mix_bwd_attn_{i}")
        d_conv = _mm_nt(dh, p["w_out"], sb, cc, F32, f"mix_bwd_conv_{i}")
        items = [(("w_gate", i), gw_gate), (("w_up", i), gw_up), (("w_out", i), gw_out)]
        dqkv, *got = _attn_bwd(sv["qkv"], d_attn, sv["rsave"], tri_after, tri_before, f"attn_bwd_{i}", sending(items))
        arrive(items, got)
        dyc, dg_ln, db_ln = _conv_bwd_ln(sv["yc"], d_conv, conv_ln_g[i:i + 1], conv_ln_b[i:i + 1], f"conv_bwd_ln_{i}")
        dcc, g_taps_i, db_conv = _conv_bwd_dw(dyc, sv["cacg"], taps[i], f"conv_bwd_dw_{i}")
        gw_in = _grad_w_in_t(dqkv, dcc, sv["hn"], f"grad_w_in_{i}")
        from_above = [(("w_in", i), gw_in)]
        items = from_above if i == 0 else []
        dh, dg_mix, *got = _mix_bwd_in(dqkv, dcc, p["w_in"], sv["h_in"], mix_norm_g[i:i + 1], dh, f"mix_bwd_in_{i}",
                                       sending(items))
        arrive(items, got)
        grads[i] = dict(taps=g_taps_i, dg_mix=dg_mix, dg_ffn=dg_ffn, dg_ln=dg_ln, db_ln=db_ln, db_conv=db_conv)
    grad_x = dh[N_META:lr][None]

    wide = lambda key: jnp.concatenate([jnp.pad(grads[i][key], ((0, 0), (0, d // depth - cc))) for i in range(depth)], axis=1)
    assert depth * cc <= d and d % depth == 0
    p8 = jnp.stack([grads[i]["dg_mix"] for i in range(depth)] + [grads[i]["dg_ffn"] for i in range(depth)]
                   + [dg_final, wide("db_conv"), wide("dg_ln"), wide("db_ln")])
    small = _all_reduce_small(p8, jnp.concatenate([wide("taps"), dh[:N_META]], axis=0), "reduce_small")
    r0 = 2 * depth + 4
    g_mix = small[0:depth]
    g_ffn = small[depth:2 * depth]
    g_final = small[2 * depth]
    narrow = lambda row: jnp.stack([row[i * (d // depth):i * (d // depth) + cc] for i in range(depth)])
    g_cb, g_lg, g_lb = narrow(small[2 * depth + 1]), narrow(small[2 * depth + 2]), narrow(small[2 * depth + 3])
    g_taps_full = jnp.stack([small[r0:r0 + CONV_K, i * (d // depth):i * (d // depth) + cc] for i in range(depth)])
    csh = cc // N_DEV
    g_taps_own = lax.dynamic_slice_in_dim(g_taps_full, me * csh, csh, axis=2)
    g_meta_full = small[r0 + CONV_PAD:r0 + CONV_PAD + N_META]
    msh = d // N_DEV
    g_meta_own = lax.dynamic_slice_in_dim(g_meta_full, me * msh, msh, axis=1)

    small_g = [g_meta_own, g_mix, g_taps_own, g_cb, g_lg, g_lb, g_ffn, g_final]
    small_w = [meta_tokens, mix_norm_g, conv_dw_w, conv_dw_b, conv_ln_g, conv_ln_b, ffn_norm_g, final_norm_g]
    small_m = [m_meta_tokens, m_mix_norm_g, m_conv_dw_w, m_conv_dw_b, m_conv_ln_g, m_conv_ln_b, m_ffn_norm_g, m_final_norm_g]
    small_v = [v_meta_tokens, v_mix_norm_g, v_conv_dw_w, v_conv_dw_b, v_conv_ln_g, v_conv_ln_b, v_ffn_norm_g, v_final_norm_g]
    sizes = [int(math.prod(t.shape)) for t in small_w]
    total = sum(sizes)
    rows = -(-total // (8 * 128)) * 8

    def flat(ts):
        v = jnp.concatenate([t.reshape(-1) for t in ts])
        return jnp.pad(v, (0, rows * 128 - total)).reshape(rows, 128)

    def unflat(t):
        v, out, o = t.reshape(-1), [], 0
        for sz, ref in zip(sizes, small_w):
            out.append(v[o:o + sz].reshape(ref.shape))
            o += sz
        return out

    sd, sm, sv_ = _adamw_flat(flat(small_g), flat(small_w), flat(small_m), flat(small_v), "adamw_small")
    s_delta, s_m, s_v = unflat(sd), unflat(sm), unflat(sv_)

    big = []
    for n, w, m, v in zip(big_names, (w_in, w_out, w_gate, w_up, w_down), (m_w_in, m_w_out, m_w_gate, m_w_up, m_w_down),
                          (v_w_in, v_w_out, v_w_gate, v_w_up, v_w_down)):
        big.append(_adamw_shard([parts[(n, i)] for i in range(depth)], w, m, v, transposed[n], f"adamw_{n}"))
    b_in, b_out, b_gate, b_up, b_down = big

    def ordered(k, smalls):
        s_meta, s_mix, s_taps, s_cb, s_lg, s_lb, s_ffn, s_final = smalls
        return [s_meta, s_mix, b_in[k], s_taps, s_cb, s_lg, s_lb, b_out[k], s_ffn, b_gate[k], b_up[k], b_down[k], s_final]

    return (loss, grad_x, *ordered(0, small_g), *ordered(1, s_delta), *ordered(2, s_m), *ordered(3, s_v))
```

```python
import functools
import math

import jax
import jax.numpy as jnp
from jax import lax
from jax.experimental import pallas as pl
from jax.experimental.pallas import tpu as pltpu

F32 = jnp.float32
BF16 = jnp.bfloat16
SDS = jax.ShapeDtypeStruct

N_META = 16
N_HEADS = 8
HEAD_DIM = 64
CONV_K = 31
CONV_PAD = 32
ABLK = 128
ATT_GROUP = 4
ATT_SPLIT = 2
ATT_TOP = 2
EXP_ZERO_AT = 104.0
GONE = 1e30
MASKED = -1e30
FF_CHUNK = 256
EPS = 1e-6
N_DEV = 8
MESH_AXES = ("x", "y", "c")
ADAM_LR = 0.001
ADAM_B1 = 0.9
ADAM_B2 = 0.999
ADAM_EPS = 1e-08
ADAM_WD = 0.01
ADAM_STEP = 10
MIB = 1 << 20
VMEM_LIMIT_MIB = 48
MM_ROWS = 1056
EW_ROWS = 528


def _cp():
    return pltpu.CompilerParams(vmem_limit_bytes=VMEM_LIMIT_MIB * MIB)


def _tile(n, cap, mult):
    best = None
    for t in range(mult, min(n, cap) + 1, mult):
        if n % t == 0:
            best = t
    assert best is not None, (n, cap, mult)
    return best


def _sum8(v):
    r, c = v.shape
    return v.reshape(r // 8, 8, c).sum(axis=0)


def _sigmoid(x):
    return 1.0 / (1.0 + jnp.exp(-x))


NT = (((1,), (1,)), ((), ()))
TN = (((0,), (0,)), ((), ()))


def _rms_mm(h, g, wt, w_row0, n_cols, out_dtype, name, hn_out, rider=None):
    lp, d = h.shape
    tm = _tile(lp, MM_ROWS, 16)
    tn = _tile(math.gcd(n_cols, w_row0), 768, 128)
    off = w_row0 // tn
    ni, nj = lp // tm, n_cols // tn
    rider = rider or _Rider()

    def body(h_ref, g_ref, w_ref, *rest):
        outs, (hn_s,) = rider.split(rest, 2 if hn_out else 1, 1)
        o_ref = outs[0]
        hn_ref = outs[1] if hn_out else None
        rider.start((pl.program_id(0) == 0) & (pl.program_id(1) == 0))
        if ni > 1:
            rider.forward((pl.program_id(0) == ni - 1) & (pl.program_id(1) == 0))

        @pl.when(pl.program_id(1) == 0)
        def _():
            x = h_ref[...]
            r = lax.rsqrt(jnp.mean(x * x, axis=-1, keepdims=True) + EPS)
            hn = ((x * r) * g_ref[...]).astype(BF16)
            hn_s[...] = hn
            if hn_out:
                hn_ref[...] = hn

        o_ref[...] = lax.dot_general(hn_s[...], w_ref[...], NT, preferred_element_type=F32).astype(out_dtype)
        rider.wait((pl.program_id(0) == ni - 1) & (pl.program_id(1) == nj - 1))

    out_shape = [SDS((lp, n_cols), out_dtype)]
    out_specs = [pl.BlockSpec((tm, tn), lambda i, j: (i, j))]
    if hn_out:
        out_shape.append(SDS((lp, d), BF16))
        out_specs.append(pl.BlockSpec((tm, d), lambda i, j: (i, 0)))
    return pl.pallas_call(
        body, out_shape=out_shape + rider.out_shape, grid=(ni, nj),
        in_specs=[pl.BlockSpec((tm, d), lambda i, j: (i, 0)),
                  pl.BlockSpec((1, d), lambda i, j: (0, 0)),
                  pl.BlockSpec((tn, d), lambda i, j: (j + off, 0))] + rider.in_specs(),
        out_specs=out_specs + rider.out_specs(), scratch_shapes=[pltpu.VMEM((tm, d), BF16)] + rider.scratch(),
        name=name, compiler_params=_cp())(h, g, wt, *rider.srcs)


def _mm_nt(a, b, b_row0, n_out, out_dtype, name):
    m, k = a.shape
    tm = _tile(m, MM_ROWS, 16)
    tn = _tile(math.gcd(n_out, b_row0), 512, 128)
    off = b_row0 // tn

    def body(a_ref, b_ref, o_ref):
        o_ref[...] = lax.dot_general(a_ref[...].astype(BF16), b_ref[...], NT,
                                     preferred_element_type=F32).astype(out_dtype)

    return pl.pallas_call(
        body, out_shape=SDS((m, n_out), out_dtype), grid=(m // tm, n_out // tn),
        in_specs=[pl.BlockSpec((tm, k), lambda i, j: (i, 0)), pl.BlockSpec((tn, k), lambda i, j: (j + off, 0))],
        out_specs=pl.BlockSpec((tm, tn), lambda i, j: (i, j)), name=name, compiler_params=_cp())(a, b)


def _mix_out(attn, conv, w, res, name):
    m, ka = attn.shape
    kc = conv.shape[1]
    n = w.shape[1]
    assert ka == kc
    tm = _tile(m, MM_ROWS, 16)
    tn = _tile(n, 512, 128)

    def body(a_ref, c_ref, wa_ref, wc_ref, r_ref, o_ref):
        o_ref[...] = (r_ref[...] + jnp.dot(a_ref[...], wa_ref[...], preferred_element_type=F32)
                      + jnp.dot(c_ref[...], wc_ref[...], preferred_element_type=F32))

    return pl.pallas_call(
        body, out_shape=SDS((m, n), F32), grid=(m // tm, n // tn),
        in_specs=[pl.BlockSpec((tm, ka), lambda i, j: (i, 0)), pl.BlockSpec((tm, kc), lambda i, j: (i, 0)),
                  pl.BlockSpec((ka, tn), lambda i, j: (0, j)), pl.BlockSpec((kc, tn), lambda i, j: (1, j)),
                  pl.BlockSpec((tm, tn), lambda i, j: (i, j))],
        out_specs=pl.BlockSpec((tm, tn), lambda i, j: (i, j)), name=name, compiler_params=_cp())(attn, conv, w, w, res)


def _mm_tn(a, b, name):
    l, m = a.shape
    n = b.shape[1]
    tm = _tile(m, 1408, 128)
    tn = _tile(n, 1024, 128)
    tl = _tile(l, 1408, 128)
    nl = l // tl

    def body(a_ref, b_ref, o_ref, acc):
        @pl.when(pl.program_id(2) == 0)
        def _():
            acc[...] = jnp.zeros_like(acc)

        acc[...] += lax.dot_general(a_ref[...].astype(BF16), b_ref[...].astype(BF16), TN, preferred_element_type=F32)

        @pl.when(pl.program_id(2) == nl - 1)
        def _():
            o_ref[...] = acc[...].astype(BF16)

    return pl.pallas_call(
        body, out_shape=SDS((m, n), BF16), grid=(m // tm, n // tn, nl),
        in_specs=[pl.BlockSpec((tl, tm), lambda i, j, s: (s, i)),
                  pl.BlockSpec((tl, tn), lambda i, j, s: (s, j))],
        out_specs=pl.BlockSpec((tm, tn), lambda i, j, s: (i, j)),
        scratch_shapes=[pltpu.VMEM((tm, tn), F32)], name=name, compiler_params=_cp())(a, b)


def _grad_w_in_t(dqkv, dcc, hn, name):
    nq, l, w = dqkv.shape
    nc = dcc.shape[0]
    d = hn.shape[1]
    assert dcc.shape[2] == w
    tn = _tile(d, 1024, 128)
    tl = _tile(l, 1408, 128)
    nl = l // tl

    def body(q_ref, c_ref, b_ref, o_ref, acc):
        p, s = pl.program_id(0), pl.program_id(2)

        @pl.when(s == 0)
        def _():
            acc[...] = jnp.zeros_like(acc)

        @pl.when(p < nq)
        def _():
            acc[...] += lax.dot_general(q_ref[...], b_ref[...], TN, preferred_element_type=F32)

        @pl.when(p >= nq)
        def _():
            acc[...] += lax.dot_general(c_ref[...], b_ref[...], TN, preferred_element_type=F32)

        @pl.when(s == nl - 1)
        def _():
            o_ref[...] = acc[...].astype(BF16)

    out = pl.pallas_call(
        body, out_shape=SDS((nq + nc, w, d), BF16), grid=(nq + nc, d // tn, nl),
        in_specs=[pl.BlockSpec((None, tl, w), lambda p, j, s: (jnp.minimum(p, nq - 1), s, 0)),
                  pl.BlockSpec((None, tl, w), lambda p, j, s: (jnp.maximum(p - nq, 0), s, 0)),
                  pl.BlockSpec((tl, tn), lambda p, j, s: (s, j))],
        out_specs=pl.BlockSpec((None, w, tn), lambda p, j, s: (p, 0, j)),
        scratch_shapes=[pltpu.VMEM((w, tn), F32)], name=name, compiler_params=_cp())(dqkv, dcc, hn)
    return out.reshape((nq + nc) * w, d)


def _rms_bwd_tail(acc, h_ref, g_ref, dres_ref, o_ref, dg_ref):
    x = h_ref[...]
    r = lax.rsqrt(jnp.mean(x * x, axis=-1, keepdims=True) + EPS)
    xr = x * r
    dyv = acc[...]
    gy = dyv * g_ref[...]
    o_ref[...] = dres_ref[...] + r * (gy - xr * jnp.mean(gy * xr, axis=-1, keepdims=True))
    dg_ref[...] += _sum8(dyv * xr)


def _mix_bwd_in(dqkv, dcc, wt, h, g, dres, name, rider=None):
    nq, lp, w = dqkv.shape
    nc = dcc.shape[0]
    d = h.shape[1]
    tm = _tile(lp, MM_ROWS, 16)
    npc = nq + nc
    ni = lp // tm
    rider = rider or _Rider()

    def body(q_ref, c_ref, w_ref, h_ref, g_ref, dres_ref, *rest):
        (o_ref, dg_ref), (acc,) = rider.split(rest, 2, 1)
        i, p = pl.program_id(0), pl.program_id(1)
        rider.start((i == 0) & (p == 0))

        @pl.when(p == 0)
        def _():
            acc[...] = jnp.zeros_like(acc)

        @pl.when((p == 0) & (i == 0))
        def _():
            dg_ref[...] = jnp.zeros_like(dg_ref)

        @pl.when(p < nq)
        def _():
            acc[...] += jnp.dot(q_ref[...], w_ref[...], preferred_element_type=F32)

        @pl.when(p >= nq)
        def _():
            acc[...] += jnp.dot(c_ref[...], w_ref[...], preferred_element_type=F32)

        @pl.when(p == npc - 1)
        def _():
            _rms_bwd_tail(acc, h_ref, g_ref, dres_ref, o_ref, dg_ref)

        rider.wait((i == ni - 1) & (p == npc - 1))

    return pl.pallas_call(
        body, out_shape=[SDS((lp, d), F32), SDS((8, d), F32)] + rider.out_shape, grid=(ni, npc),
        in_specs=[pl.BlockSpec((None, tm, w), lambda i, p: (jnp.minimum(p, nq - 1), i, 0)),
                  pl.BlockSpec((None, tm, w), lambda i, p: (jnp.maximum(p - nq, 0), i, 0)),
                  pl.BlockSpec((w, d), lambda i, p: (p, 0)),
                  pl.BlockSpec((tm, d), lambda i, p: (i, 0)),
                  pl.BlockSpec((1, d), lambda i, p: (0, 0)),
                  pl.BlockSpec((tm, d), lambda i, p: (i, 0))] + rider.in_specs(),
        out_specs=[pl.BlockSpec((tm, d), lambda i, p: (i, 0)), pl.BlockSpec((8, d), lambda i, p: (0, 0))]
        + rider.out_specs(),
        scratch_shapes=[pltpu.VMEM((tm, d), F32)] + rider.scratch(), name=name,
        compiler_params=_cp())(dqkv, dcc, wt, h, g, dres, *rider.srcs)


def _ffn_bwd_in(d_g, d_u, wg_t, wu_t, h, g, dres, name, rider=None):
    lp, f = d_g.shape
    d = h.shape[1]
    tm = _tile(lp, MM_ROWS, 16)
    tk = _tile(f, 512, 128)
    nk = f // tk
    ni = lp // tm
    rider = rider or _Rider()

    def body(dg_in, du_in, wg_ref, wu_ref, h_ref, g_ref, dres_ref, *rest):
        (o_ref, dg_ref), (acc,) = rider.split(rest, 2, 1)
        i, kk = pl.program_id(0), pl.program_id(1)
        rider.start((i == 0) & (kk == 0))

        @pl.when(kk == 0)
        def _():
            acc[...] = jnp.zeros_like(acc)

        @pl.when((kk == 0) & (i == 0))
        def _():
            dg_ref[...] = jnp.zeros_like(dg_ref)

        acc[...] += (jnp.dot(dg_in[...], wg_ref[...], preferred_element_type=F32)
                     + jnp.dot(du_in[...], wu_ref[...], preferred_element_type=F32))

        @pl.when(kk == nk - 1)
        def _():
            _rms_bwd_tail(acc, h_ref, g_ref, dres_ref, o_ref, dg_ref)

        rider.wait((i == ni - 1) & (kk == nk - 1))

    return pl.pallas_call(
        body, out_shape=[SDS((lp, d), F32), SDS((8, d), F32)] + rider.out_shape, grid=(ni, nk),
        in_specs=[pl.BlockSpec((tm, tk), lambda i, j: (i, j)), pl.BlockSpec((tm, tk), lambda i, j: (i, j)),
                  pl.BlockSpec((tk, d), lambda i, j: (j, 0)), pl.BlockSpec((tk, d), lambda i, j: (j, 0)),
                  pl.BlockSpec((tm, d), lambda i, j: (i, 0)),
                  pl.BlockSpec((1, d), lambda i, j: (0, 0)),
                  pl.BlockSpec((tm, d), lambda i, j: (i, 0))] + rider.in_specs(),
        out_specs=[pl.BlockSpec((tm, d), lambda i, j: (i, 0)), pl.BlockSpec((8, d), lambda i, j: (0, 0))]
        + rider.out_specs(),
        scratch_shapes=[pltpu.VMEM((tm, d), F32)] + rider.scratch(), name=name,
        compiler_params=_cp())(d_g, d_u, wg_t, wu_t, h, g, dres, *rider.srcs)


def _ffn_fwd(h, g, wg_t, wu_t, w_d, name, rider=None):
    lp, d = h.shape
    f = w_d.shape[0]
    tm = _tile(lp, MM_ROWS, 16)
    nc = f // FF_CHUNK
    ni = lp // tm
    rider = rider or _Rider()

    def body(h_ref, g_ref, wg_ref, wu_ref, wd_ref, *rest):
        (o_ref, hn_ref, act_ref), (hn_s, acc) = rider.split(rest, 3, 2)
        c = pl.program_id(1)
        rider.start((pl.program_id(0) == 0) & (c == 0))
        if ni > 1:
            rider.forward((pl.program_id(0) == ni - 1) & (c == 0))

        @pl.when(c == 0)
        def _():
            x = h_ref[...]
            r = lax.rsqrt(jnp.mean(x * x, axis=-1, keepdims=True) + EPS)
            hn = ((x * r) * g_ref[...]).astype(BF16)
            hn_s[...] = hn
            hn_ref[...] = hn
            acc[...] = jnp.zeros_like(acc)

        gg = lax.dot_general(hn_s[...], wg_ref[...], NT, preferred_element_type=F32)
        uu = lax.dot_general(hn_s[...], wu_ref[...], NT, preferred_element_type=F32)
        act = ((gg * _sigmoid(gg)) * uu).astype(BF16)
        act_ref[...] = act
        acc[...] += jnp.dot(act, wd_ref[...], preferred_element_type=F32)

        @pl.when(c == nc - 1)
        def _():
            o_ref[...] = h_ref[...] + acc[...]

        rider.wait((pl.program_id(0) == ni - 1) & (c == nc - 1))

    chunk = pl.BlockSpec((FF_CHUNK, d), lambda i, j: (j, 0))
    return pl.pallas_call(
        body, out_shape=[SDS((lp, d), F32), SDS((lp, d), BF16), SDS((lp, f), BF16)] + rider.out_shape, grid=(ni, nc),
        in_specs=[pl.BlockSpec((tm, d), lambda i, j: (i, 0)), pl.BlockSpec((1, d), lambda i, j: (0, 0)),
                  chunk, chunk, chunk] + rider.in_specs(),
        out_specs=[pl.BlockSpec((tm, d), lambda i, j: (i, 0)),
                   pl.BlockSpec((tm, d), lambda i, j: (i, 0)),
                   pl.BlockSpec((tm, FF_CHUNK), lambda i, j: (i, j))] + rider.out_specs(),
        scratch_shapes=[pltpu.VMEM((tm, d), BF16), pltpu.VMEM((tm, d), F32)] + rider.scratch(),
        name=name, compiler_params=_cp())(h, g, wg_t, wu_t, w_d, *rider.srcs)


def _ffn_bwd_act(dh, hn, wg_t, wu_t, w_d, name, rider=None):
    lp, d = dh.shape
    f = w_d.shape[0]
    tm = _tile(lp, MM_ROWS, 16)
    nc = f // FF_CHUNK
    ni = lp // tm
    rider = rider or _Rider()

    def body(dh_ref, hn_ref, wg_ref, wu_ref, wd_ref, *rest):
        (dg_out, du_out), (dh_s,) = rider.split(rest, 2, 1)
        rider.start((pl.program_id(0) == 0) & (pl.program_id(1) == 0))

        @pl.when(pl.program_id(1) == 0)
        def _():
            dh_s[...] = dh_ref[...].astype(BF16)

        gg = lax.dot_general(hn_ref[...], wg_ref[...], NT, preferred_element_type=F32)
        uu = lax.dot_general(hn_ref[...], wu_ref[...], NT, preferred_element_type=F32)
        dact = lax.dot_general(dh_s[...], wd_ref[...], NT, preferred_element_type=F32)
        s = _sigmoid(gg)
        dg_out[...] = (dact * uu * (s * (1.0 + gg * (1.0 - s)))).astype(BF16)
        du_out[...] = (dact * (gg * s)).astype(BF16)
        rider.wait((pl.program_id(0) == ni - 1) & (pl.program_id(1) == nc - 1))

    chunk = pl.BlockSpec((FF_CHUNK, d), lambda i, j: (j, 0))
    out = pl.BlockSpec((tm, FF_CHUNK), lambda i, j: (i, j))
    return pl.pallas_call(
        body, out_shape=[SDS((lp, f), BF16), SDS((lp, f), BF16)] + rider.out_shape, grid=(ni, nc),
        in_specs=[pl.BlockSpec((tm, d), lambda i, j: (i, 0)), pl.BlockSpec((tm, d), lambda i, j: (i, 0)),
                  chunk, chunk, chunk] + rider.in_specs(),
        out_specs=[out, out] + rider.out_specs(), scratch_shapes=[pltpu.VMEM((tm, d), BF16)] + rider.scratch(),
        name=name, compiler_params=_cp())(dh, hn, wg_t, wu_t, w_d, *rider.srcs)


def _loss_head(h, tpad, g, n_real, name):
    lp, d = h.shape
    tm = _tile(lp, EW_ROWS, 16)

    def body(h_ref, t_ref, g_ref, dh_ref, dg_ref, loss_ref):
        i = pl.program_id(0)

        @pl.when(i == 0)
        def _():
            dg_ref[...] = jnp.zeros_like(dg_ref)
            loss_ref[...] = jnp.zeros_like(loss_ref)

        x = h_ref[...]
        r = lax.rsqrt(jnp.mean(x * x, axis=-1, keepdims=True) + EPS)
        xr = x * r
        y = xr * g_ref[...]
        row = i * tm + lax.broadcasted_iota(jnp.int32, (tm, d), 0)
        valid = (row >= N_META) & (row < N_META + n_real)
        diff = jnp.where(valid, y - t_ref[...], 0.0)
        loss_ref[...] += jnp.sum(diff * diff) * (0.5 / d)
        dy = diff * (1.0 / d)
        gy = dy * g_ref[...]
        dh_ref[...] = r * (gy - xr * jnp.mean(gy * xr, axis=-1, keepdims=True))
        dg_ref[...] += _sum8(dy * xr)

    return pl.pallas_call(
        body, out_shape=[SDS((lp, d), F32), SDS((8, d), F32), SDS((8, 128), F32)], grid=(lp // tm,),
        in_specs=[pl.BlockSpec((tm, d), lambda i: (i, 0)),
                  pl.BlockSpec((tm, d), lambda i: (i, 0)),
                  pl.BlockSpec((1, d), lambda i: (0, 0))],
        out_specs=[pl.BlockSpec((tm, d), lambda i: (i, 0)),
                   pl.BlockSpec((8, d), lambda i: (0, 0)),
                   pl.BlockSpec((8, 128), lambda i: (0, 0))],
        name=name, compiler_params=_cp())(h, tpad, g)


def _tri_consts():
    j = lax.broadcasted_iota(jnp.int32, (ABLK, ABLK), 0)
    s = lax.broadcasted_iota(jnp.int32, (ABLK, ABLK), 1)
    after = (j >= s).astype(BF16)
    before = (j < s).astype(BF16)
    ones = jnp.ones((ABLK, ABLK), BF16)
    two = lambda t: jnp.concatenate([t, t], axis=0)
    return (two(jnp.concatenate([after, ones], axis=1)),
            two(after),
            two(jnp.concatenate([before, ones], axis=1)))


def _softplus(z):
    neg_abs = lax.bitcast_convert_type(lax.bitcast_convert_type(z, jnp.uint32) | jnp.uint32(0x80000000), F32)
    return jnp.log(1.0 + jnp.exp(neg_abs)) + jnp.maximum(z, 0.0)


def _split_hi_lo(m):
    hi = m.astype(BF16)
    lo = (m - hi.astype(F32)).astype(BF16)
    return jnp.concatenate([hi, lo], axis=1)


def _head_halves(t2, in_a):
    zero = jnp.zeros_like(t2)
    return jnp.concatenate([jnp.where(in_a, t2, zero), jnp.where(in_a, zero, t2)], axis=0)


def _stack_blocks(t, nblk, in_a):
    return jnp.concatenate([_head_halves(t[u * ABLK:(u + 1) * ABLK], in_a) for u in range(nblk)], axis=0)


def _attn_scale():
    scale = 1.0 / math.sqrt(HEAD_DIM)
    assert math.frexp(scale)[0] == 0.5, "a power of two, so that scaling q in bf16 is exact"
    return scale


def _pow2_below(n):
    assert n & (n - 1) == 0
    return [p for p in (64, 32, 16, 8, 4, 2, 1) if p < n]


class _Rider:
    def __init__(self, jobs=(), srcs=(), out_shape=()):
        self.jobs, self.srcs, self.out_shape = list(jobs), list(srcs), list(out_shape)
        self.any = [pl.BlockSpec(memory_space=pl.ANY)]

    def split(self, rest, n_out, n_scratch):
        ni, no = len(self.srcs), len(self.out_shape)
        self.ins, outs = rest[:ni], rest[ni:ni + n_out]
        self.outs = rest[ni + n_out:ni + n_out + no]
        scratch = rest[ni + n_out + no:ni + n_out + no + n_scratch]
        self.sems = rest[ni + n_out + no + n_scratch:]
        return outs, scratch

    def start(self, first):
        self.forwarded = False
        if self.jobs:
            @pl.when(first)
            def _():
                _exchange_start(self.jobs, self.ins, self.outs, *self.sems)

    def forward(self, late):
        self.forwarded = True
        if self.jobs:
            @pl.when(late)
            def _():
                _exchange_forward(self.jobs, self.ins, self.outs, *self.sems)

    def wait(self, last):
        if self.jobs:
            @pl.when(last)
            def _():
                if not self.forwarded:
                    _exchange_forward(self.jobs, self.ins, self.outs, *self.sems)
                _exchange_wait(self.jobs, self.ins, self.outs, *self.sems)

    def in_specs(self):
        return self.any * len(self.srcs)

    def out_specs(self):
        return self.any * len(self.out_shape)

    def scratch(self):
        return _exchange_sems(len(self.jobs)) if self.jobs else []


def _attn_fwd(qkv, tri_fwd, name, rider=None):
    lp = qkv.shape[0]
    n_pairs = (N_HEADS * HEAD_DIM) // 128
    nb = lp // ABLK
    assert nb <= 128 and 2 * HEAD_DIM == 128
    scale = _attn_scale()
    nt = (((1,), (1,)), ((), ()))
    rider = rider or _Rider()

    def body(q_ref, k_ref, v_ref, tri_ref, *rest):
        (o_ref, rs_ref), (r_s, acc_s, rs_s) = rider.split(rest, 2, 3)
        rider.start(pl.program_id(0) == 0)
        if n_pairs > 1:
            rider.forward(pl.program_id(0) == n_pairs - 1)
        lane = lax.broadcasted_iota(jnp.int32, (ABLK, 128), 1)
        row = lax.broadcasted_iota(jnp.int32, (ABLK, 128), 0)
        in_a = lane < HEAD_DIM
        causal = lane < row

        def qblock(i, carry):
            q0 = pl.multiple_of(i * ABLK, ABLK)
            q2 = q_ref[pl.ds(q0, ABLK), :] * scale
            r_s[...] = jnp.zeros_like(r_s)
            acc_s[...] = jnp.zeros_like(acc_s)
            rs_s[...] = jnp.full_like(rs_s, GONE)

            def live():
                least = jnp.min(jnp.minimum(r_s[0], r_s[1]), axis=0, keepdims=True)
                return (least[0, 0] < EXP_ZERO_AT).astype(jnp.int32)

            def step(kb0, nblk, diag):
                k0 = pl.multiple_of(kb0 * ABLK, ABLK)
                kbd = _stack_blocks(k_ref[pl.ds(k0, nblk * ABLK), :], nblk, in_a)
                vbd = _stack_blocks(v_ref[pl.ds(k0, nblk * ABLK), :], nblk, in_a)
                z = lax.dot_general(q2, kbd, nt, preferred_element_type=F32)
                ncol = 2 * nblk
                zt = [z[:, c * 128:(c + 1) * 128] for c in range(ncol)]
                if diag:
                    zt = [jnp.where(causal, t, MASKED) if c >= ncol - 2 else t for c, t in enumerate(zt)]
                bounds = [ncol * j // ATT_SPLIT for j in range(ATT_SPLIT + 1)]
                ce = [None] * ncol
                for c0, c1 in reversed([b for b in zip(bounds[:-1], bounds[1:]) if b[0] < b[1]]):
                    parts = [_split_hi_lo(_softplus(zt[c])) for c in range(c0, c1)]
                    got = jnp.dot(jnp.concatenate(parts, axis=0), tri_ref[...], preferred_element_type=F32)
                    for c in range(c0, c1):
                        ce[c] = got[(c - c0) * 128:(c - c0 + 1) * 128]
                rr = [r_s[0], r_s[1]]
                rsv = [rs_s[:, :128], rs_s[:, 128:]]
                ws = [None] * ncol
                for u in reversed(range(nblk)):
                    for hh in range(2):
                        c = 2 * u + hh
                        ws[c] = jnp.exp(zt[c] - ce[c][:, :128] - rr[hh]).astype(BF16)
                        rsv[hh] = jnp.where(lane == kb0 + u, rr[hh], rsv[hh])
                        rr[hh] = rr[hh] + ce[c][:, 128:]
                acc_s[...] += jnp.dot(jnp.concatenate(ws, axis=1), vbd, preferred_element_type=F32)
                r_s[0] = rr[0]
                r_s[1] = rr[1]
                rs_s[:, :128] = rsv[0]
                rs_s[:, 128:] = rsv[1]

            n_top = jnp.minimum(i, ATT_TOP)
            for t in range(ATT_TOP + 1):
                @pl.when(n_top == t)
                def _():
                    step(i - t, t + 1, True)

            i_low = i - n_top
            n_grp = i_low // ATT_GROUP

            def more(c):
                return (c[0] < n_grp) & (c[1] > 0)

            def inner(c):
                step(i_low - ATT_GROUP * (c[0] + 1), ATT_GROUP, False)
                return c[0] + 1, live()

            _, alive = lax.while_loop(more, inner, (jnp.int32(0), live()))
            rem = i_low - ATT_GROUP * n_grp
            for p in _pow2_below(ATT_GROUP):
                def last_steps(p=p):
                    step(rem & (p - 1), p, False)
                    return live()

                alive = lax.cond(((rem & p) != 0) & (alive > 0), last_steps, lambda alive=alive: alive)

            o_ref[pl.ds(q0, ABLK), :] = acc_s[...].astype(BF16)
            rs_ref[pl.ds(q0, ABLK), :] = rs_s[...]
            return carry

        lax.fori_loop(0, nb, qblock, 0)
        rider.wait(pl.program_id(0) == n_pairs - 1)

    col = lambda o: (lambda p: (0, p + o))
    return pl.pallas_call(
        body, out_shape=[SDS((lp, n_pairs * 128), BF16), SDS((lp, n_pairs * 256), F32)] + rider.out_shape,
        grid=(n_pairs,),
        in_specs=[pl.BlockSpec((lp, 128), col(0)), pl.BlockSpec((lp, 128), col(n_pairs)),
                  pl.BlockSpec((lp, 128), col(2 * n_pairs)), pl.BlockSpec((256, 256), lambda p: (0, 0))]
        + rider.in_specs(),
        out_specs=[pl.BlockSpec((lp, 128), col(0)), pl.BlockSpec((lp, 256), col(0))] + rider.out_specs(),
        scratch_shapes=[pltpu.VMEM((2, ABLK, 128), F32), pltpu.VMEM((ABLK, 128), F32), pltpu.VMEM((ABLK, 256), F32)]
        + rider.scratch(),
        name=name, compiler_params=_cp())(qkv, qkv, qkv, tri_fwd, *rider.srcs)


def _attn_bwd(qkv, d_out, rsave, tri_after, tri_before, name, rider=None):
    lp = qkv.shape[0]
    n_pairs = (N_HEADS * HEAD_DIM) // 128
    nb = lp // ABLK
    scale = _attn_scale()
    nt = (((1,), (1,)), ((), ()))
    tn = (((0,), (0,)), ((), ()))
    rider = rider or _Rider()

    def body(q_ref, k_ref, v_ref, do_ref, rs_ref, ta_ref, tb_ref, *rest):
        (o_ref,), (dk_s, dv_s, dq_s, pc_s) = rider.split(rest, 1, 4)
        rider.start(pl.program_id(0) == 0)
        lane = lax.broadcasted_iota(jnp.int32, (ABLK, 128), 1)
        row = lax.broadcasted_iota(jnp.int32, (ABLK, 128), 0)
        in_a = lane < HEAD_DIM
        causal = lane < row
        dk_s[...] = jnp.zeros_like(dk_s)
        dv_s[...] = jnp.zeros_like(dv_s)

        def qblock(i, carry):
            q0 = pl.multiple_of(i * ABLK, ABLK)
            q2 = q_ref[pl.ds(q0, ABLK), :] * scale
            do2 = do_ref[pl.ds(q0, ABLK), :]
            q_st = _head_halves(q2, in_a)
            do_st = _head_halves(do2, in_a)
            dq_s[...] = jnp.zeros_like(dq_s)
            pc_s[...] = jnp.zeros_like(pc_s)

            def step(kb0, nblk, diag):
                k0 = pl.multiple_of(kb0 * ABLK, ABLK)
                kbd = _stack_blocks(k_ref[pl.ds(k0, nblk * ABLK), :], nblk, in_a)
                vbd = _stack_blocks(v_ref[pl.ds(k0, nblk * ABLK), :], nblk, in_a)
                z = lax.dot_general(q2, kbd, nt, preferred_element_type=F32)
                dw = lax.dot_general(do2, vbd, nt, preferred_element_type=F32)
                ncol = 2 * nblk
                zt = [z[:, c * 128:(c + 1) * 128] for c in range(ncol)]
                if diag:
                    zt = [jnp.where(causal, t, MASKED) if c >= ncol - 2 else t for c, t in enumerate(zt)]
                bounds = [ncol * j // ATT_SPLIT for j in range(ATT_SPLIT + 1)]
                batches = [range(c0, c1) for c0, c1 in zip(bounds[:-1], bounds[1:]) if c0 < c1]
                sps, ex, ws, dls, pe = [None] * ncol, [None] * ncol, [None] * ncol, [None] * ncol, [None] * ncol

                def mass(cols):
                    for c in cols:
                        sps[c] = _softplus(zt[c])
                    got = jnp.dot(jnp.concatenate([_split_hi_lo(sps[c]) for c in cols], axis=0), ta_ref[...],
                                  preferred_element_type=F32)
                    for j, c in enumerate(cols):
                        ex[c] = got[j * 128:(j + 1) * 128]

                def weights(cols):
                    for c in cols:
                        u, hh = c // 2, c % 2
                        r_saved = jnp.sum(jnp.where(lane == kb0 + u, rs_ref[pl.ds(q0, ABLK), hh * 128:(hh + 1) * 128],
                                                    0.0), axis=1, keepdims=True)
                        w = jnp.exp(zt[c] - ex[c] - r_saved)
                        ws[c] = w.astype(BF16)
                        dls[c] = dw[:, c * 128:(c + 1) * 128] * w
                    got = jnp.dot(jnp.concatenate([_split_hi_lo(dls[c]) for c in cols], axis=0), tb_ref[...],
                                  preferred_element_type=F32)
                    for j, c in enumerate(cols):
                        pe[c] = got[j * 128:(j + 1) * 128]

                mass(batches[0])
                for j in range(len(batches)):
                    if j + 1 < len(batches):
                        mass(batches[j + 1])
                    weights(batches[j])
                pc = [pc_s[0], pc_s[1]]
                dzs = []
                for c in range(ncol):
                    hh = c % 2
                    one_minus_beta = jnp.exp(-sps[c])
                    dz = dls[c] * one_minus_beta - (pe[c][:, :128] + pc[hh]) * (1.0 - one_minus_beta)
                    pc[hh] = pc[hh] + pe[c][:, 128:]
                    dzs.append(dz.astype(BF16))
                pc_s[0] = pc[0]
                pc_s[1] = pc[1]
                dq_s[...] += jnp.dot(jnp.concatenate(dzs, axis=1), kbd, preferred_element_type=F32)
                by_head = lambda ts: jnp.concatenate([jnp.concatenate(ts[0::2], axis=1), jnp.concatenate(ts[1::2], axis=1)],
                                                     axis=0)
                rows = pl.ds(k0, nblk * ABLK)
                dk_s[rows, :] += lax.dot_general(by_head(dzs), q_st, tn, preferred_element_type=F32)
                dv_s[rows, :] += lax.dot_general(by_head(ws), do_st, tn, preferred_element_type=F32)

            gone = jnp.min(rs_ref[pl.ds(q0, ABLK), :], axis=0, keepdims=True) >= EXP_ZERO_AT
            lane1 = lax.broadcasted_iota(jnp.int32, (1, 128), 1)
            first = jnp.sum(jnp.where(gone[:, :128] & gone[:, 128:] & (lane1 < i), 1.0, 0.0)).astype(jnp.int32)
            n_top = jnp.minimum(i, ATT_TOP)
            i_low = i - n_top
            n_grp = i_low // ATT_GROUP
            rem = i_low - ATT_GROUP * n_grp
            for p in reversed(_pow2_below(ATT_GROUP)):
                @pl.when(((rem & p) != 0) & ((rem & (p - 1)) + p > first))
                def _():
                    step(rem & (p - 1), p, False)

            def inner(g, c2):
                step(rem + ATT_GROUP * g, ATT_GROUP, False)
                return c2

            lax.fori_loop(jnp.maximum(first - rem, 0) // ATT_GROUP, n_grp, inner, 0)
            for t in range(ATT_TOP + 1):
                @pl.when(n_top == t)
                def _():
                    step(i - t, t + 1, True)

            o_ref[0, pl.ds(q0, ABLK), :] = (dq_s[...] * scale).astype(BF16)
            o_ref[0, pl.ds(q0, ABLK), :] = (dq_s[...] * scale).astype(BF16)
            return carry

        lax.fori_loop(0, nb, qblock, 0)
        o_ref[1] = dk_s[...].astype(BF16)
        o_ref[2] = dv_s[...].astype(BF16)
        rider.wait(pl.program_id(0) == n_pairs - 1)

    col = lambda o: (lambda p: (0, p + o))
    return pl.pallas_call(
        body, out_shape=[SDS((3, lp, n_pairs * 128), BF16)] + rider.out_shape, grid=(n_pairs,),
        in_specs=[pl.BlockSpec((lp, 128), col(0)), pl.BlockSpec((lp, 128), col(n_pairs)),
                  pl.BlockSpec((lp, 128), col(2 * n_pairs)), pl.BlockSpec((lp, 128), col(0)),
                  pl.BlockSpec((lp, 256), col(0)),
                  pl.BlockSpec((256, 128), lambda p: (0, 0)), pl.BlockSpec((256, 256), lambda p: (0, 0))]
        + rider.in_specs(),
        out_specs=[pl.BlockSpec((3, lp, 128), lambda p: (0, 0, p))] + rider.out_specs(),
        scratch_shapes=[pltpu.VMEM((lp, 128), F32), pltpu.VMEM((lp, 128), F32),
                        pltpu.VMEM((ABLK, 128), F32), pltpu.VMEM((2, ABLK, 128), F32)] + rider.scratch(),
        name=name, compiler_params=_cp())(qkv, qkv, qkv, d_out, rsave, tri_after, tri_before, *rider.srcs)


def _conv_fwd_dw(cacg, w, b, name):
    lp = cacg.shape[0]
    c = cacg.shape[1] // 2
    ncb = c // 128
    nchunk = lp // ABLK
    off = CONV_PAD - (CONV_K - 1)

    def body(a_ref, g_ref, w_ref, b_ref, y_ref, upad):
        upad[0:CONV_PAD, :] = jnp.zeros((CONV_PAD, 128), F32)

        def fill(ch, carry):
            base = pl.multiple_of(ch * ABLK, ABLK)
            upad[pl.ds(base + CONV_PAD, ABLK), :] = a_ref[pl.ds(base, ABLK), :] * _sigmoid(g_ref[pl.ds(base, ABLK), :])
            return carry

        lax.fori_loop(0, nchunk, fill, 0)

        def comp(ch, carry):
            base = pl.multiple_of(ch * ABLK, ABLK)
            acc = jnp.zeros((ABLK, 128), F32)
            for k in range(CONV_K):
                acc = acc + upad[pl.ds(base + (off + k), ABLK), :] * w_ref[k:k + 1, :]
            y_ref[pl.ds(base, ABLK), :] = acc + b_ref[...]
            return carry

        lax.fori_loop(0, nchunk, comp, 0)

    return pl.pallas_call(
        body, out_shape=SDS((lp, c), F32), grid=(ncb,),
        in_specs=[pl.BlockSpec((lp, 128), lambda j: (0, j)), pl.BlockSpec((lp, 128), lambda j: (0, j + ncb)),
                  pl.BlockSpec((CONV_PAD, 128), lambda j: (0, j)), pl.BlockSpec((1, 128), lambda j: (0, j))],
        out_specs=pl.BlockSpec((lp, 128), lambda j: (0, j)),
        scratch_shapes=[pltpu.VMEM((lp + CONV_PAD, 128), F32)], name=name, compiler_params=_cp())(cacg, cacg, w, b)


def _ln_parts(x, g, b):
    mu = jnp.mean(x, axis=-1, keepdims=True)
    xc = x - mu
    rstd = lax.rsqrt(jnp.mean(xc * xc, axis=-1, keepdims=True) + EPS)
    xh = xc * rstd
    return xh, rstd, xh * g + b


def _conv_fwd_ln(yc, g, b, name):
    lp, c = yc.shape
    tm = _tile(lp, EW_ROWS, 16)

    def body(y_ref, g_ref, b_ref, o_ref):
        _, _, ln = _ln_parts(y_ref[...], g_ref[...], b_ref[...])
        o_ref[...] = (ln * _sigmoid(ln)).astype(BF16)

    return pl.pallas_call(
        body, out_shape=SDS((lp, c), BF16), grid=(lp // tm,),
        in_specs=[pl.BlockSpec((tm, c), lambda i: (i, 0)), pl.BlockSpec((1, c), lambda i: (0, 0)),
                  pl.BlockSpec((1, c), lambda i: (0, 0))],
        out_specs=pl.BlockSpec((tm, c), lambda i: (i, 0)), name=name, compiler_params=_cp())(yc, g, b)


def _conv_bwd_ln(yc, dout, g, b, name):
    lp, c = yc.shape
    tm = _tile(lp, EW_ROWS, 16)

    def body(y_ref, d_ref, g_ref, b_ref, o_ref, dg_ref, db_ref):
        @pl.when(pl.program_id(0) == 0)
        def _():
            dg_ref[...] = jnp.zeros_like(dg_ref)
            db_ref[...] = jnp.zeros_like(db_ref)

        xh, rstd, ln = _ln_parts(y_ref[...], g_ref[...], b_ref[...])
        s = _sigmoid(ln)
        dln = d_ref[...] * (s * (1.0 + ln * (1.0 - s)))
        dg_ref[...] += _sum8(dln * xh)
        db_ref[...] += _sum8(dln)
        dxh = dln * g_ref[...]
        o_ref[...] = rstd * (dxh - jnp.mean(dxh, axis=-1, keepdims=True)
                             - xh * jnp.mean(dxh * xh, axis=-1, keepdims=True))

    return pl.pallas_call(
        body, out_shape=[SDS((lp, c), F32), SDS((8, c), F32), SDS((8, c), F32)], grid=(lp // tm,),
        in_specs=[pl.BlockSpec((tm, c), lambda i: (i, 0)), pl.BlockSpec((tm, c), lambda i: (i, 0)),
                  pl.BlockSpec((1, c), lambda i: (0, 0)), pl.BlockSpec((1, c), lambda i: (0, 0))],
        out_specs=[pl.BlockSpec((tm, c), lambda i: (i, 0)), pl.BlockSpec((8, c), lambda i: (0, 0)),
                   pl.BlockSpec((8, c), lambda i: (0, 0))],
        name=name, compiler_params=_cp())(yc, dout, g, b)


def _conv_bwd_dw(dyc, cacg, w, name):
    lp, c = dyc.shape
    ncb = c // 128
    nchunk = lp // ABLK
    off = CONV_PAD - (CONV_K - 1)

    def body(dy_ref, a_ref, g_ref, w_ref, dcc_ref, dw_ref, db_ref, upad, dypad, dwacc):
        upad[0:CONV_PAD, :] = jnp.zeros((CONV_PAD, 128), F32)
        dypad[lp:lp + CONV_PAD, :] = jnp.zeros((CONV_PAD, 128), F32)
        dwacc[...] = jnp.zeros_like(dwacc)
        db_ref[...] = jnp.zeros_like(db_ref)

        def fill(ch, carry):
            base = pl.multiple_of(ch * ABLK, ABLK)
            upad[pl.ds(base + CONV_PAD, ABLK), :] = a_ref[pl.ds(base, ABLK), :] * _sigmoid(g_ref[pl.ds(base, ABLK), :])
            dypad[pl.ds(base, ABLK), :] = dy_ref[pl.ds(base, ABLK), :]
            return carry

        lax.fori_loop(0, nchunk, fill, 0)

        def comp(ch, carry):
            base = pl.multiple_of(ch * ABLK, ABLK)
            dy = dy_ref[pl.ds(base, ABLK), :]
            du = jnp.zeros((ABLK, 128), F32)
            for k in range(CONV_K):
                du = du + dypad[pl.ds(base + (CONV_K - 1 - k), ABLK), :] * w_ref[k:k + 1, :]
                dwacc[k * 8:(k + 1) * 8, :] += _sum8(dy * upad[pl.ds(base + (off + k), ABLK), :])
            db_ref[...] += _sum8(dy)
            a = a_ref[pl.ds(base, ABLK), :]
            s = _sigmoid(g_ref[pl.ds(base, ABLK), :])
            dcc_ref[0, pl.ds(base, ABLK), :] = (du * s).astype(BF16)
            dcc_ref[1, pl.ds(base, ABLK), :] = (du * a * (s * (1.0 - s))).astype(BF16)
            return carry

        lax.fori_loop(0, nchunk, comp, 0)
        dw_ref[...] = dwacc[...].reshape(CONV_PAD, 8, 128).sum(axis=1)

    return pl.pallas_call(
        body, out_shape=[SDS((2, lp, c), BF16), SDS((CONV_PAD, c), F32), SDS((8, c), F32)],
        grid=(ncb,),
        in_specs=[pl.BlockSpec((lp, 128), lambda j: (0, j)), pl.BlockSpec((lp, 128), lambda j: (0, j)),
                  pl.BlockSpec((lp, 128), lambda j: (0, j + ncb)), pl.BlockSpec((CONV_PAD, 128), lambda j: (0, j))],
        out_specs=[pl.BlockSpec((2, lp, 128), lambda j: (0, 0, j)),
                   pl.BlockSpec((CONV_PAD, 128), lambda j: (0, j)), pl.BlockSpec((8, 128), lambda j: (0, j))],
        scratch_shapes=[pltpu.VMEM((lp + CONV_PAD, 128), F32), pltpu.VMEM((lp + CONV_PAD, 128), F32),
                        pltpu.VMEM((CONV_PAD * 8, 128), F32)],
        name=name, compiler_params=_cp())(dyc, cacg, cacg, w)


def _mesh_pos():
    x, y, c = lax.axis_index("x"), lax.axis_index("y"), lax.axis_index("c")
    return x, y, c


def _peer(pos, r):
    x, y, c = pos
    px = (1 - x) if (r >> 2) & 1 else x
    py = (1 - y) if (r >> 1) & 1 else y
    pc = (1 - c) if r & 1 else c
    return (px, py, pc), 4 * px + 2 * py + pc


SIBLING = 1
OTHER_CHIPS = (2, 4, 6)
VIA_SIBLING = (3, 5, 7)


class _Job:
    def __init__(self, src, dst, scatter, src_layer=None, dst_layer=None):
        self.src, self.dst, self.scatter, self.src_layer, self.dst_layer = src, dst, scatter, src_layer, dst_layer

    def src_view(self, ins, idx):
        v = ins[self.src] if self.src_layer is None else ins[self.src].at[self.src_layer]
        return v.at[idx] if self.scatter else v

    def dst_view(self, outs, slot):
        v = outs[self.dst] if self.dst_layer is None else outs[self.dst].at[self.dst_layer]
        return v.at[slot]


def _remote(job, j, r, src, dst, to, send, recv):
    return pltpu.make_async_remote_copy(src_ref=src, dst_ref=dst, send_sem=send.at[j, r - 1], recv_sem=recv.at[j, r - 1],
                                        device_id=to, device_id_type=pl.DeviceIdType.MESH)


def _exchange_start(jobs, ins, outs, send, recv, loc):
    pos = _mesh_pos()
    me = 4 * pos[0] + 2 * pos[1] + pos[2]
    for j, job in enumerate(jobs):
        pltpu.make_async_copy(job.src_view(ins, me), job.dst_view(outs, me), loc.at[j]).start()
        for r in (range(1, N_DEV) if job.scatter else (SIBLING,) + OTHER_CHIPS):
            peer, peer_idx = _peer(pos, r)
            _remote(job, j, r, job.src_view(ins, peer_idx), job.dst_view(outs, me), peer, send, recv).start()


def _exchange_forward(jobs, ins, outs, send, recv, loc):
    pos = _mesh_pos()
    sibling, _ = _peer(pos, SIBLING)
    for j, job in enumerate(jobs):
        if job.scatter:
            continue
        for r in OTHER_CHIPS:
            peer, peer_idx = _peer(pos, r)
            slot = job.dst_view(outs, peer_idx)
            _remote(job, j, r, job.src_view(ins, peer_idx), slot, peer, send, recv).wait_recv()
            _remote(job, j, r ^ SIBLING, slot, slot, sibling, send, recv).start()


def _exchange_wait(jobs, ins, outs, send, recv, loc):
    pos = _mesh_pos()
    me = 4 * pos[0] + 2 * pos[1] + pos[2]
    for j, job in enumerate(jobs):
        for r in range(1, N_DEV):
            peer, peer_idx = _peer(pos, r)
            cp = _remote(job, j, r, job.src_view(ins, peer_idx), job.dst_view(outs, peer_idx), peer, send, recv)
            if job.scatter or r not in OTHER_CHIPS:
                cp.wait_recv()
            cp.wait_send()
        pltpu.make_async_copy(job.src_view(ins, me), job.dst_view(outs, me), loc.at[j]).wait()


def _exchange_sems(n_jobs):
    return [pltpu.SemaphoreType.DMA((n_jobs, N_DEV - 1)), pltpu.SemaphoreType.DMA((n_jobs, N_DEV - 1)),
            pltpu.SemaphoreType.DMA((n_jobs,))]


def _exchange(jobs, arrs, out_shape, name):
    n_in, n_out = len(arrs), len(out_shape)
    any_spec = pl.BlockSpec(memory_space=pl.ANY)

    def body(*refs):
        ins, outs, sems = refs[:n_in], refs[n_in:n_in + n_out], refs[n_in + n_out:]
        _exchange_start(jobs, ins, outs, *sems)
        _exchange_forward(jobs, ins, outs, *sems)
        _exchange_wait(jobs, ins, outs, *sems)

    return pl.pallas_call(
        body, out_shape=out_shape, in_specs=[any_spec] * n_in, out_specs=[any_spec] * n_out,
        scratch_shapes=_exchange_sems(len(jobs)),
        name=name, compiler_params=pltpu.CompilerParams(has_side_effects=True))(*arrs)


def _all_reduce_small(p8, q, name):
    ni, _, w = p8.shape
    nq = q.shape[0]
    rows = ni + nq

    def body(p_ref, q_ref, o_ref, buf, send, recv):
        pos = _mesh_pos()
        me = 4 * pos[0] + 2 * pos[1] + pos[2]
        buf[me] = jnp.concatenate([p_ref[...].sum(axis=1), q_ref[...]], axis=0)
        for r in range(1, N_DEV):
            peer, _ = _peer(pos, r)
            pltpu.make_async_remote_copy(src_ref=buf.at[me], dst_ref=buf.at[me], send_sem=send.at[r - 1],
                                         recv_sem=recv.at[r - 1], device_id=peer,
                                         device_id_type=pl.DeviceIdType.MESH).start()
        for r in range(1, N_DEV):
            peer, peer_idx = _peer(pos, r)
            cp = pltpu.make_async_remote_copy(src_ref=buf.at[me], dst_ref=buf.at[peer_idx], send_sem=send.at[r - 1],
                                              recv_sem=recv.at[r - 1], device_id=peer,
                                              device_id_type=pl.DeviceIdType.MESH)
            cp.wait_recv()
            cp.wait_send()
        acc = buf[0]
        for dev in range(1, N_DEV):
            acc = acc + buf[dev]
        o_ref[...] = acc

    vmem = pl.BlockSpec(memory_space=pltpu.VMEM)
    return pl.pallas_call(
        body, out_shape=SDS((rows, w), F32), in_specs=[vmem, vmem], out_specs=vmem,
        scratch_shapes=[pltpu.VMEM((N_DEV, rows, w), F32), pltpu.SemaphoreType.DMA((N_DEV - 1,)),
                        pltpu.SemaphoreType.DMA((N_DEV - 1,))],
        name=name, compiler_params=pltpu.CompilerParams(has_side_effects=True))(p8, q)


def _adamw_math(w, g, m, v):
    m = ADAM_B1 * m + (1.0 - ADAM_B1) * g
    v = ADAM_B2 * v + (1.0 - ADAM_B2) * (g * g)
    m_hat = m / (1.0 - ADAM_B1 ** ADAM_STEP)
    v_hat = v / (1.0 - ADAM_B2 ** ADAM_STEP)
    delta = -ADAM_LR * (m_hat / (jnp.sqrt(v_hat) + ADAM_EPS) + ADAM_WD * w)
    return delta, m, v


def _adamw_shard(parts, w, m, v, transposed, name):
    depth = len(parts)
    _, rr, cc = parts[0].shape
    if transposed:
        tc = _tile(cc, 256, 128)
        nt = cc // tc
        part_block, blk = (N_DEV, rr, tc), pl.BlockSpec((None, tc, rr), lambda l, i: (l, i, 0))
    else:
        tr = _tile(rr, 256, 8)
        nt = rr // tr
        part_block, blk = (N_DEV, tr, cc), pl.BlockSpec((None, tr, cc), lambda l, i: (l, i, 0))

    def body(*refs):
        p_refs = refs[:depth]
        w_ref, m_ref, v_ref, g_out, d_out, m_out, v_out = refs[depth:]
        for li in range(depth):
            @pl.when(pl.program_id(0) == li)
            def _(p_ref=p_refs[li]):
                g = p_ref[0].astype(F32)
                for dev in range(1, N_DEV):
                    g = g + p_ref[dev].astype(F32)
                if transposed:
                    g = g.T
                delta, mm, vv = _adamw_math(w_ref[...], g, m_ref[...], v_ref[...])
                g_out[...] = g
                d_out[...] = delta
                m_out[...] = mm
                v_out[...] = vv

    def part_spec(li):
        at = lambda l, i: jnp.where(l == li, i, jnp.where(l < li, 0, nt - 1))
        return pl.BlockSpec(part_block, (lambda l, i: (0, 0, at(l, i))) if transposed else (lambda l, i: (0, at(l, i), 0)))

    return pl.pallas_call(
        body, out_shape=[SDS(w.shape, F32)] * 4, grid=(depth, nt),
        in_specs=[part_spec(li) for li in range(depth)] + [blk, blk, blk],
        out_specs=[blk] * 4, name=name, compiler_params=_cp())(*parts, w, m, v)


def _adamw_flat(g, w, m, v, name):
    def body(g_ref, w_ref, m_ref, v_ref, d_out, m_out, v_out):
        delta, mm, vv = _adamw_math(w_ref[...], g_ref[...], m_ref[...], v_ref[...])
        d_out[...] = delta
        m_out[...] = mm
        v_out[...] = vv

    return pl.pallas_call(body, out_shape=[SDS(g.shape, F32)] * 3, name=name, compiler_params=_cp())(g, w, m, v)


def _from_cols(t):
    return jnp.transpose(t, (1, 0, 2)).reshape(t.shape[1], N_DEV * t.shape[2])


def _swap(t):
    return jnp.swapaxes(t, -1, -2)


def kernel(x, meta_tokens, mix_norm_g, w_in, conv_dw_w, conv_dw_b, conv_ln_g, conv_ln_b, w_out, ffn_norm_g, w_gate, w_up, w_down, final_norm_g, loss_target, m_meta_tokens, m_mix_norm_g, m_w_in, m_conv_dw_w, m_conv_dw_b, m_conv_ln_g, m_conv_ln_b, m_w_out, m_ffn_norm_g, m_w_gate, m_w_up, m_w_down, m_final_norm_g, v_meta_tokens, v_mix_norm_g, v_w_in, v_conv_dw_w, v_conv_dw_b, v_conv_ln_g, v_conv_ln_b, v_w_out, v_ffn_norm_g, v_w_gate, v_w_up, v_w_down, v_final_norm_g):
    depth, d, in_shard = w_in.shape
    seq = x.shape[1]
    sb = N_HEADS * HEAD_DIM
    cc = conv_dw_w.shape[2] * N_DEV
    ff = w_gate.shape[2] * N_DEV
    assert in_shard * N_DEV == 3 * sb + 2 * cc and x.shape[0] == 1
    lr = N_META + seq
    lp = -(-lr // ABLK) * ABLK
    me = 4 * lax.axis_index("x") + 2 * lax.axis_index("y") + lax.axis_index("c")

    big_names = ("w_in", "w_out", "w_gate", "w_up", "w_down")
    transposed = {"w_in": True, "w_out": False, "w_gate": True, "w_up": True, "w_down": False}
    shard = dict(w_in=_swap(w_in).astype(BF16), w_out=w_out.astype(BF16), w_gate=_swap(w_gate).astype(BF16),
                 w_up=_swap(w_up).astype(BF16), w_down=w_down.astype(BF16))

    def gather_of(keys):
        names = sorted({n for n, _ in keys}, key=big_names.index)
        jobs = [_Job(names.index(n), j, False, src_layer=i) for j, (n, i) in enumerate(keys)]
        return jobs, [shard[n] for n in names], [SDS((N_DEV,) + shard[n].shape[1:], BF16) for n, _ in keys]

    first_keys = [("w_in", 0)]
    ffn_keys = lambda i: [("w_gate", i), ("w_up", i), ("w_down", i)]
    mixer_keys = lambda i: [("w_in", i), ("w_out", i)]

    def receive(keys, arrays):
        for (n, li), t in zip(keys, arrays):
            wl[li][n] = t.reshape(-1, d)

    jobs, srcs, out_shape = gather_of(first_keys)
    for extra in (meta_tokens, conv_dw_w):
        jobs.append(_Job(len(srcs), len(out_shape), False))
        srcs.append(extra)
        out_shape.append(SDS((N_DEV,) + extra.shape, F32))
    gathered = _exchange(jobs, srcs, out_shape, "gather_first")
    wl = [dict() for _ in range(depth)]
    receive(first_keys, gathered)
    meta_full = _from_cols(gathered[-2])
    taps = jnp.transpose(gathered[-1], (1, 2, 0, 3)).reshape(depth, CONV_K, cc)
    taps = jnp.pad(taps, ((0, 0), (0, CONV_PAD - CONV_K), (0, 0)))
    tri_fwd, tri_after, tri_before = _tri_consts()

    h = jnp.concatenate([meta_full, x[0], jnp.zeros((lp - lr, d), F32)], axis=0)
    saved = []
    for i in range(depth):
        p = wl[i]
        sv = dict(h_in=h)
        qkv, hn = _rms_mm(h, mix_norm_g[i:i + 1], p["w_in"], 0, 3 * sb, BF16, f"proj_qkv_{i}", True)
        keys = [("w_out", 0)] if i == 0 else []
        cacg, *arrived = _rms_mm(h, mix_norm_g[i:i + 1], p["w_in"], 3 * sb, 2 * cc, F32, f"proj_conv_{i}", False,
                                 _Rider(*gather_of(keys)) if keys else None)
        receive(keys, arrived)
        keys = ffn_keys(i)
        attn, rsave, *arrived = _attn_fwd(qkv, tri_fwd, f"attn_fwd_{i}", _Rider(*gather_of(keys)))
        receive(keys, arrived)
        yc = _conv_fwd_dw(cacg, taps[i], conv_dw_b[i:i + 1], f"conv_fwd_dw_{i}")
        conv = _conv_fwd_ln(yc, conv_ln_g[i:i + 1], conv_ln_b[i:i + 1], f"conv_fwd_ln_{i}")
        h = _mix_out(attn, conv, p["w_out"], h, f"mix_out_{i}")
        sv.update(qkv=qkv, hn=hn, cacg=cacg, rsave=rsave, yc=yc, attn=attn, conv=conv, h_mid=h)
        keys = mixer_keys(i + 1) if i + 1 < depth else []
        h, hn2, act, *arrived = _ffn_fwd(h, ffn_norm_g[i:i + 1], p["w_gate"], p["w_up"], p["w_down"], f"ffn_fwd_{i}",
                                         _Rider(*gather_of(keys)) if keys else None)
        receive(keys, arrived)
        sv.update(hn2=hn2, act=act)
        saved.append(sv)

    tpad = jnp.pad(loss_target[0], ((N_META, lp - lr), (0, 0)))
    dh, dg_final, loss_part = _loss_head(h, tpad, final_norm_g.reshape(1, d), seq, "loss_head")
    loss = lax.psum(loss_part[0, 0], MESH_AXES)

    parts = {}

    def sending(items):
        srcs = [t.reshape(N_DEV, t.shape[0] // N_DEV, d) for _, t in items]
        return _Rider([_Job(j, j, True) for j in range(len(items))], srcs, [SDS(t.shape, BF16) for t in srcs])

    def arrive(items, arrays):
        parts.update({key: t for (key, _), t in zip(items, arrays)})

    grads = [None] * depth
    from_above = []
    for i in reversed(range(depth)):
        p, sv = wl[i], saved[i]
        d_g, d_u, *got = _ffn_bwd_act(dh, sv["hn2"], p["w_gate"], p["w_up"], p["w_down"], f"ffn_bwd_act_{i}",
                                      sending(from_above))
        arrive(from_above, got)
        gw_down = _mm_tn(sv["act"], dh, f"grad_w_down_{i}")
        gw_gate = _mm_tn(d_g, sv["hn2"], f"grad_w_gate_{i}")
        gw_up = _mm_tn(d_u, sv["hn2"], f"grad_w_up_{i}")
        items = [(("w_down", i), gw_down)]
        dh, dg_ffn, *got = _ffn_bwd_in(d_g, d_u, p["w_gate"], p["w_up"], sv["h_mid"], ffn_norm_g[i:i + 1], dh,
                                       f"ffn_bwd_in_{i}", sending(items))
        arrive(items, got)
        gw_out = jnp.concatenate([_mm_tn(sv["attn"], dh, f"grad_w_out_attn_{i}"),
                                  _mm_tn(sv["conv"], dh, f"grad_w_out_conv_{i}")], axis=0)
        d_attn = _mm_nt(dh, p["w_out"], 0, sb, BF16, f"mix_bwd_attn_{i}")
        d_conv = _mm_nt(dh, p["w_out"], sb, cc, F32, f"mix_bwd_conv_{i}")
        items = [(("w_gate", i), gw_gate), (("w_up", i), gw_up), (("w_out", i), gw_out)]
        dqkv, *got = _attn_bwd(sv["qkv"], d_attn, sv["rsave"], tri_after, tri_before, f"attn_bwd_{i}", sending(items))
        arrive(items, got)
        dyc, dg_ln, db_ln = _conv_bwd_ln(sv["yc"], d_conv, conv_ln_g[i:i + 1], conv_ln_b[i:i + 1], f"conv_bwd_ln_{i}")
        dcc, g_taps_i, db_conv = _conv_bwd_dw(dyc, sv["cacg"], taps[i], f"conv_bwd_dw_{i}")
        gw_in = _grad_w_in_t(dqkv, dcc, sv["hn"], f"grad_w_in_{i}")
        from_above = [(("w_in", i), gw_in)]
        items = from_above if i == 0 else []
        dh, dg_mix, *got = _mix_bwd_in(dqkv, dcc, p["w_in"], sv["h_in"], mix_norm_g[i:i + 1], dh, f"mix_bwd_in_{i}",
                                       sending(items))
        arrive(items, got)
        grads[i] = dict(taps=g_taps_i, dg_mix=dg_mix, dg_ffn=dg_ffn, dg_ln=dg_ln, db_ln=db_ln, db_conv=db_conv)
    grad_x = dh[N_META:lr][None]

    wide = lambda key: jnp.concatenate([jnp.pad(grads[i][key], ((0, 0), (0, d // depth - cc))) for i in range(depth)], axis=1)
    assert depth * cc <= d and d % depth == 0
    p8 = jnp.stack([grads[i]["dg_mix"] for i in range(depth)] + [grads[i]["dg_ffn"] for i in range(depth)]
                   + [dg_final, wide("db_conv"), wide("dg_ln"), wide("db_ln")])
    small = _all_reduce_small(p8, jnp.concatenate([wide("taps"), dh[:N_META]], axis=0), "reduce_small")
    r0 = 2 * depth + 4
    g_mix = small[0:depth]
    g_ffn = small[depth:2 * depth]
    g_final = small[2 * depth]
    narrow = lambda row: jnp.stack([row[i * (d // depth):i * (d // depth) + cc] for i in range(depth)])
    g_cb, g_lg, g_lb = narrow(small[2 * depth + 1]), narrow(small[2 * depth + 2]), narrow(small[2 * depth + 3])
    g_taps_full = jnp.stack([small[r0:r0 + CONV_K, i * (d // depth):i * (d // depth) + cc] for i in range(depth)])
    csh = cc // N_DEV
    g_taps_own = lax.dynamic_slice_in_dim(g_taps_full, me * csh, csh, axis=2)
    g_meta_full = small[r0 + CONV_PAD:r0 + CONV_PAD + N_META]
    msh = d // N_DEV
    g_meta_own = lax.dynamic_slice_in_dim(g_meta_full, me * msh, msh, axis=1)

    small_g = [g_meta_own, g_mix, g_taps_own, g_cb, g_lg, g_lb, g_ffn, g_final]
    small_w = [meta_tokens, mix_norm_g, conv_dw_w, conv_dw_b, conv_ln_g, conv_ln_b, ffn_norm_g, final_norm_g]
    small_m = [m_meta_tokens, m_mix_norm_g, m_conv_dw_w, m_conv_dw_b, m_conv_ln_g, m_conv_ln_b, m_ffn_norm_g, m_final_norm_g]
    small_v = [v_meta_tokens, v_mix_norm_g, v_conv_dw_w, v_conv_dw_b, v_conv_ln_g, v_conv_ln_b, v_ffn_norm_g, v_final_norm_g]
    sizes = [int(math.prod(t.shape)) for t in small_w]
    total = sum(sizes)
    rows = -(-total // (8 * 128)) * 8

    def flat(ts):
        v = jnp.concatenate([t.reshape(-1) for t in ts])
        return jnp.pad(v, (0, rows * 128 - total)).reshape(rows, 128)

    def unflat(t):
        v, out, o = t.reshape(-1), [], 0
        for sz, ref in zip(sizes, small_w):
            out.append(v[o:o + sz].reshape(ref.shape))
            o += sz
        return out

    sd, sm, sv_ = _adamw_flat(flat(small_g), flat(small_w), flat(small_m), flat(small_v), "adamw_small")
    s_delta, s_m, s_v = unflat(sd), unflat(sm), unflat(sv_)

    big = []
    for n, w, m, v in zip(big_names, (w_in, w_out, w_gate, w_up, w_down), (m_w_in, m_w_out, m_w_gate, m_w_up, m_w_down),
                          (v_w_in, v_w_out, v_w_gate, v_w_up, v_w_down)):
        big.append(_adamw_shard([parts[(n, i)] for i in range(depth)], w, m, v, transposed[n], f"adamw_{n}"))
    b_in, b_out, b_gate, b_up, b_down = big

    def ordered(k, smalls):
        s_meta, s_mix, s_taps, s_cb, s_lg, s_lb, s_ffn, s_final = smalls
        return [s_meta, s_mix, b_in[k], s_taps, s_cb, s_lg, s_lb, b_out[k], s_ffn, b_gate[k], b_up[k], b_down[k], s_final]

    return (loss, grad_x, *ordered(0, small_g), *ordered(1, s_delta), *ordered(2, s_m), *ordered(3, s_v))
```

```python
import functools
import math

import jax
import jax.numpy as jnp
from jax import lax
from jax.experimental import pallas as pl
from jax.experimental.pallas import tpu as pltpu

F32 = jnp.float32
BF16 = jnp.bfloat16
SDS = jax.ShapeDtypeStruct

N_META = 16
N_HEADS = 8
HEAD_DIM = 64
CONV_K = 31
CONV_PAD = 32
ABLK = 128
ATT_GROUP = 4
ATT_SPLIT = 2
ATT_TOP = 2
EXP_ZERO_AT = 104.0
GONE = 1e30
MASKED = -1e30
FF_CHUNK = 256
EPS = 1e-6
N_DEV = 8
MESH_AXES = ("x", "y", "c")
ADAM_LR = 0.001
ADAM_B1 = 0.9
ADAM_B2 = 0.999
ADAM_EPS = 1e-08
ADAM_WD = 0.01
ADAM_STEP = 10
MIB = 1 << 20
VMEM_LIMIT_MIB = 48
MM_ROWS = 1056
EW_ROWS = 528


def _cp():
    return pltpu.CompilerParams(vmem_limit_bytes=VMEM_LIMIT_MIB * MIB)


def _tile(n, cap, mult):
    best = None
    for t in range(mult, min(n, cap) + 1, mult):
        if n % t == 0:
            best = t
    assert best is not None, (n, cap, mult)
    return best


def _sum8(v):
    r, c = v.shape
    return v.reshape(r // 8, 8, c).sum(axis=0)


def _sigmoid(x):
    return 1.0 / (1.0 + jnp.exp(-x))


NT = (((1,), (1,)), ((), ()))
TN = (((0,), (0,)), ((), ()))


def _rms_mm(h, g, wt, w_row0, n_cols, out_dtype, name, hn_out, rider=None):
    lp, d = h.shape
    tm = _tile(lp, MM_ROWS, 16)
    tn = _tile(math.gcd(n_cols, w_row0), 768, 128)
    off = w_row0 // tn
    ni, nj = lp // tm, n_cols // tn
    rider = rider or _Rider()

    def body(h_ref, g_ref, w_ref, *rest):
        outs, (hn_s,) = rider.split(rest, 2 if hn_out else 1, 1)
        o_ref = outs[0]
        hn_ref = outs[1] if hn_out else None
        rider.start((pl.program_id(0) == 0) & (pl.program_id(1) == 0))
        if ni > 1:
            rider.forward((pl.program_id(0) == ni - 1) & (pl.program_id(1) == 0))

        @pl.when(pl.program_id(1) == 0)
        def _():
            x = h_ref[...]
            r = lax.rsqrt(jnp.mean(x * x, axis=-1, keepdims=True) + EPS)
            hn = ((x * r) * g_ref[...]).astype(BF16)
            hn_s[...] = hn
            if hn_out:
                hn_ref[...] = hn

        o_ref[...] = lax.dot_general(hn_s[...], w_ref[...], NT, preferred_element_type=F32).astype(out_dtype)
        rider.wait((pl.program_id(0) == ni - 1) & (pl.program_id(1) == nj - 1))

    out_shape = [SDS((lp, n_cols), out_dtype)]
    out_specs = [pl.BlockSpec((tm, tn), lambda i, j: (i, j))]
    if hn_out:
        out_shape.append(SDS((lp, d), BF16))
        out_specs.append(pl.BlockSpec((tm, d), lambda i, j: (i, 0)))
    return pl.pallas_call(
        body, out_shape=out_shape + rider.out_shape, grid=(ni, nj),
        in_specs=[pl.BlockSpec((tm, d), lambda i, j: (i, 0)),
                  pl.BlockSpec((1, d), lambda i, j: (0, 0)),
                  pl.BlockSpec((tn, d), lambda i, j: (j + off, 0))] + rider.in_specs(),
        out_specs=out_specs + rider.out_specs(), scratch_shapes=[pltpu.VMEM((tm, d), BF16)] + rider.scratch(),
        name=name, compiler_params=_cp())(h, g, wt, *rider.srcs)


def _mm_nt(a, b, b_row0, n_out, out_dtype, name):
    m, k = a.shape
    tm = _tile(m, MM_ROWS, 16)
    tn = _tile(math.gcd(n_out, b_row0), 512, 128)
    off = b_row0 // tn

    def body(a_ref, b_ref, o_ref):
        o_ref[...] = lax.dot_general(a_ref[...].astype(BF16), b_ref[...], NT,
                                     preferred_element_type=F32).astype(out_dtype)

    return pl.pallas_call(
        body, out_shape=SDS((m, n_out), out_dtype), grid=(m // tm, n_out // tn),
        in_specs=[pl.BlockSpec((tm, k), lambda i, j: (i, 0)), pl.BlockSpec((tn, k), lambda i, j: (j + off, 0))],
        out_specs=pl.BlockSpec((tm, tn), lambda i, j: (i, j)), name=name, compiler_params=_cp())(a, b)


def _mix_out(attn, conv, w, res, name):
    m, ka = attn.shape
    kc = conv.shape[1]
    n = w.shape[1]
    assert ka == kc
    tm = _tile(m, MM_ROWS, 16)
    tn = _tile(n, 512, 128)

    def body(a_ref, c_ref, wa_ref, wc_ref, r_ref, o_ref):
        o_ref[...] = (r_ref[...] + jnp.dot(a_ref[...], wa_ref[...], preferred_element_type=F32)
                      + jnp.dot(c_ref[...], wc_ref[...], preferred_element_type=F32))

    return pl.pallas_call(
        body, out_shape=SDS((m, n), F32), grid=(m // tm, n // tn),
        in_specs=[pl.BlockSpec((tm, ka), lambda i, j: (i, 0)), pl.BlockSpec((tm, kc), lambda i, j: (i, 0)),
                  pl.BlockSpec((ka, tn), lambda i, j: (0, j)), pl.BlockSpec((kc, tn), lambda i, j: (1, j)),
                  pl.BlockSpec((tm, tn), lambda i, j: (i, j))],
        out_specs=pl.BlockSpec((tm, tn), lambda i, j: (i, j)), name=name, compiler_params=_cp())(attn, conv, w, w, res)


def _mm_tn(a, b, name):
    l, m = a.shape
    n = b.shape[1]
    tm = _tile(m, 1408, 128)
    tn = _tile(n, 1024, 128)
    tl = _tile(l, 1408, 128)
    nl = l // tl

    def body(a_ref, b_ref, o_ref, acc):
        @pl.when(pl.program_id(2) == 0)
        def _():
            acc[...] = jnp.zeros_like(acc)

        acc[...] += lax.dot_general(a_ref[...].astype(BF16), b_ref[...].astype(BF16), TN, preferred_element_type=F32)

        @pl.when(pl.program_id(2) == nl - 1)
        def _():
            o_ref[...] = acc[...].astype(BF16)

    return pl.pallas_call(
        body, out_shape=SDS((m, n), BF16), grid=(m // tm, n // tn, nl),
        in_specs=[pl.BlockSpec((tl, tm), lambda i, j, s: (s, i)),
                  pl.BlockSpec((tl, tn), lambda i, j, s: (s, j))],
        out_specs=pl.BlockSpec((tm, tn), lambda i, j, s: (i, j)),
        scratch_shapes=[pltpu.VMEM((tm, tn), F32)], name=name, compiler_params=_cp())(a, b)


def _grad_w_in_t(dqkv, dcc, hn, name, rider=None):
    nq, l, w = dqkv.shape
    nc = dcc.shape[0]
    d = hn.shape[1]
    assert dcc.shape[2] == w
    tn = _tile(d, 1024, 128)
    tl = _tile(l, 1408, 128)
    nl = l // tl
    nj = d // tn
    rider = rider or _Rider()

    def body(q_ref, c_ref, b_ref, *rest):
        (o_ref,), (acc,) = rider.split(rest, 1, 1)
        p, jj, s = pl.program_id(0), pl.program_id(1), pl.program_id(2)
        rider.start((p == 0) & (jj == 0) & (s == 0))

        @pl.when(s == 0)
        def _():
            acc[...] = jnp.zeros_like(acc)

        @pl.when(p < nq)
        def _():
            acc[...] += lax.dot_general(q_ref[...], b_ref[...], TN, preferred_element_type=F32)

        @pl.when(p >= nq)
        def _():
            acc[...] += lax.dot_general(c_ref[...], b_ref[...], TN, preferred_element_type=F32)

        @pl.when(s == nl - 1)
        def _():
            o_ref[...] = acc[...].astype(BF16)

        rider.wait((p == nq + nc - 1) & (jj == nj - 1) & (s == nl - 1))

    out, *arrived = pl.pallas_call(
        body, out_shape=[SDS((nq + nc, w, d), BF16)] + rider.out_shape, grid=(nq + nc, nj, nl),
        in_specs=[pl.BlockSpec((None, tl, w), lambda p, j, s: (jnp.minimum(p, nq - 1), s, 0)),
                  pl.BlockSpec((None, tl, w), lambda p, j, s: (jnp.maximum(p - nq, 0), s, 0)),
                  pl.BlockSpec((tl, tn), lambda p, j, s: (s, j))] + rider.in_specs(),
        out_specs=[pl.BlockSpec((None, w, tn), lambda p, j, s: (p, 0, j))] + rider.out_specs(),
        scratch_shapes=[pltpu.VMEM((w, tn), F32)] + rider.scratch(), name=name,
        compiler_params=_cp())(dqkv, dcc, hn, *rider.srcs)
    return [out.reshape((nq + nc) * w, d)] + arrived


def _rms_bwd_tail(acc, h_ref, g_ref, dres_ref, o_ref, dg_ref):
    x = h_ref[...]
    r = lax.rsqrt(jnp.mean(x * x, axis=-1, keepdims=True) + EPS)
    xr = x * r
    dyv = acc[...]
    gy = dyv * g_ref[...]
    o_ref[...] = dres_ref[...] + r * (gy - xr * jnp.mean(gy * xr, axis=-1, keepdims=True))
    dg_ref[...] += _sum8(dyv * xr)


def _mix_bwd_in(dqkv, dcc, wt, h, g, dres, name, rider=None):
    nq, lp, w = dqkv.shape
    nc = dcc.shape[0]
    d = h.shape[1]
    tm = _tile(lp, MM_ROWS, 16)
    npc = nq + nc
    ni = lp // tm
    rider = rider or _Rider()

    def body(q_ref, c_ref, w_ref, h_ref, g_ref, dres_ref, *rest):
        (o_ref, dg_ref), (acc,) = rider.split(rest, 2, 1)
        i, p = pl.program_id(0), pl.program_id(1)
        rider.start((i == 0) & (p == 0))

        @pl.when(p == 0)
        def _():
            acc[...] = jnp.zeros_like(acc)

        @pl.when((p == 0) & (i == 0))
        def _():
            dg_ref[...] = jnp.zeros_like(dg_ref)

        @pl.when(p < nq)
        def _():
            acc[...] += jnp.dot(q_ref[...], w_ref[...], preferred_element_type=F32)

        @pl.when(p >= nq)
        def _():
            acc[...] += jnp.dot(c_ref[...], w_ref[...], preferred_element_type=F32)

        @pl.when(p == npc - 1)
        def _():
            _rms_bwd_tail(acc, h_ref, g_ref, dres_ref, o_ref, dg_ref)

        rider.wait((i == ni - 1) & (p == npc - 1))

    return pl.pallas_call(
        body, out_shape=[SDS((lp, d), F32), SDS((8, d), F32)] + rider.out_shape, grid=(ni, npc),
        in_specs=[pl.BlockSpec((None, tm, w), lambda i, p: (jnp.minimum(p, nq - 1), i, 0)),
                  pl.BlockSpec((None, tm, w), lambda i, p: (jnp.maximum(p - nq, 0), i, 0)),
                  pl.BlockSpec((w, d), lambda i, p: (p, 0)),
                  pl.BlockSpec((tm, d), lambda i, p: (i, 0)),
                  pl.BlockSpec((1, d), lambda i, p: (0, 0)),
                  pl.BlockSpec((tm, d), lambda i, p: (i, 0))] + rider.in_specs(),
        out_specs=[pl.BlockSpec((tm, d), lambda i, p: (i, 0)), pl.BlockSpec((8, d), lambda i, p: (0, 0))]
        + rider.out_specs(),
        scratch_shapes=[pltpu.VMEM((tm, d), F32)] + rider.scratch(), name=name,
        compiler_params=_cp())(dqkv, dcc, wt, h, g, dres, *rider.srcs)


def _ffn_bwd_in(d_g, d_u, wg_t, wu_t, h, g, dres, name, rider=None):
    lp, f = d_g.shape
    d = h.shape[1]
    tm = _tile(lp, MM_ROWS, 16)
    tk = _tile(f, 512, 128)
    nk = f // tk
    ni = lp // tm
    rider = rider or _Rider()

    def body(dg_in, du_in, wg_ref, wu_ref, h_ref, g_ref, dres_ref, *rest):
        (o_ref, dg_ref), (acc,) = rider.split(rest, 2, 1)
        i, kk = pl.program_id(0), pl.program_id(1)
        rider.start((i == 0) & (kk == 0))

        @pl.when(kk == 0)
        def _():
            acc[...] = jnp.zeros_like(acc)

        @pl.when((kk == 0) & (i == 0))
        def _():
            dg_ref[...] = jnp.zeros_like(dg_ref)

        acc[...] += (jnp.dot(dg_in[...], wg_ref[...], preferred_element_type=F32)
                     + jnp.dot(du_in[...], wu_ref[...], preferred_element_type=F32))

        @pl.when(kk == nk - 1)
        def _():
            _rms_bwd_tail(acc, h_ref, g_ref, dres_ref, o_ref, dg_ref)

        rider.wait((i == ni - 1) & (kk == nk - 1))

    return pl.pallas_call(
        body, out_shape=[SDS((lp, d), F32), SDS((8, d), F32)] + rider.out_shape, grid=(ni, nk),
        in_specs=[pl.BlockSpec((tm, tk), lambda i, j: (i, j)), pl.BlockSpec((tm, tk), lambda i, j: (i, j)),
                  pl.BlockSpec((tk, d), lambda i, j: (j, 0)), pl.BlockSpec((tk, d), lambda i, j: (j, 0)),
                  pl.BlockSpec((tm, d), lambda i, j: (i, 0)),
                  pl.BlockSpec((1, d), lambda i, j: (0, 0)),
                  pl.BlockSpec((tm, d), lambda i, j: (i, 0))] + rider.in_specs(),
        out_specs=[pl.BlockSpec((tm, d), lambda i, j: (i, 0)), pl.BlockSpec((8, d), lambda i, j: (0, 0))]
        + rider.out_specs(),
        scratch_shapes=[pltpu.VMEM((tm, d), F32)] + rider.scratch(), name=name,
        compiler_params=_cp())(d_g, d_u, wg_t, wu_t, h, g, dres, *rider.srcs)


def _ffn_fwd(h, g, wg_t, wu_t, w_d, name, rider=None):
    lp, d = h.shape
    f = w_d.shape[0]
    tm = _tile(lp, MM_ROWS, 16)
    nc = f // FF_CHUNK
    ni = lp // tm
    rider = rider or _Rider()

    def body(h_ref, g_ref, wg_ref, wu_ref, wd_ref, *rest):
        (o_ref, hn_ref, act_ref), (hn_s, acc) = rider.split(rest, 3, 2)
        c = pl.program_id(1)
        rider.start((pl.program_id(0) == 0) & (c == 0))
        if ni > 1:
            rider.forward((pl.program_id(0) == ni - 1) & (c == 0))

        @pl.when(c == 0)
        def _():
            x = h_ref[...]
            r = lax.rsqrt(jnp.mean(x * x, axis=-1, keepdims=True) + EPS)
            hn = ((x * r) * g_ref[...]).astype(BF16)
            hn_s[...] = hn
            hn_ref[...] = hn
            acc[...] = jnp.zeros_like(acc)

        gg = lax.dot_general(hn_s[...], wg_ref[...], NT, preferred_element_type=F32)
        uu = lax.dot_general(hn_s[...], wu_ref[...], NT, preferred_element_type=F32)
        act = ((gg * _sigmoid(gg)) * uu).astype(BF16)
        act_ref[...] = act
        acc[...] += jnp.dot(act, wd_ref[...], preferred_element_type=F32)

        @pl.when(c == nc - 1)
        def _():
            o_ref[...] = h_ref[...] + acc[...]

        rider.wait((pl.program_id(0) == ni - 1) & (c == nc - 1))

    chunk = pl.BlockSpec((FF_CHUNK, d), lambda i, j: (j, 0))
    return pl.pallas_call(
        body, out_shape=[SDS((lp, d), F32), SDS((lp, d), BF16), SDS((lp, f), BF16)] + rider.out_shape, grid=(ni, nc),
        in_specs=[pl.BlockSpec((tm, d), lambda i, j: (i, 0)), pl.BlockSpec((1, d), lambda i, j: (0, 0)),
                  chunk, chunk, chunk] + rider.in_specs(),
        out_specs=[pl.BlockSpec((tm, d), lambda i, j: (i, 0)),
                   pl.BlockSpec((tm, d), lambda i, j: (i, 0)),
                   pl.BlockSpec((tm, FF_CHUNK), lambda i, j: (i, j))] + rider.out_specs(),
        scratch_shapes=[pltpu.VMEM((tm, d), BF16), pltpu.VMEM((tm, d), F32)] + rider.scratch(),
        name=name, compiler_params=_cp())(h, g, wg_t, wu_t, w_d, *rider.srcs)


def _ffn_bwd_act(dh, hn, wg_t, wu_t, w_d, name, rider=None):
    lp, d = dh.shape
    f = w_d.shape[0]
    tm = _tile(lp, MM_ROWS, 16)
    nc = f // FF_CHUNK
    ni = lp // tm
    rider = rider or _Rider()

    def body(dh_ref, hn_ref, wg_ref, wu_ref, wd_ref, *rest):
        (dg_out, du_out), (dh_s,) = rider.split(rest, 2, 1)
        rider.start((pl.program_id(0) == 0) & (pl.program_id(1) == 0))

        @pl.when(pl.program_id(1) == 0)
        def _():
            dh_s[...] = dh_ref[...].astype(BF16)

        gg = lax.dot_general(hn_ref[...], wg_ref[...], NT, preferred_element_type=F32)
        uu = lax.dot_general(hn_ref[...], wu_ref[...], NT, preferred_element_type=F32)
        dact = lax.dot_general(dh_s[...], wd_ref[...], NT, preferred_element_type=F32)
        s = _sigmoid(gg)
        dg_out[...] = (dact * uu * (s * (1.0 + gg * (1.0 - s)))).astype(BF16)
        du_out[...] = (dact * (gg * s)).astype(BF16)
        rider.wait((pl.program_id(0) == ni - 1) & (pl.program_id(1) == nc - 1))

    chunk = pl.BlockSpec((FF_CHUNK, d), lambda i, j: (j, 0))
    out = pl.BlockSpec((tm, FF_CHUNK), lambda i, j: (i, j))
    return pl.pallas_call(
        body, out_shape=[SDS((lp, f), BF16), SDS((lp, f), BF16)] + rider.out_shape, grid=(ni, nc),
        in_specs=[pl.BlockSpec((tm, d), lambda i, j: (i, 0)), pl.BlockSpec((tm, d), lambda i, j: (i, 0)),
                  chunk, chunk, chunk] + rider.in_specs(),
        out_specs=[out, out] + rider.out_specs(), scratch_shapes=[pltpu.VMEM((tm, d), BF16)] + rider.scratch(),
        name=name, compiler_params=_cp())(dh, hn, wg_t, wu_t, w_d, *rider.srcs)


def _loss_head(h, tpad, g, n_real, name):
    lp, d = h.shape
    tm = _tile(lp, EW_ROWS, 16)

    def body(h_ref, t_ref, g_ref, dh_ref, dg_ref, loss_ref):
        i = pl.program_id(0)

        @pl.when(i == 0)
        def _():
            dg_ref[...] = jnp.zeros_like(dg_ref)
            loss_ref[...] = jnp.zeros_like(loss_ref)

        x = h_ref[...]
        r = lax.rsqrt(jnp.mean(x * x, axis=-1, keepdims=True) + EPS)
        xr = x * r
        y = xr * g_ref[...]
        row = i * tm + lax.broadcasted_iota(jnp.int32, (tm, d), 0)
        valid = (row >= N_META) & (row < N_META + n_real)
        diff = jnp.where(valid, y - t_ref[...], 0.0)
        loss_ref[...] += jnp.sum(diff * diff) * (0.5 / d)
        dy = diff * (1.0 / d)
        gy = dy * g_ref[...]
        dh_ref[...] = r * (gy - xr * jnp.mean(gy * xr, axis=-1, keepdims=True))
        dg_ref[...] += _sum8(dy * xr)

    return pl.pallas_call(
        body, out_shape=[SDS((lp, d), F32), SDS((8, d), F32), SDS((8, 128), F32)], grid=(lp // tm,),
        in_specs=[pl.BlockSpec((tm, d), lambda i: (i, 0)),
                  pl.BlockSpec((tm, d), lambda i: (i, 0)),
                  pl.BlockSpec((1, d), lambda i: (0, 0))],
        out_specs=[pl.BlockSpec((tm, d), lambda i: (i, 0)),
                   pl.BlockSpec((8, d), lambda i: (0, 0)),
                   pl.BlockSpec((8, 128), lambda i: (0, 0))],
        name=name, compiler_params=_cp())(h, tpad, g)


def _tri_consts():
    j = lax.broadcasted_iota(jnp.int32, (ABLK, ABLK), 0)
    s = lax.broadcasted_iota(jnp.int32, (ABLK, ABLK), 1)
    after = (j >= s).astype(BF16)
    before = (j < s).astype(BF16)
    ones = jnp.ones((ABLK, ABLK), BF16)
    two = lambda t: jnp.concatenate([t, t], axis=0)
    return (two(jnp.concatenate([after, ones], axis=1)),
            two(after),
            two(jnp.concatenate([before, ones], axis=1)))


def _softplus(z):
    neg_abs = lax.bitcast_convert_type(lax.bitcast_convert_type(z, jnp.uint32) | jnp.uint32(0x80000000), F32)
    return jnp.log(1.0 + jnp.exp(neg_abs)) + jnp.maximum(z, 0.0)


def _split_hi_lo(m):
    hi = m.astype(BF16)
    lo = (m - hi.astype(F32)).astype(BF16)
    return jnp.concatenate([hi, lo], axis=1)


def _head_halves(t2, in_a):
    zero = jnp.zeros_like(t2)
    return jnp.concatenate([jnp.where(in_a, t2, zero), jnp.where(in_a, zero, t2)], axis=0)


def _stack_blocks(t, nblk, in_a):
    return jnp.concatenate([_head_halves(t[u * ABLK:(u + 1) * ABLK], in_a) for u in range(nblk)], axis=0)


def _attn_scale():
    scale = 1.0 / math.sqrt(HEAD_DIM)
    assert math.frexp(scale)[0] == 0.5, "a power of two, so that scaling q in bf16 is exact"
    return scale


def _pow2_below(n):
    assert n & (n - 1) == 0
    return [p for p in (64, 32, 16, 8, 4, 2, 1) if p < n]


class _Rider:
    def __init__(self, jobs=(), srcs=(), out_shape=()):
        self.jobs, self.srcs, self.out_shape = list(jobs), list(srcs), list(out_shape)
        self.any = [pl.BlockSpec(memory_space=pl.ANY)]

    def split(self, rest, n_out, n_scratch):
        ni, no = len(self.srcs), len(self.out_shape)
        self.ins, outs = rest[:ni], rest[ni:ni + n_out]
        self.outs = rest[ni + n_out:ni + n_out + no]
        scratch = rest[ni + n_out + no:ni + n_out + no + n_scratch]
        self.sems = rest[ni + n_out + no + n_scratch:]
        return outs, scratch

    def start(self, first):
        self.forwarded = False
        if self.jobs:
            @pl.when(first)
            def _():
                _exchange_start(self.jobs, self.ins, self.outs, *self.sems)

    def forward(self, late):
        self.forwarded = True
        if self.jobs:
            @pl.when(late)
            def _():
                _exchange_forward(self.jobs, self.ins, self.outs, *self.sems)

    def wait(self, last):
        if self.jobs:
            @pl.when(last)
            def _():
                if not self.forwarded:
                    _exchange_forward(self.jobs, self.ins, self.outs, *self.sems)
                _exchange_wait(self.jobs, self.ins, self.outs, *self.sems)

    def in_specs(self):
        return self.any * len(self.srcs)

    def out_specs(self):
        return self.any * len(self.out_shape)

    def scratch(self):
        return _exchange_sems(len(self.jobs)) if self.jobs else []


def _attn_fwd(qkv, tri_fwd, name, rider=None):
    lp = qkv.shape[0]
    n_pairs = (N_HEADS * HEAD_DIM) // 128
    nb = lp // ABLK
    assert nb <= 128 and 2 * HEAD_DIM == 128
    scale = _attn_scale()
    nt = (((1,), (1,)), ((), ()))
    rider = rider or _Rider()

    def body(q_ref, k_ref, v_ref, tri_ref, *rest):
        (o_ref, rs_ref), (r_s, acc_s, rs_s) = rider.split(rest, 2, 3)
        rider.start(pl.program_id(0) == 0)
        if n_pairs > 1:
            rider.forward(pl.program_id(0) == n_pairs - 1)
        lane = lax.broadcasted_iota(jnp.int32, (ABLK, 128), 1)
        row = lax.broadcasted_iota(jnp.int32, (ABLK, 128), 0)
        in_a = lane < HEAD_DIM
        causal = lane < row

        def qblock(i, carry):
            q0 = pl.multiple_of(i * ABLK, ABLK)
            q2 = q_ref[pl.ds(q0, ABLK), :] * scale
            r_s[...] = jnp.zeros_like(r_s)
            acc_s[...] = jnp.zeros_like(acc_s)
            rs_s[...] = jnp.full_like(rs_s, GONE)

            def live():
                least = jnp.min(jnp.minimum(r_s[0], r_s[1]), axis=0, keepdims=True)
                return (least[0, 0] < EXP_ZERO_AT).astype(jnp.int32)

            def step(kb0, nblk, diag):
                k0 = pl.multiple_of(kb0 * ABLK, ABLK)
                kbd = _stack_blocks(k_ref[pl.ds(k0, nblk * ABLK), :], nblk, in_a)
                vbd = _stack_blocks(v_ref[pl.ds(k0, nblk * ABLK), :], nblk, in_a)
                z = lax.dot_general(q2, kbd, nt, preferred_element_type=F32)
                ncol = 2 * nblk
                zt = [z[:, c * 128:(c + 1) * 128] for c in range(ncol)]
                if diag:
                    zt = [jnp.where(causal, t, MASKED) if c >= ncol - 2 else t for c, t in enumerate(zt)]
                bounds = [ncol * j // ATT_SPLIT for j in range(ATT_SPLIT + 1)]
                ce = [None] * ncol
                for c0, c1 in reversed([b for b in zip(bounds[:-1], bounds[1:]) if b[0] < b[1]]):
                    parts = [_split_hi_lo(_softplus(zt[c])) for c in range(c0, c1)]
                    got = jnp.dot(jnp.concatenate(parts, axis=0), tri_ref[...], preferred_element_type=F32)
                    for c in range(c0, c1):
                        ce[c] = got[(c - c0) * 128:(c - c0 + 1) * 128]
                rr = [r_s[0], r_s[1]]
                rsv = [rs_s[:, :128], rs_s[:, 128:]]
                ws = [None] * ncol
                for u in reversed(range(nblk)):
                    for hh in range(2):
                        c = 2 * u + hh
                        ws[c] = jnp.exp(zt[c] - ce[c][:, :128] - rr[hh]).astype(BF16)
                        rsv[hh] = jnp.where(lane == kb0 + u, rr[hh], rsv[hh])
                        rr[hh] = rr[hh] + ce[c][:, 128:]
                acc_s[...] += jnp.dot(jnp.concatenate(ws, axis=1), vbd, preferred_element_type=F32)
                r_s[0] = rr[0]
                r_s[1] = rr[1]
                rs_s[:, :128] = rsv[0]
                rs_s[:, 128:] = rsv[1]

            n_top = jnp.minimum(i, ATT_TOP)
            for t in range(ATT_TOP + 1):
                @pl.when(n_top == t)
                def _():
                    step(i - t, t + 1, True)

            i_low = i - n_top
            n_grp = i_low // ATT_GROUP

            def more(c):
                return (c[0] < n_grp) & (c[1] > 0)

            def inner(c):
                step(i_low - ATT_GROUP * (c[0] + 1), ATT_GROUP, False)
                return c[0] + 1, live()

            _, alive = lax.while_loop(more, inner, (jnp.int32(0), live()))
            rem = i_low - ATT_GROUP * n_grp
            for p in _pow2_below(ATT_GROUP):
                def last_steps(p=p):
                    step(rem & (p - 1), p, False)
                    return live()

                alive = lax.cond(((rem & p) != 0) & (alive > 0), last_steps, lambda alive=alive: alive)

            o_ref[pl.ds(q0, ABLK), :] = acc_s[...].astype(BF16)
            rs_ref[pl.ds(q0, ABLK), :] = rs_s[...]
            return carry

        lax.fori_loop(0, nb, qblock, 0)
        rider.wait(pl.program_id(0) == n_pairs - 1)

    col = lambda o: (lambda p: (0, p + o))
    return pl.pallas_call(
        body, out_shape=[SDS((lp, n_pairs * 128), BF16), SDS((lp, n_pairs * 256), F32)] + rider.out_shape,
        grid=(n_pairs,),
        in_specs=[pl.BlockSpec((lp, 128), col(0)), pl.BlockSpec((lp, 128), col(n_pairs)),
                  pl.BlockSpec((lp, 128), col(2 * n_pairs)), pl.BlockSpec((256, 256), lambda p: (0, 0))]
        + rider.in_specs(),
        out_specs=[pl.BlockSpec((lp, 128), col(0)), pl.BlockSpec((lp, 256), col(0))] + rider.out_specs(),
        scratch_shapes=[pltpu.VMEM((2, ABLK, 128), F32), pltpu.VMEM((ABLK, 128), F32), pltpu.VMEM((ABLK, 256), F32)]
        + rider.scratch(),
        name=name, compiler_params=_cp())(qkv, qkv, qkv, tri_fwd, *rider.srcs)


def _attn_bwd(qkv, d_out, rsave, tri_after, tri_before, name, rider=None):
    lp = qkv.shape[0]
    n_pairs = (N_HEADS * HEAD_DIM) // 128
    nb = lp // ABLK
    scale = _attn_scale()
    nt = (((1,), (1,)), ((), ()))
    tn = (((0,), (0,)), ((), ()))
    rider = rider or _Rider()

    def body(q_ref, k_ref, v_ref, do_ref, rs_ref, ta_ref, tb_ref, *rest):
        (o_ref,), (dk_s, dv_s, dq_s, pc_s) = rider.split(rest, 1, 4)
        rider.start(pl.program_id(0) == 0)
        lane = lax.broadcasted_iota(jnp.int32, (ABLK, 128), 1)
        row = lax.broadcasted_iota(jnp.int32, (ABLK, 128), 0)
        in_a = lane < HEAD_DIM
        causal = lane < row
        dk_s[...] = jnp.zeros_like(dk_s)
        dv_s[...] = jnp.zeros_like(dv_s)

        def qblock(i, carry):
            q0 = pl.multiple_of(i * ABLK, ABLK)
            q2 = q_ref[pl.ds(q0, ABLK), :] * scale
            do2 = do_ref[pl.ds(q0, ABLK), :]
            q_st = _head_halves(q2, in_a)
            do_st = _head_halves(do2, in_a)
            dq_s[...] = jnp.zeros_like(dq_s)
            pc_s[...] = jnp.zeros_like(pc_s)

            def step(kb0, nblk, diag):
                k0 = pl.multiple_of(kb0 * ABLK, ABLK)
                kbd = _stack_blocks(k_ref[pl.ds(k0, nblk * ABLK), :], nblk, in_a)
                vbd = _stack_blocks(v_ref[pl.ds(k0, nblk * ABLK), :], nblk, in_a)
                z = lax.dot_general(q2, kbd, nt, preferred_element_type=F32)
                dw = lax.dot_general(do2, vbd, nt, preferred_element_type=F32)
                ncol = 2 * nblk
                zt = [z[:, c * 128:(c + 1) * 128] for c in range(ncol)]
                if diag:
                    zt = [jnp.where(causal, t, MASKED) if c >= ncol - 2 else t for c, t in enumerate(zt)]
                bounds = [ncol * j // ATT_SPLIT for j in range(ATT_SPLIT + 1)]
                batches = [range(c0, c1) for c0, c1 in zip(bounds[:-1], bounds[1:]) if c0 < c1]
                sps, ex, ws, dls, pe = [None] * ncol, [None] * ncol, [None] * ncol, [None] * ncol, [None] * ncol

                def mass(cols):
                    for c in cols:
                        sps[c] = _softplus(zt[c])
                    got = jnp.dot(jnp.concatenate([_split_hi_lo(sps[c]) for c in cols], axis=0), ta_ref[...],
                                  preferred_element_type=F32)
                    for j, c in enumerate(cols):
                        ex[c] = got[j * 128:(j + 1) * 128]

                def weights(cols):
                    for c in cols:
                        u, hh = c // 2, c % 2
                        r_saved = jnp.sum(jnp.where(lane == kb0 + u, rs_ref[pl.ds(q0, ABLK), hh * 128:(hh + 1) * 128],
                                                    0.0), axis=1, keepdims=True)
                        w = jnp.exp(zt[c] - ex[c] - r_saved)
                        ws[c] = w.astype(BF16)
                        dls[c] = dw[:, c * 128:(c + 1) * 128] * w
                    got = jnp.dot(jnp.concatenate([_split_hi_lo(dls[c]) for c in cols], axis=0), tb_ref[...],
                                  preferred_element_type=F32)
                    for j, c in enumerate(cols):
                        pe[c] = got[j * 128:(j + 1) * 128]

                mass(batches[0])
                for j in range(len(batches)):
                    if j + 1 < len(batches):
                        mass(batches[j + 1])
                    weights(batches[j])
                pc = [pc_s[0], pc_s[1]]
                dzs = []
                for c in range(ncol):
                    hh = c % 2
                    one_minus_beta = jnp.exp(-sps[c])
                    dz = dls[c] * one_minus_beta - (pe[c][:, :128] + pc[hh]) * (1.0 - one_minus_beta)
                    pc[hh] = pc[hh] + pe[c][:, 128:]
                    dzs.append(dz.astype(BF16))
                pc_s[0] = pc[0]
                pc_s[1] = pc[1]
                dq_s[...] += jnp.dot(jnp.concatenate(dzs, axis=1), kbd, preferred_element_type=F32)
                by_head = lambda ts: jnp.concatenate([jnp.concatenate(ts[0::2], axis=1), jnp.concatenate(ts[1::2], axis=1)],
                                                     axis=0)
                rows = pl.ds(k0, nblk * ABLK)
                dk_s[rows, :] += lax.dot_general(by_head(dzs), q_st, tn, preferred_element_type=F32)
                dv_s[rows, :] += lax.dot_general(by_head(ws), do_st, tn, preferred_element_type=F32)

            gone = jnp.min(rs_ref[pl.ds(q0, ABLK), :], axis=0, keepdims=True) >= EXP_ZERO_AT
            lane1 = lax.broadcasted_iota(jnp.int32, (1, 128), 1)
            first = jnp.sum(jnp.where(gone[:, :128] & gone[:, 128:] & (lane1 < i), 1.0, 0.0)).astype(jnp.int32)
            n_top = jnp.minimum(i, ATT_TOP)
            i_low = i - n_top
            n_grp = i_low // ATT_GROUP
            rem = i_low - ATT_GROUP * n_grp
            for p in reversed(_pow2_below(ATT_GROUP)):
                @pl.when(((rem & p) != 0) & ((rem & (p - 1)) + p > first))
                def _():
                    step(rem & (p - 1), p, False)

            def inner(g, c2):
                step(rem + ATT_GROUP * g, ATT_GROUP, False)
                return c2

            lax.fori_loop(jnp.maximum(first - rem, 0) // ATT_GROUP, n_grp, inner, 0)
            for t in range(ATT_TOP + 1):
                @pl.when(n_top == t)
                def _():
                    step(i - t, t + 1, True)

            o_ref[0, pl.ds(q0, ABLK), :] = (dq_s[...] * scale).astype(BF16)
            o_ref[0, pl.ds(q0, ABLK), :] = (dq_s[...] * scale).astype(BF16)
            return carry

        lax.fori_loop(0, nb, qblock, 0)
        o_ref[1] = dk_s[...].astype(BF16)
        o_ref[2] = dv_s[...].astype(BF16)
        rider.wait(pl.program_id(0) == n_pairs - 1)

    col = lambda o: (lambda p: (0, p + o))
    return pl.pallas_call(
        body, out_shape=[SDS((3, lp, n_pairs * 128), BF16)] + rider.out_shape, grid=(n_pairs,),
        in_specs=[pl.BlockSpec((lp, 128), col(0)), pl.BlockSpec((lp, 128), col(n_pairs)),
                  pl.BlockSpec((lp, 128), col(2 * n_pairs)), pl.BlockSpec((lp, 128), col(0)),
                  pl.BlockSpec((lp, 256), col(0)),
                  pl.BlockSpec((256, 128), lambda p: (0, 0)), pl.BlockSpec((256, 256), lambda p: (0, 0))]
        + rider.in_specs(),
        out_specs=[pl.BlockSpec((3, lp, 128), lambda p: (0, 0, p))] + rider.out_specs(),
        scratch_shapes=[pltpu.VMEM((lp, 128), F32), pltpu.VMEM((lp, 128), F32),
                        pltpu.VMEM((ABLK, 128), F32), pltpu.VMEM((2, ABLK, 128), F32)] + rider.scratch(),
        name=name, compiler_params=_cp())(qkv, qkv, qkv, d_out, rsave, tri_after, tri_before, *rider.srcs)


def _conv_fwd_dw(cacg, w, b, name, rider=None):
    lp = cacg.shape[0]
    c = cacg.shape[1] // 2
    ncb = c // 128
    nchunk = lp // ABLK
    off = CONV_PAD - (CONV_K - 1)
    rider = rider or _Rider()

    def body(a_ref, g_ref, w_ref, b_ref, *rest):
        (y_ref,), (upad,) = rider.split(rest, 1, 1)
        rider.start(pl.program_id(0) == 0)
        if ncb > 1:
            rider.forward(pl.program_id(0) == ncb - 1)
        upad[0:CONV_PAD, :] = jnp.zeros((CONV_PAD, 128), F32)

        def fill(ch, carry):
            base = pl.multiple_of(ch * ABLK, ABLK)
            upad[pl.ds(base + CONV_PAD, ABLK), :] = a_ref[pl.ds(base, ABLK), :] * _sigmoid(g_ref[pl.ds(base, ABLK), :])
            return carry

        lax.fori_loop(0, nchunk, fill, 0)

        def comp(ch, carry):
            base = pl.multiple_of(ch * ABLK, ABLK)
            acc = jnp.zeros((ABLK, 128), F32)
            for k in range(CONV_K):
                acc = acc + upad[pl.ds(base + (off + k), ABLK), :] * w_ref[k:k + 1, :]
            y_ref[pl.ds(base, ABLK), :] = acc + b_ref[...]
            return carry

        lax.fori_loop(0, nchunk, comp, 0)
        rider.wait(pl.program_id(0) == ncb - 1)

    return pl.pallas_call(
        body, out_shape=[SDS((lp, c), F32)] + rider.out_shape, grid=(ncb,),
        in_specs=[pl.BlockSpec((lp, 128), lambda j: (0, j)), pl.BlockSpec((lp, 128), lambda j: (0, j + ncb)),
                  pl.BlockSpec((CONV_PAD, 128), lambda j: (0, j)), pl.BlockSpec((1, 128), lambda j: (0, j))]
        + rider.in_specs(),
        out_specs=[pl.BlockSpec((lp, 128), lambda j: (0, j))] + rider.out_specs(),
        scratch_shapes=[pltpu.VMEM((lp + CONV_PAD, 128), F32)] + rider.scratch(), name=name,
        compiler_params=_cp())(cacg, cacg, w, b, *rider.srcs)


def _ln_parts(x, g, b):
    mu = jnp.mean(x, axis=-1, keepdims=True)
    xc = x - mu
    rstd = lax.rsqrt(jnp.mean(xc * xc, axis=-1, keepdims=True) + EPS)
    xh = xc * rstd
    return xh, rstd, xh * g + b


def _conv_fwd_ln(yc, g, b, name):
    lp, c = yc.shape
    tm = _tile(lp, EW_ROWS, 16)

    def body(y_ref, g_ref, b_ref, o_ref):
        _, _, ln = _ln_parts(y_ref[...], g_ref[...], b_ref[...])
        o_ref[...] = (ln * _sigmoid(ln)).astype(BF16)

    return pl.pallas_call(
        body, out_shape=SDS((lp, c), BF16), grid=(lp // tm,),
        in_specs=[pl.BlockSpec((tm, c), lambda i: (i, 0)), pl.BlockSpec((1, c), lambda i: (0, 0)),
                  pl.BlockSpec((1, c), lambda i: (0, 0))],
        out_specs=pl.BlockSpec((tm, c), lambda i: (i, 0)), name=name, compiler_params=_cp())(yc, g, b)


def _conv_bwd_ln(yc, dout, g, b, name):
    lp, c = yc.shape
    tm = _tile(lp, EW_ROWS, 16)

    def body(y_ref, d_ref, g_ref, b_ref, o_ref, dg_ref, db_ref):
        @pl.when(pl.program_id(0) == 0)
        def _():
            dg_ref[...] = jnp.zeros_like(dg_ref)
            db_ref[...] = jnp.zeros_like(db_ref)

        xh, rstd, ln = _ln_parts(y_ref[...], g_ref[...], b_ref[...])
        s = _sigmoid(ln)
        dln = d_ref[...] * (s * (1.0 + ln * (1.0 - s)))
        dg_ref[...] += _sum8(dln * xh)
        db_ref[...] += _sum8(dln)
        dxh = dln * g_ref[...]
        o_ref[...] = rstd * (dxh - jnp.mean(dxh, axis=-1, keepdims=True)
                             - xh * jnp.mean(dxh * xh, axis=-1, keepdims=True))

    return pl.pallas_call(
        body, out_shape=[SDS((lp, c), F32), SDS((8, c), F32), SDS((8, c), F32)], grid=(lp // tm,),
        in_specs=[pl.BlockSpec((tm, c), lambda i: (i, 0)), pl.BlockSpec((tm, c), lambda i: (i, 0)),
                  pl.BlockSpec((1, c), lambda i: (0, 0)), pl.BlockSpec((1, c), lambda i: (0, 0))],
        out_specs=[pl.BlockSpec((tm, c), lambda i: (i, 0)), pl.BlockSpec((8, c), lambda i: (0, 0)),
                   pl.BlockSpec((8, c), lambda i: (0, 0))],
        name=name, compiler_params=_cp())(yc, dout, g, b)


def _conv_bwd_dw(dyc, cacg, w, name):
    lp, c = dyc.shape
    ncb = c // 128
    nchunk = lp // ABLK
    off = CONV_PAD - (CONV_K - 1)

    def body(dy_ref, a_ref, g_ref, w_ref, dcc_ref, dw_ref, db_ref, upad, dypad, dwacc):
        upad[0:CONV_PAD, :] = jnp.zeros((CONV_PAD, 128), F32)
        dypad[lp:lp + CONV_PAD, :] = jnp.zeros((CONV_PAD, 128), F32)
        dwacc[...] = jnp.zeros_like(dwacc)
        db_ref[...] = jnp.zeros_like(db_ref)

        def fill(ch, carry):
            base = pl.multiple_of(ch * ABLK, ABLK)
            upad[pl.ds(base + CONV_PAD, ABLK), :] = a_ref[pl.ds(base, ABLK), :] * _sigmoid(g_ref[pl.ds(base, ABLK), :])
            dypad[pl.ds(base, ABLK), :] = dy_ref[pl.ds(base, ABLK), :]
            return carry

        lax.fori_loop(0, nchunk, fill, 0)

        def comp(ch, carry):
            base = pl.multiple_of(ch * ABLK, ABLK)
            dy = dy_ref[pl.ds(base, ABLK), :]
            du = jnp.zeros((ABLK, 128), F32)
            for k in range(CONV_K):
                du = du + dypad[pl.ds(base + (CONV_K - 1 - k), ABLK), :] * w_ref[k:k + 1, :]
                dwacc[k * 8:(k + 1) * 8, :] += _sum8(dy * upad[pl.ds(base + (off + k), ABLK), :])
            db_ref[...] += _sum8(dy)
            a = a_ref[pl.ds(base, ABLK), :]
            s = _sigmoid(g_ref[pl.ds(base, ABLK), :])
            dcc_ref[0, pl.ds(base, ABLK), :] = (du * s).astype(BF16)
            dcc_ref[1, pl.ds(base, ABLK), :] = (du * a * (s * (1.0 - s))).astype(BF16)
            return carry

        lax.fori_loop(0, nchunk, comp, 0)
        dw_ref[...] = dwacc[...].reshape(CONV_PAD, 8, 128).sum(axis=1)

    return pl.pallas_call(
        body, out_shape=[SDS((2, lp, c), BF16), SDS((CONV_PAD, c), F32), SDS((8, c), F32)],
        grid=(ncb,),
        in_specs=[pl.BlockSpec((lp, 128), lambda j: (0, j)), pl.BlockSpec((lp, 128), lambda j: (0, j)),
                  pl.BlockSpec((lp, 128), lambda j: (0, j + ncb)), pl.BlockSpec((CONV_PAD, 128), lambda j: (0, j))],
        out_specs=[pl.BlockSpec((2, lp, 128), lambda j: (0, 0, j)),
                   pl.BlockSpec((CONV_PAD, 128), lambda j: (0, j)), pl.BlockSpec((8, 128), lambda j: (0, j))],
        scratch_shapes=[pltpu.VMEM((lp + CONV_PAD, 128), F32), pltpu.VMEM((lp + CONV_PAD, 128), F32),
                        pltpu.VMEM((CONV_PAD * 8, 128), F32)],
        name=name, compiler_params=_cp())(dyc, cacg, cacg, w)


def _mesh_pos():
    x, y, c = lax.axis_index("x"), lax.axis_index("y"), lax.axis_index("c")
    return x, y, c


def _peer(pos, r):
    x, y, c = pos
    px = (1 - x) if (r >> 2) & 1 else x
    py = (1 - y) if (r >> 1) & 1 else y
    pc = (1 - c) if r & 1 else c
    return (px, py, pc), 4 * px + 2 * py + pc


SIBLING = 1
OTHER_CHIPS = (2, 4, 6)
VIA_SIBLING = (3, 5, 7)


class _Job:
    def __init__(self, src, dst, scatter, src_layer=None, dst_layer=None):
        self.src, self.dst, self.scatter, self.src_layer, self.dst_layer = src, dst, scatter, src_layer, dst_layer

    def src_view(self, ins, idx):
        v = ins[self.src] if self.src_layer is None else ins[self.src].at[self.src_layer]
        return v.at[idx] if self.scatter else v

    def dst_view(self, outs, slot):
        v = outs[self.dst] if self.dst_layer is None else outs[self.dst].at[self.dst_layer]
        return v.at[slot]


def _remote(job, j, r, src, dst, to, send, recv):
    return pltpu.make_async_remote_copy(src_ref=src, dst_ref=dst, send_sem=send.at[j, r - 1], recv_sem=recv.at[j, r - 1],
                                        device_id=to, device_id_type=pl.DeviceIdType.MESH)


def _exchange_start(jobs, ins, outs, send, recv, loc):
    pos = _mesh_pos()
    me = 4 * pos[0] + 2 * pos[1] + pos[2]
    for j, job in enumerate(jobs):
        pltpu.make_async_copy(job.src_view(ins, me), job.dst_view(outs, me), loc.at[j]).start()
        for r in (range(1, N_DEV) if job.scatter else (SIBLING,) + OTHER_CHIPS):
            peer, peer_idx = _peer(pos, r)
            _remote(job, j, r, job.src_view(ins, peer_idx), job.dst_view(outs, me), peer, send, recv).start()


def _exchange_forward(jobs, ins, outs, send, recv, loc):
    pos = _mesh_pos()
    sibling, _ = _peer(pos, SIBLING)
    for j, job in enumerate(jobs):
        if job.scatter:
            continue
        for r in OTHER_CHIPS:
            peer, peer_idx = _peer(pos, r)
            slot = job.dst_view(outs, peer_idx)
            _remote(job, j, r, job.src_view(ins, peer_idx), slot, peer, send, recv).wait_recv()
            _remote(job, j, r ^ SIBLING, slot, slot, sibling, send, recv).start()


def _exchange_wait(jobs, ins, outs, send, recv, loc):
    pos = _mesh_pos()
    me = 4 * pos[0] + 2 * pos[1] + pos[2]
    for j, job in enumerate(jobs):
        for r in range(1, N_DEV):
            peer, peer_idx = _peer(pos, r)
            cp = _remote(job, j, r, job.src_view(ins, peer_idx), job.dst_view(outs, peer_idx), peer, send, recv)
            if job.scatter or r not in OTHER_CHIPS:
                cp.wait_recv()
            cp.wait_send()
        pltpu.make_async_copy(job.src_view(ins, me), job.dst_view(outs, me), loc.at[j]).wait()


def _exchange_sems(n_jobs):
    return [pltpu.SemaphoreType.DMA((n_jobs, N_DEV - 1)), pltpu.SemaphoreType.DMA((n_jobs, N_DEV - 1)),
            pltpu.SemaphoreType.DMA((n_jobs,))]


def _exchange(jobs, arrs, out_shape, name):
    n_in, n_out = len(arrs), len(out_shape)
    any_spec = pl.BlockSpec(memory_space=pl.ANY)

    def body(*refs):
        ins, outs, sems = refs[:n_in], refs[n_in:n_in + n_out], refs[n_in + n_out:]
        _exchange_start(jobs, ins, outs, *sems)
        _exchange_forward(jobs, ins, outs, *sems)
        _exchange_wait(jobs, ins, outs, *sems)

    return pl.pallas_call(
        body, out_shape=out_shape, in_specs=[any_spec] * n_in, out_specs=[any_spec] * n_out,
        scratch_shapes=_exchange_sems(len(jobs)),
        name=name, compiler_params=pltpu.CompilerParams(has_side_effects=True))(*arrs)


def _all_reduce_small(placed, cuts, rows, w, name):
    n_in = len(placed)

    def body(*refs):
        in_refs, o_refs = refs[:n_in], refs[n_in:n_in + len(cuts)]
        buf, send, recv = refs[n_in + len(cuts):]
        pos = _mesh_pos()
        me = 4 * pos[0] + 2 * pos[1] + pos[2]
        buf[me] = jnp.zeros((rows, w), F32)
        for ref, (arr, row0, col0, is_partial) in zip(in_refs, placed):
            val = ref[...].sum(axis=0, keepdims=True) if is_partial else ref[...]
            buf[me, row0:row0 + val.shape[0], col0:col0 + val.shape[1]] = val
        for r in range(1, N_DEV):
            peer, _ = _peer(pos, r)
            pltpu.make_async_remote_copy(src_ref=buf.at[me], dst_ref=buf.at[me], send_sem=send.at[r - 1],
                                         recv_sem=recv.at[r - 1], device_id=peer,
                                         device_id_type=pl.DeviceIdType.MESH).start()
        for r in range(1, N_DEV):
            peer, peer_idx = _peer(pos, r)
            cp = pltpu.make_async_remote_copy(src_ref=buf.at[me], dst_ref=buf.at[peer_idx], send_sem=send.at[r - 1],
                                              recv_sem=recv.at[r - 1], device_id=peer,
                                              device_id_type=pl.DeviceIdType.MESH)
            cp.wait_recv()
            cp.wait_send()
        acc = buf[0]
        for dev in range(1, N_DEV):
            acc = acc + buf[dev]
        for o_ref, (_, pieces) in zip(o_refs, cuts):
            for index, row0, nrows, col0, ncols in pieces:
                o_ref[index] = acc[row0:row0 + nrows, col0:col0 + ncols].reshape(o_ref.at[index].shape)

    vmem = pl.BlockSpec(memory_space=pltpu.VMEM)
    return pl.pallas_call(
        body, out_shape=[SDS(shape, F32) for shape, _ in cuts], in_specs=[vmem] * n_in, out_specs=[vmem] * len(cuts),
        scratch_shapes=[pltpu.VMEM((N_DEV, rows, w), F32), pltpu.SemaphoreType.DMA((N_DEV - 1,)),
                        pltpu.SemaphoreType.DMA((N_DEV - 1,))],
        name=name, compiler_params=pltpu.CompilerParams(has_side_effects=True))(*[p[0] for p in placed])


def _adamw_math(w, g, m, v):
    m = ADAM_B1 * m + (1.0 - ADAM_B1) * g
    v = ADAM_B2 * v + (1.0 - ADAM_B2) * (g * g)
    m_hat = m / (1.0 - ADAM_B1 ** ADAM_STEP)
    v_hat = v / (1.0 - ADAM_B2 ** ADAM_STEP)
    delta = -ADAM_LR * (m_hat / (jnp.sqrt(v_hat) + ADAM_EPS) + ADAM_WD * w)
    return delta, m, v


def _adamw_shard(parts, w, m, v, transposed, name):
    depth = len(parts)
    _, rr, cc = parts[0].shape
    if transposed:
        tc = _tile(cc, 256, 128)
        nt = cc // tc
        part_block, blk = (N_DEV, rr, tc), pl.BlockSpec((None, tc, rr), lambda l, i: (l, i, 0))
    else:
        tr = _tile(rr, 256, 8)
        nt = rr // tr
        part_block, blk = (N_DEV, tr, cc), pl.BlockSpec((None, tr, cc), lambda l, i: (l, i, 0))

    def body(*refs):
        p_refs = refs[:depth]
        w_ref, m_ref, v_ref, g_out, d_out, m_out, v_out = refs[depth:]
        for li in range(depth):
            @pl.when(pl.program_id(0) == li)
            def _(p_ref=p_refs[li]):
                g = p_ref[0].astype(F32)
                for dev in range(1, N_DEV):
                    g = g + p_ref[dev].astype(F32)
                if transposed:
                    g = g.T
                delta, mm, vv = _adamw_math(w_ref[...], g, m_ref[...], v_ref[...])
                g_out[...] = g
                d_out[...] = delta
                m_out[...] = mm
                v_out[...] = vv

    def part_spec(li):
        at = lambda l, i: jnp.where(l == li, i, jnp.where(l < li, 0, nt - 1))
        return pl.BlockSpec(part_block, (lambda l, i: (0, 0, at(l, i))) if transposed else (lambda l, i: (0, at(l, i), 0)))

    return pl.pallas_call(
        body, out_shape=[SDS(w.shape, F32)] * 4, grid=(depth, nt),
        in_specs=[part_spec(li) for li in range(depth)] + [blk, blk, blk],
        out_specs=[blk] * 4, name=name, compiler_params=_cp())(*parts, w, m, v)


def _adamw_small(gs, ws, ms, vs, name):
    n = len(gs)

    def body(*refs):
        g_refs, w_refs, m_refs, v_refs = (refs[k * n:(k + 1) * n] for k in range(4))
        outs = refs[4 * n:]
        for k in range(n):
            delta, mm, vv = _adamw_math(w_refs[k][...], g_refs[k][...], m_refs[k][...], v_refs[k][...])
            outs[k][...] = delta
            outs[n + k][...] = mm
            outs[2 * n + k][...] = vv

    res = pl.pallas_call(body, out_shape=[SDS(w.shape, F32) for w in ws] * 3, name=name,
                         compiler_params=_cp())(*gs, *ws, *ms, *vs)
    return res[:n], res[n:2 * n], res[2 * n:]


def _from_cols(t):
    return jnp.transpose(t, (1, 0, 2)).reshape(t.shape[1], N_DEV * t.shape[2])


def _swap(t):
    return jnp.swapaxes(t, -1, -2)


def kernel(x, meta_tokens, mix_norm_g, w_in, conv_dw_w, conv_dw_b, conv_ln_g, conv_ln_b, w_out, ffn_norm_g, w_gate, w_up, w_down, final_norm_g, loss_target, m_meta_tokens, m_mix_norm_g, m_w_in, m_conv_dw_w, m_conv_dw_b, m_conv_ln_g, m_conv_ln_b, m_w_out, m_ffn_norm_g, m_w_gate, m_w_up, m_w_down, m_final_norm_g, v_meta_tokens, v_mix_norm_g, v_w_in, v_conv_dw_w, v_conv_dw_b, v_conv_ln_g, v_conv_ln_b, v_w_out, v_ffn_norm_g, v_w_gate, v_w_up, v_w_down, v_final_norm_g):
    depth, d, in_shard = w_in.shape
    seq = x.shape[1]
    sb = N_HEADS * HEAD_DIM
    cc = conv_dw_w.shape[2] * N_DEV
    ff = w_gate.shape[2] * N_DEV
    assert in_shard * N_DEV == 3 * sb + 2 * cc and x.shape[0] == 1
    lr = N_META + seq
    lp = -(-lr // ABLK) * ABLK
    me = 4 * lax.axis_index("x") + 2 * lax.axis_index("y") + lax.axis_index("c")

    big_names = ("w_in", "w_out", "w_gate", "w_up", "w_down")
    transposed = {"w_in": True, "w_out": False, "w_gate": True, "w_up": True, "w_down": False}
    shard = dict(w_in=_swap(w_in).astype(BF16), w_out=w_out.astype(BF16), w_gate=_swap(w_gate).astype(BF16),
                 w_up=_swap(w_up).astype(BF16), w_down=w_down.astype(BF16))

    def gather_of(keys):
        names = sorted({n for n, _ in keys}, key=big_names.index)
        jobs = [_Job(names.index(n), j, False, src_layer=i) for j, (n, i) in enumerate(keys)]
        return jobs, [shard[n] for n in names], [SDS((N_DEV,) + shard[n].shape[1:], BF16) for n, _ in keys]

    first_keys = [("w_in", 0)]

    def riding(keys):
        return _Rider(*gather_of(keys)) if keys else None

    def receive(keys, arrays):
        for (n, li), t in zip(keys, arrays):
            wl[li][n] = t.reshape(-1, d)

    jobs, srcs, out_shape = gather_of(first_keys)
    for extra in (meta_tokens, conv_dw_w):
        jobs.append(_Job(len(srcs), len(out_shape), False))
        srcs.append(extra)
        out_shape.append(SDS((N_DEV,) + extra.shape, F32))
    gathered = _exchange(jobs, srcs, out_shape, "gather_first")
    wl = [dict() for _ in range(depth)]
    receive(first_keys, gathered)
    meta_full = _from_cols(gathered[-2])
    taps = jnp.transpose(gathered[-1], (1, 2, 0, 3)).reshape(depth, CONV_K, cc)
    taps = jnp.pad(taps, ((0, 0), (0, CONV_PAD - CONV_K), (0, 0)))
    tri_fwd, tri_after, tri_before = _tri_consts()

    h = jnp.concatenate([meta_full, x[0], jnp.zeros((lp - lr, d), F32)], axis=0)
    saved = []
    for i in range(depth):
        p = wl[i]
        sv = dict(h_in=h)
        qkv, hn = _rms_mm(h, mix_norm_g[i:i + 1], p["w_in"], 0, 3 * sb, BF16, f"proj_qkv_{i}", True)
        keys = [("w_out", 0)] if i == 0 else []
        cacg, *arrived = _rms_mm(h, mix_norm_g[i:i + 1], p["w_in"], 3 * sb, 2 * cc, F32, f"proj_conv_{i}", False,
                                 riding(keys))
        receive(keys, arrived)
        keys = [("w_gate", i), ("w_up", i)]
        attn, rsave, *arrived = _attn_fwd(qkv, tri_fwd, f"attn_fwd_{i}", riding(keys))
        receive(keys, arrived)
        keys = [("w_down", 0)] if i == 0 else []
        yc, *arrived = _conv_fwd_dw(cacg, taps[i], conv_dw_b[i:i + 1], f"conv_fwd_dw_{i}", riding(keys))
        receive(keys, arrived)
        conv = _conv_fwd_ln(yc, conv_ln_g[i:i + 1], conv_ln_b[i:i + 1], f"conv_fwd_ln_{i}")
        h = _mix_out(attn, conv, p["w_out"], h, f"mix_out_{i}")
        sv.update(qkv=qkv, hn=hn, cacg=cacg, rsave=rsave, yc=yc, attn=attn, conv=conv, h_mid=h)
        keys = [("w_in", i + 1), ("w_out", i + 1), ("w_down", i + 1)] if i + 1 < depth else []
        h, hn2, act, *arrived = _ffn_fwd(h, ffn_norm_g[i:i + 1], p["w_gate"], p["w_up"], p["w_down"], f"ffn_fwd_{i}",
                                         riding(keys))
        receive(keys, arrived)
        sv.update(hn2=hn2, act=act)
        saved.append(sv)

    tpad = jnp.pad(loss_target[0], ((N_META, lp - lr), (0, 0)))
    dh, dg_final, loss_part = _loss_head(h, tpad, final_norm_g.reshape(1, d), seq, "loss_head")
    loss = lax.psum(loss_part[0, 0], MESH_AXES)

    parts = {}

    def sending(items):
        srcs = [t.reshape(N_DEV, t.shape[0] // N_DEV, d) for _, t in items]
        return _Rider([_Job(j, j, True) for j in range(len(items))], srcs, [SDS(t.shape, BF16) for t in srcs])

    def arrive(items, arrays):
        parts.update({key: t for (key, _), t in zip(items, arrays)})

    grads = [None] * depth
    from_above = []
    for i in reversed(range(depth)):
        p, sv = wl[i], saved[i]
        d_g, d_u, *got = _ffn_bwd_act(dh, sv["hn2"], p["w_gate"], p["w_up"], p["w_down"], f"ffn_bwd_act_{i}",
                                      sending(from_above))
        arrive(from_above, got)
        gw_down = _mm_tn(sv["act"], dh, f"grad_w_down_{i}")
        gw_gate = _mm_tn(d_g, sv["hn2"], f"grad_w_gate_{i}")
        gw_up = _mm_tn(d_u, sv["hn2"], f"grad_w_up_{i}")
        items = [(("w_down", i), gw_down)]
        dh, dg_ffn, *got = _ffn_bwd_in(d_g, d_u, p["w_gate"], p["w_up"], sv["h_mid"], ffn_norm_g[i:i + 1], dh,
                                       f"ffn_bwd_in_{i}", sending(items))
        arrive(items, got)
        gw_out = jnp.concatenate([_mm_tn(sv["attn"], dh, f"grad_w_out_attn_{i}"),
                                  _mm_tn(sv["conv"], dh, f"grad_w_out_conv_{i}")], axis=0)
        d_attn = _mm_nt(dh, p["w_out"], 0, sb, BF16, f"mix_bwd_attn_{i}")
        d_conv = _mm_nt(dh, p["w_out"], sb, cc, F32, f"mix_bwd_conv_{i}")
        items = [(("w_gate", i), gw_gate), (("w_up", i), gw_up)]
        dqkv, *got = _attn_bwd(sv["qkv"], d_attn, sv["rsave"], tri_after, tri_before, f"attn_bwd_{i}", sending(items))
        arrive(items, got)
        dyc, dg_ln, db_ln = _conv_bwd_ln(sv["yc"], d_conv, conv_ln_g[i:i + 1], conv_ln_b[i:i + 1], f"conv_bwd_ln_{i}")
        dcc, g_taps_i, db_conv = _conv_bwd_dw(dyc, sv["cacg"], taps[i], f"conv_bwd_dw_{i}")
        items = [(("w_out", i), gw_out)] if i == 0 else []
        gw_in, *got = _grad_w_in_t(dqkv, dcc, sv["hn"], f"grad_w_in_{i}", sending(items))
        arrive(items, got)
        from_above = [(("w_in", i), gw_in)] + ([(("w_out", i), gw_out)] if i > 0 else [])
        items = from_above if i == 0 else []
        dh, dg_mix, *got = _mix_bwd_in(dqkv, dcc, p["w_in"], sv["h_in"], mix_norm_g[i:i + 1], dh, f"mix_bwd_in_{i}",
                                       sending(items))
        arrive(items, got)
        grads[i] = dict(taps=g_taps_i, dg_mix=dg_mix, dg_ffn=dg_ffn, dg_ln=dg_ln, db_ln=db_ln, db_conv=db_conv)
    grad_x = dh[N_META:lr][None]

    per = d // depth
    assert depth * cc <= d and d % depth == 0 and per % 128 == 0
    placed, cuts, row = [], [], 0
    for key in ("dg_mix", "dg_ffn"):
        placed += [(grads[i][key], row + i, 0, True) for i in range(depth)]
        cuts.append(((depth, d), [(slice(0, depth), row, depth, 0, d)]))
        row += depth
    placed.append((dg_final, row, 0, True))
    cuts.append(((1, d), [(slice(0, 1), row, 1, 0, d)]))
    row += 1
    for key in ("db_conv", "dg_ln", "db_ln"):
        placed += [(grads[i][key], row, i * per, True) for i in range(depth)]
        cuts.append(((depth, cc), [(slice(i, i + 1), row, 1, i * per, cc) for i in range(depth)]))
        row += 1
    placed += [(grads[i]["taps"], row, i * per, False) for i in range(depth)]
    cuts.append(((depth, CONV_PAD, cc), [(i, row, CONV_PAD, i * per, cc) for i in range(depth)]))
    row += CONV_PAD
    placed.append((dh[:N_META], row, 0, False))
    cuts.append(((N_META, d), [(slice(0, N_META), row, N_META, 0, d)]))
    row += N_META
    g_mix, g_ffn, g_final, g_cb, g_lg, g_lb, g_taps_full, g_meta_full = _all_reduce_small(
        placed, cuts, -(-row // 8) * 8, d, "reduce_small")
    csh = cc // N_DEV
    g_taps_own = lax.dynamic_slice_in_dim(g_taps_full[:, :CONV_K], me * csh, csh, axis=2)
    msh = d // N_DEV
    g_meta_own = lax.dynamic_slice_in_dim(g_meta_full, me * msh, msh, axis=1)

    row1 = lambda t: t.reshape(1, d)
    small_g = [g_meta_own, g_mix, g_taps_own, g_cb, g_lg, g_lb, g_ffn, g_final]
    small_w = [meta_tokens, mix_norm_g, conv_dw_w, conv_dw_b, conv_ln_g, conv_ln_b, ffn_norm_g, row1(final_norm_g)]
    small_m = [m_meta_tokens, m_mix_norm_g, m_conv_dw_w, m_conv_dw_b, m_conv_ln_g, m_conv_ln_b, m_ffn_norm_g,
               row1(m_final_norm_g)]
    small_v = [v_meta_tokens, v_mix_norm_g, v_conv_dw_w, v_conv_dw_b, v_conv_ln_g, v_conv_ln_b, v_ffn_norm_g,
               row1(v_final_norm_g)]
    s_delta, s_m, s_v = _adamw_small(small_g, small_w, small_m, small_v, "adamw_small")
    unrow = lambda ts: list(ts[:-1]) + [ts[-1].reshape(d)]
    small_g, s_delta, s_m, s_v = unrow(small_g), unrow(s_delta), unrow(s_m), unrow(s_v)

    big = []
    for n, w, m, v in zip(big_names, (w_in, w_out, w_gate, w_up, w_down), (m_w_in, m_w_out, m_w_gate, m_w_up, m_w_down),
                          (v_w_in, v_w_out, v_w_gate, v_w_up, v_w_down)):
        big.append(_adamw_shard([parts[(n, i)] for i in range(depth)], w, m, v, transposed[n], f"adamw_{n}"))
    b_in, b_out, b_gate, b_up, b_down = big

    def ordered(k, smalls):
        s_meta, s_mix, s_taps, s_cb, s_lg, s_lb, s_ffn, s_final = smalls
        return [s_meta, s_mix, b_in[k], s_taps, s_cb, s_lg, s_lb, b_out[k], s_ffn, b_gate[k], b_up[k], b_down[k], s_final]

    return (loss, grad_x, *ordered(0, small_g), *ordered(1, s_delta), *ordered(2, s_m), *ordered(3, s_v))
```

```python
import functools
import math

import jax
import jax.numpy as jnp
from jax import lax
from jax.experimental import pallas as pl
from jax.experimental.pallas import tpu as pltpu

F32 = jnp.float32
BF16 = jnp.bfloat16
SDS = jax.ShapeDtypeStruct

N_META = 16
N_HEADS = 8
HEAD_DIM = 64
CONV_K = 31
CONV_PAD = 32
ABLK = 128
ATT_GROUP = 4
ATT_SPLIT = 2
ATT_TOP = 2
EXP_ZERO_AT = 104.0
GONE = 1e30
MASKED = -1e30
FF_CHUNK = 256
FF_BLOCK = 1408
FF_ROWS = 528
EPS = 1e-6
N_DEV = 8
MESH_AXES = ("x", "y", "c")
ADAM_LR = 0.001
ADAM_B1 = 0.9
ADAM_B2 = 0.999
ADAM_EPS = 1e-08
ADAM_WD = 0.01
ADAM_STEP = 10
MIB = 1 << 20
VMEM_LIMIT_MIB = 48
MM_ROWS = 1056
EW_ROWS = 528


def _cp():
    return pltpu.CompilerParams(vmem_limit_bytes=VMEM_LIMIT_MIB * MIB)


def _tile(n, cap, mult):
    best = None
    for t in range(mult, min(n, cap) + 1, mult):
        if n % t == 0:
            best = t
    assert best is not None, (n, cap, mult)
    return best


def _sum8(v):
    r, c = v.shape
    return v.reshape(r // 8, 8, c).sum(axis=0)


def _sigmoid(x):
    return 1.0 / (1.0 + jnp.exp(-x))


def _ff_block(f):
    fb = _tile(f, FF_BLOCK, 128)
    return fb, [(s0, min(FF_CHUNK, fb - s0)) for s0 in range(0, fb, FF_CHUNK)]


NT = (((1,), (1,)), ((), ()))
TN = (((0,), (0,)), ((), ()))


def _rms_mm(h, g, wt, w_row0, n_cols, out_dtype, name, hn_out, rider=None):
    lp, d = h.shape
    tm = _tile(lp, MM_ROWS, 16)
    tn = _tile(math.gcd(n_cols, w_row0), 768, 128)
    off = w_row0 // tn
    ni, nj = lp // tm, n_cols // tn
    rider = rider or _Rider()

    def body(h_ref, g_ref, w_ref, *rest):
        outs, (hn_s,) = rider.split(rest, 2 if hn_out else 1, 1)
        o_ref = outs[0]
        hn_ref = outs[1] if hn_out else None
        rider.start((pl.program_id(0) == 0) & (pl.program_id(1) == 0))
        if ni > 1:
            rider.forward((pl.program_id(0) == ni - 1) & (pl.program_id(1) == 0))

        @pl.when(pl.program_id(1) == 0)
        def _():
            x = h_ref[...]
            r = lax.rsqrt(jnp.mean(x * x, axis=-1, keepdims=True) + EPS)
            hn = ((x * r) * g_ref[...]).astype(BF16)
            hn_s[...] = hn
            if hn_out:
                hn_ref[...] = hn

        o_ref[...] = lax.dot_general(hn_s[...], w_ref[...], NT, preferred_element_type=F32).astype(out_dtype)
        rider.wait((pl.program_id(0) == ni - 1) & (pl.program_id(1) == nj - 1))

    out_shape = [SDS((lp, n_cols), out_dtype)]
    out_specs = [pl.BlockSpec((tm, tn), lambda i, j: (i, j))]
    if hn_out:
        out_shape.append(SDS((lp, d), BF16))
        out_specs.append(pl.BlockSpec((tm, d), lambda i, j: (i, 0)))
    return pl.pallas_call(
        body, out_shape=out_shape + rider.out_shape, grid=(ni, nj),
        in_specs=[pl.BlockSpec((tm, d), lambda i, j: (i, 0)),
                  pl.BlockSpec((1, d), lambda i, j: (0, 0)),
                  pl.BlockSpec((tn, d), lambda i, j: (j + off, 0))] + rider.in_specs(),
        out_specs=out_specs + rider.out_specs(), scratch_shapes=[pltpu.VMEM((tm, d), BF16)] + rider.scratch(),
        name=name, compiler_params=_cp())(h, g, wt, *rider.srcs)


def _mm_nt(a, b, b_row0, n_out, out_dtype, name):
    m, k = a.shape
    tm = _tile(m, MM_ROWS, 16)
    tn = _tile(math.gcd(n_out, b_row0), 512, 128)
    off = b_row0 // tn

    def body(a_ref, b_ref, o_ref):
        o_ref[...] = lax.dot_general(a_ref[...].astype(BF16), b_ref[...], NT,
                                     preferred_element_type=F32).astype(out_dtype)

    return pl.pallas_call(
        body, out_shape=SDS((m, n_out), out_dtype), grid=(m // tm, n_out // tn),
        in_specs=[pl.BlockSpec((tm, k), lambda i, j: (i, 0)), pl.BlockSpec((tn, k), lambda i, j: (j + off, 0))],
        out_specs=pl.BlockSpec((tm, tn), lambda i, j: (i, j)), name=name, compiler_params=_cp())(a, b)


def _mix_out(attn, conv, w, res, name):
    m, ka = attn.shape
    kc = conv.shape[1]
    n = w.shape[1]
    assert ka == kc
    tm = _tile(m, MM_ROWS, 16)
    tn = _tile(n, 512, 128)

    def body(a_ref, c_ref, wa_ref, wc_ref, r_ref, o_ref):
        o_ref[...] = (r_ref[...] + jnp.dot(a_ref[...], wa_ref[...], preferred_element_type=F32)
                      + jnp.dot(c_ref[...], wc_ref[...], preferred_element_type=F32))

    return pl.pallas_call(
        body, out_shape=SDS((m, n), F32), grid=(m // tm, n // tn),
        in_specs=[pl.BlockSpec((tm, ka), lambda i, j: (i, 0)), pl.BlockSpec((tm, kc), lambda i, j: (i, 0)),
                  pl.BlockSpec((ka, tn), lambda i, j: (0, j)), pl.BlockSpec((kc, tn), lambda i, j: (1, j)),
                  pl.BlockSpec((tm, tn), lambda i, j: (i, j))],
        out_specs=pl.BlockSpec((tm, tn), lambda i, j: (i, j)), name=name, compiler_params=_cp())(attn, conv, w, w, res)


def _mm_tn(a, b, name):
    l, m = a.shape
    n = b.shape[1]
    tm = _tile(m, 1408, 128)
    tn = _tile(n, 1024, 128)
    tl = _tile(l, 1408, 128)
    nl = l // tl

    def body(a_ref, b_ref, o_ref, acc):
        @pl.when(pl.program_id(2) == 0)
        def _():
            acc[...] = jnp.zeros_like(acc)

        acc[...] += lax.dot_general(a_ref[...].astype(BF16), b_ref[...].astype(BF16), TN, preferred_element_type=F32)

        @pl.when(pl.program_id(2) == nl - 1)
        def _():
            o_ref[...] = acc[...].astype(BF16)

    return pl.pallas_call(
        body, out_shape=SDS((m, n), BF16), grid=(m // tm, n // tn, nl),
        in_specs=[pl.BlockSpec((tl, tm), lambda i, j, s: (s, i)),
                  pl.BlockSpec((tl, tn), lambda i, j, s: (s, j))],
        out_specs=pl.BlockSpec((tm, tn), lambda i, j, s: (i, j)),
        scratch_shapes=[pltpu.VMEM((tm, tn), F32)], name=name, compiler_params=_cp())(a, b)


def _grad_w_in_t(dqkv, dcc, hn, name, rider=None):
    nq, l, w = dqkv.shape
    nc = dcc.shape[0]
    d = hn.shape[1]
    assert dcc.shape[2] == w
    tn = _tile(d, 1024, 128)
    tl = _tile(l, 1408, 128)
    nl = l // tl
    nj = d // tn
    rider = rider or _Rider()

    def body(q_ref, c_ref, b_ref, *rest):
        (o_ref,), (acc,) = rider.split(rest, 1, 1)
        p, jj, s = pl.program_id(0), pl.program_id(1), pl.program_id(2)
        rider.start((p == 0) & (jj == 0) & (s == 0))

        @pl.when(s == 0)
        def _():
            acc[...] = jnp.zeros_like(acc)

        @pl.when(p < nq)
        def _():
            acc[...] += lax.dot_general(q_ref[...], b_ref[...], TN, preferred_element_type=F32)

        @pl.when(p >= nq)
        def _():
            acc[...] += lax.dot_general(c_ref[...], b_ref[...], TN, preferred_element_type=F32)

        @pl.when(s == nl - 1)
        def _():
            o_ref[...] = acc[...].astype(BF16)

        rider.wait((p == nq + nc - 1) & (jj == nj - 1) & (s == nl - 1))

    out, *arrived = pl.pallas_call(
        body, out_shape=[SDS((nq + nc, w, d), BF16)] + rider.out_shape, grid=(nq + nc, nj, nl),
        in_specs=[pl.BlockSpec((None, tl, w), lambda p, j, s: (jnp.minimum(p, nq - 1), s, 0)),
                  pl.BlockSpec((None, tl, w), lambda p, j, s: (jnp.maximum(p - nq, 0), s, 0)),
                  pl.BlockSpec((tl, tn), lambda p, j, s: (s, j))] + rider.in_specs(),
        out_specs=[pl.BlockSpec((None, w, tn), lambda p, j, s: (p, 0, j))] + rider.out_specs(),
        scratch_shapes=[pltpu.VMEM((w, tn), F32)] + rider.scratch(), name=name,
        compiler_params=_cp())(dqkv, dcc, hn, *rider.srcs)
    return [out.reshape((nq + nc) * w, d)] + arrived


def _rms_bwd_tail(acc, h_ref, g_ref, dres_ref, o_ref, dg_ref):
    x = h_ref[...]
    r = lax.rsqrt(jnp.mean(x * x, axis=-1, keepdims=True) + EPS)
    xr = x * r
    dyv = acc[...]
    gy = dyv * g_ref[...]
    o_ref[...] = dres_ref[...] + r * (gy - xr * jnp.mean(gy * xr, axis=-1, keepdims=True))
    dg_ref[...] += _sum8(dyv * xr)


def _mix_bwd_in(dqkv, dcc, wt, h, g, dres, name, rider=None):
    nq, lp, w = dqkv.shape
    nc = dcc.shape[0]
    d = h.shape[1]
    tm = _tile(lp, FF_ROWS, 16)
    ni = lp // tm
    rider = rider or _Rider()

    def body(q_ref, c_ref, w_ref, h_ref, g_ref, dres_ref, *rest):
        (o_ref, dg_ref), (acc,) = rider.split(rest, 2, 1)
        i = pl.program_id(0)
        rider.start(i == 0)

        @pl.when(i == 0)
        def _():
            dg_ref[...] = jnp.zeros_like(dg_ref)

        pieces = [q_ref[p] for p in range(nq)] + [c_ref[p] for p in range(nc)]
        total = None
        for p, piece in enumerate(pieces):
            part = jnp.dot(piece, w_ref[p * w:(p + 1) * w, :], preferred_element_type=F32)
            total = part if total is None else total + part
        acc[...] = total
        _rms_bwd_tail(acc, h_ref, g_ref, dres_ref, o_ref, dg_ref)
        rider.wait(i == ni - 1)

    return pl.pallas_call(
        body, out_shape=[SDS((lp, d), F32), SDS((8, d), F32)] + rider.out_shape, grid=(ni,),
        in_specs=[pl.BlockSpec((nq, tm, w), lambda i: (0, i, 0)),
                  pl.BlockSpec((nc, tm, w), lambda i: (0, i, 0)),
                  pl.BlockSpec(((nq + nc) * w, d), lambda i: (0, 0)),
                  pl.BlockSpec((tm, d), lambda i: (i, 0)),
                  pl.BlockSpec((1, d), lambda i: (0, 0)),
                  pl.BlockSpec((tm, d), lambda i: (i, 0))] + rider.in_specs(),
        out_specs=[pl.BlockSpec((tm, d), lambda i: (i, 0)), pl.BlockSpec((8, d), lambda i: (0, 0))]
        + rider.out_specs(),
        scratch_shapes=[pltpu.VMEM((tm, d), F32)] + rider.scratch(), name=name,
        compiler_params=_cp())(dqkv, dcc, wt, h, g, dres, *rider.srcs)


def _ffn_bwd_in(d_g, d_u, wg_t, wu_t, h, g, dres, name, rider=None):
    lp, f = d_g.shape
    d = h.shape[1]
    tm = _tile(lp, FF_ROWS, 16)
    tk = _tile(f, FF_BLOCK, 128)
    nk = f // tk
    ni = lp // tm
    rider = rider or _Rider()

    def body(dg_in, du_in, wg_ref, wu_ref, h_ref, g_ref, dres_ref, *rest):
        (o_ref, dg_ref), (acc,) = rider.split(rest, 2, 1)
        i, kk = pl.program_id(0), pl.program_id(1)
        rider.start((i == 0) & (kk == 0))

        @pl.when(kk == 0)
        def _():
            acc[...] = jnp.zeros_like(acc)

        @pl.when((kk == 0) & (i == 0))
        def _():
            dg_ref[...] = jnp.zeros_like(dg_ref)

        acc[...] += (jnp.dot(dg_in[...], wg_ref[...], preferred_element_type=F32)
                     + jnp.dot(du_in[...], wu_ref[...], preferred_element_type=F32))

        @pl.when(kk == nk - 1)
        def _():
            _rms_bwd_tail(acc, h_ref, g_ref, dres_ref, o_ref, dg_ref)

        rider.wait((i == ni - 1) & (kk == nk - 1))

    return pl.pallas_call(
        body, out_shape=[SDS((lp, d), F32), SDS((8, d), F32)] + rider.out_shape, grid=(ni, nk),
        in_specs=[pl.BlockSpec((tm, tk), lambda i, j: (i, j)), pl.BlockSpec((tm, tk), lambda i, j: (i, j)),
                  pl.BlockSpec((tk, d), lambda i, j: (j, 0)), pl.BlockSpec((tk, d), lambda i, j: (j, 0)),
                  pl.BlockSpec((tm, d), lambda i, j: (i, 0)),
                  pl.BlockSpec((1, d), lambda i, j: (0, 0)),
                  pl.BlockSpec((tm, d), lambda i, j: (i, 0))] + rider.in_specs(),
        out_specs=[pl.BlockSpec((tm, d), lambda i, j: (i, 0)), pl.BlockSpec((8, d), lambda i, j: (0, 0))]
        + rider.out_specs(),
        scratch_shapes=[pltpu.VMEM((tm, d), F32)] + rider.scratch(), name=name,
        compiler_params=_cp())(d_g, d_u, wg_t, wu_t, h, g, dres, *rider.srcs)


def _ffn_fwd(h, g, wg_t, wu_t, w_d, name, rider=None):
    lp, d = h.shape
    f = w_d.shape[0]
    tm = _tile(lp, FF_ROWS, 16)
    fb, subs = _ff_block(f)
    nc = f // fb
    ni = lp // tm
    rider = rider or _Rider()

    def body(h_ref, g_ref, wg_ref, wu_ref, wd_ref, *rest):
        (o_ref, hn_ref, act_ref), (hn_s, acc) = rider.split(rest, 3, 2)
        c = pl.program_id(1)
        rider.start((pl.program_id(0) == 0) & (c == 0))
        if ni > 1:
            rider.forward((pl.program_id(0) == ni - 1) & (c == 0))

        @pl.when(c == 0)
        def _():
            x = h_ref[...]
            r = lax.rsqrt(jnp.mean(x * x, axis=-1, keepdims=True) + EPS)
            hn = ((x * r) * g_ref[...]).astype(BF16)
            hn_s[...] = hn
            hn_ref[...] = hn
            acc[...] = jnp.zeros_like(acc)

        total = None
        for s0, sw in subs:
            gg = lax.dot_general(hn_s[...], wg_ref[s0:s0 + sw, :], NT, preferred_element_type=F32)
            uu = lax.dot_general(hn_s[...], wu_ref[s0:s0 + sw, :], NT, preferred_element_type=F32)
            act = ((gg * _sigmoid(gg)) * uu).astype(BF16)
            act_ref[:, s0:s0 + sw] = act
            part = jnp.dot(act, wd_ref[s0:s0 + sw, :], preferred_element_type=F32)
            total = part if total is None else total + part
        acc[...] += total

        @pl.when(c == nc - 1)
        def _():
            o_ref[...] = h_ref[...] + acc[...]

        rider.wait((pl.program_id(0) == ni - 1) & (c == nc - 1))

    chunk = pl.BlockSpec((fb, d), lambda i, j: (j, 0))
    return pl.pallas_call(
        body, out_shape=[SDS((lp, d), F32), SDS((lp, d), BF16), SDS((lp, f), BF16)] + rider.out_shape, grid=(ni, nc),
        in_specs=[pl.BlockSpec((tm, d), lambda i, j: (i, 0)), pl.BlockSpec((1, d), lambda i, j: (0, 0)),
                  chunk, chunk, chunk] + rider.in_specs(),
        out_specs=[pl.BlockSpec((tm, d), lambda i, j: (i, 0)),
                   pl.BlockSpec((tm, d), lambda i, j: (i, 0)),
                   pl.BlockSpec((tm, fb), lambda i, j: (i, j))] + rider.out_specs(),
        scratch_shapes=[pltpu.VMEM((tm, d), BF16), pltpu.VMEM((tm, d), F32)] + rider.scratch(),
        name=name, compiler_params=_cp())(h, g, wg_t, wu_t, w_d, *rider.srcs)


def _ffn_bwd_act(dh, hn, wg_t, wu_t, w_d, name, rider=None):
    lp, d = dh.shape
    f = w_d.shape[0]
    tm = _tile(lp, FF_ROWS, 16)
    fb, subs = _ff_block(f)
    nc = f // fb
    ni = lp // tm
    rider = rider or _Rider()

    def body(dh_ref, hn_ref, wg_ref, wu_ref, wd_ref, *rest):
        (dg_out, du_out), (dh_s,) = rider.split(rest, 2, 1)
        rider.start((pl.program_id(0) == 0) & (pl.program_id(1) == 0))

        @pl.when(pl.program_id(1) == 0)
        def _():
            dh_s[...] = dh_ref[...].astype(BF16)

        for s0, sw in subs:
            gg = lax.dot_general(hn_ref[...], wg_ref[s0:s0 + sw, :], NT, preferred_element_type=F32)
            uu = lax.dot_general(hn_ref[...], wu_ref[s0:s0 + sw, :], NT, preferred_element_type=F32)
            dact = lax.dot_general(dh_s[...], wd_ref[s0:s0 + sw, :], NT, preferred_element_type=F32)
            s = _sigmoid(gg)
            dg_out[:, s0:s0 + sw] = (dact * uu * (s * (1.0 + gg * (1.0 - s)))).astype(BF16)
            du_out[:, s0:s0 + sw] = (dact * (gg * s)).astype(BF16)
        rider.wait((pl.program_id(0) == ni - 1) & (pl.program_id(1) == nc - 1))

    chunk = pl.BlockSpec((fb, d), lambda i, j: (j, 0))
    out = pl.BlockSpec((tm, fb), lambda i, j: (i, j))
    return pl.pallas_call(
        body, out_shape=[SDS((lp, f), BF16), SDS((lp, f), BF16)] + rider.out_shape, grid=(ni, nc),
        in_specs=[pl.BlockSpec((tm, d), lambda i, j: (i, 0)), pl.BlockSpec((tm, d), lambda i, j: (i, 0)),
                  chunk, chunk, chunk] + rider.in_specs(),
        out_specs=[out, out] + rider.out_specs(), scratch_shapes=[pltpu.VMEM((tm, d), BF16)] + rider.scratch(),
        name=name, compiler_params=_cp())(dh, hn, wg_t, wu_t, w_d, *rider.srcs)


def _loss_head(h, tpad, g, n_real, name):
    lp, d = h.shape
    tm = _tile(lp, EW_ROWS, 16)

    def body(h_ref, t_ref, g_ref, dh_ref, dg_ref, loss_ref):
        i = pl.program_id(0)

        @pl.when(i == 0)
        def _():
            dg_ref[...] = jnp.zeros_like(dg_ref)
            loss_ref[...] = jnp.zeros_like(loss_ref)

        x = h_ref[...]
        r = lax.rsqrt(jnp.mean(x * x, axis=-1, keepdims=True) + EPS)
        xr = x * r
        y = xr * g_ref[...]
        row = i * tm + lax.broadcasted_iota(jnp.int32, (tm, d), 0)
        valid = (row >= N_META) & (row < N_META + n_real)
        diff = jnp.where(valid, y - t_ref[...], 0.0)
        loss_ref[...] += jnp.sum(diff * diff) * (0.5 / d)
        dy = diff * (1.0 / d)
        gy = dy * g_ref[...]
        dh_ref[...] = r * (gy - xr * jnp.mean(gy * xr, axis=-1, keepdims=True))
        dg_ref[...] += _sum8(dy * xr)

    return pl.pallas_call(
        body, out_shape=[SDS((lp, d), F32), SDS((8, d), F32), SDS((8, 128), F32)], grid=(lp // tm,),
        in_specs=[pl.BlockSpec((tm, d), lambda i: (i, 0)),
                  pl.BlockSpec((tm, d), lambda i: (i, 0)),
                  pl.BlockSpec((1, d), lambda i: (0, 0))],
        out_specs=[pl.BlockSpec((tm, d), lambda i: (i, 0)),
                   pl.BlockSpec((8, d), lambda i: (0, 0)),
                   pl.BlockSpec((8, 128), lambda i: (0, 0))],
        name=name, compiler_params=_cp())(h, tpad, g)


def _tri_consts():
    j = lax.broadcasted_iota(jnp.int32, (ABLK, ABLK), 0)
    s = lax.broadcasted_iota(jnp.int32, (ABLK, ABLK), 1)
    after = (j >= s).astype(BF16)
    before = (j < s).astype(BF16)
    ones = jnp.ones((ABLK, ABLK), BF16)
    two = lambda t: jnp.concatenate([t, t], axis=0)
    return (two(jnp.concatenate([after, ones], axis=1)),
            two(after),
            two(jnp.concatenate([before, ones], axis=1)))


def _softplus(z):
    neg_abs = lax.bitcast_convert_type(lax.bitcast_convert_type(z, jnp.uint32) | jnp.uint32(0x80000000), F32)
    return jnp.log(1.0 + jnp.exp(neg_abs)) + jnp.maximum(z, 0.0)


def _split_hi_lo(m):
    hi = m.astype(BF16)
    lo = (m - hi.astype(F32)).astype(BF16)
    return jnp.concatenate([hi, lo], axis=1)


def _head_halves(t2, in_a):
    zero = jnp.zeros_like(t2)
    return jnp.concatenate([jnp.where(in_a, t2, zero), jnp.where(in_a, zero, t2)], axis=0)


def _stack_blocks(t, nblk, in_a):
    return jnp.concatenate([_head_halves(t[u * ABLK:(u + 1) * ABLK], in_a) for u in range(nblk)], axis=0)


def _attn_scale():
    scale = 1.0 / math.sqrt(HEAD_DIM)
    assert math.frexp(scale)[0] == 0.5, "a power of two, so that scaling q in bf16 is exact"
    return scale


def _pow2_below(n):
    assert n & (n - 1) == 0
    return [p for p in (64, 32, 16, 8, 4, 2, 1) if p < n]


class _Rider:
    def __init__(self, jobs=(), srcs=(), out_shape=()):
        self.jobs, self.srcs, self.out_shape = list(jobs), list(srcs), list(out_shape)
        self.any = [pl.BlockSpec(memory_space=pl.ANY)]

    def split(self, rest, n_out, n_scratch):
        ni, no = len(self.srcs), len(self.out_shape)
        self.ins, outs = rest[:ni], rest[ni:ni + n_out]
        self.outs = rest[ni + n_out:ni + n_out + no]
        scratch = rest[ni + n_out + no:ni + n_out + no + n_scratch]
        self.sems = rest[ni + n_out + no + n_scratch:]
        return outs, scratch

    def start(self, first):
        self.forwarded = False
        if self.jobs:
            @pl.when(first)
            def _():
                _exchange_start(self.jobs, self.ins, self.outs, *self.sems)

    def forward(self, late):
        self.forwarded = True
        if self.jobs:
            @pl.when(late)
            def _():
                _exchange_forward(self.jobs, self.ins, self.outs, *self.sems)

    def wait(self, last):
        if self.jobs:
            @pl.when(last)
            def _():
                if not self.forwarded:
                    _exchange_forward(self.jobs, self.ins, self.outs, *self.sems)
                _exchange_wait(self.jobs, self.ins, self.outs, *self.sems)

    def in_specs(self):
        return self.any * len(self.srcs)

    def out_specs(self):
        return self.any * len(self.out_shape)

    def scratch(self):
        return _exchange_sems(len(self.jobs)) if self.jobs else []


def _attn_fwd(qkv, tri_fwd, name, rider=None):
    lp = qkv.shape[0]
    n_pairs = (N_HEADS * HEAD_DIM) // 128
    nb = lp // ABLK
    assert nb <= 128 and 2 * HEAD_DIM == 128
    scale = _attn_scale()
    nt = (((1,), (1,)), ((), ()))
    rider = rider or _Rider()

    def body(q_ref, k_ref, v_ref, tri_ref, *rest):
        (o_ref, rs_ref), (r_s, acc_s, rs_s) = rider.split(rest, 2, 3)
        rider.start(pl.program_id(0) == 0)
        if n_pairs > 1:
            rider.forward(pl.program_id(0) == n_pairs - 1)
        lane = lax.broadcasted_iota(jnp.int32, (ABLK, 128), 1)
        row = lax.broadcasted_iota(jnp.int32, (ABLK, 128), 0)
        in_a = lane < HEAD_DIM
        causal = lane < row

        def qblock(i, carry):
            q0 = pl.multiple_of(i * ABLK, ABLK)
            q2 = q_ref[pl.ds(q0, ABLK), :] * scale
            r_s[...] = jnp.zeros_like(r_s)
            acc_s[...] = jnp.zeros_like(acc_s)
            rs_s[...] = jnp.full_like(rs_s, GONE)

            def live():
                least = jnp.min(jnp.minimum(r_s[0], r_s[1]), axis=0, keepdims=True)
                return (least[0, 0] < EXP_ZERO_AT).astype(jnp.int32)

            def step(kb0, nblk, diag):
                k0 = pl.multiple_of(kb0 * ABLK, ABLK)
                kbd = _stack_blocks(k_ref[pl.ds(k0, nblk * ABLK), :], nblk, in_a)
                vbd = _stack_blocks(v_ref[pl.ds(k0, nblk * ABLK), :], nblk, in_a)
                z = lax.dot_general(q2, kbd, nt, preferred_element_type=F32)
                ncol = 2 * nblk
                zt = [z[:, c * 128:(c + 1) * 128] for c in range(ncol)]
                if diag:
                    zt = [jnp.where(causal, t, MASKED) if c >= ncol - 2 else t for c, t in enumerate(zt)]
                bounds = [ncol * j // ATT_SPLIT for j in range(ATT_SPLIT + 1)]
                ce = [None] * ncol
                for c0, c1 in reversed([b for b in zip(bounds[:-1], bounds[1:]) if b[0] < b[1]]):
                    parts = [_split_hi_lo(_softplus(zt[c])) for c in range(c0, c1)]
                    got = jnp.dot(jnp.concatenate(parts, axis=0), tri_ref[...], preferred_element_type=F32)
                    for c in range(c0, c1):
                        ce[c] = got[(c - c0) * 128:(c - c0 + 1) * 128]
                rr = [r_s[0], r_s[1]]
                rsv = [rs_s[:, :128], rs_s[:, 128:]]
                ws = [None] * ncol
                for u in reversed(range(nblk)):
                    for hh in range(2):
                        c = 2 * u + hh
                        ws[c] = jnp.exp(zt[c] - ce[c][:, :128] - rr[hh]).astype(BF16)
                        rsv[hh] = jnp.where(lane == kb0 + u, rr[hh], rsv[hh])
                        rr[hh] = rr[hh] + ce[c][:, 128:]
                acc_s[...] += jnp.dot(jnp.concatenate(ws, axis=1), vbd, preferred_element_type=F32)
                r_s[0] = rr[0]
                r_s[1] = rr[1]
                rs_s[:, :128] = rsv[0]
                rs_s[:, 128:] = rsv[1]

            n_top = jnp.minimum(i, ATT_TOP)
            for t in range(ATT_TOP + 1):
                @pl.when(n_top == t)
                def _():
                    step(i - t, t + 1, True)

            i_low = i - n_top
            n_grp = i_low // ATT_GROUP

            def more(c):
                return (c[0] < n_grp) & (c[1] > 0)

            def inner(c):
                step(i_low - ATT_GROUP * (c[0] + 1), ATT_GROUP, False)
                return c[0] + 1, live()

            _, alive = lax.while_loop(more, inner, (jnp.int32(0), live()))
            rem = i_low - ATT_GROUP * n_grp
            for p in _pow2_below(ATT_GROUP):
                def last_steps(p=p):
                    step(rem & (p - 1), p, False)
                    return live()

                alive = lax.cond(((rem & p) != 0) & (alive > 0), last_steps, lambda alive=alive: alive)

            o_ref[pl.ds(q0, ABLK), :] = acc_s[...].astype(BF16)
            rs_ref[pl.ds(q0, ABLK), :] = rs_s[...]
            return carry

        lax.fori_loop(0, nb, qblock, 0)
        rider.wait(pl.program_id(0) == n_pairs - 1)

    col = lambda o: (lambda p: (0, p + o))
    return pl.pallas_call(
        body, out_shape=[SDS((lp, n_pairs * 128), BF16), SDS((lp, n_pairs * 256), F32)] + rider.out_shape,
        grid=(n_pairs,),
        in_specs=[pl.BlockSpec((lp, 128), col(0)), pl.BlockSpec((lp, 128), col(n_pairs)),
                  pl.BlockSpec((lp, 128), col(2 * n_pairs)), pl.BlockSpec((256, 256), lambda p: (0, 0))]
        + rider.in_specs(),
        out_specs=[pl.BlockSpec((lp, 128), col(0)), pl.BlockSpec((lp, 256), col(0))] + rider.out_specs(),
        scratch_shapes=[pltpu.VMEM((2, ABLK, 128), F32), pltpu.VMEM((ABLK, 128), F32), pltpu.VMEM((ABLK, 256), F32)]
        + rider.scratch(),
        name=name, compiler_params=_cp())(qkv, qkv, qkv, tri_fwd, *rider.srcs)


def _attn_bwd(qkv, d_out, rsave, tri_after, tri_before, name, rider=None):
    lp = qkv.shape[0]
    n_pairs = (N_HEADS * HEAD_DIM) // 128
    nb = lp // ABLK
    scale = _attn_scale()
    nt = (((1,), (1,)), ((), ()))
    tn = (((0,), (0,)), ((), ()))
    rider = rider or _Rider()

    def body(q_ref, k_ref, v_ref, do_ref, rs_ref, ta_ref, tb_ref, *rest):
        (o_ref,), (dk_s, dv_s, dq_s, pc_s) = rider.split(rest, 1, 4)
        rider.start(pl.program_id(0) == 0)
        lane = lax.broadcasted_iota(jnp.int32, (ABLK, 128), 1)
        row = lax.broadcasted_iota(jnp.int32, (ABLK, 128), 0)
        in_a = lane < HEAD_DIM
        causal = lane < row
        dk_s[...] = jnp.zeros_like(dk_s)
        dv_s[...] = jnp.zeros_like(dv_s)

        def qblock(i, carry):
            q0 = pl.multiple_of(i * ABLK, ABLK)
            q2 = q_ref[pl.ds(q0, ABLK), :] * scale
            do2 = do_ref[pl.ds(q0, ABLK), :]
            q_st = _head_halves(q2, in_a)
            do_st = _head_halves(do2, in_a)
            dq_s[...] = jnp.zeros_like(dq_s)
            pc_s[...] = jnp.zeros_like(pc_s)

            def step(kb0, nblk, diag):
                k0 = pl.multiple_of(kb0 * ABLK, ABLK)
                kbd = _stack_blocks(k_ref[pl.ds(k0, nblk * ABLK), :], nblk, in_a)
                vbd = _stack_blocks(v_ref[pl.ds(k0, nblk * ABLK), :], nblk, in_a)
                z = lax.dot_general(q2, kbd, nt, preferred_element_type=F32)
                dw = lax.dot_general(do2, vbd, nt, preferred_element_type=F32)
                ncol = 2 * nblk
                zt = [z[:, c * 128:(c + 1) * 128] for c in range(ncol)]
                if diag:
                    zt = [jnp.where(causal, t, MASKED) if c >= ncol - 2 else t for c, t in enumerate(zt)]
                bounds = [ncol * j // ATT_SPLIT for j in range(ATT_SPLIT + 1)]
                batches = [range(c0, c1) for c0, c1 in zip(bounds[:-1], bounds[1:]) if c0 < c1]
                sps, ex, ws, dls, pe = [None] * ncol, [None] * ncol, [None] * ncol, [None] * ncol, [None] * ncol

                def mass(cols):
                    for c in cols:
                        sps[c] = _softplus(zt[c])
                    got = jnp.dot(jnp.concatenate([_split_hi_lo(sps[c]) for c in cols], axis=0), ta_ref[...],
                                  preferred_element_type=F32)
                    for j, c in enumerate(cols):
                        ex[c] = got[j * 128:(j + 1) * 128]

                def weights(cols):
                    for c in cols:
                        u, hh = c // 2, c % 2
                        r_saved = jnp.sum(jnp.where(lane == kb0 + u, rs_ref[pl.ds(q0, ABLK), hh * 128:(hh + 1) * 128],
                                                    0.0), axis=1, keepdims=True)
                        w = jnp.exp(zt[c] - ex[c] - r_saved)
                        ws[c] = w.astype(BF16)
                        dls[c] = dw[:, c * 128:(c + 1) * 128] * w
                    got = jnp.dot(jnp.concatenate([_split_hi_lo(dls[c]) for c in cols], axis=0), tb_ref[...],
                                  preferred_element_type=F32)
                    for j, c in enumerate(cols):
                        pe[c] = got[j * 128:(j + 1) * 128]

                mass(batches[0])
                for j in range(len(batches)):
                    if j + 1 < len(batches):
                        mass(batches[j + 1])
                    weights(batches[j])
                pc = [pc_s[0], pc_s[1]]
                dzs = []
                for c in range(ncol):
                    hh = c % 2
                    one_minus_beta = jnp.exp(-sps[c])
                    dz = dls[c] * one_minus_beta - (pe[c][:, :128] + pc[hh]) * (1.0 - one_minus_beta)
                    pc[hh] = pc[hh] + pe[c][:, 128:]
                    dzs.append(dz.astype(BF16))
                pc_s[0] = pc[0]
                pc_s[1] = pc[1]
                dq_s[...] += jnp.dot(jnp.concatenate(dzs, axis=1), kbd, preferred_element_type=F32)
                by_head = lambda ts: jnp.concatenate([jnp.concatenate(ts[0::2], axis=1), jnp.concatenate(ts[1::2], axis=1)],
                                                     axis=0)
                rows = pl.ds(k0, nblk * ABLK)
                dk_s[rows, :] += lax.dot_general(by_head(dzs), q_st, tn, preferred_element_type=F32)
                dv_s[rows, :] += lax.dot_general(by_head(ws), do_st, tn, preferred_element_type=F32)

            gone = jnp.min(rs_ref[pl.ds(q0, ABLK), :], axis=0, keepdims=True) >= EXP_ZERO_AT
            lane1 = lax.broadcasted_iota(jnp.int32, (1, 128), 1)
            first = jnp.sum(jnp.where(gone[:, :128] & gone[:, 128:] & (lane1 < i), 1.0, 0.0)).astype(jnp.int32)
            n_top = jnp.minimum(i, ATT_TOP)
            i_low = i - n_top
            n_grp = i_low // ATT_GROUP
            rem = i_low - ATT_GROUP * n_grp
            for p in reversed(_pow2_below(ATT_GROUP)):
                @pl.when(((rem & p) != 0) & ((rem & (p - 1)) + p > first))
                def _():
                    step(rem & (p - 1), p, False)

            def inner(g, c2):
                step(rem + ATT_GROUP * g, ATT_GROUP, False)
                return c2

            lax.fori_loop(jnp.maximum(first - rem, 0) // ATT_GROUP, n_grp, inner, 0)
            for t in range(ATT_TOP + 1):
                @pl.when(n_top == t)
                def _():
                    step(i - t, t + 1, True)

            o_ref[0, pl.ds(q0, ABLK), :] = (dq_s[...] * scale).astype(BF16)
            o_ref[0, pl.ds(q0, ABLK), :] = (dq_s[...] * scale).astype(BF16)
            return carry

        lax.fori_loop(0, nb, qblock, 0)
        o_ref[1] = dk_s[...].astype(BF16)
        o_ref[2] = dv_s[...].astype(BF16)
        rider.wait(pl.program_id(0) == n_pairs - 1)

    col = lambda o: (lambda p: (0, p + o))
    return pl.pallas_call(
        body, out_shape=[SDS((3, lp, n_pairs * 128), BF16)] + rider.out_shape, grid=(n_pairs,),
        in_specs=[pl.BlockSpec((lp, 128), col(0)), pl.BlockSpec((lp, 128), col(n_pairs)),
                  pl.BlockSpec((lp, 128), col(2 * n_pairs)), pl.BlockSpec((lp, 128), col(0)),
                  pl.BlockSpec((lp, 256), col(0)),
                  pl.BlockSpec((256, 128), lambda p: (0, 0)), pl.BlockSpec((256, 256), lambda p: (0, 0))]
        + rider.in_specs(),
        out_specs=[pl.BlockSpec((3, lp, 128), lambda p: (0, 0, p))] + rider.out_specs(),
        scratch_shapes=[pltpu.VMEM((lp, 128), F32), pltpu.VMEM((lp, 128), F32),
                        pltpu.VMEM((ABLK, 128), F32), pltpu.VMEM((2, ABLK, 128), F32)] + rider.scratch(),
        name=name, compiler_params=_cp())(qkv, qkv, qkv, d_out, rsave, tri_after, tri_before, *rider.srcs)


def _conv_fwd_dw(cacg, w, b, name, rider=None):
    lp = cacg.shape[0]
    c = cacg.shape[1] // 2
    ncb = c // 128
    nchunk = lp // ABLK
    off = CONV_PAD - (CONV_K - 1)
    rider = rider or _Rider()

    def body(a_ref, g_ref, w_ref, b_ref, *rest):
        (y_ref,), (upad,) = rider.split(rest, 1, 1)
        rider.start(pl.program_id(0) == 0)
        if ncb > 1:
            rider.forward(pl.program_id(0) == ncb - 1)
        upad[0:CONV_PAD, :] = jnp.zeros((CONV_PAD, 128), F32)

        def fill(ch, carry):
            base = pl.multiple_of(ch * ABLK, ABLK)
            upad[pl.ds(base + CONV_PAD, ABLK), :] = a_ref[pl.ds(base, ABLK), :] * _sigmoid(g_ref[pl.ds(base, ABLK), :])
            return carry

        lax.fori_loop(0, nchunk, fill, 0)

        def comp(ch, carry):
            base = pl.multiple_of(ch * ABLK, ABLK)
            acc = jnp.zeros((ABLK, 128), F32)
            for k in range(CONV_K):
                acc = acc + upad[pl.ds(base + (off + k), ABLK), :] * w_ref[k:k + 1, :]
            y_ref[pl.ds(base, ABLK), :] = acc + b_ref[...]
            return carry

        lax.fori_loop(0, nchunk, comp, 0)
        rider.wait(pl.program_id(0) == ncb - 1)

    return pl.pallas_call(
        body, out_shape=[SDS((lp, c), F32)] + rider.out_shape, grid=(ncb,),
        in_specs=[pl.BlockSpec((lp, 128), lambda j: (0, j)), pl.BlockSpec((lp, 128), lambda j: (0, j + ncb)),
                  pl.BlockSpec((CONV_PAD, 128), lambda j: (0, j)), pl.BlockSpec((1, 128), lambda j: (0, j))]
        + rider.in_specs(),
        out_specs=[pl.BlockSpec((lp, 128), lambda j: (0, j))] + rider.out_specs(),
        scratch_shapes=[pltpu.VMEM((lp + CONV_PAD, 128), F32)] + rider.scratch(), name=name,
        compiler_params=_cp())(cacg, cacg, w, b, *rider.srcs)


def _ln_parts(x, g, b):
    mu = jnp.mean(x, axis=-1, keepdims=True)
    xc = x - mu
    rstd = lax.rsqrt(jnp.mean(xc * xc, axis=-1, keepdims=True) + EPS)
    xh = xc * rstd
    return xh, rstd, xh * g + b


def _conv_fwd_ln(yc, g, b, name):
    lp, c = yc.shape
    tm = _tile(lp, EW_ROWS, 16)

    def body(y_ref, g_ref, b_ref, o_ref):
        _, _, ln = _ln_parts(y_ref[...], g_ref[...], b_ref[...])
        o_ref[...] = (ln * _sigmoid(ln)).astype(BF16)

    return pl.pallas_call(
        body, out_shape=SDS((lp, c), BF16), grid=(lp // tm,),
        in_specs=[pl.BlockSpec((tm, c), lambda i: (i, 0)), pl.BlockSpec((1, c), lambda i: (0, 0)),
                  pl.BlockSpec((1, c), lambda i: (0, 0))],
        out_specs=pl.BlockSpec((tm, c), lambda i: (i, 0)), name=name, compiler_params=_cp())(yc, g, b)


def _conv_bwd_ln(yc, dout, g, b, name):
    lp, c = yc.shape
    tm = _tile(lp, EW_ROWS, 16)

    def body(y_ref, d_ref, g_ref, b_ref, o_ref, dg_ref, db_ref):
        @pl.when(pl.program_id(0) == 0)
        def _():
            dg_ref[...] = jnp.zeros_like(dg_ref)
            db_ref[...] = jnp.zeros_like(db_ref)

        xh, rstd, ln = _ln_parts(y_ref[...], g_ref[...], b_ref[...])
        s = _sigmoid(ln)
        dln = d_ref[...] * (s * (1.0 + ln * (1.0 - s)))
        dg_ref[...] += _sum8(dln * xh)
        db_ref[...] += _sum8(dln)
        dxh = dln * g_ref[...]
        o_ref[...] = rstd * (dxh - jnp.mean(dxh, axis=-1, keepdims=True)
                             - xh * jnp.mean(dxh * xh, axis=-1, keepdims=True))

    return pl.pallas_call(
        body, out_shape=[SDS((lp, c), F32), SDS((8, c), F32), SDS((8, c), F32)], grid=(lp // tm,),
        in_specs=[pl.BlockSpec((tm, c), lambda i: (i, 0)), pl.BlockSpec((tm, c), lambda i: (i, 0)),
                  pl.BlockSpec((1, c), lambda i: (0, 0)), pl.BlockSpec((1, c), lambda i: (0, 0))],
        out_specs=[pl.BlockSpec((tm, c), lambda i: (i, 0)), pl.BlockSpec((8, c), lambda i: (0, 0)),
                   pl.BlockSpec((8, c), lambda i: (0, 0))],
        name=name, compiler_params=_cp())(yc, dout, g, b)


def _conv_bwd_dw(dyc, cacg, w, name):
    lp, c = dyc.shape
    ncb = c // 128
    nchunk = lp // ABLK
    off = CONV_PAD - (CONV_K - 1)

    def body(dy_ref, a_ref, g_ref, w_ref, dcc_ref, dw_ref, db_ref, upad, dypad, dwacc):
        upad[0:CONV_PAD, :] = jnp.zeros((CONV_PAD, 128), F32)
        dypad[lp:lp + CONV_PAD, :] = jnp.zeros((CONV_PAD, 128), F32)
        dwacc[...] = jnp.zeros_like(dwacc)
        db_ref[...] = jnp.zeros_like(db_ref)

        def fill(ch, carry):
            base = pl.multiple_of(ch * ABLK, ABLK)
            upad[pl.ds(base + CONV_PAD, ABLK), :] = a_ref[pl.ds(base, ABLK), :] * _sigmoid(g_ref[pl.ds(base, ABLK), :])
            dypad[pl.ds(base, ABLK), :] = dy_ref[pl.ds(base, ABLK), :]
            return carry

        lax.fori_loop(0, nchunk, fill, 0)

        def comp(ch, carry):
            base = pl.multiple_of(ch * ABLK, ABLK)
            dy = dy_ref[pl.ds(base, ABLK), :]
            du = jnp.zeros((ABLK, 128), F32)
            for k in range(CONV_K):
                du = du + dypad[pl.ds(base + (CONV_K - 1 - k), ABLK), :] * w_ref[k:k + 1, :]
                dwacc[k * 8:(k + 1) * 8, :] += _sum8(dy * upad[pl.ds(base + (off + k), ABLK), :])
            db_ref[...] += _sum8(dy)
            a = a_ref[pl.ds(base, ABLK), :]
            s = _sigmoid(g_ref[pl.ds(base, ABLK), :])
            dcc_ref[0, pl.ds(base, ABLK), :] = (du * s).astype(BF16)
            dcc_ref[1, pl.ds(base, ABLK), :] = (du * a * (s * (1.0 - s))).astype(BF16)
            return carry

        lax.fori_loop(0, nchunk, comp, 0)
        dw_ref[...] = dwacc[...].reshape(CONV_PAD, 8, 128).sum(axis=1)

    return pl.pallas_call(
        body, out_shape=[SDS((2, lp, c), BF16), SDS((CONV_PAD, c), F32), SDS((8, c), F32)],
        grid=(ncb,),
        in_specs=[pl.BlockSpec((lp, 128), lambda j: (0, j)), pl.BlockSpec((lp, 128), lambda j: (0, j)),
                  pl.BlockSpec((lp, 128), lambda j: (0, j + ncb)), pl.BlockSpec((CONV_PAD, 128), lambda j: (0, j))],
        out_specs=[pl.BlockSpec((2, lp, 128), lambda j: (0, 0, j)),
                   pl.BlockSpec((CONV_PAD, 128), lambda j: (0, j)), pl.BlockSpec((8, 128), lambda j: (0, j))],
        scratch_shapes=[pltpu.VMEM((lp + CONV_PAD, 128), F32), pltpu.VMEM((lp + CONV_PAD, 128), F32),
                        pltpu.VMEM((CONV_PAD * 8, 128), F32)],
        name=name, compiler_params=_cp())(dyc, cacg, cacg, w)


def _mesh_pos():
    x, y, c = lax.axis_index("x"), lax.axis_index("y"), lax.axis_index("c")
    return x, y, c


def _peer(pos, r):
    x, y, c = pos
    px = (1 - x) if (r >> 2) & 1 else x
    py = (1 - y) if (r >> 1) & 1 else y
    pc = (1 - c) if r & 1 else c
    return (px, py, pc), 4 * px + 2 * py + pc


SIBLING = 1
OTHER_CHIPS = (2, 4, 6)
VIA_SIBLING = (3, 5, 7)


class _Job:
    def __init__(self, src, dst, scatter, src_layer=None, dst_layer=None):
        self.src, self.dst, self.scatter, self.src_layer, self.dst_layer = src, dst, scatter, src_layer, dst_layer

    def src_view(self, ins, idx):
        v = ins[self.src] if self.src_layer is None else ins[self.src].at[self.src_layer]
        return v.at[idx] if self.scatter else v

    def dst_view(self, outs, slot):
        v = outs[self.dst] if self.dst_layer is None else outs[self.dst].at[self.dst_layer]
        return v.at[slot]


def _remote(job, j, r, src, dst, to, send, recv):
    return pltpu.make_async_remote_copy(src_ref=src, dst_ref=dst, send_sem=send.at[j, r - 1], recv_sem=recv.at[j, r - 1],
                                        device_id=to, device_id_type=pl.DeviceIdType.MESH)


def _exchange_start(jobs, ins, outs, send, recv, loc):
    pos = _mesh_pos()
    me = 4 * pos[0] + 2 * pos[1] + pos[2]
    for j, job in enumerate(jobs):
        pltpu.make_async_copy(job.src_view(ins, me), job.dst_view(outs, me), loc.at[j]).start()
        for r in (range(1, N_DEV) if job.scatter else (SIBLING,) + OTHER_CHIPS):
            peer, peer_idx = _peer(pos, r)
            _remote(job, j, r, job.src_view(ins, peer_idx), job.dst_view(outs, me), peer, send, recv).start()


def _exchange_forward(jobs, ins, outs, send, recv, loc):
    pos = _mesh_pos()
    sibling, _ = _peer(pos, SIBLING)
    for j, job in enumerate(jobs):
        if job.scatter:
            continue
        for r in OTHER_CHIPS:
            peer, peer_idx = _peer(pos, r)
            slot = job.dst_view(outs, peer_idx)
            _remote(job, j, r, job.src_view(ins, peer_idx), slot, peer, send, recv).wait_recv()
            _remote(job, j, r ^ SIBLING, slot, slot, sibling, send, recv).start()


def _exchange_wait(jobs, ins, outs, send, recv, loc):
    pos = _mesh_pos()
    me = 4 * pos[0] + 2 * pos[1] + pos[2]
    for j, job in enumerate(jobs):
        for r in range(1, N_DEV):
            peer, peer_idx = _peer(pos, r)
            cp = _remote(job, j, r, job.src_view(ins, peer_idx), job.dst_view(outs, peer_idx), peer, send, recv)
            if job.scatter or r not in OTHER_CHIPS:
                cp.wait_recv()
            cp.wait_send()
        pltpu.make_async_copy(job.src_view(ins, me), job.dst_view(outs, me), loc.at[j]).wait()


def _exchange_sems(n_jobs):
    return [pltpu.SemaphoreType.DMA((n_jobs, N_DEV - 1)), pltpu.SemaphoreType.DMA((n_jobs, N_DEV - 1)),
            pltpu.SemaphoreType.DMA((n_jobs,))]


def _exchange(jobs, arrs, out_shape, name):
    n_in, n_out = len(arrs), len(out_shape)
    any_spec = pl.BlockSpec(memory_space=pl.ANY)

    def body(*refs):
        ins, outs, sems = refs[:n_in], refs[n_in:n_in + n_out], refs[n_in + n_out:]
        _exchange_start(jobs, ins, outs, *sems)
        _exchange_forward(jobs, ins, outs, *sems)
        _exchange_wait(jobs, ins, outs, *sems)

    return pl.pallas_call(
        body, out_shape=out_shape, in_specs=[any_spec] * n_in, out_specs=[any_spec] * n_out,
        scratch_shapes=_exchange_sems(len(jobs)),
        name=name, compiler_params=pltpu.CompilerParams(has_side_effects=True))(*arrs)


def _all_reduce_small(placed, cuts, rows, w, name):
    n_in = len(placed)

    def body(*refs):
        in_refs, o_refs = refs[:n_in], refs[n_in:n_in + len(cuts)]
        buf, send, recv = refs[n_in + len(cuts):]
        pos = _mesh_pos()
        me = 4 * pos[0] + 2 * pos[1] + pos[2]
        buf[me] = jnp.zeros((rows, w), F32)
        for ref, (arr, row0, col0, is_partial) in zip(in_refs, placed):
            val = ref[...].sum(axis=0, keepdims=True) if is_partial else ref[...]
            buf[me, row0:row0 + val.shape[0], col0:col0 + val.shape[1]] = val
        for r in range(1, N_DEV):
            peer, _ = _peer(pos, r)
            pltpu.make_async_remote_copy(src_ref=buf.at[me], dst_ref=buf.at[me], send_sem=send.at[r - 1],
                                         recv_sem=recv.at[r - 1], device_id=peer,
                                         device_id_type=pl.DeviceIdType.MESH).start()
        for r in range(1, N_DEV):
            peer, peer_idx = _peer(pos, r)
            cp = pltpu.make_async_remote_copy(src_ref=buf.at[me], dst_ref=buf.at[peer_idx], send_sem=send.at[r - 1],
                                              recv_sem=recv.at[r - 1], device_id=peer,
                                              device_id_type=pl.DeviceIdType.MESH)
            cp.wait_recv()
            cp.wait_send()
        acc = buf[0]
        for dev in range(1, N_DEV):
            acc = acc + buf[dev]
        for o_ref, (_, pieces) in zip(o_refs, cuts):
            for index, row0, nrows, col0, ncols in pieces:
                o_ref[index] = acc[row0:row0 + nrows, col0:col0 + ncols].reshape(o_ref.at[index].shape)

    vmem = pl.BlockSpec(memory_space=pltpu.VMEM)
    return pl.pallas_call(
        body, out_shape=[SDS(shape, F32) for shape, _ in cuts], in_specs=[vmem] * n_in, out_specs=[vmem] * len(cuts),
        scratch_shapes=[pltpu.VMEM((N_DEV, rows, w), F32), pltpu.SemaphoreType.DMA((N_DEV - 1,)),
                        pltpu.SemaphoreType.DMA((N_DEV - 1,))],
        name=name, compiler_params=pltpu.CompilerParams(has_side_effects=True))(*[p[0] for p in placed])


def _adamw_math(w, g, m, v):
    m = ADAM_B1 * m + (1.0 - ADAM_B1) * g
    v = ADAM_B2 * v + (1.0 - ADAM_B2) * (g * g)
    m_hat = m / (1.0 - ADAM_B1 ** ADAM_STEP)
    v_hat = v / (1.0 - ADAM_B2 ** ADAM_STEP)
    delta = -ADAM_LR * (m_hat / (jnp.sqrt(v_hat) + ADAM_EPS) + ADAM_WD * w)
    return delta, m, v


def _adamw_shard(parts, w, m, v, transposed, name):
    depth = len(parts)
    _, rr, cc = parts[0].shape
    if transposed:
        tc = _tile(cc, 256, 128)
        nt = cc // tc
        part_block, blk = (N_DEV, rr, tc), pl.BlockSpec((None, tc, rr), lambda l, i: (l, i, 0))
    else:
        tr = _tile(rr, 256, 8)
        nt = rr // tr
        part_block, blk = (N_DEV, tr, cc), pl.BlockSpec((None, tr, cc), lambda l, i: (l, i, 0))

    def body(*refs):
        p_refs = refs[:depth]
        w_ref, m_ref, v_ref, g_out, d_out, m_out, v_out = refs[depth:]
        for li in range(depth):
            @pl.when(pl.program_id(0) == li)
            def _(p_ref=p_refs[li]):
                g = p_ref[0].astype(F32)
                for dev in range(1, N_DEV):
                    g = g + p_ref[dev].astype(F32)
                if transposed:
                    g = g.T
                delta, mm, vv = _adamw_math(w_ref[...], g, m_ref[...], v_ref[...])
                g_out[...] = g
                d_out[...] = delta
                m_out[...] = mm
                v_out[...] = vv

    def part_spec(li):
        at = lambda l, i: jnp.where(l == li, i, jnp.where(l < li, 0, nt - 1))
        return pl.BlockSpec(part_block, (lambda l, i: (0, 0, at(l, i))) if transposed else (lambda l, i: (0, at(l, i), 0)))

    return pl.pallas_call(
        body, out_shape=[SDS(w.shape, F32)] * 4, grid=(depth, nt),
        in_specs=[part_spec(li) for li in range(depth)] + [blk, blk, blk],
        out_specs=[blk] * 4, name=name, compiler_params=_cp())(*parts, w, m, v)


def _adamw_small(gs, ws, ms, vs, name):
    n = len(gs)

    def body(*refs):
        g_refs, w_refs, m_refs, v_refs = (refs[k * n:(k + 1) * n] for k in range(4))
        outs = refs[4 * n:]
        for k in range(n):
            delta, mm, vv = _adamw_math(w_refs[k][...], g_refs[k][...], m_refs[k][...], v_refs[k][...])
            outs[k][...] = delta
            outs[n + k][...] = mm
            outs[2 * n + k][...] = vv

    res = pl.pallas_call(body, out_shape=[SDS(w.shape, F32) for w in ws] * 3, name=name,
                         compiler_params=_cp())(*gs, *ws, *ms, *vs)
    return res[:n], res[n:2 * n], res[2 * n:]


def _from_cols(t):
    return jnp.transpose(t, (1, 0, 2)).reshape(t.shape[1], N_DEV * t.shape[2])


def _swap(t):
    return jnp.swapaxes(t, -1, -2)


def kernel(x, meta_tokens, mix_norm_g, w_in, conv_dw_w, conv_dw_b, conv_ln_g, conv_ln_b, w_out, ffn_norm_g, w_gate, w_up, w_down, final_norm_g, loss_target, m_meta_tokens, m_mix_norm_g, m_w_in, m_conv_dw_w, m_conv_dw_b, m_conv_ln_g, m_conv_ln_b, m_w_out, m_ffn_norm_g, m_w_gate, m_w_up, m_w_down, m_final_norm_g, v_meta_tokens, v_mix_norm_g, v_w_in, v_conv_dw_w, v_conv_dw_b, v_conv_ln_g, v_conv_ln_b, v_w_out, v_ffn_norm_g, v_w_gate, v_w_up, v_w_down, v_final_norm_g):
    depth, d, in_shard = w_in.shape
    seq = x.shape[1]
    sb = N_HEADS * HEAD_DIM
    cc = conv_dw_w.shape[2] * N_DEV
    ff = w_gate.shape[2] * N_DEV
    assert in_shard * N_DEV == 3 * sb + 2 * cc and x.shape[0] == 1
    lr = N_META + seq
    lp = -(-lr // ABLK) * ABLK
    me = 4 * lax.axis_index("x") + 2 * lax.axis_index("y") + lax.axis_index("c")

    big_names = ("w_in", "w_out", "w_gate", "w_up", "w_down")
    transposed = {"w_in": True, "w_out": False, "w_gate": True, "w_up": True, "w_down": False}
    shard = dict(w_in=_swap(w_in).astype(BF16), w_out=w_out.astype(BF16), w_gate=_swap(w_gate).astype(BF16),
                 w_up=_swap(w_up).astype(BF16), w_down=w_down.astype(BF16))

    def gather_of(keys):
        names = sorted({n for n, _ in keys}, key=big_names.index)
        jobs = [_Job(names.index(n), j, False, src_layer=i) for j, (n, i) in enumerate(keys)]
        return jobs, [shard[n] for n in names], [SDS((N_DEV,) + shard[n].shape[1:], BF16) for n, _ in keys]

    first_keys = [("w_in", 0)]

    def riding(keys):
        return _Rider(*gather_of(keys)) if keys else None

    def receive(keys, arrays):
        for (n, li), t in zip(keys, arrays):
            wl[li][n] = t.reshape(-1, d)

    jobs, srcs, out_shape = gather_of(first_keys)
    for extra in (meta_tokens, conv_dw_w):
        jobs.append(_Job(len(srcs), len(out_shape), False))
        srcs.append(extra)
        out_shape.append(SDS((N_DEV,) + extra.shape, F32))
    gathered = _exchange(jobs, srcs, out_shape, "gather_first")
    wl = [dict() for _ in range(depth)]
    receive(first_keys, gathered)
    meta_full = _from_cols(gathered[-2])
    taps = jnp.transpose(gathered[-1], (1, 2, 0, 3)).reshape(depth, CONV_K, cc)
    taps = jnp.pad(taps, ((0, 0), (0, CONV_PAD - CONV_K), (0, 0)))
    tri_fwd, tri_after, tri_before = _tri_consts()

    h = jnp.concatenate([meta_full, x[0], jnp.zeros((lp - lr, d), F32)], axis=0)
    saved = []
    for i in range(depth):
        p = wl[i]
        sv = dict(h_in=h)
        qkv, hn = _rms_mm(h, mix_norm_g[i:i + 1], p["w_in"], 0, 3 * sb, BF16, f"proj_qkv_{i}", True)
        keys = [("w_out", 0)] if i == 0 else []
        cacg, *arrived = _rms_mm(h, mix_norm_g[i:i + 1], p["w_in"], 3 * sb, 2 * cc, F32, f"proj_conv_{i}", False,
                                 riding(keys))
        receive(keys, arrived)
        keys = [("w_gate", i), ("w_up", i)]
        attn, rsave, *arrived = _attn_fwd(qkv, tri_fwd, f"attn_fwd_{i}", riding(keys))
        receive(keys, arrived)
        keys = [("w_down", 0)] if i == 0 else []
        yc, *arrived = _conv_fwd_dw(cacg, taps[i], conv_dw_b[i:i + 1], f"conv_fwd_dw_{i}", riding(keys))
        receive(keys, arrived)
        conv = _conv_fwd_ln(yc, conv_ln_g[i:i + 1], conv_ln_b[i:i + 1], f"conv_fwd_ln_{i}")
        h = _mix_out(attn, conv, p["w_out"], h, f"mix_out_{i}")
        sv.update(qkv=qkv, hn=hn, cacg=cacg, rsave=rsave, yc=yc, attn=attn, conv=conv, h_mid=h)
        keys = [("w_in", i + 1), ("w_out", i + 1), ("w_down", i + 1)] if i + 1 < depth else []
        h, hn2, act, *arrived = _ffn_fwd(h, ffn_norm_g[i:i + 1], p["w_gate"], p["w_up"], p["w_down"], f"ffn_fwd_{i}",
                                         riding(keys))
        receive(keys, arrived)
        sv.update(hn2=hn2, act=act)
        saved.append(sv)

    tpad = jnp.pad(loss_target[0], ((N_META, lp - lr), (0, 0)))
    dh, dg_final, loss_part = _loss_head(h, tpad, final_norm_g.reshape(1, d), seq, "loss_head")
    loss = lax.psum(loss_part[0, 0], MESH_AXES)

    parts = {}

    def sending(items):
        srcs = [t.reshape(N_DEV, t.shape[0] // N_DEV, d) for _, t in items]
        return _Rider([_Job(j, j, True) for j in range(len(items))], srcs, [SDS(t.shape, BF16) for t in srcs])

    def arrive(items, arrays):
        parts.update({key: t for (key, _), t in zip(items, arrays)})

    grads = [None] * depth
    from_above = []
    for i in reversed(range(depth)):
        p, sv = wl[i], saved[i]
        d_g, d_u, *got = _ffn_bwd_act(dh, sv["hn2"], p["w_gate"], p["w_up"], p["w_down"], f"ffn_bwd_act_{i}",
                                      sending(from_above))
        arrive(from_above, got)
        gw_down = _mm_tn(sv["act"], dh, f"grad_w_down_{i}")
        gw_gate = _mm_tn(d_g, sv["hn2"], f"grad_w_gate_{i}")
        gw_up = _mm_tn(d_u, sv["hn2"], f"grad_w_up_{i}")
        items = [(("w_down", i), gw_down)]
        dh, dg_ffn, *got = _ffn_bwd_in(d_g, d_u, p["w_gate"], p["w_up"], sv["h_mid"], ffn_norm_g[i:i + 1], dh,
                                       f"ffn_bwd_in_{i}", sending(items))
        arrive(items, got)
        gw_out = jnp.concatenate([_mm_tn(sv["attn"], dh, f"grad_w_out_attn_{i}"),
                                  _mm_tn(sv["conv"], dh, f"grad_w_out_conv_{i}")], axis=0)
        d_attn = _mm_nt(dh, p["w_out"], 0, sb, BF16, f"mix_bwd_attn_{i}")
        d_conv = _mm_nt(dh, p["w_out"], sb, cc, F32, f"mix_bwd_conv_{i}")
        items = [(("w_gate", i), gw_gate), (("w_up", i), gw_up)]
        dqkv, *got = _attn_bwd(sv["qkv"], d_attn, sv["rsave"], tri_after, tri_before, f"attn_bwd_{i}", sending(items))
        arrive(items, got)
        dyc, dg_ln, db_ln = _conv_bwd_ln(sv["yc"], d_conv, conv_ln_g[i:i + 1], conv_ln_b[i:i + 1], f"conv_bwd_ln_{i}")
        dcc, g_taps_i, db_conv = _conv_bwd_dw(dyc, sv["cacg"], taps[i], f"conv_bwd_dw_{i}")
        items = [(("w_out", i), gw_out)] if i == 0 else []
        gw_in, *got = _grad_w_in_t(dqkv, dcc, sv["hn"], f"grad_w_in_{i}", sending(items))
        arrive(items, got)
        from_above = [(("w_in", i), gw_in)] + ([(("w_out", i), gw_out)] if i > 0 else [])
        items = from_above if i == 0 else []
        dh, dg_mix, *got = _mix_bwd_in(dqkv, dcc, p["w_in"], sv["h_in"], mix_norm_g[i:i + 1], dh, f"mix_bwd_in_{i}",
                                       sending(items))
        arrive(items, got)
        grads[i] = dict(taps=g_taps_i, dg_mix=dg_mix, dg_ffn=dg_ffn, dg_ln=dg_ln, db_ln=db_ln, db_conv=db_conv)
    grad_x = dh[N_META:lr][None]

    per = d // depth
    assert depth * cc <= d and d % depth == 0 and per % 128 == 0
    placed, cuts, row = [], [], 0
    for key in ("dg_mix", "dg_ffn"):
        placed += [(grads[i][key], row + i, 0, True) for i in range(depth)]
        cuts.append(((depth, d), [(slice(0, depth), row, depth, 0, d)]))
        row += depth
    placed.append((dg_final, row, 0, True))
    cuts.append(((1, d), [(slice(0, 1), row, 1, 0, d)]))
    row += 1
    for key in ("db_conv", "dg_ln", "db_ln"):
        placed += [(grads[i][key], row, i * per, True) for i in range(depth)]
        cuts.append(((depth, cc), [(slice(i, i + 1), row, 1, i * per, cc) for i in range(depth)]))
        row += 1
    placed += [(grads[i]["taps"], row, i * per, False) for i in range(depth)]
    cuts.append(((depth, CONV_PAD, cc), [(i, row, CONV_PAD, i * per, cc) for i in range(depth)]))
    row += CONV_PAD
    placed.append((dh[:N_META], row, 0, False))
    cuts.append(((N_META, d), [(slice(0, N_META), row, N_META, 0, d)]))
    row += N_META
    g_mix, g_ffn, g_final, g_cb, g_lg, g_lb, g_taps_full, g_meta_full = _all_reduce_small(
        placed, cuts, -(-row // 8) * 8, d, "reduce_small")
    csh = cc // N_DEV
    g_taps_own = lax.dynamic_slice_in_dim(g_taps_full[:, :CONV_K], me * csh, csh, axis=2)
    msh = d // N_DEV
    g_meta_own = lax.dynamic_slice_in_dim(g_meta_full, me * msh, msh, axis=1)

    row1 = lambda t: t.reshape(1, d)
    small_g = [g_meta_own, g_mix, g_taps_own, g_cb, g_lg, g_lb, g_ffn, g_final]
    small_w = [meta_tokens, mix_norm_g, conv_dw_w, conv_dw_b, conv_ln_g, conv_ln_b, ffn_norm_g, row1(final_norm_g)]
    small_m = [m_meta_tokens, m_mix_norm_g, m_conv_dw_w, m_conv_dw_b, m_conv_ln_g, m_conv_ln_b, m_ffn_norm_g,
               row1(m_final_norm_g)]
    small_v = [v_meta_tokens, v_mix_norm_g, v_conv_dw_w, v_conv_dw_b, v_conv_ln_g, v_conv_ln_b, v_ffn_norm_g,
               row1(v_final_norm_g)]
    s_delta, s_m, s_v = _adamw_small(small_g, small_w, small_m, small_v, "adamw_small")
    unrow = lambda ts: list(ts[:-1]) + [ts[-1].reshape(d)]
    small_g, s_delta, s_m, s_v = unrow(small_g), unrow(s_delta), unrow(s_m), unrow(s_v)

    big = []
    for n, w, m, v in zip(big_names, (w_in, w_out, w_gate, w_up, w_down), (m_w_in, m_w_out, m_w_gate, m_w_up, m_w_down),
                          (v_w_in, v_w_out, v_w_gate, v_w_up, v_w_down)):
        big.append(_adamw_shard([parts[(n, i)] for i in range(depth)], w, m, v, transposed[n], f"adamw_{n}"))
    b_in, b_out, b_gate, b_up, b_down = big

    def ordered(k, smalls):
        s_meta, s_mix, s_taps, s_cb, s_lg, s_lb, s_ffn, s_final = smalls
        return [s_meta, s_mix, b_in[k], s_taps, s_cb, s_lg, s_lb, b_out[k], s_ffn, b_gate[k], b_up[k], b_down[k], s_final]

    return (loss, grad_x, *ordered(0, small_g), *ordered(1, s_delta), *ordered(2, s_m), *ordered(3, s_v))
```

```python
import functools
import math

import jax
import jax.numpy as jnp
from jax import lax
from jax.experimental import pallas as pl
from jax.experimental.pallas import tpu as pltpu

F32 = jnp.float32
BF16 = jnp.bfloat16
SDS = jax.ShapeDtypeStruct

N_META = 16
N_HEADS = 8
HEAD_DIM = 64
CONV_K = 31
CONV_PAD = 32
ABLK = 128
ATT_GROUP = 4
ATT_SPLIT = 2
ATT_TOP = 2
EXP_ZERO_AT = 104.0
GONE = 1e30
MASKED = -1e30
FF_CHUNK = 256
FF_BLOCK = 1408
FF_ROWS = 528
EPS = 1e-6
N_DEV = 8
MESH_AXES = ("x", "y", "c")
ADAM_LR = 0.001
ADAM_B1 = 0.9
ADAM_B2 = 0.999
ADAM_EPS = 1e-08
ADAM_WD = 0.01
ADAM_STEP = 10
MIB = 1 << 20
VMEM_LIMIT_MIB = 48
MM_ROWS = 1056
EW_ROWS = 528


def _cp():
    return pltpu.CompilerParams(vmem_limit_bytes=VMEM_LIMIT_MIB * MIB)


def _tile(n, cap, mult):
    best = None
    for t in range(mult, min(n, cap) + 1, mult):
        if n % t == 0:
            best = t
    assert best is not None, (n, cap, mult)
    return best


def _sum8(v):
    r, c = v.shape
    return v.reshape(r // 8, 8, c).sum(axis=0)


def _sigmoid(x):
    return 1.0 / (1.0 + jnp.exp(-x))


def _ff_block(f):
    fb = _tile(f, FF_BLOCK, 128)
    return fb, [(s0, min(FF_CHUNK, fb - s0)) for s0 in range(0, fb, FF_CHUNK)]


NT = (((1,), (1,)), ((), ()))
TN = (((0,), (0,)), ((), ()))


def _rms_mm(h, g, wt, w_row0, n_cols, out_dtype, name, hn_out, rider=None):
    lp, d = h.shape
    tm = _tile(lp, MM_ROWS, 16)
    tn = _tile(math.gcd(n_cols, w_row0), 768, 128)
    off = w_row0 // tn
    ni, nj = lp // tm, n_cols // tn
    rider = rider or _Rider()

    def body(h_ref, g_ref, w_ref, *rest):
        outs, (hn_s,) = rider.split(rest, 2 if hn_out else 1, 1)
        o_ref = outs[0]
        hn_ref = outs[1] if hn_out else None
        rider.start((pl.program_id(0) == 0) & (pl.program_id(1) == 0))
        if ni > 1:
            rider.forward((pl.program_id(0) == ni - 1) & (pl.program_id(1) == 0))

        @pl.when(pl.program_id(1) == 0)
        def _():
            x = h_ref[...]
            r = lax.rsqrt(jnp.mean(x * x, axis=-1, keepdims=True) + EPS)
            hn = ((x * r) * g_ref[...]).astype(BF16)
            hn_s[...] = hn
            if hn_out:
                hn_ref[...] = hn

        o_ref[...] = lax.dot_general(hn_s[...], w_ref[...], NT, preferred_element_type=F32).astype(out_dtype)
        rider.wait((pl.program_id(0) == ni - 1) & (pl.program_id(1) == nj - 1))

    out_shape = [SDS((lp, n_cols), out_dtype)]
    out_specs = [pl.BlockSpec((tm, tn), lambda i, j: (i, j))]
    if hn_out:
        out_shape.append(SDS((lp, d), BF16))
        out_specs.append(pl.BlockSpec((tm, d), lambda i, j: (i, 0)))
    return pl.pallas_call(
        body, out_shape=out_shape + rider.out_shape, grid=(ni, nj),
        in_specs=[pl.BlockSpec((tm, d), lambda i, j: (i, 0)),
                  pl.BlockSpec((1, d), lambda i, j: (0, 0)),
                  pl.BlockSpec((tn, d), lambda i, j: (j + off, 0))] + rider.in_specs(),
        out_specs=out_specs + rider.out_specs(), scratch_shapes=[pltpu.VMEM((tm, d), BF16)] + rider.scratch(),
        name=name, compiler_params=_cp())(h, g, wt, *rider.srcs)


def _mm_nt(a, b, b_row0, n_out, out_dtype, name):
    m, k = a.shape
    tm = _tile(m, MM_ROWS, 16)
    tn = _tile(math.gcd(n_out, b_row0), 512, 128)
    off = b_row0 // tn

    def body(a_ref, b_ref, o_ref):
        o_ref[...] = lax.dot_general(a_ref[...].astype(BF16), b_ref[...], NT,
                                     preferred_element_type=F32).astype(out_dtype)

    return pl.pallas_call(
        body, out_shape=SDS((m, n_out), out_dtype), grid=(m // tm, n_out // tn),
        in_specs=[pl.BlockSpec((tm, k), lambda i, j: (i, 0)), pl.BlockSpec((tn, k), lambda i, j: (j + off, 0))],
        out_specs=pl.BlockSpec((tm, tn), lambda i, j: (i, j)), name=name, compiler_params=_cp())(a, b)


def _mix_out(attn, conv, w, res, name):
    m, ka = attn.shape
    kc = conv.shape[1]
    n = w.shape[1]
    assert ka == kc
    tm = _tile(m, MM_ROWS, 16)
    tn = _tile(n, 512, 128)

    def body(a_ref, c_ref, wa_ref, wc_ref, r_ref, o_ref):
        o_ref[...] = (r_ref[...] + jnp.dot(a_ref[...], wa_ref[...], preferred_element_type=F32)
                      + jnp.dot(c_ref[...], wc_ref[...], preferred_element_type=F32))

    return pl.pallas_call(
        body, out_shape=SDS((m, n), F32), grid=(m // tm, n // tn),
        in_specs=[pl.BlockSpec((tm, ka), lambda i, j: (i, 0)), pl.BlockSpec((tm, kc), lambda i, j: (i, 0)),
                  pl.BlockSpec((ka, tn), lambda i, j: (0, j)), pl.BlockSpec((kc, tn), lambda i, j: (1, j)),
                  pl.BlockSpec((tm, tn), lambda i, j: (i, j))],
        out_specs=pl.BlockSpec((tm, tn), lambda i, j: (i, j)), name=name, compiler_params=_cp())(attn, conv, w, w, res)


def _mm_tn(a, b, name):
    l, m = a.shape
    n = b.shape[1]
    tm = _tile(m, 1408, 128)
    tn = _tile(n, 1024, 128)
    tl = _tile(l, 1408, 128)
    nl = l // tl

    def body(a_ref, b_ref, o_ref, acc):
        @pl.when(pl.program_id(2) == 0)
        def _():
            acc[...] = jnp.zeros_like(acc)

        acc[...] += lax.dot_general(a_ref[...].astype(BF16), b_ref[...].astype(BF16), TN, preferred_element_type=F32)

        @pl.when(pl.program_id(2) == nl - 1)
        def _():
            o_ref[...] = acc[...].astype(BF16)

    return pl.pallas_call(
        body, out_shape=SDS((m, n), BF16), grid=(m // tm, n // tn, nl),
        in_specs=[pl.BlockSpec((tl, tm), lambda i, j, s: (s, i)),
                  pl.BlockSpec((tl, tn), lambda i, j, s: (s, j))],
        out_specs=pl.BlockSpec((tm, tn), lambda i, j, s: (i, j)),
        scratch_shapes=[pltpu.VMEM((tm, tn), F32)], name=name, compiler_params=_cp())(a, b)


def _grad_w_in_t(dqkv, dcc, hn, name, rider=None):
    nq, l, w = dqkv.shape
    nc = dcc.shape[0]
    d = hn.shape[1]
    assert dcc.shape[2] == w
    tn = _tile(d, 1024, 128)
    tl = _tile(l, 1408, 128)
    nl = l // tl
    nj = d // tn
    rider = rider or _Rider()

    def body(q_ref, c_ref, b_ref, *rest):
        (o_ref,), (acc,) = rider.split(rest, 1, 1)
        p, jj, s = pl.program_id(0), pl.program_id(1), pl.program_id(2)
        rider.start((p == 0) & (jj == 0) & (s == 0))

        @pl.when(s == 0)
        def _():
            acc[...] = jnp.zeros_like(acc)

        @pl.when(p < nq)
        def _():
            acc[...] += lax.dot_general(q_ref[...], b_ref[...], TN, preferred_element_type=F32)

        @pl.when(p >= nq)
        def _():
            acc[...] += lax.dot_general(c_ref[...], b_ref[...], TN, preferred_element_type=F32)

        @pl.when(s == nl - 1)
        def _():
            o_ref[...] = acc[...].astype(BF16)

        rider.wait((p == nq + nc - 1) & (jj == nj - 1) & (s == nl - 1))

    out, *arrived = pl.pallas_call(
        body, out_shape=[SDS((nq + nc, w, d), BF16)] + rider.out_shape, grid=(nq + nc, nj, nl),
        in_specs=[pl.BlockSpec((None, tl, w), lambda p, j, s: (jnp.minimum(p, nq - 1), s, 0)),
                  pl.BlockSpec((None, tl, w), lambda p, j, s: (jnp.maximum(p - nq, 0), s, 0)),
                  pl.BlockSpec((tl, tn), lambda p, j, s: (s, j))] + rider.in_specs(),
        out_specs=[pl.BlockSpec((None, w, tn), lambda p, j, s: (p, 0, j))] + rider.out_specs(),
        scratch_shapes=[pltpu.VMEM((w, tn), F32)] + rider.scratch(), name=name,
        compiler_params=_cp())(dqkv, dcc, hn, *rider.srcs)
    return [out.reshape((nq + nc) * w, d)] + arrived


def _rms_bwd_tail(acc, h_ref, g_ref, dres_ref, o_ref, dg_ref):
    x = h_ref[...]
    r = lax.rsqrt(jnp.mean(x * x, axis=-1, keepdims=True) + EPS)
    xr = x * r
    dyv = acc[...]
    gy = dyv * g_ref[...]
    o_ref[...] = dres_ref[...] + r * (gy - xr * jnp.mean(gy * xr, axis=-1, keepdims=True))
    dg_ref[...] += _sum8(dyv * xr)


def _mix_bwd_in(dqkv, dcc, wt, h, g, dres, name, rider=None):
    nq, lp, w = dqkv.shape
    nc = dcc.shape[0]
    d = h.shape[1]
    tm = _tile(lp, FF_ROWS, 16)
    ni = lp // tm
    rider = rider or _Rider()

    def body(q_ref, c_ref, w_ref, h_ref, g_ref, dres_ref, *rest):
        (o_ref, dg_ref), (acc,) = rider.split(rest, 2, 1)
        i = pl.program_id(0)
        rider.start(i == 0)

        @pl.when(i == 0)
        def _():
            dg_ref[...] = jnp.zeros_like(dg_ref)

        pieces = [q_ref[p] for p in range(nq)] + [c_ref[p] for p in range(nc)]
        total = None
        for p, piece in enumerate(pieces):
            part = jnp.dot(piece, w_ref[p * w:(p + 1) * w, :], preferred_element_type=F32)
            total = part if total is None else total + part
        acc[...] = total
        _rms_bwd_tail(acc, h_ref, g_ref, dres_ref, o_ref, dg_ref)
        rider.wait(i == ni - 1)

    return pl.pallas_call(
        body, out_shape=[SDS((lp, d), F32), SDS((8, d), F32)] + rider.out_shape, grid=(ni,),
        in_specs=[pl.BlockSpec((nq, tm, w), lambda i: (0, i, 0)),
                  pl.BlockSpec((nc, tm, w), lambda i: (0, i, 0)),
                  pl.BlockSpec(((nq + nc) * w, d), lambda i: (0, 0)),
                  pl.BlockSpec((tm, d), lambda i: (i, 0)),
                  pl.BlockSpec((1, d), lambda i: (0, 0)),
                  pl.BlockSpec((tm, d), lambda i: (i, 0))] + rider.in_specs(),
        out_specs=[pl.BlockSpec((tm, d), lambda i: (i, 0)), pl.BlockSpec((8, d), lambda i: (0, 0))]
        + rider.out_specs(),
        scratch_shapes=[pltpu.VMEM((tm, d), F32)] + rider.scratch(), name=name,
        compiler_params=_cp())(dqkv, dcc, wt, h, g, dres, *rider.srcs)


def _ffn_bwd_in(d_g, d_u, wg_t, wu_t, h, g, dres, name, rider=None):
    lp, f = d_g.shape
    d = h.shape[1]
    tm = _tile(lp, FF_ROWS, 16)
    tk = _tile(f, FF_BLOCK, 128)
    nk = f // tk
    ni = lp // tm
    rider = rider or _Rider()

    def body(dg_in, du_in, wg_ref, wu_ref, h_ref, g_ref, dres_ref, *rest):
        (o_ref, dg_ref), (acc,) = rider.split(rest, 2, 1)
        i, kk = pl.program_id(0), pl.program_id(1)
        rider.start((i == 0) & (kk == 0))

        @pl.when(kk == 0)
        def _():
            acc[...] = jnp.zeros_like(acc)

        @pl.when((kk == 0) & (i == 0))
        def _():
            dg_ref[...] = jnp.zeros_like(dg_ref)

        acc[...] += (jnp.dot(dg_in[...], wg_ref[...], preferred_element_type=F32)
                     + jnp.dot(du_in[...], wu_ref[...], preferred_element_type=F32))

        @pl.when(kk == nk - 1)
        def _():
            _rms_bwd_tail(acc, h_ref, g_ref, dres_ref, o_ref, dg_ref)

        rider.wait((i == ni - 1) & (kk == nk - 1))

    return pl.pallas_call(
        body, out_shape=[SDS((lp, d), F32), SDS((8, d), F32)] + rider.out_shape, grid=(ni, nk),
        in_specs=[pl.BlockSpec((tm, tk), lambda i, j: (i, j)), pl.BlockSpec((tm, tk), lambda i, j: (i, j)),
                  pl.BlockSpec((tk, d), lambda i, j: (j, 0)), pl.BlockSpec((tk, d), lambda i, j: (j, 0)),
                  pl.BlockSpec((tm, d), lambda i, j: (i, 0)),
                  pl.BlockSpec((1, d), lambda i, j: (0, 0)),
                  pl.BlockSpec((tm, d), lambda i, j: (i, 0))] + rider.in_specs(),
        out_specs=[pl.BlockSpec((tm, d), lambda i, j: (i, 0)), pl.BlockSpec((8, d), lambda i, j: (0, 0))]
        + rider.out_specs(),
        scratch_shapes=[pltpu.VMEM((tm, d), F32)] + rider.scratch(), name=name,
        compiler_params=_cp())(d_g, d_u, wg_t, wu_t, h, g, dres, *rider.srcs)


def _ffn_fwd(h, g, wg_t, wu_t, w_d, name, rider=None):
    lp, d = h.shape
    f = w_d.shape[0]
    tm = _tile(lp, FF_ROWS, 16)
    fb, subs = _ff_block(f)
    nc = f // fb
    ni = lp // tm
    rider = rider or _Rider()

    def body(h_ref, g_ref, wg_ref, wu_ref, wd_ref, *rest):
        (o_ref, hn_ref, act_ref), (hn_s, acc) = rider.split(rest, 3, 2)
        c = pl.program_id(1)
        rider.start((pl.program_id(0) == 0) & (c == 0))
        if ni > 1:
            rider.forward((pl.program_id(0) == ni - 1) & (c == 0))

        @pl.when(c == 0)
        def _():
            x = h_ref[...]
            r = lax.rsqrt(jnp.mean(x * x, axis=-1, keepdims=True) + EPS)
            hn = ((x * r) * g_ref[...]).astype(BF16)
            hn_s[...] = hn
            hn_ref[...] = hn
            acc[...] = jnp.zeros_like(acc)

        total = None
        for s0, sw in subs:
            gg = lax.dot_general(hn_s[...], wg_ref[s0:s0 + sw, :], NT, preferred_element_type=F32)
            uu = lax.dot_general(hn_s[...], wu_ref[s0:s0 + sw, :], NT, preferred_element_type=F32)
            act = ((gg * _sigmoid(gg)) * uu).astype(BF16)
            act_ref[:, s0:s0 + sw] = act
            part = jnp.dot(act, wd_ref[s0:s0 + sw, :], preferred_element_type=F32)
            total = part if total is None else total + part
        acc[...] += total

        @pl.when(c == nc - 1)
        def _():
            o_ref[...] = h_ref[...] + acc[...]

        rider.wait((pl.program_id(0) == ni - 1) & (c == nc - 1))

    chunk = pl.BlockSpec((fb, d), lambda i, j: (j, 0))
    return pl.pallas_call(
        body, out_shape=[SDS((lp, d), F32), SDS((lp, d), BF16), SDS((lp, f), BF16)] + rider.out_shape, grid=(ni, nc),
        in_specs=[pl.BlockSpec((tm, d), lambda i, j: (i, 0)), pl.BlockSpec((1, d), lambda i, j: (0, 0)),
                  chunk, chunk, chunk] + rider.in_specs(),
        out_specs=[pl.BlockSpec((tm, d), lambda i, j: (i, 0)),
                   pl.BlockSpec((tm, d), lambda i, j: (i, 0)),
                   pl.BlockSpec((tm, fb), lambda i, j: (i, j))] + rider.out_specs(),
        scratch_shapes=[pltpu.VMEM((tm, d), BF16), pltpu.VMEM((tm, d), F32)] + rider.scratch(),
        name=name, compiler_params=_cp())(h, g, wg_t, wu_t, w_d, *rider.srcs)


def _ffn_bwd_act(dh, hn, wg_t, wu_t, w_d, name, rider=None):
    lp, d = dh.shape
    f = w_d.shape[0]
    tm = _tile(lp, FF_ROWS, 16)
    fb, subs = _ff_block(f)
    nc = f // fb
    ni = lp // tm
    rider = rider or _Rider()

    def body(dh_ref, hn_ref, wg_ref, wu_ref, wd_ref, *rest):
        (dg_out, du_out), (dh_s,) = rider.split(rest, 2, 1)
        rider.start((pl.program_id(0) == 0) & (pl.program_id(1) == 0))

        @pl.when(pl.program_id(1) == 0)
        def _():
            dh_s[...] = dh_ref[...].astype(BF16)

        for s0, sw in subs:
            gg = lax.dot_general(hn_ref[...], wg_ref[s0:s0 + sw, :], NT, preferred_element_type=F32)
            uu = lax.dot_general(hn_ref[...], wu_ref[s0:s0 + sw, :], NT, preferred_element_type=F32)
            dact = lax.dot_general(dh_s[...], wd_ref[s0:s0 + sw, :], NT, preferred_element_type=F32)
            s = _sigmoid(gg)
            dg_out[:, s0:s0 + sw] = (dact * uu * (s * (1.0 + gg * (1.0 - s)))).astype(BF16)
            du_out[:, s0:s0 + sw] = (dact * (gg * s)).astype(BF16)
        rider.wait((pl.program_id(0) == ni - 1) & (pl.program_id(1) == nc - 1))

    chunk = pl.BlockSpec((fb, d), lambda i, j: (j, 0))
    out = pl.BlockSpec((tm, fb), lambda i, j: (i, j))
    return pl.pallas_call(
        body, out_shape=[SDS((lp, f), BF16), SDS((lp, f), BF16)] + rider.out_shape, grid=(ni, nc),
        in_specs=[pl.BlockSpec((tm, d), lambda i, j: (i, 0)), pl.BlockSpec((tm, d), lambda i, j: (i, 0)),
                  chunk, chunk, chunk] + rider.in_specs(),
        out_specs=[out, out] + rider.out_specs(), scratch_shapes=[pltpu.VMEM((tm, d), BF16)] + rider.scratch(),
        name=name, compiler_params=_cp())(dh, hn, wg_t, wu_t, w_d, *rider.srcs)


def _loss_head(h, tpad, g, n_real, name):
    lp, d = h.shape
    tm = _tile(lp, EW_ROWS, 16)

    def body(h_ref, t_ref, g_ref, dh_ref, dg_ref, loss_ref):
        i = pl.program_id(0)

        @pl.when(i == 0)
        def _():
            dg_ref[...] = jnp.zeros_like(dg_ref)
            loss_ref[...] = jnp.zeros_like(loss_ref)

        x = h_ref[...]
        r = lax.rsqrt(jnp.mean(x * x, axis=-1, keepdims=True) + EPS)
        xr = x * r
        y = xr * g_ref[...]
        row = i * tm + lax.broadcasted_iota(jnp.int32, (tm, d), 0)
        valid = (row >= N_META) & (row < N_META + n_real)
        diff = jnp.where(valid, y - t_ref[...], 0.0)
        loss_ref[...] += jnp.sum(diff * diff) * (0.5 / d)
        dy = diff * (1.0 / d)
        gy = dy * g_ref[...]
        dh_ref[...] = r * (gy - xr * jnp.mean(gy * xr, axis=-1, keepdims=True))
        dg_ref[...] += _sum8(dy * xr)

    return pl.pallas_call(
        body, out_shape=[SDS((lp, d), F32), SDS((8, d), F32), SDS((8, 128), F32)], grid=(lp // tm,),
        in_specs=[pl.BlockSpec((tm, d), lambda i: (i, 0)),
                  pl.BlockSpec((tm, d), lambda i: (i, 0)),
                  pl.BlockSpec((1, d), lambda i: (0, 0))],
        out_specs=[pl.BlockSpec((tm, d), lambda i: (i, 0)),
                   pl.BlockSpec((8, d), lambda i: (0, 0)),
                   pl.BlockSpec((8, 128), lambda i: (0, 0))],
        name=name, compiler_params=_cp())(h, tpad, g)


def _tri_consts():
    j = lax.broadcasted_iota(jnp.int32, (ABLK, ABLK), 0)
    s = lax.broadcasted_iota(jnp.int32, (ABLK, ABLK), 1)
    after = (j >= s).astype(BF16)
    before = (j < s).astype(BF16)
    ones = jnp.ones((ABLK, ABLK), BF16)
    two = lambda t: jnp.concatenate([t, t], axis=0)
    return (two(jnp.concatenate([after, ones], axis=1)),
            two(after),
            two(jnp.concatenate([before, ones], axis=1)))


def _softplus(z):
    neg_abs = lax.bitcast_convert_type(lax.bitcast_convert_type(z, jnp.uint32) | jnp.uint32(0x80000000), F32)
    return jnp.log(1.0 + jnp.exp(neg_abs)) + jnp.maximum(z, 0.0)


def _split_hi_lo(m):
    hi = m.astype(BF16)
    lo = (m - hi.astype(F32)).astype(BF16)
    return jnp.concatenate([hi, lo], axis=1)


def _head_halves(t2, in_a):
    zero = jnp.zeros_like(t2)
    return jnp.concatenate([jnp.where(in_a, t2, zero), jnp.where(in_a, zero, t2)], axis=0)


def _stack_blocks(t, nblk, in_a):
    return jnp.concatenate([_head_halves(t[u * ABLK:(u + 1) * ABLK], in_a) for u in range(nblk)], axis=0)


def _interleave(*gens):
    alive = list(gens)
    while alive:
        for g in list(alive):
            if next(g, alive) is alive:
                alive.remove(g)


def _attn_scale():
    scale = 1.0 / math.sqrt(HEAD_DIM)
    assert math.frexp(scale)[0] == 0.5, "a power of two, so that scaling q in bf16 is exact"
    return scale


def _pow2_below(n):
    assert n & (n - 1) == 0
    return [p for p in (64, 32, 16, 8, 4, 2, 1) if p < n]


class _Rider:
    def __init__(self, jobs=(), srcs=(), out_shape=()):
        self.jobs, self.srcs, self.out_shape = list(jobs), list(srcs), list(out_shape)
        self.any = [pl.BlockSpec(memory_space=pl.ANY)]

    def split(self, rest, n_out, n_scratch):
        ni, no = len(self.srcs), len(self.out_shape)
        self.ins, outs = rest[:ni], rest[ni:ni + n_out]
        self.outs = rest[ni + n_out:ni + n_out + no]
        scratch = rest[ni + n_out + no:ni + n_out + no + n_scratch]
        self.sems = rest[ni + n_out + no + n_scratch:]
        return outs, scratch

    def start(self, first):
        self.forwarded = False
        if self.jobs:
            @pl.when(first)
            def _():
                _exchange_start(self.jobs, self.ins, self.outs, *self.sems)

    def forward(self, late):
        self.forwarded = True
        if self.jobs:
            @pl.when(late)
            def _():
                _exchange_forward(self.jobs, self.ins, self.outs, *self.sems)

    def wait(self, last):
        if self.jobs:
            @pl.when(last)
            def _():
                if not self.forwarded:
                    _exchange_forward(self.jobs, self.ins, self.outs, *self.sems)
                _exchange_wait(self.jobs, self.ins, self.outs, *self.sems)

    def in_specs(self):
        return self.any * len(self.srcs)

    def out_specs(self):
        return self.any * len(self.out_shape)

    def scratch(self):
        return _exchange_sems(len(self.jobs)) if self.jobs else []


def _attn_fwd(qkv, tri_fwd, name, rider=None):
    lp = qkv.shape[0]
    n_pairs = (N_HEADS * HEAD_DIM) // 128
    nb = lp // ABLK
    assert nb <= 128 and 2 * HEAD_DIM == 128
    scale = _attn_scale()
    nt = (((1,), (1,)), ((), ()))
    rider = rider or _Rider()

    def body(q_ref, k_ref, v_ref, tri_ref, *rest):
        (o_ref, rs_ref), (r_s, acc_s, rs_s) = rider.split(rest, 2, 3)
        rider.start(pl.program_id(0) == 0)
        if n_pairs > 1:
            rider.forward(pl.program_id(0) == n_pairs - 1)
        lane = lax.broadcasted_iota(jnp.int32, (ABLK, 128), 1)
        row = lax.broadcasted_iota(jnp.int32, (ABLK, 128), 0)
        in_a = lane < HEAD_DIM
        causal = lane < row

        def begin(st, i):
            r_s[st] = jnp.zeros(r_s.shape[1:], F32)
            acc_s[st] = jnp.zeros(acc_s.shape[1:], F32)
            rs_s[st] = jnp.full(rs_s.shape[1:], GONE, F32)
            return q_ref[pl.ds(pl.multiple_of(i * ABLK, ABLK), ABLK), :] * scale

        def live(st):
            least = jnp.min(jnp.minimum(r_s[st, 0], r_s[st, 1]), axis=0, keepdims=True)
            return (least[0, 0] < EXP_ZERO_AT).astype(jnp.int32)

        def step(*args):
            _interleave(stages(*args))

        def stages(st, q2, kb0, nblk, diag):
            k0 = pl.multiple_of(kb0 * ABLK, ABLK)
            kbd = _stack_blocks(k_ref[pl.ds(k0, nblk * ABLK), :], nblk, in_a)
            vbd = _stack_blocks(v_ref[pl.ds(k0, nblk * ABLK), :], nblk, in_a)
            z = lax.dot_general(q2, kbd, nt, preferred_element_type=F32)
            ncol = 2 * nblk
            zt = [z[:, c * 128:(c + 1) * 128] for c in range(ncol)]
            if diag:
                zt = [jnp.where(causal, t, MASKED) if c >= ncol - 2 else t for c, t in enumerate(zt)]
            yield
            bounds = [ncol * j // ATT_SPLIT for j in range(ATT_SPLIT + 1)]
            ce = [None] * ncol
            for c0, c1 in reversed([b for b in zip(bounds[:-1], bounds[1:]) if b[0] < b[1]]):
                parts = [_split_hi_lo(_softplus(zt[c])) for c in range(c0, c1)]
                got = jnp.dot(jnp.concatenate(parts, axis=0), tri_ref[...], preferred_element_type=F32)
                for c in range(c0, c1):
                    ce[c] = got[(c - c0) * 128:(c - c0 + 1) * 128]
                yield
            rr = [r_s[st, 0], r_s[st, 1]]
            rsv = [rs_s[st, :, :128], rs_s[st, :, 128:]]
            ws = [None] * ncol
            for u in reversed(range(nblk)):
                for hh in range(2):
                    c = 2 * u + hh
                    ws[c] = jnp.exp(zt[c] - ce[c][:, :128] - rr[hh]).astype(BF16)
                    rsv[hh] = jnp.where(lane == kb0 + u, rr[hh], rsv[hh])
                    rr[hh] = rr[hh] + ce[c][:, 128:]
            acc_s[st] += jnp.dot(jnp.concatenate(ws, axis=1), vbd, preferred_element_type=F32)
            r_s[st, 0] = rr[0]
            r_s[st, 1] = rr[1]
            rs_s[st, :, :128] = rsv[0]
            rs_s[st, :, 128:] = rsv[1]

        def finish(st, q2, i, n_top):
            i_low = i - n_top
            n_grp = i_low // ATT_GROUP

            def more(c):
                return (c[0] < n_grp) & (c[1] > 0)

            def inner(c):
                step(st, q2, i_low - ATT_GROUP * (c[0] + 1), ATT_GROUP, False)
                return c[0] + 1, live(st)

            _, alive = lax.while_loop(more, inner, (jnp.int32(0), live(st)))
            rem = i_low - ATT_GROUP * n_grp
            for p in _pow2_below(ATT_GROUP):
                def last_steps(p=p):
                    step(st, q2, rem & (p - 1), p, False)
                    return live(st)

                alive = lax.cond(((rem & p) != 0) & (alive > 0), last_steps, lambda alive=alive: alive)

            q0 = pl.multiple_of(i * ABLK, ABLK)
            o_ref[pl.ds(q0, ABLK), :] = acc_s[st].astype(BF16)
            rs_ref[pl.ds(q0, ABLK), :] = rs_s[st]

        def single(i):
            q2 = begin(0, i)
            n_top = jnp.minimum(i, ATT_TOP)
            for t in range(ATT_TOP + 1):
                @pl.when(n_top == t)
                def _():
                    step(0, q2, i - t, t + 1, True)

            finish(0, q2, i, n_top)

        def pair(i):
            qa, qb = begin(0, i), begin(1, i + 1)
            _interleave(stages(0, qa, i - ATT_TOP, ATT_TOP + 1, True), stages(1, qb, i + 1 - ATT_TOP, ATT_TOP + 1, True))
            finish(0, qa, i, ATT_TOP)
            finish(1, qb, i + 1, ATT_TOP)

        n_head = min(ATT_TOP, nb)
        n_pair = (nb - n_head) // 2
        n_single = nb - 2 * n_pair

        def singles(k, carry):
            single(jnp.where(k < n_head, k, nb - n_single + k))
            return carry

        def pairs(j, carry):
            pair(n_head + 2 * j)
            return carry

        lax.fori_loop(0, n_single, singles, 0)
        lax.fori_loop(0, n_pair, pairs, 0)
        rider.wait(pl.program_id(0) == n_pairs - 1)

    col = lambda o: (lambda p: (0, p + o))
    return pl.pallas_call(
        body, out_shape=[SDS((lp, n_pairs * 128), BF16), SDS((lp, n_pairs * 256), F32)] + rider.out_shape,
        grid=(n_pairs,),
        in_specs=[pl.BlockSpec((lp, 128), col(0)), pl.BlockSpec((lp, 128), col(n_pairs)),
                  pl.BlockSpec((lp, 128), col(2 * n_pairs)), pl.BlockSpec((256, 256), lambda p: (0, 0))]
        + rider.in_specs(),
        out_specs=[pl.BlockSpec((lp, 128), col(0)), pl.BlockSpec((lp, 256), col(0))] + rider.out_specs(),
        scratch_shapes=[pltpu.VMEM((2, 2, ABLK, 128), F32), pltpu.VMEM((2, ABLK, 128), F32),
                        pltpu.VMEM((2, ABLK, 256), F32)] + rider.scratch(),
        name=name, compiler_params=_cp())(qkv, qkv, qkv, tri_fwd, *rider.srcs)


def _attn_bwd(qkv, d_out, rsave, tri_after, tri_before, name, rider=None):
    lp = qkv.shape[0]
    n_pairs = (N_HEADS * HEAD_DIM) // 128
    nb = lp // ABLK
    scale = _attn_scale()
    nt = (((1,), (1,)), ((), ()))
    tn = (((0,), (0,)), ((), ()))
    rider = rider or _Rider()

    def body(q_ref, k_ref, v_ref, do_ref, rs_ref, ta_ref, tb_ref, *rest):
        (o_ref,), (dk_s, dv_s, dq_s, pc_s) = rider.split(rest, 1, 4)
        rider.start(pl.program_id(0) == 0)
        lane = lax.broadcasted_iota(jnp.int32, (ABLK, 128), 1)
        row = lax.broadcasted_iota(jnp.int32, (ABLK, 128), 0)
        in_a = lane < HEAD_DIM
        causal = lane < row
        dk_s[...] = jnp.zeros_like(dk_s)
        dv_s[...] = jnp.zeros_like(dv_s)

        def begin(st, i):
            q0 = pl.multiple_of(i * ABLK, ABLK)
            q2 = q_ref[pl.ds(q0, ABLK), :] * scale
            do2 = do_ref[pl.ds(q0, ABLK), :]
            dq_s[st] = jnp.zeros(dq_s.shape[1:], F32)
            pc_s[st] = jnp.zeros(pc_s.shape[1:], F32)
            return dict(q0=q0, q2=q2, do2=do2, q_st=_head_halves(q2, in_a), do_st=_head_halves(do2, in_a))

        def step(*args):
            _interleave(stages(*args))

        def stages(st, blk, kb0, nblk, diag):
            q0, q2, do2 = blk["q0"], blk["q2"], blk["do2"]
            k0 = pl.multiple_of(kb0 * ABLK, ABLK)
            kbd = _stack_blocks(k_ref[pl.ds(k0, nblk * ABLK), :], nblk, in_a)
            vbd = _stack_blocks(v_ref[pl.ds(k0, nblk * ABLK), :], nblk, in_a)
            z = lax.dot_general(q2, kbd, nt, preferred_element_type=F32)
            dw = lax.dot_general(do2, vbd, nt, preferred_element_type=F32)
            ncol = 2 * nblk
            zt = [z[:, c * 128:(c + 1) * 128] for c in range(ncol)]
            if diag:
                zt = [jnp.where(causal, t, MASKED) if c >= ncol - 2 else t for c, t in enumerate(zt)]
            bounds = [ncol * j // ATT_SPLIT for j in range(ATT_SPLIT + 1)]
            batches = [range(c0, c1) for c0, c1 in zip(bounds[:-1], bounds[1:]) if c0 < c1]
            sps, ex, ws, dls, pe = [None] * ncol, [None] * ncol, [None] * ncol, [None] * ncol, [None] * ncol

            def mass(cols):
                for c in cols:
                    sps[c] = _softplus(zt[c])
                got = jnp.dot(jnp.concatenate([_split_hi_lo(sps[c]) for c in cols], axis=0), ta_ref[...],
                              preferred_element_type=F32)
                for j, c in enumerate(cols):
                    ex[c] = got[j * 128:(j + 1) * 128]

            def weights(cols):
                for c in cols:
                    u, hh = c // 2, c % 2
                    r_saved = jnp.sum(jnp.where(lane == kb0 + u, rs_ref[pl.ds(q0, ABLK), hh * 128:(hh + 1) * 128],
                                                0.0), axis=1, keepdims=True)
                    w = jnp.exp(zt[c] - ex[c] - r_saved)
                    ws[c] = w.astype(BF16)
                    dls[c] = dw[:, c * 128:(c + 1) * 128] * w
                got = jnp.dot(jnp.concatenate([_split_hi_lo(dls[c]) for c in cols], axis=0), tb_ref[...],
                              preferred_element_type=F32)
                for j, c in enumerate(cols):
                    pe[c] = got[j * 128:(j + 1) * 128]

            yield
            mass(batches[0])
            yield
            for j in range(len(batches)):
                if j + 1 < len(batches):
                    mass(batches[j + 1])
                    yield
                weights(batches[j])
                yield
            pc = [pc_s[st, 0], pc_s[st, 1]]
            dzs = []
            for c in range(ncol):
                hh = c % 2
                one_minus_beta = jnp.exp(-sps[c])
                dz = dls[c] * one_minus_beta - (pe[c][:, :128] + pc[hh]) * (1.0 - one_minus_beta)
                pc[hh] = pc[hh] + pe[c][:, 128:]
                dzs.append(dz.astype(BF16))
            pc_s[st, 0] = pc[0]
            pc_s[st, 1] = pc[1]
            dq_s[st] += jnp.dot(jnp.concatenate(dzs, axis=1), kbd, preferred_element_type=F32)
            by_head = lambda ts: jnp.concatenate([jnp.concatenate(ts[0::2], axis=1), jnp.concatenate(ts[1::2], axis=1)],
                                                 axis=0)
            rows = pl.ds(k0, nblk * ABLK)
            dk_s[rows, :] += lax.dot_general(by_head(dzs), blk["q_st"], tn, preferred_element_type=F32)
            dv_s[rows, :] += lax.dot_general(by_head(ws), blk["do_st"], tn, preferred_element_type=F32)

        def below(st, blk, i, n_top):
            gone = jnp.min(rs_ref[pl.ds(blk["q0"], ABLK), :], axis=0, keepdims=True) >= EXP_ZERO_AT
            lane1 = lax.broadcasted_iota(jnp.int32, (1, 128), 1)
            first = jnp.sum(jnp.where(gone[:, :128] & gone[:, 128:] & (lane1 < i), 1.0, 0.0)).astype(jnp.int32)
            i_low = i - n_top
            n_grp = i_low // ATT_GROUP
            rem = i_low - ATT_GROUP * n_grp
            for p in reversed(_pow2_below(ATT_GROUP)):
                @pl.when(((rem & p) != 0) & ((rem & (p - 1)) + p > first))
                def _():
                    step(st, blk, rem & (p - 1), p, False)

            def inner(g, c2):
                step(st, blk, rem + ATT_GROUP * g, ATT_GROUP, False)
                return c2

            lax.fori_loop(jnp.maximum(first - rem, 0) // ATT_GROUP, n_grp, inner, 0)

        def done(st, blk):
            o_ref[0, pl.ds(blk["q0"], ABLK), :] = (dq_s[st] * scale).astype(BF16)

        def single(i):
            blk = begin(0, i)
            n_top = jnp.minimum(i, ATT_TOP)
            below(0, blk, i, n_top)
            for t in range(ATT_TOP + 1):
                @pl.when(n_top == t)
                def _():
                    step(0, blk, i - t, t + 1, True)

            done(0, blk)

        def pair(i):
            a, b = begin(0, i), begin(1, i + 1)
            below(0, a, i, ATT_TOP)
            below(1, b, i + 1, ATT_TOP)
            _interleave(stages(0, a, i - ATT_TOP, ATT_TOP + 1, True), stages(1, b, i + 1 - ATT_TOP, ATT_TOP + 1, True))
            done(0, a)
            done(1, b)

        n_head = min(ATT_TOP, nb)
        n_pair = (nb - n_head) // 2
        n_single = nb - 2 * n_pair

        def singles(k, carry):
            single(jnp.where(k < n_head, k, nb - n_single + k))
            return carry

        def pairs(j, carry):
            pair(n_head + 2 * j)
            return carry

        lax.fori_loop(0, n_single, singles, 0)
        lax.fori_loop(0, n_pair, pairs, 0)
        o_ref[1] = dk_s[...].astype(BF16)
        o_ref[2] = dv_s[...].astype(BF16)
        rider.wait(pl.program_id(0) == n_pairs - 1)

    col = lambda o: (lambda p: (0, p + o))
    return pl.pallas_call(
        body, out_shape=[SDS((3, lp, n_pairs * 128), BF16)] + rider.out_shape, grid=(n_pairs,),
        in_specs=[pl.BlockSpec((lp, 128), col(0)), pl.BlockSpec((lp, 128), col(n_pairs)),
                  pl.BlockSpec((lp, 128), col(2 * n_pairs)), pl.BlockSpec((lp, 128), col(0)),
                  pl.BlockSpec((lp, 256), col(0)),
                  pl.BlockSpec((256, 128), lambda p: (0, 0)), pl.BlockSpec((256, 256), lambda p: (0, 0))]
        + rider.in_specs(),
        out_specs=[pl.BlockSpec((3, lp, 128), lambda p: (0, 0, p))] + rider.out_specs(),
        scratch_shapes=[pltpu.VMEM((lp, 128), F32), pltpu.VMEM((lp, 128), F32),
                        pltpu.VMEM((2, ABLK, 128), F32), pltpu.VMEM((2, 2, ABLK, 128), F32)] + rider.scratch(),
        name=name, compiler_params=_cp())(qkv, qkv, qkv, d_out, rsave, tri_after, tri_before, *rider.srcs)


def _conv_fwd_dw(cacg, w, b, name, rider=None):
    lp = cacg.shape[0]
    c = cacg.shape[1] // 2
    ncb = c // 128
    nchunk = lp // ABLK
    off = CONV_PAD - (CONV_K - 1)
    rider = rider or _Rider()

    def body(a_ref, g_ref, w_ref, b_ref, *rest):
        (y_ref,), (upad,) = rider.split(rest, 1, 1)
        rider.start(pl.program_id(0) == 0)
        if ncb > 1:
            rider.forward(pl.program_id(0) == ncb - 1)
        upad[0:CONV_PAD, :] = jnp.zeros((CONV_PAD, 128), F32)

        def fill(ch, carry):
            base = pl.multiple_of(ch * ABLK, ABLK)
            upad[pl.ds(base + CONV_PAD, ABLK), :] = a_ref[pl.ds(base, ABLK), :] * _sigmoid(g_ref[pl.ds(base, ABLK), :])
            return carry

        lax.fori_loop(0, nchunk, fill, 0)

        def comp(ch, carry):
            base = pl.multiple_of(ch * ABLK, ABLK)
            acc = jnp.zeros((ABLK, 128), F32)
            for k in range(CONV_K):
                acc = acc + upad[pl.ds(base + (off + k), ABLK), :] * w_ref[k:k + 1, :]
            y_ref[pl.ds(base, ABLK), :] = acc + b_ref[...]
            return carry

        lax.fori_loop(0, nchunk, comp, 0)
        rider.wait(pl.program_id(0) == ncb - 1)

    return pl.pallas_call(
        body, out_shape=[SDS((lp, c), F32)] + rider.out_shape, grid=(ncb,),
        in_specs=[pl.BlockSpec((lp, 128), lambda j: (0, j)), pl.BlockSpec((lp, 128), lambda j: (0, j + ncb)),
                  pl.BlockSpec((CONV_PAD, 128), lambda j: (0, j)), pl.BlockSpec((1, 128), lambda j: (0, j))]
        + rider.in_specs(),
        out_specs=[pl.BlockSpec((lp, 128), lambda j: (0, j))] + rider.out_specs(),
        scratch_shapes=[pltpu.VMEM((lp + CONV_PAD, 128), F32)] + rider.scratch(), name=name,
        compiler_params=_cp())(cacg, cacg, w, b, *rider.srcs)


def _ln_parts(x, g, b):
    mu = jnp.mean(x, axis=-1, keepdims=True)
    xc = x - mu
    rstd = lax.rsqrt(jnp.mean(xc * xc, axis=-1, keepdims=True) + EPS)
    xh = xc * rstd
    return xh, rstd, xh * g + b


def _conv_fwd_ln(yc, g, b, name):
    lp, c = yc.shape
    tm = _tile(lp, EW_ROWS, 16)

    def body(y_ref, g_ref, b_ref, o_ref):
        _, _, ln = _ln_parts(y_ref[...], g_ref[...], b_ref[...])
        o_ref[...] = (ln * _sigmoid(ln)).astype(BF16)

    return pl.pallas_call(
        body, out_shape=SDS((lp, c), BF16), grid=(lp // tm,),
        in_specs=[pl.BlockSpec((tm, c), lambda i: (i, 0)), pl.BlockSpec((1, c), lambda i: (0, 0)),
                  pl.BlockSpec((1, c), lambda i: (0, 0))],
        out_specs=pl.BlockSpec((tm, c), lambda i: (i, 0)), name=name, compiler_params=_cp())(yc, g, b)


def _conv_bwd_ln(yc, dout, g, b, name):
    lp, c = yc.shape
    tm = _tile(lp, EW_ROWS, 16)

    def body(y_ref, d_ref, g_ref, b_ref, o_ref, dg_ref, db_ref):
        @pl.when(pl.program_id(0) == 0)
        def _():
            dg_ref[...] = jnp.zeros_like(dg_ref)
            db_ref[...] = jnp.zeros_like(db_ref)

        xh, rstd, ln = _ln_parts(y_ref[...], g_ref[...], b_ref[...])
        s = _sigmoid(ln)
        dln = d_ref[...] * (s * (1.0 + ln * (1.0 - s)))
        dg_ref[...] += _sum8(dln * xh)
        db_ref[...] += _sum8(dln)
        dxh = dln * g_ref[...]
        o_ref[...] = rstd * (dxh - jnp.mean(dxh, axis=-1, keepdims=True)
                             - xh * jnp.mean(dxh * xh, axis=-1, keepdims=True))

    return pl.pallas_call(
        body, out_shape=[SDS((lp, c), F32), SDS((8, c), F32), SDS((8, c), F32)], grid=(lp // tm,),
        in_specs=[pl.BlockSpec((tm, c), lambda i: (i, 0)), pl.BlockSpec((tm, c), lambda i: (i, 0)),
                  pl.BlockSpec((1, c), lambda i: (0, 0)), pl.BlockSpec((1, c), lambda i: (0, 0))],
        out_specs=[pl.BlockSpec((tm, c), lambda i: (i, 0)), pl.BlockSpec((8, c), lambda i: (0, 0)),
                   pl.BlockSpec((8, c), lambda i: (0, 0))],
        name=name, compiler_params=_cp())(yc, dout, g, b)


def _conv_bwd_dw(dyc, cacg, w, name):
    lp, c = dyc.shape
    ncb = c // 128
    nchunk = lp // ABLK
    off = CONV_PAD - (CONV_K - 1)

    def body(dy_ref, a_ref, g_ref, w_ref, dcc_ref, dw_ref, db_ref, upad, dypad, dwacc):
        upad[0:CONV_PAD, :] = jnp.zeros((CONV_PAD, 128), F32)
        dypad[lp:lp + CONV_PAD, :] = jnp.zeros((CONV_PAD, 128), F32)
        dwacc[...] = jnp.zeros_like(dwacc)
        db_ref[...] = jnp.zeros_like(db_ref)

        def fill(ch, carry):
            base = pl.multiple_of(ch * ABLK, ABLK)
            upad[pl.ds(base + CONV_PAD, ABLK), :] = a_ref[pl.ds(base, ABLK), :] * _sigmoid(g_ref[pl.ds(base, ABLK), :])
            dypad[pl.ds(base, ABLK), :] = dy_ref[pl.ds(base, ABLK), :]
            return carry

        lax.fori_loop(0, nchunk, fill, 0)

        def comp(ch, carry):
            base = pl.multiple_of(ch * ABLK, ABLK)
            dy = dy_ref[pl.ds(base, ABLK), :]
            du = jnp.zeros((ABLK, 128), F32)
            for k in range(CONV_K):
                du = du + dypad[pl.ds(base + (CONV_K - 1 - k), ABLK), :] * w_ref[k:k + 1, :]
                dwacc[k * 8:(k + 1) * 8, :] += _sum8(dy * upad[pl.ds(base + (off + k), ABLK), :])
            db_ref[...] += _sum8(dy)
            a = a_ref[pl.ds(base, ABLK), :]
            s = _sigmoid(g_ref[pl.ds(base, ABLK), :])
            dcc_ref[0, pl.ds(base, ABLK), :] = (du * s).astype(BF16)
            dcc_ref[1, pl.ds(base, ABLK), :] = (du * a * (s * (1.0 - s))).astype(BF16)
            return carry

        lax.fori_loop(0, nchunk, comp, 0)
        dw_ref[...] = dwacc[...].reshape(CONV_PAD, 8, 128).sum(axis=1)

    return pl.pallas_call(
        body, out_shape=[SDS((2, lp, c), BF16), SDS((CONV_PAD, c), F32), SDS((8, c), F32)],
        grid=(ncb,),
        in_specs=[pl.BlockSpec((lp, 128), lambda j: (0, j)), pl.BlockSpec((lp, 128), lambda j: (0, j)),
                  pl.BlockSpec((lp, 128), lambda j: (0, j + ncb)), pl.BlockSpec((CONV_PAD, 128), lambda j: (0, j))],
        out_specs=[pl.BlockSpec((2, lp, 128), lambda j: (0, 0, j)),
                   pl.BlockSpec((CONV_PAD, 128), lambda j: (0, j)), pl.BlockSpec((8, 128), lambda j: (0, j))],
        scratch_shapes=[pltpu.VMEM((lp + CONV_PAD, 128), F32), pltpu.VMEM((lp + CONV_PAD, 128), F32),
                        pltpu.VMEM((CONV_PAD * 8, 128), F32)],
        name=name, compiler_params=_cp())(dyc, cacg, cacg, w)


def _mesh_pos():
    x, y, c = lax.axis_index("x"), lax.axis_index("y"), lax.axis_index("c")
    return x, y, c


def _peer(pos, r):
    x, y, c = pos
    px = (1 - x) if (r >> 2) & 1 else x
    py = (1 - y) if (r >> 1) & 1 else y
    pc = (1 - c) if r & 1 else c
    return (px, py, pc), 4 * px + 2 * py + pc


SIBLING = 1
OTHER_CHIPS = (2, 4, 6)
VIA_SIBLING = (3, 5, 7)


class _Job:
    def __init__(self, src, dst, scatter, src_layer=None, dst_layer=None):
        self.src, self.dst, self.scatter, self.src_layer, self.dst_layer = src, dst, scatter, src_layer, dst_layer

    def src_view(self, ins, idx):
        v = ins[self.src] if self.src_layer is None else ins[self.src].at[self.src_layer]
        return v.at[idx] if self.scatter else v

    def dst_view(self, outs, slot):
        v = outs[self.dst] if self.dst_layer is None else outs[self.dst].at[self.dst_layer]
        return v.at[slot]


def _remote(job, j, r, src, dst, to, send, recv):
    return pltpu.make_async_remote_copy(src_ref=src, dst_ref=dst, send_sem=send.at[j, r - 1], recv_sem=recv.at[j, r - 1],
                                        device_id=to, device_id_type=pl.DeviceIdType.MESH)


def _exchange_start(jobs, ins, outs, send, recv, loc):
    pos = _mesh_pos()
    me = 4 * pos[0] + 2 * pos[1] + pos[2]
    for j, job in enumerate(jobs):
        pltpu.make_async_copy(job.src_view(ins, me), job.dst_view(outs, me), loc.at[j]).start()
        for r in (range(1, N_DEV) if job.scatter else (SIBLING,) + OTHER_CHIPS):
            peer, peer_idx = _peer(pos, r)
            _remote(job, j, r, job.src_view(ins, peer_idx), job.dst_view(outs, me), peer, send, recv).start()


def _exchange_forward(jobs, ins, outs, send, recv, loc):
    pos = _mesh_pos()
    sibling, _ = _peer(pos, SIBLING)
    for j, job in enumerate(jobs):
        if job.scatter:
            continue
        for r in OTHER_CHIPS:
            peer, peer_idx = _peer(pos, r)
            slot = job.dst_view(outs, peer_idx)
            _remote(job, j, r, job.src_view(ins, peer_idx), slot, peer, send, recv).wait_recv()
            _remote(job, j, r ^ SIBLING, slot, slot, sibling, send, recv).start()


def _exchange_wait(jobs, ins, outs, send, recv, loc):
    pos = _mesh_pos()
    me = 4 * pos[0] + 2 * pos[1] + pos[2]
    for j, job in enumerate(jobs):
        for r in range(1, N_DEV):
            peer, peer_idx = _peer(pos, r)
            cp = _remote(job, j, r, job.src_view(ins, peer_idx), job.dst_view(outs, peer_idx), peer, send, recv)
            if job.scatter or r not in OTHER_CHIPS:
                cp.wait_recv()
            cp.wait_send()
        pltpu.make_async_copy(job.src_view(ins, me), job.dst_view(outs, me), loc.at[j]).wait()


def _exchange_sems(n_jobs):
    return [pltpu.SemaphoreType.DMA((n_jobs, N_DEV - 1)), pltpu.SemaphoreType.DMA((n_jobs, N_DEV - 1)),
            pltpu.SemaphoreType.DMA((n_jobs,))]


def _exchange(jobs, arrs, out_shape, name):
    n_in, n_out = len(arrs), len(out_shape)
    any_spec = pl.BlockSpec(memory_space=pl.ANY)

    def body(*refs):
        ins, outs, sems = refs[:n_in], refs[n_in:n_in + n_out], refs[n_in + n_out:]
        _exchange_start(jobs, ins, outs, *sems)
        _exchange_forward(jobs, ins, outs, *sems)
        _exchange_wait(jobs, ins, outs, *sems)

    return pl.pallas_call(
        body, out_shape=out_shape, in_specs=[any_spec] * n_in, out_specs=[any_spec] * n_out,
        scratch_shapes=_exchange_sems(len(jobs)),
        name=name, compiler_params=pltpu.CompilerParams(has_side_effects=True))(*arrs)


def _all_reduce_small(placed, cuts, rows, w, name):
    n_in = len(placed)

    def body(*refs):
        in_refs, o_refs = refs[:n_in], refs[n_in:n_in + len(cuts)]
        buf, send, recv = refs[n_in + len(cuts):]
        pos = _mesh_pos()
        me = 4 * pos[0] + 2 * pos[1] + pos[2]
        buf[me] = jnp.zeros((rows, w), F32)
        for ref, (arr, row0, col0, is_partial) in zip(in_refs, placed):
            val = ref[...].sum(axis=0, keepdims=True) if is_partial else ref[...]
            buf[me, row0:row0 + val.shape[0], col0:col0 + val.shape[1]] = val
        for r in range(1, N_DEV):
            peer, _ = _peer(pos, r)
            pltpu.make_async_remote_copy(src_ref=buf.at[me], dst_ref=buf.at[me], send_sem=send.at[r - 1],
                                         recv_sem=recv.at[r - 1], device_id=peer,
                                         device_id_type=pl.DeviceIdType.MESH).start()
        for r in range(1, N_DEV):
            peer, peer_idx = _peer(pos, r)
            cp = pltpu.make_async_remote_copy(src_ref=buf.at[me], dst_ref=buf.at[peer_idx], send_sem=send.at[r - 1],
                                              recv_sem=recv.at[r - 1], device_id=peer,
                                              device_id_type=pl.DeviceIdType.MESH)
            cp.wait_recv()
            cp.wait_send()
        acc = buf[0]
        for dev in range(1, N_DEV):
            acc = acc + buf[dev]
        for o_ref, (_, pieces) in zip(o_refs, cuts):
            for index, row0, nrows, col0, ncols in pieces:
                o_ref[index] = acc[row0:row0 + nrows, col0:col0 + ncols].reshape(o_ref.at[index].shape)

    vmem = pl.BlockSpec(memory_space=pltpu.VMEM)
    return pl.pallas_call(
        body, out_shape=[SDS(shape, F32) for shape, _ in cuts], in_specs=[vmem] * n_in, out_specs=[vmem] * len(cuts),
        scratch_shapes=[pltpu.VMEM((N_DEV, rows, w), F32), pltpu.SemaphoreType.DMA((N_DEV - 1,)),
                        pltpu.SemaphoreType.DMA((N_DEV - 1,))],
        name=name, compiler_params=pltpu.CompilerParams(has_side_effects=True))(*[p[0] for p in placed])


def _adamw_math(w, g, m, v):
    m = ADAM_B1 * m + (1.0 - ADAM_B1) * g
    v = ADAM_B2 * v + (1.0 - ADAM_B2) * (g * g)
    m_hat = m / (1.0 - ADAM_B1 ** ADAM_STEP)
    v_hat = v / (1.0 - ADAM_B2 ** ADAM_STEP)
    delta = -ADAM_LR * (m_hat / (jnp.sqrt(v_hat) + ADAM_EPS) + ADAM_WD * w)
    return delta, m, v


def _adamw_shard(parts, w, m, v, transposed, name):
    depth = len(parts)
    _, rr, cc = parts[0].shape
    if transposed:
        tc = _tile(cc, 256, 128)
        nt = cc // tc
        part_block, blk = (N_DEV, rr, tc), pl.BlockSpec((None, tc, rr), lambda l, i: (l, i, 0))
    else:
        tr = _tile(rr, 256, 8)
        nt = rr // tr
        part_block, blk = (N_DEV, tr, cc), pl.BlockSpec((None, tr, cc), lambda l, i: (l, i, 0))

    def body(*refs):
        p_refs = refs[:depth]
        w_ref, m_ref, v_ref, g_out, d_out, m_out, v_out = refs[depth:]
        for li in range(depth):
            @pl.when(pl.program_id(0) == li)
            def _(p_ref=p_refs[li]):
                g = p_ref[0].astype(F32)
                for dev in range(1, N_DEV):
                    g = g + p_ref[dev].astype(F32)
                if transposed:
                    g = g.T
                delta, mm, vv = _adamw_math(w_ref[...], g, m_ref[...], v_ref[...])
                g_out[...] = g
                d_out[...] = delta
                m_out[...] = mm
                v_out[...] = vv

    def part_spec(li):
        at = lambda l, i: jnp.where(l == li, i, jnp.where(l < li, 0, nt - 1))
        return pl.BlockSpec(part_block, (lambda l, i: (0, 0, at(l, i))) if transposed else (lambda l, i: (0, at(l, i), 0)))

    return pl.pallas_call(
        body, out_shape=[SDS(w.shape, F32)] * 4, grid=(depth, nt),
        in_specs=[part_spec(li) for li in range(depth)] + [blk, blk, blk],
        out_specs=[blk] * 4, name=name, compiler_params=_cp())(*parts, w, m, v)


def _adamw_small(gs, ws, ms, vs, name):
    n = len(gs)

    def body(*refs):
        g_refs, w_refs, m_refs, v_refs = (refs[k * n:(k + 1) * n] for k in range(4))
        outs = refs[4 * n:]
        for k in range(n):
            delta, mm, vv = _adamw_math(w_refs[k][...], g_refs[k][...], m_refs[k][...], v_refs[k][...])
            outs[k][...] = delta
            outs[n + k][...] = mm
            outs[2 * n + k][...] = vv

    res = pl.pallas_call(body, out_shape=[SDS(w.shape, F32) for w in ws] * 3, name=name,
                         compiler_params=_cp())(*gs, *ws, *ms, *vs)
    return res[:n], res[n:2 * n], res[2 * n:]


def _from_cols(t):
    return jnp.transpose(t, (1, 0, 2)).reshape(t.shape[1], N_DEV * t.shape[2])


def _swap(t):
    return jnp.swapaxes(t, -1, -2)


def kernel(x, meta_tokens, mix_norm_g, w_in, conv_dw_w, conv_dw_b, conv_ln_g, conv_ln_b, w_out, ffn_norm_g, w_gate, w_up, w_down, final_norm_g, loss_target, m_meta_tokens, m_mix_norm_g, m_w_in, m_conv_dw_w, m_conv_dw_b, m_conv_ln_g, m_conv_ln_b, m_w_out, m_ffn_norm_g, m_w_gate, m_w_up, m_w_down, m_final_norm_g, v_meta_tokens, v_mix_norm_g, v_w_in, v_conv_dw_w, v_conv_dw_b, v_conv_ln_g, v_conv_ln_b, v_w_out, v_ffn_norm_g, v_w_gate, v_w_up, v_w_down, v_final_norm_g):
    depth, d, in_shard = w_in.shape
    seq = x.shape[1]
    sb = N_HEADS * HEAD_DIM
    cc = conv_dw_w.shape[2] * N_DEV
    ff = w_gate.shape[2] * N_DEV
    assert in_shard * N_DEV == 3 * sb + 2 * cc and x.shape[0] == 1
    lr = N_META + seq
    lp = -(-lr // ABLK) * ABLK
    me = 4 * lax.axis_index("x") + 2 * lax.axis_index("y") + lax.axis_index("c")

    big_names = ("w_in", "w_out", "w_gate", "w_up", "w_down")
    transposed = {"w_in": True, "w_out": False, "w_gate": True, "w_up": True, "w_down": False}
    shard = dict(w_in=_swap(w_in).astype(BF16), w_out=w_out.astype(BF16), w_gate=_swap(w_gate).astype(BF16),
                 w_up=_swap(w_up).astype(BF16), w_down=w_down.astype(BF16))

    def gather_of(keys):
        names = sorted({n for n, _ in keys}, key=big_names.index)
        jobs = [_Job(names.index(n), j, False, src_layer=i) for j, (n, i) in enumerate(keys)]
        return jobs, [shard[n] for n in names], [SDS((N_DEV,) + shard[n].shape[1:], BF16) for n, _ in keys]

    first_keys = [("w_in", 0)]

    def riding(keys):
        return _Rider(*gather_of(keys)) if keys else None

    def receive(keys, arrays):
        for (n, li), t in zip(keys, arrays):
            wl[li][n] = t.reshape(-1, d)

    jobs, srcs, out_shape = gather_of(first_keys)
    for extra in (meta_tokens, conv_dw_w):
        jobs.append(_Job(len(srcs), len(out_shape), False))
        srcs.append(extra)
        out_shape.append(SDS((N_DEV,) + extra.shape, F32))
    gathered = _exchange(jobs, srcs, out_shape, "gather_first")
    wl = [dict() for _ in range(depth)]
    receive(first_keys, gathered)
    meta_full = _from_cols(gathered[-2])
    taps = jnp.transpose(gathered[-1], (1, 2, 0, 3)).reshape(depth, CONV_K, cc)
    taps = jnp.pad(taps, ((0, 0), (0, CONV_PAD - CONV_K), (0, 0)))
    tri_fwd, tri_after, tri_before = _tri_consts()

    h = jnp.concatenate([meta_full, x[0], jnp.zeros((lp - lr, d), F32)], axis=0)
    saved = []
    for i in range(depth):
        p = wl[i]
        sv = dict(h_in=h)
        qkv, hn = _rms_mm(h, mix_norm_g[i:i + 1], p["w_in"], 0, 3 * sb, BF16, f"proj_qkv_{i}", True)
        keys = [("w_out", 0)] if i == 0 else []
        cacg, *arrived = _rms_mm(h, mix_norm_g[i:i + 1], p["w_in"], 3 * sb, 2 * cc, F32, f"proj_conv_{i}", False,
                                 riding(keys))
        receive(keys, arrived)
        keys = [("w_gate", i), ("w_up", i)]
        attn, rsave, *arrived = _attn_fwd(qkv, tri_fwd, f"attn_fwd_{i}", riding(keys))
        receive(keys, arrived)
        keys = [("w_down", 0)] if i == 0 else []
        yc, *arrived = _conv_fwd_dw(cacg, taps[i], conv_dw_b[i:i + 1], f"conv_fwd_dw_{i}", riding(keys))
        receive(keys, arrived)
        conv = _conv_fwd_ln(yc, conv_ln_g[i:i + 1], conv_ln_b[i:i + 1], f"conv_fwd_ln_{i}")
        h = _mix_out(attn, conv, p["w_out"], h, f"mix_out_{i}")
        sv.update(qkv=qkv, hn=hn, cacg=cacg, rsave=rsave, yc=yc, attn=attn, conv=conv, h_mid=h)
        keys = [("w_in", i + 1), ("w_out", i + 1), ("w_down", i + 1)] if i + 1 < depth else []
        h, hn2, act, *arrived = _ffn_fwd(h, ffn_norm_g[i:i + 1], p["w_gate"], p["w_up"], p["w_down"], f"ffn_fwd_{i}",
                                         riding(keys))
        receive(keys, arrived)
        sv.update(hn2=hn2, act=act)
        saved.append(sv)

    tpad = jnp.pad(loss_target[0], ((N_META, lp - lr), (0, 0)))
    dh, dg_final, loss_part = _loss_head(h, tpad, final_norm_g.reshape(1, d), seq, "loss_head")
    loss = lax.psum(loss_part[0, 0], MESH_AXES)

    parts = {}

    def sending(items):
        srcs = [t.reshape(N_DEV, t.shape[0] // N_DEV, d) for _, t in items]
        return _Rider([_Job(j, j, True) for j in range(len(items))], srcs, [SDS(t.shape, BF16) for t in srcs])

    def arrive(items, arrays):
        parts.update({key: t for (key, _), t in zip(items, arrays)})

    grads = [None] * depth
    from_above = []
    for i in reversed(range(depth)):
        p, sv = wl[i], saved[i]
        d_g, d_u, *got = _ffn_bwd_act(dh, sv["hn2"], p["w_gate"], p["w_up"], p["w_down"], f"ffn_bwd_act_{i}",
                                      sending(from_above))
        arrive(from_above, got)
        gw_down = _mm_tn(sv["act"], dh, f"grad_w_down_{i}")
        gw_gate = _mm_tn(d_g, sv["hn2"], f"grad_w_gate_{i}")
        gw_up = _mm_tn(d_u, sv["hn2"], f"grad_w_up_{i}")
        items = [(("w_down", i), gw_down)]
        dh, dg_ffn, *got = _ffn_bwd_in(d_g, d_u, p["w_gate"], p["w_up"], sv["h_mid"], ffn_norm_g[i:i + 1], dh,
                                       f"ffn_bwd_in_{i}", sending(items))
        arrive(items, got)
        gw_out = jnp.concatenate([_mm_tn(sv["attn"], dh, f"grad_w_out_attn_{i}"),
                                  _mm_tn(sv["conv"], dh, f"grad_w_out_conv_{i}")], axis=0)
        d_attn = _mm_nt(dh, p["w_out"], 0, sb, BF16, f"mix_bwd_attn_{i}")
        d_conv = _mm_nt(dh, p["w_out"], sb, cc, F32, f"mix_bwd_conv_{i}")
        items = [(("w_gate", i), gw_gate), (("w_up", i), gw_up)]
        dqkv, *got = _attn_bwd(sv["qkv"], d_attn, sv["rsave"], tri_after, tri_before, f"attn_bwd_{i}", sending(items))
        arrive(items, got)
        dyc, dg_ln, db_ln = _conv_bwd_ln(sv["yc"], d_conv, conv_ln_g[i:i + 1], conv_ln_b[i:i + 1], f"conv_bwd_ln_{i}")
        dcc, g_taps_i, db_conv = _conv_bwd_dw(dyc, sv["cacg"], taps[i], f"conv_bwd_dw_{i}")
        items = [(("w_out", i), gw_out)] if i == 0 else []
        gw_in, *got = _grad_w_in_t(dqkv, dcc, sv["hn"], f"grad_w_in_{i}", sending(items))
        arrive(items, got)
        from_above = [(("w_in", i), gw_in)] + ([(("w_out", i), gw_out)] if i > 0 else [])
        items = from_above if i == 0 else []
        dh, dg_mix, *got = _mix_bwd_in(dqkv, dcc, p["w_in"], sv["h_in"], mix_norm_g[i:i + 1], dh, f"mix_bwd_in_{i}",
                                       sending(items))
        arrive(items, got)
        grads[i] = dict(taps=g_taps_i, dg_mix=dg_mix, dg_ffn=dg_ffn, dg_ln=dg_ln, db_ln=db_ln, db_conv=db_conv)
    grad_x = dh[N_META:lr][None]

    per = d // depth
    assert depth * cc <= d and d % depth == 0 and per % 128 == 0
    placed, cuts, row = [], [], 0
    for key in ("dg_mix", "dg_ffn"):
        placed += [(grads[i][key], row + i, 0, True) for i in range(depth)]
        cuts.append(((depth, d), [(slice(0, depth), row, depth, 0, d)]))
        row += depth
    placed.append((dg_final, row, 0, True))
    cuts.append(((1, d), [(slice(0, 1), row, 1, 0, d)]))
    row += 1
    for key in ("db_conv", "dg_ln", "db_ln"):
        placed += [(grads[i][key], row, i * per, True) for i in range(depth)]
        cuts.append(((depth, cc), [(slice(i, i + 1), row, 1, i * per, cc) for i in range(depth)]))
        row += 1
    placed += [(grads[i]["taps"], row, i * per, False) for i in range(depth)]
    cuts.append(((depth, CONV_PAD, cc), [(i, row, CONV_PAD, i * per, cc) for i in range(depth)]))
    row += CONV_PAD
    placed.append((dh[:N_META], row, 0, False))
    cuts.append(((N_META, d), [(slice(0, N_META), row, N_META, 0, d)]))
    row += N_META
    g_mix, g_ffn, g_final, g_cb, g_lg, g_lb, g_taps_full, g_meta_full = _all_reduce_small(
        placed, cuts, -(-row // 8) * 8, d, "reduce_small")
    csh = cc // N_DEV
    g_taps_own = lax.dynamic_slice_in_dim(g_taps_full[:, :CONV_K], me * csh, csh, axis=2)
    msh = d // N_DEV
    g_meta_own = lax.dynamic_slice_in_dim(g_meta_full, me * msh, msh, axis=1)

    row1 = lambda t: t.reshape(1, d)
    small_g = [g_meta_own, g_mix, g_taps_own, g_cb, g_lg, g_lb, g_ffn, g_final]
    small_w = [meta_tokens, mix_norm_g, conv_dw_w, conv_dw_b, conv_ln_g, conv_ln_b, ffn_norm_g, row1(final_norm_g)]
    small_m = [m_meta_tokens, m_mix_norm_g, m_conv_dw_w, m_conv_dw_b, m_conv_ln_g, m_conv_ln_b, m_ffn_norm_g,
               row1(m_final_norm_g)]
    small_v = [v_meta_tokens, v_mix_norm_g, v_conv_dw_w, v_conv_dw_b, v_conv_ln_g, v_conv_ln_b, v_ffn_norm_g,
               row1(v_final_norm_g)]
    s_delta, s_m, s_v = _adamw_small(small_g, small_w, small_m, small_v, "adamw_small")
    unrow = lambda ts: list(ts[:-1]) + [ts[-1].reshape(d)]
    small_g, s_delta, s_m, s_v = unrow(small_g), unrow(s_delta), unrow(s_m), unrow(s_v)

    big = []
    for n, w, m, v in zip(big_names, (w_in, w_out, w_gate, w_up, w_down), (m_w_in, m_w_out, m_w_gate, m_w_up, m_w_down),
                          (v_w_in, v_w_out, v_w_gate, v_w_up, v_w_down)):
        big.append(_adamw_shard([parts[(n, i)] for i in range(depth)], w, m, v, transposed[n], f"adamw_{n}"))
    b_in, b_out, b_gate, b_up, b_down = big

    def ordered(k, smalls):
        s_meta, s_mix, s_taps, s_cb, s_lg, s_lb, s_ffn, s_final = smalls
        return [s_meta, s_mix, b_in[k], s_taps, s_cb, s_lg, s_lb, b_out[k], s_ffn, b_gate[k], b_up[k], b_down[k], s_final]

    return (loss, grad_x, *ordered(0, small_g), *ordered(1, s_delta), *ordered(2, s_m), *ordered(3, s_v))
```

```python
import functools
import math

import jax
import jax.numpy as jnp
from jax import lax
from jax.experimental import pallas as pl
from jax.experimental.pallas import tpu as pltpu

F32 = jnp.float32
BF16 = jnp.bfloat16
SDS = jax.ShapeDtypeStruct

N_META = 16
N_HEADS = 8
HEAD_DIM = 64
CONV_K = 31
CONV_PAD = 32
ABLK = 128
ATT_GROUP = 4
ATT_SPLIT = 2
ATT_TOP = 2
EXP_ZERO_AT = 104.0
GONE = 1e30
MASKED = -1e30
FF_CHUNK = 256
FF_BLOCK = 1408
FF_ROWS = 528
EPS = 1e-6
N_DEV = 8
MESH_AXES = ("x", "y", "c")
ADAM_LR = 0.001
ADAM_B1 = 0.9
ADAM_B2 = 0.999
ADAM_EPS = 1e-08
ADAM_WD = 0.01
ADAM_STEP = 10
MIB = 1 << 20
VMEM_LIMIT_MIB = 48
MM_ROWS = 1056
EW_ROWS = 528


def _cp():
    return pltpu.CompilerParams(vmem_limit_bytes=VMEM_LIMIT_MIB * MIB)


def _tile(n, cap, mult):
    best = None
    for t in range(mult, min(n, cap) + 1, mult):
        if n % t == 0:
            best = t
    assert best is not None, (n, cap, mult)
    return best


def _sum8(v):
    r, c = v.shape
    return v.reshape(r // 8, 8, c).sum(axis=0)


def _sigmoid(x):
    return 1.0 / (1.0 + jnp.exp(-x))


def _ff_block(f):
    fb = _tile(f, FF_BLOCK, 128)
    return fb, [(s0, min(FF_CHUNK, fb - s0)) for s0 in range(0, fb, FF_CHUNK)]


NT = (((1,), (1,)), ((), ()))
TN = (((0,), (0,)), ((), ()))


def _rms_mm(h, g, wt, w_row0, n_cols, out_dtype, name, hn_out, rider=None):
    lp, d = h.shape
    tm = _tile(lp, MM_ROWS, 16)
    tn = _tile(math.gcd(n_cols, w_row0), 768, 128)
    off = w_row0 // tn
    ni, nj = lp // tm, n_cols // tn
    rider = rider or _Rider()

    def body(h_ref, g_ref, w_ref, *rest):
        outs, (hn_s,) = rider.split(rest, 2 if hn_out else 1, 1)
        o_ref = outs[0]
        hn_ref = outs[1] if hn_out else None
        rider.start((pl.program_id(0) == 0) & (pl.program_id(1) == 0))
        if ni > 1:
            rider.forward((pl.program_id(0) == ni - 1) & (pl.program_id(1) == 0))

        @pl.when(pl.program_id(1) == 0)
        def _():
            x = h_ref[...]
            r = lax.rsqrt(jnp.mean(x * x, axis=-1, keepdims=True) + EPS)
            hn = ((x * r) * g_ref[...]).astype(BF16)
            hn_s[...] = hn
            if hn_out:
                hn_ref[...] = hn

        o_ref[...] = lax.dot_general(hn_s[...], w_ref[...], NT, preferred_element_type=F32).astype(out_dtype)
        rider.wait((pl.program_id(0) == ni - 1) & (pl.program_id(1) == nj - 1))

    out_shape = [SDS((lp, n_cols), out_dtype)]
    out_specs = [pl.BlockSpec((tm, tn), lambda i, j: (i, j))]
    if hn_out:
        out_shape.append(SDS((lp, d), BF16))
        out_specs.append(pl.BlockSpec((tm, d), lambda i, j: (i, 0)))
    return pl.pallas_call(
        body, out_shape=out_shape + rider.out_shape, grid=(ni, nj),
        in_specs=[pl.BlockSpec((tm, d), lambda i, j: (i, 0)),
                  pl.BlockSpec((1, d), lambda i, j: (0, 0)),
                  pl.BlockSpec((tn, d), lambda i, j: (j + off, 0))] + rider.in_specs(),
        out_specs=out_specs + rider.out_specs(), scratch_shapes=[pltpu.VMEM((tm, d), BF16)] + rider.scratch(),
        name=name, compiler_params=_cp())(h, g, wt, *rider.srcs)


def _mm_nt(a, b, b_row0, n_out, out_dtype, name):
    m, k = a.shape
    tm = _tile(m, MM_ROWS, 16)
    tn = _tile(math.gcd(n_out, b_row0), 512, 128)
    off = b_row0 // tn

    def body(a_ref, b_ref, o_ref):
        o_ref[...] = lax.dot_general(a_ref[...].astype(BF16), b_ref[...], NT,
                                     preferred_element_type=F32).astype(out_dtype)

    return pl.pallas_call(
        body, out_shape=SDS((m, n_out), out_dtype), grid=(m // tm, n_out // tn),
        in_specs=[pl.BlockSpec((tm, k), lambda i, j: (i, 0)), pl.BlockSpec((tn, k), lambda i, j: (j + off, 0))],
        out_specs=pl.BlockSpec((tm, tn), lambda i, j: (i, j)), name=name, compiler_params=_cp())(a, b)


def _mix_out(attn, conv, w, res, name):
    m, ka = attn.shape
    kc = conv.shape[1]
    n = w.shape[1]
    assert ka == kc
    tm = _tile(m, MM_ROWS, 16)
    tn = _tile(n, 512, 128)

    def body(a_ref, c_ref, wa_ref, wc_ref, r_ref, o_ref):
        o_ref[...] = (r_ref[...] + jnp.dot(a_ref[...], wa_ref[...], preferred_element_type=F32)
                      + jnp.dot(c_ref[...], wc_ref[...], preferred_element_type=F32))

    return pl.pallas_call(
        body, out_shape=SDS((m, n), F32), grid=(m // tm, n // tn),
        in_specs=[pl.BlockSpec((tm, ka), lambda i, j: (i, 0)), pl.BlockSpec((tm, kc), lambda i, j: (i, 0)),
                  pl.BlockSpec((ka, tn), lambda i, j: (0, j)), pl.BlockSpec((kc, tn), lambda i, j: (1, j)),
                  pl.BlockSpec((tm, tn), lambda i, j: (i, j))],
        out_specs=pl.BlockSpec((tm, tn), lambda i, j: (i, j)), name=name, compiler_params=_cp())(attn, conv, w, w, res)


def _mm_tn(a, b, name):
    l, m = a.shape
    n = b.shape[1]
    tm = _tile(m, 1408, 128)
    tn = _tile(n, 1024, 128)
    tl = _tile(l, 1408, 128)
    nl = l // tl

    def body(a_ref, b_ref, o_ref, acc):
        @pl.when(pl.program_id(2) == 0)
        def _():
            acc[...] = jnp.zeros_like(acc)

        acc[...] += lax.dot_general(a_ref[...].astype(BF16), b_ref[...].astype(BF16), TN, preferred_element_type=F32)

        @pl.when(pl.program_id(2) == nl - 1)
        def _():
            o_ref[...] = acc[...].astype(BF16)

    return pl.pallas_call(
        body, out_shape=SDS((m, n), BF16), grid=(m // tm, n // tn, nl),
        in_specs=[pl.BlockSpec((tl, tm), lambda i, j, s: (s, i)),
                  pl.BlockSpec((tl, tn), lambda i, j, s: (s, j))],
        out_specs=pl.BlockSpec((tm, tn), lambda i, j, s: (i, j)),
        scratch_shapes=[pltpu.VMEM((tm, tn), F32)], name=name, compiler_params=_cp())(a, b)


def _grad_w_in_t(dqkv, dcc, hn, name, rider=None):
    nq, l, w = dqkv.shape
    nc = dcc.shape[0]
    d = hn.shape[1]
    assert dcc.shape[2] == w
    tn = _tile(d, 1024, 128)
    tl = _tile(l, 1408, 128)
    nl = l // tl
    nj = d // tn
    rider = rider or _Rider()

    def body(q_ref, c_ref, b_ref, *rest):
        (o_ref,), (acc,) = rider.split(rest, 1, 1)
        p, jj, s = pl.program_id(0), pl.program_id(1), pl.program_id(2)
        rider.start((p == 0) & (jj == 0) & (s == 0))

        @pl.when(s == 0)
        def _():
            acc[...] = jnp.zeros_like(acc)

        @pl.when(p < nq)
        def _():
            acc[...] += lax.dot_general(q_ref[...], b_ref[...], TN, preferred_element_type=F32)

        @pl.when(p >= nq)
        def _():
            acc[...] += lax.dot_general(c_ref[...], b_ref[...], TN, preferred_element_type=F32)

        @pl.when(s == nl - 1)
        def _():
            o_ref[...] = acc[...].astype(BF16)

        rider.wait((p == nq + nc - 1) & (jj == nj - 1) & (s == nl - 1))

    out, *arrived = pl.pallas_call(
        body, out_shape=[SDS((nq + nc, w, d), BF16)] + rider.out_shape, grid=(nq + nc, nj, nl),
        in_specs=[pl.BlockSpec((None, tl, w), lambda p, j, s: (jnp.minimum(p, nq - 1), s, 0)),
                  pl.BlockSpec((None, tl, w), lambda p, j, s: (jnp.maximum(p - nq, 0), s, 0)),
                  pl.BlockSpec((tl, tn), lambda p, j, s: (s, j))] + rider.in_specs(),
        out_specs=[pl.BlockSpec((None, w, tn), lambda p, j, s: (p, 0, j))] + rider.out_specs(),
        scratch_shapes=[pltpu.VMEM((w, tn), F32)] + rider.scratch(), name=name,
        compiler_params=_cp())(dqkv, dcc, hn, *rider.srcs)
    return [out.reshape((nq + nc) * w, d)] + arrived


def _rms_bwd_tail(acc, h_ref, g_ref, dres_ref, o_ref, dg_ref):
    x = h_ref[...]
    r = lax.rsqrt(jnp.mean(x * x, axis=-1, keepdims=True) + EPS)
    xr = x * r
    dyv = acc[...]
    gy = dyv * g_ref[...]
    o_ref[...] = dres_ref[...] + r * (gy - xr * jnp.mean(gy * xr, axis=-1, keepdims=True))
    dg_ref[...] += _sum8(dyv * xr)


def _mix_bwd_in(dqkv, dcc, wt, h, g, dres, name, rider=None):
    nq, lp, w = dqkv.shape
    nc = dcc.shape[0]
    d = h.shape[1]
    tm = _tile(lp, FF_ROWS, 16)
    ni = lp // tm
    rider = rider or _Rider()

    def body(q_ref, c_ref, w_ref, h_ref, g_ref, dres_ref, *rest):
        (o_ref, dg_ref), (acc,) = rider.split(rest, 2, 1)
        i = pl.program_id(0)
        rider.start(i == 0)

        @pl.when(i == 0)
        def _():
            dg_ref[...] = jnp.zeros_like(dg_ref)

        pieces = [q_ref[p] for p in range(nq)] + [c_ref[p] for p in range(nc)]
        total = None
        for p, piece in enumerate(pieces):
            part = jnp.dot(piece, w_ref[p * w:(p + 1) * w, :], preferred_element_type=F32)
            total = part if total is None else total + part
        acc[...] = total
        _rms_bwd_tail(acc, h_ref, g_ref, dres_ref, o_ref, dg_ref)
        rider.wait(i == ni - 1)

    return pl.pallas_call(
        body, out_shape=[SDS((lp, d), F32), SDS((8, d), F32)] + rider.out_shape, grid=(ni,),
        in_specs=[pl.BlockSpec((nq, tm, w), lambda i: (0, i, 0)),
                  pl.BlockSpec((nc, tm, w), lambda i: (0, i, 0)),
                  pl.BlockSpec(((nq + nc) * w, d), lambda i: (0, 0)),
                  pl.BlockSpec((tm, d), lambda i: (i, 0)),
                  pl.BlockSpec((1, d), lambda i: (0, 0)),
                  pl.BlockSpec((tm, d), lambda i: (i, 0))] + rider.in_specs(),
        out_specs=[pl.BlockSpec((tm, d), lambda i: (i, 0)), pl.BlockSpec((8, d), lambda i: (0, 0))]
        + rider.out_specs(),
        scratch_shapes=[pltpu.VMEM((tm, d), F32)] + rider.scratch(), name=name,
        compiler_params=_cp())(dqkv, dcc, wt, h, g, dres, *rider.srcs)


def _ffn_bwd_in(d_g, d_u, wg_t, wu_t, h, g, dres, name, rider=None):
    lp, f = d_g.shape
    d = h.shape[1]
    tm = _tile(lp, FF_ROWS, 16)
    tk = _tile(f, FF_BLOCK, 128)
    nk = f // tk
    ni = lp // tm
    rider = rider or _Rider()

    def body(dg_in, du_in, wg_ref, wu_ref, h_ref, g_ref, dres_ref, *rest):
        (o_ref, dg_ref), (acc,) = rider.split(rest, 2, 1)
        i, kk = pl.program_id(0), pl.program_id(1)
        rider.start((i == 0) & (kk == 0))

        @pl.when(kk == 0)
        def _():
            acc[...] = jnp.zeros_like(acc)

        @pl.when((kk == 0) & (i == 0))
        def _():
            dg_ref[...] = jnp.zeros_like(dg_ref)

        acc[...] += (jnp.dot(dg_in[...], wg_ref[...], preferred_element_type=F32)
                     + jnp.dot(du_in[...], wu_ref[...], preferred_element_type=F32))

        @pl.when(kk == nk - 1)
        def _():
            _rms_bwd_tail(acc, h_ref, g_ref, dres_ref, o_ref, dg_ref)

        rider.wait((i == ni - 1) & (kk == nk - 1))

    return pl.pallas_call(
        body, out_shape=[SDS((lp, d), F32), SDS((8, d), F32)] + rider.out_shape, grid=(ni, nk),
        in_specs=[pl.BlockSpec((tm, tk), lambda i, j: (i, j)), pl.BlockSpec((tm, tk), lambda i, j: (i, j)),
                  pl.BlockSpec((tk, d), lambda i, j: (j, 0)), pl.BlockSpec((tk, d), lambda i, j: (j, 0)),
                  pl.BlockSpec((tm, d), lambda i, j: (i, 0)),
                  pl.BlockSpec((1, d), lambda i, j: (0, 0)),
                  pl.BlockSpec((tm, d), lambda i, j: (i, 0))] + rider.in_specs(),
        out_specs=[pl.BlockSpec((tm, d), lambda i, j: (i, 0)), pl.BlockSpec((8, d), lambda i, j: (0, 0))]
        + rider.out_specs(),
        scratch_shapes=[pltpu.VMEM((tm, d), F32)] + rider.scratch(), name=name,
        compiler_params=_cp())(d_g, d_u, wg_t, wu_t, h, g, dres, *rider.srcs)


def _ffn_fwd(h, g, wg_t, wu_t, w_d, name, rider=None):
    lp, d = h.shape
    f = w_d.shape[0]
    tm = _tile(lp, FF_ROWS, 16)
    fb, subs = _ff_block(f)
    nc = f // fb
    ni = lp // tm
    rider = rider or _Rider()

    def body(h_ref, g_ref, wg_ref, wu_ref, wd_ref, *rest):
        (o_ref, hn_ref, act_ref), (hn_s, acc) = rider.split(rest, 3, 2)
        c = pl.program_id(1)
        rider.start((pl.program_id(0) == 0) & (c == 0))
        if ni > 1:
            rider.forward((pl.program_id(0) == ni - 1) & (c == 0))

        @pl.when(c == 0)
        def _():
            x = h_ref[...]
            r = lax.rsqrt(jnp.mean(x * x, axis=-1, keepdims=True) + EPS)
            hn = ((x * r) * g_ref[...]).astype(BF16)
            hn_s[...] = hn
            hn_ref[...] = hn
            acc[...] = jnp.zeros_like(acc)

        total = None
        for s0, sw in subs:
            gg = lax.dot_general(hn_s[...], wg_ref[s0:s0 + sw, :], NT, preferred_element_type=F32)
            uu = lax.dot_general(hn_s[...], wu_ref[s0:s0 + sw, :], NT, preferred_element_type=F32)
            act = ((gg * _sigmoid(gg)) * uu).astype(BF16)
            act_ref[:, s0:s0 + sw] = act
            part = jnp.dot(act, wd_ref[s0:s0 + sw, :], preferred_element_type=F32)
            total = part if total is None else total + part
        acc[...] += total

        @pl.when(c == nc - 1)
        def _():
            o_ref[...] = h_ref[...] + acc[...]

        rider.wait((pl.program_id(0) == ni - 1) & (c == nc - 1))

    chunk = pl.BlockSpec((fb, d), lambda i, j: (j, 0))
    return pl.pallas_call(
        body, out_shape=[SDS((lp, d), F32), SDS((lp, d), BF16), SDS((lp, f), BF16)] + rider.out_shape, grid=(ni, nc),
        in_specs=[pl.BlockSpec((tm, d), lambda i, j: (i, 0)), pl.BlockSpec((1, d), lambda i, j: (0, 0)),
                  chunk, chunk, chunk] + rider.in_specs(),
        out_specs=[pl.BlockSpec((tm, d), lambda i, j: (i, 0)),
                   pl.BlockSpec((tm, d), lambda i, j: (i, 0)),
                   pl.BlockSpec((tm, fb), lambda i, j: (i, j))] + rider.out_specs(),
        scratch_shapes=[pltpu.VMEM((tm, d), BF16), pltpu.VMEM((tm, d), F32)] + rider.scratch(),
        name=name, compiler_params=_cp())(h, g, wg_t, wu_t, w_d, *rider.srcs)


def _ffn_bwd_act(dh, hn, wg_t, wu_t, w_d, name, rider=None):
    lp, d = dh.shape
    f = w_d.shape[0]
    tm = _tile(lp, FF_ROWS, 16)
    fb, subs = _ff_block(f)
    nc = f // fb
    ni = lp // tm
    rider = rider or _Rider()

    def body(dh_ref, hn_ref, wg_ref, wu_ref, wd_ref, *rest):
        (dg_out, du_out), (dh_s,) = rider.split(rest, 2, 1)
        rider.start((pl.program_id(0) == 0) & (pl.program_id(1) == 0))

        @pl.when(pl.program_id(1) == 0)
        def _():
            dh_s[...] = dh_ref[...].astype(BF16)

        for s0, sw in subs:
            gg = lax.dot_general(hn_ref[...], wg_ref[s0:s0 + sw, :], NT, preferred_element_type=F32)
            uu = lax.dot_general(hn_ref[...], wu_ref[s0:s0 + sw, :], NT, preferred_element_type=F32)
            dact = lax.dot_general(dh_s[...], wd_ref[s0:s0 + sw, :], NT, preferred_element_type=F32)
            s = _sigmoid(gg)
            dg_out[:, s0:s0 + sw] = (dact * uu * (s * (1.0 + gg * (1.0 - s)))).astype(BF16)
            du_out[:, s0:s0 + sw] = (dact * (gg * s)).astype(BF16)
        rider.wait((pl.program_id(0) == ni - 1) & (pl.program_id(1) == nc - 1))

    chunk = pl.BlockSpec((fb, d), lambda i, j: (j, 0))
    out = pl.BlockSpec((tm, fb), lambda i, j: (i, j))
    return pl.pallas_call(
        body, out_shape=[SDS((lp, f), BF16), SDS((lp, f), BF16)] + rider.out_shape, grid=(ni, nc),
        in_specs=[pl.BlockSpec((tm, d), lambda i, j: (i, 0)), pl.BlockSpec((tm, d), lambda i, j: (i, 0)),
                  chunk, chunk, chunk] + rider.in_specs(),
        out_specs=[out, out] + rider.out_specs(), scratch_shapes=[pltpu.VMEM((tm, d), BF16)] + rider.scratch(),
        name=name, compiler_params=_cp())(dh, hn, wg_t, wu_t, w_d, *rider.srcs)


def _loss_head(h, tpad, g, n_real, name):
    lp, d = h.shape
    tm = _tile(lp, EW_ROWS, 16)

    def body(h_ref, t_ref, g_ref, dh_ref, dg_ref, loss_ref):
        i = pl.program_id(0)

        @pl.when(i == 0)
        def _():
            dg_ref[...] = jnp.zeros_like(dg_ref)
            loss_ref[...] = jnp.zeros_like(loss_ref)

        x = h_ref[...]
        r = lax.rsqrt(jnp.mean(x * x, axis=-1, keepdims=True) + EPS)
        xr = x * r
        y = xr * g_ref[...]
        row = i * tm + lax.broadcasted_iota(jnp.int32, (tm, d), 0)
        valid = (row >= N_META) & (row < N_META + n_real)
        diff = jnp.where(valid, y - t_ref[...], 0.0)
        loss_ref[...] += jnp.sum(diff * diff) * (0.5 / d)
        dy = diff * (1.0 / d)
        gy = dy * g_ref[...]
        dh_ref[...] = r * (gy - xr * jnp.mean(gy * xr, axis=-1, keepdims=True))
        dg_ref[...] += _sum8(dy * xr)

    return pl.pallas_call(
        body, out_shape=[SDS((lp, d), F32), SDS((8, d), F32), SDS((8, 128), F32)], grid=(lp // tm,),
        in_specs=[pl.BlockSpec((tm, d), lambda i: (i, 0)),
                  pl.BlockSpec((tm, d), lambda i: (i, 0)),
                  pl.BlockSpec((1, d), lambda i: (0, 0))],
        out_specs=[pl.BlockSpec((tm, d), lambda i: (i, 0)),
                   pl.BlockSpec((8, d), lambda i: (0, 0)),
                   pl.BlockSpec((8, 128), lambda i: (0, 0))],
        name=name, compiler_params=_cp())(h, tpad, g)


def _tri_consts():
    j = lax.broadcasted_iota(jnp.int32, (ABLK, ABLK), 0)
    s = lax.broadcasted_iota(jnp.int32, (ABLK, ABLK), 1)
    after = (j >= s).astype(BF16)
    before = (j < s).astype(BF16)
    ones = jnp.ones((ABLK, ABLK), BF16)
    two = lambda t: jnp.concatenate([t, t], axis=0)
    return (two(jnp.concatenate([after, ones], axis=1)),
            two(after),
            two(jnp.concatenate([before, ones], axis=1)))


def _softplus(z):
    neg_abs = lax.bitcast_convert_type(lax.bitcast_convert_type(z, jnp.uint32) | jnp.uint32(0x80000000), F32)
    return jnp.log(1.0 + jnp.exp(neg_abs)) + jnp.maximum(z, 0.0)


def _split_hi_lo(m):
    hi = m.astype(BF16)
    lo = (m - hi.astype(F32)).astype(BF16)
    return jnp.concatenate([hi, lo], axis=1)


def _head_halves(t2, in_a):
    zero = jnp.zeros_like(t2)
    return jnp.concatenate([jnp.where(in_a, t2, zero), jnp.where(in_a, zero, t2)], axis=0)


def _stack_blocks(t, nblk, in_a):
    return jnp.concatenate([_head_halves(t[u * ABLK:(u + 1) * ABLK], in_a) for u in range(nblk)], axis=0)


def _interleave(*gens):
    alive = list(gens)
    while alive:
        for g in list(alive):
            if next(g, alive) is alive:
                alive.remove(g)


def _attn_scale():
    scale = 1.0 / math.sqrt(HEAD_DIM)
    assert math.frexp(scale)[0] == 0.5, "a power of two, so that scaling q in bf16 is exact"
    return scale


def _pow2_below(n):
    assert n & (n - 1) == 0
    return [p for p in (64, 32, 16, 8, 4, 2, 1) if p < n]


class _Rider:
    def __init__(self, jobs=(), srcs=(), out_shape=()):
        self.jobs, self.srcs, self.out_shape = list(jobs), list(srcs), list(out_shape)
        self.any = [pl.BlockSpec(memory_space=pl.ANY)]

    def split(self, rest, n_out, n_scratch):
        ni, no = len(self.srcs), len(self.out_shape)
        self.ins, outs = rest[:ni], rest[ni:ni + n_out]
        self.outs = rest[ni + n_out:ni + n_out + no]
        scratch = rest[ni + n_out + no:ni + n_out + no + n_scratch]
        self.sems = rest[ni + n_out + no + n_scratch:]
        return outs, scratch

    def start(self, first):
        self.forwarded = False
        if self.jobs:
            @pl.when(first)
            def _():
                _exchange_start(self.jobs, self.ins, self.outs, *self.sems)

    def forward(self, late):
        self.forwarded = True
        if self.jobs:
            @pl.when(late)
            def _():
                _exchange_forward(self.jobs, self.ins, self.outs, *self.sems)

    def wait(self, last):
        if self.jobs:
            @pl.when(last)
            def _():
                if not self.forwarded:
                    _exchange_forward(self.jobs, self.ins, self.outs, *self.sems)
                _exchange_wait(self.jobs, self.ins, self.outs, *self.sems)

    def in_specs(self):
        return self.any * len(self.srcs)

    def out_specs(self):
        return self.any * len(self.out_shape)

    def scratch(self):
        return _exchange_sems(len(self.jobs)) if self.jobs else []


def _attn_fwd(qkv, tri_fwd, name, rider=None):
    lp = qkv.shape[0]
    n_pairs = (N_HEADS * HEAD_DIM) // 128
    nb = lp // ABLK
    assert nb <= 128 and 2 * HEAD_DIM == 128
    scale = _attn_scale()
    nt = (((1,), (1,)), ((), ()))
    rider = rider or _Rider()

    def body(q_ref, k_ref, v_ref, tri_ref, *rest):
        (o_ref, rs_ref), (r_s, acc_s, rs_s) = rider.split(rest, 2, 3)
        rider.start(pl.program_id(0) == 0)
        if n_pairs > 1:
            rider.forward(pl.program_id(0) == n_pairs - 1)
        lane = lax.broadcasted_iota(jnp.int32, (ABLK, 128), 1)
        row = lax.broadcasted_iota(jnp.int32, (ABLK, 128), 0)
        in_a = lane < HEAD_DIM
        causal = lane < row

        def begin(st, i):
            r_s[st] = jnp.zeros(r_s.shape[1:], F32)
            acc_s[st] = jnp.zeros(acc_s.shape[1:], F32)
            rs_s[st] = jnp.full(rs_s.shape[1:], GONE, F32)
            return q_ref[pl.ds(pl.multiple_of(i * ABLK, ABLK), ABLK), :] * scale

        def live(st):
            least = jnp.min(jnp.minimum(r_s[st, 0], r_s[st, 1]), axis=0, keepdims=True)
            return (least[0, 0] < EXP_ZERO_AT).astype(jnp.int32)

        def step(*args):
            _interleave(stages(*args))

        def stages(st, q2, kb0, nblk, diag):
            k0 = pl.multiple_of(kb0 * ABLK, ABLK)
            kbd = _stack_blocks(k_ref[pl.ds(k0, nblk * ABLK), :], nblk, in_a)
            vbd = _stack_blocks(v_ref[pl.ds(k0, nblk * ABLK), :], nblk, in_a)
            z = lax.dot_general(q2, kbd, nt, preferred_element_type=F32)
            ncol = 2 * nblk
            zt = [z[:, c * 128:(c + 1) * 128] for c in range(ncol)]
            if diag:
                zt = [jnp.where(causal, t, MASKED) if c >= ncol - 2 else t for c, t in enumerate(zt)]
            yield
            bounds = [ncol * j // ATT_SPLIT for j in range(ATT_SPLIT + 1)]
            ce = [None] * ncol
            for c0, c1 in reversed([b for b in zip(bounds[:-1], bounds[1:]) if b[0] < b[1]]):
                parts = [_split_hi_lo(_softplus(zt[c])) for c in range(c0, c1)]
                got = jnp.dot(jnp.concatenate(parts, axis=0), tri_ref[...], preferred_element_type=F32)
                for c in range(c0, c1):
                    ce[c] = got[(c - c0) * 128:(c - c0 + 1) * 128]
                yield
            rr = [r_s[st, 0], r_s[st, 1]]
            rsv = [rs_s[st, :, :128], rs_s[st, :, 128:]]
            ws = [None] * ncol
            for u in reversed(range(nblk)):
                for hh in range(2):
                    c = 2 * u + hh
                    ws[c] = jnp.exp(zt[c] - ce[c][:, :128] - rr[hh]).astype(BF16)
                    rsv[hh] = jnp.where(lane == kb0 + u, rr[hh], rsv[hh])
                    rr[hh] = rr[hh] + ce[c][:, 128:]
            acc_s[st] += jnp.dot(jnp.concatenate(ws, axis=1), vbd, preferred_element_type=F32)
            r_s[st, 0] = rr[0]
            r_s[st, 1] = rr[1]
            rs_s[st, :, :128] = rsv[0]
            rs_s[st, :, 128:] = rsv[1]

        def finish(st, q2, i, n_top):
            i_low = i - n_top
            n_grp = i_low // ATT_GROUP

            def more(c):
                return (c[0] < n_grp) & (c[1] > 0)

            def inner(c):
                step(st, q2, i_low - ATT_GROUP * (c[0] + 1), ATT_GROUP, False)
                return c[0] + 1, live(st)

            _, alive = lax.while_loop(more, inner, (jnp.int32(0), live(st)))
            rem = i_low - ATT_GROUP * n_grp
            for p in _pow2_below(ATT_GROUP):
                def last_steps(p=p):
                    step(st, q2, rem & (p - 1), p, False)
                    return live(st)

                alive = lax.cond(((rem & p) != 0) & (alive > 0), last_steps, lambda alive=alive: alive)

            q0 = pl.multiple_of(i * ABLK, ABLK)
            o_ref[pl.ds(q0, ABLK), :] = acc_s[st].astype(BF16)
            rs_ref[pl.ds(q0, ABLK), :] = rs_s[st]

        def single(i):
            q2 = begin(0, i)
            n_top = jnp.minimum(i, ATT_TOP)
            for t in range(ATT_TOP + 1):
                @pl.when(n_top == t)
                def _():
                    step(0, q2, i - t, t + 1, True)

            finish(0, q2, i, n_top)

        def pair(i):
            qa, qb = begin(0, i), begin(1, i + 1)
            _interleave(stages(0, qa, i - ATT_TOP, ATT_TOP + 1, True), stages(1, qb, i + 1 - ATT_TOP, ATT_TOP + 1, True))
            finish(0, qa, i, ATT_TOP)
            finish(1, qb, i + 1, ATT_TOP)

        n_head = min(ATT_TOP, nb)
        n_pair = (nb - n_head) // 2
        n_single = nb - 2 * n_pair

        def singles(k, carry):
            single(jnp.where(k < n_head, k, nb - n_single + k))
            return carry

        def pairs(j, carry):
            pair(n_head + 2 * j)
            return carry

        lax.fori_loop(0, n_single, singles, 0)
        lax.fori_loop(0, n_pair, pairs, 0)
        rider.wait(pl.program_id(0) == n_pairs - 1)

    col = lambda o: (lambda p: (0, p + o))
    return pl.pallas_call(
        body, out_shape=[SDS((lp, n_pairs * 128), BF16), SDS((lp, n_pairs * 256), F32)] + rider.out_shape,
        grid=(n_pairs,),
        in_specs=[pl.BlockSpec((lp, 128), col(0)), pl.BlockSpec((lp, 128), col(n_pairs)),
                  pl.BlockSpec((lp, 128), col(2 * n_pairs)), pl.BlockSpec((256, 256), lambda p: (0, 0))]
        + rider.in_specs(),
        out_specs=[pl.BlockSpec((lp, 128), col(0)), pl.BlockSpec((lp, 256), col(0))] + rider.out_specs(),
        scratch_shapes=[pltpu.VMEM((2, 2, ABLK, 128), F32), pltpu.VMEM((2, ABLK, 128), F32),
                        pltpu.VMEM((2, ABLK, 256), F32)] + rider.scratch(),
        name=name, compiler_params=_cp())(qkv, qkv, qkv, tri_fwd, *rider.srcs)


def _attn_bwd(qkv, d_out, rsave, tri_after, tri_before, name, rider=None):
    lp = qkv.shape[0]
    n_pairs = (N_HEADS * HEAD_DIM) // 128
    nb = lp // ABLK
    scale = _attn_scale()
    nt = (((1,), (1,)), ((), ()))
    tn = (((0,), (0,)), ((), ()))
    rider = rider or _Rider()

    def body(q_ref, k_ref, v_ref, do_ref, rs_ref, ta_ref, tb_ref, *rest):
        (o_ref,), (dk_s, dv_s, dq_s, pc_s) = rider.split(rest, 1, 4)
        rider.start(pl.program_id(0) == 0)
        lane = lax.broadcasted_iota(jnp.int32, (ABLK, 128), 1)
        row = lax.broadcasted_iota(jnp.int32, (ABLK, 128), 0)
        in_a = lane < HEAD_DIM
        causal = lane < row
        dk_s[...] = jnp.zeros_like(dk_s)
        dv_s[...] = jnp.zeros_like(dv_s)

        def begin(st, i):
            q0 = pl.multiple_of(i * ABLK, ABLK)
            q2 = q_ref[pl.ds(q0, ABLK), :] * scale
            do2 = do_ref[pl.ds(q0, ABLK), :]
            dq_s[st] = jnp.zeros(dq_s.shape[1:], F32)
            pc_s[st] = jnp.zeros(pc_s.shape[1:], F32)
            return dict(q0=q0, q2=q2, do2=do2, q_st=_head_halves(q2, in_a), do_st=_head_halves(do2, in_a))

        def step(*args):
            _interleave(stages(*args))

        def stages(st, blk, kb0, nblk, diag):
            q0, q2, do2 = blk["q0"], blk["q2"], blk["do2"]
            k0 = pl.multiple_of(kb0 * ABLK, ABLK)
            kbd = _stack_blocks(k_ref[pl.ds(k0, nblk * ABLK), :], nblk, in_a)
            vbd = _stack_blocks(v_ref[pl.ds(k0, nblk * ABLK), :], nblk, in_a)
            z = lax.dot_general(q2, kbd, nt, preferred_element_type=F32)
            dw = lax.dot_general(do2, vbd, nt, preferred_element_type=F32)
            ncol = 2 * nblk
            zt = [z[:, c * 128:(c + 1) * 128] for c in range(ncol)]
            if diag:
                zt = [jnp.where(causal, t, MASKED) if c >= ncol - 2 else t for c, t in enumerate(zt)]
            bounds = [ncol * j // ATT_SPLIT for j in range(ATT_SPLIT + 1)]
            batches = [range(c0, c1) for c0, c1 in zip(bounds[:-1], bounds[1:]) if c0 < c1]
            sps, ex, ws, dls, pe = [None] * ncol, [None] * ncol, [None] * ncol, [None] * ncol, [None] * ncol

            def mass(cols):
                for c in cols:
                    sps[c] = _softplus(zt[c])
                got = jnp.dot(jnp.concatenate([_split_hi_lo(sps[c]) for c in cols], axis=0), ta_ref[...],
                              preferred_element_type=F32)
                for j, c in enumerate(cols):
                    ex[c] = got[j * 128:(j + 1) * 128]

            def weights(cols):
                for c in cols:
                    u, hh = c // 2, c % 2
                    r_saved = jnp.sum(jnp.where(lane == kb0 + u, rs_ref[pl.ds(q0, ABLK), hh * 128:(hh + 1) * 128],
                                                0.0), axis=1, keepdims=True)
                    w = jnp.exp(zt[c] - ex[c] - r_saved)
                    ws[c] = w.astype(BF16)
                    dls[c] = dw[:, c * 128:(c + 1) * 128] * w
                got = jnp.dot(jnp.concatenate([_split_hi_lo(dls[c]) for c in cols], axis=0), tb_ref[...],
                              preferred_element_type=F32)
                for j, c in enumerate(cols):
                    pe[c] = got[j * 128:(j + 1) * 128]

            yield
            mass(batches[0])
            yield
            for j in range(len(batches)):
                if j + 1 < len(batches):
                    mass(batches[j + 1])
                    yield
                weights(batches[j])
                yield
            pc = [pc_s[st, 0], pc_s[st, 1]]
            dzs = []
            for c in range(ncol):
                hh = c % 2
                one_minus_beta = jnp.exp(-sps[c])
                dz = dls[c] * one_minus_beta - (pe[c][:, :128] + pc[hh]) * (1.0 - one_minus_beta)
                pc[hh] = pc[hh] + pe[c][:, 128:]
                dzs.append(dz.astype(BF16))
            pc_s[st, 0] = pc[0]
            pc_s[st, 1] = pc[1]
            dq_s[st] += jnp.dot(jnp.concatenate(dzs, axis=1), kbd, preferred_element_type=F32)
            by_head = lambda ts: jnp.concatenate([jnp.concatenate(ts[0::2], axis=1), jnp.concatenate(ts[1::2], axis=1)],
                                                 axis=0)
            rows = pl.ds(k0, nblk * ABLK)
            dk_s[rows, :] += lax.dot_general(by_head(dzs), blk["q_st"], tn, preferred_element_type=F32)
            dv_s[rows, :] += lax.dot_general(by_head(ws), blk["do_st"], tn, preferred_element_type=F32)

        def below(st, blk, i, n_top):
            gone = jnp.min(rs_ref[pl.ds(blk["q0"], ABLK), :], axis=0, keepdims=True) >= EXP_ZERO_AT
            lane1 = lax.broadcasted_iota(jnp.int32, (1, 128), 1)
            first = jnp.sum(jnp.where(gone[:, :128] & gone[:, 128:] & (lane1 < i), 1.0, 0.0)).astype(jnp.int32)
            i_low = i - n_top
            n_grp = i_low // ATT_GROUP
            rem = i_low - ATT_GROUP * n_grp
            for p in reversed(_pow2_below(ATT_GROUP)):
                @pl.when(((rem & p) != 0) & ((rem & (p - 1)) + p > first))
                def _():
                    step(st, blk, rem & (p - 1), p, False)

            def inner(g, c2):
                step(st, blk, rem + ATT_GROUP * g, ATT_GROUP, False)
                return c2

            lax.fori_loop(jnp.maximum(first - rem, 0) // ATT_GROUP, n_grp, inner, 0)

        def done(st, blk):
            o_ref[0, pl.ds(blk["q0"], ABLK), :] = (dq_s[st] * scale).astype(BF16)

        def single(i):
            blk = begin(0, i)
            n_top = jnp.minimum(i, ATT_TOP)
            below(0, blk, i, n_top)
            for t in range(ATT_TOP + 1):
                @pl.when(n_top == t)
                def _():
                    step(0, blk, i - t, t + 1, True)

            done(0, blk)

        def pair(i):
            a, b = begin(0, i), begin(1, i + 1)
            below(0, a, i, ATT_TOP)
            below(1, b, i + 1, ATT_TOP)
            _interleave(stages(0, a, i - ATT_TOP, ATT_TOP + 1, True), stages(1, b, i + 1 - ATT_TOP, ATT_TOP + 1, True))
            done(0, a)
            done(1, b)

        n_head = min(ATT_TOP, nb)
        n_pair = (nb - n_head) // 2
        n_single = nb - 2 * n_pair

        def singles(k, carry):
            single(jnp.where(k < n_head, k, nb - n_single + k))
            return carry

        def pairs(j, carry):
            pair(n_head + 2 * j)
            return carry

        lax.fori_loop(0, n_single, singles, 0)
        lax.fori_loop(0, n_pair, pairs, 0)
        o_ref[1] = dk_s[...].astype(BF16)
        o_ref[2] = dv_s[...].astype(BF16)
        rider.wait(pl.program_id(0) == n_pairs - 1)

    col = lambda o: (lambda p: (0, p + o))
    return pl.pallas_call(
        body, out_shape=[SDS((3, lp, n_pairs * 128), BF16)] + rider.out_shape, grid=(n_pairs,),
        in_specs=[pl.BlockSpec((lp, 128), col(0)), pl.BlockSpec((lp, 128), col(n_pairs)),
                  pl.BlockSpec((lp, 128), col(2 * n_pairs)), pl.BlockSpec((lp, 128), col(0)),
                  pl.BlockSpec((lp, 256), col(0)),
                  pl.BlockSpec((256, 128), lambda p: (0, 0)), pl.BlockSpec((256, 256), lambda p: (0, 0))]
        + rider.in_specs(),
        out_specs=[pl.BlockSpec((3, lp, 128), lambda p: (0, 0, p))] + rider.out_specs(),
        scratch_shapes=[pltpu.VMEM((lp, 128), F32), pltpu.VMEM((lp, 128), F32),
                        pltpu.VMEM((2, ABLK, 128), F32), pltpu.VMEM((2, 2, ABLK, 128), F32)] + rider.scratch(),
        name=name, compiler_params=_cp())(qkv, qkv, qkv, d_out, rsave, tri_after, tri_before, *rider.srcs)


def _conv_fwd_dw(cacg, w, b, name, rider=None):
    lp = cacg.shape[0]
    c = cacg.shape[1] // 2
    ncb = c // 128
    nchunk = lp // ABLK
    off = CONV_PAD - (CONV_K - 1)
    rider = rider or _Rider()

    def body(a_ref, g_ref, w_ref, b_ref, *rest):
        (y_ref,), (upad,) = rider.split(rest, 1, 1)
        rider.start(pl.program_id(0) == 0)
        if ncb > 1:
            rider.forward(pl.program_id(0) == ncb - 1)
        upad[0:CONV_PAD, :] = jnp.zeros((CONV_PAD, 128), F32)

        def fill(ch, carry):
            base = pl.multiple_of(ch * ABLK, ABLK)
            upad[pl.ds(base + CONV_PAD, ABLK), :] = a_ref[pl.ds(base, ABLK), :] * _sigmoid(g_ref[pl.ds(base, ABLK), :])
            return carry

        lax.fori_loop(0, nchunk, fill, 0)

        def comp(ch, carry):
            base = pl.multiple_of(ch * ABLK, ABLK)
            acc = jnp.zeros((ABLK, 128), F32)
            for k in range(CONV_K):
                acc = acc + upad[pl.ds(base + (off + k), ABLK), :] * w_ref[k:k + 1, :]
            y_ref[pl.ds(base, ABLK), :] = acc + b_ref[...]
            return carry

        lax.fori_loop(0, nchunk, comp, 0)
        rider.wait(pl.program_id(0) == ncb - 1)

    return pl.pallas_call(
        body, out_shape=[SDS((lp, c), F32)] + rider.out_shape, grid=(ncb,),
        in_specs=[pl.BlockSpec((lp, 128), lambda j: (0, j)), pl.BlockSpec((lp, 128), lambda j: (0, j + ncb)),
                  pl.BlockSpec((CONV_PAD, 128), lambda j: (0, j)), pl.BlockSpec((1, 128), lambda j: (0, j))]
        + rider.in_specs(),
        out_specs=[pl.BlockSpec((lp, 128), lambda j: (0, j))] + rider.out_specs(),
        scratch_shapes=[pltpu.VMEM((lp + CONV_PAD, 128), F32)] + rider.scratch(), name=name,
        compiler_params=_cp())(cacg, cacg, w, b, *rider.srcs)


def _ln_parts(x, g, b):
    mu = jnp.mean(x, axis=-1, keepdims=True)
    xc = x - mu
    rstd = lax.rsqrt(jnp.mean(xc * xc, axis=-1, keepdims=True) + EPS)
    xh = xc * rstd
    return xh, rstd, xh * g + b


def _conv_fwd_ln(yc, g, b, name):
    lp, c = yc.shape
    tm = _tile(lp, EW_ROWS, 16)

    def body(y_ref, g_ref, b_ref, o_ref):
        _, _, ln = _ln_parts(y_ref[...], g_ref[...], b_ref[...])
        o_ref[...] = (ln * _sigmoid(ln)).astype(BF16)

    return pl.pallas_call(
        body, out_shape=SDS((lp, c), BF16), grid=(lp // tm,),
        in_specs=[pl.BlockSpec((tm, c), lambda i: (i, 0)), pl.BlockSpec((1, c), lambda i: (0, 0)),
                  pl.BlockSpec((1, c), lambda i: (0, 0))],
        out_specs=pl.BlockSpec((tm, c), lambda i: (i, 0)), name=name, compiler_params=_cp())(yc, g, b)


def _conv_bwd_ln(yc, dout, g, b, name):
    lp, c = yc.shape
    tm = _tile(lp, EW_ROWS, 16)

    def body(y_ref, d_ref, g_ref, b_ref, o_ref, dg_ref, db_ref):
        @pl.when(pl.program_id(0) == 0)
        def _():
            dg_ref[...] = jnp.zeros_like(dg_ref)
            db_ref[...] = jnp.zeros_like(db_ref)

        xh, rstd, ln = _ln_parts(y_ref[...], g_ref[...], b_ref[...])
        s = _sigmoid(ln)
        dln = d_ref[...] * (s * (1.0 + ln * (1.0 - s)))
        dg_ref[...] += _sum8(dln * xh)
        db_ref[...] += _sum8(dln)
        dxh = dln * g_ref[...]
        o_ref[...] = rstd * (dxh - jnp.mean(dxh, axis=-1, keepdims=True)
                             - xh * jnp.mean(dxh * xh, axis=-1, keepdims=True))

    return pl.pallas_call(
        body, out_shape=[SDS((lp, c), F32), SDS((8, c), F32), SDS((8, c), F32)], grid=(lp // tm,),
        in_specs=[pl.BlockSpec((tm, c), lambda i: (i, 0)), pl.BlockSpec((tm, c), lambda i: (i, 0)),
                  pl.BlockSpec((1, c), lambda i: (0, 0)), pl.BlockSpec((1, c), lambda i: (0, 0))],
        out_specs=[pl.BlockSpec((tm, c), lambda i: (i, 0)), pl.BlockSpec((8, c), lambda i: (0, 0)),
                   pl.BlockSpec((8, c), lambda i: (0, 0))],
        name=name, compiler_params=_cp())(yc, dout, g, b)


def _conv_bwd_dw(dyc, cacg, w, name):
    lp, c = dyc.shape
    ncb = c // 128
    nchunk = lp // ABLK
    off = CONV_PAD - (CONV_K - 1)

    def body(dy_ref, a_ref, g_ref, w_ref, dcc_ref, dw_ref, db_ref, upad, dypad, dwacc):
        upad[0:CONV_PAD, :] = jnp.zeros((CONV_PAD, 128), F32)
        dypad[lp:lp + CONV_PAD, :] = jnp.zeros((CONV_PAD, 128), F32)
        dwacc[...] = jnp.zeros_like(dwacc)
        db_ref[...] = jnp.zeros_like(db_ref)

        def fill(ch, carry):
            base = pl.multiple_of(ch * ABLK, ABLK)
            upad[pl.ds(base + CONV_PAD, ABLK), :] = a_ref[pl.ds(base, ABLK), :] * _sigmoid(g_ref[pl.ds(base, ABLK), :])
            dypad[pl.ds(base, ABLK), :] = dy_ref[pl.ds(base, ABLK), :]
            return carry

        lax.fori_loop(0, nchunk, fill, 0)

        def comp(ch, carry):
            base = pl.multiple_of(ch * ABLK, ABLK)
            dy = dy_ref[pl.ds(base, ABLK), :]
            du = jnp.zeros((ABLK, 128), F32)
            for k in range(CONV_K):
                du = du + dypad[pl.ds(base + (CONV_K - 1 - k), ABLK), :] * w_ref[k:k + 1, :]
                dwacc[k * 8:(k + 1) * 8, :] += _sum8(dy * upad[pl.ds(base + (off + k), ABLK), :])
            db_ref[...] += _sum8(dy)
            a = a_ref[pl.ds(base, ABLK), :]
            s = _sigmoid(g_ref[pl.ds(base, ABLK), :])
            dcc_ref[0, pl.ds(base, ABLK), :] = (du * s).astype(BF16)
            dcc_ref[1, pl.ds(base, ABLK), :] = (du * a * (s * (1.0 - s))).astype(BF16)
            return carry

        lax.fori_loop(0, nchunk, comp, 0)
        dw_ref[...] = dwacc[...].reshape(CONV_PAD, 8, 128).sum(axis=1)

    return pl.pallas_call(
        body, out_shape=[SDS((2, lp, c), BF16), SDS((CONV_PAD, c), F32), SDS((8, c), F32)],
        grid=(ncb,),
        in_specs=[pl.BlockSpec((lp, 128), lambda j: (0, j)), pl.BlockSpec((lp, 128), lambda j: (0, j)),
                  pl.BlockSpec((lp, 128), lambda j: (0, j + ncb)), pl.BlockSpec((CONV_PAD, 128), lambda j: (0, j))],
        out_specs=[pl.BlockSpec((2, lp, 128), lambda j: (0, 0, j)),
                   pl.BlockSpec((CONV_PAD, 128), lambda j: (0, j)), pl.BlockSpec((8, 128), lambda j: (0, j))],
        scratch_shapes=[pltpu.VMEM((lp + CONV_PAD, 128), F32), pltpu.VMEM((lp + CONV_PAD, 128), F32),
                        pltpu.VMEM((CONV_PAD * 8, 128), F32)],
        name=name, compiler_params=_cp())(dyc, cacg, cacg, w)


def _mesh_pos():
    x, y, c = lax.axis_index("x"), lax.axis_index("y"), lax.axis_index("c")
    return x, y, c


def _peer(pos, r):
    x, y, c = pos
    px = (1 - x) if (r >> 2) & 1 else x
    py = (1 - y) if (r >> 1) & 1 else y
    pc = (1 - c) if r & 1 else c
    return (px, py, pc), 4 * px + 2 * py + pc


SIBLING = 1
OTHER_CHIPS = (2, 4, 6)
VIA_SIBLING = (3, 5, 7)


class _Job:
    def __init__(self, src, dst, scatter, src_layer=None, dst_layer=None):
        self.src, self.dst, self.scatter, self.src_layer, self.dst_layer = src, dst, scatter, src_layer, dst_layer

    def src_view(self, ins, idx):
        v = ins[self.src] if self.src_layer is None else ins[self.src].at[self.src_layer]
        return v.at[idx] if self.scatter else v

    def dst_view(self, outs, slot):
        v = outs[self.dst] if self.dst_layer is None else outs[self.dst].at[self.dst_layer]
        return v.at[slot]


def _remote(job, j, r, src, dst, to, send, recv):
    return pltpu.make_async_remote_copy(src_ref=src, dst_ref=dst, send_sem=send.at[j, r - 1], recv_sem=recv.at[j, r - 1],
                                        device_id=to, device_id_type=pl.DeviceIdType.MESH)


def _exchange_start(jobs, ins, outs, send, recv, loc):
    pos = _mesh_pos()
    me = 4 * pos[0] + 2 * pos[1] + pos[2]
    for j, job in enumerate(jobs):
        pltpu.make_async_copy(job.src_view(ins, me), job.dst_view(outs, me), loc.at[j]).start()
        for r in (range(1, N_DEV) if job.scatter else (SIBLING,) + OTHER_CHIPS):
            peer, peer_idx = _peer(pos, r)
            _remote(job, j, r, job.src_view(ins, peer_idx), job.dst_view(outs, me), peer, send, recv).start()


def _exchange_forward(jobs, ins, outs, send, recv, loc):
    pos = _mesh_pos()
    sibling, _ = _peer(pos, SIBLING)
    for j, job in enumerate(jobs):
        if job.scatter:
            continue
        for r in OTHER_CHIPS:
            peer, peer_idx = _peer(pos, r)
            slot = job.dst_view(outs, peer_idx)
            _remote(job, j, r, job.src_view(ins, peer_idx), slot, peer, send, recv).wait_recv()
            _remote(job, j, r ^ SIBLING, slot, slot, sibling, send, recv).start()


def _exchange_wait(jobs, ins, outs, send, recv, loc):
    pos = _mesh_pos()
    me = 4 * pos[0] + 2 * pos[1] + pos[2]
    for j, job in enumerate(jobs):
        for r in range(1, N_DEV):
            peer, peer_idx = _peer(pos, r)
            cp = _remote(job, j, r, job.src_view(ins, peer_idx), job.dst_view(outs, peer_idx), peer, send, recv)
            if job.scatter or r not in OTHER_CHIPS:
                cp.wait_recv()
            cp.wait_send()
        pltpu.make_async_copy(job.src_view(ins, me), job.dst_view(outs, me), loc.at[j]).wait()


def _exchange_sems(n_jobs):
    return [pltpu.SemaphoreType.DMA((n_jobs, N_DEV - 1)), pltpu.SemaphoreType.DMA((n_jobs, N_DEV - 1)),
            pltpu.SemaphoreType.DMA((n_jobs,))]


def _exchange(jobs, arrs, out_shape, name):
    n_in, n_out = len(arrs), len(out_shape)
    any_spec = pl.BlockSpec(memory_space=pl.ANY)

    def body(*refs):
        ins, outs, sems = refs[:n_in], refs[n_in:n_in + n_out], refs[n_in + n_out:]
        _exchange_start(jobs, ins, outs, *sems)
        _exchange_forward(jobs, ins, outs, *sems)
        _exchange_wait(jobs, ins, outs, *sems)

    return pl.pallas_call(
        body, out_shape=out_shape, in_specs=[any_spec] * n_in, out_specs=[any_spec] * n_out,
        scratch_shapes=_exchange_sems(len(jobs)),
        name=name, compiler_params=pltpu.CompilerParams(has_side_effects=True))(*arrs)


def _all_reduce_small(placed, cuts, rows, w, name):
    n_in = len(placed)

    def body(*refs):
        in_refs, o_refs = refs[:n_in], refs[n_in:n_in + len(cuts)]
        buf, send, recv = refs[n_in + len(cuts):]
        pos = _mesh_pos()
        me = 4 * pos[0] + 2 * pos[1] + pos[2]
        buf[me] = jnp.zeros((rows, w), F32)
        for ref, (arr, row0, col0, is_partial) in zip(in_refs, placed):
            val = ref[...].sum(axis=0, keepdims=True) if is_partial else ref[...]
            buf[me, row0:row0 + val.shape[0], col0:col0 + val.shape[1]] = val
        for r in range(1, N_DEV):
            peer, _ = _peer(pos, r)
            pltpu.make_async_remote_copy(src_ref=buf.at[me], dst_ref=buf.at[me], send_sem=send.at[r - 1],
                                         recv_sem=recv.at[r - 1], device_id=peer,
                                         device_id_type=pl.DeviceIdType.MESH).start()
        for r in range(1, N_DEV):
            peer, peer_idx = _peer(pos, r)
            cp = pltpu.make_async_remote_copy(src_ref=buf.at[me], dst_ref=buf.at[peer_idx], send_sem=send.at[r - 1],
                                              recv_sem=recv.at[r - 1], device_id=peer,
                                              device_id_type=pl.DeviceIdType.MESH)
            cp.wait_recv()
            cp.wait_send()
        acc = buf[0]
        for dev in range(1, N_DEV):
            acc = acc + buf[dev]
        for o_ref, (_, pieces) in zip(o_refs, cuts):
            for index, row0, nrows, col0, ncols in pieces:
                o_ref[index] = acc[row0:row0 + nrows, col0:col0 + ncols].reshape(o_ref.at[index].shape)

    vmem = pl.BlockSpec(memory_space=pltpu.VMEM)
    return pl.pallas_call(
        body, out_shape=[SDS(shape, F32) for shape, _ in cuts], in_specs=[vmem] * n_in, out_specs=[vmem] * len(cuts),
        scratch_shapes=[pltpu.VMEM((N_DEV, rows, w), F32), pltpu.SemaphoreType.DMA((N_DEV - 1,)),
                        pltpu.SemaphoreType.DMA((N_DEV - 1,))],
        name=name, compiler_params=pltpu.CompilerParams(has_side_effects=True))(*[p[0] for p in placed])


def _adamw_math(w, g, m, v):
    m = ADAM_B1 * m + (1.0 - ADAM_B1) * g
    v = ADAM_B2 * v + (1.0 - ADAM_B2) * (g * g)
    m_hat = m / (1.0 - ADAM_B1 ** ADAM_STEP)
    v_hat = v / (1.0 - ADAM_B2 ** ADAM_STEP)
    delta = -ADAM_LR * (m_hat / (jnp.sqrt(v_hat) + ADAM_EPS) + ADAM_WD * w)
    return delta, m, v


def _adamw_shard(parts, w, m, v, name):
    depth = len(parts)
    _, rr, cc = parts[0].shape
    tr = _tile(rr, 256, 8)
    nt = rr // tr
    part_block, blk = (N_DEV, tr, cc), pl.BlockSpec((None, tr, cc), lambda l, i: (l, i, 0))

    def body(*refs):
        p_refs = refs[:depth]
        w_ref, m_ref, v_ref, g_out, d_out, m_out, v_out = refs[depth:]
        for li in range(depth):
            @pl.when(pl.program_id(0) == li)
            def _(p_ref=p_refs[li]):
                g = p_ref[0].astype(F32)
                for dev in range(1, N_DEV):
                    g = g + p_ref[dev].astype(F32)
                delta, mm, vv = _adamw_math(w_ref[...], g, m_ref[...], v_ref[...])
                g_out[...] = g
                d_out[...] = delta
                m_out[...] = mm
                v_out[...] = vv

    def part_spec(li):
        return pl.BlockSpec(part_block, lambda l, i: (0, jnp.where(l == li, i, jnp.where(l < li, 0, nt - 1)), 0))

    return pl.pallas_call(
        body, out_shape=[SDS(w.shape, F32)] * 4, grid=(depth, nt),
        in_specs=[part_spec(li) for li in range(depth)] + [blk, blk, blk],
        out_specs=[blk] * 4, name=name, compiler_params=_cp())(*parts, w, m, v)


def _adamw_small(gs, ws, ms, vs, name):
    n = len(gs)

    def body(*refs):
        g_refs, w_refs, m_refs, v_refs = (refs[k * n:(k + 1) * n] for k in range(4))
        outs = refs[4 * n:]
        for k in range(n):
            delta, mm, vv = _adamw_math(w_refs[k][...], g_refs[k][...], m_refs[k][...], v_refs[k][...])
            outs[k][...] = delta
            outs[n + k][...] = mm
            outs[2 * n + k][...] = vv

    res = pl.pallas_call(body, out_shape=[SDS(w.shape, F32) for w in ws] * 3, name=name,
                         compiler_params=_cp())(*gs, *ws, *ms, *vs)
    return res[:n], res[n:2 * n], res[2 * n:]


def _from_cols(t):
    return jnp.transpose(t, (1, 0, 2)).reshape(t.shape[1], N_DEV * t.shape[2])


def _swap(t):
    return jnp.swapaxes(t, -1, -2)


def kernel(x, meta_tokens, mix_norm_g, w_in, conv_dw_w, conv_dw_b, conv_ln_g, conv_ln_b, w_out, ffn_norm_g, w_gate, w_up, w_down, final_norm_g, loss_target, m_meta_tokens, m_mix_norm_g, m_w_in, m_conv_dw_w, m_conv_dw_b, m_conv_ln_g, m_conv_ln_b, m_w_out, m_ffn_norm_g, m_w_gate, m_w_up, m_w_down, m_final_norm_g, v_meta_tokens, v_mix_norm_g, v_w_in, v_conv_dw_w, v_conv_dw_b, v_conv_ln_g, v_conv_ln_b, v_w_out, v_ffn_norm_g, v_w_gate, v_w_up, v_w_down, v_final_norm_g):
    depth, d, in_shard = w_in.shape
    seq = x.shape[1]
    sb = N_HEADS * HEAD_DIM
    cc = conv_dw_w.shape[2] * N_DEV
    ff = w_gate.shape[2] * N_DEV
    assert in_shard * N_DEV == 3 * sb + 2 * cc and x.shape[0] == 1
    lr = N_META + seq
    lp = -(-lr // ABLK) * ABLK
    me = 4 * lax.axis_index("x") + 2 * lax.axis_index("y") + lax.axis_index("c")

    big_names = ("w_in", "w_out", "w_gate", "w_up", "w_down")
    transposed = {"w_in": True, "w_out": False, "w_gate": True, "w_up": True, "w_down": False}
    shard = dict(w_in=_swap(w_in).astype(BF16), w_out=w_out.astype(BF16), w_gate=_swap(w_gate).astype(BF16),
                 w_up=_swap(w_up).astype(BF16), w_down=w_down.astype(BF16))

    def gather_of(keys):
        names = sorted({n for n, _ in keys}, key=big_names.index)
        jobs = [_Job(names.index(n), j, False, src_layer=i) for j, (n, i) in enumerate(keys)]
        return jobs, [shard[n] for n in names], [SDS((N_DEV,) + shard[n].shape[1:], BF16) for n, _ in keys]

    first_keys = [("w_in", 0)]

    def riding(keys):
        return _Rider(*gather_of(keys)) if keys else None

    def receive(keys, arrays):
        for (n, li), t in zip(keys, arrays):
            wl[li][n] = t.reshape(-1, d)

    jobs, srcs, out_shape = gather_of(first_keys)
    for extra in (meta_tokens, conv_dw_w):
        jobs.append(_Job(len(srcs), len(out_shape), False))
        srcs.append(extra)
        out_shape.append(SDS((N_DEV,) + extra.shape, F32))
    gathered = _exchange(jobs, srcs, out_shape, "gather_first")
    wl = [dict() for _ in range(depth)]
    receive(first_keys, gathered)
    meta_full = _from_cols(gathered[-2])
    taps = jnp.transpose(gathered[-1], (1, 2, 0, 3)).reshape(depth, CONV_K, cc)
    taps = jnp.pad(taps, ((0, 0), (0, CONV_PAD - CONV_K), (0, 0)))
    tri_fwd, tri_after, tri_before = _tri_consts()

    h = jnp.concatenate([meta_full, x[0], jnp.zeros((lp - lr, d), F32)], axis=0)
    saved = []
    for i in range(depth):
        p = wl[i]
        sv = dict(h_in=h)
        qkv, hn = _rms_mm(h, mix_norm_g[i:i + 1], p["w_in"], 0, 3 * sb, BF16, f"proj_qkv_{i}", True)
        keys = [("w_out", 0)] if i == 0 else []
        cacg, *arrived = _rms_mm(h, mix_norm_g[i:i + 1], p["w_in"], 3 * sb, 2 * cc, F32, f"proj_conv_{i}", False,
                                 riding(keys))
        receive(keys, arrived)
        keys = [("w_gate", i), ("w_up", i)]
        attn, rsave, *arrived = _attn_fwd(qkv, tri_fwd, f"attn_fwd_{i}", riding(keys))
        receive(keys, arrived)
        keys = [("w_down", 0)] if i == 0 else []
        yc, *arrived = _conv_fwd_dw(cacg, taps[i], conv_dw_b[i:i + 1], f"conv_fwd_dw_{i}", riding(keys))
        receive(keys, arrived)
        conv = _conv_fwd_ln(yc, conv_ln_g[i:i + 1], conv_ln_b[i:i + 1], f"conv_fwd_ln_{i}")
        h = _mix_out(attn, conv, p["w_out"], h, f"mix_out_{i}")
        sv.update(qkv=qkv, hn=hn, cacg=cacg, rsave=rsave, yc=yc, attn=attn, conv=conv, h_mid=h)
        keys = [("w_in", i + 1), ("w_out", i + 1), ("w_down", i + 1)] if i + 1 < depth else []
        h, hn2, act, *arrived = _ffn_fwd(h, ffn_norm_g[i:i + 1], p["w_gate"], p["w_up"], p["w_down"], f"ffn_fwd_{i}",
                                         riding(keys))
        receive(keys, arrived)
        sv.update(hn2=hn2, act=act)
        saved.append(sv)

    tpad = jnp.pad(loss_target[0], ((N_META, lp - lr), (0, 0)))
    dh, dg_final, loss_part = _loss_head(h, tpad, final_norm_g.reshape(1, d), seq, "loss_head")
    loss = lax.psum(loss_part[0, 0], MESH_AXES)

    parts = {}

    def sending(items):
        srcs = [t.reshape(N_DEV, t.shape[0] // N_DEV, d) for _, t in items]
        return _Rider([_Job(j, j, True) for j in range(len(items))], srcs, [SDS(t.shape, BF16) for t in srcs])

    def arrive(items, arrays):
        parts.update({key: t for (key, _), t in zip(items, arrays)})

    grads = [None] * depth
    from_above = []
    for i in reversed(range(depth)):
        p, sv = wl[i], saved[i]
        d_g, d_u, *got = _ffn_bwd_act(dh, sv["hn2"], p["w_gate"], p["w_up"], p["w_down"], f"ffn_bwd_act_{i}",
                                      sending(from_above))
        arrive(from_above, got)
        gw_down = _mm_tn(sv["act"], dh, f"grad_w_down_{i}")
        gw_gate = _mm_tn(d_g, sv["hn2"], f"grad_w_gate_{i}")
        gw_up = _mm_tn(d_u, sv["hn2"], f"grad_w_up_{i}")
        items = [(("w_down", i), gw_down)]
        dh, dg_ffn, *got = _ffn_bwd_in(d_g, d_u, p["w_gate"], p["w_up"], sv["h_mid"], ffn_norm_g[i:i + 1], dh,
                                       f"ffn_bwd_in_{i}", sending(items))
        arrive(items, got)
        gw_out = jnp.concatenate([_mm_tn(sv["attn"], dh, f"grad_w_out_attn_{i}"),
                                  _mm_tn(sv["conv"], dh, f"grad_w_out_conv_{i}")], axis=0)
        d_attn = _mm_nt(dh, p["w_out"], 0, sb, BF16, f"mix_bwd_attn_{i}")
        d_conv = _mm_nt(dh, p["w_out"], sb, cc, F32, f"mix_bwd_conv_{i}")
        items = [(("w_gate", i), gw_gate), (("w_up", i), gw_up)]
        dqkv, *got = _attn_bwd(sv["qkv"], d_attn, sv["rsave"], tri_after, tri_before, f"attn_bwd_{i}", sending(items))
        arrive(items, got)
        dyc, dg_ln, db_ln = _conv_bwd_ln(sv["yc"], d_conv, conv_ln_g[i:i + 1], conv_ln_b[i:i + 1], f"conv_bwd_ln_{i}")
        dcc, g_taps_i, db_conv = _conv_bwd_dw(dyc, sv["cacg"], taps[i], f"conv_bwd_dw_{i}")
        items = [(("w_out", i), gw_out)] if i == 0 else []
        gw_in, *got = _grad_w_in_t(dqkv, dcc, sv["hn"], f"grad_w_in_{i}", sending(items))
        arrive(items, got)
        from_above = [(("w_in", i), gw_in)] + ([(("w_out", i), gw_out)] if i > 0 else [])
        items = from_above if i == 0 else []
        dh, dg_mix, *got = _mix_bwd_in(dqkv, dcc, p["w_in"], sv["h_in"], mix_norm_g[i:i + 1], dh, f"mix_bwd_in_{i}",
                                       sending(items))
        arrive(items, got)
        grads[i] = dict(taps=g_taps_i, dg_mix=dg_mix, dg_ffn=dg_ffn, dg_ln=dg_ln, db_ln=db_ln, db_conv=db_conv)
    grad_x = dh[N_META:lr][None]

    per = d // depth
    assert depth * cc <= d and d % depth == 0 and per % 128 == 0
    placed, cuts, row = [], [], 0
    for key in ("dg_mix", "dg_ffn"):
        placed += [(grads[i][key], row + i, 0, True) for i in range(depth)]
        cuts.append(((depth, d), [(slice(0, depth), row, depth, 0, d)]))
        row += depth
    placed.append((dg_final, row, 0, True))
    cuts.append(((1, d), [(slice(0, 1), row, 1, 0, d)]))
    row += 1
    for key in ("db_conv", "dg_ln", "db_ln"):
        placed += [(grads[i][key], row, i * per, True) for i in range(depth)]
        cuts.append(((depth, cc), [(slice(i, i + 1), row, 1, i * per, cc) for i in range(depth)]))
        row += 1
    placed += [(grads[i]["taps"], row, i * per, False) for i in range(depth)]
    cuts.append(((depth, CONV_PAD, cc), [(i, row, CONV_PAD, i * per, cc) for i in range(depth)]))
    row += CONV_PAD
    placed.append((dh[:N_META], row, 0, False))
    cuts.append(((N_META, d), [(slice(0, N_META), row, N_META, 0, d)]))
    row += N_META
    g_mix, g_ffn, g_final, g_cb, g_lg, g_lb, g_taps_full, g_meta_full = _all_reduce_small(
        placed, cuts, -(-row // 8) * 8, d, "reduce_small")
    csh = cc // N_DEV
    g_taps_own = lax.dynamic_slice_in_dim(g_taps_full[:, :CONV_K], me * csh, csh, axis=2)
    msh = d // N_DEV
    g_meta_own = lax.dynamic_slice_in_dim(g_meta_full, me * msh, msh, axis=1)

    row1 = lambda t: t.reshape(1, d)
    small_g = [g_meta_own, g_mix, g_taps_own, g_cb, g_lg, g_lb, g_ffn, g_final]
    small_w = [meta_tokens, mix_norm_g, conv_dw_w, conv_dw_b, conv_ln_g, conv_ln_b, ffn_norm_g, row1(final_norm_g)]
    small_m = [m_meta_tokens, m_mix_norm_g, m_conv_dw_w, m_conv_dw_b, m_conv_ln_g, m_conv_ln_b, m_ffn_norm_g,
               row1(m_final_norm_g)]
    small_v = [v_meta_tokens, v_mix_norm_g, v_conv_dw_w, v_conv_dw_b, v_conv_ln_g, v_conv_ln_b, v_ffn_norm_g,
               row1(v_final_norm_g)]
    s_delta, s_m, s_v = _adamw_small(small_g, small_w, small_m, small_v, "adamw_small")
    unrow = lambda ts: list(ts[:-1]) + [ts[-1].reshape(d)]
    small_g, s_delta, s_m, s_v = unrow(small_g), unrow(s_delta), unrow(s_m), unrow(s_v)

    big = []
    for n, w, m, v in zip(big_names, (w_in, w_out, w_gate, w_up, w_down), (m_w_in, m_w_out, m_w_gate, m_w_up, m_w_down),
                          (v_w_in, v_w_out, v_w_gate, v_w_up, v_w_down)):
        fix = _swap if transposed[n] else (lambda t: t)
        res = _adamw_shard([parts[(n, i)] for i in range(depth)], fix(w), fix(m), fix(v), f"adamw_{n}")
        big.append([fix(t) for t in res])
    b_in, b_out, b_gate, b_up, b_down = big

    def ordered(k, smalls):
        s_meta, s_mix, s_taps, s_cb, s_lg, s_lb, s_ffn, s_final = smalls
        return [s_meta, s_mix, b_in[k], s_taps, s_cb, s_lg, s_lb, b_out[k], s_ffn, b_gate[k], b_up[k], b_down[k], s_final]

    return (loss, grad_x, *ordered(0, small_g), *ordered(1, s_delta), *ordered(2, s_m), *ordered(3, s_v))
```

```python
import functools
import math

import jax
import jax.numpy as jnp
from jax import lax
from jax.experimental import pallas as pl
from jax.experimental.pallas import tpu as pltpu

F32 = jnp.float32
BF16 = jnp.bfloat16
SDS = jax.ShapeDtypeStruct

N_META = 16
N_HEADS = 8
HEAD_DIM = 64
CONV_K = 31
CONV_PAD = 32
ABLK = 128
ATT_GROUP = 4
ATT_SPLIT = 2
ATT_TOP = 2
EXP_ZERO_AT = 104.0
GONE = 1e30
MASKED = -1e30
FF_CHUNK = 256
FF_BLOCK = 1408
FF_ROWS = 528
EPS = 1e-6
N_DEV = 8
MESH_AXES = ("x", "y", "c")
ADAM_LR = 0.001
ADAM_B1 = 0.9
ADAM_B2 = 0.999
ADAM_EPS = 1e-08
ADAM_WD = 0.01
ADAM_STEP = 10
MIB = 1 << 20
VMEM_LIMIT_MIB = 48
MM_ROWS = 1056
EW_ROWS = 528


def _cp():
    return pltpu.CompilerParams(vmem_limit_bytes=VMEM_LIMIT_MIB * MIB)


def _tile(n, cap, mult):
    best = None
    for t in range(mult, min(n, cap) + 1, mult):
        if n % t == 0:
            best = t
    assert best is not None, (n, cap, mult)
    return best


def _sum8(v):
    r, c = v.shape
    return v.reshape(r // 8, 8, c).sum(axis=0)


def _sigmoid(x):
    return 1.0 / (1.0 + jnp.exp(-x))


def _ff_block(f):
    fb = _tile(f, FF_BLOCK, 128)
    return fb, [(s0, min(FF_CHUNK, fb - s0)) for s0 in range(0, fb, FF_CHUNK)]


NT = (((1,), (1,)), ((), ()))
TN = (((0,), (0,)), ((), ()))


def _rms_mm(h, g, wt, w_row0, n_cols, out_dtype, name, hn_out, rider=None):
    lp, d = h.shape
    tm = _tile(lp, MM_ROWS, 16)
    tn = _tile(math.gcd(n_cols, w_row0), 768, 128)
    off = w_row0 // tn
    ni, nj = lp // tm, n_cols // tn
    rider = rider or _Rider()

    def body(h_ref, g_ref, w_ref, *rest):
        outs, (hn_s,) = rider.split(rest, 2 if hn_out else 1, 1)
        o_ref = outs[0]
        hn_ref = outs[1] if hn_out else None
        rider.start((pl.program_id(0) == 0) & (pl.program_id(1) == 0))
        if ni > 1:
            rider.forward((pl.program_id(0) == ni - 1) & (pl.program_id(1) == 0))

        @pl.when(pl.program_id(1) == 0)
        def _():
            x = h_ref[...]
            r = lax.rsqrt(jnp.mean(x * x, axis=-1, keepdims=True) + EPS)
            hn = ((x * r) * g_ref[...]).astype(BF16)
            hn_s[...] = hn
            if hn_out:
                hn_ref[...] = hn

        o_ref[...] = lax.dot_general(hn_s[...], w_ref[...], NT, preferred_element_type=F32).astype(out_dtype)
        rider.wait((pl.program_id(0) == ni - 1) & (pl.program_id(1) == nj - 1))

    out_shape = [SDS((lp, n_cols), out_dtype)]
    out_specs = [pl.BlockSpec((tm, tn), lambda i, j: (i, j))]
    if hn_out:
        out_shape.append(SDS((lp, d), BF16))
        out_specs.append(pl.BlockSpec((tm, d), lambda i, j: (i, 0)))
    return pl.pallas_call(
        body, out_shape=out_shape + rider.out_shape, grid=(ni, nj),
        in_specs=[pl.BlockSpec((tm, d), lambda i, j: (i, 0)),
                  pl.BlockSpec((1, d), lambda i, j: (0, 0)),
                  pl.BlockSpec((tn, d), lambda i, j: (j + off, 0))] + rider.in_specs(),
        out_specs=out_specs + rider.out_specs(), scratch_shapes=[pltpu.VMEM((tm, d), BF16)] + rider.scratch(),
        name=name, compiler_params=_cp())(h, g, wt, *rider.srcs)


def _mm_nt(a, b, b_row0, n_out, out_dtype, name):
    m, k = a.shape
    tm = _tile(m, MM_ROWS, 16)
    tn = _tile(math.gcd(n_out, b_row0), 512, 128)
    off = b_row0 // tn

    def body(a_ref, b_ref, o_ref):
        o_ref[...] = lax.dot_general(a_ref[...].astype(BF16), b_ref[...], NT,
                                     preferred_element_type=F32).astype(out_dtype)

    return pl.pallas_call(
        body, out_shape=SDS((m, n_out), out_dtype), grid=(m // tm, n_out // tn),
        in_specs=[pl.BlockSpec((tm, k), lambda i, j: (i, 0)), pl.BlockSpec((tn, k), lambda i, j: (j + off, 0))],
        out_specs=pl.BlockSpec((tm, tn), lambda i, j: (i, j)), name=name, compiler_params=_cp())(a, b)


def _mix_out(attn, conv, w, res, name):
    m, ka = attn.shape
    kc = conv.shape[1]
    n = w.shape[1]
    assert ka == kc
    tm = _tile(m, MM_ROWS, 16)
    tn = _tile(n, 512, 128)

    def body(a_ref, c_ref, wa_ref, wc_ref, r_ref, o_ref):
        o_ref[...] = (r_ref[...] + jnp.dot(a_ref[...], wa_ref[...], preferred_element_type=F32)
                      + jnp.dot(c_ref[...], wc_ref[...], preferred_element_type=F32))

    return pl.pallas_call(
        body, out_shape=SDS((m, n), F32), grid=(m // tm, n // tn),
        in_specs=[pl.BlockSpec((tm, ka), lambda i, j: (i, 0)), pl.BlockSpec((tm, kc), lambda i, j: (i, 0)),
                  pl.BlockSpec((ka, tn), lambda i, j: (0, j)), pl.BlockSpec((kc, tn), lambda i, j: (1, j)),
                  pl.BlockSpec((tm, tn), lambda i, j: (i, j))],
        out_specs=pl.BlockSpec((tm, tn), lambda i, j: (i, j)), name=name, compiler_params=_cp())(attn, conv, w, w, res)


def _mm_tn(a, b, name):
    l, m = a.shape
    n = b.shape[1]
    tm = _tile(m, 1408, 128)
    tn = _tile(n, 1024, 128)
    tl = _tile(l, 1408, 128)
    nl = l // tl

    def body(a_ref, b_ref, o_ref, acc):
        @pl.when(pl.program_id(2) == 0)
        def _():
            acc[...] = jnp.zeros_like(acc)

        acc[...] += lax.dot_general(a_ref[...].astype(BF16), b_ref[...].astype(BF16), TN, preferred_element_type=F32)

        @pl.when(pl.program_id(2) == nl - 1)
        def _():
            o_ref[...] = acc[...].astype(BF16)

    return pl.pallas_call(
        body, out_shape=SDS((m, n), BF16), grid=(m // tm, n // tn, nl),
        in_specs=[pl.BlockSpec((tl, tm), lambda i, j, s: (s, i)),
                  pl.BlockSpec((tl, tn), lambda i, j, s: (s, j))],
        out_specs=pl.BlockSpec((tm, tn), lambda i, j, s: (i, j)),
        scratch_shapes=[pltpu.VMEM((tm, tn), F32)], name=name, compiler_params=_cp())(a, b)


def _grad_w_in_t(dqkv, dcc, hn, name, rider=None):
    nq, l, w = dqkv.shape
    nc = dcc.shape[0]
    d = hn.shape[1]
    assert dcc.shape[2] == w
    tn = _tile(d, 1024, 128)
    tl = _tile(l, 1408, 128)
    nl = l // tl
    nj = d // tn
    rider = rider or _Rider()

    def body(q_ref, c_ref, b_ref, *rest):
        (o_ref,), (acc,) = rider.split(rest, 1, 1)
        p, jj, s = pl.program_id(0), pl.program_id(1), pl.program_id(2)
        rider.start((p == 0) & (jj == 0) & (s == 0))

        @pl.when(s == 0)
        def _():
            acc[...] = jnp.zeros_like(acc)

        @pl.when(p < nq)
        def _():
            acc[...] += lax.dot_general(q_ref[...], b_ref[...], TN, preferred_element_type=F32)

        @pl.when(p >= nq)
        def _():
            acc[...] += lax.dot_general(c_ref[...], b_ref[...], TN, preferred_element_type=F32)

        @pl.when(s == nl - 1)
        def _():
            o_ref[...] = acc[...].astype(BF16)

        rider.wait((p == nq + nc - 1) & (jj == nj - 1) & (s == nl - 1))

    out, *arrived = pl.pallas_call(
        body, out_shape=[SDS((nq + nc, w, d), BF16)] + rider.out_shape, grid=(nq + nc, nj, nl),
        in_specs=[pl.BlockSpec((None, tl, w), lambda p, j, s: (jnp.minimum(p, nq - 1), s, 0)),
                  pl.BlockSpec((None, tl, w), lambda p, j, s: (jnp.maximum(p - nq, 0), s, 0)),
                  pl.BlockSpec((tl, tn), lambda p, j, s: (s, j))] + rider.in_specs(),
        out_specs=[pl.BlockSpec((None, w, tn), lambda p, j, s: (p, 0, j))] + rider.out_specs(),
        scratch_shapes=[pltpu.VMEM((w, tn), F32)] + rider.scratch(), name=name,
        compiler_params=_cp())(dqkv, dcc, hn, *rider.srcs)
    return [out.reshape((nq + nc) * w, d)] + arrived


def _rms_bwd_tail(acc, h_ref, g_ref, dres_ref, o_ref, dg_ref):
    x = h_ref[...]
    r = lax.rsqrt(jnp.mean(x * x, axis=-1, keepdims=True) + EPS)
    xr = x * r
    dyv = acc[...]
    gy = dyv * g_ref[...]
    o_ref[...] = dres_ref[...] + r * (gy - xr * jnp.mean(gy * xr, axis=-1, keepdims=True))
    dg_ref[...] += _sum8(dyv * xr)


def _mix_bwd_in(dqkv, dcc, wt, h, g, dres, name, rider=None):
    nq, lp, w = dqkv.shape
    nc = dcc.shape[0]
    d = h.shape[1]
    tm = _tile(lp, FF_ROWS, 16)
    ni = lp // tm
    rider = rider or _Rider()

    def body(q_ref, c_ref, w_ref, h_ref, g_ref, dres_ref, *rest):
        (o_ref, dg_ref), (acc,) = rider.split(rest, 2, 1)
        i = pl.program_id(0)
        rider.start(i == 0)

        @pl.when(i == 0)
        def _():
            dg_ref[...] = jnp.zeros_like(dg_ref)

        pieces = [q_ref[p] for p in range(nq)] + [c_ref[p] for p in range(nc)]
        total = None
        for p, piece in enumerate(pieces):
            part = jnp.dot(piece, w_ref[p * w:(p + 1) * w, :], preferred_element_type=F32)
            total = part if total is None else total + part
        acc[...] = total
        _rms_bwd_tail(acc, h_ref, g_ref, dres_ref, o_ref, dg_ref)
        rider.wait(i == ni - 1)

    return pl.pallas_call(
        body, out_shape=[SDS((lp, d), F32), SDS((8, d), F32)] + rider.out_shape, grid=(ni,),
        in_specs=[pl.BlockSpec((nq, tm, w), lambda i: (0, i, 0)),
                  pl.BlockSpec((nc, tm, w), lambda i: (0, i, 0)),
                  pl.BlockSpec(((nq + nc) * w, d), lambda i: (0, 0)),
                  pl.BlockSpec((tm, d), lambda i: (i, 0)),
                  pl.BlockSpec((1, d), lambda i: (0, 0)),
                  pl.BlockSpec((tm, d), lambda i: (i, 0))] + rider.in_specs(),
        out_specs=[pl.BlockSpec((tm, d), lambda i: (i, 0)), pl.BlockSpec((8, d), lambda i: (0, 0))]
        + rider.out_specs(),
        scratch_shapes=[pltpu.VMEM((tm, d), F32)] + rider.scratch(), name=name,
        compiler_params=_cp())(dqkv, dcc, wt, h, g, dres, *rider.srcs)


def _ffn_bwd_in(d_g, d_u, wg_t, wu_t, h, g, dres, name, rider=None):
    lp, f = d_g.shape
    d = h.shape[1]
    tm = _tile(lp, FF_ROWS, 16)
    tk = _tile(f, FF_BLOCK, 128)
    nk = f // tk
    ni = lp // tm
    rider = rider or _Rider()

    def body(dg_in, du_in, wg_ref, wu_ref, h_ref, g_ref, dres_ref, *rest):
        (o_ref, dg_ref), (acc,) = rider.split(rest, 2, 1)
        i, kk = pl.program_id(0), pl.program_id(1)
        rider.start((i == 0) & (kk == 0))

        @pl.when(kk == 0)
        def _():
            acc[...] = jnp.zeros_like(acc)

        @pl.when((kk == 0) & (i == 0))
        def _():
            dg_ref[...] = jnp.zeros_like(dg_ref)

        acc[...] += (jnp.dot(dg_in[...], wg_ref[...], preferred_element_type=F32)
                     + jnp.dot(du_in[...], wu_ref[...], preferred_element_type=F32))

        @pl.when(kk == nk - 1)
        def _():
            _rms_bwd_tail(acc, h_ref, g_ref, dres_ref, o_ref, dg_ref)

        rider.wait((i == ni - 1) & (kk == nk - 1))

    return pl.pallas_call(
        body, out_shape=[SDS((lp, d), F32), SDS((8, d), F32)] + rider.out_shape, grid=(ni, nk),
        in_specs=[pl.BlockSpec((tm, tk), lambda i, j: (i, j)), pl.BlockSpec((tm, tk), lambda i, j: (i, j)),
                  pl.BlockSpec((tk, d), lambda i, j: (j, 0)), pl.BlockSpec((tk, d), lambda i, j: (j, 0)),
                  pl.BlockSpec((tm, d), lambda i, j: (i, 0)),
                  pl.BlockSpec((1, d), lambda i, j: (0, 0)),
                  pl.BlockSpec((tm, d), lambda i, j: (i, 0))] + rider.in_specs(),
        out_specs=[pl.BlockSpec((tm, d), lambda i, j: (i, 0)), pl.BlockSpec((8, d), lambda i, j: (0, 0))]
        + rider.out_specs(),
        scratch_shapes=[pltpu.VMEM((tm, d), F32)] + rider.scratch(), name=name,
        compiler_params=_cp())(d_g, d_u, wg_t, wu_t, h, g, dres, *rider.srcs)


def _ffn_fwd(h, g, wg_t, wu_t, w_d, name, rider=None):
    lp, d = h.shape
    f = w_d.shape[0]
    tm = _tile(lp, FF_ROWS, 16)
    fb, subs = _ff_block(f)
    nc = f // fb
    ni = lp // tm
    rider = rider or _Rider()

    def body(h_ref, g_ref, wg_ref, wu_ref, wd_ref, *rest):
        (o_ref, hn_ref, act_ref), (hn_s, acc) = rider.split(rest, 3, 2)
        c = pl.program_id(1)
        rider.start((pl.program_id(0) == 0) & (c == 0))
        if ni > 1:
            rider.forward((pl.program_id(0) == ni - 1) & (c == 0))

        @pl.when(c == 0)
        def _():
            x = h_ref[...]
            r = lax.rsqrt(jnp.mean(x * x, axis=-1, keepdims=True) + EPS)
            hn = ((x * r) * g_ref[...]).astype(BF16)
            hn_s[...] = hn
            hn_ref[...] = hn
            acc[...] = jnp.zeros_like(acc)

        total = None
        for s0, sw in subs:
            gg = lax.dot_general(hn_s[...], wg_ref[s0:s0 + sw, :], NT, preferred_element_type=F32)
            uu = lax.dot_general(hn_s[...], wu_ref[s0:s0 + sw, :], NT, preferred_element_type=F32)
            act = ((gg * _sigmoid(gg)) * uu).astype(BF16)
            act_ref[:, s0:s0 + sw] = act
            part = jnp.dot(act, wd_ref[s0:s0 + sw, :], preferred_element_type=F32)
            total = part if total is None else total + part
        acc[...] += total

        @pl.when(c == nc - 1)
        def _():
            o_ref[...] = h_ref[...] + acc[...]

        rider.wait((pl.program_id(0) == ni - 1) & (c == nc - 1))

    chunk = pl.BlockSpec((fb, d), lambda i, j: (j, 0))
    return pl.pallas_call(
        body, out_shape=[SDS((lp, d), F32), SDS((lp, d), BF16), SDS((lp, f), BF16)] + rider.out_shape, grid=(ni, nc),
        in_specs=[pl.BlockSpec((tm, d), lambda i, j: (i, 0)), pl.BlockSpec((1, d), lambda i, j: (0, 0)),
                  chunk, chunk, chunk] + rider.in_specs(),
        out_specs=[pl.BlockSpec((tm, d), lambda i, j: (i, 0)),
                   pl.BlockSpec((tm, d), lambda i, j: (i, 0)),
                   pl.BlockSpec((tm, fb), lambda i, j: (i, j))] + rider.out_specs(),
        scratch_shapes=[pltpu.VMEM((tm, d), BF16), pltpu.VMEM((tm, d), F32)] + rider.scratch(),
        name=name, compiler_params=_cp())(h, g, wg_t, wu_t, w_d, *rider.srcs)


def _ffn_bwd_act(dh, hn, wg_t, wu_t, w_d, name, rider=None):
    lp, d = dh.shape
    f = w_d.shape[0]
    tm = _tile(lp, FF_ROWS, 16)
    fb, subs = _ff_block(f)
    nc = f // fb
    ni = lp // tm
    rider = rider or _Rider()

    def body(dh_ref, hn_ref, wg_ref, wu_ref, wd_ref, *rest):
        (dg_out, du_out), (dh_s,) = rider.split(rest, 2, 1)
        rider.start((pl.program_id(0) == 0) & (pl.program_id(1) == 0))

        @pl.when(pl.program_id(1) == 0)
        def _():
            dh_s[...] = dh_ref[...].astype(BF16)

        for s0, sw in subs:
            gg = lax.dot_general(hn_ref[...], wg_ref[s0:s0 + sw, :], NT, preferred_element_type=F32)
            uu = lax.dot_general(hn_ref[...], wu_ref[s0:s0 + sw, :], NT, preferred_element_type=F32)
            dact = lax.dot_general(dh_s[...], wd_ref[s0:s0 + sw, :], NT, preferred_element_type=F32)
            s = _sigmoid(gg)
            dg_out[:, s0:s0 + sw] = (dact * uu * (s * (1.0 + gg * (1.0 - s)))).astype(BF16)
            du_out[:, s0:s0 + sw] = (dact * (gg * s)).astype(BF16)
        rider.wait((pl.program_id(0) == ni - 1) & (pl.program_id(1) == nc - 1))

    chunk = pl.BlockSpec((fb, d), lambda i, j: (j, 0))
    out = pl.BlockSpec((tm, fb), lambda i, j: (i, j))
    return pl.pallas_call(
        body, out_shape=[SDS((lp, f), BF16), SDS((lp, f), BF16)] + rider.out_shape, grid=(ni, nc),
        in_specs=[pl.BlockSpec((tm, d), lambda i, j: (i, 0)), pl.BlockSpec((tm, d), lambda i, j: (i, 0)),
                  chunk, chunk, chunk] + rider.in_specs(),
        out_specs=[out, out] + rider.out_specs(), scratch_shapes=[pltpu.VMEM((tm, d), BF16)] + rider.scratch(),
        name=name, compiler_params=_cp())(dh, hn, wg_t, wu_t, w_d, *rider.srcs)


def _loss_head(h, tpad, g, n_real, name):
    lp, d = h.shape
    tm = _tile(lp, EW_ROWS, 16)

    def body(h_ref, t_ref, g_ref, dh_ref, dg_ref, loss_ref):
        i = pl.program_id(0)

        @pl.when(i == 0)
        def _():
            dg_ref[...] = jnp.zeros_like(dg_ref)
            loss_ref[...] = jnp.zeros_like(loss_ref)

        x = h_ref[...]
        r = lax.rsqrt(jnp.mean(x * x, axis=-1, keepdims=True) + EPS)
        xr = x * r
        y = xr * g_ref[...]
        row = i * tm + lax.broadcasted_iota(jnp.int32, (tm, d), 0)
        valid = (row >= N_META) & (row < N_META + n_real)
        diff = jnp.where(valid, y - t_ref[...], 0.0)
        loss_ref[...] += jnp.sum(diff * diff) * (0.5 / d)
        dy = diff * (1.0 / d)
        gy = dy * g_ref[...]
        dh_ref[...] = r * (gy - xr * jnp.mean(gy * xr, axis=-1, keepdims=True))
        dg_ref[...] += _sum8(dy * xr)

    return pl.pallas_call(
        body, out_shape=[SDS((lp, d), F32), SDS((8, d), F32), SDS((8, 128), F32)], grid=(lp // tm,),
        in_specs=[pl.BlockSpec((tm, d), lambda i: (i, 0)),
                  pl.BlockSpec((tm, d), lambda i: (i, 0)),
                  pl.BlockSpec((1, d), lambda i: (0, 0))],
        out_specs=[pl.BlockSpec((tm, d), lambda i: (i, 0)),
                   pl.BlockSpec((8, d), lambda i: (0, 0)),
                   pl.BlockSpec((8, 128), lambda i: (0, 0))],
        name=name, compiler_params=_cp())(h, tpad, g)


def _tri_consts():
    j = lax.broadcasted_iota(jnp.int32, (ABLK, ABLK), 0)
    s = lax.broadcasted_iota(jnp.int32, (ABLK, ABLK), 1)
    after = (j >= s).astype(BF16)
    before = (j < s).astype(BF16)
    ones = jnp.ones((ABLK, ABLK), BF16)
    two = lambda t: jnp.concatenate([t, t], axis=0)
    return (two(jnp.concatenate([after, ones], axis=1)),
            two(after),
            two(jnp.concatenate([before, ones], axis=1)))


def _softplus(z):
    neg_abs = lax.bitcast_convert_type(lax.bitcast_convert_type(z, jnp.uint32) | jnp.uint32(0x80000000), F32)
    return jnp.log(1.0 + jnp.exp(neg_abs)) + jnp.maximum(z, 0.0)


def _split_hi_lo(m):
    hi = m.astype(BF16)
    lo = (m - hi.astype(F32)).astype(BF16)
    return jnp.concatenate([hi, lo], axis=1)


def _head_halves(t2, in_a):
    zero = jnp.zeros_like(t2)
    return jnp.concatenate([jnp.where(in_a, t2, zero), jnp.where(in_a, zero, t2)], axis=0)


def _stack_blocks(t, nblk, in_a):
    return jnp.concatenate([_head_halves(t[u * ABLK:(u + 1) * ABLK], in_a) for u in range(nblk)], axis=0)


def _interleave(*gens):
    alive = list(gens)
    while alive:
        for g in list(alive):
            if next(g, alive) is alive:
                alive.remove(g)


def _attn_scale():
    scale = 1.0 / math.sqrt(HEAD_DIM)
    assert math.frexp(scale)[0] == 0.5, "a power of two, so that scaling q in bf16 is exact"
    return scale


def _pow2_below(n):
    assert n & (n - 1) == 0
    return [p for p in (64, 32, 16, 8, 4, 2, 1) if p < n]


class _Rider:
    def __init__(self, jobs=(), srcs=(), out_shape=()):
        self.jobs, self.srcs, self.out_shape = list(jobs), list(srcs), list(out_shape)
        self.any = [pl.BlockSpec(memory_space=pl.ANY)]

    def split(self, rest, n_out, n_scratch):
        ni, no = len(self.srcs), len(self.out_shape)
        self.ins, outs = rest[:ni], rest[ni:ni + n_out]
        self.outs = rest[ni + n_out:ni + n_out + no]
        scratch = rest[ni + n_out + no:ni + n_out + no + n_scratch]
        self.sems = rest[ni + n_out + no + n_scratch:]
        return outs, scratch

    def start(self, first):
        self.forwarded = False
        if self.jobs:
            @pl.when(first)
            def _():
                _exchange_start(self.jobs, self.ins, self.outs, *self.sems)

    def forward(self, late):
        self.forwarded = True
        if self.jobs:
            @pl.when(late)
            def _():
                _exchange_forward(self.jobs, self.ins, self.outs, *self.sems)

    def wait(self, last):
        if self.jobs:
            @pl.when(last)
            def _():
                if not self.forwarded:
                    _exchange_forward(self.jobs, self.ins, self.outs, *self.sems)
                _exchange_wait(self.jobs, self.ins, self.outs, *self.sems)

    def in_specs(self):
        return self.any * len(self.srcs)

    def out_specs(self):
        return self.any * len(self.out_shape)

    def scratch(self):
        return _exchange_sems(len(self.jobs)) if self.jobs else []


def _attn_fwd(qkv, tri_fwd, name, rider=None):
    lp = qkv.shape[0]
    n_pairs = (N_HEADS * HEAD_DIM) // 128
    nb = lp // ABLK
    assert nb <= 128 and 2 * HEAD_DIM == 128
    scale = _attn_scale()
    nt = (((1,), (1,)), ((), ()))
    rider = rider or _Rider()

    def body(q_ref, k_ref, v_ref, tri_ref, *rest):
        (o_ref, rs_ref), (r_s, acc_s, rs_s) = rider.split(rest, 2, 3)
        rider.start(pl.program_id(0) == 0)
        lane = lax.broadcasted_iota(jnp.int32, (ABLK, 128), 1)
        row = lax.broadcasted_iota(jnp.int32, (ABLK, 128), 0)
        in_a = lane < HEAD_DIM
        causal = lane < row

        def begin(st, i):
            r_s[st] = jnp.zeros(r_s.shape[1:], F32)
            acc_s[st] = jnp.zeros(acc_s.shape[1:], F32)
            rs_s[st] = jnp.full(rs_s.shape[1:], GONE, F32)
            return q_ref[pl.ds(pl.multiple_of(i * ABLK, ABLK), ABLK), :] * scale

        def live(st):
            least = jnp.min(jnp.minimum(r_s[st, 0], r_s[st, 1]), axis=0, keepdims=True)
            return (least[0, 0] < EXP_ZERO_AT).astype(jnp.int32)

        def step(*args):
            _interleave(stages(*args))

        def stages(st, q2, kb0, nblk, diag):
            k0 = pl.multiple_of(kb0 * ABLK, ABLK)
            kbd = _stack_blocks(k_ref[pl.ds(k0, nblk * ABLK), :], nblk, in_a)
            vbd = _stack_blocks(v_ref[pl.ds(k0, nblk * ABLK), :], nblk, in_a)
            z = lax.dot_general(q2, kbd, nt, preferred_element_type=F32)
            ncol = 2 * nblk
            zt = [z[:, c * 128:(c + 1) * 128] for c in range(ncol)]
            if diag:
                zt = [jnp.where(causal, t, MASKED) if c >= ncol - 2 else t for c, t in enumerate(zt)]
            yield
            bounds = [ncol * j // ATT_SPLIT for j in range(ATT_SPLIT + 1)]
            ce = [None] * ncol
            for c0, c1 in reversed([b for b in zip(bounds[:-1], bounds[1:]) if b[0] < b[1]]):
                parts = [_split_hi_lo(_softplus(zt[c])) for c in range(c0, c1)]
                got = jnp.dot(jnp.concatenate(parts, axis=0), tri_ref[...], preferred_element_type=F32)
                for c in range(c0, c1):
                    ce[c] = got[(c - c0) * 128:(c - c0 + 1) * 128]
                yield
            rr = [r_s[st, 0], r_s[st, 1]]
            rsv = [rs_s[st, :, :128], rs_s[st, :, 128:]]
            ws = [None] * ncol
            for u in reversed(range(nblk)):
                for hh in range(2):
                    c = 2 * u + hh
                    ws[c] = jnp.exp(zt[c] - ce[c][:, :128] - rr[hh]).astype(BF16)
                    rsv[hh] = jnp.where(lane == kb0 + u, rr[hh], rsv[hh])
                    rr[hh] = rr[hh] + ce[c][:, 128:]
            acc_s[st] += jnp.dot(jnp.concatenate(ws, axis=1), vbd, preferred_element_type=F32)
            r_s[st, 0] = rr[0]
            r_s[st, 1] = rr[1]
            rs_s[st, :, :128] = rsv[0]
            rs_s[st, :, 128:] = rsv[1]

        def finish(st, q2, i, n_top):
            i_low = i - n_top
            n_grp = i_low // ATT_GROUP

            def more(c):
                return (c[0] < n_grp) & (c[1] > 0)

            def inner(c):
                step(st, q2, i_low - ATT_GROUP * (c[0] + 1), ATT_GROUP, False)
                return c[0] + 1, live(st)

            _, alive = lax.while_loop(more, inner, (jnp.int32(0), live(st)))
            rem = i_low - ATT_GROUP * n_grp
            for p in _pow2_below(ATT_GROUP):
                def last_steps(p=p):
                    step(st, q2, rem & (p - 1), p, False)
                    return live(st)

                alive = lax.cond(((rem & p) != 0) & (alive > 0), last_steps, lambda alive=alive: alive)

            q0 = pl.multiple_of(i * ABLK, ABLK)
            o_ref[pl.ds(q0, ABLK), :] = acc_s[st].astype(BF16)
            rs_ref[pl.ds(q0, ABLK), :] = rs_s[st]

        def single(i):
            q2 = begin(0, i)
            n_top = jnp.minimum(i, ATT_TOP)
            for t in range(ATT_TOP + 1):
                @pl.when(n_top == t)
                def _():
                    step(0, q2, i - t, t + 1, True)

            finish(0, q2, i, n_top)

        def pair(i):
            qa, qb = begin(0, i), begin(1, i + 1)
            _interleave(stages(0, qa, i - ATT_TOP, ATT_TOP + 1, True), stages(1, qb, i + 1 - ATT_TOP, ATT_TOP + 1, True))
            finish(0, qa, i, ATT_TOP)
            finish(1, qb, i + 1, ATT_TOP)

        n_head = min(ATT_TOP, nb)
        n_pair = (nb - n_head) // 2
        n_single = nb - 2 * n_pair

        def singles(k, carry):
            single(jnp.where(k < n_head, k, nb - n_single + k))
            return carry

        def pairs(j, carry):
            pair(n_head + 2 * j)
            return carry

        lax.fori_loop(0, n_single, singles, 0)
        lax.fori_loop(0, n_pair, pairs, 0)
        rider.wait(pl.program_id(0) == n_pairs - 1)

    col = lambda o: (lambda p: (0, p + o))
    return pl.pallas_call(
        body, out_shape=[SDS((lp, n_pairs * 128), BF16), SDS((lp, n_pairs * 256), F32)] + rider.out_shape,
        grid=(n_pairs,),
        in_specs=[pl.BlockSpec((lp, 128), col(0)), pl.BlockSpec((lp, 128), col(n_pairs)),
                  pl.BlockSpec((lp, 128), col(2 * n_pairs)), pl.BlockSpec((256, 256), lambda p: (0, 0))]
        + rider.in_specs(),
        out_specs=[pl.BlockSpec((lp, 128), col(0)), pl.BlockSpec((lp, 256), col(0))] + rider.out_specs(),
        scratch_shapes=[pltpu.VMEM((2, 2, ABLK, 128), F32), pltpu.VMEM((2, ABLK, 128), F32),
                        pltpu.VMEM((2, ABLK, 256), F32)] + rider.scratch(),
        name=name, compiler_params=_cp())(qkv, qkv, qkv, tri_fwd, *rider.srcs)


def _attn_bwd(qkv, d_out, rsave, tri_after, tri_before, name, rider=None):
    lp = qkv.shape[0]
    n_pairs = (N_HEADS * HEAD_DIM) // 128
    nb = lp // ABLK
    scale = _attn_scale()
    nt = (((1,), (1,)), ((), ()))
    tn = (((0,), (0,)), ((), ()))
    rider = rider or _Rider()

    def body(q_ref, k_ref, v_ref, do_ref, rs_ref, ta_ref, tb_ref, *rest):
        (o_ref,), (dk_s, dv_s, dq_s, pc_s) = rider.split(rest, 1, 4)
        rider.start(pl.program_id(0) == 0)
        lane = lax.broadcasted_iota(jnp.int32, (ABLK, 128), 1)
        row = lax.broadcasted_iota(jnp.int32, (ABLK, 128), 0)
        in_a = lane < HEAD_DIM
        causal = lane < row
        dk_s[...] = jnp.zeros_like(dk_s)
        dv_s[...] = jnp.zeros_like(dv_s)

        def begin(st, i):
            q0 = pl.multiple_of(i * ABLK, ABLK)
            q2 = q_ref[pl.ds(q0, ABLK), :] * scale
            do2 = do_ref[pl.ds(q0, ABLK), :]
            dq_s[st] = jnp.zeros(dq_s.shape[1:], F32)
            pc_s[st] = jnp.zeros(pc_s.shape[1:], F32)
            return dict(q0=q0, q2=q2, do2=do2, q_st=_head_halves(q2, in_a), do_st=_head_halves(do2, in_a))

        def step(*args):
            _interleave(stages(*args))

        def stages(st, blk, kb0, nblk, diag):
            q0, q2, do2 = blk["q0"], blk["q2"], blk["do2"]
            k0 = pl.multiple_of(kb0 * ABLK, ABLK)
            kbd = _stack_blocks(k_ref[pl.ds(k0, nblk * ABLK), :], nblk, in_a)
            vbd = _stack_blocks(v_ref[pl.ds(k0, nblk * ABLK), :], nblk, in_a)
            z = lax.dot_general(q2, kbd, nt, preferred_element_type=F32)
            dw = lax.dot_general(do2, vbd, nt, preferred_element_type=F32)
            ncol = 2 * nblk
            zt = [z[:, c * 128:(c + 1) * 128] for c in range(ncol)]
            if diag:
                zt = [jnp.where(causal, t, MASKED) if c >= ncol - 2 else t for c, t in enumerate(zt)]
            bounds = [ncol * j // ATT_SPLIT for j in range(ATT_SPLIT + 1)]
            batches = [range(c0, c1) for c0, c1 in zip(bounds[:-1], bounds[1:]) if c0 < c1]
            sps, ex, ws, dls, pe = [None] * ncol, [None] * ncol, [None] * ncol, [None] * ncol, [None] * ncol

            def mass(cols):
                for c in cols:
                    sps[c] = _softplus(zt[c])
                got = jnp.dot(jnp.concatenate([_split_hi_lo(sps[c]) for c in cols], axis=0), ta_ref[...],
                              preferred_element_type=F32)
                for j, c in enumerate(cols):
                    ex[c] = got[j * 128:(j + 1) * 128]

            def weights(cols):
                for c in cols:
                    u, hh = c // 2, c % 2
                    r_saved = jnp.sum(jnp.where(lane == kb0 + u, rs_ref[pl.ds(q0, ABLK), hh * 128:(hh + 1) * 128],
                                                0.0), axis=1, keepdims=True)
                    w = jnp.exp(zt[c] - ex[c] - r_saved)
                    ws[c] = w.astype(BF16)
                    dls[c] = dw[:, c * 128:(c + 1) * 128] * w
                got = jnp.dot(jnp.concatenate([_split_hi_lo(dls[c]) for c in cols], axis=0), tb_ref[...],
                              preferred_element_type=F32)
                for j, c in enumerate(cols):
                    pe[c] = got[j * 128:(j + 1) * 128]

            yield
            mass(batches[0])
            yield
            for j in range(len(batches)):
                if j + 1 < len(batches):
                    mass(batches[j + 1])
                    yield
                weights(batches[j])
                yield
            pc = [pc_s[st, 0], pc_s[st, 1]]
            dzs = []
            for c in range(ncol):
                hh = c % 2
                one_minus_beta = jnp.exp(-sps[c])
                dz = dls[c] * one_minus_beta - (pe[c][:, :128] + pc[hh]) * (1.0 - one_minus_beta)
                pc[hh] = pc[hh] + pe[c][:, 128:]
                dzs.append(dz.astype(BF16))
            pc_s[st, 0] = pc[0]
            pc_s[st, 1] = pc[1]
            dq_s[st] += jnp.dot(jnp.concatenate(dzs, axis=1), kbd, preferred_element_type=F32)
            by_head = lambda ts: jnp.concatenate([jnp.concatenate(ts[0::2], axis=1), jnp.concatenate(ts[1::2], axis=1)],
                                                 axis=0)
            rows = pl.ds(k0, nblk * ABLK)
            dk_s[rows, :] += lax.dot_general(by_head(dzs), blk["q_st"], tn, preferred_element_type=F32)
            dv_s[rows, :] += lax.dot_general(by_head(ws), blk["do_st"], tn, preferred_element_type=F32)

        def below(st, blk, i, n_top):
            gone = jnp.min(rs_ref[pl.ds(blk["q0"], ABLK), :], axis=0, keepdims=True) >= EXP_ZERO_AT
            lane1 = lax.broadcasted_iota(jnp.int32, (1, 128), 1)
            first = jnp.sum(jnp.where(gone[:, :128] & gone[:, 128:] & (lane1 < i), 1.0, 0.0)).astype(jnp.int32)
            i_low = i - n_top
            n_grp = i_low // ATT_GROUP
            rem = i_low - ATT_GROUP * n_grp
            for p in reversed(_pow2_below(ATT_GROUP)):
                @pl.when(((rem & p) != 0) & ((rem & (p - 1)) + p > first))
                def _():
                    step(st, blk, rem & (p - 1), p, False)

            def inner(g, c2):
                step(st, blk, rem + ATT_GROUP * g, ATT_GROUP, False)
                return c2

            lax.fori_loop(jnp.maximum(first - rem, 0) // ATT_GROUP, n_grp, inner, 0)

        def done(st, blk):
            o_ref[0, pl.ds(blk["q0"], ABLK), :] = (dq_s[st] * scale).astype(BF16)

        def single(i):
            blk = begin(0, i)
            n_top = jnp.minimum(i, ATT_TOP)
            below(0, blk, i, n_top)
            for t in range(ATT_TOP + 1):
                @pl.when(n_top == t)
                def _():
                    step(0, blk, i - t, t + 1, True)

            done(0, blk)

        def pair(i):
            a, b = begin(0, i), begin(1, i + 1)
            below(0, a, i, ATT_TOP)
            below(1, b, i + 1, ATT_TOP)
            _interleave(stages(0, a, i - ATT_TOP, ATT_TOP + 1, True), stages(1, b, i + 1 - ATT_TOP, ATT_TOP + 1, True))
            done(0, a)
            done(1, b)

        n_head = min(ATT_TOP, nb)
        n_pair = (nb - n_head) // 2
        n_single = nb - 2 * n_pair

        def singles(k, carry):
            single(jnp.where(k < n_head, k, nb - n_single + k))
            return carry

        def pairs(j, carry):
            pair(n_head + 2 * j)
            return carry

        lax.fori_loop(0, n_single, singles, 0)
        lax.fori_loop(0, n_pair, pairs, 0)
        o_ref[1] = dk_s[...].astype(BF16)
        o_ref[2] = dv_s[...].astype(BF16)
        rider.wait(pl.program_id(0) == n_pairs - 1)

    col = lambda o: (lambda p: (0, p + o))
    return pl.pallas_call(
        body, out_shape=[SDS((3, lp, n_pairs * 128), BF16)] + rider.out_shape, grid=(n_pairs,),
        in_specs=[pl.BlockSpec((lp, 128), col(0)), pl.BlockSpec((lp, 128), col(n_pairs)),
                  pl.BlockSpec((lp, 128), col(2 * n_pairs)), pl.BlockSpec((lp, 128), col(0)),
                  pl.BlockSpec((lp, 256), col(0)),
                  pl.BlockSpec((256, 128), lambda p: (0, 0)), pl.BlockSpec((256, 256), lambda p: (0, 0))]
        + rider.in_specs(),
        out_specs=[pl.BlockSpec((3, lp, 128), lambda p: (0, 0, p))] + rider.out_specs(),
        scratch_shapes=[pltpu.VMEM((lp, 128), F32), pltpu.VMEM((lp, 128), F32),
                        pltpu.VMEM((2, ABLK, 128), F32), pltpu.VMEM((2, 2, ABLK, 128), F32)] + rider.scratch(),
        name=name, compiler_params=_cp())(qkv, qkv, qkv, d_out, rsave, tri_after, tri_before, *rider.srcs)


def _conv_fwd_dw(cacg, w, b, name, rider=None):
    lp = cacg.shape[0]
    c = cacg.shape[1] // 2
    ncb = c // 128
    nchunk = lp // ABLK
    off = CONV_PAD - (CONV_K - 1)
    rider = rider or _Rider()

    def body(a_ref, g_ref, w_ref, b_ref, *rest):
        (y_ref,), (upad,) = rider.split(rest, 1, 1)
        rider.start(pl.program_id(0) == 0)
        upad[0:CONV_PAD, :] = jnp.zeros((CONV_PAD, 128), F32)

        def fill(ch, carry):
            base = pl.multiple_of(ch * ABLK, ABLK)
            upad[pl.ds(base + CONV_PAD, ABLK), :] = a_ref[pl.ds(base, ABLK), :] * _sigmoid(g_ref[pl.ds(base, ABLK), :])
            return carry

        lax.fori_loop(0, nchunk, fill, 0)

        def comp(ch, carry):
            base = pl.multiple_of(ch * ABLK, ABLK)
            acc = jnp.zeros((ABLK, 128), F32)
            for k in range(CONV_K):
                acc = acc + upad[pl.ds(base + (off + k), ABLK), :] * w_ref[k:k + 1, :]
            y_ref[pl.ds(base, ABLK), :] = acc + b_ref[...]
            return carry

        lax.fori_loop(0, nchunk, comp, 0)
        rider.wait(pl.program_id(0) == ncb - 1)

    return pl.pallas_call(
        body, out_shape=[SDS((lp, c), F32)] + rider.out_shape, grid=(ncb,),
        in_specs=[pl.BlockSpec((lp, 128), lambda j: (0, j)), pl.BlockSpec((lp, 128), lambda j: (0, j + ncb)),
                  pl.BlockSpec((CONV_PAD, 128), lambda j: (0, j)), pl.BlockSpec((1, 128), lambda j: (0, j))]
        + rider.in_specs(),
        out_specs=[pl.BlockSpec((lp, 128), lambda j: (0, j))] + rider.out_specs(),
        scratch_shapes=[pltpu.VMEM((lp + CONV_PAD, 128), F32)] + rider.scratch(), name=name,
        compiler_params=_cp())(cacg, cacg, w, b, *rider.srcs)


def _ln_parts(x, g, b):
    mu = jnp.mean(x, axis=-1, keepdims=True)
    xc = x - mu
    rstd = lax.rsqrt(jnp.mean(xc * xc, axis=-1, keepdims=True) + EPS)
    xh = xc * rstd
    return xh, rstd, xh * g + b


def _conv_fwd_ln(yc, g, b, name):
    lp, c = yc.shape
    tm = _tile(lp, EW_ROWS, 16)

    def body(y_ref, g_ref, b_ref, o_ref):
        _, _, ln = _ln_parts(y_ref[...], g_ref[...], b_ref[...])
        o_ref[...] = (ln * _sigmoid(ln)).astype(BF16)

    return pl.pallas_call(
        body, out_shape=SDS((lp, c), BF16), grid=(lp // tm,),
        in_specs=[pl.BlockSpec((tm, c), lambda i: (i, 0)), pl.BlockSpec((1, c), lambda i: (0, 0)),
                  pl.BlockSpec((1, c), lambda i: (0, 0))],
        out_specs=pl.BlockSpec((tm, c), lambda i: (i, 0)), name=name, compiler_params=_cp())(yc, g, b)


def _conv_bwd_ln(yc, dout, g, b, name):
    lp, c = yc.shape
    tm = _tile(lp, EW_ROWS, 16)

    def body(y_ref, d_ref, g_ref, b_ref, o_ref, dg_ref, db_ref):
        @pl.when(pl.program_id(0) == 0)
        def _():
            dg_ref[...] = jnp.zeros_like(dg_ref)
            db_ref[...] = jnp.zeros_like(db_ref)

        xh, rstd, ln = _ln_parts(y_ref[...], g_ref[...], b_ref[...])
        s = _sigmoid(ln)
        dln = d_ref[...] * (s * (1.0 + ln * (1.0 - s)))
        dg_ref[...] += _sum8(dln * xh)
        db_ref[...] += _sum8(dln)
        dxh = dln * g_ref[...]
        o_ref[...] = rstd * (dxh - jnp.mean(dxh, axis=-1, keepdims=True)
                             - xh * jnp.mean(dxh * xh, axis=-1, keepdims=True))

    return pl.pallas_call(
        body, out_shape=[SDS((lp, c), F32), SDS((8, c), F32), SDS((8, c), F32)], grid=(lp // tm,),
        in_specs=[pl.BlockSpec((tm, c), lambda i: (i, 0)), pl.BlockSpec((tm, c), lambda i: (i, 0)),
                  pl.BlockSpec((1, c), lambda i: (0, 0)), pl.BlockSpec((1, c), lambda i: (0, 0))],
        out_specs=[pl.BlockSpec((tm, c), lambda i: (i, 0)), pl.BlockSpec((8, c), lambda i: (0, 0)),
                   pl.BlockSpec((8, c), lambda i: (0, 0))],
        name=name, compiler_params=_cp())(yc, dout, g, b)


def _conv_bwd_dw(dyc, cacg, w, name):
    lp, c = dyc.shape
    ncb = c // 128
    nchunk = lp // ABLK
    off = CONV_PAD - (CONV_K - 1)

    def body(dy_ref, a_ref, g_ref, w_ref, dcc_ref, dw_ref, db_ref, upad, dypad, dwacc):
        upad[0:CONV_PAD, :] = jnp.zeros((CONV_PAD, 128), F32)
        dypad[lp:lp + CONV_PAD, :] = jnp.zeros((CONV_PAD, 128), F32)
        dwacc[...] = jnp.zeros_like(dwacc)
        db_ref[...] = jnp.zeros_like(db_ref)

        def fill(ch, carry):
            base = pl.multiple_of(ch * ABLK, ABLK)
            upad[pl.ds(base + CONV_PAD, ABLK), :] = a_ref[pl.ds(base, ABLK), :] * _sigmoid(g_ref[pl.ds(base, ABLK), :])
            dypad[pl.ds(base, ABLK), :] = dy_ref[pl.ds(base, ABLK), :]
            return carry

        lax.fori_loop(0, nchunk, fill, 0)

        def comp(ch, carry):
            base = pl.multiple_of(ch * ABLK, ABLK)
            dy = dy_ref[pl.ds(base, ABLK), :]
            du = jnp.zeros((ABLK, 128), F32)
            for k in range(CONV_K):
                du = du + dypad[pl.ds(base + (CONV_K - 1 - k), ABLK), :] * w_ref[k:k + 1, :]
                dwacc[k * 8:(k + 1) * 8, :] += _sum8(dy * upad[pl.ds(base + (off + k), ABLK), :])
            db_ref[...] += _sum8(dy)
            a = a_ref[pl.ds(base, ABLK), :]
            s = _sigmoid(g_ref[pl.ds(base, ABLK), :])
            dcc_ref[0, pl.ds(base, ABLK), :] = (du * s).astype(BF16)
            dcc_ref[1, pl.ds(base, ABLK), :] = (du * a * (s * (1.0 - s))).astype(BF16)
            return carry

        lax.fori_loop(0, nchunk, comp, 0)
        dw_ref[...] = dwacc[...].reshape(CONV_PAD, 8, 128).sum(axis=1)

    return pl.pallas_call(
        body, out_shape=[SDS((2, lp, c), BF16), SDS((CONV_PAD, c), F32), SDS((8, c), F32)],
        grid=(ncb,),
        in_specs=[pl.BlockSpec((lp, 128), lambda j: (0, j)), pl.BlockSpec((lp, 128), lambda j: (0, j)),
                  pl.BlockSpec((lp, 128), lambda j: (0, j + ncb)), pl.BlockSpec((CONV_PAD, 128), lambda j: (0, j))],
        out_specs=[pl.BlockSpec((2, lp, 128), lambda j: (0, 0, j)),
                   pl.BlockSpec((CONV_PAD, 128), lambda j: (0, j)), pl.BlockSpec((8, 128), lambda j: (0, j))],
        scratch_shapes=[pltpu.VMEM((lp + CONV_PAD, 128), F32), pltpu.VMEM((lp + CONV_PAD, 128), F32),
                        pltpu.VMEM((CONV_PAD * 8, 128), F32)],
        name=name, compiler_params=_cp())(dyc, cacg, cacg, w)


def _mesh_pos():
    x, y, c = lax.axis_index("x"), lax.axis_index("y"), lax.axis_index("c")
    return x, y, c


def _peer(pos, r):
    x, y, c = pos
    px = (1 - x) if (r >> 2) & 1 else x
    py = (1 - y) if (r >> 1) & 1 else y
    pc = (1 - c) if r & 1 else c
    return (px, py, pc), 4 * px + 2 * py + pc


SIBLING = 1
OTHER_CHIPS = (2, 4, 6)
VIA_SIBLING = (3, 5, 7)


class _Job:
    def __init__(self, src, dst, scatter, src_layer=None, dst_layer=None):
        self.src, self.dst, self.scatter, self.src_layer, self.dst_layer = src, dst, scatter, src_layer, dst_layer

    def src_view(self, ins, idx):
        v = ins[self.src] if self.src_layer is None else ins[self.src].at[self.src_layer]
        return v.at[idx] if self.scatter else v

    def dst_view(self, outs, slot):
        v = outs[self.dst] if self.dst_layer is None else outs[self.dst].at[self.dst_layer]
        return v.at[slot]


def _remote(job, j, r, src, dst, to, send, recv):
    return pltpu.make_async_remote_copy(src_ref=src, dst_ref=dst, send_sem=send.at[j, r - 1], recv_sem=recv.at[j, r - 1],
                                        device_id=to, device_id_type=pl.DeviceIdType.MESH)


def _exchange_start(jobs, ins, outs, send, recv, loc):
    pos = _mesh_pos()
    me = 4 * pos[0] + 2 * pos[1] + pos[2]
    for j, job in enumerate(jobs):
        pltpu.make_async_copy(job.src_view(ins, me), job.dst_view(outs, me), loc.at[j]).start()
        for r in (range(1, N_DEV) if job.scatter else (SIBLING,) + OTHER_CHIPS):
            peer, peer_idx = _peer(pos, r)
            _remote(job, j, r, job.src_view(ins, peer_idx), job.dst_view(outs, me), peer, send, recv).start()


def _exchange_forward(jobs, ins, outs, send, recv, loc):
    pos = _mesh_pos()
    sibling, _ = _peer(pos, SIBLING)
    for j, job in enumerate(jobs):
        if job.scatter:
            continue
        for r in OTHER_CHIPS:
            peer, peer_idx = _peer(pos, r)
            slot = job.dst_view(outs, peer_idx)
            _remote(job, j, r, job.src_view(ins, peer_idx), slot, peer, send, recv).wait_recv()
            _remote(job, j, r ^ SIBLING, slot, slot, sibling, send, recv).start()


def _exchange_wait(jobs, ins, outs, send, recv, loc):
    pos = _mesh_pos()
    me = 4 * pos[0] + 2 * pos[1] + pos[2]
    for j, job in enumerate(jobs):
        for r in range(1, N_DEV):
            peer, peer_idx = _peer(pos, r)
            cp = _remote(job, j, r, job.src_view(ins, peer_idx), job.dst_view(outs, peer_idx), peer, send, recv)
            if job.scatter or r not in OTHER_CHIPS:
                cp.wait_recv()
            cp.wait_send()
        pltpu.make_async_copy(job.src_view(ins, me), job.dst_view(outs, me), loc.at[j]).wait()


def _exchange_sems(n_jobs):
    return [pltpu.SemaphoreType.DMA((n_jobs, N_DEV - 1)), pltpu.SemaphoreType.DMA((n_jobs, N_DEV - 1)),
            pltpu.SemaphoreType.DMA((n_jobs,))]


def _exchange(jobs, arrs, out_shape, name):
    n_in, n_out = len(arrs), len(out_shape)
    any_spec = pl.BlockSpec(memory_space=pl.ANY)

    def body(*refs):
        ins, outs, sems = refs[:n_in], refs[n_in:n_in + n_out], refs[n_in + n_out:]
        _exchange_start(jobs, ins, outs, *sems)
        _exchange_forward(jobs, ins, outs, *sems)
        _exchange_wait(jobs, ins, outs, *sems)

    return pl.pallas_call(
        body, out_shape=out_shape, in_specs=[any_spec] * n_in, out_specs=[any_spec] * n_out,
        scratch_shapes=_exchange_sems(len(jobs)),
        name=name, compiler_params=pltpu.CompilerParams(has_side_effects=True))(*arrs)


def _all_reduce_small(placed, cuts, rows, w, name):
    n_in = len(placed)

    def body(*refs):
        in_refs, o_refs = refs[:n_in], refs[n_in:n_in + len(cuts)]
        buf, send, recv = refs[n_in + len(cuts):]
        pos = _mesh_pos()
        me = 4 * pos[0] + 2 * pos[1] + pos[2]
        buf[me] = jnp.zeros((rows, w), F32)
        for ref, (arr, row0, col0, is_partial) in zip(in_refs, placed):
            val = ref[...].sum(axis=0, keepdims=True) if is_partial else ref[...]
            buf[me, row0:row0 + val.shape[0], col0:col0 + val.shape[1]] = val
        for r in range(1, N_DEV):
            peer, _ = _peer(pos, r)
            pltpu.make_async_remote_copy(src_ref=buf.at[me], dst_ref=buf.at[me], send_sem=send.at[r - 1],
                                         recv_sem=recv.at[r - 1], device_id=peer,
                                         device_id_type=pl.DeviceIdType.MESH).start()
        for r in range(1, N_DEV):
            peer, peer_idx = _peer(pos, r)
            cp = pltpu.make_async_remote_copy(src_ref=buf.at[me], dst_ref=buf.at[peer_idx], send_sem=send.at[r - 1],
                                              recv_sem=recv.at[r - 1], device_id=peer,
                                              device_id_type=pl.DeviceIdType.MESH)
            cp.wait_recv()
            cp.wait_send()
        acc = buf[0]
        for dev in range(1, N_DEV):
            acc = acc + buf[dev]
        for o_ref, (_, pieces) in zip(o_refs, cuts):
            for index, row0, nrows, col0, ncols in pieces:
                o_ref[index] = acc[row0:row0 + nrows, col0:col0 + ncols].reshape(o_ref.at[index].shape)

    vmem = pl.BlockSpec(memory_space=pltpu.VMEM)
    return pl.pallas_call(
        body, out_shape=[SDS(shape, F32) for shape, _ in cuts], in_specs=[vmem] * n_in, out_specs=[vmem] * len(cuts),
        scratch_shapes=[pltpu.VMEM((N_DEV, rows, w), F32), pltpu.SemaphoreType.DMA((N_DEV - 1,)),
                        pltpu.SemaphoreType.DMA((N_DEV - 1,))],
        name=name, compiler_params=pltpu.CompilerParams(has_side_effects=True))(*[p[0] for p in placed])


def _adamw_math(w, g, m, v):
    m = ADAM_B1 * m + (1.0 - ADAM_B1) * g
    v = ADAM_B2 * v + (1.0 - ADAM_B2) * (g * g)
    m_hat = m / (1.0 - ADAM_B1 ** ADAM_STEP)
    v_hat = v / (1.0 - ADAM_B2 ** ADAM_STEP)
    delta = -ADAM_LR * (m_hat / (jnp.sqrt(v_hat) + ADAM_EPS) + ADAM_WD * w)
    return delta, m, v


def _adamw_shard(parts, w, m, v, name):
    depth = len(parts)
    _, rr, cc = parts[0].shape
    tr = _tile(rr, 256, 8)
    nt = rr // tr
    part_block, blk = (N_DEV, tr, cc), pl.BlockSpec((None, tr, cc), lambda l, i: (l, i, 0))

    def body(*refs):
        p_refs = refs[:depth]
        w_ref, m_ref, v_ref, g_out, d_out, m_out, v_out = refs[depth:]
        for li in range(depth):
            @pl.when(pl.program_id(0) == li)
            def _(p_ref=p_refs[li]):
                g = p_ref[0].astype(F32)
                for dev in range(1, N_DEV):
                    g = g + p_ref[dev].astype(F32)
                delta, mm, vv = _adamw_math(w_ref[...], g, m_ref[...], v_ref[...])
                g_out[...] = g
                d_out[...] = delta
                m_out[...] = mm
                v_out[...] = vv

    def part_spec(li):
        return pl.BlockSpec(part_block, lambda l, i: (0, jnp.where(l == li, i, jnp.where(l < li, 0, nt - 1)), 0))

    return pl.pallas_call(
        body, out_shape=[SDS(w.shape, F32)] * 4, grid=(depth, nt),
        in_specs=[part_spec(li) for li in range(depth)] + [blk, blk, blk],
        out_specs=[blk] * 4, name=name, compiler_params=_cp())(*parts, w, m, v)


def _adamw_small(gs, ws, ms, vs, name):
    n = len(gs)

    def body(*refs):
        g_refs, w_refs, m_refs, v_refs = (refs[k * n:(k + 1) * n] for k in range(4))
        outs = refs[4 * n:]
        for k in range(n):
            delta, mm, vv = _adamw_math(w_refs[k][...], g_refs[k][...], m_refs[k][...], v_refs[k][...])
            outs[k][...] = delta
            outs[n + k][...] = mm
            outs[2 * n + k][...] = vv

    res = pl.pallas_call(body, out_shape=[SDS(w.shape, F32) for w in ws] * 3, name=name,
                         compiler_params=_cp())(*gs, *ws, *ms, *vs)
    return res[:n], res[n:2 * n], res[2 * n:]


def _from_cols(t):
    return jnp.transpose(t, (1, 0, 2)).reshape(t.shape[1], N_DEV * t.shape[2])


def _swap(t):
    return jnp.swapaxes(t, -1, -2)


def kernel(x, meta_tokens, mix_norm_g, w_in, conv_dw_w, conv_dw_b, conv_ln_g, conv_ln_b, w_out, ffn_norm_g, w_gate, w_up, w_down, final_norm_g, loss_target, m_meta_tokens, m_mix_norm_g, m_w_in, m_conv_dw_w, m_conv_dw_b, m_conv_ln_g, m_conv_ln_b, m_w_out, m_ffn_norm_g, m_w_gate, m_w_up, m_w_down, m_final_norm_g, v_meta_tokens, v_mix_norm_g, v_w_in, v_conv_dw_w, v_conv_dw_b, v_conv_ln_g, v_conv_ln_b, v_w_out, v_ffn_norm_g, v_w_gate, v_w_up, v_w_down, v_final_norm_g):
    depth, d, in_shard = w_in.shape
    seq = x.shape[1]
    sb = N_HEADS * HEAD_DIM
    cc = conv_dw_w.shape[2] * N_DEV
    ff = w_gate.shape[2] * N_DEV
    assert in_shard * N_DEV == 3 * sb + 2 * cc and x.shape[0] == 1
    lr = N_META + seq
    lp = -(-lr // ABLK) * ABLK
    me = 4 * lax.axis_index("x") + 2 * lax.axis_index("y") + lax.axis_index("c")

    big_names = ("w_in", "w_out", "w_gate", "w_up", "w_down")
    transposed = {"w_in": True, "w_out": False, "w_gate": True, "w_up": True, "w_down": False}
    shard = dict(w_in=_swap(w_in).astype(BF16), w_out=w_out.astype(BF16), w_gate=_swap(w_gate).astype(BF16),
                 w_up=_swap(w_up).astype(BF16), w_down=w_down.astype(BF16))

    def gather_of(keys):
        names = sorted({n for n, _ in keys}, key=big_names.index)
        jobs = [_Job(names.index(n), j, False, src_layer=i) for j, (n, i) in enumerate(keys)]
        return jobs, [shard[n] for n in names], [SDS((N_DEV,) + shard[n].shape[1:], BF16) for n, _ in keys]

    first_keys = [("w_in", 0)]

    def riding(keys):
        return _Rider(*gather_of(keys)) if keys else None

    def receive(keys, arrays):
        for (n, li), t in zip(keys, arrays):
            wl[li][n] = t.reshape(-1, d)

    jobs, srcs, out_shape = gather_of(first_keys)
    for extra in (meta_tokens, conv_dw_w):
        jobs.append(_Job(len(srcs), len(out_shape), False))
        srcs.append(extra)
        out_shape.append(SDS((N_DEV,) + extra.shape, F32))
    gathered = _exchange(jobs, srcs, out_shape, "gather_first")
    wl = [dict() for _ in range(depth)]
    receive(first_keys, gathered)
    meta_full = _from_cols(gathered[-2])
    taps = jnp.transpose(gathered[-1], (1, 2, 0, 3)).reshape(depth, CONV_K, cc)
    taps = jnp.pad(taps, ((0, 0), (0, CONV_PAD - CONV_K), (0, 0)))
    tri_fwd, tri_after, tri_before = _tri_consts()

    h = jnp.concatenate([meta_full, x[0], jnp.zeros((lp - lr, d), F32)], axis=0)
    saved = []
    for i in range(depth):
        p = wl[i]
        sv = dict(h_in=h)
        qkv, hn = _rms_mm(h, mix_norm_g[i:i + 1], p["w_in"], 0, 3 * sb, BF16, f"proj_qkv_{i}", True)
        keys = [("w_out", 0)] if i == 0 else []
        cacg, *arrived = _rms_mm(h, mix_norm_g[i:i + 1], p["w_in"], 3 * sb, 2 * cc, F32, f"proj_conv_{i}", False,
                                 riding(keys))
        receive(keys, arrived)
        keys = [("w_gate", i), ("w_up", i)]
        attn, rsave, *arrived = _attn_fwd(qkv, tri_fwd, f"attn_fwd_{i}", riding(keys))
        receive(keys, arrived)
        keys = [("w_down", 0)] if i == 0 else []
        yc, *arrived = _conv_fwd_dw(cacg, taps[i], conv_dw_b[i:i + 1], f"conv_fwd_dw_{i}", riding(keys))
        receive(keys, arrived)
        conv = _conv_fwd_ln(yc, conv_ln_g[i:i + 1], conv_ln_b[i:i + 1], f"conv_fwd_ln_{i}")
        h = _mix_out(attn, conv, p["w_out"], h, f"mix_out_{i}")
        sv.update(qkv=qkv, hn=hn, cacg=cacg, rsave=rsave, yc=yc, attn=attn, conv=conv, h_mid=h)
        keys = [("w_in", i + 1), ("w_out", i + 1), ("w_down", i + 1)] if i + 1 < depth else []
        h, hn2, act, *arrived = _ffn_fwd(h, ffn_norm_g[i:i + 1], p["w_gate"], p["w_up"], p["w_down"], f"ffn_fwd_{i}",
                                         riding(keys))
        receive(keys, arrived)
        sv.update(hn2=hn2, act=act)
        saved.append(sv)

    tpad = jnp.pad(loss_target[0], ((N_META, lp - lr), (0, 0)))
    dh, dg_final, loss_part = _loss_head(h, tpad, final_norm_g.reshape(1, d), seq, "loss_head")
    loss = lax.psum(loss_part[0, 0], MESH_AXES)

    parts = {}

    def sending(items):
        srcs = [t.reshape(N_DEV, t.shape[0] // N_DEV, d) for _, t in items]
        return _Rider([_Job(j, j, True) for j in range(len(items))], srcs, [SDS(t.shape, BF16) for t in srcs])

    def arrive(items, arrays):
        parts.update({key: t for (key, _), t in zip(items, arrays)})

    grads = [None] * depth
    from_above = []
    for i in reversed(range(depth)):
        p, sv = wl[i], saved[i]
        d_g, d_u, *got = _ffn_bwd_act(dh, sv["hn2"], p["w_gate"], p["w_up"], p["w_down"], f"ffn_bwd_act_{i}",
                                      sending(from_above))
        arrive(from_above, got)
        gw_down = _mm_tn(sv["act"], dh, f"grad_w_down_{i}")
        gw_gate = _mm_tn(d_g, sv["hn2"], f"grad_w_gate_{i}")
        gw_up = _mm_tn(d_u, sv["hn2"], f"grad_w_up_{i}")
        items = [(("w_down", i), gw_down)]
        dh, dg_ffn, *got = _ffn_bwd_in(d_g, d_u, p["w_gate"], p["w_up"], sv["h_mid"], ffn_norm_g[i:i + 1], dh,
                                       f"ffn_bwd_in_{i}", sending(items))
        arrive(items, got)
        gw_out = jnp.concatenate([_mm_tn(sv["attn"], dh, f"grad_w_out_attn_{i}"),
                                  _mm_tn(sv["conv"], dh, f"grad_w_out_conv_{i}")], axis=0)
        d_attn = _mm_nt(dh, p["w_out"], 0, sb, BF16, f"mix_bwd_attn_{i}")
        d_conv = _mm_nt(dh, p["w_out"], sb, cc, F32, f"mix_bwd_conv_{i}")
        items = [(("w_gate", i), gw_gate), (("w_up", i), gw_up)]
        dqkv, *got = _attn_bwd(sv["qkv"], d_attn, sv["rsave"], tri_after, tri_before, f"attn_bwd_{i}", sending(items))
        arrive(items, got)
        dyc, dg_ln, db_ln = _conv_bwd_ln(sv["yc"], d_conv, conv_ln_g[i:i + 1], conv_ln_b[i:i + 1], f"conv_bwd_ln_{i}")
        dcc, g_taps_i, db_conv = _conv_bwd_dw(dyc, sv["cacg"], taps[i], f"conv_bwd_dw_{i}")
        items = [(("w_out", i), gw_out)] if i == 0 else []
        gw_in, *got = _grad_w_in_t(dqkv, dcc, sv["hn"], f"grad_w_in_{i}", sending(items))
        arrive(items, got)
        from_above = [(("w_in", i), gw_in)] + ([(("w_out", i), gw_out)] if i > 0 else [])
        items = from_above if i == 0 else []
        dh, dg_mix, *got = _mix_bwd_in(dqkv, dcc, p["w_in"], sv["h_in"], mix_norm_g[i:i + 1], dh, f"mix_bwd_in_{i}",
                                       sending(items))
        arrive(items, got)
        grads[i] = dict(taps=g_taps_i, dg_mix=dg_mix, dg_ffn=dg_ffn, dg_ln=dg_ln, db_ln=db_ln, db_conv=db_conv)
    grad_x = dh[N_META:lr][None]

    per = d // depth
    assert depth * cc <= d and d % depth == 0 and per % 128 == 0
    placed, cuts, row = [], [], 0
    for key in ("dg_mix", "dg_ffn"):
        placed += [(grads[i][key], row + i, 0, True) for i in range(depth)]
        cuts.append(((depth, d), [(slice(0, depth), row, depth, 0, d)]))
        row += depth
    placed.append((dg_final, row, 0, True))
    cuts.append(((1, d), [(slice(0, 1), row, 1, 0, d)]))
    row += 1
    for key in ("db_conv", "dg_ln", "db_ln"):
        placed += [(grads[i][key], row, i * per, True) for i in range(depth)]
        cuts.append(((depth, cc), [(slice(i, i + 1), row, 1, i * per, cc) for i in range(depth)]))
        row += 1
    placed += [(grads[i]["taps"], row, i * per, False) for i in range(depth)]
    cuts.append(((depth, CONV_PAD, cc), [(i, row, CONV_PAD, i * per, cc) for i in range(depth)]))
    row += CONV_PAD
    placed.append((dh[:N_META], row, 0, False))
    cuts.append(((N_META, d), [(slice(0, N_META), row, N_META, 0, d)]))
    row += N_META
    g_mix, g_ffn, g_final, g_cb, g_lg, g_lb, g_taps_full, g_meta_full = _all_reduce_small(
        placed, cuts, -(-row // 8) * 8, d, "reduce_small")
    csh = cc // N_DEV
    g_taps_own = lax.dynamic_slice_in_dim(g_taps_full[:, :CONV_K], me * csh, csh, axis=2)
    msh = d // N_DEV
    g_meta_own = lax.dynamic_slice_in_dim(g_meta_full, me * msh, msh, axis=1)

    row1 = lambda t: t.reshape(1, d)
    small_g = [g_meta_own, g_mix, g_taps_own, g_cb, g_lg, g_lb, g_ffn, g_final]
    small_w = [meta_tokens, mix_norm_g, conv_dw_w, conv_dw_b, conv_ln_g, conv_ln_b, ffn_norm_g, row1(final_norm_g)]
    small_m = [m_meta_tokens, m_mix_norm_g, m_conv_dw_w, m_conv_dw_b, m_conv_ln_g, m_conv_ln_b, m_ffn_norm_g,
               row1(m_final_norm_g)]
    small_v = [v_meta_tokens, v_mix_norm_g, v_conv_dw_w, v_conv_dw_b, v_conv_ln_g, v_conv_ln_b, v_ffn_norm_g,
               row1(v_final_norm_g)]
    s_delta, s_m, s_v = _adamw_small(small_g, small_w, small_m, small_v, "adamw_small")
    unrow = lambda ts: list(ts[:-1]) + [ts[-1].reshape(d)]
    small_g, s_delta, s_m, s_v = unrow(small_g), unrow(s_delta), unrow(s_m), unrow(s_v)

    big = []
    for n, w, m, v in zip(big_names, (w_in, w_out, w_gate, w_up, w_down), (m_w_in, m_w_out, m_w_gate, m_w_up, m_w_down),
                          (v_w_in, v_w_out, v_w_gate, v_w_up, v_w_down)):
        fix = _swap if transposed[n] else (lambda t: t)
        res = _adamw_shard([parts[(n, i)] for i in range(depth)], fix(w), fix(m), fix(v), f"adamw_{n}")
        big.append([fix(t) for t in res])
    b_in, b_out, b_gate, b_up, b_down = big

    def ordered(k, smalls):
        s_meta, s_mix, s_taps, s_cb, s_lg, s_lb, s_ffn, s_final = smalls
        return [s_meta, s_mix, b_in[k], s_taps, s_cb, s_lg, s_lb, b_out[k], s_ffn, b_gate[k], b_up[k], b_down[k], s_final]

    return (loss, grad_x, *ordered(0, small_g), *ordered(1, s_delta), *ordered(2, s_m), *ordered(3, s_v))
```

```python
import functools
import math

import jax
import jax.numpy as jnp
from jax import lax
from jax.experimental import pallas as pl
from jax.experimental.pallas import tpu as pltpu

F32 = jnp.float32
BF16 = jnp.bfloat16
SDS = jax.ShapeDtypeStruct

N_META = 16
N_HEADS = 8
HEAD_DIM = 64
CONV_K = 31
CONV_PAD = 32
ABLK = 128
ATT_GROUP = 4
ATT_SPLIT = 2
ATT_TOP = 2
EXP_ZERO_AT = 104.0
GONE = 1e30
MASKED = -1e30
FF_CHUNK = 256
FF_BLOCK = 1408
FF_ROWS = 528
EPS = 1e-6
N_DEV = 8
MESH_AXES = ("x", "y", "c")
ADAM_LR = 0.001
ADAM_B1 = 0.9
ADAM_B2 = 0.999
ADAM_EPS = 1e-08
ADAM_WD = 0.01
ADAM_STEP = 10
MIB = 1 << 20
VMEM_LIMIT_MIB = 48
FFN_FWD_VMEM_MIB = 56
MM_ROWS = 1056
EW_ROWS = 528


def _cp(mib=None):
    return pltpu.CompilerParams(vmem_limit_bytes=(mib or VMEM_LIMIT_MIB) * MIB)


def _tile(n, cap, mult):
    best = None
    for t in range(mult, min(n, cap) + 1, mult):
        if n % t == 0:
            best = t
    assert best is not None, (n, cap, mult)
    return best


def _sum8(v):
    r, c = v.shape
    return v.reshape(r // 8, 8, c).sum(axis=0)


def _sigmoid(x):
    return 1.0 / (1.0 + jnp.exp(-x))


def _ff_block(f):
    fb = _tile(f, FF_BLOCK, 128)
    return fb, [(s0, min(FF_CHUNK, fb - s0)) for s0 in range(0, fb, FF_CHUNK)]


NT = (((1,), (1,)), ((), ()))
TN = (((0,), (0,)), ((), ()))


def _rms_mm(h, g, wt, w_row0, n_cols, out_dtype, name, hn_out, rider=None):
    lp, d = h.shape
    tm = _tile(lp, MM_ROWS, 16)
    tn = _tile(math.gcd(n_cols, w_row0), 768, 128)
    off = w_row0 // tn
    ni, nj = lp // tm, n_cols // tn
    rider = rider or _Rider()

    def body(h_ref, g_ref, w_ref, *rest):
        outs, (hn_s,) = rider.split(rest, 2 if hn_out else 1, 1)
        o_ref = outs[0]
        hn_ref = outs[1] if hn_out else None
        rider.start((pl.program_id(0) == 0) & (pl.program_id(1) == 0))
        if ni > 1:
            rider.forward((pl.program_id(0) == ni - 1) & (pl.program_id(1) == 0))

        @pl.when(pl.program_id(1) == 0)
        def _():
            x = h_ref[...]
            r = lax.rsqrt(jnp.mean(x * x, axis=-1, keepdims=True) + EPS)
            hn = ((x * r) * g_ref[...]).astype(BF16)
            hn_s[...] = hn
            if hn_out:
                hn_ref[...] = hn

        o_ref[...] = lax.dot_general(hn_s[...], w_ref[...], NT, preferred_element_type=F32).astype(out_dtype)
        rider.wait((pl.program_id(0) == ni - 1) & (pl.program_id(1) == nj - 1))

    out_shape = [SDS((lp, n_cols), out_dtype)]
    out_specs = [pl.BlockSpec((tm, tn), lambda i, j: (i, j))]
    if hn_out:
        out_shape.append(SDS((lp, d), BF16))
        out_specs.append(pl.BlockSpec((tm, d), lambda i, j: (i, 0)))
    return pl.pallas_call(
        body, out_shape=out_shape + rider.out_shape, grid=(ni, nj),
        in_specs=[pl.BlockSpec((tm, d), lambda i, j: (i, 0)),
                  pl.BlockSpec((1, d), lambda i, j: (0, 0)),
                  pl.BlockSpec((tn, d), lambda i, j: (j + off, 0))] + rider.in_specs(),
        out_specs=out_specs + rider.out_specs(), scratch_shapes=[pltpu.VMEM((tm, d), BF16)] + rider.scratch(),
        name=name, compiler_params=_cp())(h, g, wt, *rider.srcs)


def _mm_nt(a, b, b_row0, n_out, out_dtype, name):
    m, k = a.shape
    tm = _tile(m, MM_ROWS, 16)
    tn = _tile(math.gcd(n_out, b_row0), 512, 128)
    off = b_row0 // tn

    def body(a_ref, b_ref, o_ref):
        o_ref[...] = lax.dot_general(a_ref[...].astype(BF16), b_ref[...], NT,
                                     preferred_element_type=F32).astype(out_dtype)

    return pl.pallas_call(
        body, out_shape=SDS((m, n_out), out_dtype), grid=(m // tm, n_out // tn),
        in_specs=[pl.BlockSpec((tm, k), lambda i, j: (i, 0)), pl.BlockSpec((tn, k), lambda i, j: (j + off, 0))],
        out_specs=pl.BlockSpec((tm, tn), lambda i, j: (i, j)), name=name, compiler_params=_cp())(a, b)


def _mix_out(attn, conv, w, res, name):
    m, ka = attn.shape
    kc = conv.shape[1]
    n = w.shape[1]
    assert ka == kc
    tm = _tile(m, MM_ROWS, 16)
    tn = _tile(n, 512, 128)

    def body(a_ref, c_ref, wa_ref, wc_ref, r_ref, o_ref):
        o_ref[...] = (r_ref[...] + jnp.dot(a_ref[...], wa_ref[...], preferred_element_type=F32)
                      + jnp.dot(c_ref[...], wc_ref[...], preferred_element_type=F32))

    return pl.pallas_call(
        body, out_shape=SDS((m, n), F32), grid=(m // tm, n // tn),
        in_specs=[pl.BlockSpec((tm, ka), lambda i, j: (i, 0)), pl.BlockSpec((tm, kc), lambda i, j: (i, 0)),
                  pl.BlockSpec((ka, tn), lambda i, j: (0, j)), pl.BlockSpec((kc, tn), lambda i, j: (1, j)),
                  pl.BlockSpec((tm, tn), lambda i, j: (i, j))],
        out_specs=pl.BlockSpec((tm, tn), lambda i, j: (i, j)), name=name, compiler_params=_cp())(attn, conv, w, w, res)


def _mm_tn(a, b, name):
    l, m = a.shape
    n = b.shape[1]
    tm = _tile(m, 1408, 128)
    tn = _tile(n, 1024, 128)
    tl = _tile(l, 1408, 128)
    nl = l // tl

    def body(a_ref, b_ref, o_ref, acc):
        @pl.when(pl.program_id(2) == 0)
        def _():
            acc[...] = jnp.zeros_like(acc)

        acc[...] += lax.dot_general(a_ref[...].astype(BF16), b_ref[...].astype(BF16), TN, preferred_element_type=F32)

        @pl.when(pl.program_id(2) == nl - 1)
        def _():
            o_ref[...] = acc[...].astype(BF16)

    return pl.pallas_call(
        body, out_shape=SDS((m, n), BF16), grid=(m // tm, n // tn, nl),
        in_specs=[pl.BlockSpec((tl, tm), lambda i, j, s: (s, i)),
                  pl.BlockSpec((tl, tn), lambda i, j, s: (s, j))],
        out_specs=pl.BlockSpec((tm, tn), lambda i, j, s: (i, j)),
        scratch_shapes=[pltpu.VMEM((tm, tn), F32)], name=name, compiler_params=_cp())(a, b)


def _grad_w_in_t(dqkv, dcc, hn, name, rider=None):
    nq, l, w = dqkv.shape
    nc = dcc.shape[0]
    d = hn.shape[1]
    assert dcc.shape[2] == w
    tn = _tile(d, 1024, 128)
    tl = _tile(l, 1408, 128)
    nl = l // tl
    nj = d // tn
    rider = rider or _Rider()

    def body(q_ref, c_ref, b_ref, *rest):
        (o_ref,), (acc,) = rider.split(rest, 1, 1)
        p, jj, s = pl.program_id(0), pl.program_id(1), pl.program_id(2)
        rider.start((p == 0) & (jj == 0) & (s == 0))

        @pl.when(s == 0)
        def _():
            acc[...] = jnp.zeros_like(acc)

        @pl.when(p < nq)
        def _():
            acc[...] += lax.dot_general(q_ref[...], b_ref[...], TN, preferred_element_type=F32)

        @pl.when(p >= nq)
        def _():
            acc[...] += lax.dot_general(c_ref[...], b_ref[...], TN, preferred_element_type=F32)

        @pl.when(s == nl - 1)
        def _():
            o_ref[...] = acc[...].astype(BF16)

        rider.wait((p == nq + nc - 1) & (jj == nj - 1) & (s == nl - 1))

    out, *arrived = pl.pallas_call(
        body, out_shape=[SDS((nq + nc, w, d), BF16)] + rider.out_shape, grid=(nq + nc, nj, nl),
        in_specs=[pl.BlockSpec((None, tl, w), lambda p, j, s: (jnp.minimum(p, nq - 1), s, 0)),
                  pl.BlockSpec((None, tl, w), lambda p, j, s: (jnp.maximum(p - nq, 0), s, 0)),
                  pl.BlockSpec((tl, tn), lambda p, j, s: (s, j))] + rider.in_specs(),
        out_specs=[pl.BlockSpec((None, w, tn), lambda p, j, s: (p, 0, j))] + rider.out_specs(),
        scratch_shapes=[pltpu.VMEM((w, tn), F32)] + rider.scratch(), name=name,
        compiler_params=_cp())(dqkv, dcc, hn, *rider.srcs)
    return [out.reshape((nq + nc) * w, d)] + arrived


def _rms_bwd_tail(acc, h_ref, g_ref, dres_ref, o_ref, dg_ref):
    x = h_ref[...]
    r = lax.rsqrt(jnp.mean(x * x, axis=-1, keepdims=True) + EPS)
    xr = x * r
    dyv = acc[...]
    gy = dyv * g_ref[...]
    o_ref[...] = dres_ref[...] + r * (gy - xr * jnp.mean(gy * xr, axis=-1, keepdims=True))
    dg_ref[...] += _sum8(dyv * xr)


def _mix_bwd_in(dqkv, dcc, wt, h, g, dres, name, rider=None):
    nq, lp, w = dqkv.shape
    nc = dcc.shape[0]
    d = h.shape[1]
    tm = _tile(lp, FF_ROWS, 16)
    ni = lp // tm
    rider = rider or _Rider()

    def body(q_ref, c_ref, w_ref, h_ref, g_ref, dres_ref, *rest):
        (o_ref, dg_ref), (acc,) = rider.split(rest, 2, 1)
        i = pl.program_id(0)
        rider.start(i == 0)

        @pl.when(i == 0)
        def _():
            dg_ref[...] = jnp.zeros_like(dg_ref)

        pieces = [q_ref[p] for p in range(nq)] + [c_ref[p] for p in range(nc)]
        total = None
        for p, piece in enumerate(pieces):
            part = jnp.dot(piece, w_ref[p * w:(p + 1) * w, :], preferred_element_type=F32)
            total = part if total is None else total + part
        acc[...] = total
        _rms_bwd_tail(acc, h_ref, g_ref, dres_ref, o_ref, dg_ref)
        rider.wait(i == ni - 1)

    return pl.pallas_call(
        body, out_shape=[SDS((lp, d), F32), SDS((8, d), F32)] + rider.out_shape, grid=(ni,),
        in_specs=[pl.BlockSpec((nq, tm, w), lambda i: (0, i, 0)),
                  pl.BlockSpec((nc, tm, w), lambda i: (0, i, 0)),
                  pl.BlockSpec(((nq + nc) * w, d), lambda i: (0, 0)),
                  pl.BlockSpec((tm, d), lambda i: (i, 0)),
                  pl.BlockSpec((1, d), lambda i: (0, 0)),
                  pl.BlockSpec((tm, d), lambda i: (i, 0))] + rider.in_specs(),
        out_specs=[pl.BlockSpec((tm, d), lambda i: (i, 0)), pl.BlockSpec((8, d), lambda i: (0, 0))]
        + rider.out_specs(),
        scratch_shapes=[pltpu.VMEM((tm, d), F32)] + rider.scratch(), name=name,
        compiler_params=_cp())(dqkv, dcc, wt, h, g, dres, *rider.srcs)


def _ffn_bwd_in(d_g, d_u, wg_t, wu_t, h, g, dres, name, rider=None):
    lp, f = d_g.shape
    d = h.shape[1]
    tm = _tile(lp, FF_ROWS, 16)
    tk = _tile(f, FF_BLOCK, 128)
    nk = f // tk
    ni = lp // tm
    rider = rider or _Rider()

    def body(dg_in, du_in, wg_ref, wu_ref, h_ref, g_ref, dres_ref, *rest):
        (o_ref, dg_ref), (acc,) = rider.split(rest, 2, 1)
        i, kk = pl.program_id(0), pl.program_id(1)
        rider.start((i == 0) & (kk == 0))

        @pl.when(kk == 0)
        def _():
            acc[...] = jnp.zeros_like(acc)

        @pl.when((kk == 0) & (i == 0))
        def _():
            dg_ref[...] = jnp.zeros_like(dg_ref)

        acc[...] += (jnp.dot(dg_in[...], wg_ref[...], preferred_element_type=F32)
                     + jnp.dot(du_in[...], wu_ref[...], preferred_element_type=F32))

        @pl.when(kk == nk - 1)
        def _():
            _rms_bwd_tail(acc, h_ref, g_ref, dres_ref, o_ref, dg_ref)

        rider.wait((i == ni - 1) & (kk == nk - 1))

    return pl.pallas_call(
        body, out_shape=[SDS((lp, d), F32), SDS((8, d), F32)] + rider.out_shape, grid=(ni, nk),
        in_specs=[pl.BlockSpec((tm, tk), lambda i, j: (i, j)), pl.BlockSpec((tm, tk), lambda i, j: (i, j)),
                  pl.BlockSpec((tk, d), lambda i, j: (j, 0)), pl.BlockSpec((tk, d), lambda i, j: (j, 0)),
                  pl.BlockSpec((tm, d), lambda i, j: (i, 0)),
                  pl.BlockSpec((1, d), lambda i, j: (0, 0)),
                  pl.BlockSpec((tm, d), lambda i, j: (i, 0))] + rider.in_specs(),
        out_specs=[pl.BlockSpec((tm, d), lambda i, j: (i, 0)), pl.BlockSpec((8, d), lambda i, j: (0, 0))]
        + rider.out_specs(),
        scratch_shapes=[pltpu.VMEM((tm, d), F32)] + rider.scratch(), name=name,
        compiler_params=_cp())(d_g, d_u, wg_t, wu_t, h, g, dres, *rider.srcs)


def _ffn_fwd(h, g, wg_t, wu_t, w_d, name, rider=None):
    lp, d = h.shape
    f = w_d.shape[0]
    tm = _tile(lp, FF_ROWS, 16)
    fb, subs = _ff_block(f)
    nc = f // fb
    ni = lp // tm
    rider = rider or _Rider()

    def body(h_ref, g_ref, wg_ref, wu_ref, wd_ref, *rest):
        (o_ref, hn_ref, act_ref, gate_ref, up_ref), (hn_s, acc) = rider.split(rest, 5, 2)
        c = pl.program_id(1)
        rider.start((pl.program_id(0) == 0) & (c == 0))
        if ni > 1:
            rider.forward((pl.program_id(0) == ni - 1) & (c == 0))

        @pl.when(c == 0)
        def _():
            x = h_ref[...]
            r = lax.rsqrt(jnp.mean(x * x, axis=-1, keepdims=True) + EPS)
            hn = ((x * r) * g_ref[...]).astype(BF16)
            hn_s[...] = hn
            hn_ref[...] = hn
            acc[...] = jnp.zeros_like(acc)

        total = None
        for s0, sw in subs:
            gg = lax.dot_general(hn_s[...], wg_ref[s0:s0 + sw, :], NT, preferred_element_type=F32)
            uu = lax.dot_general(hn_s[...], wu_ref[s0:s0 + sw, :], NT, preferred_element_type=F32)
            gate_ref[:, s0:s0 + sw] = gg
            up_ref[:, s0:s0 + sw] = uu
            act = ((gg * _sigmoid(gg)) * uu).astype(BF16)
            act_ref[:, s0:s0 + sw] = act
            part = jnp.dot(act, wd_ref[s0:s0 + sw, :], preferred_element_type=F32)
            total = part if total is None else total + part
        acc[...] += total

        @pl.when(c == nc - 1)
        def _():
            o_ref[...] = h_ref[...] + acc[...]

        rider.wait((pl.program_id(0) == ni - 1) & (c == nc - 1))

    chunk = pl.BlockSpec((fb, d), lambda i, j: (j, 0))
    wide = pl.BlockSpec((tm, fb), lambda i, j: (i, j))
    return pl.pallas_call(
        body, out_shape=[SDS((lp, d), F32), SDS((lp, d), BF16), SDS((lp, f), BF16), SDS((lp, f), F32),
                         SDS((lp, f), F32)] + rider.out_shape, grid=(ni, nc),
        in_specs=[pl.BlockSpec((tm, d), lambda i, j: (i, 0)), pl.BlockSpec((1, d), lambda i, j: (0, 0)),
                  chunk, chunk, chunk] + rider.in_specs(),
        out_specs=[pl.BlockSpec((tm, d), lambda i, j: (i, 0)), pl.BlockSpec((tm, d), lambda i, j: (i, 0)),
                   wide, wide, wide] + rider.out_specs(),
        scratch_shapes=[pltpu.VMEM((tm, d), BF16), pltpu.VMEM((tm, d), F32)] + rider.scratch(),
        name=name, compiler_params=_cp(FFN_FWD_VMEM_MIB))(h, g, wg_t, wu_t, w_d, *rider.srcs)


def _ffn_bwd_act(dh, gate, up, w_d, name, rider=None):
    lp, d = dh.shape
    f = w_d.shape[0]
    tm = _tile(lp, FF_ROWS, 16)
    fb, subs = _ff_block(f)
    nc = f // fb
    ni = lp // tm
    rider = rider or _Rider()

    def body(dh_ref, gate_ref, up_ref, wd_ref, *rest):
        (dg_out, du_out), (dh_s,) = rider.split(rest, 2, 1)
        rider.start((pl.program_id(0) == 0) & (pl.program_id(1) == 0))

        @pl.when(pl.program_id(1) == 0)
        def _():
            dh_s[...] = dh_ref[...].astype(BF16)

        for s0, sw in subs:
            gg = gate_ref[:, s0:s0 + sw]
            uu = up_ref[:, s0:s0 + sw]
            dact = lax.dot_general(dh_s[...], wd_ref[s0:s0 + sw, :], NT, preferred_element_type=F32)
            s = _sigmoid(gg)
            dg_out[:, s0:s0 + sw] = (dact * uu * (s * (1.0 + gg * (1.0 - s)))).astype(BF16)
            du_out[:, s0:s0 + sw] = (dact * (gg * s)).astype(BF16)
        rider.wait((pl.program_id(0) == ni - 1) & (pl.program_id(1) == nc - 1))

    wide = pl.BlockSpec((tm, fb), lambda i, j: (i, j))
    return pl.pallas_call(
        body, out_shape=[SDS((lp, f), BF16), SDS((lp, f), BF16)] + rider.out_shape, grid=(ni, nc),
        in_specs=[pl.BlockSpec((tm, d), lambda i, j: (i, 0)), wide, wide,
                  pl.BlockSpec((fb, d), lambda i, j: (j, 0))] + rider.in_specs(),
        out_specs=[wide, wide] + rider.out_specs(), scratch_shapes=[pltpu.VMEM((tm, d), BF16)] + rider.scratch(),
        name=name, compiler_params=_cp())(dh, gate, up, w_d, *rider.srcs)


def _loss_head(h, tpad, g, n_real, name):
    lp, d = h.shape
    tm = _tile(lp, EW_ROWS, 16)

    def body(h_ref, t_ref, g_ref, dh_ref, dg_ref, loss_ref):
        i = pl.program_id(0)

        @pl.when(i == 0)
        def _():
            dg_ref[...] = jnp.zeros_like(dg_ref)
            loss_ref[...] = jnp.zeros_like(loss_ref)

        x = h_ref[...]
        r = lax.rsqrt(jnp.mean(x * x, axis=-1, keepdims=True) + EPS)
        xr = x * r
        y = xr * g_ref[...]
        row = i * tm + lax.broadcasted_iota(jnp.int32, (tm, d), 0)
        valid = (row >= N_META) & (row < N_META + n_real)
        diff = jnp.where(valid, y - t_ref[...], 0.0)
        loss_ref[...] += jnp.sum(diff * diff) * (0.5 / d)
        dy = diff * (1.0 / d)
        gy = dy * g_ref[...]
        dh_ref[...] = r * (gy - xr * jnp.mean(gy * xr, axis=-1, keepdims=True))
        dg_ref[...] += _sum8(dy * xr)

    return pl.pallas_call(
        body, out_shape=[SDS((lp, d), F32), SDS((8, d), F32), SDS((8, 128), F32)], grid=(lp // tm,),
        in_specs=[pl.BlockSpec((tm, d), lambda i: (i, 0)),
                  pl.BlockSpec((tm, d), lambda i: (i, 0)),
                  pl.BlockSpec((1, d), lambda i: (0, 0))],
        out_specs=[pl.BlockSpec((tm, d), lambda i: (i, 0)),
                   pl.BlockSpec((8, d), lambda i: (0, 0)),
                   pl.BlockSpec((8, 128), lambda i: (0, 0))],
        name=name, compiler_params=_cp())(h, tpad, g)


def _tri_consts():
    j = lax.broadcasted_iota(jnp.int32, (ABLK, ABLK), 0)
    s = lax.broadcasted_iota(jnp.int32, (ABLK, ABLK), 1)
    after = (j >= s).astype(BF16)
    before = (j < s).astype(BF16)
    ones = jnp.ones((ABLK, ABLK), BF16)
    two = lambda t: jnp.concatenate([t, t], axis=0)
    return (two(jnp.concatenate([after, ones], axis=1)),
            two(after),
            two(jnp.concatenate([before, ones], axis=1)))


def _softplus(z):
    neg_abs = lax.bitcast_convert_type(lax.bitcast_convert_type(z, jnp.uint32) | jnp.uint32(0x80000000), F32)
    return jnp.log(1.0 + jnp.exp(neg_abs)) + jnp.maximum(z, 0.0)


def _split_hi_lo(m):
    hi = m.astype(BF16)
    lo = (m - hi.astype(F32)).astype(BF16)
    return jnp.concatenate([hi, lo], axis=1)


def _head_halves(t2, in_a):
    zero = jnp.zeros_like(t2)
    return jnp.concatenate([jnp.where(in_a, t2, zero), jnp.where(in_a, zero, t2)], axis=0)


def _stack_blocks(t, nblk, in_a):
    return jnp.concatenate([_head_halves(t[u * ABLK:(u + 1) * ABLK], in_a) for u in range(nblk)], axis=0)


def _interleave(*gens):
    alive = list(gens)
    while alive:
        for g in list(alive):
            if next(g, alive) is alive:
                alive.remove(g)


def _attn_scale():
    scale = 1.0 / math.sqrt(HEAD_DIM)
    assert math.frexp(scale)[0] == 0.5, "a power of two, so that scaling q in bf16 is exact"
    return scale


def _pow2_below(n):
    assert n & (n - 1) == 0
    return [p for p in (64, 32, 16, 8, 4, 2, 1) if p < n]


class _Rider:
    def __init__(self, jobs=(), srcs=(), out_shape=()):
        self.jobs, self.srcs, self.out_shape = list(jobs), list(srcs), list(out_shape)
        self.any = [pl.BlockSpec(memory_space=pl.ANY)]

    def split(self, rest, n_out, n_scratch):
        ni, no = len(self.srcs), len(self.out_shape)
        self.ins, outs = rest[:ni], rest[ni:ni + n_out]
        self.outs = rest[ni + n_out:ni + n_out + no]
        scratch = rest[ni + n_out + no:ni + n_out + no + n_scratch]
        self.sems = rest[ni + n_out + no + n_scratch:]
        return outs, scratch

    def start(self, first):
        self.forwarded = False
        if self.jobs:
            @pl.when(first)
            def _():
                _exchange_start(self.jobs, self.ins, self.outs, *self.sems)

    def forward(self, late):
        self.forwarded = True
        if self.jobs:
            @pl.when(late)
            def _():
                _exchange_forward(self.jobs, self.ins, self.outs, *self.sems)

    def wait(self, last):
        if self.jobs:
            @pl.when(last)
            def _():
                if not self.forwarded:
                    _exchange_forward(self.jobs, self.ins, self.outs, *self.sems)
                _exchange_wait(self.jobs, self.ins, self.outs, *self.sems)

    def in_specs(self):
        return self.any * len(self.srcs)

    def out_specs(self):
        return self.any * len(self.out_shape)

    def scratch(self):
        return _exchange_sems(len(self.jobs)) if self.jobs else []


def _attn_fwd(qkv, tri_fwd, name, rider=None):
    lp = qkv.shape[0]
    n_pairs = (N_HEADS * HEAD_DIM) // 128
    nb = lp // ABLK
    assert nb <= 128 and 2 * HEAD_DIM == 128
    scale = _attn_scale()
    nt = (((1,), (1,)), ((), ()))
    rider = rider or _Rider()

    def body(q_ref, k_ref, v_ref, tri_ref, *rest):
        (o_ref, rs_ref), (r_s, acc_s, rs_s) = rider.split(rest, 2, 3)
        rider.start(pl.program_id(0) == 0)
        lane = lax.broadcasted_iota(jnp.int32, (ABLK, 128), 1)
        row = lax.broadcasted_iota(jnp.int32, (ABLK, 128), 0)
        in_a = lane < HEAD_DIM
        causal = lane < row

        def begin(st, i):
            r_s[st] = jnp.zeros(r_s.shape[1:], F32)
            acc_s[st] = jnp.zeros(acc_s.shape[1:], F32)
            rs_s[st] = jnp.full(rs_s.shape[1:], GONE, F32)
            return q_ref[pl.ds(pl.multiple_of(i * ABLK, ABLK), ABLK), :] * scale

        def live(st):
            least = jnp.min(jnp.minimum(r_s[st, 0], r_s[st, 1]), axis=0, keepdims=True)
            return (least[0, 0] < EXP_ZERO_AT).astype(jnp.int32)

        def step(*args):
            _interleave(stages(*args))

        def stages(st, q2, kb0, nblk, diag):
            k0 = pl.multiple_of(kb0 * ABLK, ABLK)
            kbd = _stack_blocks(k_ref[pl.ds(k0, nblk * ABLK), :], nblk, in_a)
            vbd = _stack_blocks(v_ref[pl.ds(k0, nblk * ABLK), :], nblk, in_a)
            z = lax.dot_general(q2, kbd, nt, preferred_element_type=F32)
            ncol = 2 * nblk
            zt = [z[:, c * 128:(c + 1) * 128] for c in range(ncol)]
            if diag:
                zt = [jnp.where(causal, t, MASKED) if c >= ncol - 2 else t for c, t in enumerate(zt)]
            yield
            bounds = [ncol * j // ATT_SPLIT for j in range(ATT_SPLIT + 1)]
            ce = [None] * ncol
            for c0, c1 in reversed([b for b in zip(bounds[:-1], bounds[1:]) if b[0] < b[1]]):
                parts = [_split_hi_lo(_softplus(zt[c])) for c in range(c0, c1)]
                got = jnp.dot(jnp.concatenate(parts, axis=0), tri_ref[...], preferred_element_type=F32)
                for c in range(c0, c1):
                    ce[c] = got[(c - c0) * 128:(c - c0 + 1) * 128]
                yield
            rr = [r_s[st, 0], r_s[st, 1]]
            rsv = [rs_s[st, :, :128], rs_s[st, :, 128:]]
            ws = [None] * ncol
            for u in reversed(range(nblk)):
                for hh in range(2):
                    c = 2 * u + hh
                    ws[c] = jnp.exp(zt[c] - ce[c][:, :128] - rr[hh]).astype(BF16)
                    rsv[hh] = jnp.where(lane == kb0 + u, rr[hh], rsv[hh])
                    rr[hh] = rr[hh] + ce[c][:, 128:]
            acc_s[st] += jnp.dot(jnp.concatenate(ws, axis=1), vbd, preferred_element_type=F32)
            r_s[st, 0] = rr[0]
            r_s[st, 1] = rr[1]
            rs_s[st, :, :128] = rsv[0]
            rs_s[st, :, 128:] = rsv[1]

        def finish(st, q2, i, n_top):
            i_low = i - n_top
            n_grp = i_low // ATT_GROUP

            def more(c):
                return (c[0] < n_grp) & (c[1] > 0)

            def inner(c):
                step(st, q2, i_low - ATT_GROUP * (c[0] + 1), ATT_GROUP, False)
                return c[0] + 1, live(st)

            _, alive = lax.while_loop(more, inner, (jnp.int32(0), live(st)))
            rem = i_low - ATT_GROUP * n_grp
            for p in _pow2_below(ATT_GROUP):
                def last_steps(p=p):
                    step(st, q2, rem & (p - 1), p, False)
                    return live(st)

                alive = lax.cond(((rem & p) != 0) & (alive > 0), last_steps, lambda alive=alive: alive)

            q0 = pl.multiple_of(i * ABLK, ABLK)
            o_ref[pl.ds(q0, ABLK), :] = acc_s[st].astype(BF16)
            rs_ref[pl.ds(q0, ABLK), :] = rs_s[st]

        def single(i):
            q2 = begin(0, i)
            n_top = jnp.minimum(i, ATT_TOP)
            for t in range(ATT_TOP + 1):
                @pl.when(n_top == t)
                def _():
                    step(0, q2, i - t, t + 1, True)

            finish(0, q2, i, n_top)

        def pair(i):
            qa, qb = begin(0, i), begin(1, i + 1)
            _interleave(stages(0, qa, i - ATT_TOP, ATT_TOP + 1, True), stages(1, qb, i + 1 - ATT_TOP, ATT_TOP + 1, True))
            finish(0, qa, i, ATT_TOP)
            finish(1, qb, i + 1, ATT_TOP)

        n_head = min(ATT_TOP, nb)
        n_pair = (nb - n_head) // 2
        n_single = nb - 2 * n_pair

        def singles(k, carry):
            single(jnp.where(k < n_head, k, nb - n_single + k))
            return carry

        def pairs(j, carry):
            pair(n_head + 2 * j)
            return carry

        lax.fori_loop(0, n_single, singles, 0)
        lax.fori_loop(0, n_pair, pairs, 0)
        rider.wait(pl.program_id(0) == n_pairs - 1)

    col = lambda o: (lambda p: (0, p + o))
    return pl.pallas_call(
        body, out_shape=[SDS((lp, n_pairs * 128), BF16), SDS((lp, n_pairs * 256), F32)] + rider.out_shape,
        grid=(n_pairs,),
        in_specs=[pl.BlockSpec((lp, 128), col(0)), pl.BlockSpec((lp, 128), col(n_pairs)),
                  pl.BlockSpec((lp, 128), col(2 * n_pairs)), pl.BlockSpec((256, 256), lambda p: (0, 0))]
        + rider.in_specs(),
        out_specs=[pl.BlockSpec((lp, 128), col(0)), pl.BlockSpec((lp, 256), col(0))] + rider.out_specs(),
        scratch_shapes=[pltpu.VMEM((2, 2, ABLK, 128), F32), pltpu.VMEM((2, ABLK, 128), F32),
                        pltpu.VMEM((2, ABLK, 256), F32)] + rider.scratch(),
        name=name, compiler_params=_cp())(qkv, qkv, qkv, tri_fwd, *rider.srcs)


def _attn_bwd(qkv, d_out, rsave, tri_after, tri_before, name, rider=None):
    lp = qkv.shape[0]
    n_pairs = (N_HEADS * HEAD_DIM) // 128
    nb = lp // ABLK
    scale = _attn_scale()
    nt = (((1,), (1,)), ((), ()))
    tn = (((0,), (0,)), ((), ()))
    rider = rider or _Rider()

    def body(q_ref, k_ref, v_ref, do_ref, rs_ref, ta_ref, tb_ref, *rest):
        (o_ref,), (dk_s, dv_s, dq_s, pc_s) = rider.split(rest, 1, 4)
        rider.start(pl.program_id(0) == 0)
        lane = lax.broadcasted_iota(jnp.int32, (ABLK, 128), 1)
        row = lax.broadcasted_iota(jnp.int32, (ABLK, 128), 0)
        in_a = lane < HEAD_DIM
        causal = lane < row
        dk_s[...] = jnp.zeros_like(dk_s)
        dv_s[...] = jnp.zeros_like(dv_s)

        def begin(st, i):
            q0 = pl.multiple_of(i * ABLK, ABLK)
            q2 = q_ref[pl.ds(q0, ABLK), :] * scale
            do2 = do_ref[pl.ds(q0, ABLK), :]
            dq_s[st] = jnp.zeros(dq_s.shape[1:], F32)
            pc_s[st] = jnp.zeros(pc_s.shape[1:], F32)
            return dict(q0=q0, q2=q2, do2=do2, q_st=_head_halves(q2, in_a), do_st=_head_halves(do2, in_a))

        def step(*args):
            _interleave(stages(*args))

        def stages(st, blk, kb0, nblk, diag):
            q0, q2, do2 = blk["q0"], blk["q2"], blk["do2"]
            k0 = pl.multiple_of(kb0 * ABLK, ABLK)
            kbd = _stack_blocks(k_ref[pl.ds(k0, nblk * ABLK), :], nblk, in_a)
            vbd = _stack_blocks(v_ref[pl.ds(k0, nblk * ABLK), :], nblk, in_a)
            z = lax.dot_general(q2, kbd, nt, preferred_element_type=F32)
            dw = lax.dot_general(do2, vbd, nt, preferred_element_type=F32)
            ncol = 2 * nblk
            zt = [z[:, c * 128:(c + 1) * 128] for c in range(ncol)]
            if diag:
                zt = [jnp.where(causal, t, MASKED) if c >= ncol - 2 else t for c, t in enumerate(zt)]
            bounds = [ncol * j // ATT_SPLIT for j in range(ATT_SPLIT + 1)]
            batches = [range(c0, c1) for c0, c1 in zip(bounds[:-1], bounds[1:]) if c0 < c1]
            sps, ex, ws, dls, pe = [None] * ncol, [None] * ncol, [None] * ncol, [None] * ncol, [None] * ncol

            def mass(cols):
                for c in cols:
                    sps[c] = _softplus(zt[c])
                got = jnp.dot(jnp.concatenate([_split_hi_lo(sps[c]) for c in cols], axis=0), ta_ref[...],
                              preferred_element_type=F32)
                for j, c in enumerate(cols):
                    ex[c] = got[j * 128:(j + 1) * 128]

            def weights(cols):
                for c in cols:
                    u, hh = c // 2, c % 2
                    r_saved = jnp.sum(jnp.where(lane == kb0 + u, rs_ref[pl.ds(q0, ABLK), hh * 128:(hh + 1) * 128],
                                                0.0), axis=1, keepdims=True)
                    w = jnp.exp(zt[c] - ex[c] - r_saved)
                    ws[c] = w.astype(BF16)
                    dls[c] = dw[:, c * 128:(c + 1) * 128] * w
                got = jnp.dot(jnp.concatenate([_split_hi_lo(dls[c]) for c in cols], axis=0), tb_ref[...],
                              preferred_element_type=F32)
                for j, c in enumerate(cols):
                    pe[c] = got[j * 128:(j + 1) * 128]

            yield
            mass(batches[0])
            yield
            for j in range(len(batches)):
                if j + 1 < len(batches):
                    mass(batches[j + 1])
                    yield
                weights(batches[j])
                yield
            pc = [pc_s[st, 0], pc_s[st, 1]]
            dzs = []
            for c in range(ncol):
                hh = c % 2
                one_minus_beta = jnp.exp(-sps[c])
                dz = dls[c] * one_minus_beta - (pe[c][:, :128] + pc[hh]) * (1.0 - one_minus_beta)
                pc[hh] = pc[hh] + pe[c][:, 128:]
                dzs.append(dz.astype(BF16))
            pc_s[st, 0] = pc[0]
            pc_s[st, 1] = pc[1]
            dq_s[st] += jnp.dot(jnp.concatenate(dzs, axis=1), kbd, preferred_element_type=F32)
            by_head = lambda ts: jnp.concatenate([jnp.concatenate(ts[0::2], axis=1), jnp.concatenate(ts[1::2], axis=1)],
                                                 axis=0)
            rows = pl.ds(k0, nblk * ABLK)
            dk_s[rows, :] += lax.dot_general(by_head(dzs), blk["q_st"], tn, preferred_element_type=F32)
            dv_s[rows, :] += lax.dot_general(by_head(ws), blk["do_st"], tn, preferred_element_type=F32)

        def below(st, blk, i, n_top):
            gone = jnp.min(rs_ref[pl.ds(blk["q0"], ABLK), :], axis=0, keepdims=True) >= EXP_ZERO_AT
            lane1 = lax.broadcasted_iota(jnp.int32, (1, 128), 1)
            first = jnp.sum(jnp.where(gone[:, :128] & gone[:, 128:] & (lane1 < i), 1.0, 0.0)).astype(jnp.int32)
            i_low = i - n_top
            n_grp = i_low // ATT_GROUP
            rem = i_low - ATT_GROUP * n_grp
            for p in reversed(_pow2_below(ATT_GROUP)):
                @pl.when(((rem & p) != 0) & ((rem & (p - 1)) + p > first))
                def _():
                    step(st, blk, rem & (p - 1), p, False)

            def inner(g, c2):
                step(st, blk, rem + ATT_GROUP * g, ATT_GROUP, False)
                return c2

            lax.fori_loop(jnp.maximum(first - rem, 0) // ATT_GROUP, n_grp, inner, 0)

        def done(st, blk):
            o_ref[0, pl.ds(blk["q0"], ABLK), :] = (dq_s[st] * scale).astype(BF16)

        def single(i):
            blk = begin(0, i)
            n_top = jnp.minimum(i, ATT_TOP)
            below(0, blk, i, n_top)
            for t in range(ATT_TOP + 1):
                @pl.when(n_top == t)
                def _():
                    step(0, blk, i - t, t + 1, True)

            done(0, blk)

        def pair(i):
            a, b = begin(0, i), begin(1, i + 1)
            below(0, a, i, ATT_TOP)
            below(1, b, i + 1, ATT_TOP)
            _interleave(stages(0, a, i - ATT_TOP, ATT_TOP + 1, True), stages(1, b, i + 1 - ATT_TOP, ATT_TOP + 1, True))
            done(0, a)
            done(1, b)

        n_head = min(ATT_TOP, nb)
        n_pair = (nb - n_head) // 2
        n_single = nb - 2 * n_pair

        def singles(k, carry):
            single(jnp.where(k < n_head, k, nb - n_single + k))
            return carry

        def pairs(j, carry):
            pair(n_head + 2 * j)
            return carry

        lax.fori_loop(0, n_single, singles, 0)
        lax.fori_loop(0, n_pair, pairs, 0)
        o_ref[1] = dk_s[...].astype(BF16)
        o_ref[2] = dv_s[...].astype(BF16)
        rider.wait(pl.program_id(0) == n_pairs - 1)

    col = lambda o: (lambda p: (0, p + o))
    return pl.pallas_call(
        body, out_shape=[SDS((3, lp, n_pairs * 128), BF16)] + rider.out_shape, grid=(n_pairs,),
        in_specs=[pl.BlockSpec((lp, 128), col(0)), pl.BlockSpec((lp, 128), col(n_pairs)),
                  pl.BlockSpec((lp, 128), col(2 * n_pairs)), pl.BlockSpec((lp, 128), col(0)),
                  pl.BlockSpec((lp, 256), col(0)),
                  pl.BlockSpec((256, 128), lambda p: (0, 0)), pl.BlockSpec((256, 256), lambda p: (0, 0))]
        + rider.in_specs(),
        out_specs=[pl.BlockSpec((3, lp, 128), lambda p: (0, 0, p))] + rider.out_specs(),
        scratch_shapes=[pltpu.VMEM((lp, 128), F32), pltpu.VMEM((lp, 128), F32),
                        pltpu.VMEM((2, ABLK, 128), F32), pltpu.VMEM((2, 2, ABLK, 128), F32)] + rider.scratch(),
        name=name, compiler_params=_cp())(qkv, qkv, qkv, d_out, rsave, tri_after, tri_before, *rider.srcs)


def _conv_fwd_dw(cacg, w, b, name, rider=None):
    lp = cacg.shape[0]
    c = cacg.shape[1] // 2
    ncb = c // 128
    nchunk = lp // ABLK
    off = CONV_PAD - (CONV_K - 1)
    rider = rider or _Rider()

    def body(a_ref, g_ref, w_ref, b_ref, *rest):
        (y_ref,), (upad,) = rider.split(rest, 1, 1)
        rider.start(pl.program_id(0) == 0)
        upad[0:CONV_PAD, :] = jnp.zeros((CONV_PAD, 128), F32)

        def fill(ch, carry):
            base = pl.multiple_of(ch * ABLK, ABLK)
            upad[pl.ds(base + CONV_PAD, ABLK), :] = a_ref[pl.ds(base, ABLK), :] * _sigmoid(g_ref[pl.ds(base, ABLK), :])
            return carry

        lax.fori_loop(0, nchunk, fill, 0)

        def comp(ch, carry):
            base = pl.multiple_of(ch * ABLK, ABLK)
            acc = jnp.zeros((ABLK, 128), F32)
            for k in range(CONV_K):
                acc = acc + upad[pl.ds(base + (off + k), ABLK), :] * w_ref[k:k + 1, :]
            y_ref[pl.ds(base, ABLK), :] = acc + b_ref[...]
            return carry

        lax.fori_loop(0, nchunk, comp, 0)
        rider.wait(pl.program_id(0) == ncb - 1)

    return pl.pallas_call(
        body, out_shape=[SDS((lp, c), F32)] + rider.out_shape, grid=(ncb,),
        in_specs=[pl.BlockSpec((lp, 128), lambda j: (0, j)), pl.BlockSpec((lp, 128), lambda j: (0, j + ncb)),
                  pl.BlockSpec((CONV_PAD, 128), lambda j: (0, j)), pl.BlockSpec((1, 128), lambda j: (0, j))]
        + rider.in_specs(),
        out_specs=[pl.BlockSpec((lp, 128), lambda j: (0, j))] + rider.out_specs(),
        scratch_shapes=[pltpu.VMEM((lp + CONV_PAD, 128), F32)] + rider.scratch(), name=name,
        compiler_params=_cp())(cacg, cacg, w, b, *rider.srcs)


def _ln_parts(x, g, b):
    mu = jnp.mean(x, axis=-1, keepdims=True)
    xc = x - mu
    rstd = lax.rsqrt(jnp.mean(xc * xc, axis=-1, keepdims=True) + EPS)
    xh = xc * rstd
    return xh, rstd, xh * g + b


def _conv_fwd_ln(yc, g, b, name):
    lp, c = yc.shape
    tm = _tile(lp, EW_ROWS, 16)

    def body(y_ref, g_ref, b_ref, o_ref):
        _, _, ln = _ln_parts(y_ref[...], g_ref[...], b_ref[...])
        o_ref[...] = (ln * _sigmoid(ln)).astype(BF16)

    return pl.pallas_call(
        body, out_shape=SDS((lp, c), BF16), grid=(lp // tm,),
        in_specs=[pl.BlockSpec((tm, c), lambda i: (i, 0)), pl.BlockSpec((1, c), lambda i: (0, 0)),
                  pl.BlockSpec((1, c), lambda i: (0, 0))],
        out_specs=pl.BlockSpec((tm, c), lambda i: (i, 0)), name=name, compiler_params=_cp())(yc, g, b)


def _conv_bwd_ln(yc, dout, g, b, name):
    lp, c = yc.shape
    tm = _tile(lp, EW_ROWS, 16)

    def body(y_ref, d_ref, g_ref, b_ref, o_ref, dg_ref, db_ref):
        @pl.when(pl.program_id(0) == 0)
        def _():
            dg_ref[...] = jnp.zeros_like(dg_ref)
            db_ref[...] = jnp.zeros_like(db_ref)

        xh, rstd, ln = _ln_parts(y_ref[...], g_ref[...], b_ref[...])
        s = _sigmoid(ln)
        dln = d_ref[...] * (s * (1.0 + ln * (1.0 - s)))
        dg_ref[...] += _sum8(dln * xh)
        db_ref[...] += _sum8(dln)
        dxh = dln * g_ref[...]
        o_ref[...] = rstd * (dxh - jnp.mean(dxh, axis=-1, keepdims=True)
                             - xh * jnp.mean(dxh * xh, axis=-1, keepdims=True))

    return pl.pallas_call(
        body, out_shape=[SDS((lp, c), F32), SDS((8, c), F32), SDS((8, c), F32)], grid=(lp // tm,),
        in_specs=[pl.BlockSpec((tm, c), lambda i: (i, 0)), pl.BlockSpec((tm, c), lambda i: (i, 0)),
                  pl.BlockSpec((1, c), lambda i: (0, 0)), pl.BlockSpec((1, c), lambda i: (0, 0))],
        out_specs=[pl.BlockSpec((tm, c), lambda i: (i, 0)), pl.BlockSpec((8, c), lambda i: (0, 0)),
                   pl.BlockSpec((8, c), lambda i: (0, 0))],
        name=name, compiler_params=_cp())(yc, dout, g, b)


def _conv_bwd_dw(dyc, cacg, w, name):
    lp, c = dyc.shape
    ncb = c // 128
    nchunk = lp // ABLK
    off = CONV_PAD - (CONV_K - 1)

    def body(dy_ref, a_ref, g_ref, w_ref, dcc_ref, dw_ref, db_ref, upad, dypad, dwacc):
        upad[0:CONV_PAD, :] = jnp.zeros((CONV_PAD, 128), F32)
        dypad[lp:lp + CONV_PAD, :] = jnp.zeros((CONV_PAD, 128), F32)
        dwacc[...] = jnp.zeros_like(dwacc)
        db_ref[...] = jnp.zeros_like(db_ref)

        def fill(ch, carry):
            base = pl.multiple_of(ch * ABLK, ABLK)
            upad[pl.ds(base + CONV_PAD, ABLK), :] = a_ref[pl.ds(base, ABLK), :] * _sigmoid(g_ref[pl.ds(base, ABLK), :])
            dypad[pl.ds(base, ABLK), :] = dy_ref[pl.ds(base, ABLK), :]
            return carry

        lax.fori_loop(0, nchunk, fill, 0)

        def comp(ch, carry):
            base = pl.multiple_of(ch * ABLK, ABLK)
            dy = dy_ref[pl.ds(base, ABLK), :]
            du = jnp.zeros((ABLK, 128), F32)
            for k in range(CONV_K):
                du = du + dypad[pl.ds(base + (CONV_K - 1 - k), ABLK), :] * w_ref[k:k + 1, :]
                dwacc[k * 8:(k + 1) * 8, :] += _sum8(dy * upad[pl.ds(base + (off + k), ABLK), :])
            db_ref[...] += _sum8(dy)
            a = a_ref[pl.ds(base, ABLK), :]
            s = _sigmoid(g_ref[pl.ds(base, ABLK), :])
            dcc_ref[0, pl.ds(base, ABLK), :] = (du * s).astype(BF16)
            dcc_ref[1, pl.ds(base, ABLK), :] = (du * a * (s * (1.0 - s))).astype(BF16)
            return carry

        lax.fori_loop(0, nchunk, comp, 0)
        dw_ref[...] = dwacc[...].reshape(CONV_PAD, 8, 128).sum(axis=1)

    return pl.pallas_call(
        body, out_shape=[SDS((2, lp, c), BF16), SDS((CONV_PAD, c), F32), SDS((8, c), F32)],
        grid=(ncb,),
        in_specs=[pl.BlockSpec((lp, 128), lambda j: (0, j)), pl.BlockSpec((lp, 128), lambda j: (0, j)),
                  pl.BlockSpec((lp, 128), lambda j: (0, j + ncb)), pl.BlockSpec((CONV_PAD, 128), lambda j: (0, j))],
        out_specs=[pl.BlockSpec((2, lp, 128), lambda j: (0, 0, j)),
                   pl.BlockSpec((CONV_PAD, 128), lambda j: (0, j)), pl.BlockSpec((8, 128), lambda j: (0, j))],
        scratch_shapes=[pltpu.VMEM((lp + CONV_PAD, 128), F32), pltpu.VMEM((lp + CONV_PAD, 128), F32),
                        pltpu.VMEM((CONV_PAD * 8, 128), F32)],
        name=name, compiler_params=_cp())(dyc, cacg, cacg, w)


def _mesh_pos():
    x, y, c = lax.axis_index("x"), lax.axis_index("y"), lax.axis_index("c")
    return x, y, c


def _peer(pos, r):
    x, y, c = pos
    px = (1 - x) if (r >> 2) & 1 else x
    py = (1 - y) if (r >> 1) & 1 else y
    pc = (1 - c) if r & 1 else c
    return (px, py, pc), 4 * px + 2 * py + pc


SIBLING = 1
OTHER_CHIPS = (2, 4, 6)
VIA_SIBLING = (3, 5, 7)


class _Job:
    def __init__(self, src, dst, scatter, src_layer=None, dst_layer=None):
        self.src, self.dst, self.scatter, self.src_layer, self.dst_layer = src, dst, scatter, src_layer, dst_layer

    def src_view(self, ins, idx):
        v = ins[self.src] if self.src_layer is None else ins[self.src].at[self.src_layer]
        return v.at[idx] if self.scatter else v

    def dst_view(self, outs, slot):
        v = outs[self.dst] if self.dst_layer is None else outs[self.dst].at[self.dst_layer]
        return v.at[slot]


def _remote(job, j, r, src, dst, to, send, recv):
    return pltpu.make_async_remote_copy(src_ref=src, dst_ref=dst, send_sem=send.at[j, r - 1], recv_sem=recv.at[j, r - 1],
                                        device_id=to, device_id_type=pl.DeviceIdType.MESH)


def _exchange_start(jobs, ins, outs, send, recv, loc):
    pos = _mesh_pos()
    me = 4 * pos[0] + 2 * pos[1] + pos[2]
    for j, job in enumerate(jobs):
        pltpu.make_async_copy(job.src_view(ins, me), job.dst_view(outs, me), loc.at[j]).start()
        for r in (range(1, N_DEV) if job.scatter else (SIBLING,) + OTHER_CHIPS):
            peer, peer_idx = _peer(pos, r)
            _remote(job, j, r, job.src_view(ins, peer_idx), job.dst_view(outs, me), peer, send, recv).start()


def _exchange_forward(jobs, ins, outs, send, recv, loc):
    pos = _mesh_pos()
    sibling, _ = _peer(pos, SIBLING)
    for j, job in enumerate(jobs):
        if job.scatter:
            continue
        for r in OTHER_CHIPS:
            peer, peer_idx = _peer(pos, r)
            slot = job.dst_view(outs, peer_idx)
            _remote(job, j, r, job.src_view(ins, peer_idx), slot, peer, send, recv).wait_recv()
            _remote(job, j, r ^ SIBLING, slot, slot, sibling, send, recv).start()


def _exchange_wait(jobs, ins, outs, send, recv, loc):
    pos = _mesh_pos()
    me = 4 * pos[0] + 2 * pos[1] + pos[2]
    for j, job in enumerate(jobs):
        for r in range(1, N_DEV):
            peer, peer_idx = _peer(pos, r)
            cp = _remote(job, j, r, job.src_view(ins, peer_idx), job.dst_view(outs, peer_idx), peer, send, recv)
            if job.scatter or r not in OTHER_CHIPS:
                cp.wait_recv()
            cp.wait_send()
        pltpu.make_async_copy(job.src_view(ins, me), job.dst_view(outs, me), loc.at[j]).wait()


def _exchange_sems(n_jobs):
    return [pltpu.SemaphoreType.DMA((n_jobs, N_DEV - 1)), pltpu.SemaphoreType.DMA((n_jobs, N_DEV - 1)),
            pltpu.SemaphoreType.DMA((n_jobs,))]


def _exchange(jobs, arrs, out_shape, name):
    n_in, n_out = len(arrs), len(out_shape)
    any_spec = pl.BlockSpec(memory_space=pl.ANY)

    def body(*refs):
        ins, outs, sems = refs[:n_in], refs[n_in:n_in + n_out], refs[n_in + n_out:]
        _exchange_start(jobs, ins, outs, *sems)
        _exchange_forward(jobs, ins, outs, *sems)
        _exchange_wait(jobs, ins, outs, *sems)

    return pl.pallas_call(
        body, out_shape=out_shape, in_specs=[any_spec] * n_in, out_specs=[any_spec] * n_out,
        scratch_shapes=_exchange_sems(len(jobs)),
        name=name, compiler_params=pltpu.CompilerParams(has_side_effects=True))(*arrs)


def _all_reduce_small(placed, cuts, rows, w, name):
    n_in = len(placed)

    def body(*refs):
        in_refs, o_refs = refs[:n_in], refs[n_in:n_in + len(cuts)]
        buf, send, recv = refs[n_in + len(cuts):]
        pos = _mesh_pos()
        me = 4 * pos[0] + 2 * pos[1] + pos[2]
        buf[me] = jnp.zeros((rows, w), F32)
        for ref, (arr, row0, col0, is_partial) in zip(in_refs, placed):
            val = ref[...].sum(axis=0, keepdims=True) if is_partial else ref[...]
            buf[me, row0:row0 + val.shape[0], col0:col0 + val.shape[1]] = val
        for r in range(1, N_DEV):
            peer, _ = _peer(pos, r)
            pltpu.make_async_remote_copy(src_ref=buf.at[me], dst_ref=buf.at[me], send_sem=send.at[r - 1],
                                         recv_sem=recv.at[r - 1], device_id=peer,
                                         device_id_type=pl.DeviceIdType.MESH).start()
        for r in range(1, N_DEV):
            peer, peer_idx = _peer(pos, r)
            cp = pltpu.make_async_remote_copy(src_ref=buf.at[me], dst_ref=buf.at[peer_idx], send_sem=send.at[r - 1],
                                              recv_sem=recv.at[r - 1], device_id=peer,
                                              device_id_type=pl.DeviceIdType.MESH)
            cp.wait_recv()
            cp.wait_send()
        acc = buf[0]
        for dev in range(1, N_DEV):
            acc = acc + buf[dev]
        for o_ref, (_, pieces) in zip(o_refs, cuts):
            for index, row0, nrows, col0, ncols in pieces:
                o_ref[index] = acc[row0:row0 + nrows, col0:col0 + ncols].reshape(o_ref.at[index].shape)

    vmem = pl.BlockSpec(memory_space=pltpu.VMEM)
    return pl.pallas_call(
        body, out_shape=[SDS(shape, F32) for shape, _ in cuts], in_specs=[vmem] * n_in, out_specs=[vmem] * len(cuts),
        scratch_shapes=[pltpu.VMEM((N_DEV, rows, w), F32), pltpu.SemaphoreType.DMA((N_DEV - 1,)),
                        pltpu.SemaphoreType.DMA((N_DEV - 1,))],
        name=name, compiler_params=pltpu.CompilerParams(has_side_effects=True))(*[p[0] for p in placed])


def _adamw_math(w, g, m, v):
    m = ADAM_B1 * m + (1.0 - ADAM_B1) * g
    v = ADAM_B2 * v + (1.0 - ADAM_B2) * (g * g)
    m_hat = m / (1.0 - ADAM_B1 ** ADAM_STEP)
    v_hat = v / (1.0 - ADAM_B2 ** ADAM_STEP)
    delta = -ADAM_LR * (m_hat / (jnp.sqrt(v_hat) + ADAM_EPS) + ADAM_WD * w)
    return delta, m, v


def _adamw_shard(parts, w, m, v, name):
    depth = len(parts)
    _, rr, cc = parts[0].shape
    tr = _tile(rr, 256, 8)
    nt = rr // tr
    part_block, blk = (N_DEV, tr, cc), pl.BlockSpec((None, tr, cc), lambda l, i: (l, i, 0))

    def body(*refs):
        p_refs = refs[:depth]
        w_ref, m_ref, v_ref, g_out, d_out, m_out, v_out = refs[depth:]
        for li in range(depth):
            @pl.when(pl.program_id(0) == li)
            def _(p_ref=p_refs[li]):
                g = p_ref[0].astype(F32)
                for dev in range(1, N_DEV):
                    g = g + p_ref[dev].astype(F32)
                delta, mm, vv = _adamw_math(w_ref[...], g, m_ref[...], v_ref[...])
                g_out[...] = g
                d_out[...] = delta
                m_out[...] = mm
                v_out[...] = vv

    def part_spec(li):
        return pl.BlockSpec(part_block, lambda l, i: (0, jnp.where(l == li, i, jnp.where(l < li, 0, nt - 1)), 0))

    return pl.pallas_call(
        body, out_shape=[SDS(w.shape, F32)] * 4, grid=(depth, nt),
        in_specs=[part_spec(li) for li in range(depth)] + [blk, blk, blk],
        out_specs=[blk] * 4, name=name, compiler_params=_cp())(*parts, w, m, v)


def _adamw_small(gs, ws, ms, vs, name):
    n = len(gs)

    def body(*refs):
        g_refs, w_refs, m_refs, v_refs = (refs[k * n:(k + 1) * n] for k in range(4))
        outs = refs[4 * n:]
        for k in range(n):
            delta, mm, vv = _adamw_math(w_refs[k][...], g_refs[k][...], m_refs[k][...], v_refs[k][...])
            outs[k][...] = delta
            outs[n + k][...] = mm
            outs[2 * n + k][...] = vv

    res = pl.pallas_call(body, out_shape=[SDS(w.shape, F32) for w in ws] * 3, name=name,
                         compiler_params=_cp())(*gs, *ws, *ms, *vs)
    return res[:n], res[n:2 * n], res[2 * n:]


def _from_cols(t):
    return jnp.transpose(t, (1, 0, 2)).reshape(t.shape[1], N_DEV * t.shape[2])


def _swap(t):
    return jnp.swapaxes(t, -1, -2)


def kernel(x, meta_tokens, mix_norm_g, w_in, conv_dw_w, conv_dw_b, conv_ln_g, conv_ln_b, w_out, ffn_norm_g, w_gate, w_up, w_down, final_norm_g, loss_target, m_meta_tokens, m_mix_norm_g, m_w_in, m_conv_dw_w, m_conv_dw_b, m_conv_ln_g, m_conv_ln_b, m_w_out, m_ffn_norm_g, m_w_gate, m_w_up, m_w_down, m_final_norm_g, v_meta_tokens, v_mix_norm_g, v_w_in, v_conv_dw_w, v_conv_dw_b, v_conv_ln_g, v_conv_ln_b, v_w_out, v_ffn_norm_g, v_w_gate, v_w_up, v_w_down, v_final_norm_g):
    depth, d, in_shard = w_in.shape
    seq = x.shape[1]
    sb = N_HEADS * HEAD_DIM
    cc = conv_dw_w.shape[2] * N_DEV
    ff = w_gate.shape[2] * N_DEV
    assert in_shard * N_DEV == 3 * sb + 2 * cc and x.shape[0] == 1
    lr = N_META + seq
    lp = -(-lr // ABLK) * ABLK
    me = 4 * lax.axis_index("x") + 2 * lax.axis_index("y") + lax.axis_index("c")

    big_names = ("w_in", "w_out", "w_gate", "w_up", "w_down")
    transposed = {"w_in": True, "w_out": False, "w_gate": True, "w_up": True, "w_down": False}
    shard = dict(w_in=_swap(w_in).astype(BF16), w_out=w_out.astype(BF16), w_gate=_swap(w_gate).astype(BF16),
                 w_up=_swap(w_up).astype(BF16), w_down=w_down.astype(BF16))

    def gather_of(keys):
        names = sorted({n for n, _ in keys}, key=big_names.index)
        jobs = [_Job(names.index(n), j, False, src_layer=i) for j, (n, i) in enumerate(keys)]
        return jobs, [shard[n] for n in names], [SDS((N_DEV,) + shard[n].shape[1:], BF16) for n, _ in keys]

    first_keys = [("w_in", 0)]

    def riding(keys):
        return _Rider(*gather_of(keys)) if keys else None

    def receive(keys, arrays):
        for (n, li), t in zip(keys, arrays):
            wl[li][n] = t.reshape(-1, d)

    jobs, srcs, out_shape = gather_of(first_keys)
    for extra in (meta_tokens, conv_dw_w):
        jobs.append(_Job(len(srcs), len(out_shape), False))
        srcs.append(extra)
        out_shape.append(SDS((N_DEV,) + extra.shape, F32))
    gathered = _exchange(jobs, srcs, out_shape, "gather_first")
    wl = [dict() for _ in range(depth)]
    receive(first_keys, gathered)
    meta_full = _from_cols(gathered[-2])
    taps = jnp.transpose(gathered[-1], (1, 2, 0, 3)).reshape(depth, CONV_K, cc)
    taps = jnp.pad(taps, ((0, 0), (0, CONV_PAD - CONV_K), (0, 0)))
    tri_fwd, tri_after, tri_before = _tri_consts()

    h = jnp.concatenate([meta_full, x[0], jnp.zeros((lp - lr, d), F32)], axis=0)
    saved = []
    for i in range(depth):
        p = wl[i]
        sv = dict(h_in=h)
        qkv, hn = _rms_mm(h, mix_norm_g[i:i + 1], p["w_in"], 0, 3 * sb, BF16, f"proj_qkv_{i}", True)
        keys = [("w_out", 0)] if i == 0 else []
        cacg, *arrived = _rms_mm(h, mix_norm_g[i:i + 1], p["w_in"], 3 * sb, 2 * cc, F32, f"proj_conv_{i}", False,
                                 riding(keys))
        receive(keys, arrived)
        keys = [("w_gate", i), ("w_up", i)]
        attn, rsave, *arrived = _attn_fwd(qkv, tri_fwd, f"attn_fwd_{i}", riding(keys))
        receive(keys, arrived)
        keys = [("w_down", 0)] if i == 0 else []
        yc, *arrived = _conv_fwd_dw(cacg, taps[i], conv_dw_b[i:i + 1], f"conv_fwd_dw_{i}", riding(keys))
        receive(keys, arrived)
        conv = _conv_fwd_ln(yc, conv_ln_g[i:i + 1], conv_ln_b[i:i + 1], f"conv_fwd_ln_{i}")
        h = _mix_out(attn, conv, p["w_out"], h, f"mix_out_{i}")
        sv.update(qkv=qkv, hn=hn, cacg=cacg, rsave=rsave, yc=yc, attn=attn, conv=conv, h_mid=h)
        keys = [("w_in", i + 1), ("w_out", i + 1), ("w_down", i + 1)] if i + 1 < depth else []
        h, hn2, act, gate, up, *arrived = _ffn_fwd(h, ffn_norm_g[i:i + 1], p["w_gate"], p["w_up"], p["w_down"],
                                                   f"ffn_fwd_{i}", riding(keys))
        receive(keys, arrived)
        sv.update(hn2=hn2, act=act, gate=gate, up=up)
        saved.append(sv)

    tpad = jnp.pad(loss_target[0], ((N_META, lp - lr), (0, 0)))
    dh, dg_final, loss_part = _loss_head(h, tpad, final_norm_g.reshape(1, d), seq, "loss_head")
    loss = lax.psum(loss_part[0, 0], MESH_AXES)

    parts = {}

    def sending(items):
        srcs = [t.reshape(N_DEV, t.shape[0] // N_DEV, d) for _, t in items]
        return _Rider([_Job(j, j, True) for j in range(len(items))], srcs, [SDS(t.shape, BF16) for t in srcs])

    def arrive(items, arrays):
        parts.update({key: t for (key, _), t in zip(items, arrays)})

    grads = [None] * depth
    from_above = []
    for i in reversed(range(depth)):
        p, sv = wl[i], saved[i]
        d_g, d_u, *got = _ffn_bwd_act(dh, sv["gate"], sv["up"], p["w_down"], f"ffn_bwd_act_{i}", sending(from_above))
        arrive(from_above, got)
        gw_down = _mm_tn(sv["act"], dh, f"grad_w_down_{i}")
        gw_gate = _mm_tn(d_g, sv["hn2"], f"grad_w_gate_{i}")
        gw_up = _mm_tn(d_u, sv["hn2"], f"grad_w_up_{i}")
        items = [(("w_down", i), gw_down)]
        dh, dg_ffn, *got = _ffn_bwd_in(d_g, d_u, p["w_gate"], p["w_up"], sv["h_mid"], ffn_norm_g[i:i + 1], dh,
                                       f"ffn_bwd_in_{i}", sending(items))
        arrive(items, got)
        gw_out = jnp.concatenate([_mm_tn(sv["attn"], dh, f"grad_w_out_attn_{i}"),
                                  _mm_tn(sv["conv"], dh, f"grad_w_out_conv_{i}")], axis=0)
        d_attn = _mm_nt(dh, p["w_out"], 0, sb, BF16, f"mix_bwd_attn_{i}")
        d_conv = _mm_nt(dh, p["w_out"], sb, cc, F32, f"mix_bwd_conv_{i}")
        items = [(("w_gate", i), gw_gate), (("w_up", i), gw_up)]
        dqkv, *got = _attn_bwd(sv["qkv"], d_attn, sv["rsave"], tri_after, tri_before, f"attn_bwd_{i}", sending(items))
        arrive(items, got)
        dyc, dg_ln, db_ln = _conv_bwd_ln(sv["yc"], d_conv, conv_ln_g[i:i + 1], conv_ln_b[i:i + 1], f"conv_bwd_ln_{i}")
        dcc, g_taps_i, db_conv = _conv_bwd_dw(dyc, sv["cacg"], taps[i], f"conv_bwd_dw_{i}")
        items = [(("w_out", i), gw_out)] if i == 0 else []
        gw_in, *got = _grad_w_in_t(dqkv, dcc, sv["hn"], f"grad_w_in_{i}", sending(items))
        arrive(items, got)
        from_above = [(("w_in", i), gw_in)] + ([(("w_out", i), gw_out)] if i > 0 else [])
        items = from_above if i == 0 else []
        dh, dg_mix, *got = _mix_bwd_in(dqkv, dcc, p["w_in"], sv["h_in"], mix_norm_g[i:i + 1], dh, f"mix_bwd_in_{i}",
                                       sending(items))
        arrive(items, got)
        grads[i] = dict(taps=g_taps_i, dg_mix=dg_mix, dg_ffn=dg_ffn, dg_ln=dg_ln, db_ln=db_ln, db_conv=db_conv)
    grad_x = dh[N_META:lr][None]

    per = d // depth
    assert depth * cc <= d and d % depth == 0 and per % 128 == 0
    placed, cuts, row = [], [], 0
    for key in ("dg_mix", "dg_ffn"):
        placed += [(grads[i][key], row + i, 0, True) for i in range(depth)]
        cuts.append(((depth, d), [(slice(0, depth), row, depth, 0, d)]))
        row += depth
    placed.append((dg_final, row, 0, True))
    cuts.append(((1, d), [(slice(0, 1), row, 1, 0, d)]))
    row += 1
    for key in ("db_conv", "dg_ln", "db_ln"):
        placed += [(grads[i][key], row, i * per, True) for i in range(depth)]
        cuts.append(((depth, cc), [(slice(i, i + 1), row, 1, i * per, cc) for i in range(depth)]))
        row += 1
    placed += [(grads[i]["taps"], row, i * per, False) for i in range(depth)]
    cuts.append(((depth, CONV_PAD, cc), [(i, row, CONV_PAD, i * per, cc) for i in range(depth)]))
    row += CONV_PAD
    placed.append((dh[:N_META], row, 0, False))
    cuts.append(((N_META, d), [(slice(0, N_META), row, N_META, 0, d)]))
    row += N_META
    g_mix, g_ffn, g_final, g_cb, g_lg, g_lb, g_taps_full, g_meta_full = _all_reduce_small(
        placed, cuts, -(-row // 8) * 8, d, "reduce_small")
    csh = cc // N_DEV
    g_taps_own = lax.dynamic_slice_in_dim(g_taps_full[:, :CONV_K], me * csh, csh, axis=2)
    msh = d // N_DEV
    g_meta_own = lax.dynamic_slice_in_dim(g_meta_full, me * msh, msh, axis=1)

    row1 = lambda t: t.reshape(1, d)
    small_g = [g_meta_own, g_mix, g_taps_own, g_cb, g_lg, g_lb, g_ffn, g_final]
    small_w = [meta_tokens, mix_norm_g, conv_dw_w, conv_dw_b, conv_ln_g, conv_ln_b, ffn_norm_g, row1(final_norm_g)]
    small_m = [m_meta_tokens, m_mix_norm_g, m_conv_dw_w, m_conv_dw_b, m_conv_ln_g, m_conv_ln_b, m_ffn_norm_g,
               row1(m_final_norm_g)]
    small_v = [v_meta_tokens, v_mix_norm_g, v_conv_dw_w, v_conv_dw_b, v_conv_ln_g, v_conv_ln_b, v_ffn_norm_g,
               row1(v_final_norm_g)]
    s_delta, s_m, s_v = _adamw_small(small_g, small_w, small_m, small_v, "adamw_small")
    unrow = lambda ts: list(ts[:-1]) + [ts[-1].reshape(d)]
    small_g, s_delta, s_m, s_v = unrow(small_g), unrow(s_delta), unrow(s_m), unrow(s_v)

    big = []
    for n, w, m, v in zip(big_names, (w_in, w_out, w_gate, w_up, w_down), (m_w_in, m_w_out, m_w_gate, m_w_up, m_w_down),
                          (v_w_in, v_w_out, v_w_gate, v_w_up, v_w_down)):
        fix = _swap if transposed[n] else (lambda t: t)
        res = _adamw_shard([parts[(n, i)] for i in range(depth)], fix(w), fix(m), fix(v), f"adamw_{n}")
        big.append([fix(t) for t in res])
    b_in, b_out, b_gate, b_up, b_down = big

    def ordered(k, smalls):
        s_meta, s_mix, s_taps, s_cb, s_lg, s_lb, s_ffn, s_final = smalls
        return [s_meta, s_mix, b_in[k], s_taps, s_cb, s_lg, s_lb, b_out[k], s_ffn, b_gate[k], b_up[k], b_down[k], s_final]

    return (loss, grad_x, *ordered(0, small_g), *ordered(1, s_delta), *ordered(2, s_m), *ordered(3, s_v))
```

```python
import functools
import math

import jax
import jax.numpy as jnp
from jax import lax
from jax.experimental import pallas as pl
from jax.experimental.pallas import tpu as pltpu

F32 = jnp.float32
BF16 = jnp.bfloat16
SDS = jax.ShapeDtypeStruct

N_META = 16
N_HEADS = 8
HEAD_DIM = 64
CONV_K = 31
CONV_PAD = 32
ABLK = 128
ATT_GROUP = 4
ATT_SPLIT = 2
ATT_TOP = 2
EXP_ZERO_AT = 104.0
GONE = 1e30
MASKED = -1e30
FF_CHUNK = 256
FF_BLOCK = 1408
FF_ROWS = 528
EPS = 1e-6
N_DEV = 8
MESH_AXES = ("x", "y", "c")
ADAM_LR = 0.001
ADAM_B1 = 0.9
ADAM_B2 = 0.999
ADAM_EPS = 1e-08
ADAM_WD = 0.01
ADAM_STEP = 10
MIB = 1 << 20
VMEM_LIMIT_MIB = 48
FFN_FWD_VMEM_MIB = 56
MM_ROWS = 1056
EW_ROWS = 528


def _cp(mib=None):
    return pltpu.CompilerParams(vmem_limit_bytes=(mib or VMEM_LIMIT_MIB) * MIB)


def _tile(n, cap, mult):
    best = None
    for t in range(mult, min(n, cap) + 1, mult):
        if n % t == 0:
            best = t
    assert best is not None, (n, cap, mult)
    return best


def _sum8(v):
    r, c = v.shape
    return v.reshape(r // 8, 8, c).sum(axis=0)


def _sigmoid(x):
    return 1.0 / (1.0 + jnp.exp(-x))


def _ff_block(f):
    fb = _tile(f, FF_BLOCK, 128)
    return fb, [(s0, min(FF_CHUNK, fb - s0)) for s0 in range(0, fb, FF_CHUNK)]


NT = (((1,), (1,)), ((), ()))
TN = (((0,), (0,)), ((), ()))


def _rms_mm(h, g, wt, w_row0, n_cols, out_dtype, name, hn_out, rider=None):
    lp, d = h.shape
    tm = _tile(lp, MM_ROWS, 16)
    tn = _tile(math.gcd(n_cols, w_row0), 768, 128)
    off = w_row0 // tn
    ni, nj = lp // tm, n_cols // tn
    rider = rider or _Rider()

    def body(h_ref, g_ref, w_ref, *rest):
        outs, (hn_s,) = rider.split(rest, 2 if hn_out else 1, 1)
        o_ref = outs[0]
        hn_ref = outs[1] if hn_out else None
        rider.start((pl.program_id(0) == 0) & (pl.program_id(1) == 0))
        if ni > 1:
            rider.forward((pl.program_id(0) == ni - 1) & (pl.program_id(1) == 0))

        @pl.when(pl.program_id(1) == 0)
        def _():
            x = h_ref[...]
            r = lax.rsqrt(jnp.mean(x * x, axis=-1, keepdims=True) + EPS)
            hn = ((x * r) * g_ref[...]).astype(BF16)
            hn_s[...] = hn
            if hn_out:
                hn_ref[...] = hn

        o_ref[...] = lax.dot_general(hn_s[...], w_ref[...], NT, preferred_element_type=F32).astype(out_dtype)
        rider.wait((pl.program_id(0) == ni - 1) & (pl.program_id(1) == nj - 1))

    out_shape = [SDS((lp, n_cols), out_dtype)]
    out_specs = [pl.BlockSpec((tm, tn), lambda i, j: (i, j))]
    if hn_out:
        out_shape.append(SDS((lp, d), BF16))
        out_specs.append(pl.BlockSpec((tm, d), lambda i, j: (i, 0)))
    return pl.pallas_call(
        body, out_shape=out_shape + rider.out_shape, grid=(ni, nj),
        in_specs=[pl.BlockSpec((tm, d), lambda i, j: (i, 0)),
                  pl.BlockSpec((1, d), lambda i, j: (0, 0)),
                  pl.BlockSpec((tn, d), lambda i, j: (j + off, 0))] + rider.in_specs(),
        out_specs=out_specs + rider.out_specs(), scratch_shapes=[pltpu.VMEM((tm, d), BF16)] + rider.scratch(),
        name=name, compiler_params=_cp())(h, g, wt, *rider.srcs)


def _mm_nt(a, b, b_row0, n_out, out_dtype, name):
    m, k = a.shape
    tm = _tile(m, MM_ROWS, 16)
    tn = _tile(math.gcd(n_out, b_row0), 512, 128)
    off = b_row0 // tn

    def body(a_ref, b_ref, o_ref):
        o_ref[...] = lax.dot_general(a_ref[...].astype(BF16), b_ref[...], NT,
                                     preferred_element_type=F32).astype(out_dtype)

    return pl.pallas_call(
        body, out_shape=SDS((m, n_out), out_dtype), grid=(m // tm, n_out // tn),
        in_specs=[pl.BlockSpec((tm, k), lambda i, j: (i, 0)), pl.BlockSpec((tn, k), lambda i, j: (j + off, 0))],
        out_specs=pl.BlockSpec((tm, tn), lambda i, j: (i, j)), name=name, compiler_params=_cp())(a, b)


def _mix_out(attn, conv, w, res, name):
    m, ka = attn.shape
    kc = conv.shape[1]
    n = w.shape[1]
    assert ka == kc
    tm = _tile(m, MM_ROWS, 16)
    tn = _tile(n, 512, 128)

    def body(a_ref, c_ref, wa_ref, wc_ref, r_ref, o_ref):
        o_ref[...] = (r_ref[...] + jnp.dot(a_ref[...], wa_ref[...], preferred_element_type=F32)
                      + jnp.dot(c_ref[...], wc_ref[...], preferred_element_type=F32))

    return pl.pallas_call(
        body, out_shape=SDS((m, n), F32), grid=(m // tm, n // tn),
        in_specs=[pl.BlockSpec((tm, ka), lambda i, j: (i, 0)), pl.BlockSpec((tm, kc), lambda i, j: (i, 0)),
                  pl.BlockSpec((ka, tn), lambda i, j: (0, j)), pl.BlockSpec((kc, tn), lambda i, j: (1, j)),
                  pl.BlockSpec((tm, tn), lambda i, j: (i, j))],
        out_specs=pl.BlockSpec((tm, tn), lambda i, j: (i, j)), name=name, compiler_params=_cp())(attn, conv, w, w, res)


def _mm_tn(a, b, name):
    l, m = a.shape
    n = b.shape[1]
    tm = _tile(m, 1408, 128)
    tn = _tile(n, 1024, 128)
    tl = _tile(l, 1408, 128)
    nl = l // tl

    def body(a_ref, b_ref, o_ref, acc):
        @pl.when(pl.program_id(2) == 0)
        def _():
            acc[...] = jnp.zeros_like(acc)

        acc[...] += lax.dot_general(a_ref[...].astype(BF16), b_ref[...].astype(BF16), TN, preferred_element_type=F32)

        @pl.when(pl.program_id(2) == nl - 1)
        def _():
            o_ref[...] = acc[...].astype(BF16)

    return pl.pallas_call(
        body, out_shape=SDS((m, n), BF16), grid=(m // tm, n // tn, nl),
        in_specs=[pl.BlockSpec((tl, tm), lambda i, j, s: (s, i)),
                  pl.BlockSpec((tl, tn), lambda i, j, s: (s, j))],
        out_specs=pl.BlockSpec((tm, tn), lambda i, j, s: (i, j)),
        scratch_shapes=[pltpu.VMEM((tm, tn), F32)], name=name, compiler_params=_cp())(a, b)


def _grad_w_in_t(dqkv, dcc, hn, name, rider=None):
    nq, l, w = dqkv.shape
    nc = dcc.shape[0]
    d = hn.shape[1]
    assert dcc.shape[2] == w
    tn = _tile(d, 1024, 128)
    tl = _tile(l, 1408, 128)
    nl = l // tl
    nj = d // tn
    rider = rider or _Rider()

    def body(q_ref, c_ref, b_ref, *rest):
        (o_ref,), (acc,) = rider.split(rest, 1, 1)
        p, jj, s = pl.program_id(0), pl.program_id(1), pl.program_id(2)
        rider.start((p == 0) & (jj == 0) & (s == 0))

        @pl.when(s == 0)
        def _():
            acc[...] = jnp.zeros_like(acc)

        @pl.when(p < nq)
        def _():
            acc[...] += lax.dot_general(q_ref[...], b_ref[...], TN, preferred_element_type=F32)

        @pl.when(p >= nq)
        def _():
            acc[...] += lax.dot_general(c_ref[...], b_ref[...], TN, preferred_element_type=F32)

        @pl.when(s == nl - 1)
        def _():
            o_ref[...] = acc[...].astype(BF16)

        rider.wait((p == nq + nc - 1) & (jj == nj - 1) & (s == nl - 1))

    out, *arrived = pl.pallas_call(
        body, out_shape=[SDS((nq + nc, w, d), BF16)] + rider.out_shape, grid=(nq + nc, nj, nl),
        in_specs=[pl.BlockSpec((None, tl, w), lambda p, j, s: (jnp.minimum(p, nq - 1), s, 0)),
                  pl.BlockSpec((None, tl, w), lambda p, j, s: (jnp.maximum(p - nq, 0), s, 0)),
                  pl.BlockSpec((tl, tn), lambda p, j, s: (s, j))] + rider.in_specs(),
        out_specs=[pl.BlockSpec((None, w, tn), lambda p, j, s: (p, 0, j))] + rider.out_specs(),
        scratch_shapes=[pltpu.VMEM((w, tn), F32)] + rider.scratch(), name=name,
        compiler_params=_cp())(dqkv, dcc, hn, *rider.srcs)
    return [out.reshape((nq + nc) * w, d)] + arrived


def _rms_bwd_tail(acc, h_ref, g_ref, dres_ref, o_ref, dg_ref):
    x = h_ref[...]
    r = lax.rsqrt(jnp.mean(x * x, axis=-1, keepdims=True) + EPS)
    xr = x * r
    dyv = acc[...]
    gy = dyv * g_ref[...]
    o_ref[...] = dres_ref[...] + r * (gy - xr * jnp.mean(gy * xr, axis=-1, keepdims=True))
    dg_ref[...] += _sum8(dyv * xr)


def _mix_bwd_in(dqkv, dcc, wt, h, g, dres, name, rider=None):
    nq, lp, w = dqkv.shape
    nc = dcc.shape[0]
    d = h.shape[1]
    tm = _tile(lp, FF_ROWS, 16)
    ni = lp // tm
    rider = rider or _Rider()

    def body(q_ref, c_ref, w_ref, h_ref, g_ref, dres_ref, *rest):
        (o_ref, dg_ref), (acc,) = rider.split(rest, 2, 1)
        i = pl.program_id(0)
        rider.start(i == 0)

        @pl.when(i == 0)
        def _():
            dg_ref[...] = jnp.zeros_like(dg_ref)

        pieces = [q_ref[p] for p in range(nq)] + [c_ref[p] for p in range(nc)]
        total = None
        for p, piece in enumerate(pieces):
            part = jnp.dot(piece, w_ref[p * w:(p + 1) * w, :], preferred_element_type=F32)
            total = part if total is None else total + part
        acc[...] = total
        _rms_bwd_tail(acc, h_ref, g_ref, dres_ref, o_ref, dg_ref)
        rider.wait(i == ni - 1)

    return pl.pallas_call(
        body, out_shape=[SDS((lp, d), F32), SDS((8, d), F32)] + rider.out_shape, grid=(ni,),
        in_specs=[pl.BlockSpec((nq, tm, w), lambda i: (0, i, 0)),
                  pl.BlockSpec((nc, tm, w), lambda i: (0, i, 0)),
                  pl.BlockSpec(((nq + nc) * w, d), lambda i: (0, 0)),
                  pl.BlockSpec((tm, d), lambda i: (i, 0)),
                  pl.BlockSpec((1, d), lambda i: (0, 0)),
                  pl.BlockSpec((tm, d), lambda i: (i, 0))] + rider.in_specs(),
        out_specs=[pl.BlockSpec((tm, d), lambda i: (i, 0)), pl.BlockSpec((8, d), lambda i: (0, 0))]
        + rider.out_specs(),
        scratch_shapes=[pltpu.VMEM((tm, d), F32)] + rider.scratch(), name=name,
        compiler_params=_cp())(dqkv, dcc, wt, h, g, dres, *rider.srcs)


def _ffn_bwd_in(d_g, d_u, wg_t, wu_t, h, g, dres, name, rider=None):
    lp, f = d_g.shape
    d = h.shape[1]
    tm = _tile(lp, FF_ROWS, 16)
    tk = _tile(f, FF_BLOCK, 128)
    nk = f // tk
    ni = lp // tm
    rider = rider or _Rider()

    def body(dg_in, du_in, wg_ref, wu_ref, h_ref, g_ref, dres_ref, *rest):
        (o_ref, dg_ref), (acc,) = rider.split(rest, 2, 1)
        i, kk = pl.program_id(0), pl.program_id(1)
        rider.start((i == 0) & (kk == 0))

        @pl.when(kk == 0)
        def _():
            acc[...] = jnp.zeros_like(acc)

        @pl.when((kk == 0) & (i == 0))
        def _():
            dg_ref[...] = jnp.zeros_like(dg_ref)

        acc[...] += (jnp.dot(dg_in[...], wg_ref[...], preferred_element_type=F32)
                     + jnp.dot(du_in[...], wu_ref[...], preferred_element_type=F32))

        @pl.when(kk == nk - 1)
        def _():
            _rms_bwd_tail(acc, h_ref, g_ref, dres_ref, o_ref, dg_ref)

        rider.wait((i == ni - 1) & (kk == nk - 1))

    return pl.pallas_call(
        body, out_shape=[SDS((lp, d), F32), SDS((8, d), F32)] + rider.out_shape, grid=(ni, nk),
        in_specs=[pl.BlockSpec((tm, tk), lambda i, j: (i, j)), pl.BlockSpec((tm, tk), lambda i, j: (i, j)),
                  pl.BlockSpec((tk, d), lambda i, j: (j, 0)), pl.BlockSpec((tk, d), lambda i, j: (j, 0)),
                  pl.BlockSpec((tm, d), lambda i, j: (i, 0)),
                  pl.BlockSpec((1, d), lambda i, j: (0, 0)),
                  pl.BlockSpec((tm, d), lambda i, j: (i, 0))] + rider.in_specs(),
        out_specs=[pl.BlockSpec((tm, d), lambda i, j: (i, 0)), pl.BlockSpec((8, d), lambda i, j: (0, 0))]
        + rider.out_specs(),
        scratch_shapes=[pltpu.VMEM((tm, d), F32)] + rider.scratch(), name=name,
        compiler_params=_cp())(d_g, d_u, wg_t, wu_t, h, g, dres, *rider.srcs)


def _ffn_fwd(h, g, wg_t, wu_t, w_d, name, rider=None):
    lp, d = h.shape
    f = w_d.shape[0]
    tm = _tile(lp, FF_ROWS, 16)
    fb, subs = _ff_block(f)
    nc = f // fb
    ni = lp // tm
    rider = rider or _Rider()

    def body(h_ref, g_ref, wg_ref, wu_ref, wd_ref, *rest):
        (o_ref, hn_ref, act_ref, gate_ref, up_ref), (hn_s, acc) = rider.split(rest, 5, 2)
        c = pl.program_id(1)
        rider.start((pl.program_id(0) == 0) & (c == 0))
        if ni > 1:
            rider.forward((pl.program_id(0) == ni - 1) & (c == 0))

        @pl.when(c == 0)
        def _():
            x = h_ref[...]
            r = lax.rsqrt(jnp.mean(x * x, axis=-1, keepdims=True) + EPS)
            hn = ((x * r) * g_ref[...]).astype(BF16)
            hn_s[...] = hn
            hn_ref[...] = hn
            acc[...] = jnp.zeros_like(acc)

        total = None
        for s0, sw in subs:
            gg = lax.dot_general(hn_s[...], wg_ref[s0:s0 + sw, :], NT, preferred_element_type=F32)
            uu = lax.dot_general(hn_s[...], wu_ref[s0:s0 + sw, :], NT, preferred_element_type=F32)
            gate_ref[:, s0:s0 + sw] = gg
            up_ref[:, s0:s0 + sw] = uu
            act = ((gg * _sigmoid(gg)) * uu).astype(BF16)
            act_ref[:, s0:s0 + sw] = act
            part = jnp.dot(act, wd_ref[s0:s0 + sw, :], preferred_element_type=F32)
            total = part if total is None else total + part
        acc[...] += total

        @pl.when(c == nc - 1)
        def _():
            o_ref[...] = h_ref[...] + acc[...]

        rider.wait((pl.program_id(0) == ni - 1) & (c == nc - 1))

    chunk = pl.BlockSpec((fb, d), lambda i, j: (j, 0))
    wide = pl.BlockSpec((tm, fb), lambda i, j: (i, j))
    return pl.pallas_call(
        body, out_shape=[SDS((lp, d), F32), SDS((lp, d), BF16), SDS((lp, f), BF16), SDS((lp, f), F32),
                         SDS((lp, f), F32)] + rider.out_shape, grid=(ni, nc),
        in_specs=[pl.BlockSpec((tm, d), lambda i, j: (i, 0)), pl.BlockSpec((1, d), lambda i, j: (0, 0)),
                  chunk, chunk, chunk] + rider.in_specs(),
        out_specs=[pl.BlockSpec((tm, d), lambda i, j: (i, 0)), pl.BlockSpec((tm, d), lambda i, j: (i, 0)),
                   wide, wide, wide] + rider.out_specs(),
        scratch_shapes=[pltpu.VMEM((tm, d), BF16), pltpu.VMEM((tm, d), F32)] + rider.scratch(),
        name=name, compiler_params=_cp(FFN_FWD_VMEM_MIB))(h, g, wg_t, wu_t, w_d, *rider.srcs)


def _ffn_bwd_act(dh, gate, up, w_d, name, rider=None):
    lp, d = dh.shape
    f = w_d.shape[0]
    tm = _tile(lp, FF_ROWS, 16)
    fb, subs = _ff_block(f)
    nc = f // fb
    ni = lp // tm
    rider = rider or _Rider()

    def body(dh_ref, gate_ref, up_ref, wd_ref, *rest):
        (dg_out, du_out), (dh_s,) = rider.split(rest, 2, 1)
        rider.start((pl.program_id(0) == 0) & (pl.program_id(1) == 0))

        @pl.when(pl.program_id(1) == 0)
        def _():
            dh_s[...] = dh_ref[...].astype(BF16)

        for s0, sw in subs:
            gg = gate_ref[:, s0:s0 + sw]
            uu = up_ref[:, s0:s0 + sw]
            dact = lax.dot_general(dh_s[...], wd_ref[s0:s0 + sw, :], NT, preferred_element_type=F32)
            s = _sigmoid(gg)
            dg_out[:, s0:s0 + sw] = (dact * uu * (s * (1.0 + gg * (1.0 - s)))).astype(BF16)
            du_out[:, s0:s0 + sw] = (dact * (gg * s)).astype(BF16)
        rider.wait((pl.program_id(0) == ni - 1) & (pl.program_id(1) == nc - 1))

    wide = pl.BlockSpec((tm, fb), lambda i, j: (i, j))
    return pl.pallas_call(
        body, out_shape=[SDS((lp, f), BF16), SDS((lp, f), BF16)] + rider.out_shape, grid=(ni, nc),
        in_specs=[pl.BlockSpec((tm, d), lambda i, j: (i, 0)), wide, wide,
                  pl.BlockSpec((fb, d), lambda i, j: (j, 0))] + rider.in_specs(),
        out_specs=[wide, wide] + rider.out_specs(), scratch_shapes=[pltpu.VMEM((tm, d), BF16)] + rider.scratch(),
        name=name, compiler_params=_cp())(dh, gate, up, w_d, *rider.srcs)


def _loss_head(h, tpad, g, n_real, name):
    lp, d = h.shape
    tm = _tile(lp, EW_ROWS, 16)

    def body(h_ref, t_ref, g_ref, dh_ref, dg_ref, loss_ref):
        i = pl.program_id(0)

        @pl.when(i == 0)
        def _():
            dg_ref[...] = jnp.zeros_like(dg_ref)
            loss_ref[...] = jnp.zeros_like(loss_ref)

        x = h_ref[...]
        r = lax.rsqrt(jnp.mean(x * x, axis=-1, keepdims=True) + EPS)
        xr = x * r
        y = xr * g_ref[...]
        row = i * tm + lax.broadcasted_iota(jnp.int32, (tm, d), 0)
        valid = (row >= N_META) & (row < N_META + n_real)
        diff = jnp.where(valid, y - t_ref[...], 0.0)
        loss_ref[...] += jnp.sum(diff * diff) * (0.5 / d)
        dy = diff * (1.0 / d)
        gy = dy * g_ref[...]
        dh_ref[...] = r * (gy - xr * jnp.mean(gy * xr, axis=-1, keepdims=True))
        dg_ref[...] += _sum8(dy * xr)

    return pl.pallas_call(
        body, out_shape=[SDS((lp, d), F32), SDS((8, d), F32), SDS((8, 128), F32)], grid=(lp // tm,),
        in_specs=[pl.BlockSpec((tm, d), lambda i: (i, 0)),
                  pl.BlockSpec((tm, d), lambda i: (i, 0)),
                  pl.BlockSpec((1, d), lambda i: (0, 0))],
        out_specs=[pl.BlockSpec((tm, d), lambda i: (i, 0)),
                   pl.BlockSpec((8, d), lambda i: (0, 0)),
                   pl.BlockSpec((8, 128), lambda i: (0, 0))],
        name=name, compiler_params=_cp())(h, tpad, g)


def _tri_consts():
    j = lax.broadcasted_iota(jnp.int32, (ABLK, ABLK), 0)
    s = lax.broadcasted_iota(jnp.int32, (ABLK, ABLK), 1)
    after = (j >= s).astype(BF16)
    before = (j < s).astype(BF16)
    ones = jnp.ones((ABLK, ABLK), BF16)
    two = lambda t: jnp.concatenate([t, t], axis=0)
    return (two(jnp.concatenate([after, ones], axis=1)),
            two(after),
            two(jnp.concatenate([before, ones], axis=1)))


def _softplus(z):
    neg_abs = lax.bitcast_convert_type(lax.bitcast_convert_type(z, jnp.uint32) | jnp.uint32(0x80000000), F32)
    return jnp.log(1.0 + jnp.exp(neg_abs)) + jnp.maximum(z, 0.0)


def _split_hi_lo(m):
    hi = m.astype(BF16)
    lo = (m - hi.astype(F32)).astype(BF16)
    return jnp.concatenate([hi, lo], axis=1)


def _head_halves(t2, in_a):
    zero = jnp.zeros_like(t2)
    return jnp.concatenate([jnp.where(in_a, t2, zero), jnp.where(in_a, zero, t2)], axis=0)


def _stack_blocks(t, nblk, in_a):
    return jnp.concatenate([_head_halves(t[u * ABLK:(u + 1) * ABLK], in_a) for u in range(nblk)], axis=0)


def _interleave(*gens):
    alive = list(gens)
    while alive:
        for g in list(alive):
            if next(g, alive) is alive:
                alive.remove(g)


def _attn_scale():
    scale = 1.0 / math.sqrt(HEAD_DIM)
    assert math.frexp(scale)[0] == 0.5, "a power of two, so that scaling q in bf16 is exact"
    return scale


def _pow2_below(n):
    assert n & (n - 1) == 0
    return [p for p in (64, 32, 16, 8, 4, 2, 1) if p < n]


class _Rider:
    def __init__(self, jobs=(), srcs=(), out_shape=()):
        self.jobs, self.srcs, self.out_shape = list(jobs), list(srcs), list(out_shape)
        self.any = [pl.BlockSpec(memory_space=pl.ANY)]

    def split(self, rest, n_out, n_scratch):
        ni, no = len(self.srcs), len(self.out_shape)
        self.ins, outs = rest[:ni], rest[ni:ni + n_out]
        self.outs = rest[ni + n_out:ni + n_out + no]
        scratch = rest[ni + n_out + no:ni + n_out + no + n_scratch]
        self.sems = rest[ni + n_out + no + n_scratch:]
        return outs, scratch

    def start(self, first):
        self.forwarded = False
        if self.jobs:
            @pl.when(first)
            def _():
                _exchange_start(self.jobs, self.ins, self.outs, *self.sems)

    def forward(self, late):
        self.forwarded = True
        if self.jobs:
            @pl.when(late)
            def _():
                _exchange_forward(self.jobs, self.ins, self.outs, *self.sems)

    def wait(self, last):
        if self.jobs:
            @pl.when(last)
            def _():
                if not self.forwarded:
                    _exchange_forward(self.jobs, self.ins, self.outs, *self.sems)
                _exchange_wait(self.jobs, self.ins, self.outs, *self.sems)

    def in_specs(self):
        return self.any * len(self.srcs)

    def out_specs(self):
        return self.any * len(self.out_shape)

    def scratch(self):
        return _exchange_sems(len(self.jobs)) if self.jobs else []


def _attn_fwd(qkv, tri_fwd, name, rider=None):
    lp = qkv.shape[0]
    n_pairs = (N_HEADS * HEAD_DIM) // 128
    nb = lp // ABLK
    assert nb <= 128 and 2 * HEAD_DIM == 128
    scale = _attn_scale()
    nt = (((1,), (1,)), ((), ()))
    rider = rider or _Rider()

    def body(q_ref, k_ref, v_ref, tri_ref, *rest):
        (o_ref, rs_ref), (r_s, acc_s, rs_s) = rider.split(rest, 2, 3)
        rider.start(pl.program_id(0) == 0)
        lane = lax.broadcasted_iota(jnp.int32, (ABLK, 128), 1)
        row = lax.broadcasted_iota(jnp.int32, (ABLK, 128), 0)
        in_a = lane < HEAD_DIM
        causal = lane < row

        def begin(st, i):
            r_s[st] = jnp.zeros(r_s.shape[1:], F32)
            acc_s[st] = jnp.zeros(acc_s.shape[1:], F32)
            rs_s[st] = jnp.full(rs_s.shape[1:], GONE, F32)
            return q_ref[pl.ds(pl.multiple_of(i * ABLK, ABLK), ABLK), :] * scale

        def live(st):
            least = jnp.min(jnp.minimum(r_s[st, 0], r_s[st, 1]), axis=0, keepdims=True)
            return (least[0, 0] < EXP_ZERO_AT).astype(jnp.int32)

        def step(*args):
            _interleave(stages(*args))

        def stages(st, q2, kb0, nblk, diag):
            k0 = pl.multiple_of(kb0 * ABLK, ABLK)
            kbd = _stack_blocks(k_ref[pl.ds(k0, nblk * ABLK), :], nblk, in_a)
            vbd = _stack_blocks(v_ref[pl.ds(k0, nblk * ABLK), :], nblk, in_a)
            z = lax.dot_general(q2, kbd, nt, preferred_element_type=F32)
            ncol = 2 * nblk
            zt = [z[:, c * 128:(c + 1) * 128] for c in range(ncol)]
            if diag:
                zt = [jnp.where(causal, t, MASKED) if c >= ncol - 2 else t for c, t in enumerate(zt)]
            yield
            bounds = [ncol * j // ATT_SPLIT for j in range(ATT_SPLIT + 1)]
            ce = [None] * ncol
            for c0, c1 in reversed([b for b in zip(bounds[:-1], bounds[1:]) if b[0] < b[1]]):
                parts = [_split_hi_lo(_softplus(zt[c])) for c in range(c0, c1)]
                got = jnp.dot(jnp.concatenate(parts, axis=0), tri_ref[...], preferred_element_type=F32)
                for c in range(c0, c1):
                    ce[c] = got[(c - c0) * 128:(c - c0 + 1) * 128]
                yield
            rr = [r_s[st, 0], r_s[st, 1]]
            rsv = [rs_s[st, :, :128], rs_s[st, :, 128:]]
            ws = [None] * ncol
            for u in reversed(range(nblk)):
                for hh in range(2):
                    c = 2 * u + hh
                    ws[c] = jnp.exp(zt[c] - ce[c][:, :128] - rr[hh]).astype(BF16)
                    rsv[hh] = jnp.where(lane == kb0 + u, rr[hh], rsv[hh])
                    rr[hh] = rr[hh] + ce[c][:, 128:]
            acc_s[st] += jnp.dot(jnp.concatenate(ws, axis=1), vbd, preferred_element_type=F32)
            r_s[st, 0] = rr[0]
            r_s[st, 1] = rr[1]
            rs_s[st, :, :128] = rsv[0]
            rs_s[st, :, 128:] = rsv[1]

        def finish(st, q2, i, n_top):
            i_low = i - n_top
            n_grp = i_low // ATT_GROUP

            def more(c):
                return (c[0] < n_grp) & (c[1] > 0)

            def inner(c):
                step(st, q2, i_low - ATT_GROUP * (c[0] + 1), ATT_GROUP, False)
                return c[0] + 1, live(st)

            _, alive = lax.while_loop(more, inner, (jnp.int32(0), live(st)))
            rem = i_low - ATT_GROUP * n_grp
            for p in _pow2_below(ATT_GROUP):
                def last_steps(p=p):
                    step(st, q2, rem & (p - 1), p, False)
                    return live(st)

                alive = lax.cond(((rem & p) != 0) & (alive > 0), last_steps, lambda alive=alive: alive)

            q0 = pl.multiple_of(i * ABLK, ABLK)
            o_ref[pl.ds(q0, ABLK), :] = acc_s[st].astype(BF16)
            rs_ref[pl.ds(q0, ABLK), :] = rs_s[st]

        def single(i):
            q2 = begin(0, i)
            n_top = jnp.minimum(i, ATT_TOP)
            for t in range(ATT_TOP + 1):
                @pl.when(n_top == t)
                def _():
                    step(0, q2, i - t, t + 1, True)

            finish(0, q2, i, n_top)

        def pair(i):
            qa, qb = begin(0, i), begin(1, i + 1)
            _interleave(stages(0, qa, i - ATT_TOP, ATT_TOP + 1, True), stages(1, qb, i + 1 - ATT_TOP, ATT_TOP + 1, True))
            finish(0, qa, i, ATT_TOP)
            finish(1, qb, i + 1, ATT_TOP)

        n_head = min(ATT_TOP, nb)
        n_pair = (nb - n_head) // 2
        n_single = nb - 2 * n_pair

        def singles(k, carry):
            single(jnp.where(k < n_head, k, nb - n_single + k))
            return carry

        def pairs(j, carry):
            pair(n_head + 2 * j)
            return carry

        lax.fori_loop(0, n_single, singles, 0)
        lax.fori_loop(0, n_pair, pairs, 0)
        rider.wait(pl.program_id(0) == n_pairs - 1)

    col = lambda o: (lambda p: (0, p + o))
    return pl.pallas_call(
        body, out_shape=[SDS((lp, n_pairs * 128), BF16), SDS((lp, n_pairs * 256), F32)] + rider.out_shape,
        grid=(n_pairs,),
        in_specs=[pl.BlockSpec((lp, 128), col(0)), pl.BlockSpec((lp, 128), col(n_pairs)),
                  pl.BlockSpec((lp, 128), col(2 * n_pairs)), pl.BlockSpec((256, 256), lambda p: (0, 0))]
        + rider.in_specs(),
        out_specs=[pl.BlockSpec((lp, 128), col(0)), pl.BlockSpec((lp, 256), col(0))] + rider.out_specs(),
        scratch_shapes=[pltpu.VMEM((2, 2, ABLK, 128), F32), pltpu.VMEM((2, ABLK, 128), F32),
                        pltpu.VMEM((2, ABLK, 256), F32)] + rider.scratch(),
        name=name, compiler_params=_cp())(qkv, qkv, qkv, tri_fwd, *rider.srcs)


def _attn_bwd(qkv, d_out, rsave, tri_after, tri_before, name, rider=None):
    lp = qkv.shape[0]
    n_pairs = (N_HEADS * HEAD_DIM) // 128
    nb = lp // ABLK
    scale = _attn_scale()
    nt = (((1,), (1,)), ((), ()))
    tn = (((0,), (0,)), ((), ()))
    rider = rider or _Rider()

    def body(q_ref, k_ref, v_ref, do_ref, rs_ref, ta_ref, tb_ref, *rest):
        (o_ref,), (dk_s, dv_s, dq_s, pc_s) = rider.split(rest, 1, 4)
        rider.start(pl.program_id(0) == 0)
        lane = lax.broadcasted_iota(jnp.int32, (ABLK, 128), 1)
        row = lax.broadcasted_iota(jnp.int32, (ABLK, 128), 0)
        in_a = lane < HEAD_DIM
        causal = lane < row
        dk_s[...] = jnp.zeros_like(dk_s)
        dv_s[...] = jnp.zeros_like(dv_s)

        def begin(st, i):
            q0 = pl.multiple_of(i * ABLK, ABLK)
            q2 = q_ref[pl.ds(q0, ABLK), :] * scale
            do2 = do_ref[pl.ds(q0, ABLK), :]
            dq_s[st] = jnp.zeros(dq_s.shape[1:], F32)
            pc_s[st] = jnp.zeros(pc_s.shape[1:], F32)
            return dict(q0=q0, q2=q2, do2=do2, q_st=_head_halves(q2, in_a), do_st=_head_halves(do2, in_a))

        def step(*args):
            _interleave(stages(*args))

        def stages(st, blk, kb0, nblk, diag):
            q0, q2, do2 = blk["q0"], blk["q2"], blk["do2"]
            k0 = pl.multiple_of(kb0 * ABLK, ABLK)
            kbd = _stack_blocks(k_ref[pl.ds(k0, nblk * ABLK), :], nblk, in_a)
            vbd = _stack_blocks(v_ref[pl.ds(k0, nblk * ABLK), :], nblk, in_a)
            z = lax.dot_general(q2, kbd, nt, preferred_element_type=F32)
            dw = lax.dot_general(do2, vbd, nt, preferred_element_type=F32)
            ncol = 2 * nblk
            zt = [z[:, c * 128:(c + 1) * 128] for c in range(ncol)]
            if diag:
                zt = [jnp.where(causal, t, MASKED) if c >= ncol - 2 else t for c, t in enumerate(zt)]
            bounds = [ncol * j // ATT_SPLIT for j in range(ATT_SPLIT + 1)]
            batches = [range(c0, c1) for c0, c1 in zip(bounds[:-1], bounds[1:]) if c0 < c1]
            sps, ex, ws, dls, pe = [None] * ncol, [None] * ncol, [None] * ncol, [None] * ncol, [None] * ncol

            def mass(cols):
                for c in cols:
                    sps[c] = _softplus(zt[c])
                got = jnp.dot(jnp.concatenate([_split_hi_lo(sps[c]) for c in cols], axis=0), ta_ref[...],
                              preferred_element_type=F32)
                for j, c in enumerate(cols):
                    ex[c] = got[j * 128:(j + 1) * 128]

            def weights(cols):
                for c in cols:
                    u, hh = c // 2, c % 2
                    r_saved = jnp.sum(jnp.where(lane == kb0 + u, rs_ref[pl.ds(q0, ABLK), hh * 128:(hh + 1) * 128],
                                                0.0), axis=1, keepdims=True)
                    w = jnp.exp(zt[c] - ex[c] - r_saved)
                    ws[c] = w.astype(BF16)
                    dls[c] = dw[:, c * 128:(c + 1) * 128] * w
                got = jnp.dot(jnp.concatenate([_split_hi_lo(dls[c]) for c in cols], axis=0), tb_ref[...],
                              preferred_element_type=F32)
                for j, c in enumerate(cols):
                    pe[c] = got[j * 128:(j + 1) * 128]

            yield
            mass(batches[0])
            yield
            for j in range(len(batches)):
                if j + 1 < len(batches):
                    mass(batches[j + 1])
                    yield
                weights(batches[j])
                yield
            pc = [pc_s[st, 0], pc_s[st, 1]]
            dzs = []
            for c in range(ncol):
                hh = c % 2
                one_minus_beta = jnp.exp(-sps[c])
                dz = dls[c] * one_minus_beta - (pe[c][:, :128] + pc[hh]) * (1.0 - one_minus_beta)
                pc[hh] = pc[hh] + pe[c][:, 128:]
                dzs.append(dz.astype(BF16))
            pc_s[st, 0] = pc[0]
            pc_s[st, 1] = pc[1]
            dq_s[st] += jnp.dot(jnp.concatenate(dzs, axis=1), kbd, preferred_element_type=F32)
            by_head = lambda ts: jnp.concatenate([jnp.concatenate(ts[0::2], axis=1), jnp.concatenate(ts[1::2], axis=1)],
                                                 axis=0)
            rows = pl.ds(k0, nblk * ABLK)
            dk_s[rows, :] += lax.dot_general(by_head(dzs), blk["q_st"], tn, preferred_element_type=F32)
            dv_s[rows, :] += lax.dot_general(by_head(ws), blk["do_st"], tn, preferred_element_type=F32)

        def below(st, blk, i, n_top):
            gone = jnp.min(rs_ref[pl.ds(blk["q0"], ABLK), :], axis=0, keepdims=True) >= EXP_ZERO_AT
            lane1 = lax.broadcasted_iota(jnp.int32, (1, 128), 1)
            first = jnp.sum(jnp.where(gone[:, :128] & gone[:, 128:] & (lane1 < i), 1.0, 0.0)).astype(jnp.int32)
            i_low = i - n_top
            n_grp = i_low // ATT_GROUP
            rem = i_low - ATT_GROUP * n_grp
            for p in reversed(_pow2_below(ATT_GROUP)):
                @pl.when(((rem & p) != 0) & ((rem & (p - 1)) + p > first))
                def _():
                    step(st, blk, rem & (p - 1), p, False)

            def inner(g, c2):
                step(st, blk, rem + ATT_GROUP * g, ATT_GROUP, False)
                return c2

            lax.fori_loop(jnp.maximum(first - rem, 0) // ATT_GROUP, n_grp, inner, 0)

        def done(st, blk):
            o_ref[0, pl.ds(blk["q0"], ABLK), :] = (dq_s[st] * scale).astype(BF16)

        def single(i):
            blk = begin(0, i)
            n_top = jnp.minimum(i, ATT_TOP)
            below(0, blk, i, n_top)
            for t in range(ATT_TOP + 1):
                @pl.when(n_top == t)
                def _():
                    step(0, blk, i - t, t + 1, True)

            done(0, blk)

        def pair(i):
            a, b = begin(0, i), begin(1, i + 1)
            below(0, a, i, ATT_TOP)
            below(1, b, i + 1, ATT_TOP)
            _interleave(stages(0, a, i - ATT_TOP, ATT_TOP + 1, True), stages(1, b, i + 1 - ATT_TOP, ATT_TOP + 1, True))
            done(0, a)
            done(1, b)

        n_head = min(ATT_TOP, nb)
        n_pair = (nb - n_head) // 2
        n_single = nb - 2 * n_pair

        def singles(k, carry):
            single(jnp.where(k < n_head, k, nb - n_single + k))
            return carry

        def pairs(j, carry):
            pair(n_head + 2 * j)
            return carry

        lax.fori_loop(0, n_single, singles, 0)
        lax.fori_loop(0, n_pair, pairs, 0)
        o_ref[1] = dk_s[...].astype(BF16)
        o_ref[2] = dv_s[...].astype(BF16)
        rider.wait(pl.program_id(0) == n_pairs - 1)

    col = lambda o: (lambda p: (0, p + o))
    return pl.pallas_call(
        body, out_shape=[SDS((3, lp, n_pairs * 128), BF16)] + rider.out_shape, grid=(n_pairs,),
        in_specs=[pl.BlockSpec((lp, 128), col(0)), pl.BlockSpec((lp, 128), col(n_pairs)),
                  pl.BlockSpec((lp, 128), col(2 * n_pairs)), pl.BlockSpec((lp, 128), col(0)),
                  pl.BlockSpec((lp, 256), col(0)),
                  pl.BlockSpec((256, 128), lambda p: (0, 0)), pl.BlockSpec((256, 256), lambda p: (0, 0))]
        + rider.in_specs(),
        out_specs=[pl.BlockSpec((3, lp, 128), lambda p: (0, 0, p))] + rider.out_specs(),
        scratch_shapes=[pltpu.VMEM((lp, 128), F32), pltpu.VMEM((lp, 128), F32),
                        pltpu.VMEM((2, ABLK, 128), F32), pltpu.VMEM((2, 2, ABLK, 128), F32)] + rider.scratch(),
        name=name, compiler_params=_cp())(qkv, qkv, qkv, d_out, rsave, tri_after, tri_before, *rider.srcs)


def _conv_fwd_dw(cacg, w, b, name, rider=None):
    lp = cacg.shape[0]
    c = cacg.shape[1] // 2
    ncb = c // 128
    nchunk = lp // ABLK
    off = CONV_PAD - (CONV_K - 1)
    rider = rider or _Rider()

    def body(a_ref, g_ref, w_ref, b_ref, *rest):
        (y_ref,), (upad,) = rider.split(rest, 1, 1)
        rider.start(pl.program_id(0) == 0)
        upad[0:CONV_PAD, :] = jnp.zeros((CONV_PAD, 128), F32)

        def fill(ch, carry):
            base = pl.multiple_of(ch * ABLK, ABLK)
            upad[pl.ds(base + CONV_PAD, ABLK), :] = a_ref[pl.ds(base, ABLK), :] * _sigmoid(g_ref[pl.ds(base, ABLK), :])
            return carry

        lax.fori_loop(0, nchunk, fill, 0)

        def comp(ch, carry):
            base = pl.multiple_of(ch * ABLK, ABLK)
            acc = jnp.zeros((ABLK, 128), F32)
            for k in range(CONV_K):
                acc = acc + upad[pl.ds(base + (off + k), ABLK), :] * w_ref[k:k + 1, :]
            y_ref[pl.ds(base, ABLK), :] = acc + b_ref[...]
            return carry

        lax.fori_loop(0, nchunk, comp, 0)
        rider.wait(pl.program_id(0) == ncb - 1)

    return pl.pallas_call(
        body, out_shape=[SDS((lp, c), F32)] + rider.out_shape, grid=(ncb,),
        in_specs=[pl.BlockSpec((lp, 128), lambda j: (0, j)), pl.BlockSpec((lp, 128), lambda j: (0, j + ncb)),
                  pl.BlockSpec((CONV_PAD, 128), lambda j: (0, j)), pl.BlockSpec((1, 128), lambda j: (0, j))]
        + rider.in_specs(),
        out_specs=[pl.BlockSpec((lp, 128), lambda j: (0, j))] + rider.out_specs(),
        scratch_shapes=[pltpu.VMEM((lp + CONV_PAD, 128), F32)] + rider.scratch(), name=name,
        compiler_params=_cp())(cacg, cacg, w, b, *rider.srcs)


def _ln_parts(x, g, b):
    mu = jnp.mean(x, axis=-1, keepdims=True)
    xc = x - mu
    rstd = lax.rsqrt(jnp.mean(xc * xc, axis=-1, keepdims=True) + EPS)
    xh = xc * rstd
    return xh, rstd, xh * g + b


def _conv_fwd_ln(yc, g, b, name):
    lp, c = yc.shape
    tm = _tile(lp, EW_ROWS, 16)

    def body(y_ref, g_ref, b_ref, o_ref):
        _, _, ln = _ln_parts(y_ref[...], g_ref[...], b_ref[...])
        o_ref[...] = (ln * _sigmoid(ln)).astype(BF16)

    return pl.pallas_call(
        body, out_shape=SDS((lp, c), BF16), grid=(lp // tm,),
        in_specs=[pl.BlockSpec((tm, c), lambda i: (i, 0)), pl.BlockSpec((1, c), lambda i: (0, 0)),
                  pl.BlockSpec((1, c), lambda i: (0, 0))],
        out_specs=pl.BlockSpec((tm, c), lambda i: (i, 0)), name=name, compiler_params=_cp())(yc, g, b)


def _conv_bwd_ln(yc, dout, g, b, name):
    lp, c = yc.shape
    tm = _tile(lp, EW_ROWS, 16)

    def body(y_ref, d_ref, g_ref, b_ref, o_ref, dg_ref, db_ref):
        @pl.when(pl.program_id(0) == 0)
        def _():
            dg_ref[...] = jnp.zeros_like(dg_ref)
            db_ref[...] = jnp.zeros_like(db_ref)

        xh, rstd, ln = _ln_parts(y_ref[...], g_ref[...], b_ref[...])
        s = _sigmoid(ln)
        dln = d_ref[...] * (s * (1.0 + ln * (1.0 - s)))
        dg_ref[...] += _sum8(dln * xh)
        db_ref[...] += _sum8(dln)
        dxh = dln * g_ref[...]
        o_ref[...] = rstd * (dxh - jnp.mean(dxh, axis=-1, keepdims=True)
                             - xh * jnp.mean(dxh * xh, axis=-1, keepdims=True))

    return pl.pallas_call(
        body, out_shape=[SDS((lp, c), F32), SDS((8, c), F32), SDS((8, c), F32)], grid=(lp // tm,),
        in_specs=[pl.BlockSpec((tm, c), lambda i: (i, 0)), pl.BlockSpec((tm, c), lambda i: (i, 0)),
                  pl.BlockSpec((1, c), lambda i: (0, 0)), pl.BlockSpec((1, c), lambda i: (0, 0))],
        out_specs=[pl.BlockSpec((tm, c), lambda i: (i, 0)), pl.BlockSpec((8, c), lambda i: (0, 0)),
                   pl.BlockSpec((8, c), lambda i: (0, 0))],
        name=name, compiler_params=_cp())(yc, dout, g, b)


def _conv_bwd_dw(dyc, cacg, w, name):
    lp, c = dyc.shape
    ncb = c // 128
    nchunk = lp // ABLK
    off = CONV_PAD - (CONV_K - 1)

    def body(dy_ref, a_ref, g_ref, w_ref, dcc_ref, dw_ref, db_ref, upad, dypad, dwacc):
        upad[0:CONV_PAD, :] = jnp.zeros((CONV_PAD, 128), F32)
        dypad[lp:lp + CONV_PAD, :] = jnp.zeros((CONV_PAD, 128), F32)
        dwacc[...] = jnp.zeros_like(dwacc)
        db_ref[...] = jnp.zeros_like(db_ref)

        def fill(ch, carry):
            base = pl.multiple_of(ch * ABLK, ABLK)
            upad[pl.ds(base + CONV_PAD, ABLK), :] = a_ref[pl.ds(base, ABLK), :] * _sigmoid(g_ref[pl.ds(base, ABLK), :])
            dypad[pl.ds(base, ABLK), :] = dy_ref[pl.ds(base, ABLK), :]
            return carry

        lax.fori_loop(0, nchunk, fill, 0)

        def comp(ch, carry):
            base = pl.multiple_of(ch * ABLK, ABLK)
            dy = dy_ref[pl.ds(base, ABLK), :]
            du = jnp.zeros((ABLK, 128), F32)
            for k in range(CONV_K):
                du = du + dypad[pl.ds(base + (CONV_K - 1 - k), ABLK), :] * w_ref[k:k + 1, :]
                dwacc[k * 8:(k + 1) * 8, :] += _sum8(dy * upad[pl.ds(base + (off + k), ABLK), :])
            db_ref[...] += _sum8(dy)
            a = a_ref[pl.ds(base, ABLK), :]
            s = _sigmoid(g_ref[pl.ds(base, ABLK), :])
            dcc_ref[0, pl.ds(base, ABLK), :] = (du * s).astype(BF16)
            dcc_ref[1, pl.ds(base, ABLK), :] = (du * a * (s * (1.0 - s))).astype(BF16)
            return carry

        lax.fori_loop(0, nchunk, comp, 0)
        dw_ref[...] = dwacc[...].reshape(CONV_PAD, 8, 128).sum(axis=1)

    return pl.pallas_call(
        body, out_shape=[SDS((2, lp, c), BF16), SDS((CONV_PAD, c), F32), SDS((8, c), F32)],
        grid=(ncb,),
        in_specs=[pl.BlockSpec((lp, 128), lambda j: (0, j)), pl.BlockSpec((lp, 128), lambda j: (0, j)),
                  pl.BlockSpec((lp, 128), lambda j: (0, j + ncb)), pl.BlockSpec((CONV_PAD, 128), lambda j: (0, j))],
        out_specs=[pl.BlockSpec((2, lp, 128), lambda j: (0, 0, j)),
                   pl.BlockSpec((CONV_PAD, 128), lambda j: (0, j)), pl.BlockSpec((8, 128), lambda j: (0, j))],
        scratch_shapes=[pltpu.VMEM((lp + CONV_PAD, 128), F32), pltpu.VMEM((lp + CONV_PAD, 128), F32),
                        pltpu.VMEM((CONV_PAD * 8, 128), F32)],
        name=name, compiler_params=_cp())(dyc, cacg, cacg, w)


def _mesh_pos():
    x, y, c = lax.axis_index("x"), lax.axis_index("y"), lax.axis_index("c")
    return x, y, c


def _peer(pos, r):
    x, y, c = pos
    px = (1 - x) if (r >> 2) & 1 else x
    py = (1 - y) if (r >> 1) & 1 else y
    pc = (1 - c) if r & 1 else c
    return (px, py, pc), 4 * px + 2 * py + pc


SIBLING = 1
OTHER_CHIPS = (2, 4, 6)
VIA_SIBLING = (3, 5, 7)


class _Job:
    def __init__(self, src, dst, scatter, src_layer=None, dst_layer=None):
        self.src, self.dst, self.scatter, self.src_layer, self.dst_layer = src, dst, scatter, src_layer, dst_layer

    def src_view(self, ins, idx):
        v = ins[self.src] if self.src_layer is None else ins[self.src].at[self.src_layer]
        return v.at[idx] if self.scatter else v

    def dst_view(self, outs, slot):
        v = outs[self.dst] if self.dst_layer is None else outs[self.dst].at[self.dst_layer]
        return v.at[slot]


def _remote(job, j, r, src, dst, to, send, recv):
    return pltpu.make_async_remote_copy(src_ref=src, dst_ref=dst, send_sem=send.at[j, r - 1], recv_sem=recv.at[j, r - 1],
                                        device_id=to, device_id_type=pl.DeviceIdType.MESH)


def _exchange_start(jobs, ins, outs, send, recv, loc):
    pos = _mesh_pos()
    me = 4 * pos[0] + 2 * pos[1] + pos[2]
    for j, job in enumerate(jobs):
        pltpu.make_async_copy(job.src_view(ins, me), job.dst_view(outs, me), loc.at[j]).start()
        for r in (range(1, N_DEV) if job.scatter else (SIBLING,) + OTHER_CHIPS):
            peer, peer_idx = _peer(pos, r)
            _remote(job, j, r, job.src_view(ins, peer_idx), job.dst_view(outs, me), peer, send, recv).start()


def _exchange_forward(jobs, ins, outs, send, recv, loc):
    pos = _mesh_pos()
    sibling, _ = _peer(pos, SIBLING)
    for j, job in enumerate(jobs):
        if job.scatter:
            continue
        for r in OTHER_CHIPS:
            peer, peer_idx = _peer(pos, r)
            slot = job.dst_view(outs, peer_idx)
            _remote(job, j, r, job.src_view(ins, peer_idx), slot, peer, send, recv).wait_recv()
            _remote(job, j, r ^ SIBLING, slot, slot, sibling, send, recv).start()


def _exchange_wait(jobs, ins, outs, send, recv, loc):
    pos = _mesh_pos()
    me = 4 * pos[0] + 2 * pos[1] + pos[2]
    for j, job in enumerate(jobs):
        for r in range(1, N_DEV):
            peer, peer_idx = _peer(pos, r)
            cp = _remote(job, j, r, job.src_view(ins, peer_idx), job.dst_view(outs, peer_idx), peer, send, recv)
            if job.scatter or r not in OTHER_CHIPS:
                cp.wait_recv()
            cp.wait_send()
        pltpu.make_async_copy(job.src_view(ins, me), job.dst_view(outs, me), loc.at[j]).wait()


def _exchange_sems(n_jobs):
    return [pltpu.SemaphoreType.DMA((n_jobs, N_DEV - 1)), pltpu.SemaphoreType.DMA((n_jobs, N_DEV - 1)),
            pltpu.SemaphoreType.DMA((n_jobs,))]


def _exchange(jobs, arrs, out_shape, name):
    n_in, n_out = len(arrs), len(out_shape)
    any_spec = pl.BlockSpec(memory_space=pl.ANY)

    def body(*refs):
        ins, outs, sems = refs[:n_in], refs[n_in:n_in + n_out], refs[n_in + n_out:]
        _exchange_start(jobs, ins, outs, *sems)
        _exchange_forward(jobs, ins, outs, *sems)
        _exchange_wait(jobs, ins, outs, *sems)

    return pl.pallas_call(
        body, out_shape=out_shape, in_specs=[any_spec] * n_in, out_specs=[any_spec] * n_out,
        scratch_shapes=_exchange_sems(len(jobs)),
        name=name, compiler_params=pltpu.CompilerParams(has_side_effects=True))(*arrs)


def _all_reduce_small(placed, cuts, rows, w, name):
    n_in = len(placed)

    def body(*refs):
        in_refs, o_refs = refs[:n_in], refs[n_in:n_in + len(cuts)]
        buf, send, recv = refs[n_in + len(cuts):]
        pos = _mesh_pos()
        me = 4 * pos[0] + 2 * pos[1] + pos[2]
        buf[me] = jnp.zeros((rows, w), F32)
        for ref, (arr, row0, col0, is_partial) in zip(in_refs, placed):
            val = ref[...].sum(axis=0, keepdims=True) if is_partial else ref[...]
            buf[me, row0:row0 + val.shape[0], col0:col0 + val.shape[1]] = val
        for r in range(1, N_DEV):
            peer, _ = _peer(pos, r)
            pltpu.make_async_remote_copy(src_ref=buf.at[me], dst_ref=buf.at[me], send_sem=send.at[r - 1],
                                         recv_sem=recv.at[r - 1], device_id=peer,
                                         device_id_type=pl.DeviceIdType.MESH).start()
        for r in range(1, N_DEV):
            peer, peer_idx = _peer(pos, r)
            cp = pltpu.make_async_remote_copy(src_ref=buf.at[me], dst_ref=buf.at[peer_idx], send_sem=send.at[r - 1],
                                              recv_sem=recv.at[r - 1], device_id=peer,
                                              device_id_type=pl.DeviceIdType.MESH)
            cp.wait_recv()
            cp.wait_send()
        acc = buf[0]
        for dev in range(1, N_DEV):
            acc = acc + buf[dev]
        for o_ref, (_, pieces) in zip(o_refs, cuts):
            for index, row0, nrows, col0, ncols in pieces:
                o_ref[index] = acc[row0:row0 + nrows, col0:col0 + ncols].reshape(o_ref.at[index].shape)

    vmem = pl.BlockSpec(memory_space=pltpu.VMEM)
    return pl.pallas_call(
        body, out_shape=[SDS(shape, F32) for shape, _ in cuts], in_specs=[vmem] * n_in, out_specs=[vmem] * len(cuts),
        scratch_shapes=[pltpu.VMEM((N_DEV, rows, w), F32), pltpu.SemaphoreType.DMA((N_DEV - 1,)),
                        pltpu.SemaphoreType.DMA((N_DEV - 1,))],
        name=name, compiler_params=pltpu.CompilerParams(has_side_effects=True))(*[p[0] for p in placed])


def _adamw_math(w, g, m, v):
    m = ADAM_B1 * m + (1.0 - ADAM_B1) * g
    v = ADAM_B2 * v + (1.0 - ADAM_B2) * (g * g)
    m_hat = m / (1.0 - ADAM_B1 ** ADAM_STEP)
    v_hat = v / (1.0 - ADAM_B2 ** ADAM_STEP)
    delta = -ADAM_LR * (m_hat / (jnp.sqrt(v_hat) + ADAM_EPS) + ADAM_WD * w)
    return delta, m, v


def _adamw_shard(parts, w, m, v, name):
    depth = len(parts)
    _, rr, cc = parts[0].shape
    tr = _tile(rr, 256, 8)
    nt = rr // tr
    part_block, blk = (N_DEV, tr, cc), pl.BlockSpec((None, tr, cc), lambda l, i: (l, i, 0))

    def body(*refs):
        p_refs = refs[:depth]
        w_ref, m_ref, v_ref, g_out, d_out, m_out, v_out = refs[depth:]
        for li in range(depth):
            @pl.when(pl.program_id(0) == li)
            def _(p_ref=p_refs[li]):
                g = p_ref[0].astype(F32)
                for dev in range(1, N_DEV):
                    g = g + p_ref[dev].astype(F32)
                delta, mm, vv = _adamw_math(w_ref[...], g, m_ref[...], v_ref[...])
                g_out[...] = g
                d_out[...] = delta
                m_out[...] = mm
                v_out[...] = vv

    def part_spec(li):
        return pl.BlockSpec(part_block, lambda l, i: (0, jnp.where(l == li, i, jnp.where(l < li, 0, nt - 1)), 0))

    return pl.pallas_call(
        body, out_shape=[SDS(w.shape, F32)] * 4, grid=(depth, nt),
        in_specs=[part_spec(li) for li in range(depth)] + [blk, blk, blk],
        out_specs=[blk] * 4, name=name, compiler_params=_cp())(*parts, w, m, v)


def _adamw_small(gs, ws, ms, vs, name):
    n = len(gs)

    def body(*refs):
        g_refs, w_refs, m_refs, v_refs = (refs[k * n:(k + 1) * n] for k in range(4))
        outs = refs[4 * n:]
        for k in range(n):
            delta, mm, vv = _adamw_math(w_refs[k][...], g_refs[k][...], m_refs[k][...], v_refs[k][...])
            outs[k][...] = delta
            outs[n + k][...] = mm
            outs[2 * n + k][...] = vv

    res = pl.pallas_call(body, out_shape=[SDS(w.shape, F32) for w in ws] * 3, name=name,
                         compiler_params=_cp())(*gs, *ws, *ms, *vs)
    return res[:n], res[n:2 * n], res[2 * n:]


def _from_cols(t):
    return jnp.transpose(t, (1, 0, 2)).reshape(t.shape[1], N_DEV * t.shape[2])


def _swap(t):
    return jnp.swapaxes(t, -1, -2)


def kernel(x, meta_tokens, mix_norm_g, w_in, conv_dw_w, conv_dw_b, conv_ln_g, conv_ln_b, w_out, ffn_norm_g, w_gate, w_up, w_down, final_norm_g, loss_target, m_meta_tokens, m_mix_norm_g, m_w_in, m_conv_dw_w, m_conv_dw_b, m_conv_ln_g, m_conv_ln_b, m_w_out, m_ffn_norm_g, m_w_gate, m_w_up, m_w_down, m_final_norm_g, v_meta_tokens, v_mix_norm_g, v_w_in, v_conv_dw_w, v_conv_dw_b, v_conv_ln_g, v_conv_ln_b, v_w_out, v_ffn_norm_g, v_w_gate, v_w_up, v_w_down, v_final_norm_g):
    depth, d, in_shard = w_in.shape
    seq = x.shape[1]
    sb = N_HEADS * HEAD_DIM
    cc = conv_dw_w.shape[2] * N_DEV
    ff = w_gate.shape[2] * N_DEV
    assert in_shard * N_DEV == 3 * sb + 2 * cc and x.shape[0] == 1
    lr = N_META + seq
    lp = -(-lr // ABLK) * ABLK
    me = 4 * lax.axis_index("x") + 2 * lax.axis_index("y") + lax.axis_index("c")

    big_names = ("w_in", "w_out", "w_gate", "w_up", "w_down")
    transposed = {"w_in": True, "w_out": False, "w_gate": True, "w_up": True, "w_down": False}
    shard = dict(w_in=_swap(w_in).astype(BF16), w_out=w_out.astype(BF16), w_gate=_swap(w_gate).astype(BF16),
                 w_up=_swap(w_up).astype(BF16), w_down=w_down.astype(BF16))

    def gather_of(keys):
        names = sorted({n for n, _ in keys}, key=big_names.index)
        jobs = [_Job(names.index(n), j, False, src_layer=i) for j, (n, i) in enumerate(keys)]
        return jobs, [shard[n] for n in names], [SDS((N_DEV,) + shard[n].shape[1:], BF16) for n, _ in keys]

    first_keys = [("w_in", 0)]

    def riding(keys):
        return _Rider(*gather_of(keys)) if keys else None

    def receive(keys, arrays):
        for (n, li), t in zip(keys, arrays):
            wl[li][n] = t.reshape(-1, d)

    jobs, srcs, out_shape = gather_of(first_keys)
    for extra in (meta_tokens, conv_dw_w):
        jobs.append(_Job(len(srcs), len(out_shape), False))
        srcs.append(extra)
        out_shape.append(SDS((N_DEV,) + extra.shape, F32))
    gathered = _exchange(jobs, srcs, out_shape, "gather_first")
    wl = [dict() for _ in range(depth)]
    receive(first_keys, gathered)
    meta_full = _from_cols(gathered[-2])
    taps = jnp.transpose(gathered[-1], (1, 2, 0, 3)).reshape(depth, CONV_K, cc)
    taps = jnp.pad(taps, ((0, 0), (0, CONV_PAD - CONV_K), (0, 0)))
    tri_fwd, tri_after, tri_before = _tri_consts()

    h = jnp.concatenate([meta_full, x[0], jnp.zeros((lp - lr, d), F32)], axis=0)
    saved = []
    for i in range(depth):
        p = wl[i]
        sv = dict(h_in=h)
        keys = [("w_up", 0)] if i == 0 else []
        qkv, hn, *arrived = _rms_mm(h, mix_norm_g[i:i + 1], p["w_in"], 0, 3 * sb, BF16, f"proj_qkv_{i}", True,
                                    riding(keys))
        receive(keys, arrived)
        keys = [("w_out", 0)] if i == 0 else []
        cacg, *arrived = _rms_mm(h, mix_norm_g[i:i + 1], p["w_in"], 3 * sb, 2 * cc, F32, f"proj_conv_{i}", False,
                                 riding(keys))
        receive(keys, arrived)
        keys = [("w_gate", i)]
        attn, rsave, *arrived = _attn_fwd(qkv, tri_fwd, f"attn_fwd_{i}", riding(keys))
        receive(keys, arrived)
        keys = [("w_down", 0)] if i == 0 else []
        yc, *arrived = _conv_fwd_dw(cacg, taps[i], conv_dw_b[i:i + 1], f"conv_fwd_dw_{i}", riding(keys))
        receive(keys, arrived)
        conv = _conv_fwd_ln(yc, conv_ln_g[i:i + 1], conv_ln_b[i:i + 1], f"conv_fwd_ln_{i}")
        h = _mix_out(attn, conv, p["w_out"], h, f"mix_out_{i}")
        sv.update(qkv=qkv, hn=hn, cacg=cacg, rsave=rsave, yc=yc, attn=attn, conv=conv, h_mid=h)
        nxt = i + 1
        keys = [("w_in", nxt), ("w_out", nxt), ("w_down", nxt), ("w_up", nxt)] if nxt < depth else []
        h, hn2, act, gate, up, *arrived = _ffn_fwd(h, ffn_norm_g[i:i + 1], p["w_gate"], p["w_up"], p["w_down"],
                                                   f"ffn_fwd_{i}", riding(keys))
        receive(keys, arrived)
        sv.update(hn2=hn2, act=act, gate=gate, up=up)
        saved.append(sv)

    tpad = jnp.pad(loss_target[0], ((N_META, lp - lr), (0, 0)))
    dh, dg_final, loss_part = _loss_head(h, tpad, final_norm_g.reshape(1, d), seq, "loss_head")
    loss = lax.psum(loss_part[0, 0], MESH_AXES)

    parts = {}

    def sending(items):
        srcs = [t.reshape(N_DEV, t.shape[0] // N_DEV, d) for _, t in items]
        return _Rider([_Job(j, j, True) for j in range(len(items))], srcs, [SDS(t.shape, BF16) for t in srcs])

    def arrive(items, arrays):
        parts.update({key: t for (key, _), t in zip(items, arrays)})

    grads = [None] * depth
    from_above = []
    for i in reversed(range(depth)):
        p, sv = wl[i], saved[i]
        d_g, d_u, *got = _ffn_bwd_act(dh, sv["gate"], sv["up"], p["w_down"], f"ffn_bwd_act_{i}", sending(from_above))
        arrive(from_above, got)
        gw_down = _mm_tn(sv["act"], dh, f"grad_w_down_{i}")
        gw_gate = _mm_tn(d_g, sv["hn2"], f"grad_w_gate_{i}")
        gw_up = _mm_tn(d_u, sv["hn2"], f"grad_w_up_{i}")
        items = [(("w_down", i), gw_down)]
        dh, dg_ffn, *got = _ffn_bwd_in(d_g, d_u, p["w_gate"], p["w_up"], sv["h_mid"], ffn_norm_g[i:i + 1], dh,
                                       f"ffn_bwd_in_{i}", sending(items))
        arrive(items, got)
        gw_out = jnp.concatenate([_mm_tn(sv["attn"], dh, f"grad_w_out_attn_{i}"),
                                  _mm_tn(sv["conv"], dh, f"grad_w_out_conv_{i}")], axis=0)
        d_attn = _mm_nt(dh, p["w_out"], 0, sb, BF16, f"mix_bwd_attn_{i}")
        d_conv = _mm_nt(dh, p["w_out"], sb, cc, F32, f"mix_bwd_conv_{i}")
        items = [(("w_gate", i), gw_gate), (("w_up", i), gw_up)]
        dqkv, *got = _attn_bwd(sv["qkv"], d_attn, sv["rsave"], tri_after, tri_before, f"attn_bwd_{i}", sending(items))
        arrive(items, got)
        dyc, dg_ln, db_ln = _conv_bwd_ln(sv["yc"], d_conv, conv_ln_g[i:i + 1], conv_ln_b[i:i + 1], f"conv_bwd_ln_{i}")
        dcc, g_taps_i, db_conv = _conv_bwd_dw(dyc, sv["cacg"], taps[i], f"conv_bwd_dw_{i}")
        items = [(("w_out", i), gw_out)]
        gw_in, *got = _grad_w_in_t(dqkv, dcc, sv["hn"], f"grad_w_in_{i}", sending(items))
        arrive(items, got)
        from_above = [(("w_in", i), gw_in)]
        items = from_above if i == 0 else []
        dh, dg_mix, *got = _mix_bwd_in(dqkv, dcc, p["w_in"], sv["h_in"], mix_norm_g[i:i + 1], dh, f"mix_bwd_in_{i}",
                                       sending(items))
        arrive(items, got)
        grads[i] = dict(taps=g_taps_i, dg_mix=dg_mix, dg_ffn=dg_ffn, dg_ln=dg_ln, db_ln=db_ln, db_conv=db_conv)
    grad_x = dh[N_META:lr][None]

    per = d // depth
    assert depth * cc <= d and d % depth == 0 and per % 128 == 0
    placed, cuts, row = [], [], 0
    for key in ("dg_mix", "dg_ffn"):
        placed += [(grads[i][key], row + i, 0, True) for i in range(depth)]
        cuts.append(((depth, d), [(slice(0, depth), row, depth, 0, d)]))
        row += depth
    placed.append((dg_final, row, 0, True))
    cuts.append(((1, d), [(slice(0, 1), row, 1, 0, d)]))
    row += 1
    for key in ("db_conv", "dg_ln", "db_ln"):
        placed += [(grads[i][key], row, i * per, True) for i in range(depth)]
        cuts.append(((depth, cc), [(slice(i, i + 1), row, 1, i * per, cc) for i in range(depth)]))
        row += 1
    placed += [(grads[i]["taps"], row, i * per, False) for i in range(depth)]
    cuts.append(((depth, CONV_PAD, cc), [(i, row, CONV_PAD, i * per, cc) for i in range(depth)]))
    row += CONV_PAD
    placed.append((dh[:N_META], row, 0, False))
    cuts.append(((N_META, d), [(slice(0, N_META), row, N_META, 0, d)]))
    row += N_META
    g_mix, g_ffn, g_final, g_cb, g_lg, g_lb, g_taps_full, g_meta_full = _all_reduce_small(
        placed, cuts, -(-row // 8) * 8, d, "reduce_small")
    csh = cc // N_DEV
    g_taps_own = lax.dynamic_slice_in_dim(g_taps_full[:, :CONV_K], me * csh, csh, axis=2)
    msh = d // N_DEV
    g_meta_own = lax.dynamic_slice_in_dim(g_meta_full, me * msh, msh, axis=1)

    row1 = lambda t: t.reshape(1, d)
    small_g = [g_meta_own, g_mix, g_taps_own, g_cb, g_lg, g_lb, g_ffn, g_final]
    small_w = [meta_tokens, mix_norm_g, conv_dw_w, conv_dw_b, conv_ln_g, conv_ln_b, ffn_norm_g, row1(final_norm_g)]
    small_m = [m_meta_tokens, m_mix_norm_g, m_conv_dw_w, m_conv_dw_b, m_conv_ln_g, m_conv_ln_b, m_ffn_norm_g,
               row1(m_final_norm_g)]
    small_v = [v_meta_tokens, v_mix_norm_g, v_conv_dw_w, v_conv_dw_b, v_conv_ln_g, v_conv_ln_b, v_ffn_norm_g,
               row1(v_final_norm_g)]
    s_delta, s_m, s_v = _adamw_small(small_g, small_w, small_m, small_v, "adamw_small")
    unrow = lambda ts: list(ts[:-1]) + [ts[-1].reshape(d)]
    small_g, s_delta, s_m, s_v = unrow(small_g), unrow(s_delta), unrow(s_m), unrow(s_v)

    big = []
    for n, w, m, v in zip(big_names, (w_in, w_out, w_gate, w_up, w_down), (m_w_in, m_w_out, m_w_gate, m_w_up, m_w_down),
                          (v_w_in, v_w_out, v_w_gate, v_w_up, v_w_down)):
        fix = _swap if transposed[n] else (lambda t: t)
        res = _adamw_shard([parts[(n, i)] for i in range(depth)], fix(w), fix(m), fix(v), f"adamw_{n}")
        big.append([fix(t) for t in res])
    b_in, b_out, b_gate, b_up, b_down = big

    def ordered(k, smalls):
        s_meta, s_mix, s_taps, s_cb, s_lg, s_lb, s_ffn, s_final = smalls
        return [s_meta, s_mix, b_in[k], s_taps, s_cb, s_lg, s_lb, b_out[k], s_ffn, b_gate[k], b_up[k], b_down[k], s_final]

    return (loss, grad_x, *ordered(0, small_g), *ordered(1, s_delta), *ordered(2, s_m), *ordered(3, s_v))
```

```python
import math

import jax
import jax.numpy as jnp
from jax import lax
from jax.experimental import pallas as pl
from jax.experimental.pallas import tpu as pltpu

F32 = jnp.float32
BF16 = jnp.bfloat16
SDS = jax.ShapeDtypeStruct

N_META = 16
N_HEADS = 8
HEAD_DIM = 64
CONV_K = 31
CONV_PAD = 32
ABLK = 128
ATT_GROUP = 4
ATT_SPLIT = 2
ATT_TOP = 2
EXP_ZERO_AT = 104.0
GONE = 1e30
MASKED = -1e30
FF_CHUNK = 256
FF_BLOCK = 1408
FF_ROWS = 528
EPS = 1e-6
N_DEV = 8
MESH_AXES = ("x", "y", "c")
ADAM_LR = 0.001
ADAM_B1 = 0.9
ADAM_B2 = 0.999
ADAM_EPS = 1e-08
ADAM_WD = 0.01
ADAM_STEP = 10
MIB = 1 << 20
VMEM_LIMIT_MIB = 48
FFN_FWD_VMEM_MIB = 56
MM_ROWS = 1056
EW_ROWS = 528


def _cp(mib=None):
    return pltpu.CompilerParams(vmem_limit_bytes=(mib or VMEM_LIMIT_MIB) * MIB)


def _tile(n, cap, mult):
    best = None
    for t in range(mult, min(n, cap) + 1, mult):
        if n % t == 0:
            best = t
    assert best is not None, (n, cap, mult)
    return best


def _sum8(v):
    r, c = v.shape
    return v.reshape(r // 8, 8, c).sum(axis=0)


def _sigmoid(x):
    return 1.0 / (1.0 + jnp.exp(-x))


def _ff_block(f):
    fb = _tile(f, FF_BLOCK, 128)
    return fb, [(s0, min(FF_CHUNK, fb - s0)) for s0 in range(0, fb, FF_CHUNK)]


NT = (((1,), (1,)), ((), ()))
TN = (((0,), (0,)), ((), ()))


def _rms_mm(h, g, wt, w_row0, n_cols, out_dtype, name, hn_out, rider=None):
    lp, d = h.shape
    tm = _tile(lp, MM_ROWS, 16)
    tn = _tile(math.gcd(n_cols, w_row0), 768, 128)
    off = w_row0 // tn
    ni, nj = lp // tm, n_cols // tn
    rider = rider or _Rider()

    def body(h_ref, g_ref, w_ref, *rest):
        outs, (hn_s,) = rider.split(rest, 2 if hn_out else 1, 1)
        o_ref = outs[0]
        hn_ref = outs[1] if hn_out else None
        rider.start((pl.program_id(0) == 0) & (pl.program_id(1) == 0))
        if ni > 1:
            rider.forward((pl.program_id(0) == ni - 1) & (pl.program_id(1) == 0))

        @pl.when(pl.program_id(1) == 0)
        def _():
            x = h_ref[...]
            r = lax.rsqrt(jnp.mean(x * x, axis=-1, keepdims=True) + EPS)
            hn = ((x * r) * g_ref[...]).astype(BF16)
            hn_s[...] = hn
            if hn_out:
                hn_ref[...] = hn

        o_ref[...] = lax.dot_general(hn_s[...], w_ref[...], NT, preferred_element_type=F32).astype(out_dtype)
        rider.wait((pl.program_id(0) == ni - 1) & (pl.program_id(1) == nj - 1))

    out_shape = [SDS((lp, n_cols), out_dtype)]
    out_specs = [pl.BlockSpec((tm, tn), lambda i, j: (i, j))]
    if hn_out:
        out_shape.append(SDS((lp, d), BF16))
        out_specs.append(pl.BlockSpec((tm, d), lambda i, j: (i, 0)))
    return pl.pallas_call(
        body, out_shape=out_shape + rider.out_shape, grid=(ni, nj),
        in_specs=[pl.BlockSpec((tm, d), lambda i, j: (i, 0)),
                  pl.BlockSpec((1, d), lambda i, j: (0, 0)),
                  pl.BlockSpec((tn, d), lambda i, j: (j + off, 0))] + rider.in_specs(),
        out_specs=out_specs + rider.out_specs(), scratch_shapes=[pltpu.VMEM((tm, d), BF16)] + rider.scratch(),
        name=name, compiler_params=_cp())(h, g, wt, *rider.srcs)


def _mm_nt(a, b, b_row0, n_out, out_dtype, name):
    m, k = a.shape
    tm = _tile(m, MM_ROWS, 16)
    tn = _tile(math.gcd(n_out, b_row0), 512, 128)
    off = b_row0 // tn

    def body(a_ref, b_ref, o_ref):
        o_ref[...] = lax.dot_general(a_ref[...].astype(BF16), b_ref[...], NT,
                                     preferred_element_type=F32).astype(out_dtype)

    return pl.pallas_call(
        body, out_shape=SDS((m, n_out), out_dtype), grid=(m // tm, n_out // tn),
        in_specs=[pl.BlockSpec((tm, k), lambda i, j: (i, 0)), pl.BlockSpec((tn, k), lambda i, j: (j + off, 0))],
        out_specs=pl.BlockSpec((tm, tn), lambda i, j: (i, j)), name=name, compiler_params=_cp())(a, b)


def _mix_out(attn, conv, w, res, name):
    m, ka = attn.shape
    kc = conv.shape[1]
    n = w.shape[1]
    assert ka == kc
    tm = _tile(m, MM_ROWS, 16)
    tn = _tile(n, 512, 128)

    def body(a_ref, c_ref, wa_ref, wc_ref, r_ref, o_ref):
        o_ref[...] = (r_ref[...] + jnp.dot(a_ref[...], wa_ref[...], preferred_element_type=F32)
                      + jnp.dot(c_ref[...], wc_ref[...], preferred_element_type=F32))

    return pl.pallas_call(
        body, out_shape=SDS((m, n), F32), grid=(m // tm, n // tn),
        in_specs=[pl.BlockSpec((tm, ka), lambda i, j: (i, 0)), pl.BlockSpec((tm, kc), lambda i, j: (i, 0)),
                  pl.BlockSpec((ka, tn), lambda i, j: (0, j)), pl.BlockSpec((kc, tn), lambda i, j: (1, j)),
                  pl.BlockSpec((tm, tn), lambda i, j: (i, j))],
        out_specs=pl.BlockSpec((tm, tn), lambda i, j: (i, j)), name=name, compiler_params=_cp())(attn, conv, w, w, res)


def _mm_tn(a, b, name):
    l, m = a.shape
    n = b.shape[1]
    tm = _tile(m, 1408, 128)
    tn = _tile(n, 1024, 128)
    tl = _tile(l, 1408, 128)
    nl = l // tl

    def body(a_ref, b_ref, o_ref, acc):
        @pl.when(pl.program_id(2) == 0)
        def _():
            acc[...] = jnp.zeros_like(acc)

        acc[...] += lax.dot_general(a_ref[...].astype(BF16), b_ref[...].astype(BF16), TN, preferred_element_type=F32)

        @pl.when(pl.program_id(2) == nl - 1)
        def _():
            o_ref[...] = acc[...].astype(BF16)

    return pl.pallas_call(
        body, out_shape=SDS((m, n), BF16), grid=(m // tm, n // tn, nl),
        in_specs=[pl.BlockSpec((tl, tm), lambda i, j, s: (s, i)),
                  pl.BlockSpec((tl, tn), lambda i, j, s: (s, j))],
        out_specs=pl.BlockSpec((tm, tn), lambda i, j, s: (i, j)),
        scratch_shapes=[pltpu.VMEM((tm, tn), F32)], name=name, compiler_params=_cp())(a, b)


def _grad_w_in_t(dqkv, dcc, hn, name, rider=None):
    nq, l, w = dqkv.shape
    nc = dcc.shape[0]
    d = hn.shape[1]
    assert dcc.shape[2] == w
    tn = _tile(d, 1024, 128)
    tl = _tile(l, 1408, 128)
    nl = l // tl
    nj = d // tn
    rider = rider or _Rider()

    def body(q_ref, c_ref, b_ref, *rest):
        (o_ref,), (acc,) = rider.split(rest, 1, 1)
        p, jj, s = pl.program_id(0), pl.program_id(1), pl.program_id(2)
        rider.start((p == 0) & (jj == 0) & (s == 0))

        @pl.when(s == 0)
        def _():
            acc[...] = jnp.zeros_like(acc)

        @pl.when(p < nq)
        def _():
            acc[...] += lax.dot_general(q_ref[...], b_ref[...], TN, preferred_element_type=F32)

        @pl.when(p >= nq)
        def _():
            acc[...] += lax.dot_general(c_ref[...], b_ref[...], TN, preferred_element_type=F32)

        @pl.when(s == nl - 1)
        def _():
            o_ref[...] = acc[...].astype(BF16)

        rider.wait((p == nq + nc - 1) & (jj == nj - 1) & (s == nl - 1))

    out, *arrived = pl.pallas_call(
        body, out_shape=[SDS((nq + nc, w, d), BF16)] + rider.out_shape, grid=(nq + nc, nj, nl),
        in_specs=[pl.BlockSpec((None, tl, w), lambda p, j, s: (jnp.minimum(p, nq - 1), s, 0)),
                  pl.BlockSpec((None, tl, w), lambda p, j, s: (jnp.maximum(p - nq, 0), s, 0)),
                  pl.BlockSpec((tl, tn), lambda p, j, s: (s, j))] + rider.in_specs(),
        out_specs=[pl.BlockSpec((None, w, tn), lambda p, j, s: (p, 0, j))] + rider.out_specs(),
        scratch_shapes=[pltpu.VMEM((w, tn), F32)] + rider.scratch(), name=name,
        compiler_params=_cp())(dqkv, dcc, hn, *rider.srcs)
    return [out.reshape((nq + nc) * w, d)] + arrived


def _rms_bwd_tail(acc, h_ref, g_ref, dres_ref, o_ref, dg_ref):
    x = h_ref[...]
    r = lax.rsqrt(jnp.mean(x * x, axis=-1, keepdims=True) + EPS)
    xr = x * r
    dyv = acc[...]
    gy = dyv * g_ref[...]
    o_ref[...] = dres_ref[...] + r * (gy - xr * jnp.mean(gy * xr, axis=-1, keepdims=True))
    dg_ref[...] += _sum8(dyv * xr)


def _mix_bwd_in(dqkv, dcc, wt, h, g, dres, name, rider=None):
    nq, lp, w = dqkv.shape
    nc = dcc.shape[0]
    d = h.shape[1]
    tm = _tile(lp, FF_ROWS, 16)
    ni = lp // tm
    rider = rider or _Rider()

    def body(q_ref, c_ref, w_ref, h_ref, g_ref, dres_ref, *rest):
        (o_ref, dg_ref), (acc,) = rider.split(rest, 2, 1)
        i = pl.program_id(0)
        rider.start(i == 0)

        @pl.when(i == 0)
        def _():
            dg_ref[...] = jnp.zeros_like(dg_ref)

        pieces = [q_ref[p] for p in range(nq)] + [c_ref[p] for p in range(nc)]
        total = None
        for p, piece in enumerate(pieces):
            part = jnp.dot(piece, w_ref[p * w:(p + 1) * w, :], preferred_element_type=F32)
            total = part if total is None else total + part
        acc[...] = total
        _rms_bwd_tail(acc, h_ref, g_ref, dres_ref, o_ref, dg_ref)
        rider.wait(i == ni - 1)

    return pl.pallas_call(
        body, out_shape=[SDS((lp, d), F32), SDS((8, d), F32)] + rider.out_shape, grid=(ni,),
        in_specs=[pl.BlockSpec((nq, tm, w), lambda i: (0, i, 0)),
                  pl.BlockSpec((nc, tm, w), lambda i: (0, i, 0)),
                  pl.BlockSpec(((nq + nc) * w, d), lambda i: (0, 0)),
                  pl.BlockSpec((tm, d), lambda i: (i, 0)),
                  pl.BlockSpec((1, d), lambda i: (0, 0)),
                  pl.BlockSpec((tm, d), lambda i: (i, 0))] + rider.in_specs(),
        out_specs=[pl.BlockSpec((tm, d), lambda i: (i, 0)), pl.BlockSpec((8, d), lambda i: (0, 0))]
        + rider.out_specs(),
        scratch_shapes=[pltpu.VMEM((tm, d), F32)] + rider.scratch(), name=name,
        compiler_params=_cp())(dqkv, dcc, wt, h, g, dres, *rider.srcs)


def _ffn_bwd_in(d_g, d_u, wg_t, wu_t, h, g, dres, name, rider=None):
    lp, f = d_g.shape
    d = h.shape[1]
    tm = _tile(lp, FF_ROWS, 16)
    tk = _tile(f, FF_BLOCK, 128)
    nk = f // tk
    ni = lp // tm
    rider = rider or _Rider()

    def body(dg_in, du_in, wg_ref, wu_ref, h_ref, g_ref, dres_ref, *rest):
        (o_ref, dg_ref), (acc,) = rider.split(rest, 2, 1)
        i, kk = pl.program_id(0), pl.program_id(1)
        rider.start((i == 0) & (kk == 0))

        @pl.when(kk == 0)
        def _():
            acc[...] = jnp.zeros_like(acc)

        @pl.when((kk == 0) & (i == 0))
        def _():
            dg_ref[...] = jnp.zeros_like(dg_ref)

        acc[...] += (jnp.dot(dg_in[...], wg_ref[...], preferred_element_type=F32)
                     + jnp.dot(du_in[...], wu_ref[...], preferred_element_type=F32))

        @pl.when(kk == nk - 1)
        def _():
            _rms_bwd_tail(acc, h_ref, g_ref, dres_ref, o_ref, dg_ref)

        rider.wait((i == ni - 1) & (kk == nk - 1))

    return pl.pallas_call(
        body, out_shape=[SDS((lp, d), F32), SDS((8, d), F32)] + rider.out_shape, grid=(ni, nk),
        in_specs=[pl.BlockSpec((tm, tk), lambda i, j: (i, j)), pl.BlockSpec((tm, tk), lambda i, j: (i, j)),
                  pl.BlockSpec((tk, d), lambda i, j: (j, 0)), pl.BlockSpec((tk, d), lambda i, j: (j, 0)),
                  pl.BlockSpec((tm, d), lambda i, j: (i, 0)),
                  pl.BlockSpec((1, d), lambda i, j: (0, 0)),
                  pl.BlockSpec((tm, d), lambda i, j: (i, 0))] + rider.in_specs(),
        out_specs=[pl.BlockSpec((tm, d), lambda i, j: (i, 0)), pl.BlockSpec((8, d), lambda i, j: (0, 0))]
        + rider.out_specs(),
        scratch_shapes=[pltpu.VMEM((tm, d), F32)] + rider.scratch(), name=name,
        compiler_params=_cp())(d_g, d_u, wg_t, wu_t, h, g, dres, *rider.srcs)


def _ffn_fwd(h, g, wg_t, wu_t, w_d, name, rider=None):
    lp, d = h.shape
    f = w_d.shape[0]
    tm = _tile(lp, FF_ROWS, 16)
    fb, subs = _ff_block(f)
    nc = f // fb
    ni = lp // tm
    rider = rider or _Rider()

    def body(h_ref, g_ref, wg_ref, wu_ref, wd_ref, *rest):
        (o_ref, hn_ref, act_ref, gate_ref, up_ref), (hn_s, acc) = rider.split(rest, 5, 2)
        c = pl.program_id(1)
        rider.start((pl.program_id(0) == 0) & (c == 0))
        if ni > 1:
            rider.forward((pl.program_id(0) == ni - 1) & (c == 0))

        @pl.when(c == 0)
        def _():
            x = h_ref[...]
            r = lax.rsqrt(jnp.mean(x * x, axis=-1, keepdims=True) + EPS)
            hn = ((x * r) * g_ref[...]).astype(BF16)
            hn_s[...] = hn
            hn_ref[...] = hn
            acc[...] = jnp.zeros_like(acc)

        total = None
        for s0, sw in subs:
            gg = lax.dot_general(hn_s[...], wg_ref[s0:s0 + sw, :], NT, preferred_element_type=F32)
            uu = lax.dot_general(hn_s[...], wu_ref[s0:s0 + sw, :], NT, preferred_element_type=F32)
            gate_ref[:, s0:s0 + sw] = gg
            up_ref[:, s0:s0 + sw] = uu
            act = ((gg * _sigmoid(gg)) * uu).astype(BF16)
            act_ref[:, s0:s0 + sw] = act
            part = jnp.dot(act, wd_ref[s0:s0 + sw, :], preferred_element_type=F32)
            total = part if total is None else total + part
        acc[...] += total

        @pl.when(c == nc - 1)
        def _():
            o_ref[...] = h_ref[...] + acc[...]

        rider.wait((pl.program_id(0) == ni - 1) & (c == nc - 1))

    chunk = pl.BlockSpec((fb, d), lambda i, j: (j, 0))
    wide = pl.BlockSpec((tm, fb), lambda i, j: (i, j))
    return pl.pallas_call(
        body, out_shape=[SDS((lp, d), F32), SDS((lp, d), BF16), SDS((lp, f), BF16), SDS((lp, f), F32),
                         SDS((lp, f), F32)] + rider.out_shape, grid=(ni, nc),
        in_specs=[pl.BlockSpec((tm, d), lambda i, j: (i, 0)), pl.BlockSpec((1, d), lambda i, j: (0, 0)),
                  chunk, chunk, chunk] + rider.in_specs(),
        out_specs=[pl.BlockSpec((tm, d), lambda i, j: (i, 0)), pl.BlockSpec((tm, d), lambda i, j: (i, 0)),
                   wide, wide, wide] + rider.out_specs(),
        scratch_shapes=[pltpu.VMEM((tm, d), BF16), pltpu.VMEM((tm, d), F32)] + rider.scratch(),
        name=name, compiler_params=_cp(FFN_FWD_VMEM_MIB))(h, g, wg_t, wu_t, w_d, *rider.srcs)


def _ffn_bwd_act(dh, gate, up, w_d, name, rider=None):
    lp, d = dh.shape
    f = w_d.shape[0]
    tm = _tile(lp, FF_ROWS, 16)
    fb, subs = _ff_block(f)
    nc = f // fb
    ni = lp // tm
    rider = rider or _Rider()

    def body(dh_ref, gate_ref, up_ref, wd_ref, *rest):
        (dg_out, du_out), (dh_s,) = rider.split(rest, 2, 1)
        rider.start((pl.program_id(0) == 0) & (pl.program_id(1) == 0))

        @pl.when(pl.program_id(1) == 0)
        def _():
            dh_s[...] = dh_ref[...].astype(BF16)

        for s0, sw in subs:
            gg = gate_ref[:, s0:s0 + sw]
            uu = up_ref[:, s0:s0 + sw]
            dact = lax.dot_general(dh_s[...], wd_ref[s0:s0 + sw, :], NT, preferred_element_type=F32)
            s = _sigmoid(gg)
            dg_out[:, s0:s0 + sw] = (dact * uu * (s * (1.0 + gg * (1.0 - s)))).astype(BF16)
            du_out[:, s0:s0 + sw] = (dact * (gg * s)).astype(BF16)
        rider.wait((pl.program_id(0) == ni - 1) & (pl.program_id(1) == nc - 1))

    wide = pl.BlockSpec((tm, fb), lambda i, j: (i, j))
    return pl.pallas_call(
        body, out_shape=[SDS((lp, f), BF16), SDS((lp, f), BF16)] + rider.out_shape, grid=(ni, nc),
        in_specs=[pl.BlockSpec((tm, d), lambda i, j: (i, 0)), wide, wide,
                  pl.BlockSpec((fb, d), lambda i, j: (j, 0))] + rider.in_specs(),
        out_specs=[wide, wide] + rider.out_specs(), scratch_shapes=[pltpu.VMEM((tm, d), BF16)] + rider.scratch(),
        name=name, compiler_params=_cp())(dh, gate, up, w_d, *rider.srcs)


def _loss_head(h, tpad, g, n_real, name):
    lp, d = h.shape
    tm = _tile(lp, EW_ROWS, 16)

    def body(h_ref, t_ref, g_ref, dh_ref, dg_ref, loss_ref):
        i = pl.program_id(0)

        @pl.when(i == 0)
        def _():
            dg_ref[...] = jnp.zeros_like(dg_ref)
            loss_ref[...] = jnp.zeros_like(loss_ref)

        x = h_ref[...]
        r = lax.rsqrt(jnp.mean(x * x, axis=-1, keepdims=True) + EPS)
        xr = x * r
        y = xr * g_ref[...]
        row = i * tm + lax.broadcasted_iota(jnp.int32, (tm, d), 0)
        valid = (row >= N_META) & (row < N_META + n_real)
        diff = jnp.where(valid, y - t_ref[...], 0.0)
        loss_ref[...] += jnp.sum(diff * diff) * (0.5 / d)
        dy = diff * (1.0 / d)
        gy = dy * g_ref[...]
        dh_ref[...] = r * (gy - xr * jnp.mean(gy * xr, axis=-1, keepdims=True))
        dg_ref[...] += _sum8(dy * xr)

    return pl.pallas_call(
        body, out_shape=[SDS((lp, d), F32), SDS((8, d), F32), SDS((8, 128), F32)], grid=(lp // tm,),
        in_specs=[pl.BlockSpec((tm, d), lambda i: (i, 0)),
                  pl.BlockSpec((tm, d), lambda i: (i, 0)),
                  pl.BlockSpec((1, d), lambda i: (0, 0))],
        out_specs=[pl.BlockSpec((tm, d), lambda i: (i, 0)),
                   pl.BlockSpec((8, d), lambda i: (0, 0)),
                   pl.BlockSpec((8, 128), lambda i: (0, 0))],
        name=name, compiler_params=_cp())(h, tpad, g)


def _tri_consts():
    j = lax.broadcasted_iota(jnp.int32, (ABLK, ABLK), 0)
    s = lax.broadcasted_iota(jnp.int32, (ABLK, ABLK), 1)
    after = (j >= s).astype(BF16)
    before = (j < s).astype(BF16)
    ones = jnp.ones((ABLK, ABLK), BF16)
    two = lambda t: jnp.concatenate([t, t], axis=0)
    return (two(jnp.concatenate([after, ones], axis=1)),
            two(after),
            two(jnp.concatenate([before, ones], axis=1)))


def _softplus(z):
    neg_abs = lax.bitcast_convert_type(lax.bitcast_convert_type(z, jnp.uint32) | jnp.uint32(0x80000000), F32)
    return jnp.log(1.0 + jnp.exp(neg_abs)) + jnp.maximum(z, 0.0)


def _split_hi_lo(m):
    hi = m.astype(BF16)
    lo = (m - hi.astype(F32)).astype(BF16)
    return jnp.concatenate([hi, lo], axis=1)


def _head_halves(t2, in_a):
    zero = jnp.zeros_like(t2)
    return jnp.concatenate([jnp.where(in_a, t2, zero), jnp.where(in_a, zero, t2)], axis=0)


def _stack_blocks(t, nblk, in_a):
    return jnp.concatenate([_head_halves(t[u * ABLK:(u + 1) * ABLK], in_a) for u in range(nblk)], axis=0)


def _interleave(*gens):
    alive = list(gens)
    while alive:
        for g in list(alive):
            if next(g, alive) is alive:
                alive.remove(g)


def _attn_scale():
    scale = 1.0 / math.sqrt(HEAD_DIM)
    assert math.frexp(scale)[0] == 0.5, "a power of two, so that scaling q in bf16 is exact"
    return scale


def _pow2_below(n):
    assert n & (n - 1) == 0
    return [p for p in (64, 32, 16, 8, 4, 2, 1) if p < n]


class _Rider:
    def __init__(self, jobs=(), srcs=(), out_shape=()):
        self.jobs, self.srcs, self.out_shape = list(jobs), list(srcs), list(out_shape)
        self.any = [pl.BlockSpec(memory_space=pl.ANY)]

    def split(self, rest, n_out, n_scratch):
        ni, no = len(self.srcs), len(self.out_shape)
        self.ins, outs = rest[:ni], rest[ni:ni + n_out]
        self.outs = rest[ni + n_out:ni + n_out + no]
        scratch = rest[ni + n_out + no:ni + n_out + no + n_scratch]
        self.sems = rest[ni + n_out + no + n_scratch:]
        return outs, scratch

    def start(self, first):
        self.forwarded = False
        if self.jobs:
            @pl.when(first)
            def _():
                _exchange_start(self.jobs, self.ins, self.outs, *self.sems)

    def forward(self, late):
        self.forwarded = True
        if self.jobs:
            @pl.when(late)
            def _():
                _exchange_forward(self.jobs, self.ins, self.outs, *self.sems)

    def wait(self, last):
        if self.jobs:
            @pl.when(last)
            def _():
                if not self.forwarded:
                    _exchange_forward(self.jobs, self.ins, self.outs, *self.sems)
                _exchange_wait(self.jobs, self.ins, self.outs, *self.sems)

    def in_specs(self):
        return self.any * len(self.srcs)

    def out_specs(self):
        return self.any * len(self.out_shape)

    def scratch(self):
        return _exchange_sems(len(self.jobs)) if self.jobs else []


def _attn_fwd(qkv, tri_fwd, name, rider=None):
    lp = qkv.shape[0]
    n_pairs = (N_HEADS * HEAD_DIM) // 128
    nb = lp // ABLK
    assert nb <= 128 and 2 * HEAD_DIM == 128
    scale = _attn_scale()
    nt = (((1,), (1,)), ((), ()))
    rider = rider or _Rider()

    def body(q_ref, k_ref, v_ref, tri_ref, *rest):
        (o_ref, rs_ref), (r_s, acc_s, rs_s) = rider.split(rest, 2, 3)
        rider.start(pl.program_id(0) == 0)
        lane = lax.broadcasted_iota(jnp.int32, (ABLK, 128), 1)
        row = lax.broadcasted_iota(jnp.int32, (ABLK, 128), 0)
        in_a = lane < HEAD_DIM
        causal = lane < row

        def begin(st, i):
            r_s[st] = jnp.zeros(r_s.shape[1:], F32)
            acc_s[st] = jnp.zeros(acc_s.shape[1:], F32)
            rs_s[st] = jnp.full(rs_s.shape[1:], GONE, F32)
            return q_ref[pl.ds(pl.multiple_of(i * ABLK, ABLK), ABLK), :] * scale

        def live(st):
            least = jnp.min(jnp.minimum(r_s[st, 0], r_s[st, 1]), axis=0, keepdims=True)
            return (least[0, 0] < EXP_ZERO_AT).astype(jnp.int32)

        def step(*args):
            _interleave(stages(*args))

        def stages(st, q2, kb0, nblk, diag):
            k0 = pl.multiple_of(kb0 * ABLK, ABLK)
            kbd = _stack_blocks(k_ref[pl.ds(k0, nblk * ABLK), :], nblk, in_a)
            vbd = _stack_blocks(v_ref[pl.ds(k0, nblk * ABLK), :], nblk, in_a)
            z = lax.dot_general(q2, kbd, nt, preferred_element_type=F32)
            ncol = 2 * nblk
            zt = [z[:, c * 128:(c + 1) * 128] for c in range(ncol)]
            if diag:
                zt = [jnp.where(causal, t, MASKED) if c >= ncol - 2 else t for c, t in enumerate(zt)]
            yield
            bounds = [ncol * j // ATT_SPLIT for j in range(ATT_SPLIT + 1)]
            ce = [None] * ncol
            for c0, c1 in reversed([b for b in zip(bounds[:-1], bounds[1:]) if b[0] < b[1]]):
                parts = [_split_hi_lo(_softplus(zt[c])) for c in range(c0, c1)]
                got = jnp.dot(jnp.concatenate(parts, axis=0), tri_ref[...], preferred_element_type=F32)
                for c in range(c0, c1):
                    ce[c] = got[(c - c0) * 128:(c - c0 + 1) * 128]
                yield
            rr = [r_s[st, 0], r_s[st, 1]]
            rsv = [rs_s[st, :, :128], rs_s[st, :, 128:]]
            ws = [None] * ncol
            for u in reversed(range(nblk)):
                for hh in range(2):
                    c = 2 * u + hh
                    ws[c] = jnp.exp(zt[c] - ce[c][:, :128] - rr[hh]).astype(BF16)
                    rsv[hh] = jnp.where(lane == kb0 + u, rr[hh], rsv[hh])
                    rr[hh] = rr[hh] + ce[c][:, 128:]
            acc_s[st] += jnp.dot(jnp.concatenate(ws, axis=1), vbd, preferred_element_type=F32)
            r_s[st, 0] = rr[0]
            r_s[st, 1] = rr[1]
            rs_s[st, :, :128] = rsv[0]
            rs_s[st, :, 128:] = rsv[1]

        def finish(st, q2, i, n_top):
            i_low = i - n_top
            n_grp = i_low // ATT_GROUP

            def more(c):
                return (c[0] < n_grp) & (c[1] > 0)

            def inner(c):
                step(st, q2, i_low - ATT_GROUP * (c[0] + 1), ATT_GROUP, False)
                return c[0] + 1, live(st)

            _, alive = lax.while_loop(more, inner, (jnp.int32(0), live(st)))
            rem = i_low - ATT_GROUP * n_grp
            for p in _pow2_below(ATT_GROUP):
                def last_steps(p=p):
                    step(st, q2, rem & (p - 1), p, False)
                    return live(st)

                alive = lax.cond(((rem & p) != 0) & (alive > 0), last_steps, lambda alive=alive: alive)

            q0 = pl.multiple_of(i * ABLK, ABLK)
            o_ref[pl.ds(q0, ABLK), :] = acc_s[st].astype(BF16)
            rs_ref[pl.ds(q0, ABLK), :] = rs_s[st]

        def single(i):
            q2 = begin(0, i)
            n_top = jnp.minimum(i, ATT_TOP)
            for t in range(ATT_TOP + 1):
                @pl.when(n_top == t)
                def _():
                    step(0, q2, i - t, t + 1, True)

            finish(0, q2, i, n_top)

        def pair(i):
            qa, qb = begin(0, i), begin(1, i + 1)
            _interleave(stages(0, qa, i - ATT_TOP, ATT_TOP + 1, True), stages(1, qb, i + 1 - ATT_TOP, ATT_TOP + 1, True))
            finish(0, qa, i, ATT_TOP)
            finish(1, qb, i + 1, ATT_TOP)

        n_head = min(ATT_TOP, nb)
        n_pair = (nb - n_head) // 2
        n_single = nb - 2 * n_pair

        def singles(k, carry):
            single(jnp.where(k < n_head, k, nb - n_single + k))
            return carry

        def pairs(j, carry):
            pair(n_head + 2 * j)
            return carry

        lax.fori_loop(0, n_single, singles, 0)
        lax.fori_loop(0, n_pair, pairs, 0)
        rider.wait(pl.program_id(0) == n_pairs - 1)

    col = lambda o: (lambda p: (0, p + o))
    return pl.pallas_call(
        body, out_shape=[SDS((lp, n_pairs * 128), BF16), SDS((lp, n_pairs * 256), F32)] + rider.out_shape,
        grid=(n_pairs,),
        in_specs=[pl.BlockSpec((lp, 128), col(0)), pl.BlockSpec((lp, 128), col(n_pairs)),
                  pl.BlockSpec((lp, 128), col(2 * n_pairs)), pl.BlockSpec((256, 256), lambda p: (0, 0))]
        + rider.in_specs(),
        out_specs=[pl.BlockSpec((lp, 128), col(0)), pl.BlockSpec((lp, 256), col(0))] + rider.out_specs(),
        scratch_shapes=[pltpu.VMEM((2, 2, ABLK, 128), F32), pltpu.VMEM((2, ABLK, 128), F32),
                        pltpu.VMEM((2, ABLK, 256), F32)] + rider.scratch(),
        name=name, compiler_params=_cp())(qkv, qkv, qkv, tri_fwd, *rider.srcs)


def _attn_bwd(qkv, d_out, rsave, tri_after, tri_before, name, rider=None):
    lp = qkv.shape[0]
    n_pairs = (N_HEADS * HEAD_DIM) // 128
    nb = lp // ABLK
    scale = _attn_scale()
    nt = (((1,), (1,)), ((), ()))
    tn = (((0,), (0,)), ((), ()))
    rider = rider or _Rider()

    def body(q_ref, k_ref, v_ref, do_ref, rs_ref, ta_ref, tb_ref, *rest):
        (o_ref,), (dk_s, dv_s, dq_s, pc_s) = rider.split(rest, 1, 4)
        rider.start(pl.program_id(0) == 0)
        lane = lax.broadcasted_iota(jnp.int32, (ABLK, 128), 1)
        row = lax.broadcasted_iota(jnp.int32, (ABLK, 128), 0)
        in_a = lane < HEAD_DIM
        causal = lane < row
        dk_s[...] = jnp.zeros_like(dk_s)
        dv_s[...] = jnp.zeros_like(dv_s)

        def begin(st, i):
            q0 = pl.multiple_of(i * ABLK, ABLK)
            q2 = q_ref[pl.ds(q0, ABLK), :] * scale
            do2 = do_ref[pl.ds(q0, ABLK), :]
            dq_s[st] = jnp.zeros(dq_s.shape[1:], F32)
            pc_s[st] = jnp.zeros(pc_s.shape[1:], F32)
            return dict(q0=q0, q2=q2, do2=do2, q_st=_head_halves(q2, in_a), do_st=_head_halves(do2, in_a))

        def step(*args):
            _interleave(stages(*args))

        def stages(st, blk, kb0, nblk, diag):
            q0, q2, do2 = blk["q0"], blk["q2"], blk["do2"]
            k0 = pl.multiple_of(kb0 * ABLK, ABLK)
            kbd = _stack_blocks(k_ref[pl.ds(k0, nblk * ABLK), :], nblk, in_a)
            vbd = _stack_blocks(v_ref[pl.ds(k0, nblk * ABLK), :], nblk, in_a)
            z = lax.dot_general(q2, kbd, nt, preferred_element_type=F32)
            dw = lax.dot_general(do2, vbd, nt, preferred_element_type=F32)
            ncol = 2 * nblk
            zt = [z[:, c * 128:(c + 1) * 128] for c in range(ncol)]
            if diag:
                zt = [jnp.where(causal, t, MASKED) if c >= ncol - 2 else t for c, t in enumerate(zt)]
            bounds = [ncol * j // ATT_SPLIT for j in range(ATT_SPLIT + 1)]
            batches = [range(c0, c1) for c0, c1 in zip(bounds[:-1], bounds[1:]) if c0 < c1]
            sps, ex, ws, dls, pe = [None] * ncol, [None] * ncol, [None] * ncol, [None] * ncol, [None] * ncol

            def mass(cols):
                for c in cols:
                    sps[c] = _softplus(zt[c])
                got = jnp.dot(jnp.concatenate([_split_hi_lo(sps[c]) for c in cols], axis=0), ta_ref[...],
                              preferred_element_type=F32)
                for j, c in enumerate(cols):
                    ex[c] = got[j * 128:(j + 1) * 128]

            def weights(cols):
                for c in cols:
                    u, hh = c // 2, c % 2
                    r_saved = jnp.sum(jnp.where(lane == kb0 + u, rs_ref[pl.ds(q0, ABLK), hh * 128:(hh + 1) * 128],
                                                0.0), axis=1, keepdims=True)
                    w = jnp.exp(zt[c] - ex[c] - r_saved)
                    ws[c] = w.astype(BF16)
                    dls[c] = dw[:, c * 128:(c + 1) * 128] * w
                got = jnp.dot(jnp.concatenate([_split_hi_lo(dls[c]) for c in cols], axis=0), tb_ref[...],
                              preferred_element_type=F32)
                for j, c in enumerate(cols):
                    pe[c] = got[j * 128:(j + 1) * 128]

            yield
            mass(batches[0])
            yield
            for j in range(len(batches)):
                if j + 1 < len(batches):
                    mass(batches[j + 1])
                    yield
                weights(batches[j])
                yield
            pc = [pc_s[st, 0], pc_s[st, 1]]
            dzs = []
            for c in range(ncol):
                hh = c % 2
                one_minus_beta = jnp.exp(-sps[c])
                dz = dls[c] * one_minus_beta - (pe[c][:, :128] + pc[hh]) * (1.0 - one_minus_beta)
                pc[hh] = pc[hh] + pe[c][:, 128:]
                dzs.append(dz.astype(BF16))
            pc_s[st, 0] = pc[0]
            pc_s[st, 1] = pc[1]
            dq_s[st] += jnp.dot(jnp.concatenate(dzs, axis=1), kbd, preferred_element_type=F32)
            by_head = lambda ts: jnp.concatenate([jnp.concatenate(ts[0::2], axis=1), jnp.concatenate(ts[1::2], axis=1)],
                                                 axis=0)
            rows = pl.ds(k0, nblk * ABLK)
            dk_s[rows, :] += lax.dot_general(by_head(dzs), blk["q_st"], tn, preferred_element_type=F32)
            dv_s[rows, :] += lax.dot_general(by_head(ws), blk["do_st"], tn, preferred_element_type=F32)

        def below(st, blk, i, n_top):
            gone = jnp.min(rs_ref[pl.ds(blk["q0"], ABLK), :], axis=0, keepdims=True) >= EXP_ZERO_AT
            lane1 = lax.broadcasted_iota(jnp.int32, (1, 128), 1)
            first = jnp.sum(jnp.where(gone[:, :128] & gone[:, 128:] & (lane1 < i), 1.0, 0.0)).astype(jnp.int32)
            i_low = i - n_top
            n_grp = i_low // ATT_GROUP
            rem = i_low - ATT_GROUP * n_grp
            for p in reversed(_pow2_below(ATT_GROUP)):
                @pl.when(((rem & p) != 0) & ((rem & (p - 1)) + p > first))
                def _():
                    step(st, blk, rem & (p - 1), p, False)

            def inner(g, c2):
                step(st, blk, rem + ATT_GROUP * g, ATT_GROUP, False)
                return c2

            lax.fori_loop(jnp.maximum(first - rem, 0) // ATT_GROUP, n_grp, inner, 0)

        def done(st, blk):
            o_ref[0, pl.ds(blk["q0"], ABLK), :] = (dq_s[st] * scale).astype(BF16)

        def single(i):
            blk = begin(0, i)
            n_top = jnp.minimum(i, ATT_TOP)
            below(0, blk, i, n_top)
            for t in range(ATT_TOP + 1):
                @pl.when(n_top == t)
                def _():
                    step(0, blk, i - t, t + 1, True)

            done(0, blk)

        def pair(i):
            a, b = begin(0, i), begin(1, i + 1)
            below(0, a, i, ATT_TOP)
            below(1, b, i + 1, ATT_TOP)
            _interleave(stages(0, a, i - ATT_TOP, ATT_TOP + 1, True), stages(1, b, i + 1 - ATT_TOP, ATT_TOP + 1, True))
            done(0, a)
            done(1, b)

        n_head = min(ATT_TOP, nb)
        n_pair = (nb - n_head) // 2
        n_single = nb - 2 * n_pair

        def singles(k, carry):
            single(jnp.where(k < n_head, k, nb - n_single + k))
            return carry

        def pairs(j, carry):
            pair(n_head + 2 * j)
            return carry

        lax.fori_loop(0, n_single, singles, 0)
        lax.fori_loop(0, n_pair, pairs, 0)
        o_ref[1] = dk_s[...].astype(BF16)
        o_ref[2] = dv_s[...].astype(BF16)
        rider.wait(pl.program_id(0) == n_pairs - 1)

    col = lambda o: (lambda p: (0, p + o))
    return pl.pallas_call(
        body, out_shape=[SDS((3, lp, n_pairs * 128), BF16)] + rider.out_shape, grid=(n_pairs,),
        in_specs=[pl.BlockSpec((lp, 128), col(0)), pl.BlockSpec((lp, 128), col(n_pairs)),
                  pl.BlockSpec((lp, 128), col(2 * n_pairs)), pl.BlockSpec((lp, 128), col(0)),
                  pl.BlockSpec((lp, 256), col(0)),
                  pl.BlockSpec((256, 128), lambda p: (0, 0)), pl.BlockSpec((256, 256), lambda p: (0, 0))]
        + rider.in_specs(),
        out_specs=[pl.BlockSpec((3, lp, 128), lambda p: (0, 0, p))] + rider.out_specs(),
        scratch_shapes=[pltpu.VMEM((lp, 128), F32), pltpu.VMEM((lp, 128), F32),
                        pltpu.VMEM((2, ABLK, 128), F32), pltpu.VMEM((2, 2, ABLK, 128), F32)] + rider.scratch(),
        name=name, compiler_params=_cp())(qkv, qkv, qkv, d_out, rsave, tri_after, tri_before, *rider.srcs)


def _conv_fwd_dw(cacg, w, b, name, rider=None):
    lp = cacg.shape[0]
    c = cacg.shape[1] // 2
    ncb = c // 128
    nchunk = lp // ABLK
    off = CONV_PAD - (CONV_K - 1)
    rider = rider or _Rider()

    def body(a_ref, g_ref, w_ref, b_ref, *rest):
        (y_ref,), (upad,) = rider.split(rest, 1, 1)
        rider.start(pl.program_id(0) == 0)
        upad[0:CONV_PAD, :] = jnp.zeros((CONV_PAD, 128), F32)

        def fill(ch, carry):
            base = pl.multiple_of(ch * ABLK, ABLK)
            upad[pl.ds(base + CONV_PAD, ABLK), :] = a_ref[pl.ds(base, ABLK), :] * _sigmoid(g_ref[pl.ds(base, ABLK), :])
            return carry

        lax.fori_loop(0, nchunk, fill, 0)

        def comp(ch, carry):
            base = pl.multiple_of(ch * ABLK, ABLK)
            acc = jnp.zeros((ABLK, 128), F32)
            for k in range(CONV_K):
                acc = acc + upad[pl.ds(base + (off + k), ABLK), :] * w_ref[k:k + 1, :]
            y_ref[pl.ds(base, ABLK), :] = acc + b_ref[...]
            return carry

        lax.fori_loop(0, nchunk, comp, 0)
        rider.wait(pl.program_id(0) == ncb - 1)

    return pl.pallas_call(
        body, out_shape=[SDS((lp, c), F32)] + rider.out_shape, grid=(ncb,),
        in_specs=[pl.BlockSpec((lp, 128), lambda j: (0, j)), pl.BlockSpec((lp, 128), lambda j: (0, j + ncb)),
                  pl.BlockSpec((CONV_PAD, 128), lambda j: (0, j)), pl.BlockSpec((1, 128), lambda j: (0, j))]
        + rider.in_specs(),
        out_specs=[pl.BlockSpec((lp, 128), lambda j: (0, j))] + rider.out_specs(),
        scratch_shapes=[pltpu.VMEM((lp + CONV_PAD, 128), F32)] + rider.scratch(), name=name,
        compiler_params=_cp())(cacg, cacg, w, b, *rider.srcs)


def _ln_parts(x, g, b):
    mu = jnp.mean(x, axis=-1, keepdims=True)
    xc = x - mu
    rstd = lax.rsqrt(jnp.mean(xc * xc, axis=-1, keepdims=True) + EPS)
    xh = xc * rstd
    return xh, rstd, xh * g + b


def _conv_fwd_ln(yc, g, b, name):
    lp, c = yc.shape
    tm = _tile(lp, EW_ROWS, 16)

    def body(y_ref, g_ref, b_ref, o_ref):
        _, _, ln = _ln_parts(y_ref[...], g_ref[...], b_ref[...])
        o_ref[...] = (ln * _sigmoid(ln)).astype(BF16)

    return pl.pallas_call(
        body, out_shape=SDS((lp, c), BF16), grid=(lp // tm,),
        in_specs=[pl.BlockSpec((tm, c), lambda i: (i, 0)), pl.BlockSpec((1, c), lambda i: (0, 0)),
                  pl.BlockSpec((1, c), lambda i: (0, 0))],
        out_specs=pl.BlockSpec((tm, c), lambda i: (i, 0)), name=name, compiler_params=_cp())(yc, g, b)


def _conv_bwd_ln(yc, dout, g, b, name):
    lp, c = yc.shape
    tm = _tile(lp, EW_ROWS, 16)

    def body(y_ref, d_ref, g_ref, b_ref, o_ref, dg_ref, db_ref):
        @pl.when(pl.program_id(0) == 0)
        def _():
            dg_ref[...] = jnp.zeros_like(dg_ref)
            db_ref[...] = jnp.zeros_like(db_ref)

        xh, rstd, ln = _ln_parts(y_ref[...], g_ref[...], b_ref[...])
        s = _sigmoid(ln)
        dln = d_ref[...] * (s * (1.0 + ln * (1.0 - s)))
        dg_ref[...] += _sum8(dln * xh)
        db_ref[...] += _sum8(dln)
        dxh = dln * g_ref[...]
        o_ref[...] = rstd * (dxh - jnp.mean(dxh, axis=-1, keepdims=True)
                             - xh * jnp.mean(dxh * xh, axis=-1, keepdims=True))

    return pl.pallas_call(
        body, out_shape=[SDS((lp, c), F32), SDS((8, c), F32), SDS((8, c), F32)], grid=(lp // tm,),
        in_specs=[pl.BlockSpec((tm, c), lambda i: (i, 0)), pl.BlockSpec((tm, c), lambda i: (i, 0)),
                  pl.BlockSpec((1, c), lambda i: (0, 0)), pl.BlockSpec((1, c), lambda i: (0, 0))],
        out_specs=[pl.BlockSpec((tm, c), lambda i: (i, 0)), pl.BlockSpec((8, c), lambda i: (0, 0)),
                   pl.BlockSpec((8, c), lambda i: (0, 0))],
        name=name, compiler_params=_cp())(yc, dout, g, b)


def _conv_bwd_dw(dyc, cacg, w, name):
    lp, c = dyc.shape
    ncb = c // 128
    nchunk = lp // ABLK
    off = CONV_PAD - (CONV_K - 1)

    def body(dy_ref, a_ref, g_ref, w_ref, dcc_ref, dw_ref, db_ref, upad, dypad, dwacc):
        upad[0:CONV_PAD, :] = jnp.zeros((CONV_PAD, 128), F32)
        dypad[lp:lp + CONV_PAD, :] = jnp.zeros((CONV_PAD, 128), F32)
        dwacc[...] = jnp.zeros_like(dwacc)
        db_ref[...] = jnp.zeros_like(db_ref)

        def fill(ch, carry):
            base = pl.multiple_of(ch * ABLK, ABLK)
            upad[pl.ds(base + CONV_PAD, ABLK), :] = a_ref[pl.ds(base, ABLK), :] * _sigmoid(g_ref[pl.ds(base, ABLK), :])
            dypad[pl.ds(base, ABLK), :] = dy_ref[pl.ds(base, ABLK), :]
            return carry

        lax.fori_loop(0, nchunk, fill, 0)

        def comp(ch, carry):
            base = pl.multiple_of(ch * ABLK, ABLK)
            dy = dy_ref[pl.ds(base, ABLK), :]
            du = jnp.zeros((ABLK, 128), F32)
            for k in range(CONV_K):
                du = du + dypad[pl.ds(base + (CONV_K - 1 - k), ABLK), :] * w_ref[k:k + 1, :]
                dwacc[k * 8:(k + 1) * 8, :] += _sum8(dy * upad[pl.ds(base + (off + k), ABLK), :])
            db_ref[...] += _sum8(dy)
            a = a_ref[pl.ds(base, ABLK), :]
            s = _sigmoid(g_ref[pl.ds(base, ABLK), :])
            dcc_ref[0, pl.ds(base, ABLK), :] = (du * s).astype(BF16)
            dcc_ref[1, pl.ds(base, ABLK), :] = (du * a * (s * (1.0 - s))).astype(BF16)
            return carry

        lax.fori_loop(0, nchunk, comp, 0)
        dw_ref[...] = dwacc[...].reshape(CONV_PAD, 8, 128).sum(axis=1)

    return pl.pallas_call(
        body, out_shape=[SDS((2, lp, c), BF16), SDS((CONV_PAD, c), F32), SDS((8, c), F32)],
        grid=(ncb,),
        in_specs=[pl.BlockSpec((lp, 128), lambda j: (0, j)), pl.BlockSpec((lp, 128), lambda j: (0, j)),
                  pl.BlockSpec((lp, 128), lambda j: (0, j + ncb)), pl.BlockSpec((CONV_PAD, 128), lambda j: (0, j))],
        out_specs=[pl.BlockSpec((2, lp, 128), lambda j: (0, 0, j)),
                   pl.BlockSpec((CONV_PAD, 128), lambda j: (0, j)), pl.BlockSpec((8, 128), lambda j: (0, j))],
        scratch_shapes=[pltpu.VMEM((lp + CONV_PAD, 128), F32), pltpu.VMEM((lp + CONV_PAD, 128), F32),
                        pltpu.VMEM((CONV_PAD * 8, 128), F32)],
        name=name, compiler_params=_cp())(dyc, cacg, cacg, w)


def _mesh_pos():
    x, y, c = lax.axis_index("x"), lax.axis_index("y"), lax.axis_index("c")
    return x, y, c


def _peer(pos, r):
    x, y, c = pos
    px = (1 - x) if (r >> 2) & 1 else x
    py = (1 - y) if (r >> 1) & 1 else y
    pc = (1 - c) if r & 1 else c
    return (px, py, pc), 4 * px + 2 * py + pc


SIBLING = 1
OTHER_CHIPS = (2, 4, 6)


class _Job:
    def __init__(self, src, dst, scatter, src_layer=None, dst_layer=None):
        self.src, self.dst, self.scatter, self.src_layer, self.dst_layer = src, dst, scatter, src_layer, dst_layer

    def src_view(self, ins, idx):
        v = ins[self.src] if self.src_layer is None else ins[self.src].at[self.src_layer]
        return v.at[idx] if self.scatter else v

    def dst_view(self, outs, slot):
        v = outs[self.dst] if self.dst_layer is None else outs[self.dst].at[self.dst_layer]
        return v.at[slot]


def _remote(job, j, r, src, dst, to, send, recv):
    return pltpu.make_async_remote_copy(src_ref=src, dst_ref=dst, send_sem=send.at[j, r - 1], recv_sem=recv.at[j, r - 1],
                                        device_id=to, device_id_type=pl.DeviceIdType.MESH)


def _exchange_start(jobs, ins, outs, send, recv, loc):
    pos = _mesh_pos()
    me = 4 * pos[0] + 2 * pos[1] + pos[2]
    for j, job in enumerate(jobs):
        pltpu.make_async_copy(job.src_view(ins, me), job.dst_view(outs, me), loc.at[j]).start()
        for r in (range(1, N_DEV) if job.scatter else (SIBLING,) + OTHER_CHIPS):
            peer, peer_idx = _peer(pos, r)
            _remote(job, j, r, job.src_view(ins, peer_idx), job.dst_view(outs, me), peer, send, recv).start()


def _exchange_forward(jobs, ins, outs, send, recv, loc):
    pos = _mesh_pos()
    sibling, _ = _peer(pos, SIBLING)
    for j, job in enumerate(jobs):
        if job.scatter:
            continue
        for r in OTHER_CHIPS:
            peer, peer_idx = _peer(pos, r)
            slot = job.dst_view(outs, peer_idx)
            _remote(job, j, r, job.src_view(ins, peer_idx), slot, peer, send, recv).wait_recv()
            _remote(job, j, r ^ SIBLING, slot, slot, sibling, send, recv).start()


def _exchange_wait(jobs, ins, outs, send, recv, loc):
    pos = _mesh_pos()
    me = 4 * pos[0] + 2 * pos[1] + pos[2]
    for j, job in enumerate(jobs):
        for r in range(1, N_DEV):
            peer, peer_idx = _peer(pos, r)
            cp = _remote(job, j, r, job.src_view(ins, peer_idx), job.dst_view(outs, peer_idx), peer, send, recv)
            if job.scatter or r not in OTHER_CHIPS:
                cp.wait_recv()
            cp.wait_send()
        pltpu.make_async_copy(job.src_view(ins, me), job.dst_view(outs, me), loc.at[j]).wait()


def _exchange_sems(n_jobs):
    return [pltpu.SemaphoreType.DMA((n_jobs, N_DEV - 1)), pltpu.SemaphoreType.DMA((n_jobs, N_DEV - 1)),
            pltpu.SemaphoreType.DMA((n_jobs,))]


def _exchange(jobs, arrs, out_shape, name):
    n_in, n_out = len(arrs), len(out_shape)
    any_spec = pl.BlockSpec(memory_space=pl.ANY)

    def body(*refs):
        ins, outs, sems = refs[:n_in], refs[n_in:n_in + n_out], refs[n_in + n_out:]
        _exchange_start(jobs, ins, outs, *sems)
        _exchange_forward(jobs, ins, outs, *sems)
        _exchange_wait(jobs, ins, outs, *sems)

    return pl.pallas_call(
        body, out_shape=out_shape, in_specs=[any_spec] * n_in, out_specs=[any_spec] * n_out,
        scratch_shapes=_exchange_sems(len(jobs)),
        name=name, compiler_params=pltpu.CompilerParams(has_side_effects=True))(*arrs)


def _all_reduce_small(placed, cuts, rows, w, name):
    n_in = len(placed)

    def body(*refs):
        in_refs, o_refs = refs[:n_in], refs[n_in:n_in + len(cuts)]
        buf, send, recv = refs[n_in + len(cuts):]
        pos = _mesh_pos()
        me = 4 * pos[0] + 2 * pos[1] + pos[2]
        buf[me] = jnp.zeros((rows, w), F32)
        for ref, (arr, row0, col0, is_partial) in zip(in_refs, placed):
            val = ref[...].sum(axis=0, keepdims=True) if is_partial else ref[...]
            buf[me, row0:row0 + val.shape[0], col0:col0 + val.shape[1]] = val
        for r in range(1, N_DEV):
            peer, _ = _peer(pos, r)
            pltpu.make_async_remote_copy(src_ref=buf.at[me], dst_ref=buf.at[me], send_sem=send.at[r - 1],
                                         recv_sem=recv.at[r - 1], device_id=peer,
                                         device_id_type=pl.DeviceIdType.MESH).start()
        for r in range(1, N_DEV):
            peer, peer_idx = _peer(pos, r)
            cp = pltpu.make_async_remote_copy(src_ref=buf.at[me], dst_ref=buf.at[peer_idx], send_sem=send.at[r - 1],
                                              recv_sem=recv.at[r - 1], device_id=peer,
                                              device_id_type=pl.DeviceIdType.MESH)
            cp.wait_recv()
            cp.wait_send()
        acc = buf[0]
        for dev in range(1, N_DEV):
            acc = acc + buf[dev]
        for o_ref, (_, pieces) in zip(o_refs, cuts):
            for index, row0, nrows, col0, ncols in pieces:
                o_ref[index] = acc[row0:row0 + nrows, col0:col0 + ncols].reshape(o_ref.at[index].shape)

    vmem = pl.BlockSpec(memory_space=pltpu.VMEM)
    return pl.pallas_call(
        body, out_shape=[SDS(shape, F32) for shape, _ in cuts], in_specs=[vmem] * n_in, out_specs=[vmem] * len(cuts),
        scratch_shapes=[pltpu.VMEM((N_DEV, rows, w), F32), pltpu.SemaphoreType.DMA((N_DEV - 1,)),
                        pltpu.SemaphoreType.DMA((N_DEV - 1,))],
        name=name, compiler_params=pltpu.CompilerParams(has_side_effects=True))(*[p[0] for p in placed])


def _adamw_math(w, g, m, v):
    m = ADAM_B1 * m + (1.0 - ADAM_B1) * g
    v = ADAM_B2 * v + (1.0 - ADAM_B2) * (g * g)
    m_hat = m / (1.0 - ADAM_B1 ** ADAM_STEP)
    v_hat = v / (1.0 - ADAM_B2 ** ADAM_STEP)
    delta = -ADAM_LR * (m_hat / (jnp.sqrt(v_hat) + ADAM_EPS) + ADAM_WD * w)
    return delta, m, v


def _adamw_shard(parts, w, m, v, name):
    depth = len(parts)
    _, rr, cc = parts[0].shape
    tr = _tile(rr, 256, 8)
    nt = rr // tr
    part_block, blk = (N_DEV, tr, cc), pl.BlockSpec((None, tr, cc), lambda l, i: (l, i, 0))

    def body(*refs):
        p_refs = refs[:depth]
        w_ref, m_ref, v_ref, g_out, d_out, m_out, v_out = refs[depth:]
        for li in range(depth):
            @pl.when(pl.program_id(0) == li)
            def _(p_ref=p_refs[li]):
                g = p_ref[0].astype(F32)
                for dev in range(1, N_DEV):
                    g = g + p_ref[dev].astype(F32)
                delta, mm, vv = _adamw_math(w_ref[...], g, m_ref[...], v_ref[...])
                g_out[...] = g
                d_out[...] = delta
                m_out[...] = mm
                v_out[...] = vv

    def part_spec(li):
        return pl.BlockSpec(part_block, lambda l, i: (0, jnp.where(l == li, i, jnp.where(l < li, 0, nt - 1)), 0))

    return pl.pallas_call(
        body, out_shape=[SDS(w.shape, F32)] * 4, grid=(depth, nt),
        in_specs=[part_spec(li) for li in range(depth)] + [blk, blk, blk],
        out_specs=[blk] * 4, name=name, compiler_params=_cp())(*parts, w, m, v)


def _adamw_small(gs, ws, ms, vs, name):
    n = len(gs)

    def body(*refs):
        g_refs, w_refs, m_refs, v_refs = (refs[k * n:(k + 1) * n] for k in range(4))
        outs = refs[4 * n:]
        for k in range(n):
            delta, mm, vv = _adamw_math(w_refs[k][...], g_refs[k][...], m_refs[k][...], v_refs[k][...])
            outs[k][...] = delta
            outs[n + k][...] = mm
            outs[2 * n + k][...] = vv

    res = pl.pallas_call(body, out_shape=[SDS(w.shape, F32) for w in ws] * 3, name=name,
                         compiler_params=_cp())(*gs, *ws, *ms, *vs)
    return res[:n], res[n:2 * n], res[2 * n:]


def _from_cols(t):
    return jnp.transpose(t, (1, 0, 2)).reshape(t.shape[1], N_DEV * t.shape[2])


def _swap(t):
    return jnp.swapaxes(t, -1, -2)


def kernel(x, meta_tokens, mix_norm_g, w_in, conv_dw_w, conv_dw_b, conv_ln_g, conv_ln_b, w_out, ffn_norm_g, w_gate, w_up, w_down, final_norm_g, loss_target, m_meta_tokens, m_mix_norm_g, m_w_in, m_conv_dw_w, m_conv_dw_b, m_conv_ln_g, m_conv_ln_b, m_w_out, m_ffn_norm_g, m_w_gate, m_w_up, m_w_down, m_final_norm_g, v_meta_tokens, v_mix_norm_g, v_w_in, v_conv_dw_w, v_conv_dw_b, v_conv_ln_g, v_conv_ln_b, v_w_out, v_ffn_norm_g, v_w_gate, v_w_up, v_w_down, v_final_norm_g):
    depth, d, in_shard = w_in.shape
    seq = x.shape[1]
    sb = N_HEADS * HEAD_DIM
    cc = conv_dw_w.shape[2] * N_DEV
    ff = w_gate.shape[2] * N_DEV
    assert in_shard * N_DEV == 3 * sb + 2 * cc and x.shape[0] == 1
    lr = N_META + seq
    lp = -(-lr // ABLK) * ABLK
    me = 4 * lax.axis_index("x") + 2 * lax.axis_index("y") + lax.axis_index("c")

    big_names = ("w_in", "w_out", "w_gate", "w_up", "w_down")
    transposed = {"w_in": True, "w_out": False, "w_gate": True, "w_up": True, "w_down": False}
    shard = dict(w_in=_swap(w_in).astype(BF16), w_out=w_out.astype(BF16), w_gate=_swap(w_gate).astype(BF16),
                 w_up=_swap(w_up).astype(BF16), w_down=w_down.astype(BF16))

    def gather_of(keys):
        names = sorted({n for n, _ in keys}, key=big_names.index)
        jobs = [_Job(names.index(n), j, False, src_layer=i) for j, (n, i) in enumerate(keys)]
        return jobs, [shard[n] for n in names], [SDS((N_DEV,) + shard[n].shape[1:], BF16) for n, _ in keys]

    first_keys = [("w_in", 0)]

    def riding(keys):
        return _Rider(*gather_of(keys)) if keys else None

    def receive(keys, arrays):
        for (n, li), t in zip(keys, arrays):
            wl[li][n] = t.reshape(-1, d)

    jobs, srcs, out_shape = gather_of(first_keys)
    for extra in (meta_tokens, conv_dw_w):
        jobs.append(_Job(len(srcs), len(out_shape), False))
        srcs.append(extra)
        out_shape.append(SDS((N_DEV,) + extra.shape, F32))
    gathered = _exchange(jobs, srcs, out_shape, "gather_first")
    wl = [dict() for _ in range(depth)]
    receive(first_keys, gathered)
    meta_full = _from_cols(gathered[-2])
    taps = jnp.transpose(gathered[-1], (1, 2, 0, 3)).reshape(depth, CONV_K, cc)
    taps = jnp.pad(taps, ((0, 0), (0, CONV_PAD - CONV_K), (0, 0)))
    tri_fwd, tri_after, tri_before = _tri_consts()

    h = jnp.concatenate([meta_full, x[0], jnp.zeros((lp - lr, d), F32)], axis=0)
    saved = []
    for i in range(depth):
        p = wl[i]
        sv = dict(h_in=h)
        qkv, hn = _rms_mm(h, mix_norm_g[i:i + 1], p["w_in"], 0, 3 * sb, BF16, f"proj_qkv_{i}", True)
        keys = [("w_out", 0)] if i == 0 else []
        cacg, *arrived = _rms_mm(h, mix_norm_g[i:i + 1], p["w_in"], 3 * sb, 2 * cc, F32, f"proj_conv_{i}", False,
                                 riding(keys))
        receive(keys, arrived)
        keys = [("w_gate", i), ("w_up", i)]
        attn, rsave, *arrived = _attn_fwd(qkv, tri_fwd, f"attn_fwd_{i}", riding(keys))
        receive(keys, arrived)
        keys = [("w_down", 0)] if i == 0 else []
        yc, *arrived = _conv_fwd_dw(cacg, taps[i], conv_dw_b[i:i + 1], f"conv_fwd_dw_{i}", riding(keys))
        receive(keys, arrived)
        conv = _conv_fwd_ln(yc, conv_ln_g[i:i + 1], conv_ln_b[i:i + 1], f"conv_fwd_ln_{i}")
        h = _mix_out(attn, conv, p["w_out"], h, f"mix_out_{i}")
        sv.update(qkv=qkv, hn=hn, cacg=cacg, rsave=rsave, yc=yc, attn=attn, conv=conv, h_mid=h)
        nxt = i + 1
        keys = [("w_in", nxt), ("w_out", nxt), ("w_down", nxt)] if nxt < depth else []
        h, hn2, act, gate, up, *arrived = _ffn_fwd(h, ffn_norm_g[i:i + 1], p["w_gate"], p["w_up"], p["w_down"],
                                                   f"ffn_fwd_{i}", riding(keys))
        receive(keys, arrived)
        sv.update(hn2=hn2, act=act, gate=gate, up=up)
        saved.append(sv)

    tpad = jnp.pad(loss_target[0], ((N_META, lp - lr), (0, 0)))
    dh, dg_final, loss_part = _loss_head(h, tpad, final_norm_g.reshape(1, d), seq, "loss_head")
    loss = lax.psum(loss_part[0, 0], MESH_AXES)

    parts = {}

    def sending(items):
        srcs = [t.reshape(N_DEV, t.shape[0] // N_DEV, d) for _, t in items]
        return _Rider([_Job(j, j, True) for j in range(len(items))], srcs, [SDS(t.shape, BF16) for t in srcs])

    def arrive(items, arrays):
        parts.update({key: t for (key, _), t in zip(items, arrays)})

    grads = [None] * depth
    from_above = []
    for i in reversed(range(depth)):
        p, sv = wl[i], saved[i]
        d_g, d_u, *got = _ffn_bwd_act(dh, sv["gate"], sv["up"], p["w_down"], f"ffn_bwd_act_{i}", sending(from_above))
        arrive(from_above, got)
        gw_down = _mm_tn(sv["act"], dh, f"grad_w_down_{i}")
        gw_gate = _mm_tn(d_g, sv["hn2"], f"grad_w_gate_{i}")
        gw_up = _mm_tn(d_u, sv["hn2"], f"grad_w_up_{i}")
        items = [(("w_down", i), gw_down)]
        dh, dg_ffn, *got = _ffn_bwd_in(d_g, d_u, p["w_gate"], p["w_up"], sv["h_mid"], ffn_norm_g[i:i + 1], dh,
                                       f"ffn_bwd_in_{i}", sending(items))
        arrive(items, got)
        gw_out = jnp.concatenate([_mm_tn(sv["attn"], dh, f"grad_w_out_attn_{i}"),
                                  _mm_tn(sv["conv"], dh, f"grad_w_out_conv_{i}")], axis=0)
        d_attn = _mm_nt(dh, p["w_out"], 0, sb, BF16, f"mix_bwd_attn_{i}")
        d_conv = _mm_nt(dh, p["w_out"], sb, cc, F32, f"mix_bwd_conv_{i}")
        items = [(("w_gate", i), gw_gate), (("w_up", i), gw_up)]
        dqkv, *got = _attn_bwd(sv["qkv"], d_attn, sv["rsave"], tri_after, tri_before, f"attn_bwd_{i}", sending(items))
        arrive(items, got)
        dyc, dg_ln, db_ln = _conv_bwd_ln(sv["yc"], d_conv, conv_ln_g[i:i + 1], conv_ln_b[i:i + 1], f"conv_bwd_ln_{i}")
        dcc, g_taps_i, db_conv = _conv_bwd_dw(dyc, sv["cacg"], taps[i], f"conv_bwd_dw_{i}")
        items = [(("w_out", i), gw_out)]
        gw_in, *got = _grad_w_in_t(dqkv, dcc, sv["hn"], f"grad_w_in_{i}", sending(items))
        arrive(items, got)
        from_above = [(("w_in", i), gw_in)]
        items = from_above if i == 0 else []
        dh, dg_mix, *got = _mix_bwd_in(dqkv, dcc, p["w_in"], sv["h_in"], mix_norm_g[i:i + 1], dh, f"mix_bwd_in_{i}",
                                       sending(items))
        arrive(items, got)
        grads[i] = dict(taps=g_taps_i, dg_mix=dg_mix, dg_ffn=dg_ffn, dg_ln=dg_ln, db_ln=db_ln, db_conv=db_conv)
    grad_x = dh[N_META:lr][None]

    per = d // depth
    assert depth * cc <= d and d % depth == 0 and per % 128 == 0
    placed, cuts, row = [], [], 0
    for key in ("dg_mix", "dg_ffn"):
        placed += [(grads[i][key], row + i, 0, True) for i in range(depth)]
        cuts.append(((depth, d), [(slice(0, depth), row, depth, 0, d)]))
        row += depth
    placed.append((dg_final, row, 0, True))
    cuts.append(((1, d), [(slice(0, 1), row, 1, 0, d)]))
    row += 1
    for key in ("db_conv", "dg_ln", "db_ln"):
        placed += [(grads[i][key], row, i * per, True) for i in range(depth)]
        cuts.append(((depth, cc), [(slice(i, i + 1), row, 1, i * per, cc) for i in range(depth)]))
        row += 1
    placed += [(grads[i]["taps"], row, i * per, False) for i in range(depth)]
    cuts.append(((depth, CONV_PAD, cc), [(i, row, CONV_PAD, i * per, cc) for i in range(depth)]))
    row += CONV_PAD
    placed.append((dh[:N_META], row, 0, False))
    cuts.append(((N_META, d), [(slice(0, N_META), row, N_META, 0, d)]))
    row += N_META
    g_mix, g_ffn, g_final, g_cb, g_lg, g_lb, g_taps_full, g_meta_full = _all_reduce_small(
        placed, cuts, -(-row // 8) * 8, d, "reduce_small")
    csh = cc // N_DEV
    g_taps_own = lax.dynamic_slice_in_dim(g_taps_full[:, :CONV_K], me * csh, csh, axis=2)
    msh = d // N_DEV
    g_meta_own = lax.dynamic_slice_in_dim(g_meta_full, me * msh, msh, axis=1)

    row1 = lambda t: t.reshape(1, d)
    small_g = [g_meta_own, g_mix, g_taps_own, g_cb, g_lg, g_lb, g_ffn, g_final]
    small_w = [meta_tokens, mix_norm_g, conv_dw_w, conv_dw_b, conv_ln_g, conv_ln_b, ffn_norm_g, row1(final_norm_g)]
    small_m = [m_meta_tokens, m_mix_norm_g, m_conv_dw_w, m_conv_dw_b, m_conv_ln_g, m_conv_ln_b, m_ffn_norm_g,
               row1(m_final_norm_g)]
    small_v = [v_meta_tokens, v_mix_norm_g, v_conv_dw_w, v_conv_dw_b, v_conv_ln_g, v_conv_ln_b, v_ffn_norm_g,
               row1(v_final_norm_g)]
    s_delta, s_m, s_v = _adamw_small(small_g, small_w, small_m, small_v, "adamw_small")
    unrow = lambda ts: list(ts[:-1]) + [ts[-1].reshape(d)]
    small_g, s_delta, s_m, s_v = unrow(small_g), unrow(s_delta), unrow(s_m), unrow(s_v)

    big = []
    for n, w, m, v in zip(big_names, (w_in, w_out, w_gate, w_up, w_down), (m_w_in, m_w_out, m_w_gate, m_w_up, m_w_down),
                          (v_w_in, v_w_out, v_w_gate, v_w_up, v_w_down)):
        fix = _swap if transposed[n] else (lambda t: t)
        res = _adamw_shard([parts[(n, i)] for i in range(depth)], fix(w), fix(m), fix(v), f"adamw_{n}")
        big.append([fix(t) for t in res])
    b_in, b_out, b_gate, b_up, b_down = big

    def ordered(k, smalls):
        s_meta, s_mix, s_taps, s_cb, s_lg, s_lb, s_ffn, s_final = smalls
        return [s_meta, s_mix, b_in[k], s_taps, s_cb, s_lg, s_lb, b_out[k], s_ffn, b_gate[k], b_up[k], b_down[k], s_final]

    return (loss, grad_x, *ordered(0, small_g), *ordered(1, s_delta), *ordered(2, s_m), *ordered(3, s_v))
```

```python
import math

import jax
import jax.numpy as jnp
from jax import lax
from jax.experimental import pallas as pl
from jax.experimental.pallas import tpu as pltpu

F32 = jnp.float32
BF16 = jnp.bfloat16
SDS = jax.ShapeDtypeStruct

N_META = 16
N_HEADS = 8
HEAD_DIM = 64
CONV_K = 31
CONV_PAD = 32
ABLK = 128
ATT_GROUP = 4
ATT_SPLIT = 2
ATT_TOP = 2
ATT_FLIGHT = 3
EXP_ZERO_AT = 104.0
GONE = 1e30
MASKED = -1e30
FF_CHUNK = 256
FF_BLOCK = 1408
FF_ROWS = 528
EPS = 1e-6
N_DEV = 8
MESH_AXES = ("x", "y", "c")
ADAM_LR = 0.001
ADAM_B1 = 0.9
ADAM_B2 = 0.999
ADAM_EPS = 1e-08
ADAM_WD = 0.01
ADAM_STEP = 10
MIB = 1 << 20
VMEM_LIMIT_MIB = 48
FFN_FWD_VMEM_MIB = 56
MM_ROWS = 1056
EW_ROWS = 528


def _cp(mib=None):
    return pltpu.CompilerParams(vmem_limit_bytes=(mib or VMEM_LIMIT_MIB) * MIB)


def _tile(n, cap, mult):
    best = None
    for t in range(mult, min(n, cap) + 1, mult):
        if n % t == 0:
            best = t
    assert best is not None, (n, cap, mult)
    return best


def _sum8(v):
    r, c = v.shape
    return v.reshape(r // 8, 8, c).sum(axis=0)


def _sigmoid(x):
    return 1.0 / (1.0 + jnp.exp(-x))


def _ff_block(f):
    fb = _tile(f, FF_BLOCK, 128)
    return fb, [(s0, min(FF_CHUNK, fb - s0)) for s0 in range(0, fb, FF_CHUNK)]


NT = (((1,), (1,)), ((), ()))
TN = (((0,), (0,)), ((), ()))


def _rms_mm(h, g, wt, w_row0, n_cols, out_dtype, name, hn_out, rider=None):
    lp, d = h.shape
    tm = _tile(lp, MM_ROWS, 16)
    tn = _tile(math.gcd(n_cols, w_row0), 768, 128)
    off = w_row0 // tn
    ni, nj = lp // tm, n_cols // tn
    rider = rider or _Rider()

    def body(h_ref, g_ref, w_ref, *rest):
        outs, (hn_s,) = rider.split(rest, 2 if hn_out else 1, 1)
        o_ref = outs[0]
        hn_ref = outs[1] if hn_out else None
        rider.start((pl.program_id(0) == 0) & (pl.program_id(1) == 0))
        if ni > 1:
            rider.forward((pl.program_id(0) == ni - 1) & (pl.program_id(1) == 0))

        @pl.when(pl.program_id(1) == 0)
        def _():
            x = h_ref[...]
            r = lax.rsqrt(jnp.mean(x * x, axis=-1, keepdims=True) + EPS)
            hn = ((x * r) * g_ref[...]).astype(BF16)
            hn_s[...] = hn
            if hn_out:
                hn_ref[...] = hn

        o_ref[...] = lax.dot_general(hn_s[...], w_ref[...], NT, preferred_element_type=F32).astype(out_dtype)
        rider.wait((pl.program_id(0) == ni - 1) & (pl.program_id(1) == nj - 1))

    out_shape = [SDS((lp, n_cols), out_dtype)]
    out_specs = [pl.BlockSpec((tm, tn), lambda i, j: (i, j))]
    if hn_out:
        out_shape.append(SDS((lp, d), BF16))
        out_specs.append(pl.BlockSpec((tm, d), lambda i, j: (i, 0)))
    return pl.pallas_call(
        body, out_shape=out_shape + rider.out_shape, grid=(ni, nj),
        in_specs=[pl.BlockSpec((tm, d), lambda i, j: (i, 0)),
                  pl.BlockSpec((1, d), lambda i, j: (0, 0)),
                  pl.BlockSpec((tn, d), lambda i, j: (j + off, 0))] + rider.in_specs(),
        out_specs=out_specs + rider.out_specs(), scratch_shapes=[pltpu.VMEM((tm, d), BF16)] + rider.scratch(),
        name=name, compiler_params=_cp())(h, g, wt, *rider.srcs)


def _mm_nt(a, b, b_row0, n_out, out_dtype, name):
    m, k = a.shape
    tm = _tile(m, MM_ROWS, 16)
    tn = _tile(math.gcd(n_out, b_row0), 512, 128)
    off = b_row0 // tn

    def body(a_ref, b_ref, o_ref):
        o_ref[...] = lax.dot_general(a_ref[...].astype(BF16), b_ref[...], NT,
                                     preferred_element_type=F32).astype(out_dtype)

    return pl.pallas_call(
        body, out_shape=SDS((m, n_out), out_dtype), grid=(m // tm, n_out // tn),
        in_specs=[pl.BlockSpec((tm, k), lambda i, j: (i, 0)), pl.BlockSpec((tn, k), lambda i, j: (j + off, 0))],
        out_specs=pl.BlockSpec((tm, tn), lambda i, j: (i, j)), name=name, compiler_params=_cp())(a, b)


def _mix_out(attn, conv, w, res, name):
    m, ka = attn.shape
    kc = conv.shape[1]
    n = w.shape[1]
    assert ka == kc
    tm = _tile(m, MM_ROWS, 16)
    tn = _tile(n, 512, 128)

    def body(a_ref, c_ref, wa_ref, wc_ref, r_ref, o_ref):
        o_ref[...] = (r_ref[...] + jnp.dot(a_ref[...], wa_ref[...], preferred_element_type=F32)
                      + jnp.dot(c_ref[...], wc_ref[...], preferred_element_type=F32))

    return pl.pallas_call(
        body, out_shape=SDS((m, n), F32), grid=(m // tm, n // tn),
        in_specs=[pl.BlockSpec((tm, ka), lambda i, j: (i, 0)), pl.BlockSpec((tm, kc), lambda i, j: (i, 0)),
                  pl.BlockSpec((ka, tn), lambda i, j: (0, j)), pl.BlockSpec((kc, tn), lambda i, j: (1, j)),
                  pl.BlockSpec((tm, tn), lambda i, j: (i, j))],
        out_specs=pl.BlockSpec((tm, tn), lambda i, j: (i, j)), name=name, compiler_params=_cp())(attn, conv, w, w, res)


def _mm_tn(a, b, name):
    l, m = a.shape
    n = b.shape[1]
    tm = _tile(m, 1408, 128)
    tn = _tile(n, 1024, 128)
    tl = _tile(l, 1408, 128)
    nl = l // tl

    def body(a_ref, b_ref, o_ref, acc):
        @pl.when(pl.program_id(2) == 0)
        def _():
            acc[...] = jnp.zeros_like(acc)

        acc[...] += lax.dot_general(a_ref[...].astype(BF16), b_ref[...].astype(BF16), TN, preferred_element_type=F32)

        @pl.when(pl.program_id(2) == nl - 1)
        def _():
            o_ref[...] = acc[...].astype(BF16)

    return pl.pallas_call(
        body, out_shape=SDS((m, n), BF16), grid=(m // tm, n // tn, nl),
        in_specs=[pl.BlockSpec((tl, tm), lambda i, j, s: (s, i)),
                  pl.BlockSpec((tl, tn), lambda i, j, s: (s, j))],
        out_specs=pl.BlockSpec((tm, tn), lambda i, j, s: (i, j)),
        scratch_shapes=[pltpu.VMEM((tm, tn), F32)], name=name, compiler_params=_cp())(a, b)


def _grad_w_in_t(dqkv, dcc, hn, name, rider=None):
    nq, l, w = dqkv.shape
    nc = dcc.shape[0]
    d = hn.shape[1]
    assert dcc.shape[2] == w
    tn = _tile(d, 1024, 128)
    tl = _tile(l, 1408, 128)
    nl = l // tl
    nj = d // tn
    rider = rider or _Rider()

    def body(q_ref, c_ref, b_ref, *rest):
        (o_ref,), (acc,) = rider.split(rest, 1, 1)
        p, jj, s = pl.program_id(0), pl.program_id(1), pl.program_id(2)
        rider.start((p == 0) & (jj == 0) & (s == 0))

        @pl.when(s == 0)
        def _():
            acc[...] = jnp.zeros_like(acc)

        @pl.when(p < nq)
        def _():
            acc[...] += lax.dot_general(q_ref[...], b_ref[...], TN, preferred_element_type=F32)

        @pl.when(p >= nq)
        def _():
            acc[...] += lax.dot_general(c_ref[...], b_ref[...], TN, preferred_element_type=F32)

        @pl.when(s == nl - 1)
        def _():
            o_ref[...] = acc[...].astype(BF16)

        rider.wait((p == nq + nc - 1) & (jj == nj - 1) & (s == nl - 1))

    out, *arrived = pl.pallas_call(
        body, out_shape=[SDS((nq + nc, w, d), BF16)] + rider.out_shape, grid=(nq + nc, nj, nl),
        in_specs=[pl.BlockSpec((None, tl, w), lambda p, j, s: (jnp.minimum(p, nq - 1), s, 0)),
                  pl.BlockSpec((None, tl, w), lambda p, j, s: (jnp.maximum(p - nq, 0), s, 0)),
                  pl.BlockSpec((tl, tn), lambda p, j, s: (s, j))] + rider.in_specs(),
        out_specs=[pl.BlockSpec((None, w, tn), lambda p, j, s: (p, 0, j))] + rider.out_specs(),
        scratch_shapes=[pltpu.VMEM((w, tn), F32)] + rider.scratch(), name=name,
        compiler_params=_cp())(dqkv, dcc, hn, *rider.srcs)
    return [out.reshape((nq + nc) * w, d)] + arrived


def _rms_bwd_tail(acc, h_ref, g_ref, dres_ref, o_ref, dg_ref):
    x = h_ref[...]
    r = lax.rsqrt(jnp.mean(x * x, axis=-1, keepdims=True) + EPS)
    xr = x * r
    dyv = acc[...]
    gy = dyv * g_ref[...]
    o_ref[...] = dres_ref[...] + r * (gy - xr * jnp.mean(gy * xr, axis=-1, keepdims=True))
    dg_ref[...] += _sum8(dyv * xr)


def _mix_bwd_in(dqkv, dcc, wt, h, g, dres, name, rider=None):
    nq, lp, w = dqkv.shape
    nc = dcc.shape[0]
    d = h.shape[1]
    tm = _tile(lp, FF_ROWS, 16)
    ni = lp // tm
    rider = rider or _Rider()

    def body(q_ref, c_ref, w_ref, h_ref, g_ref, dres_ref, *rest):
        (o_ref, dg_ref), (acc,) = rider.split(rest, 2, 1)
        i = pl.program_id(0)
        rider.start(i == 0)

        @pl.when(i == 0)
        def _():
            dg_ref[...] = jnp.zeros_like(dg_ref)

        pieces = [q_ref[p] for p in range(nq)] + [c_ref[p] for p in range(nc)]
        total = None
        for p, piece in enumerate(pieces):
            part = jnp.dot(piece, w_ref[p * w:(p + 1) * w, :], preferred_element_type=F32)
            total = part if total is None else total + part
        acc[...] = total
        _rms_bwd_tail(acc, h_ref, g_ref, dres_ref, o_ref, dg_ref)
        rider.wait(i == ni - 1)

    return pl.pallas_call(
        body, out_shape=[SDS((lp, d), F32), SDS((8, d), F32)] + rider.out_shape, grid=(ni,),
        in_specs=[pl.BlockSpec((nq, tm, w), lambda i: (0, i, 0)),
                  pl.BlockSpec((nc, tm, w), lambda i: (0, i, 0)),
                  pl.BlockSpec(((nq + nc) * w, d), lambda i: (0, 0)),
                  pl.BlockSpec((tm, d), lambda i: (i, 0)),
                  pl.BlockSpec((1, d), lambda i: (0, 0)),
                  pl.BlockSpec((tm, d), lambda i: (i, 0))] + rider.in_specs(),
        out_specs=[pl.BlockSpec((tm, d), lambda i: (i, 0)), pl.BlockSpec((8, d), lambda i: (0, 0))]
        + rider.out_specs(),
        scratch_shapes=[pltpu.VMEM((tm, d), F32)] + rider.scratch(), name=name,
        compiler_params=_cp())(dqkv, dcc, wt, h, g, dres, *rider.srcs)


def _ffn_bwd_in(d_g, d_u, wg_t, wu_t, h, g, dres, name, rider=None):
    lp, f = d_g.shape
    d = h.shape[1]
    tm = _tile(lp, FF_ROWS, 16)
    tk = _tile(f, FF_BLOCK, 128)
    nk = f // tk
    ni = lp // tm
    rider = rider or _Rider()

    def body(dg_in, du_in, wg_ref, wu_ref, h_ref, g_ref, dres_ref, *rest):
        (o_ref, dg_ref), (acc,) = rider.split(rest, 2, 1)
        i, kk = pl.program_id(0), pl.program_id(1)
        rider.start((i == 0) & (kk == 0))

        @pl.when(kk == 0)
        def _():
            acc[...] = jnp.zeros_like(acc)

        @pl.when((kk == 0) & (i == 0))
        def _():
            dg_ref[...] = jnp.zeros_like(dg_ref)

        acc[...] += (jnp.dot(dg_in[...], wg_ref[...], preferred_element_type=F32)
                     + jnp.dot(du_in[...], wu_ref[...], preferred_element_type=F32))

        @pl.when(kk == nk - 1)
        def _():
            _rms_bwd_tail(acc, h_ref, g_ref, dres_ref, o_ref, dg_ref)

        rider.wait((i == ni - 1) & (kk == nk - 1))

    return pl.pallas_call(
        body, out_shape=[SDS((lp, d), F32), SDS((8, d), F32)] + rider.out_shape, grid=(ni, nk),
        in_specs=[pl.BlockSpec((tm, tk), lambda i, j: (i, j)), pl.BlockSpec((tm, tk), lambda i, j: (i, j)),
                  pl.BlockSpec((tk, d), lambda i, j: (j, 0)), pl.BlockSpec((tk, d), lambda i, j: (j, 0)),
                  pl.BlockSpec((tm, d), lambda i, j: (i, 0)),
                  pl.BlockSpec((1, d), lambda i, j: (0, 0)),
                  pl.BlockSpec((tm, d), lambda i, j: (i, 0))] + rider.in_specs(),
        out_specs=[pl.BlockSpec((tm, d), lambda i, j: (i, 0)), pl.BlockSpec((8, d), lambda i, j: (0, 0))]
        + rider.out_specs(),
        scratch_shapes=[pltpu.VMEM((tm, d), F32)] + rider.scratch(), name=name,
        compiler_params=_cp())(d_g, d_u, wg_t, wu_t, h, g, dres, *rider.srcs)


def _ffn_fwd(h, g, wg_t, wu_t, w_d, name, rider=None):
    lp, d = h.shape
    f = w_d.shape[0]
    tm = _tile(lp, FF_ROWS, 16)
    fb, subs = _ff_block(f)
    nc = f // fb
    ni = lp // tm
    rider = rider or _Rider()

    def body(h_ref, g_ref, wg_ref, wu_ref, wd_ref, *rest):
        (o_ref, hn_ref, act_ref, gate_ref, up_ref), (hn_s, acc) = rider.split(rest, 5, 2)
        c = pl.program_id(1)
        rider.start((pl.program_id(0) == 0) & (c == 0))
        if ni > 1:
            rider.forward((pl.program_id(0) == ni - 1) & (c == 0))

        @pl.when(c == 0)
        def _():
            x = h_ref[...]
            r = lax.rsqrt(jnp.mean(x * x, axis=-1, keepdims=True) + EPS)
            hn = ((x * r) * g_ref[...]).astype(BF16)
            hn_s[...] = hn
            hn_ref[...] = hn
            acc[...] = jnp.zeros_like(acc)

        total = None
        for s0, sw in subs:
            gg = lax.dot_general(hn_s[...], wg_ref[s0:s0 + sw, :], NT, preferred_element_type=F32)
            uu = lax.dot_general(hn_s[...], wu_ref[s0:s0 + sw, :], NT, preferred_element_type=F32)
            gate_ref[:, s0:s0 + sw] = gg
            up_ref[:, s0:s0 + sw] = uu
            act = ((gg * _sigmoid(gg)) * uu).astype(BF16)
            act_ref[:, s0:s0 + sw] = act
            part = jnp.dot(act, wd_ref[s0:s0 + sw, :], preferred_element_type=F32)
            total = part if total is None else total + part
        acc[...] += total

        @pl.when(c == nc - 1)
        def _():
            o_ref[...] = h_ref[...] + acc[...]

        rider.wait((pl.program_id(0) == ni - 1) & (c == nc - 1))

    chunk = pl.BlockSpec((fb, d), lambda i, j: (j, 0))
    wide = pl.BlockSpec((tm, fb), lambda i, j: (i, j))
    return pl.pallas_call(
        body, out_shape=[SDS((lp, d), F32), SDS((lp, d), BF16), SDS((lp, f), BF16), SDS((lp, f), F32),
                         SDS((lp, f), F32)] + rider.out_shape, grid=(ni, nc),
        in_specs=[pl.BlockSpec((tm, d), lambda i, j: (i, 0)), pl.BlockSpec((1, d), lambda i, j: (0, 0)),
                  chunk, chunk, chunk] + rider.in_specs(),
        out_specs=[pl.BlockSpec((tm, d), lambda i, j: (i, 0)), pl.BlockSpec((tm, d), lambda i, j: (i, 0)),
                   wide, wide, wide] + rider.out_specs(),
        scratch_shapes=[pltpu.VMEM((tm, d), BF16), pltpu.VMEM((tm, d), F32)] + rider.scratch(),
        name=name, compiler_params=_cp(FFN_FWD_VMEM_MIB))(h, g, wg_t, wu_t, w_d, *rider.srcs)


def _ffn_bwd_act(dh, gate, up, w_d, name, rider=None):
    lp, d = dh.shape
    f = w_d.shape[0]
    tm = _tile(lp, FF_ROWS, 16)
    fb, subs = _ff_block(f)
    nc = f // fb
    ni = lp // tm
    rider = rider or _Rider()

    def body(dh_ref, gate_ref, up_ref, wd_ref, *rest):
        (dg_out, du_out), (dh_s,) = rider.split(rest, 2, 1)
        rider.start((pl.program_id(0) == 0) & (pl.program_id(1) == 0))

        @pl.when(pl.program_id(1) == 0)
        def _():
            dh_s[...] = dh_ref[...].astype(BF16)

        for s0, sw in subs:
            gg = gate_ref[:, s0:s0 + sw]
            uu = up_ref[:, s0:s0 + sw]
            dact = lax.dot_general(dh_s[...], wd_ref[s0:s0 + sw, :], NT, preferred_element_type=F32)
            s = _sigmoid(gg)
            dg_out[:, s0:s0 + sw] = (dact * uu * (s * (1.0 + gg * (1.0 - s)))).astype(BF16)
            du_out[:, s0:s0 + sw] = (dact * (gg * s)).astype(BF16)
        rider.wait((pl.program_id(0) == ni - 1) & (pl.program_id(1) == nc - 1))

    wide = pl.BlockSpec((tm, fb), lambda i, j: (i, j))
    return pl.pallas_call(
        body, out_shape=[SDS((lp, f), BF16), SDS((lp, f), BF16)] + rider.out_shape, grid=(ni, nc),
        in_specs=[pl.BlockSpec((tm, d), lambda i, j: (i, 0)), wide, wide,
                  pl.BlockSpec((fb, d), lambda i, j: (j, 0))] + rider.in_specs(),
        out_specs=[wide, wide] + rider.out_specs(), scratch_shapes=[pltpu.VMEM((tm, d), BF16)] + rider.scratch(),
        name=name, compiler_params=_cp())(dh, gate, up, w_d, *rider.srcs)


def _loss_head(h, tpad, g, n_real, name):
    lp, d = h.shape
    tm = _tile(lp, EW_ROWS, 16)

    def body(h_ref, t_ref, g_ref, dh_ref, dg_ref, loss_ref):
        i = pl.program_id(0)

        @pl.when(i == 0)
        def _():
            dg_ref[...] = jnp.zeros_like(dg_ref)
            loss_ref[...] = jnp.zeros_like(loss_ref)

        x = h_ref[...]
        r = lax.rsqrt(jnp.mean(x * x, axis=-1, keepdims=True) + EPS)
        xr = x * r
        y = xr * g_ref[...]
        row = i * tm + lax.broadcasted_iota(jnp.int32, (tm, d), 0)
        valid = (row >= N_META) & (row < N_META + n_real)
        diff = jnp.where(valid, y - t_ref[...], 0.0)
        loss_ref[...] += jnp.sum(diff * diff) * (0.5 / d)
        dy = diff * (1.0 / d)
        gy = dy * g_ref[...]
        dh_ref[...] = r * (gy - xr * jnp.mean(gy * xr, axis=-1, keepdims=True))
        dg_ref[...] += _sum8(dy * xr)

    return pl.pallas_call(
        body, out_shape=[SDS((lp, d), F32), SDS((8, d), F32), SDS((8, 128), F32)], grid=(lp // tm,),
        in_specs=[pl.BlockSpec((tm, d), lambda i: (i, 0)),
                  pl.BlockSpec((tm, d), lambda i: (i, 0)),
                  pl.BlockSpec((1, d), lambda i: (0, 0))],
        out_specs=[pl.BlockSpec((tm, d), lambda i: (i, 0)),
                   pl.BlockSpec((8, d), lambda i: (0, 0)),
                   pl.BlockSpec((8, 128), lambda i: (0, 0))],
        name=name, compiler_params=_cp())(h, tpad, g)


def _tri_consts():
    j = lax.broadcasted_iota(jnp.int32, (ABLK, ABLK), 0)
    s = lax.broadcasted_iota(jnp.int32, (ABLK, ABLK), 1)
    after = (j >= s).astype(BF16)
    before = (j < s).astype(BF16)
    ones = jnp.ones((ABLK, ABLK), BF16)
    two = lambda t: jnp.concatenate([t, t], axis=0)
    return (two(jnp.concatenate([after, ones], axis=1)),
            two(after),
            two(jnp.concatenate([before, ones], axis=1)))


def _softplus(z):
    neg_abs = lax.bitcast_convert_type(lax.bitcast_convert_type(z, jnp.uint32) | jnp.uint32(0x80000000), F32)
    return jnp.log(1.0 + jnp.exp(neg_abs)) + jnp.maximum(z, 0.0)


def _split_hi_lo(m):
    hi = m.astype(BF16)
    lo = (m - hi.astype(F32)).astype(BF16)
    return jnp.concatenate([hi, lo], axis=1)


def _head_halves(t2, in_a):
    zero = jnp.zeros_like(t2)
    return jnp.concatenate([jnp.where(in_a, t2, zero), jnp.where(in_a, zero, t2)], axis=0)


def _stack_blocks(t, nblk, in_a):
    return jnp.concatenate([_head_halves(t[u * ABLK:(u + 1) * ABLK], in_a) for u in range(nblk)], axis=0)


def _interleave(*gens):
    alive = list(gens)
    while alive:
        for g in list(alive):
            if next(g, alive) is alive:
                alive.remove(g)


def _attn_scale():
    scale = 1.0 / math.sqrt(HEAD_DIM)
    assert math.frexp(scale)[0] == 0.5, "a power of two, so that scaling q in bf16 is exact"
    return scale


def _pow2_below(n):
    assert n & (n - 1) == 0
    return [p for p in (64, 32, 16, 8, 4, 2, 1) if p < n]


class _Rider:
    def __init__(self, jobs=(), srcs=(), out_shape=()):
        self.jobs, self.srcs, self.out_shape = list(jobs), list(srcs), list(out_shape)
        self.any = [pl.BlockSpec(memory_space=pl.ANY)]

    def split(self, rest, n_out, n_scratch):
        ni, no = len(self.srcs), len(self.out_shape)
        self.ins, outs = rest[:ni], rest[ni:ni + n_out]
        self.outs = rest[ni + n_out:ni + n_out + no]
        scratch = rest[ni + n_out + no:ni + n_out + no + n_scratch]
        self.sems = rest[ni + n_out + no + n_scratch:]
        return outs, scratch

    def start(self, first):
        self.forwarded = False
        if self.jobs:
            @pl.when(first)
            def _():
                _exchange_start(self.jobs, self.ins, self.outs, *self.sems)

    def forward(self, late):
        self.forwarded = True
        if self.jobs:
            @pl.when(late)
            def _():
                _exchange_forward(self.jobs, self.ins, self.outs, *self.sems)

    def wait(self, last):
        if self.jobs:
            @pl.when(last)
            def _():
                if not self.forwarded:
                    _exchange_forward(self.jobs, self.ins, self.outs, *self.sems)
                _exchange_wait(self.jobs, self.ins, self.outs, *self.sems)

    def in_specs(self):
        return self.any * len(self.srcs)

    def out_specs(self):
        return self.any * len(self.out_shape)

    def scratch(self):
        return _exchange_sems(len(self.jobs)) if self.jobs else []


def _attn_fwd(qkv, tri_fwd, name, rider=None):
    lp = qkv.shape[0]
    n_pairs = (N_HEADS * HEAD_DIM) // 128
    nb = lp // ABLK
    assert nb <= 128 and 2 * HEAD_DIM == 128
    scale = _attn_scale()
    nt = (((1,), (1,)), ((), ()))
    rider = rider or _Rider()

    def body(q_ref, k_ref, v_ref, tri_ref, *rest):
        (o_ref, rs_ref), (r_s, acc_s, rs_s) = rider.split(rest, 2, 3)
        rider.start(pl.program_id(0) == 0)
        lane = lax.broadcasted_iota(jnp.int32, (ABLK, 128), 1)
        row = lax.broadcasted_iota(jnp.int32, (ABLK, 128), 0)
        in_a = lane < HEAD_DIM
        causal = lane < row

        def begin(st, i):
            r_s[st] = jnp.zeros(r_s.shape[1:], F32)
            acc_s[st] = jnp.zeros(acc_s.shape[1:], F32)
            rs_s[st] = jnp.full(rs_s.shape[1:], GONE, F32)
            return q_ref[pl.ds(pl.multiple_of(i * ABLK, ABLK), ABLK), :] * scale

        def live(st):
            least = jnp.min(jnp.minimum(r_s[st, 0], r_s[st, 1]), axis=0, keepdims=True)
            return (least[0, 0] < EXP_ZERO_AT).astype(jnp.int32)

        def step(*args):
            _interleave(stages(*args))

        def stages(st, q2, kb0, nblk, diag):
            k0 = pl.multiple_of(kb0 * ABLK, ABLK)
            kbd = _stack_blocks(k_ref[pl.ds(k0, nblk * ABLK), :], nblk, in_a)
            vbd = _stack_blocks(v_ref[pl.ds(k0, nblk * ABLK), :], nblk, in_a)
            z = lax.dot_general(q2, kbd, nt, preferred_element_type=F32)
            ncol = 2 * nblk
            zt = [z[:, c * 128:(c + 1) * 128] for c in range(ncol)]
            if diag:
                zt = [jnp.where(causal, t, MASKED) if c >= ncol - 2 else t for c, t in enumerate(zt)]
            yield
            bounds = [ncol * j // ATT_SPLIT for j in range(ATT_SPLIT + 1)]
            ce = [None] * ncol
            for c0, c1 in reversed([b for b in zip(bounds[:-1], bounds[1:]) if b[0] < b[1]]):
                parts = [_split_hi_lo(_softplus(zt[c])) for c in range(c0, c1)]
                got = jnp.dot(jnp.concatenate(parts, axis=0), tri_ref[...], preferred_element_type=F32)
                for c in range(c0, c1):
                    ce[c] = got[(c - c0) * 128:(c - c0 + 1) * 128]
                yield
            rr = [r_s[st, 0], r_s[st, 1]]
            rsv = [rs_s[st, :, :128], rs_s[st, :, 128:]]
            ws = [None] * ncol
            for u in reversed(range(nblk)):
                for hh in range(2):
                    c = 2 * u + hh
                    ws[c] = jnp.exp(zt[c] - ce[c][:, :128] - rr[hh]).astype(BF16)
                    rsv[hh] = jnp.where(lane == kb0 + u, rr[hh], rsv[hh])
                    rr[hh] = rr[hh] + ce[c][:, 128:]
            acc_s[st] += jnp.dot(jnp.concatenate(ws, axis=1), vbd, preferred_element_type=F32)
            r_s[st, 0] = rr[0]
            r_s[st, 1] = rr[1]
            rs_s[st, :, :128] = rsv[0]
            rs_s[st, :, 128:] = rsv[1]

        def finish(st, q2, i, n_top):
            i_low = i - n_top
            n_grp = i_low // ATT_GROUP

            def more(c):
                return (c[0] < n_grp) & (c[1] > 0)

            def inner(c):
                step(st, q2, i_low - ATT_GROUP * (c[0] + 1), ATT_GROUP, False)
                return c[0] + 1, live(st)

            _, alive = lax.while_loop(more, inner, (jnp.int32(0), live(st)))
            rem = i_low - ATT_GROUP * n_grp
            for p in _pow2_below(ATT_GROUP):
                def last_steps(p=p):
                    step(st, q2, rem & (p - 1), p, False)
                    return live(st)

                alive = lax.cond(((rem & p) != 0) & (alive > 0), last_steps, lambda alive=alive: alive)

            q0 = pl.multiple_of(i * ABLK, ABLK)
            o_ref[pl.ds(q0, ABLK), :] = acc_s[st].astype(BF16)
            rs_ref[pl.ds(q0, ABLK), :] = rs_s[st]

        def single(i):
            q2 = begin(0, i)
            n_top = jnp.minimum(i, ATT_TOP)
            for t in range(ATT_TOP + 1):
                @pl.when(n_top == t)
                def _():
                    step(0, q2, i - t, t + 1, True)

            finish(0, q2, i, n_top)

        def together(i):
            qs = [begin(st, i + st) for st in range(ATT_FLIGHT)]
            _interleave(*[stages(st, qs[st], i + st - ATT_TOP, ATT_TOP + 1, True) for st in range(ATT_FLIGHT)])
            for st in range(ATT_FLIGHT):
                finish(st, qs[st], i + st, ATT_TOP)

        n_head = min(ATT_TOP, nb)
        n_group = (nb - n_head) // ATT_FLIGHT
        n_single = nb - ATT_FLIGHT * n_group

        def singles(k, carry):
            single(jnp.where(k < n_head, k, nb - n_single + k))
            return carry

        def groups(j, carry):
            together(n_head + ATT_FLIGHT * j)
            return carry

        lax.fori_loop(0, n_single, singles, 0)
        lax.fori_loop(0, n_group, groups, 0)
        rider.wait(pl.program_id(0) == n_pairs - 1)

    col = lambda o: (lambda p: (0, p + o))
    return pl.pallas_call(
        body, out_shape=[SDS((lp, n_pairs * 128), BF16), SDS((lp, n_pairs * 256), F32)] + rider.out_shape,
        grid=(n_pairs,),
        in_specs=[pl.BlockSpec((lp, 128), col(0)), pl.BlockSpec((lp, 128), col(n_pairs)),
                  pl.BlockSpec((lp, 128), col(2 * n_pairs)), pl.BlockSpec((256, 256), lambda p: (0, 0))]
        + rider.in_specs(),
        out_specs=[pl.BlockSpec((lp, 128), col(0)), pl.BlockSpec((lp, 256), col(0))] + rider.out_specs(),
        scratch_shapes=[pltpu.VMEM((ATT_FLIGHT, 2, ABLK, 128), F32), pltpu.VMEM((ATT_FLIGHT, ABLK, 128), F32),
                        pltpu.VMEM((ATT_FLIGHT, ABLK, 256), F32)] + rider.scratch(),
        name=name, compiler_params=_cp())(qkv, qkv, qkv, tri_fwd, *rider.srcs)


def _attn_bwd(qkv, d_out, rsave, tri_after, tri_before, name, rider=None):
    lp = qkv.shape[0]
    n_pairs = (N_HEADS * HEAD_DIM) // 128
    nb = lp // ABLK
    scale = _attn_scale()
    nt = (((1,), (1,)), ((), ()))
    tn = (((0,), (0,)), ((), ()))
    rider = rider or _Rider()

    def body(q_ref, k_ref, v_ref, do_ref, rs_ref, ta_ref, tb_ref, *rest):
        (o_ref,), (dk_s, dv_s, dq_s, pc_s) = rider.split(rest, 1, 4)
        rider.start(pl.program_id(0) == 0)
        lane = lax.broadcasted_iota(jnp.int32, (ABLK, 128), 1)
        row = lax.broadcasted_iota(jnp.int32, (ABLK, 128), 0)
        in_a = lane < HEAD_DIM
        causal = lane < row
        dk_s[...] = jnp.zeros_like(dk_s)
        dv_s[...] = jnp.zeros_like(dv_s)

        def begin(st, i):
            q0 = pl.multiple_of(i * ABLK, ABLK)
            q2 = q_ref[pl.ds(q0, ABLK), :] * scale
            do2 = do_ref[pl.ds(q0, ABLK), :]
            dq_s[st] = jnp.zeros(dq_s.shape[1:], F32)
            pc_s[st] = jnp.zeros(pc_s.shape[1:], F32)
            return dict(q0=q0, q2=q2, do2=do2, q_st=_head_halves(q2, in_a), do_st=_head_halves(do2, in_a))

        def step(*args):
            _interleave(stages(*args))

        def stages(st, blk, kb0, nblk, diag):
            q0, q2, do2 = blk["q0"], blk["q2"], blk["do2"]
            k0 = pl.multiple_of(kb0 * ABLK, ABLK)
            kbd = _stack_blocks(k_ref[pl.ds(k0, nblk * ABLK), :], nblk, in_a)
            vbd = _stack_blocks(v_ref[pl.ds(k0, nblk * ABLK), :], nblk, in_a)
            z = lax.dot_general(q2, kbd, nt, preferred_element_type=F32)
            dw = lax.dot_general(do2, vbd, nt, preferred_element_type=F32)
            ncol = 2 * nblk
            zt = [z[:, c * 128:(c + 1) * 128] for c in range(ncol)]
            if diag:
                zt = [jnp.where(causal, t, MASKED) if c >= ncol - 2 else t for c, t in enumerate(zt)]
            bounds = [ncol * j // ATT_SPLIT for j in range(ATT_SPLIT + 1)]
            batches = [range(c0, c1) for c0, c1 in zip(bounds[:-1], bounds[1:]) if c0 < c1]
            sps, ex, ws, dls, pe = [None] * ncol, [None] * ncol, [None] * ncol, [None] * ncol, [None] * ncol

            def mass(cols):
                for c in cols:
                    sps[c] = _softplus(zt[c])
                got = jnp.dot(jnp.concatenate([_split_hi_lo(sps[c]) for c in cols], axis=0), ta_ref[...],
                              preferred_element_type=F32)
                for j, c in enumerate(cols):
                    ex[c] = got[j * 128:(j + 1) * 128]

            def weights(cols):
                for c in cols:
                    u, hh = c // 2, c % 2
                    r_saved = jnp.sum(jnp.where(lane == kb0 + u, rs_ref[pl.ds(q0, ABLK), hh * 128:(hh + 1) * 128],
                                                0.0), axis=1, keepdims=True)
                    w = jnp.exp(zt[c] - ex[c] - r_saved)
                    ws[c] = w.astype(BF16)
                    dls[c] = dw[:, c * 128:(c + 1) * 128] * w
                got = jnp.dot(jnp.concatenate([_split_hi_lo(dls[c]) for c in cols], axis=0), tb_ref[...],
                              preferred_element_type=F32)
                for j, c in enumerate(cols):
                    pe[c] = got[j * 128:(j + 1) * 128]

            yield
            mass(batches[0])
            yield
            for j in range(len(batches)):
                if j + 1 < len(batches):
                    mass(batches[j + 1])
                    yield
                weights(batches[j])
                yield
            pc = [pc_s[st, 0], pc_s[st, 1]]
            dzs = []
            for c in range(ncol):
                hh = c % 2
                one_minus_beta = jnp.exp(-sps[c])
                dz = dls[c] * one_minus_beta - (pe[c][:, :128] + pc[hh]) * (1.0 - one_minus_beta)
                pc[hh] = pc[hh] + pe[c][:, 128:]
                dzs.append(dz.astype(BF16))
            pc_s[st, 0] = pc[0]
            pc_s[st, 1] = pc[1]
            dq_s[st] += jnp.dot(jnp.concatenate(dzs, axis=1), kbd, preferred_element_type=F32)
            by_head = lambda ts: jnp.concatenate([jnp.concatenate(ts[0::2], axis=1), jnp.concatenate(ts[1::2], axis=1)],
                                                 axis=0)
            rows = pl.ds(k0, nblk * ABLK)
            dk_s[rows, :] += lax.dot_general(by_head(dzs), blk["q_st"], tn, preferred_element_type=F32)
            dv_s[rows, :] += lax.dot_general(by_head(ws), blk["do_st"], tn, preferred_element_type=F32)

        def below(st, blk, i, n_top):
            gone = jnp.min(rs_ref[pl.ds(blk["q0"], ABLK), :], axis=0, keepdims=True) >= EXP_ZERO_AT
            lane1 = lax.broadcasted_iota(jnp.int32, (1, 128), 1)
            first = jnp.sum(jnp.where(gone[:, :128] & gone[:, 128:] & (lane1 < i), 1.0, 0.0)).astype(jnp.int32)
            i_low = i - n_top
            n_grp = i_low // ATT_GROUP
            rem = i_low - ATT_GROUP * n_grp
            for p in reversed(_pow2_below(ATT_GROUP)):
                @pl.when(((rem & p) != 0) & ((rem & (p - 1)) + p > first))
                def _():
                    step(st, blk, rem & (p - 1), p, False)

            def inner(g, c2):
                step(st, blk, rem + ATT_GROUP * g, ATT_GROUP, False)
                return c2

            lax.fori_loop(jnp.maximum(first - rem, 0) // ATT_GROUP, n_grp, inner, 0)

        def done(st, blk):
            o_ref[0, pl.ds(blk["q0"], ABLK), :] = (dq_s[st] * scale).astype(BF16)

        def single(i):
            blk = begin(0, i)
            n_top = jnp.minimum(i, ATT_TOP)
            below(0, blk, i, n_top)
            for t in range(ATT_TOP + 1):
                @pl.when(n_top == t)
                def _():
                    step(0, blk, i - t, t + 1, True)

            done(0, blk)

        def together(i):
            blks = [begin(st, i + st) for st in range(ATT_FLIGHT)]
            for st in range(ATT_FLIGHT):
                below(st, blks[st], i + st, ATT_TOP)
            _interleave(*[stages(st, blks[st], i + st - ATT_TOP, ATT_TOP + 1, True) for st in range(ATT_FLIGHT)])
            for st in range(ATT_FLIGHT):
                done(st, blks[st])

        n_head = min(ATT_TOP, nb)
        n_group = (nb - n_head) // ATT_FLIGHT
        n_single = nb - ATT_FLIGHT * n_group

        def singles(k, carry):
            single(jnp.where(k < n_head, k, nb - n_single + k))
            return carry

        def groups(j, carry):
            together(n_head + ATT_FLIGHT * j)
            return carry

        lax.fori_loop(0, n_single, singles, 0)
        lax.fori_loop(0, n_group, groups, 0)
        o_ref[1] = dk_s[...].astype(BF16)
        o_ref[2] = dv_s[...].astype(BF16)
        rider.wait(pl.program_id(0) == n_pairs - 1)

    col = lambda o: (lambda p: (0, p + o))
    return pl.pallas_call(
        body, out_shape=[SDS((3, lp, n_pairs * 128), BF16)] + rider.out_shape, grid=(n_pairs,),
        in_specs=[pl.BlockSpec((lp, 128), col(0)), pl.BlockSpec((lp, 128), col(n_pairs)),
                  pl.BlockSpec((lp, 128), col(2 * n_pairs)), pl.BlockSpec((lp, 128), col(0)),
                  pl.BlockSpec((lp, 256), col(0)),
                  pl.BlockSpec((256, 128), lambda p: (0, 0)), pl.BlockSpec((256, 256), lambda p: (0, 0))]
        + rider.in_specs(),
        out_specs=[pl.BlockSpec((3, lp, 128), lambda p: (0, 0, p))] + rider.out_specs(),
        scratch_shapes=[pltpu.VMEM((lp, 128), F32), pltpu.VMEM((lp, 128), F32),
                        pltpu.VMEM((ATT_FLIGHT, ABLK, 128), F32), pltpu.VMEM((ATT_FLIGHT, 2, ABLK, 128), F32)]
        + rider.scratch(),
        name=name, compiler_params=_cp())(qkv, qkv, qkv, d_out, rsave, tri_after, tri_before, *rider.srcs)


def _conv_fwd_dw(cacg, w, b, name, rider=None):
    lp = cacg.shape[0]
    c = cacg.shape[1] // 2
    ncb = c // 128
    nchunk = lp // ABLK
    off = CONV_PAD - (CONV_K - 1)
    rider = rider or _Rider()

    def body(a_ref, g_ref, w_ref, b_ref, *rest):
        (y_ref,), (upad,) = rider.split(rest, 1, 1)
        rider.start(pl.program_id(0) == 0)
        upad[0:CONV_PAD, :] = jnp.zeros((CONV_PAD, 128), F32)

        def fill(ch, carry):
            base = pl.multiple_of(ch * ABLK, ABLK)
            upad[pl.ds(base + CONV_PAD, ABLK), :] = a_ref[pl.ds(base, ABLK), :] * _sigmoid(g_ref[pl.ds(base, ABLK), :])
            return carry

        lax.fori_loop(0, nchunk, fill, 0)

        def comp(ch, carry):
            base = pl.multiple_of(ch * ABLK, ABLK)
            acc = jnp.zeros((ABLK, 128), F32)
            for k in range(CONV_K):
                acc = acc + upad[pl.ds(base + (off + k), ABLK), :] * w_ref[k:k + 1, :]
            y_ref[pl.ds(base, ABLK), :] = acc + b_ref[...]
            return carry

        lax.fori_loop(0, nchunk, comp, 0)
        rider.wait(pl.program_id(0) == ncb - 1)

    return pl.pallas_call(
        body, out_shape=[SDS((lp, c), F32)] + rider.out_shape, grid=(ncb,),
        in_specs=[pl.BlockSpec((lp, 128), lambda j: (0, j)), pl.BlockSpec((lp, 128), lambda j: (0, j + ncb)),
                  pl.BlockSpec((CONV_PAD, 128), lambda j: (0, j)), pl.BlockSpec((1, 128), lambda j: (0, j))]
        + rider.in_specs(),
        out_specs=[pl.BlockSpec((lp, 128), lambda j: (0, j))] + rider.out_specs(),
        scratch_shapes=[pltpu.VMEM((lp + CONV_PAD, 128), F32)] + rider.scratch(), name=name,
        compiler_params=_cp())(cacg, cacg, w, b, *rider.srcs)


def _ln_parts(x, g, b):
    mu = jnp.mean(x, axis=-1, keepdims=True)
    xc = x - mu
    rstd = lax.rsqrt(jnp.mean(xc * xc, axis=-1, keepdims=True) + EPS)
    xh = xc * rstd
    return xh, rstd, xh * g + b


def _conv_fwd_ln(yc, g, b, name):
    lp, c = yc.shape
    tm = _tile(lp, EW_ROWS, 16)

    def body(y_ref, g_ref, b_ref, o_ref):
        _, _, ln = _ln_parts(y_ref[...], g_ref[...], b_ref[...])
        o_ref[...] = (ln * _sigmoid(ln)).astype(BF16)

    return pl.pallas_call(
        body, out_shape=SDS((lp, c), BF16), grid=(lp // tm,),
        in_specs=[pl.BlockSpec((tm, c), lambda i: (i, 0)), pl.BlockSpec((1, c), lambda i: (0, 0)),
                  pl.BlockSpec((1, c), lambda i: (0, 0))],
        out_specs=pl.BlockSpec((tm, c), lambda i: (i, 0)), name=name, compiler_params=_cp())(yc, g, b)


def _conv_bwd_ln(yc, dout, g, b, name):
    lp, c = yc.shape
    tm = _tile(lp, EW_ROWS, 16)

    def body(y_ref, d_ref, g_ref, b_ref, o_ref, dg_ref, db_ref):
        @pl.when(pl.program_id(0) == 0)
        def _():
            dg_ref[...] = jnp.zeros_like(dg_ref)
            db_ref[...] = jnp.zeros_like(db_ref)

        xh, rstd, ln = _ln_parts(y_ref[...], g_ref[...], b_ref[...])
        s = _sigmoid(ln)
        dln = d_ref[...] * (s * (1.0 + ln * (1.0 - s)))
        dg_ref[...] += _sum8(dln * xh)
        db_ref[...] += _sum8(dln)
        dxh = dln * g_ref[...]
        o_ref[...] = rstd * (dxh - jnp.mean(dxh, axis=-1, keepdims=True)
                             - xh * jnp.mean(dxh * xh, axis=-1, keepdims=True))

    return pl.pallas_call(
        body, out_shape=[SDS((lp, c), F32), SDS((8, c), F32), SDS((8, c), F32)], grid=(lp // tm,),
        in_specs=[pl.BlockSpec((tm, c), lambda i: (i, 0)), pl.BlockSpec((tm, c), lambda i: (i, 0)),
                  pl.BlockSpec((1, c), lambda i: (0, 0)), pl.BlockSpec((1, c), lambda i: (0, 0))],
        out_specs=[pl.BlockSpec((tm, c), lambda i: (i, 0)), pl.BlockSpec((8, c), lambda i: (0, 0)),
                   pl.BlockSpec((8, c), lambda i: (0, 0))],
        name=name, compiler_params=_cp())(yc, dout, g, b)


def _conv_bwd_dw(dyc, cacg, w, name):
    lp, c = dyc.shape
    ncb = c // 128
    nchunk = lp // ABLK
    off = CONV_PAD - (CONV_K - 1)

    def body(dy_ref, a_ref, g_ref, w_ref, dcc_ref, dw_ref, db_ref, upad, dypad, dwacc):
        upad[0:CONV_PAD, :] = jnp.zeros((CONV_PAD, 128), F32)
        dypad[lp:lp + CONV_PAD, :] = jnp.zeros((CONV_PAD, 128), F32)
        dwacc[...] = jnp.zeros_like(dwacc)
        db_ref[...] = jnp.zeros_like(db_ref)

        def fill(ch, carry):
            base = pl.multiple_of(ch * ABLK, ABLK)
            upad[pl.ds(base + CONV_PAD, ABLK), :] = a_ref[pl.ds(base, ABLK), :] * _sigmoid(g_ref[pl.ds(base, ABLK), :])
            dypad[pl.ds(base, ABLK), :] = dy_ref[pl.ds(base, ABLK), :]
            return carry

        lax.fori_loop(0, nchunk, fill, 0)

        def comp(ch, carry):
            base = pl.multiple_of(ch * ABLK, ABLK)
            dy = dy_ref[pl.ds(base, ABLK), :]
            du = jnp.zeros((ABLK, 128), F32)
            for k in range(CONV_K):
                du = du + dypad[pl.ds(base + (CONV_K - 1 - k), ABLK), :] * w_ref[k:k + 1, :]
                dwacc[k * 8:(k + 1) * 8, :] += _sum8(dy * upad[pl.ds(base + (off + k), ABLK), :])
            db_ref[...] += _sum8(dy)
            a = a_ref[pl.ds(base, ABLK), :]
            s = _sigmoid(g_ref[pl.ds(base, ABLK), :])
            dcc_ref[0, pl.ds(base, ABLK), :] = (du * s).astype(BF16)
            dcc_ref[1, pl.ds(base, ABLK), :] = (du * a * (s * (1.0 - s))).astype(BF16)
            return carry

        lax.fori_loop(0, nchunk, comp, 0)
        dw_ref[...] = dwacc[...].reshape(CONV_PAD, 8, 128).sum(axis=1)

    return pl.pallas_call(
        body, out_shape=[SDS((2, lp, c), BF16), SDS((CONV_PAD, c), F32), SDS((8, c), F32)],
        grid=(ncb,),
        in_specs=[pl.BlockSpec((lp, 128), lambda j: (0, j)), pl.BlockSpec((lp, 128), lambda j: (0, j)),
                  pl.BlockSpec((lp, 128), lambda j: (0, j + ncb)), pl.BlockSpec((CONV_PAD, 128), lambda j: (0, j))],
        out_specs=[pl.BlockSpec((2, lp, 128), lambda j: (0, 0, j)),
                   pl.BlockSpec((CONV_PAD, 128), lambda j: (0, j)), pl.BlockSpec((8, 128), lambda j: (0, j))],
        scratch_shapes=[pltpu.VMEM((lp + CONV_PAD, 128), F32), pltpu.VMEM((lp + CONV_PAD, 128), F32),
                        pltpu.VMEM((CONV_PAD * 8, 128), F32)],
        name=name, compiler_params=_cp())(dyc, cacg, cacg, w)


def _mesh_pos():
    x, y, c = lax.axis_index("x"), lax.axis_index("y"), lax.axis_index("c")
    return x, y, c


def _peer(pos, r):
    x, y, c = pos
    px = (1 - x) if (r >> 2) & 1 else x
    py = (1 - y) if (r >> 1) & 1 else y
    pc = (1 - c) if r & 1 else c
    return (px, py, pc), 4 * px + 2 * py + pc


SIBLING = 1
OTHER_CHIPS = (2, 4, 6)


class _Job:
    def __init__(self, src, dst, scatter, src_layer=None, dst_layer=None):
        self.src, self.dst, self.scatter, self.src_layer, self.dst_layer = src, dst, scatter, src_layer, dst_layer

    def src_view(self, ins, idx):
        v = ins[self.src] if self.src_layer is None else ins[self.src].at[self.src_layer]
        return v.at[idx] if self.scatter else v

    def dst_view(self, outs, slot):
        v = outs[self.dst] if self.dst_layer is None else outs[self.dst].at[self.dst_layer]
        return v.at[slot]


def _remote(job, j, r, src, dst, to, send, recv):
    return pltpu.make_async_remote_copy(src_ref=src, dst_ref=dst, send_sem=send.at[j, r - 1], recv_sem=recv.at[j, r - 1],
                                        device_id=to, device_id_type=pl.DeviceIdType.MESH)


def _exchange_start(jobs, ins, outs, send, recv, loc):
    pos = _mesh_pos()
    me = 4 * pos[0] + 2 * pos[1] + pos[2]
    for j, job in enumerate(jobs):
        pltpu.make_async_copy(job.src_view(ins, me), job.dst_view(outs, me), loc.at[j]).start()
        for r in (range(1, N_DEV) if job.scatter else (SIBLING,) + OTHER_CHIPS):
            peer, peer_idx = _peer(pos, r)
            _remote(job, j, r, job.src_view(ins, peer_idx), job.dst_view(outs, me), peer, send, recv).start()


def _exchange_forward(jobs, ins, outs, send, recv, loc):
    pos = _mesh_pos()
    sibling, _ = _peer(pos, SIBLING)
    for j, job in enumerate(jobs):
        if job.scatter:
            continue
        for r in OTHER_CHIPS:
            peer, peer_idx = _peer(pos, r)
            slot = job.dst_view(outs, peer_idx)
            _remote(job, j, r, job.src_view(ins, peer_idx), slot, peer, send, recv).wait_recv()
            _remote(job, j, r ^ SIBLING, slot, slot, sibling, send, recv).start()


def _exchange_wait(jobs, ins, outs, send, recv, loc):
    pos = _mesh_pos()
    me = 4 * pos[0] + 2 * pos[1] + pos[2]
    for j, job in enumerate(jobs):
        for r in range(1, N_DEV):
            peer, peer_idx = _peer(pos, r)
            cp = _remote(job, j, r, job.src_view(ins, peer_idx), job.dst_view(outs, peer_idx), peer, send, recv)
            if job.scatter or r not in OTHER_CHIPS:
                cp.wait_recv()
            cp.wait_send()
        pltpu.make_async_copy(job.src_view(ins, me), job.dst_view(outs, me), loc.at[j]).wait()


def _exchange_sems(n_jobs):
    return [pltpu.SemaphoreType.DMA((n_jobs, N_DEV - 1)), pltpu.SemaphoreType.DMA((n_jobs, N_DEV - 1)),
            pltpu.SemaphoreType.DMA((n_jobs,))]


def _exchange(jobs, arrs, out_shape, name):
    n_in, n_out = len(arrs), len(out_shape)
    any_spec = pl.BlockSpec(memory_space=pl.ANY)

    def body(*refs):
        ins, outs, sems = refs[:n_in], refs[n_in:n_in + n_out], refs[n_in + n_out:]
        _exchange_start(jobs, ins, outs, *sems)
        _exchange_forward(jobs, ins, outs, *sems)
        _exchange_wait(jobs, ins, outs, *sems)

    return pl.pallas_call(
        body, out_shape=out_shape, in_specs=[any_spec] * n_in, out_specs=[any_spec] * n_out,
        scratch_shapes=_exchange_sems(len(jobs)),
        name=name, compiler_params=pltpu.CompilerParams(has_side_effects=True))(*arrs)


def _all_reduce_small(placed, cuts, rows, w, name):
    n_in = len(placed)

    def body(*refs):
        in_refs, o_refs = refs[:n_in], refs[n_in:n_in + len(cuts)]
        buf, send, recv = refs[n_in + len(cuts):]
        pos = _mesh_pos()
        me = 4 * pos[0] + 2 * pos[1] + pos[2]
        buf[me] = jnp.zeros((rows, w), F32)
        for ref, (arr, row0, col0, is_partial) in zip(in_refs, placed):
            val = ref[...].sum(axis=0, keepdims=True) if is_partial else ref[...]
            buf[me, row0:row0 + val.shape[0], col0:col0 + val.shape[1]] = val
        for r in range(1, N_DEV):
            peer, _ = _peer(pos, r)
            pltpu.make_async_remote_copy(src_ref=buf.at[me], dst_ref=buf.at[me], send_sem=send.at[r - 1],
                                         recv_sem=recv.at[r - 1], device_id=peer,
                                         device_id_type=pl.DeviceIdType.MESH).start()
        for r in range(1, N_DEV):
            peer, peer_idx = _peer(pos, r)
            cp = pltpu.make_async_remote_copy(src_ref=buf.at[me], dst_ref=buf.at[peer_idx], send_sem=send.at[r - 1],
                                              recv_sem=recv.at[r - 1], device_id=peer,
                                              device_id_type=pl.DeviceIdType.MESH)
            cp.wait_recv()
            cp.wait_send()
        acc = buf[0]
        for dev in range(1, N_DEV):
            acc = acc + buf[dev]
        for o_ref, (_, pieces) in zip(o_refs, cuts):
            for index, row0, nrows, col0, ncols in pieces:
                o_ref[index] = acc[row0:row0 + nrows, col0:col0 + ncols].reshape(o_ref.at[index].shape)

    vmem = pl.BlockSpec(memory_space=pltpu.VMEM)
    return pl.pallas_call(
        body, out_shape=[SDS(shape, F32) for shape, _ in cuts], in_specs=[vmem] * n_in, out_specs=[vmem] * len(cuts),
        scratch_shapes=[pltpu.VMEM((N_DEV, rows, w), F32), pltpu.SemaphoreType.DMA((N_DEV - 1,)),
                        pltpu.SemaphoreType.DMA((N_DEV - 1,))],
        name=name, compiler_params=pltpu.CompilerParams(has_side_effects=True))(*[p[0] for p in placed])


def _adamw_math(w, g, m, v):
    m = ADAM_B1 * m + (1.0 - ADAM_B1) * g
    v = ADAM_B2 * v + (1.0 - ADAM_B2) * (g * g)
    m_hat = m / (1.0 - ADAM_B1 ** ADAM_STEP)
    v_hat = v / (1.0 - ADAM_B2 ** ADAM_STEP)
    delta = -ADAM_LR * (m_hat / (jnp.sqrt(v_hat) + ADAM_EPS) + ADAM_WD * w)
    return delta, m, v


def _adamw_shard(parts, w, m, v, name):
    depth = len(parts)
    _, rr, cc = parts[0].shape
    tr = _tile(rr, 256, 8)
    nt = rr // tr
    part_block, blk = (N_DEV, tr, cc), pl.BlockSpec((None, tr, cc), lambda l, i: (l, i, 0))

    def body(*refs):
        p_refs = refs[:depth]
        w_ref, m_ref, v_ref, g_out, d_out, m_out, v_out = refs[depth:]
        for li in range(depth):
            @pl.when(pl.program_id(0) == li)
            def _(p_ref=p_refs[li]):
                g = p_ref[0].astype(F32)
                for dev in range(1, N_DEV):
                    g = g + p_ref[dev].astype(F32)
                delta, mm, vv = _adamw_math(w_ref[...], g, m_ref[...], v_ref[...])
                g_out[...] = g
                d_out[...] = delta
                m_out[...] = mm
                v_out[...] = vv

    def part_spec(li):
        return pl.BlockSpec(part_block, lambda l, i: (0, jnp.where(l == li, i, jnp.where(l < li, 0, nt - 1)), 0))

    return pl.pallas_call(
        body, out_shape=[SDS(w.shape, F32)] * 4, grid=(depth, nt),
        in_specs=[part_spec(li) for li in range(depth)] + [blk, blk, blk],
        out_specs=[blk] * 4, name=name, compiler_params=_cp())(*parts, w, m, v)


def _adamw_small(gs, ws, ms, vs, name):
    n = len(gs)

    def body(*refs):
        g_refs, w_refs, m_refs, v_refs = (refs[k * n:(k + 1) * n] for k in range(4))
        outs = refs[4 * n:]
        for k in range(n):
            delta, mm, vv = _adamw_math(w_refs[k][...], g_refs[k][...], m_refs[k][...], v_refs[k][...])
            outs[k][...] = delta
            outs[n + k][...] = mm
            outs[2 * n + k][...] = vv

    res = pl.pallas_call(body, out_shape=[SDS(w.shape, F32) for w in ws] * 3, name=name,
                         compiler_params=_cp())(*gs, *ws, *ms, *vs)
    return res[:n], res[n:2 * n], res[2 * n:]


def _from_cols(t):
    return jnp.transpose(t, (1, 0, 2)).reshape(t.shape[1], N_DEV * t.shape[2])


def _swap(t):
    return jnp.swapaxes(t, -1, -2)


def kernel(x, meta_tokens, mix_norm_g, w_in, conv_dw_w, conv_dw_b, conv_ln_g, conv_ln_b, w_out, ffn_norm_g, w_gate, w_up, w_down, final_norm_g, loss_target, m_meta_tokens, m_mix_norm_g, m_w_in, m_conv_dw_w, m_conv_dw_b, m_conv_ln_g, m_conv_ln_b, m_w_out, m_ffn_norm_g, m_w_gate, m_w_up, m_w_down, m_final_norm_g, v_meta_tokens, v_mix_norm_g, v_w_in, v_conv_dw_w, v_conv_dw_b, v_conv_ln_g, v_conv_ln_b, v_w_out, v_ffn_norm_g, v_w_gate, v_w_up, v_w_down, v_final_norm_g):
    depth, d, in_shard = w_in.shape
    seq = x.shape[1]
    sb = N_HEADS * HEAD_DIM
    cc = conv_dw_w.shape[2] * N_DEV
    ff = w_gate.shape[2] * N_DEV
    assert in_shard * N_DEV == 3 * sb + 2 * cc and x.shape[0] == 1
    lr = N_META + seq
    lp = -(-lr // ABLK) * ABLK
    me = 4 * lax.axis_index("x") + 2 * lax.axis_index("y") + lax.axis_index("c")

    big_names = ("w_in", "w_out", "w_gate", "w_up", "w_down")
    transposed = {"w_in": True, "w_out": False, "w_gate": True, "w_up": True, "w_down": False}
    shard = dict(w_in=_swap(w_in).astype(BF16), w_out=w_out.astype(BF16), w_gate=_swap(w_gate).astype(BF16),
                 w_up=_swap(w_up).astype(BF16), w_down=w_down.astype(BF16))

    def gather_of(keys):
        names = sorted({n for n, _ in keys}, key=big_names.index)
        jobs = [_Job(names.index(n), j, False, src_layer=i) for j, (n, i) in enumerate(keys)]
        return jobs, [shard[n] for n in names], [SDS((N_DEV,) + shard[n].shape[1:], BF16) for n, _ in keys]

    first_keys = [("w_in", 0)]

    def riding(keys):
        return _Rider(*gather_of(keys)) if keys else None

    def receive(keys, arrays):
        for (n, li), t in zip(keys, arrays):
            wl[li][n] = t.reshape(-1, d)

    jobs, srcs, out_shape = gather_of(first_keys)
    for extra in (meta_tokens, conv_dw_w):
        jobs.append(_Job(len(srcs), len(out_shape), False))
        srcs.append(extra)
        out_shape.append(SDS((N_DEV,) + extra.shape, F32))
    gathered = _exchange(jobs, srcs, out_shape, "gather_first")
    wl = [dict() for _ in range(depth)]
    receive(first_keys, gathered)
    meta_full = _from_cols(gathered[-2])
    taps = jnp.transpose(gathered[-1], (1, 2, 0, 3)).reshape(depth, CONV_K, cc)
    taps = jnp.pad(taps, ((0, 0), (0, CONV_PAD - CONV_K), (0, 0)))
    tri_fwd, tri_after, tri_before = _tri_consts()

    h = jnp.concatenate([meta_full, x[0], jnp.zeros((lp - lr, d), F32)], axis=0)
    saved = []
    for i in range(depth):
        p = wl[i]
        sv = dict(h_in=h)
        qkv, hn = _rms_mm(h, mix_norm_g[i:i + 1], p["w_in"], 0, 3 * sb, BF16, f"proj_qkv_{i}", True)
        keys = [("w_out", 0)] if i == 0 else []
        cacg, *arrived = _rms_mm(h, mix_norm_g[i:i + 1], p["w_in"], 3 * sb, 2 * cc, F32, f"proj_conv_{i}", False,
                                 riding(keys))
        receive(keys, arrived)
        keys = [("w_gate", i), ("w_up", i)]
        attn, rsave, *arrived = _attn_fwd(qkv, tri_fwd, f"attn_fwd_{i}", riding(keys))
        receive(keys, arrived)
        keys = [("w_down", 0)] if i == 0 else []
        yc, *arrived = _conv_fwd_dw(cacg, taps[i], conv_dw_b[i:i + 1], f"conv_fwd_dw_{i}", riding(keys))
        receive(keys, arrived)
        conv = _conv_fwd_ln(yc, conv_ln_g[i:i + 1], conv_ln_b[i:i + 1], f"conv_fwd_ln_{i}")
        h = _mix_out(attn, conv, p["w_out"], h, f"mix_out_{i}")
        sv.update(qkv=qkv, hn=hn, cacg=cacg, rsave=rsave, yc=yc, attn=attn, conv=conv, h_mid=h)
        nxt = i + 1
        keys = [("w_in", nxt), ("w_out", nxt), ("w_down", nxt)] if nxt < depth else []
        h, hn2, act, gate, up, *arrived = _ffn_fwd(h, ffn_norm_g[i:i + 1], p["w_gate"], p["w_up"], p["w_down"],
                                                   f"ffn_fwd_{i}", riding(keys))
        receive(keys, arrived)
        sv.update(hn2=hn2, act=act, gate=gate, up=up)
        saved.append(sv)

    tpad = jnp.pad(loss_target[0], ((N_META, lp - lr), (0, 0)))
    dh, dg_final, loss_part = _loss_head(h, tpad, final_norm_g.reshape(1, d), seq, "loss_head")
    loss = lax.psum(loss_part[0, 0], MESH_AXES)

    parts = {}

    def sending(items):
        srcs = [t.reshape(N_DEV, t.shape[0] // N_DEV, d) for _, t in items]
        return _Rider([_Job(j, j, True) for j in range(len(items))], srcs, [SDS(t.shape, BF16) for t in srcs])

    def arrive(items, arrays):
        parts.update({key: t for (key, _), t in zip(items, arrays)})

    grads = [None] * depth
    from_above = []
    for i in reversed(range(depth)):
        p, sv = wl[i], saved[i]
        d_g, d_u, *got = _ffn_bwd_act(dh, sv["gate"], sv["up"], p["w_down"], f"ffn_bwd_act_{i}", sending(from_above))
        arrive(from_above, got)
        gw_down = _mm_tn(sv["act"], dh, f"grad_w_down_{i}")
        gw_gate = _mm_tn(d_g, sv["hn2"], f"grad_w_gate_{i}")
        gw_up = _mm_tn(d_u, sv["hn2"], f"grad_w_up_{i}")
        items = [(("w_down", i), gw_down)]
        dh, dg_ffn, *got = _ffn_bwd_in(d_g, d_u, p["w_gate"], p["w_up"], sv["h_mid"], ffn_norm_g[i:i + 1], dh,
                                       f"ffn_bwd_in_{i}", sending(items))
        arrive(items, got)
        gw_out = jnp.concatenate([_mm_tn(sv["attn"], dh, f"grad_w_out_attn_{i}"),
                                  _mm_tn(sv["conv"], dh, f"grad_w_out_conv_{i}")], axis=0)
        d_attn = _mm_nt(dh, p["w_out"], 0, sb, BF16, f"mix_bwd_attn_{i}")
        d_conv = _mm_nt(dh, p["w_out"], sb, cc, F32, f"mix_bwd_conv_{i}")
        items = [(("w_gate", i), gw_gate), (("w_up", i), gw_up)]
        dqkv, *got = _attn_bwd(sv["qkv"], d_attn, sv["rsave"], tri_after, tri_before, f"attn_bwd_{i}", sending(items))
        arrive(items, got)
        dyc, dg_ln, db_ln = _conv_bwd_ln(sv["yc"], d_conv, conv_ln_g[i:i + 1], conv_ln_b[i:i + 1], f"conv_bwd_ln_{i}")
        dcc, g_taps_i, db_conv = _conv_bwd_dw(dyc, sv["cacg"], taps[i], f"conv_bwd_dw_{i}")
        items = [(("w_out", i), gw_out)]
        gw_in, *got = _grad_w_in_t(dqkv, dcc, sv["hn"], f"grad_w_in_{i}", sending(items))
        arrive(items, got)
        from_above = [(("w_in", i), gw_in)]
        items = from_above if i == 0 else []
        dh, dg_mix, *got = _mix_bwd_in(dqkv, dcc, p["w_in"], sv["h_in"], mix_norm_g[i:i + 1], dh, f"mix_bwd_in_{i}",
                                       sending(items))
        arrive(items, got)
        grads[i] = dict(taps=g_taps_i, dg_mix=dg_mix, dg_ffn=dg_ffn, dg_ln=dg_ln, db_ln=db_ln, db_conv=db_conv)
    grad_x = dh[N_META:lr][None]

    per = d // depth
    assert depth * cc <= d and d % depth == 0 and per % 128 == 0
    placed, cuts, row = [], [], 0
    for key in ("dg_mix", "dg_ffn"):
        placed += [(grads[i][key], row + i, 0, True) for i in range(depth)]
        cuts.append(((depth, d), [(slice(0, depth), row, depth, 0, d)]))
        row += depth
    placed.append((dg_final, row, 0, True))
    cuts.append(((1, d), [(slice(0, 1), row, 1, 0, d)]))
    row += 1
    for key in ("db_conv", "dg_ln", "db_ln"):
        placed += [(grads[i][key], row, i * per, True) for i in range(depth)]
        cuts.append(((depth, cc), [(slice(i, i + 1), row, 1, i * per, cc) for i in range(depth)]))
        row += 1
    placed += [(grads[i]["taps"], row, i * per, False) for i in range(depth)]
    cuts.append(((depth, CONV_PAD, cc), [(i, row, CONV_PAD, i * per, cc) for i in range(depth)]))
    row += CONV_PAD
    placed.append((dh[:N_META], row, 0, False))
    cuts.append(((N_META, d), [(slice(0, N_META), row, N_META, 0, d)]))
    row += N_META
    g_mix, g_ffn, g_final, g_cb, g_lg, g_lb, g_taps_full, g_meta_full = _all_reduce_small(
        placed, cuts, -(-row // 8) * 8, d, "reduce_small")
    csh = cc // N_DEV
    g_taps_own = lax.dynamic_slice_in_dim(g_taps_full[:, :CONV_K], me * csh, csh, axis=2)
    msh = d // N_DEV
    g_meta_own = lax.dynamic_slice_in_dim(g_meta_full, me * msh, msh, axis=1)

    row1 = lambda t: t.reshape(1, d)
    small_g = [g_meta_own, g_mix, g_taps_own, g_cb, g_lg, g_lb, g_ffn, g_final]
    small_w = [meta_tokens, mix_norm_g, conv_dw_w, conv_dw_b, conv_ln_g, conv_ln_b, ffn_norm_g, row1(final_norm_g)]
    small_m = [m_meta_tokens, m_mix_norm_g, m_conv_dw_w, m_conv_dw_b, m_conv_ln_g, m_conv_ln_b, m_ffn_norm_g,
               row1(m_final_norm_g)]
    small_v = [v_meta_tokens, v_mix_norm_g, v_conv_dw_w, v_conv_dw_b, v_conv_ln_g, v_conv_ln_b, v_ffn_norm_g,
               row1(v_final_norm_g)]
    s_delta, s_m, s_v = _adamw_small(small_g, small_w, small_m, small_v, "adamw_small")
    unrow = lambda ts: list(ts[:-1]) + [ts[-1].reshape(d)]
    small_g, s_delta, s_m, s_v = unrow(small_g), unrow(s_delta), unrow(s_m), unrow(s_v)

    big = []
    for n, w, m, v in zip(big_names, (w_in, w_out, w_gate, w_up, w_down), (m_w_in, m_w_out, m_w_gate, m_w_up, m_w_down),
                          (v_w_in, v_w_out, v_w_gate, v_w_up, v_w_down)):
        fix = _swap if transposed[n] else (lambda t: t)
        res = _adamw_shard([parts[(n, i)] for i in range(depth)], fix(w), fix(m), fix(v), f"adamw_{n}")
        big.append([fix(t) for t in res])
    b_in, b_out, b_gate, b_up, b_down = big

    def ordered(k, smalls):
        s_meta, s_mix, s_taps, s_cb, s_lg, s_lb, s_ffn, s_final = smalls
        return [s_meta, s_mix, b_in[k], s_taps, s_cb, s_lg, s_lb, b_out[k], s_ffn, b_gate[k], b_up[k], b_down[k], s_final]

    return (loss, grad_x, *ordered(0, small_g), *ordered(1, s_delta), *ordered(2, s_m), *ordered(3, s_v))
```

```python
import math

import jax
import jax.numpy as jnp
from jax import lax
from jax.experimental import pallas as pl
from jax.experimental.pallas import tpu as pltpu

F32 = jnp.float32
BF16 = jnp.bfloat16
SDS = jax.ShapeDtypeStruct

N_META = 16
N_HEADS = 8
HEAD_DIM = 64
CONV_K = 31
CONV_PAD = 32
ABLK = 128
ATT_GROUP = 4
ATT_SPLIT = 2
ATT_TOP = 2
ATT_FLIGHT = 3
EXP_ZERO_AT = 104.0
GONE = 1e30
MASKED = -1e30
FF_CHUNK = 256
FF_BLOCK = 1408
FF_ROWS = 528
EPS = 1e-6
N_DEV = 8
MESH_AXES = ("x", "y", "c")
ADAM_LR = 0.001
ADAM_B1 = 0.9
ADAM_B2 = 0.999
ADAM_EPS = 1e-08
ADAM_WD = 0.01
ADAM_STEP = 10
MIB = 1 << 20
VMEM_LIMIT_MIB = 48
FFN_FWD_VMEM_MIB = 56
MM_ROWS = 1056
EW_ROWS = 528


def _cp(mib=None):
    return pltpu.CompilerParams(vmem_limit_bytes=(mib or VMEM_LIMIT_MIB) * MIB)


def _tile(n, cap, mult):
    best = None
    for t in range(mult, min(n, cap) + 1, mult):
        if n % t == 0:
            best = t
    assert best is not None, (n, cap, mult)
    return best


def _sum8(v):
    r, c = v.shape
    return v.reshape(r // 8, 8, c).sum(axis=0)


def _sigmoid(x):
    return 1.0 / (1.0 + jnp.exp(-x))


def _ff_block(f):
    fb = _tile(f, FF_BLOCK, 128)
    return fb, [(s0, min(FF_CHUNK, fb - s0)) for s0 in range(0, fb, FF_CHUNK)]


NT = (((1,), (1,)), ((), ()))
TN = (((0,), (0,)), ((), ()))


def _rms_mm(h, g, wt, w_row0, n_cols, out_dtype, name, hn_out, rider=None):
    lp, d = h.shape
    tm = _tile(lp, MM_ROWS, 16)
    tn = _tile(math.gcd(n_cols, w_row0), 768, 128)
    off = w_row0 // tn
    ni, nj = lp // tm, n_cols // tn
    rider = rider or _Rider()

    def body(h_ref, g_ref, w_ref, *rest):
        outs, (hn_s,) = rider.split(rest, 2 if hn_out else 1, 1)
        o_ref = outs[0]
        hn_ref = outs[1] if hn_out else None
        rider.start((pl.program_id(0) == 0) & (pl.program_id(1) == 0))
        if ni > 1:
            rider.forward((pl.program_id(0) == ni - 1) & (pl.program_id(1) == 0))

        @pl.when(pl.program_id(1) == 0)
        def _():
            x = h_ref[...]
            r = lax.rsqrt(jnp.mean(x * x, axis=-1, keepdims=True) + EPS)
            hn = ((x * r) * g_ref[...]).astype(BF16)
            hn_s[...] = hn
            if hn_out:
                hn_ref[...] = hn

        o_ref[...] = lax.dot_general(hn_s[...], w_ref[...], NT, preferred_element_type=F32).astype(out_dtype)
        rider.wait((pl.program_id(0) == ni - 1) & (pl.program_id(1) == nj - 1))

    out_shape = [SDS((lp, n_cols), out_dtype)]
    out_specs = [pl.BlockSpec((tm, tn), lambda i, j: (i, j))]
    if hn_out:
        out_shape.append(SDS((lp, d), BF16))
        out_specs.append(pl.BlockSpec((tm, d), lambda i, j: (i, 0)))
    return pl.pallas_call(
        body, out_shape=out_shape + rider.out_shape, grid=(ni, nj),
        in_specs=[pl.BlockSpec((tm, d), lambda i, j: (i, 0)),
                  pl.BlockSpec((1, d), lambda i, j: (0, 0)),
                  pl.BlockSpec((tn, d), lambda i, j: (j + off, 0))] + rider.in_specs(),
        out_specs=out_specs + rider.out_specs(), scratch_shapes=[pltpu.VMEM((tm, d), BF16)] + rider.scratch(),
        name=name, compiler_params=_cp())(h, g, wt, *rider.srcs)


def _mix_bwd(dh, w, sb, cc, name):
    m, k = dh.shape
    tm = _tile(m, MM_ROWS, 16)

    def body(a_ref, w_ref, attn_ref, conv_ref):
        a = a_ref[...].astype(BF16)
        attn_ref[...] = lax.dot_general(a, w_ref[0:sb, :], NT, preferred_element_type=F32).astype(BF16)
        conv_ref[...] = lax.dot_general(a, w_ref[sb:sb + cc, :], NT, preferred_element_type=F32)

    return pl.pallas_call(
        body, out_shape=[SDS((m, sb), BF16), SDS((m, cc), F32)], grid=(m // tm,),
        in_specs=[pl.BlockSpec((tm, k), lambda i: (i, 0)), pl.BlockSpec((sb + cc, k), lambda i: (0, 0))],
        out_specs=[pl.BlockSpec((tm, sb), lambda i: (i, 0)), pl.BlockSpec((tm, cc), lambda i: (i, 0))],
        name=name, compiler_params=_cp())(dh, w)


def _mix_out(attn, conv, w, res, name):
    m, ka = attn.shape
    kc = conv.shape[1]
    n = w.shape[1]
    assert ka == kc
    tm = _tile(m, MM_ROWS, 16)
    tn = _tile(n, 512, 128)

    def body(a_ref, c_ref, wa_ref, wc_ref, r_ref, o_ref):
        o_ref[...] = (r_ref[...] + jnp.dot(a_ref[...], wa_ref[...], preferred_element_type=F32)
                      + jnp.dot(c_ref[...], wc_ref[...], preferred_element_type=F32))

    return pl.pallas_call(
        body, out_shape=SDS((m, n), F32), grid=(m // tm, n // tn),
        in_specs=[pl.BlockSpec((tm, ka), lambda i, j: (i, 0)), pl.BlockSpec((tm, kc), lambda i, j: (i, 0)),
                  pl.BlockSpec((ka, tn), lambda i, j: (0, j)), pl.BlockSpec((kc, tn), lambda i, j: (1, j)),
                  pl.BlockSpec((tm, tn), lambda i, j: (i, j))],
        out_specs=pl.BlockSpec((tm, tn), lambda i, j: (i, j)), name=name, compiler_params=_cp())(attn, conv, w, w, res)


def _mm_tn(a, b, name):
    l, m = a.shape
    n = b.shape[1]
    tm = _tile(m, 1408, 128)
    tn = _tile(n, 1024, 128)
    tl = _tile(l, 1408, 128)
    nl = l // tl

    def body(a_ref, b_ref, o_ref, acc):
        @pl.when(pl.program_id(2) == 0)
        def _():
            acc[...] = jnp.zeros_like(acc)

        acc[...] += lax.dot_general(a_ref[...].astype(BF16), b_ref[...].astype(BF16), TN, preferred_element_type=F32)

        @pl.when(pl.program_id(2) == nl - 1)
        def _():
            o_ref[...] = acc[...].astype(BF16)

    return pl.pallas_call(
        body, out_shape=SDS((m, n), BF16), grid=(m // tm, n // tn, nl),
        in_specs=[pl.BlockSpec((tl, tm), lambda i, j, s: (s, i)),
                  pl.BlockSpec((tl, tn), lambda i, j, s: (s, j))],
        out_specs=pl.BlockSpec((tm, tn), lambda i, j, s: (i, j)),
        scratch_shapes=[pltpu.VMEM((tm, tn), F32)], name=name, compiler_params=_cp())(a, b)


def _grad_w_out(attn, conv, dh, name):
    l, w = attn.shape
    d = dh.shape[1]
    assert conv.shape == attn.shape
    tn = _tile(d, 1024, 128)
    tl = _tile(l, 1408, 128)
    nl = l // tl

    def body(a_ref, c_ref, b_ref, o_ref, acc):
        p, s = pl.program_id(0), pl.program_id(2)

        @pl.when(s == 0)
        def _():
            acc[...] = jnp.zeros_like(acc)

        @pl.when(p == 0)
        def _():
            acc[...] += lax.dot_general(a_ref[...], b_ref[...].astype(BF16), TN, preferred_element_type=F32)

        @pl.when(p == 1)
        def _():
            acc[...] += lax.dot_general(c_ref[...], b_ref[...].astype(BF16), TN, preferred_element_type=F32)

        @pl.when(s == nl - 1)
        def _():
            o_ref[...] = acc[...].astype(BF16)

    out = pl.pallas_call(
        body, out_shape=SDS((2, w, d), BF16), grid=(2, d // tn, nl),
        in_specs=[pl.BlockSpec((tl, w), lambda p, j, s: (jnp.where(p == 0, s, nl - 1), 0)),
                  pl.BlockSpec((tl, w), lambda p, j, s: (jnp.where(p == 1, s, 0), 0)),
                  pl.BlockSpec((tl, tn), lambda p, j, s: (s, j))],
        out_specs=pl.BlockSpec((None, w, tn), lambda p, j, s: (p, 0, j)),
        scratch_shapes=[pltpu.VMEM((w, tn), F32)], name=name, compiler_params=_cp())(attn, conv, dh)
    return out.reshape(2 * w, d)


def _grad_w_in_t(dqkv, dcc, hn, name, rider=None):
    nq, l, w = dqkv.shape
    nc = dcc.shape[0]
    d = hn.shape[1]
    assert dcc.shape[2] == w
    tn = _tile(d, 1024, 128)
    tl = _tile(l, 1408, 128)
    nl = l // tl
    nj = d // tn
    rider = rider or _Rider()

    def body(q_ref, c_ref, b_ref, *rest):
        (o_ref,), (acc,) = rider.split(rest, 1, 1)
        p, jj, s = pl.program_id(0), pl.program_id(1), pl.program_id(2)
        rider.start((p == 0) & (jj == 0) & (s == 0))

        @pl.when(s == 0)
        def _():
            acc[...] = jnp.zeros_like(acc)

        @pl.when(p < nq)
        def _():
            acc[...] += lax.dot_general(q_ref[...], b_ref[...], TN, preferred_element_type=F32)

        @pl.when(p >= nq)
        def _():
            acc[...] += lax.dot_general(c_ref[...], b_ref[...], TN, preferred_element_type=F32)

        @pl.when(s == nl - 1)
        def _():
            o_ref[...] = acc[...].astype(BF16)

        rider.wait((p == nq + nc - 1) & (jj == nj - 1) & (s == nl - 1))

    out, *arrived = pl.pallas_call(
        body, out_shape=[SDS((nq + nc, w, d), BF16)] + rider.out_shape, grid=(nq + nc, nj, nl),
        in_specs=[pl.BlockSpec((None, tl, w), lambda p, j, s: (jnp.minimum(p, nq - 1), s, 0)),
                  pl.BlockSpec((None, tl, w), lambda p, j, s: (jnp.maximum(p - nq, 0), s, 0)),
                  pl.BlockSpec((tl, tn), lambda p, j, s: (s, j))] + rider.in_specs(),
        out_specs=[pl.BlockSpec((None, w, tn), lambda p, j, s: (p, 0, j))] + rider.out_specs(),
        scratch_shapes=[pltpu.VMEM((w, tn), F32)] + rider.scratch(), name=name,
        compiler_params=_cp())(dqkv, dcc, hn, *rider.srcs)
    return [out.reshape((nq + nc) * w, d)] + arrived


def _rms_bwd_tail(acc, h_ref, g_ref, dres_ref, o_ref, dg_ref):
    x = h_ref[...]
    r = lax.rsqrt(jnp.mean(x * x, axis=-1, keepdims=True) + EPS)
    xr = x * r
    dyv = acc[...]
    gy = dyv * g_ref[...]
    o_ref[...] = dres_ref[...] + r * (gy - xr * jnp.mean(gy * xr, axis=-1, keepdims=True))
    dg_ref[...] += _sum8(dyv * xr)


def _mix_bwd_in(dqkv, dcc, wt, h, g, dres, name, rider=None):
    nq, lp, w = dqkv.shape
    nc = dcc.shape[0]
    d = h.shape[1]
    tm = _tile(lp, FF_ROWS, 16)
    ni = lp // tm
    rider = rider or _Rider()

    def body(q_ref, c_ref, w_ref, h_ref, g_ref, dres_ref, *rest):
        (o_ref, dg_ref), (acc,) = rider.split(rest, 2, 1)
        i = pl.program_id(0)
        rider.start(i == 0)

        @pl.when(i == 0)
        def _():
            dg_ref[...] = jnp.zeros_like(dg_ref)

        pieces = [q_ref[p] for p in range(nq)] + [c_ref[p] for p in range(nc)]
        total = None
        for p, piece in enumerate(pieces):
            part = jnp.dot(piece, w_ref[p * w:(p + 1) * w, :], preferred_element_type=F32)
            total = part if total is None else total + part
        acc[...] = total
        _rms_bwd_tail(acc, h_ref, g_ref, dres_ref, o_ref, dg_ref)
        rider.wait(i == ni - 1)

    return pl.pallas_call(
        body, out_shape=[SDS((lp, d), F32), SDS((8, d), F32)] + rider.out_shape, grid=(ni,),
        in_specs=[pl.BlockSpec((nq, tm, w), lambda i: (0, i, 0)),
                  pl.BlockSpec((nc, tm, w), lambda i: (0, i, 0)),
                  pl.BlockSpec(((nq + nc) * w, d), lambda i: (0, 0)),
                  pl.BlockSpec((tm, d), lambda i: (i, 0)),
                  pl.BlockSpec((1, d), lambda i: (0, 0)),
                  pl.BlockSpec((tm, d), lambda i: (i, 0))] + rider.in_specs(),
        out_specs=[pl.BlockSpec((tm, d), lambda i: (i, 0)), pl.BlockSpec((8, d), lambda i: (0, 0))]
        + rider.out_specs(),
        scratch_shapes=[pltpu.VMEM((tm, d), F32)] + rider.scratch(), name=name,
        compiler_params=_cp())(dqkv, dcc, wt, h, g, dres, *rider.srcs)


def _ffn_bwd_in(d_g, d_u, wg_t, wu_t, h, g, dres, name, rider=None):
    lp, f = d_g.shape
    d = h.shape[1]
    tm = _tile(lp, FF_ROWS, 16)
    tk = _tile(f, FF_BLOCK, 128)
    nk = f // tk
    ni = lp // tm
    rider = rider or _Rider()

    def body(dg_in, du_in, wg_ref, wu_ref, h_ref, g_ref, dres_ref, *rest):
        (o_ref, dg_ref), (acc,) = rider.split(rest, 2, 1)
        i, kk = pl.program_id(0), pl.program_id(1)
        rider.start((i == 0) & (kk == 0))

        @pl.when(kk == 0)
        def _():
            acc[...] = jnp.zeros_like(acc)

        @pl.when((kk == 0) & (i == 0))
        def _():
            dg_ref[...] = jnp.zeros_like(dg_ref)

        acc[...] += (jnp.dot(dg_in[...], wg_ref[...], preferred_element_type=F32)
                     + jnp.dot(du_in[...], wu_ref[...], preferred_element_type=F32))

        @pl.when(kk == nk - 1)
        def _():
            _rms_bwd_tail(acc, h_ref, g_ref, dres_ref, o_ref, dg_ref)

        rider.wait((i == ni - 1) & (kk == nk - 1))

    return pl.pallas_call(
        body, out_shape=[SDS((lp, d), F32), SDS((8, d), F32)] + rider.out_shape, grid=(ni, nk),
        in_specs=[pl.BlockSpec((tm, tk), lambda i, j: (i, j)), pl.BlockSpec((tm, tk), lambda i, j: (i, j)),
                  pl.BlockSpec((tk, d), lambda i, j: (j, 0)), pl.BlockSpec((tk, d), lambda i, j: (j, 0)),
                  pl.BlockSpec((tm, d), lambda i, j: (i, 0)),
                  pl.BlockSpec((1, d), lambda i, j: (0, 0)),
                  pl.BlockSpec((tm, d), lambda i, j: (i, 0))] + rider.in_specs(),
        out_specs=[pl.BlockSpec((tm, d), lambda i, j: (i, 0)), pl.BlockSpec((8, d), lambda i, j: (0, 0))]
        + rider.out_specs(),
        scratch_shapes=[pltpu.VMEM((tm, d), F32)] + rider.scratch(), name=name,
        compiler_params=_cp())(d_g, d_u, wg_t, wu_t, h, g, dres, *rider.srcs)


def _ffn_fwd(h, g, wg_t, wu_t, w_d, name, rider=None):
    lp, d = h.shape
    f = w_d.shape[0]
    tm = _tile(lp, FF_ROWS, 16)
    fb, subs = _ff_block(f)
    nc = f // fb
    ni = lp // tm
    rider = rider or _Rider()

    def body(h_ref, g_ref, wg_ref, wu_ref, wd_ref, *rest):
        (o_ref, hn_ref, act_ref, gate_ref, up_ref), (hn_s, acc) = rider.split(rest, 5, 2)
        c = pl.program_id(1)
        rider.start((pl.program_id(0) == 0) & (c == 0))
        if ni > 1:
            rider.forward((pl.program_id(0) == ni - 1) & (c == 0))

        @pl.when(c == 0)
        def _():
            x = h_ref[...]
            r = lax.rsqrt(jnp.mean(x * x, axis=-1, keepdims=True) + EPS)
            hn = ((x * r) * g_ref[...]).astype(BF16)
            hn_s[...] = hn
            hn_ref[...] = hn
            acc[...] = jnp.zeros_like(acc)

        total = None
        for s0, sw in subs:
            gg = lax.dot_general(hn_s[...], wg_ref[s0:s0 + sw, :], NT, preferred_element_type=F32)
            uu = lax.dot_general(hn_s[...], wu_ref[s0:s0 + sw, :], NT, preferred_element_type=F32)
            gate_ref[:, s0:s0 + sw] = gg
            up_ref[:, s0:s0 + sw] = uu
            act = ((gg * _sigmoid(gg)) * uu).astype(BF16)
            act_ref[:, s0:s0 + sw] = act
            part = jnp.dot(act, wd_ref[s0:s0 + sw, :], preferred_element_type=F32)
            total = part if total is None else total + part
        acc[...] += total

        @pl.when(c == nc - 1)
        def _():
            o_ref[...] = h_ref[...] + acc[...]

        rider.wait((pl.program_id(0) == ni - 1) & (c == nc - 1))

    chunk = pl.BlockSpec((fb, d), lambda i, j: (j, 0))
    wide = pl.BlockSpec((tm, fb), lambda i, j: (i, j))
    return pl.pallas_call(
        body, out_shape=[SDS((lp, d), F32), SDS((lp, d), BF16), SDS((lp, f), BF16), SDS((lp, f), F32),
                         SDS((lp, f), F32)] + rider.out_shape, grid=(ni, nc),
        in_specs=[pl.BlockSpec((tm, d), lambda i, j: (i, 0)), pl.BlockSpec((1, d), lambda i, j: (0, 0)),
                  chunk, chunk, chunk] + rider.in_specs(),
        out_specs=[pl.BlockSpec((tm, d), lambda i, j: (i, 0)), pl.BlockSpec((tm, d), lambda i, j: (i, 0)),
                   wide, wide, wide] + rider.out_specs(),
        scratch_shapes=[pltpu.VMEM((tm, d), BF16), pltpu.VMEM((tm, d), F32)] + rider.scratch(),
        name=name, compiler_params=_cp(FFN_FWD_VMEM_MIB))(h, g, wg_t, wu_t, w_d, *rider.srcs)


def _ffn_bwd_act(dh, gate, up, w_d, name, rider=None):
    lp, d = dh.shape
    f = w_d.shape[0]
    tm = _tile(lp, FF_ROWS, 16)
    fb, subs = _ff_block(f)
    nc = f // fb
    ni = lp // tm
    rider = rider or _Rider()

    def body(dh_ref, gate_ref, up_ref, wd_ref, *rest):
        (dg_out, du_out), (dh_s,) = rider.split(rest, 2, 1)
        rider.start((pl.program_id(0) == 0) & (pl.program_id(1) == 0))

        @pl.when(pl.program_id(1) == 0)
        def _():
            dh_s[...] = dh_ref[...].astype(BF16)

        for s0, sw in subs:
            gg = gate_ref[:, s0:s0 + sw]
            uu = up_ref[:, s0:s0 + sw]
            dact = lax.dot_general(dh_s[...], wd_ref[s0:s0 + sw, :], NT, preferred_element_type=F32)
            s = _sigmoid(gg)
            dg_out[:, s0:s0 + sw] = (dact * uu * (s * (1.0 + gg * (1.0 - s)))).astype(BF16)
            du_out[:, s0:s0 + sw] = (dact * (gg * s)).astype(BF16)
        rider.wait((pl.program_id(0) == ni - 1) & (pl.program_id(1) == nc - 1))

    wide = pl.BlockSpec((tm, fb), lambda i, j: (i, j))
    return pl.pallas_call(
        body, out_shape=[SDS((lp, f), BF16), SDS((lp, f), BF16)] + rider.out_shape, grid=(ni, nc),
        in_specs=[pl.BlockSpec((tm, d), lambda i, j: (i, 0)), wide, wide,
                  pl.BlockSpec((fb, d), lambda i, j: (j, 0))] + rider.in_specs(),
        out_specs=[wide, wide] + rider.out_specs(), scratch_shapes=[pltpu.VMEM((tm, d), BF16)] + rider.scratch(),
        name=name, compiler_params=_cp())(dh, gate, up, w_d, *rider.srcs)


def _loss_head(h, tpad, g, n_real, name):
    lp, d = h.shape
    tm = _tile(lp, EW_ROWS, 16)

    def body(h_ref, t_ref, g_ref, dh_ref, dg_ref, loss_ref):
        i = pl.program_id(0)

        @pl.when(i == 0)
        def _():
            dg_ref[...] = jnp.zeros_like(dg_ref)
            loss_ref[...] = jnp.zeros_like(loss_ref)

        x = h_ref[...]
        r = lax.rsqrt(jnp.mean(x * x, axis=-1, keepdims=True) + EPS)
        xr = x * r
        y = xr * g_ref[...]
        row = i * tm + lax.broadcasted_iota(jnp.int32, (tm, d), 0)
        valid = (row >= N_META) & (row < N_META + n_real)
        diff = jnp.where(valid, y - t_ref[...], 0.0)
        loss_ref[...] += jnp.sum(diff * diff) * (0.5 / d)
        dy = diff * (1.0 / d)
        gy = dy * g_ref[...]
        dh_ref[...] = r * (gy - xr * jnp.mean(gy * xr, axis=-1, keepdims=True))
        dg_ref[...] += _sum8(dy * xr)

    return pl.pallas_call(
        body, out_shape=[SDS((lp, d), F32), SDS((8, d), F32), SDS((8, 128), F32)], grid=(lp // tm,),
        in_specs=[pl.BlockSpec((tm, d), lambda i: (i, 0)),
                  pl.BlockSpec((tm, d), lambda i: (i, 0)),
                  pl.BlockSpec((1, d), lambda i: (0, 0))],
        out_specs=[pl.BlockSpec((tm, d), lambda i: (i, 0)),
                   pl.BlockSpec((8, d), lambda i: (0, 0)),
                   pl.BlockSpec((8, 128), lambda i: (0, 0))],
        name=name, compiler_params=_cp())(h, tpad, g)


def _tri_consts():
    j = lax.broadcasted_iota(jnp.int32, (ABLK, ABLK), 0)
    s = lax.broadcasted_iota(jnp.int32, (ABLK, ABLK), 1)
    after = (j >= s).astype(BF16)
    before = (j < s).astype(BF16)
    ones = jnp.ones((ABLK, ABLK), BF16)
    two = lambda t: jnp.concatenate([t, t], axis=0)
    return (two(jnp.concatenate([after, ones], axis=1)),
            two(after),
            two(jnp.concatenate([before, ones], axis=1)))


def _softplus(z):
    neg_abs = lax.bitcast_convert_type(lax.bitcast_convert_type(z, jnp.uint32) | jnp.uint32(0x80000000), F32)
    return jnp.log(1.0 + jnp.exp(neg_abs)) + jnp.maximum(z, 0.0)


def _split_hi_lo(m):
    hi = m.astype(BF16)
    lo = (m - hi.astype(F32)).astype(BF16)
    return jnp.concatenate([hi, lo], axis=1)


def _head_halves(t2, in_a):
    zero = jnp.zeros_like(t2)
    return jnp.concatenate([jnp.where(in_a, t2, zero), jnp.where(in_a, zero, t2)], axis=0)


def _stack_blocks(t, nblk, in_a):
    return jnp.concatenate([_head_halves(t[u * ABLK:(u + 1) * ABLK], in_a) for u in range(nblk)], axis=0)


def _interleave(*gens):
    alive = list(gens)
    while alive:
        for g in list(alive):
            if next(g, alive) is alive:
                alive.remove(g)


def _attn_scale():
    scale = 1.0 / math.sqrt(HEAD_DIM)
    assert math.frexp(scale)[0] == 0.5, "a power of two, so that scaling q in bf16 is exact"
    return scale


def _pow2_below(n):
    assert n & (n - 1) == 0
    return [p for p in (64, 32, 16, 8, 4, 2, 1) if p < n]


class _Rider:
    def __init__(self, jobs=(), srcs=(), out_shape=()):
        self.jobs, self.srcs, self.out_shape = list(jobs), list(srcs), list(out_shape)
        self.any = [pl.BlockSpec(memory_space=pl.ANY)]

    def split(self, rest, n_out, n_scratch):
        ni, no = len(self.srcs), len(self.out_shape)
        self.ins, outs = rest[:ni], rest[ni:ni + n_out]
        self.outs = rest[ni + n_out:ni + n_out + no]
        scratch = rest[ni + n_out + no:ni + n_out + no + n_scratch]
        self.sems = rest[ni + n_out + no + n_scratch:]
        return outs, scratch

    def start(self, first):
        self.forwarded = False
        if self.jobs:
            @pl.when(first)
            def _():
                _exchange_start(self.jobs, self.ins, self.outs, *self.sems)

    def forward(self, late):
        self.forwarded = True
        if self.jobs:
            @pl.when(late)
            def _():
                _exchange_forward(self.jobs, self.ins, self.outs, *self.sems)

    def wait(self, last):
        if self.jobs:
            @pl.when(last)
            def _():
                if not self.forwarded:
                    _exchange_forward(self.jobs, self.ins, self.outs, *self.sems)
                _exchange_wait(self.jobs, self.ins, self.outs, *self.sems)

    def in_specs(self):
        return self.any * len(self.srcs)

    def out_specs(self):
        return self.any * len(self.out_shape)

    def scratch(self):
        return _exchange_sems(len(self.jobs)) if self.jobs else []


def _attn_fwd(qkv, tri_fwd, name, rider=None):
    lp = qkv.shape[0]
    n_pairs = (N_HEADS * HEAD_DIM) // 128
    nb = lp // ABLK
    assert nb <= 128 and 2 * HEAD_DIM == 128
    scale = _attn_scale()
    nt = (((1,), (1,)), ((), ()))
    rider = rider or _Rider()

    def body(q_ref, k_ref, v_ref, tri_ref, *rest):
        (o_ref, rs_ref), (r_s, acc_s, rs_s) = rider.split(rest, 2, 3)
        rider.start(pl.program_id(0) == 0)
        lane = lax.broadcasted_iota(jnp.int32, (ABLK, 128), 1)
        row = lax.broadcasted_iota(jnp.int32, (ABLK, 128), 0)
        in_a = lane < HEAD_DIM
        causal = lane < row

        def begin(st, i):
            r_s[st] = jnp.zeros(r_s.shape[1:], F32)
            acc_s[st] = jnp.zeros(acc_s.shape[1:], F32)
            rs_s[st] = jnp.full(rs_s.shape[1:], GONE, F32)
            return q_ref[pl.ds(pl.multiple_of(i * ABLK, ABLK), ABLK), :] * scale

        def live(st):
            least = jnp.min(jnp.minimum(r_s[st, 0], r_s[st, 1]), axis=0, keepdims=True)
            return (least[0, 0] < EXP_ZERO_AT).astype(jnp.int32)

        def step(*args):
            _interleave(stages(*args))

        def stages(st, q2, kb0, nblk, diag):
            k0 = pl.multiple_of(kb0 * ABLK, ABLK)
            kbd = _stack_blocks(k_ref[pl.ds(k0, nblk * ABLK), :], nblk, in_a)
            vbd = _stack_blocks(v_ref[pl.ds(k0, nblk * ABLK), :], nblk, in_a)
            z = lax.dot_general(q2, kbd, nt, preferred_element_type=F32)
            ncol = 2 * nblk
            zt = [z[:, c * 128:(c + 1) * 128] for c in range(ncol)]
            if diag:
                zt = [jnp.where(causal, t, MASKED) if c >= ncol - 2 else t for c, t in enumerate(zt)]
            yield
            bounds = [ncol * j // ATT_SPLIT for j in range(ATT_SPLIT + 1)]
            ce = [None] * ncol
            for c0, c1 in reversed([b for b in zip(bounds[:-1], bounds[1:]) if b[0] < b[1]]):
                parts = [_split_hi_lo(_softplus(zt[c])) for c in range(c0, c1)]
                got = jnp.dot(jnp.concatenate(parts, axis=0), tri_ref[...], preferred_element_type=F32)
                for c in range(c0, c1):
                    ce[c] = got[(c - c0) * 128:(c - c0 + 1) * 128]
                yield
            rr = [r_s[st, 0], r_s[st, 1]]
            rsv = [rs_s[st, :, :128], rs_s[st, :, 128:]]
            ws = [None] * ncol
            for u in reversed(range(nblk)):
                for hh in range(2):
                    c = 2 * u + hh
                    ws[c] = jnp.exp(zt[c] - ce[c][:, :128] - rr[hh]).astype(BF16)
                    rsv[hh] = jnp.where(lane == kb0 + u, rr[hh], rsv[hh])
                    rr[hh] = rr[hh] + ce[c][:, 128:]
            acc_s[st] += jnp.dot(jnp.concatenate(ws, axis=1), vbd, preferred_element_type=F32)
            r_s[st, 0] = rr[0]
            r_s[st, 1] = rr[1]
            rs_s[st, :, :128] = rsv[0]
            rs_s[st, :, 128:] = rsv[1]

        def finish(st, q2, i, n_top):
            i_low = i - n_top
            n_grp = i_low // ATT_GROUP

            def more(c):
                return (c[0] < n_grp) & (c[1] > 0)

            def inner(c):
                step(st, q2, i_low - ATT_GROUP * (c[0] + 1), ATT_GROUP, False)
                return c[0] + 1, live(st)

            _, alive = lax.while_loop(more, inner, (jnp.int32(0), live(st)))
            rem = i_low - ATT_GROUP * n_grp
            for p in _pow2_below(ATT_GROUP):
                def last_steps(p=p):
                    step(st, q2, rem & (p - 1), p, False)
                    return live(st)

                alive = lax.cond(((rem & p) != 0) & (alive > 0), last_steps, lambda alive=alive: alive)

            q0 = pl.multiple_of(i * ABLK, ABLK)
            o_ref[pl.ds(q0, ABLK), :] = acc_s[st].astype(BF16)
            rs_ref[pl.ds(q0, ABLK), :] = rs_s[st]

        def single(i):
            q2 = begin(0, i)
            n_top = jnp.minimum(i, ATT_TOP)
            for t in range(ATT_TOP + 1):
                @pl.when(n_top == t)
                def _():
                    step(0, q2, i - t, t + 1, True)

            finish(0, q2, i, n_top)

        def together(i):
            qs = [begin(st, i + st) for st in range(ATT_FLIGHT)]
            _interleave(*[stages(st, qs[st], i + st - ATT_TOP, ATT_TOP + 1, True) for st in range(ATT_FLIGHT)])
            for st in range(ATT_FLIGHT):
                finish(st, qs[st], i + st, ATT_TOP)

        n_head = min(ATT_TOP, nb)
        n_group = (nb - n_head) // ATT_FLIGHT
        n_single = nb - ATT_FLIGHT * n_group

        def singles(k, carry):
            single(jnp.where(k < n_head, k, nb - n_single + k))
            return carry

        def groups(j, carry):
            together(n_head + ATT_FLIGHT * j)
            return carry

        lax.fori_loop(0, n_single, singles, 0)
        lax.fori_loop(0, n_group, groups, 0)
        rider.wait(pl.program_id(0) == n_pairs - 1)

    col = lambda o: (lambda p: (0, p + o))
    return pl.pallas_call(
        body, out_shape=[SDS((lp, n_pairs * 128), BF16), SDS((lp, n_pairs * 256), F32)] + rider.out_shape,
        grid=(n_pairs,),
        in_specs=[pl.BlockSpec((lp, 128), col(0)), pl.BlockSpec((lp, 128), col(n_pairs)),
                  pl.BlockSpec((lp, 128), col(2 * n_pairs)), pl.BlockSpec((256, 256), lambda p: (0, 0))]
        + rider.in_specs(),
        out_specs=[pl.BlockSpec((lp, 128), col(0)), pl.BlockSpec((lp, 256), col(0))] + rider.out_specs(),
        scratch_shapes=[pltpu.VMEM((ATT_FLIGHT, 2, ABLK, 128), F32), pltpu.VMEM((ATT_FLIGHT, ABLK, 128), F32),
                        pltpu.VMEM((ATT_FLIGHT, ABLK, 256), F32)] + rider.scratch(),
        name=name, compiler_params=_cp())(qkv, qkv, qkv, tri_fwd, *rider.srcs)


def _attn_bwd(qkv, d_out, rsave, tri_after, tri_before, name, rider=None):
    lp = qkv.shape[0]
    n_pairs = (N_HEADS * HEAD_DIM) // 128
    nb = lp // ABLK
    scale = _attn_scale()
    nt = (((1,), (1,)), ((), ()))
    tn = (((0,), (0,)), ((), ()))
    rider = rider or _Rider()

    def body(q_ref, k_ref, v_ref, do_ref, rs_ref, ta_ref, tb_ref, *rest):
        (o_ref,), (dk_s, dv_s, dq_s, pc_s) = rider.split(rest, 1, 4)
        rider.start(pl.program_id(0) == 0)
        lane = lax.broadcasted_iota(jnp.int32, (ABLK, 128), 1)
        row = lax.broadcasted_iota(jnp.int32, (ABLK, 128), 0)
        in_a = lane < HEAD_DIM
        causal = lane < row
        dk_s[...] = jnp.zeros_like(dk_s)
        dv_s[...] = jnp.zeros_like(dv_s)

        def begin(st, i):
            q0 = pl.multiple_of(i * ABLK, ABLK)
            q2 = q_ref[pl.ds(q0, ABLK), :] * scale
            do2 = do_ref[pl.ds(q0, ABLK), :]
            dq_s[st] = jnp.zeros(dq_s.shape[1:], F32)
            pc_s[st] = jnp.zeros(pc_s.shape[1:], F32)
            return dict(q0=q0, q2=q2, do2=do2, q_st=_head_halves(q2, in_a), do_st=_head_halves(do2, in_a))

        def step(*args):
            _interleave(stages(*args))

        def stages(st, blk, kb0, nblk, diag):
            q0, q2, do2 = blk["q0"], blk["q2"], blk["do2"]
            k0 = pl.multiple_of(kb0 * ABLK, ABLK)
            kbd = _stack_blocks(k_ref[pl.ds(k0, nblk * ABLK), :], nblk, in_a)
            vbd = _stack_blocks(v_ref[pl.ds(k0, nblk * ABLK), :], nblk, in_a)
            z = lax.dot_general(q2, kbd, nt, preferred_element_type=F32)
            dw = lax.dot_general(do2, vbd, nt, preferred_element_type=F32)
            ncol = 2 * nblk
            zt = [z[:, c * 128:(c + 1) * 128] for c in range(ncol)]
            if diag:
                zt = [jnp.where(causal, t, MASKED) if c >= ncol - 2 else t for c, t in enumerate(zt)]
            bounds = [ncol * j // ATT_SPLIT for j in range(ATT_SPLIT + 1)]
            batches = [range(c0, c1) for c0, c1 in zip(bounds[:-1], bounds[1:]) if c0 < c1]
            sps, ex, ws, dls, pe = [None] * ncol, [None] * ncol, [None] * ncol, [None] * ncol, [None] * ncol

            def mass(cols):
                for c in cols:
                    sps[c] = _softplus(zt[c])
                got = jnp.dot(jnp.concatenate([_split_hi_lo(sps[c]) for c in cols], axis=0), ta_ref[...],
                              preferred_element_type=F32)
                for j, c in enumerate(cols):
                    ex[c] = got[j * 128:(j + 1) * 128]

            def weights(cols):
                for c in cols:
                    u, hh = c // 2, c % 2
                    r_saved = jnp.sum(jnp.where(lane == kb0 + u, rs_ref[pl.ds(q0, ABLK), hh * 128:(hh + 1) * 128],
                                                0.0), axis=1, keepdims=True)
                    w = jnp.exp(zt[c] - ex[c] - r_saved)
                    ws[c] = w.astype(BF16)
                    dls[c] = dw[:, c * 128:(c + 1) * 128] * w
                got = jnp.dot(jnp.concatenate([_split_hi_lo(dls[c]) for c in cols], axis=0), tb_ref[...],
                              preferred_element_type=F32)
                for j, c in enumerate(cols):
                    pe[c] = got[j * 128:(j + 1) * 128]

            yield
            mass(batches[0])
            yield
            for j in range(len(batches)):
                if j + 1 < len(batches):
                    mass(batches[j + 1])
                    yield
                weights(batches[j])
                yield
            pc = [pc_s[st, 0], pc_s[st, 1]]
            dzs = []
            for c in range(ncol):
                hh = c % 2
                one_minus_beta = jnp.exp(-sps[c])
                dz = dls[c] * one_minus_beta - (pe[c][:, :128] + pc[hh]) * (1.0 - one_minus_beta)
                pc[hh] = pc[hh] + pe[c][:, 128:]
                dzs.append(dz.astype(BF16))
            pc_s[st, 0] = pc[0]
            pc_s[st, 1] = pc[1]
            dq_s[st] += jnp.dot(jnp.concatenate(dzs, axis=1), kbd, preferred_element_type=F32)
            by_head = lambda ts: jnp.concatenate([jnp.concatenate(ts[0::2], axis=1), jnp.concatenate(ts[1::2], axis=1)],
                                                 axis=0)
            rows = pl.ds(k0, nblk * ABLK)
            dk_s[rows, :] += lax.dot_general(by_head(dzs), blk["q_st"], tn, preferred_element_type=F32)
            dv_s[rows, :] += lax.dot_general(by_head(ws), blk["do_st"], tn, preferred_element_type=F32)

        def below(st, blk, i, n_top):
            gone = jnp.min(rs_ref[pl.ds(blk["q0"], ABLK), :], axis=0, keepdims=True) >= EXP_ZERO_AT
            lane1 = lax.broadcasted_iota(jnp.int32, (1, 128), 1)
            first = jnp.sum(jnp.where(gone[:, :128] & gone[:, 128:] & (lane1 < i), 1.0, 0.0)).astype(jnp.int32)
            i_low = i - n_top
            n_grp = i_low // ATT_GROUP
            rem = i_low - ATT_GROUP * n_grp
            for p in reversed(_pow2_below(ATT_GROUP)):
                @pl.when(((rem & p) != 0) & ((rem & (p - 1)) + p > first))
                def _():
                    step(st, blk, rem & (p - 1), p, False)

            def inner(g, c2):
                step(st, blk, rem + ATT_GROUP * g, ATT_GROUP, False)
                return c2

            lax.fori_loop(jnp.maximum(first - rem, 0) // ATT_GROUP, n_grp, inner, 0)

        def done(st, blk):
            o_ref[0, pl.ds(blk["q0"], ABLK), :] = (dq_s[st] * scale).astype(BF16)

        def single(i):
            blk = begin(0, i)
            n_top = jnp.minimum(i, ATT_TOP)
            below(0, blk, i, n_top)
            for t in range(ATT_TOP + 1):
                @pl.when(n_top == t)
                def _():
                    step(0, blk, i - t, t + 1, True)

            done(0, blk)

        def together(i):
            blks = [begin(st, i + st) for st in range(ATT_FLIGHT)]
            for st in range(ATT_FLIGHT):
                below(st, blks[st], i + st, ATT_TOP)
            _interleave(*[stages(st, blks[st], i + st - ATT_TOP, ATT_TOP + 1, True) for st in range(ATT_FLIGHT)])
            for st in range(ATT_FLIGHT):
                done(st, blks[st])

        n_head = min(ATT_TOP, nb)
        n_group = (nb - n_head) // ATT_FLIGHT
        n_single = nb - ATT_FLIGHT * n_group

        def singles(k, carry):
            single(jnp.where(k < n_head, k, nb - n_single + k))
            return carry

        def groups(j, carry):
            together(n_head + ATT_FLIGHT * j)
            return carry

        lax.fori_loop(0, n_single, singles, 0)
        lax.fori_loop(0, n_group, groups, 0)
        o_ref[1] = dk_s[...].astype(BF16)
        o_ref[2] = dv_s[...].astype(BF16)
        rider.wait(pl.program_id(0) == n_pairs - 1)

    col = lambda o: (lambda p: (0, p + o))
    return pl.pallas_call(
        body, out_shape=[SDS((3, lp, n_pairs * 128), BF16)] + rider.out_shape, grid=(n_pairs,),
        in_specs=[pl.BlockSpec((lp, 128), col(0)), pl.BlockSpec((lp, 128), col(n_pairs)),
                  pl.BlockSpec((lp, 128), col(2 * n_pairs)), pl.BlockSpec((lp, 128), col(0)),
                  pl.BlockSpec((lp, 256), col(0)),
                  pl.BlockSpec((256, 128), lambda p: (0, 0)), pl.BlockSpec((256, 256), lambda p: (0, 0))]
        + rider.in_specs(),
        out_specs=[pl.BlockSpec((3, lp, 128), lambda p: (0, 0, p))] + rider.out_specs(),
        scratch_shapes=[pltpu.VMEM((lp, 128), F32), pltpu.VMEM((lp, 128), F32),
                        pltpu.VMEM((ATT_FLIGHT, ABLK, 128), F32), pltpu.VMEM((ATT_FLIGHT, 2, ABLK, 128), F32)]
        + rider.scratch(),
        name=name, compiler_params=_cp())(qkv, qkv, qkv, d_out, rsave, tri_after, tri_before, *rider.srcs)


def _conv_fwd_dw(cacg, w, b, name, rider=None):
    lp = cacg.shape[0]
    c = cacg.shape[1] // 2
    ncb = c // 128
    nchunk = lp // ABLK
    off = CONV_PAD - (CONV_K - 1)
    rider = rider or _Rider()

    def body(a_ref, g_ref, w_ref, b_ref, *rest):
        (y_ref,), (upad,) = rider.split(rest, 1, 1)
        rider.start(pl.program_id(0) == 0)
        upad[0:CONV_PAD, :] = jnp.zeros((CONV_PAD, 128), F32)

        def fill(ch, carry):
            base = pl.multiple_of(ch * ABLK, ABLK)
            upad[pl.ds(base + CONV_PAD, ABLK), :] = a_ref[pl.ds(base, ABLK), :] * _sigmoid(g_ref[pl.ds(base, ABLK), :])
            return carry

        lax.fori_loop(0, nchunk, fill, 0)

        def comp(ch, carry):
            base = pl.multiple_of(ch * ABLK, ABLK)
            acc = jnp.zeros((ABLK, 128), F32)
            for k in range(CONV_K):
                acc = acc + upad[pl.ds(base + (off + k), ABLK), :] * w_ref[k:k + 1, :]
            y_ref[pl.ds(base, ABLK), :] = acc + b_ref[...]
            return carry

        lax.fori_loop(0, nchunk, comp, 0)
        rider.wait(pl.program_id(0) == ncb - 1)

    return pl.pallas_call(
        body, out_shape=[SDS((lp, c), F32)] + rider.out_shape, grid=(ncb,),
        in_specs=[pl.BlockSpec((lp, 128), lambda j: (0, j)), pl.BlockSpec((lp, 128), lambda j: (0, j + ncb)),
                  pl.BlockSpec((CONV_PAD, 128), lambda j: (0, j)), pl.BlockSpec((1, 128), lambda j: (0, j))]
        + rider.in_specs(),
        out_specs=[pl.BlockSpec((lp, 128), lambda j: (0, j))] + rider.out_specs(),
        scratch_shapes=[pltpu.VMEM((lp + CONV_PAD, 128), F32)] + rider.scratch(), name=name,
        compiler_params=_cp())(cacg, cacg, w, b, *rider.srcs)


def _ln_parts(x, g, b):
    mu = jnp.mean(x, axis=-1, keepdims=True)
    xc = x - mu
    rstd = lax.rsqrt(jnp.mean(xc * xc, axis=-1, keepdims=True) + EPS)
    xh = xc * rstd
    return xh, rstd, xh * g + b


def _conv_fwd_ln(yc, g, b, name):
    lp, c = yc.shape
    tm = _tile(lp, EW_ROWS, 16)

    def body(y_ref, g_ref, b_ref, o_ref):
        _, _, ln = _ln_parts(y_ref[...], g_ref[...], b_ref[...])
        o_ref[...] = (ln * _sigmoid(ln)).astype(BF16)

    return pl.pallas_call(
        body, out_shape=SDS((lp, c), BF16), grid=(lp // tm,),
        in_specs=[pl.BlockSpec((tm, c), lambda i: (i, 0)), pl.BlockSpec((1, c), lambda i: (0, 0)),
                  pl.BlockSpec((1, c), lambda i: (0, 0))],
        out_specs=pl.BlockSpec((tm, c), lambda i: (i, 0)), name=name, compiler_params=_cp())(yc, g, b)


def _conv_bwd_ln(yc, dout, g, b, name):
    lp, c = yc.shape
    tm = _tile(lp, EW_ROWS, 16)

    def body(y_ref, d_ref, g_ref, b_ref, o_ref, dg_ref, db_ref):
        @pl.when(pl.program_id(0) == 0)
        def _():
            dg_ref[...] = jnp.zeros_like(dg_ref)
            db_ref[...] = jnp.zeros_like(db_ref)

        xh, rstd, ln = _ln_parts(y_ref[...], g_ref[...], b_ref[...])
        s = _sigmoid(ln)
        dln = d_ref[...] * (s * (1.0 + ln * (1.0 - s)))
        dg_ref[...] += _sum8(dln * xh)
        db_ref[...] += _sum8(dln)
        dxh = dln * g_ref[...]
        o_ref[...] = rstd * (dxh - jnp.mean(dxh, axis=-1, keepdims=True)
                             - xh * jnp.mean(dxh * xh, axis=-1, keepdims=True))

    return pl.pallas_call(
        body, out_shape=[SDS((lp, c), F32), SDS((8, c), F32), SDS((8, c), F32)], grid=(lp // tm,),
        in_specs=[pl.BlockSpec((tm, c), lambda i: (i, 0)), pl.BlockSpec((tm, c), lambda i: (i, 0)),
                  pl.BlockSpec((1, c), lambda i: (0, 0)), pl.BlockSpec((1, c), lambda i: (0, 0))],
        out_specs=[pl.BlockSpec((tm, c), lambda i: (i, 0)), pl.BlockSpec((8, c), lambda i: (0, 0)),
                   pl.BlockSpec((8, c), lambda i: (0, 0))],
        name=name, compiler_params=_cp())(yc, dout, g, b)


def _conv_bwd_dw(dyc, cacg, w, name):
    lp, c = dyc.shape
    ncb = c // 128
    nchunk = lp // ABLK
    off = CONV_PAD - (CONV_K - 1)

    def body(dy_ref, a_ref, g_ref, w_ref, dcc_ref, dw_ref, db_ref, upad, dypad, dwacc):
        upad[0:CONV_PAD, :] = jnp.zeros((CONV_PAD, 128), F32)
        dypad[lp:lp + CONV_PAD, :] = jnp.zeros((CONV_PAD, 128), F32)
        dwacc[...] = jnp.zeros_like(dwacc)
        db_ref[...] = jnp.zeros_like(db_ref)

        def fill(ch, carry):
            base = pl.multiple_of(ch * ABLK, ABLK)
            upad[pl.ds(base + CONV_PAD, ABLK), :] = a_ref[pl.ds(base, ABLK), :] * _sigmoid(g_ref[pl.ds(base, ABLK), :])
            dypad[pl.ds(base, ABLK), :] = dy_ref[pl.ds(base, ABLK), :]
            return carry

        lax.fori_loop(0, nchunk, fill, 0)

        def comp(ch, carry):
            base = pl.multiple_of(ch * ABLK, ABLK)
            dy = dy_ref[pl.ds(base, ABLK), :]
            du = jnp.zeros((ABLK, 128), F32)
            for k in range(CONV_K):
                du = du + dypad[pl.ds(base + (CONV_K - 1 - k), ABLK), :] * w_ref[k:k + 1, :]
                dwacc[k * 8:(k + 1) * 8, :] += _sum8(dy * upad[pl.ds(base + (off + k), ABLK), :])
            db_ref[...] += _sum8(dy)
            a = a_ref[pl.ds(base, ABLK), :]
            s = _sigmoid(g_ref[pl.ds(base, ABLK), :])
            dcc_ref[0, pl.ds(base, ABLK), :] = (du * s).astype(BF16)
            dcc_ref[1, pl.ds(base, ABLK), :] = (du * a * (s * (1.0 - s))).astype(BF16)
            return carry

        lax.fori_loop(0, nchunk, comp, 0)
        dw_ref[...] = dwacc[...].reshape(CONV_PAD, 8, 128).sum(axis=1)

    return pl.pallas_call(
        body, out_shape=[SDS((2, lp, c), BF16), SDS((CONV_PAD, c), F32), SDS((8, c), F32)],
        grid=(ncb,),
        in_specs=[pl.BlockSpec((lp, 128), lambda j: (0, j)), pl.BlockSpec((lp, 128), lambda j: (0, j)),
                  pl.BlockSpec((lp, 128), lambda j: (0, j + ncb)), pl.BlockSpec((CONV_PAD, 128), lambda j: (0, j))],
        out_specs=[pl.BlockSpec((2, lp, 128), lambda j: (0, 0, j)),
                   pl.BlockSpec((CONV_PAD, 128), lambda j: (0, j)), pl.BlockSpec((8, 128), lambda j: (0, j))],
        scratch_shapes=[pltpu.VMEM((lp + CONV_PAD, 128), F32), pltpu.VMEM((lp + CONV_PAD, 128), F32),
                        pltpu.VMEM((CONV_PAD * 8, 128), F32)],
        name=name, compiler_params=_cp())(dyc, cacg, cacg, w)


def _mesh_pos():
    x, y, c = lax.axis_index("x"), lax.axis_index("y"), lax.axis_index("c")
    return x, y, c


def _peer(pos, r):
    x, y, c = pos
    px = (1 - x) if (r >> 2) & 1 else x
    py = (1 - y) if (r >> 1) & 1 else y
    pc = (1 - c) if r & 1 else c
    return (px, py, pc), 4 * px + 2 * py + pc


SIBLING = 1
OTHER_CHIPS = (2, 4, 6)


class _Job:
    def __init__(self, src, dst, scatter, src_layer=None, dst_layer=None):
        self.src, self.dst, self.scatter, self.src_layer, self.dst_layer = src, dst, scatter, src_layer, dst_layer

    def src_view(self, ins, idx):
        v = ins[self.src] if self.src_layer is None else ins[self.src].at[self.src_layer]
        return v.at[idx] if self.scatter else v

    def dst_view(self, outs, slot):
        v = outs[self.dst] if self.dst_layer is None else outs[self.dst].at[self.dst_layer]
        return v.at[slot]


def _remote(job, j, r, src, dst, to, send, recv):
    return pltpu.make_async_remote_copy(src_ref=src, dst_ref=dst, send_sem=send.at[j, r - 1], recv_sem=recv.at[j, r - 1],
                                        device_id=to, device_id_type=pl.DeviceIdType.MESH)


def _exchange_start(jobs, ins, outs, send, recv, loc):
    pos = _mesh_pos()
    me = 4 * pos[0] + 2 * pos[1] + pos[2]
    for j, job in enumerate(jobs):
        pltpu.make_async_copy(job.src_view(ins, me), job.dst_view(outs, me), loc.at[j]).start()
        for r in (range(1, N_DEV) if job.scatter else (SIBLING,) + OTHER_CHIPS):
            peer, peer_idx = _peer(pos, r)
            _remote(job, j, r, job.src_view(ins, peer_idx), job.dst_view(outs, me), peer, send, recv).start()


def _exchange_forward(jobs, ins, outs, send, recv, loc):
    pos = _mesh_pos()
    sibling, _ = _peer(pos, SIBLING)
    for j, job in enumerate(jobs):
        if job.scatter:
            continue
        for r in OTHER_CHIPS:
            peer, peer_idx = _peer(pos, r)
            slot = job.dst_view(outs, peer_idx)
            _remote(job, j, r, job.src_view(ins, peer_idx), slot, peer, send, recv).wait_recv()
            _remote(job, j, r ^ SIBLING, slot, slot, sibling, send, recv).start()


def _exchange_wait(jobs, ins, outs, send, recv, loc):
    pos = _mesh_pos()
    me = 4 * pos[0] + 2 * pos[1] + pos[2]
    for j, job in enumerate(jobs):
        for r in range(1, N_DEV):
            peer, peer_idx = _peer(pos, r)
            cp = _remote(job, j, r, job.src_view(ins, peer_idx), job.dst_view(outs, peer_idx), peer, send, recv)
            if job.scatter or r not in OTHER_CHIPS:
                cp.wait_recv()
            cp.wait_send()
        pltpu.make_async_copy(job.src_view(ins, me), job.dst_view(outs, me), loc.at[j]).wait()


def _exchange_sems(n_jobs):
    return [pltpu.SemaphoreType.DMA((n_jobs, N_DEV - 1)), pltpu.SemaphoreType.DMA((n_jobs, N_DEV - 1)),
            pltpu.SemaphoreType.DMA((n_jobs,))]


def _exchange(jobs, arrs, out_shape, name):
    n_in, n_out = len(arrs), len(out_shape)
    any_spec = pl.BlockSpec(memory_space=pl.ANY)

    def body(*refs):
        ins, outs, sems = refs[:n_in], refs[n_in:n_in + n_out], refs[n_in + n_out:]
        _exchange_start(jobs, ins, outs, *sems)
        _exchange_forward(jobs, ins, outs, *sems)
        _exchange_wait(jobs, ins, outs, *sems)

    return pl.pallas_call(
        body, out_shape=out_shape, in_specs=[any_spec] * n_in, out_specs=[any_spec] * n_out,
        scratch_shapes=_exchange_sems(len(jobs)),
        name=name, compiler_params=pltpu.CompilerParams(has_side_effects=True))(*arrs)


def _all_reduce_small(placed, cuts, rows, w, name):
    n_in = len(placed)

    def body(*refs):
        in_refs, o_refs = refs[:n_in], refs[n_in:n_in + len(cuts)]
        buf, send, recv = refs[n_in + len(cuts):]
        pos = _mesh_pos()
        me = 4 * pos[0] + 2 * pos[1] + pos[2]
        buf[me] = jnp.zeros((rows, w), F32)
        for ref, (arr, row0, col0, is_partial) in zip(in_refs, placed):
            val = ref[...].sum(axis=0, keepdims=True) if is_partial else ref[...]
            buf[me, row0:row0 + val.shape[0], col0:col0 + val.shape[1]] = val
        for r in range(1, N_DEV):
            peer, _ = _peer(pos, r)
            pltpu.make_async_remote_copy(src_ref=buf.at[me], dst_ref=buf.at[me], send_sem=send.at[r - 1],
                                         recv_sem=recv.at[r - 1], device_id=peer,
                                         device_id_type=pl.DeviceIdType.MESH).start()
        for r in range(1, N_DEV):
            peer, peer_idx = _peer(pos, r)
            cp = pltpu.make_async_remote_copy(src_ref=buf.at[me], dst_ref=buf.at[peer_idx], send_sem=send.at[r - 1],
                                              recv_sem=recv.at[r - 1], device_id=peer,
                                              device_id_type=pl.DeviceIdType.MESH)
            cp.wait_recv()
            cp.wait_send()
        acc = buf[0]
        for dev in range(1, N_DEV):
            acc = acc + buf[dev]
        for o_ref, (_, pieces) in zip(o_refs, cuts):
            for index, row0, nrows, col0, ncols in pieces:
                o_ref[index] = acc[row0:row0 + nrows, col0:col0 + ncols].reshape(o_ref.at[index].shape)

    vmem = pl.BlockSpec(memory_space=pltpu.VMEM)
    return pl.pallas_call(
        body, out_shape=[SDS(shape, F32) for shape, _ in cuts], in_specs=[vmem] * n_in, out_specs=[vmem] * len(cuts),
        scratch_shapes=[pltpu.VMEM((N_DEV, rows, w), F32), pltpu.SemaphoreType.DMA((N_DEV - 1,)),
                        pltpu.SemaphoreType.DMA((N_DEV - 1,))],
        name=name, compiler_params=pltpu.CompilerParams(has_side_effects=True))(*[p[0] for p in placed])


def _adamw_math(w, g, m, v):
    m = ADAM_B1 * m + (1.0 - ADAM_B1) * g
    v = ADAM_B2 * v + (1.0 - ADAM_B2) * (g * g)
    m_hat = m / (1.0 - ADAM_B1 ** ADAM_STEP)
    v_hat = v / (1.0 - ADAM_B2 ** ADAM_STEP)
    delta = -ADAM_LR * (m_hat / (jnp.sqrt(v_hat) + ADAM_EPS) + ADAM_WD * w)
    return delta, m, v


def _adamw_shard(parts, w, m, v, name):
    depth = len(parts)
    _, rr, cc = parts[0].shape
    tr = _tile(rr, 256, 8)
    nt = rr // tr
    part_block, blk = (N_DEV, tr, cc), pl.BlockSpec((None, tr, cc), lambda l, i: (l, i, 0))

    def body(*refs):
        p_refs = refs[:depth]
        w_ref, m_ref, v_ref, g_out, d_out, m_out, v_out = refs[depth:]
        for li in range(depth):
            @pl.when(pl.program_id(0) == li)
            def _(p_ref=p_refs[li]):
                g = p_ref[0].astype(F32)
                for dev in range(1, N_DEV):
                    g = g + p_ref[dev].astype(F32)
                delta, mm, vv = _adamw_math(w_ref[...], g, m_ref[...], v_ref[...])
                g_out[...] = g
                d_out[...] = delta
                m_out[...] = mm
                v_out[...] = vv

    def part_spec(li):
        return pl.BlockSpec(part_block, lambda l, i: (0, jnp.where(l == li, i, jnp.where(l < li, 0, nt - 1)), 0))

    return pl.pallas_call(
        body, out_shape=[SDS(w.shape, F32)] * 4, grid=(depth, nt),
        in_specs=[part_spec(li) for li in range(depth)] + [blk, blk, blk],
        out_specs=[blk] * 4, name=name, compiler_params=_cp())(*parts, w, m, v)


def _adamw_small(gs, ws, ms, vs, name):
    n = len(gs)

    def body(*refs):
        g_refs, w_refs, m_refs, v_refs = (refs[k * n:(k + 1) * n] for k in range(4))
        outs = refs[4 * n:]
        for k in range(n):
            delta, mm, vv = _adamw_math(w_refs[k][...], g_refs[k][...], m_refs[k][...], v_refs[k][...])
            outs[k][...] = delta
            outs[n + k][...] = mm
            outs[2 * n + k][...] = vv

    res = pl.pallas_call(body, out_shape=[SDS(w.shape, F32) for w in ws] * 3, name=name,
                         compiler_params=_cp())(*gs, *ws, *ms, *vs)
    return res[:n], res[n:2 * n], res[2 * n:]


def _from_cols(t):
    return jnp.transpose(t, (1, 0, 2)).reshape(t.shape[1], N_DEV * t.shape[2])


def _swap(t):
    return jnp.swapaxes(t, -1, -2)


def kernel(x, meta_tokens, mix_norm_g, w_in, conv_dw_w, conv_dw_b, conv_ln_g, conv_ln_b, w_out, ffn_norm_g, w_gate, w_up, w_down, final_norm_g, loss_target, m_meta_tokens, m_mix_norm_g, m_w_in, m_conv_dw_w, m_conv_dw_b, m_conv_ln_g, m_conv_ln_b, m_w_out, m_ffn_norm_g, m_w_gate, m_w_up, m_w_down, m_final_norm_g, v_meta_tokens, v_mix_norm_g, v_w_in, v_conv_dw_w, v_conv_dw_b, v_conv_ln_g, v_conv_ln_b, v_w_out, v_ffn_norm_g, v_w_gate, v_w_up, v_w_down, v_final_norm_g):
    depth, d, in_shard = w_in.shape
    seq = x.shape[1]
    sb = N_HEADS * HEAD_DIM
    cc = conv_dw_w.shape[2] * N_DEV
    ff = w_gate.shape[2] * N_DEV
    assert in_shard * N_DEV == 3 * sb + 2 * cc and x.shape[0] == 1
    lr = N_META + seq
    lp = -(-lr // ABLK) * ABLK
    me = 4 * lax.axis_index("x") + 2 * lax.axis_index("y") + lax.axis_index("c")

    big_names = ("w_in", "w_out", "w_gate", "w_up", "w_down")
    transposed = {"w_in": True, "w_out": False, "w_gate": True, "w_up": True, "w_down": False}
    shard = dict(w_in=_swap(w_in).astype(BF16), w_out=w_out.astype(BF16), w_gate=_swap(w_gate).astype(BF16),
                 w_up=_swap(w_up).astype(BF16), w_down=w_down.astype(BF16))

    def gather_of(keys):
        names = sorted({n for n, _ in keys}, key=big_names.index)
        jobs = [_Job(names.index(n), j, False, src_layer=i) for j, (n, i) in enumerate(keys)]
        return jobs, [shard[n] for n in names], [SDS((N_DEV,) + shard[n].shape[1:], BF16) for n, _ in keys]

    first_keys = [("w_in", 0)]

    def riding(keys):
        return _Rider(*gather_of(keys)) if keys else None

    def receive(keys, arrays):
        for (n, li), t in zip(keys, arrays):
            wl[li][n] = t.reshape(-1, d)

    jobs, srcs, out_shape = gather_of(first_keys)
    for extra in (meta_tokens, conv_dw_w):
        jobs.append(_Job(len(srcs), len(out_shape), False))
        srcs.append(extra)
        out_shape.append(SDS((N_DEV,) + extra.shape, F32))
    gathered = _exchange(jobs, srcs, out_shape, "gather_first")
    wl = [dict() for _ in range(depth)]
    receive(first_keys, gathered)
    meta_full = _from_cols(gathered[-2])
    taps = jnp.transpose(gathered[-1], (1, 2, 0, 3)).reshape(depth, CONV_K, cc)
    taps = jnp.pad(taps, ((0, 0), (0, CONV_PAD - CONV_K), (0, 0)))
    tri_fwd, tri_after, tri_before = _tri_consts()

    h = jnp.concatenate([meta_full, x[0], jnp.zeros((lp - lr, d), F32)], axis=0)
    saved = []
    for i in range(depth):
        p = wl[i]
        sv = dict(h_in=h)
        qkv, hn = _rms_mm(h, mix_norm_g[i:i + 1], p["w_in"], 0, 3 * sb, BF16, f"proj_qkv_{i}", True)
        keys = [("w_out", 0)] if i == 0 else []
        cacg, *arrived = _rms_mm(h, mix_norm_g[i:i + 1], p["w_in"], 3 * sb, 2 * cc, F32, f"proj_conv_{i}", False,
                                 riding(keys))
        receive(keys, arrived)
        keys = [("w_gate", i), ("w_up", i)]
        attn, rsave, *arrived = _attn_fwd(qkv, tri_fwd, f"attn_fwd_{i}", riding(keys))
        receive(keys, arrived)
        keys = [("w_down", 0)] if i == 0 else []
        yc, *arrived = _conv_fwd_dw(cacg, taps[i], conv_dw_b[i:i + 1], f"conv_fwd_dw_{i}", riding(keys))
        receive(keys, arrived)
        conv = _conv_fwd_ln(yc, conv_ln_g[i:i + 1], conv_ln_b[i:i + 1], f"conv_fwd_ln_{i}")
        h = _mix_out(attn, conv, p["w_out"], h, f"mix_out_{i}")
        sv.update(qkv=qkv, hn=hn, cacg=cacg, rsave=rsave, yc=yc, attn=attn, conv=conv, h_mid=h)
        nxt = i + 1
        keys = [("w_in", nxt), ("w_out", nxt), ("w_down", nxt)] if nxt < depth else []
        h, hn2, act, gate, up, *arrived = _ffn_fwd(h, ffn_norm_g[i:i + 1], p["w_gate"], p["w_up"], p["w_down"],
                                                   f"ffn_fwd_{i}", riding(keys))
        receive(keys, arrived)
        sv.update(hn2=hn2, act=act, gate=gate, up=up)
        saved.append(sv)

    tpad = jnp.pad(loss_target[0], ((N_META, lp - lr), (0, 0)))
    dh, dg_final, loss_part = _loss_head(h, tpad, final_norm_g.reshape(1, d), seq, "loss_head")
    loss = lax.psum(loss_part[0, 0], MESH_AXES)

    parts = {}

    def sending(items):
        srcs = [t.reshape(N_DEV, t.shape[0] // N_DEV, d) for _, t in items]
        return _Rider([_Job(j, j, True) for j in range(len(items))], srcs, [SDS(t.shape, BF16) for t in srcs])

    def arrive(items, arrays):
        parts.update({key: t for (key, _), t in zip(items, arrays)})

    grads = [None] * depth
    from_above = []
    for i in reversed(range(depth)):
        p, sv = wl[i], saved[i]
        d_g, d_u, *got = _ffn_bwd_act(dh, sv["gate"], sv["up"], p["w_down"], f"ffn_bwd_act_{i}", sending(from_above))
        arrive(from_above, got)
        gw_down = _mm_tn(sv["act"], dh, f"grad_w_down_{i}")
        gw_gate = _mm_tn(d_g, sv["hn2"], f"grad_w_gate_{i}")
        gw_up = _mm_tn(d_u, sv["hn2"], f"grad_w_up_{i}")
        items = [(("w_down", i), gw_down)]
        dh, dg_ffn, *got = _ffn_bwd_in(d_g, d_u, p["w_gate"], p["w_up"], sv["h_mid"], ffn_norm_g[i:i + 1], dh,
                                       f"ffn_bwd_in_{i}", sending(items))
        arrive(items, got)
        gw_out = _grad_w_out(sv["attn"], sv["conv"], dh, f"grad_w_out_{i}")
        d_attn, d_conv = _mix_bwd(dh, p["w_out"], sb, cc, f"mix_bwd_{i}")
        items = [(("w_gate", i), gw_gate), (("w_up", i), gw_up)]
        dqkv, *got = _attn_bwd(sv["qkv"], d_attn, sv["rsave"], tri_after, tri_before, f"attn_bwd_{i}", sending(items))
        arrive(items, got)
        dyc, dg_ln, db_ln = _conv_bwd_ln(sv["yc"], d_conv, conv_ln_g[i:i + 1], conv_ln_b[i:i + 1], f"conv_bwd_ln_{i}")
        dcc, g_taps_i, db_conv = _conv_bwd_dw(dyc, sv["cacg"], taps[i], f"conv_bwd_dw_{i}")
        items = [(("w_out", i), gw_out)]
        gw_in, *got = _grad_w_in_t(dqkv, dcc, sv["hn"], f"grad_w_in_{i}", sending(items))
        arrive(items, got)
        from_above = [(("w_in", i), gw_in)]
        items = from_above if i == 0 else []
        dh, dg_mix, *got = _mix_bwd_in(dqkv, dcc, p["w_in"], sv["h_in"], mix_norm_g[i:i + 1], dh, f"mix_bwd_in_{i}",
                                       sending(items))
        arrive(items, got)
        grads[i] = dict(taps=g_taps_i, dg_mix=dg_mix, dg_ffn=dg_ffn, dg_ln=dg_ln, db_ln=db_ln, db_conv=db_conv)
    grad_x = dh[N_META:lr][None]

    per = d // depth
    assert depth * cc <= d and d % depth == 0 and per % 128 == 0
    placed, cuts, row = [], [], 0
    for key in ("dg_mix", "dg_ffn"):
        placed += [(grads[i][key], row + i, 0, True) for i in range(depth)]
        cuts.append(((depth, d), [(slice(0, depth), row, depth, 0, d)]))
        row += depth
    placed.append((dg_final, row, 0, True))
    cuts.append(((1, d), [(slice(0, 1), row, 1, 0, d)]))
    row += 1
    for key in ("db_conv", "dg_ln", "db_ln"):
        placed += [(grads[i][key], row, i * per, True) for i in range(depth)]
        cuts.append(((depth, cc), [(slice(i, i + 1), row, 1, i * per, cc) for i in range(depth)]))
        row += 1
    placed += [(grads[i]["taps"], row, i * per, False) for i in range(depth)]
    cuts.append(((depth, CONV_PAD, cc), [(i, row, CONV_PAD, i * per, cc) for i in range(depth)]))
    row += CONV_PAD
    placed.append((dh[:N_META], row, 0, False))
    cuts.append(((N_META, d), [(slice(0, N_META), row, N_META, 0, d)]))
    row += N_META
    g_mix, g_ffn, g_final, g_cb, g_lg, g_lb, g_taps_full, g_meta_full = _all_reduce_small(
        placed, cuts, -(-row // 8) * 8, d, "reduce_small")
    csh = cc // N_DEV
    g_taps_own = lax.dynamic_slice_in_dim(g_taps_full[:, :CONV_K], me * csh, csh, axis=2)
    msh = d // N_DEV
    g_meta_own = lax.dynamic_slice_in_dim(g_meta_full, me * msh, msh, axis=1)

    row1 = lambda t: t.reshape(1, d)
    small_g = [g_meta_own, g_mix, g_taps_own, g_cb, g_lg, g_lb, g_ffn, g_final]
    small_w = [meta_tokens, mix_norm_g, conv_dw_w, conv_dw_b, conv_ln_g, conv_ln_b, ffn_norm_g, row1(final_norm_g)]
    small_m = [m_meta_tokens, m_mix_norm_g, m_conv_dw_w, m_conv_dw_b, m_conv_ln_g, m_conv_ln_b, m_ffn_norm_g,
               row1(m_final_norm_g)]
    small_v = [v_meta_tokens, v_mix_norm_g, v_conv_dw_w, v_conv_dw_b, v_conv_ln_g, v_conv_ln_b, v_ffn_norm_g,
               row1(v_final_norm_g)]
    s_delta, s_m, s_v = _adamw_small(small_g, small_w, small_m, small_v, "adamw_small")
    unrow = lambda ts: list(ts[:-1]) + [ts[-1].reshape(d)]
    small_g, s_delta, s_m, s_v = unrow(small_g), unrow(s_delta), unrow(s_m), unrow(s_v)

    big = []
    for n, w, m, v in zip(big_names, (w_in, w_out, w_gate, w_up, w_down), (m_w_in, m_w_out, m_w_gate, m_w_up, m_w_down),
                          (v_w_in, v_w_out, v_w_gate, v_w_up, v_w_down)):
        fix = _swap if transposed[n] else (lambda t: t)
        res = _adamw_shard([parts[(n, i)] for i in range(depth)], fix(w), fix(m), fix(v), f"adamw_{n}")
        big.append([fix(t) for t in res])
    b_in, b_out, b_gate, b_up, b_down = big

    def ordered(k, smalls):
        s_meta, s_mix, s_taps, s_cb, s_lg, s_lb, s_ffn, s_final = smalls
        return [s_meta, s_mix, b_in[k], s_taps, s_cb, s_lg, s_lb, b_out[k], s_ffn, b_gate[k], b_up[k], b_down[k], s_final]

    return (loss, grad_x, *ordered(0, small_g), *ordered(1, s_delta), *ordered(2, s_m), *ordered(3, s_v))
```

```python
import math

import jax
import jax.numpy as jnp
from jax import lax
from jax.experimental import pallas as pl
from jax.experimental.pallas import tpu as pltpu

F32 = jnp.float32
BF16 = jnp.bfloat16
SDS = jax.ShapeDtypeStruct

N_META = 16
N_HEADS = 8
HEAD_DIM = 64
CONV_K = 31
CONV_PAD = 32
ABLK = 128
ATT_GROUP = 4
ATT_SPLIT = 2
ATT_TOP = 2
ATT_FLIGHT = 3
EXP_ZERO_AT = 104.0
GONE = 1e30
MASKED = -1e30
FF_CHUNK = 256
FF_BLOCK = 1408
FF_ROWS = 528
EPS = 1e-6
N_DEV = 8
MESH_AXES = ("x", "y", "c")
ADAM_LR = 0.001
ADAM_B1 = 0.9
ADAM_B2 = 0.999
ADAM_EPS = 1e-08
ADAM_WD = 0.01
ADAM_STEP = 10
MIB = 1 << 20
VMEM_LIMIT_MIB = 48
FFN_FWD_VMEM_MIB = 56
MM_ROWS = 1056
EW_ROWS = 528


def _cp(mib=None):
    return pltpu.CompilerParams(vmem_limit_bytes=(mib or VMEM_LIMIT_MIB) * MIB)


def _tile(n, cap, mult):
    best = None
    for t in range(mult, min(n, cap) + 1, mult):
        if n % t == 0:
            best = t
    assert best is not None, (n, cap, mult)
    return best


def _sum8(v):
    r, c = v.shape
    return v.reshape(r // 8, 8, c).sum(axis=0)


def _sigmoid(x):
    return 1.0 / (1.0 + jnp.exp(-x))


def _ff_block(f):
    fb = _tile(f, FF_BLOCK, 128)
    return fb, [(s0, min(FF_CHUNK, fb - s0)) for s0 in range(0, fb, FF_CHUNK)]


NT = (((1,), (1,)), ((), ()))
TN = (((0,), (0,)), ((), ()))


def _rms_mm(h, g, wt, w_row0, n_cols, out_dtype, name, hn_out, rider=None):
    lp, d = h.shape
    tm = _tile(lp, MM_ROWS, 16)
    tn = _tile(math.gcd(n_cols, w_row0), 768, 128)
    off = w_row0 // tn
    ni, nj = lp // tm, n_cols // tn
    rider = rider or _Rider()

    def body(h_ref, g_ref, w_ref, *rest):
        outs, (hn_s,) = rider.split(rest, 2 if hn_out else 1, 1)
        o_ref = outs[0]
        hn_ref = outs[1] if hn_out else None
        rider.start((pl.program_id(0) == 0) & (pl.program_id(1) == 0))
        if ni > 1:
            rider.forward((pl.program_id(0) == ni - 1) & (pl.program_id(1) == 0))

        @pl.when(pl.program_id(1) == 0)
        def _():
            x = h_ref[...]
            r = lax.rsqrt(jnp.mean(x * x, axis=-1, keepdims=True) + EPS)
            hn = ((x * r) * g_ref[...]).astype(BF16)
            hn_s[...] = hn
            if hn_out:
                hn_ref[...] = hn

        o_ref[...] = lax.dot_general(hn_s[...], w_ref[...], NT, preferred_element_type=F32).astype(out_dtype)
        rider.wait((pl.program_id(0) == ni - 1) & (pl.program_id(1) == nj - 1))

    out_shape = [SDS((lp, n_cols), out_dtype)]
    out_specs = [pl.BlockSpec((tm, tn), lambda i, j: (i, j))]
    if hn_out:
        out_shape.append(SDS((lp, d), BF16))
        out_specs.append(pl.BlockSpec((tm, d), lambda i, j: (i, 0)))
    return pl.pallas_call(
        body, out_shape=out_shape + rider.out_shape, grid=(ni, nj),
        in_specs=[pl.BlockSpec((tm, d), lambda i, j: (i, 0)),
                  pl.BlockSpec((1, d), lambda i, j: (0, 0)),
                  pl.BlockSpec((tn, d), lambda i, j: (j + off, 0))] + rider.in_specs(),
        out_specs=out_specs + rider.out_specs(), scratch_shapes=[pltpu.VMEM((tm, d), BF16)] + rider.scratch(),
        name=name, compiler_params=_cp())(h, g, wt, *rider.srcs)


def _mix_bwd(dh, w, sb, cc, name):
    m, k = dh.shape
    tm = _tile(m, MM_ROWS, 16)

    def body(a_ref, w_ref, attn_ref, conv_ref):
        a = a_ref[...].astype(BF16)
        attn_ref[...] = lax.dot_general(a, w_ref[0:sb, :], NT, preferred_element_type=F32).astype(BF16)
        conv_ref[...] = lax.dot_general(a, w_ref[sb:sb + cc, :], NT, preferred_element_type=F32)

    return pl.pallas_call(
        body, out_shape=[SDS((m, sb), BF16), SDS((m, cc), F32)], grid=(m // tm,),
        in_specs=[pl.BlockSpec((tm, k), lambda i: (i, 0)), pl.BlockSpec((sb + cc, k), lambda i: (0, 0))],
        out_specs=[pl.BlockSpec((tm, sb), lambda i: (i, 0)), pl.BlockSpec((tm, cc), lambda i: (i, 0))],
        name=name, compiler_params=_cp())(dh, w)


def _mix_out(attn, conv, w, res, name):
    m, ka = attn.shape
    kc = conv.shape[1]
    n = w.shape[1]
    assert ka == kc
    tm = _tile(m, MM_ROWS, 16)
    tn = _tile(n, 512, 128)

    def body(a_ref, c_ref, wa_ref, wc_ref, r_ref, o_ref):
        o_ref[...] = (r_ref[...] + jnp.dot(a_ref[...], wa_ref[...], preferred_element_type=F32)
                      + jnp.dot(c_ref[...], wc_ref[...], preferred_element_type=F32))

    return pl.pallas_call(
        body, out_shape=SDS((m, n), F32), grid=(m // tm, n // tn),
        in_specs=[pl.BlockSpec((tm, ka), lambda i, j: (i, 0)), pl.BlockSpec((tm, kc), lambda i, j: (i, 0)),
                  pl.BlockSpec((ka, tn), lambda i, j: (0, j)), pl.BlockSpec((kc, tn), lambda i, j: (1, j)),
                  pl.BlockSpec((tm, tn), lambda i, j: (i, j))],
        out_specs=pl.BlockSpec((tm, tn), lambda i, j: (i, j)), name=name, compiler_params=_cp())(attn, conv, w, w, res)


def _mm_tn(a, b, name):
    l, m = a.shape
    n = b.shape[1]
    tm = _tile(m, 1408, 128)
    tn = _tile(n, 1024, 128)
    tl = _tile(l, 1408, 128)
    nl = l // tl

    def body(a_ref, b_ref, o_ref, acc):
        @pl.when(pl.program_id(2) == 0)
        def _():
            acc[...] = jnp.zeros_like(acc)

        acc[...] += lax.dot_general(a_ref[...].astype(BF16), b_ref[...].astype(BF16), TN, preferred_element_type=F32)

        @pl.when(pl.program_id(2) == nl - 1)
        def _():
            o_ref[...] = acc[...].astype(BF16)

    return pl.pallas_call(
        body, out_shape=SDS((m, n), BF16), grid=(m // tm, n // tn, nl),
        in_specs=[pl.BlockSpec((tl, tm), lambda i, j, s: (s, i)),
                  pl.BlockSpec((tl, tn), lambda i, j, s: (s, j))],
        out_specs=pl.BlockSpec((tm, tn), lambda i, j, s: (i, j)),
        scratch_shapes=[pltpu.VMEM((tm, tn), F32)], name=name, compiler_params=_cp())(a, b)


def _grad_w_out(attn, conv, dh, name):
    l, w = attn.shape
    d = dh.shape[1]
    assert conv.shape == attn.shape
    tn = _tile(d, 1024, 128)
    tl = _tile(l, 1408, 128)
    nl = l // tl

    def body(a_ref, c_ref, b_ref, o_ref, acc):
        p, s = pl.program_id(0), pl.program_id(2)

        @pl.when(s == 0)
        def _():
            acc[...] = jnp.zeros_like(acc)

        @pl.when(p == 0)
        def _():
            acc[...] += lax.dot_general(a_ref[...], b_ref[...].astype(BF16), TN, preferred_element_type=F32)

        @pl.when(p == 1)
        def _():
            acc[...] += lax.dot_general(c_ref[...], b_ref[...].astype(BF16), TN, preferred_element_type=F32)

        @pl.when(s == nl - 1)
        def _():
            o_ref[...] = acc[...].astype(BF16)

    out = pl.pallas_call(
        body, out_shape=SDS((2, w, d), BF16), grid=(2, d // tn, nl),
        in_specs=[pl.BlockSpec((tl, w), lambda p, j, s: (jnp.where(p == 0, s, nl - 1), 0)),
                  pl.BlockSpec((tl, w), lambda p, j, s: (jnp.where(p == 1, s, 0), 0)),
                  pl.BlockSpec((tl, tn), lambda p, j, s: (s, j))],
        out_specs=pl.BlockSpec((None, w, tn), lambda p, j, s: (p, 0, j)),
        scratch_shapes=[pltpu.VMEM((w, tn), F32)], name=name, compiler_params=_cp())(attn, conv, dh)
    return out.reshape(2 * w, d)


def _grad_w_in_t(dqkv, dcc, hn, name, rider=None):
    nq, l, w = dqkv.shape
    nc = dcc.shape[0]
    d = hn.shape[1]
    assert dcc.shape[2] == w
    tn = _tile(d, 1024, 128)
    tl = _tile(l, 1408, 128)
    nl = l // tl
    nj = d // tn
    rider = rider or _Rider()

    def body(q_ref, c_ref, b_ref, *rest):
        (o_ref,), (acc,) = rider.split(rest, 1, 1)
        p, jj, s = pl.program_id(0), pl.program_id(1), pl.program_id(2)
        rider.start((p == 0) & (jj == 0) & (s == 0))

        @pl.when(s == 0)
        def _():
            acc[...] = jnp.zeros_like(acc)

        @pl.when(p < nq)
        def _():
            acc[...] += lax.dot_general(q_ref[...], b_ref[...], TN, preferred_element_type=F32)

        @pl.when(p >= nq)
        def _():
            acc[...] += lax.dot_general(c_ref[...], b_ref[...], TN, preferred_element_type=F32)

        @pl.when(s == nl - 1)
        def _():
            o_ref[...] = acc[...].astype(BF16)

        rider.wait((p == nq + nc - 1) & (jj == nj - 1) & (s == nl - 1))

    out, *arrived = pl.pallas_call(
        body, out_shape=[SDS((nq + nc, w, d), BF16)] + rider.out_shape, grid=(nq + nc, nj, nl),
        in_specs=[pl.BlockSpec((None, tl, w), lambda p, j, s: (jnp.minimum(p, nq - 1), s, 0)),
                  pl.BlockSpec((None, tl, w), lambda p, j, s: (jnp.maximum(p - nq, 0), s, 0)),
                  pl.BlockSpec((tl, tn), lambda p, j, s: (s, j))] + rider.in_specs(),
        out_specs=[pl.BlockSpec((None, w, tn), lambda p, j, s: (p, 0, j))] + rider.out_specs(),
        scratch_shapes=[pltpu.VMEM((w, tn), F32)] + rider.scratch(), name=name,
        compiler_params=_cp())(dqkv, dcc, hn, *rider.srcs)
    return [out.reshape((nq + nc) * w, d)] + arrived


def _rms_bwd_tail(acc, h_ref, g_ref, dres_ref, o_ref, dg_ref):
    x = h_ref[...]
    r = lax.rsqrt(jnp.mean(x * x, axis=-1, keepdims=True) + EPS)
    xr = x * r
    dyv = acc[...]
    gy = dyv * g_ref[...]
    o_ref[...] = dres_ref[...] + r * (gy - xr * jnp.mean(gy * xr, axis=-1, keepdims=True))
    dg_ref[...] += _sum8(dyv * xr)


def _mix_bwd_in(dqkv, dcc, wt, h, g, dres, name, rider=None):
    nq, lp, w = dqkv.shape
    nc = dcc.shape[0]
    d = h.shape[1]
    tm = _tile(lp, FF_ROWS, 16)
    ni = lp // tm
    rider = rider or _Rider()

    def body(q_ref, c_ref, w_ref, h_ref, g_ref, dres_ref, *rest):
        (o_ref, dg_ref), (acc,) = rider.split(rest, 2, 1)
        i = pl.program_id(0)
        rider.start(i == 0)

        @pl.when(i == 0)
        def _():
            dg_ref[...] = jnp.zeros_like(dg_ref)

        pieces = [q_ref[p] for p in range(nq)] + [c_ref[p] for p in range(nc)]
        total = None
        for p, piece in enumerate(pieces):
            part = jnp.dot(piece, w_ref[p * w:(p + 1) * w, :], preferred_element_type=F32)
            total = part if total is None else total + part
        acc[...] = total
        _rms_bwd_tail(acc, h_ref, g_ref, dres_ref, o_ref, dg_ref)
        rider.wait(i == ni - 1)

    return pl.pallas_call(
        body, out_shape=[SDS((lp, d), F32), SDS((8, d), F32)] + rider.out_shape, grid=(ni,),
        in_specs=[pl.BlockSpec((nq, tm, w), lambda i: (0, i, 0)),
                  pl.BlockSpec((nc, tm, w), lambda i: (0, i, 0)),
                  pl.BlockSpec(((nq + nc) * w, d), lambda i: (0, 0)),
                  pl.BlockSpec((tm, d), lambda i: (i, 0)),
                  pl.BlockSpec((1, d), lambda i: (0, 0)),
                  pl.BlockSpec((tm, d), lambda i: (i, 0))] + rider.in_specs(),
        out_specs=[pl.BlockSpec((tm, d), lambda i: (i, 0)), pl.BlockSpec((8, d), lambda i: (0, 0))]
        + rider.out_specs(),
        scratch_shapes=[pltpu.VMEM((tm, d), F32)] + rider.scratch(), name=name,
        compiler_params=_cp())(dqkv, dcc, wt, h, g, dres, *rider.srcs)


def _ffn_bwd_in(d_g, d_u, wg_t, wu_t, h, g, dres, name, rider=None):
    lp, f = d_g.shape
    d = h.shape[1]
    tm = _tile(lp, FF_ROWS, 16)
    tk = _tile(f, FF_BLOCK, 128)
    nk = f // tk
    ni = lp // tm
    rider = rider or _Rider()

    def body(dg_in, du_in, wg_ref, wu_ref, h_ref, g_ref, dres_ref, *rest):
        (o_ref, dg_ref), (acc,) = rider.split(rest, 2, 1)
        i, kk = pl.program_id(0), pl.program_id(1)
        rider.start((i == 0) & (kk == 0))

        @pl.when(kk == 0)
        def _():
            acc[...] = jnp.zeros_like(acc)

        @pl.when((kk == 0) & (i == 0))
        def _():
            dg_ref[...] = jnp.zeros_like(dg_ref)

        acc[...] += (jnp.dot(dg_in[...], wg_ref[...], preferred_element_type=F32)
                     + jnp.dot(du_in[...], wu_ref[...], preferred_element_type=F32))

        @pl.when(kk == nk - 1)
        def _():
            _rms_bwd_tail(acc, h_ref, g_ref, dres_ref, o_ref, dg_ref)

        rider.wait((i == ni - 1) & (kk == nk - 1))

    return pl.pallas_call(
        body, out_shape=[SDS((lp, d), F32), SDS((8, d), F32)] + rider.out_shape, grid=(ni, nk),
        in_specs=[pl.BlockSpec((tm, tk), lambda i, j: (i, j)), pl.BlockSpec((tm, tk), lambda i, j: (i, j)),
                  pl.BlockSpec((tk, d), lambda i, j: (j, 0)), pl.BlockSpec((tk, d), lambda i, j: (j, 0)),
                  pl.BlockSpec((tm, d), lambda i, j: (i, 0)),
                  pl.BlockSpec((1, d), lambda i, j: (0, 0)),
                  pl.BlockSpec((tm, d), lambda i, j: (i, 0))] + rider.in_specs(),
        out_specs=[pl.BlockSpec((tm, d), lambda i, j: (i, 0)), pl.BlockSpec((8, d), lambda i, j: (0, 0))]
        + rider.out_specs(),
        scratch_shapes=[pltpu.VMEM((tm, d), F32)] + rider.scratch(), name=name,
        compiler_params=_cp())(d_g, d_u, wg_t, wu_t, h, g, dres, *rider.srcs)


def _ffn_fwd(h, g, wg_t, wu_t, w_d, name, rider=None):
    lp, d = h.shape
    f = w_d.shape[0]
    tm = _tile(lp, FF_ROWS, 16)
    fb, subs = _ff_block(f)
    nc = f // fb
    ni = lp // tm
    rider = rider or _Rider()

    def body(h_ref, g_ref, wg_ref, wu_ref, wd_ref, *rest):
        (o_ref, hn_ref, act_ref, gate_ref, up_ref), (hn_s, acc) = rider.split(rest, 5, 2)
        c = pl.program_id(1)
        rider.start((pl.program_id(0) == 0) & (c == 0))
        if ni > 1:
            rider.forward((pl.program_id(0) == ni - 1) & (c == 0))

        @pl.when(c == 0)
        def _():
            x = h_ref[...]
            r = lax.rsqrt(jnp.mean(x * x, axis=-1, keepdims=True) + EPS)
            hn = ((x * r) * g_ref[...]).astype(BF16)
            hn_s[...] = hn
            hn_ref[...] = hn
            acc[...] = jnp.zeros_like(acc)

        total = None
        for s0, sw in subs:
            gg = lax.dot_general(hn_s[...], wg_ref[s0:s0 + sw, :], NT, preferred_element_type=F32)
            uu = lax.dot_general(hn_s[...], wu_ref[s0:s0 + sw, :], NT, preferred_element_type=F32)
            gate_ref[:, s0:s0 + sw] = gg
            up_ref[:, s0:s0 + sw] = uu
            act = ((gg * _sigmoid(gg)) * uu).astype(BF16)
            act_ref[:, s0:s0 + sw] = act
            part = jnp.dot(act, wd_ref[s0:s0 + sw, :], preferred_element_type=F32)
            total = part if total is None else total + part
        acc[...] += total

        @pl.when(c == nc - 1)
        def _():
            o_ref[...] = h_ref[...] + acc[...]

        rider.wait((pl.program_id(0) == ni - 1) & (c == nc - 1))

    chunk = pl.BlockSpec((fb, d), lambda i, j: (j, 0))
    wide = pl.BlockSpec((tm, fb), lambda i, j: (i, j))
    return pl.pallas_call(
        body, out_shape=[SDS((lp, d), F32), SDS((lp, d), BF16), SDS((lp, f), BF16), SDS((lp, f), F32),
                         SDS((lp, f), F32)] + rider.out_shape, grid=(ni, nc),
        in_specs=[pl.BlockSpec((tm, d), lambda i, j: (i, 0)), pl.BlockSpec((1, d), lambda i, j: (0, 0)),
                  chunk, chunk, chunk] + rider.in_specs(),
        out_specs=[pl.BlockSpec((tm, d), lambda i, j: (i, 0)), pl.BlockSpec((tm, d), lambda i, j: (i, 0)),
                   wide, wide, wide] + rider.out_specs(),
        scratch_shapes=[pltpu.VMEM((tm, d), BF16), pltpu.VMEM((tm, d), F32)] + rider.scratch(),
        name=name, compiler_params=_cp(FFN_FWD_VMEM_MIB))(h, g, wg_t, wu_t, w_d, *rider.srcs)


def _ffn_bwd_act(dh, gate, up, w_d, name, rider=None):
    lp, d = dh.shape
    f = w_d.shape[0]
    tm = _tile(lp, FF_ROWS, 16)
    fb, subs = _ff_block(f)
    nc = f // fb
    ni = lp // tm
    rider = rider or _Rider()

    def body(dh_ref, gate_ref, up_ref, wd_ref, *rest):
        (dg_out, du_out), (dh_s,) = rider.split(rest, 2, 1)
        rider.start((pl.program_id(0) == 0) & (pl.program_id(1) == 0))

        @pl.when(pl.program_id(1) == 0)
        def _():
            dh_s[...] = dh_ref[...].astype(BF16)

        for s0, sw in subs:
            gg = gate_ref[:, s0:s0 + sw]
            uu = up_ref[:, s0:s0 + sw]
            dact = lax.dot_general(dh_s[...], wd_ref[s0:s0 + sw, :], NT, preferred_element_type=F32)
            s = _sigmoid(gg)
            dg_out[:, s0:s0 + sw] = (dact * uu * (s * (1.0 + gg * (1.0 - s)))).astype(BF16)
            du_out[:, s0:s0 + sw] = (dact * (gg * s)).astype(BF16)
        rider.wait((pl.program_id(0) == ni - 1) & (pl.program_id(1) == nc - 1))

    wide = pl.BlockSpec((tm, fb), lambda i, j: (i, j))
    return pl.pallas_call(
        body, out_shape=[SDS((lp, f), BF16), SDS((lp, f), BF16)] + rider.out_shape, grid=(ni, nc),
        in_specs=[pl.BlockSpec((tm, d), lambda i, j: (i, 0)), wide, wide,
                  pl.BlockSpec((fb, d), lambda i, j: (j, 0))] + rider.in_specs(),
        out_specs=[wide, wide] + rider.out_specs(), scratch_shapes=[pltpu.VMEM((tm, d), BF16)] + rider.scratch(),
        name=name, compiler_params=_cp())(dh, gate, up, w_d, *rider.srcs)


def _loss_head(h, tpad, g, n_real, name):
    lp, d = h.shape
    tm = _tile(lp, EW_ROWS, 16)

    def body(h_ref, t_ref, g_ref, dh_ref, dg_ref, loss_ref):
        i = pl.program_id(0)

        @pl.when(i == 0)
        def _():
            dg_ref[...] = jnp.zeros_like(dg_ref)
            loss_ref[...] = jnp.zeros_like(loss_ref)

        x = h_ref[...]
        r = lax.rsqrt(jnp.mean(x * x, axis=-1, keepdims=True) + EPS)
        xr = x * r
        y = xr * g_ref[...]
        row = i * tm + lax.broadcasted_iota(jnp.int32, (tm, d), 0)
        valid = (row >= N_META) & (row < N_META + n_real)
        diff = jnp.where(valid, y - t_ref[...], 0.0)
        loss_ref[...] += jnp.sum(diff * diff) * (0.5 / d)
        dy = diff * (1.0 / d)
        gy = dy * g_ref[...]
        dh_ref[...] = r * (gy - xr * jnp.mean(gy * xr, axis=-1, keepdims=True))
        dg_ref[...] += _sum8(dy * xr)

    return pl.pallas_call(
        body, out_shape=[SDS((lp, d), F32), SDS((8, d), F32), SDS((8, 128), F32)], grid=(lp // tm,),
        in_specs=[pl.BlockSpec((tm, d), lambda i: (i, 0)),
                  pl.BlockSpec((tm, d), lambda i: (i, 0)),
                  pl.BlockSpec((1, d), lambda i: (0, 0))],
        out_specs=[pl.BlockSpec((tm, d), lambda i: (i, 0)),
                   pl.BlockSpec((8, d), lambda i: (0, 0)),
                   pl.BlockSpec((8, 128), lambda i: (0, 0))],
        name=name, compiler_params=_cp())(h, tpad, g)


def _tri_consts():
    j = lax.broadcasted_iota(jnp.int32, (ABLK, ABLK), 0)
    s = lax.broadcasted_iota(jnp.int32, (ABLK, ABLK), 1)
    after = (j >= s).astype(BF16)
    before = (j < s).astype(BF16)
    ones = jnp.ones((ABLK, ABLK), BF16)
    two = lambda t: jnp.concatenate([t, t], axis=0)
    return (two(jnp.concatenate([after, ones], axis=1)),
            two(after),
            two(jnp.concatenate([before, ones], axis=1)))


def _softplus(z):
    neg_abs = lax.bitcast_convert_type(lax.bitcast_convert_type(z, jnp.uint32) | jnp.uint32(0x80000000), F32)
    return jnp.log(1.0 + jnp.exp(neg_abs)) + jnp.maximum(z, 0.0)


def _split_hi_lo(m):
    hi = m.astype(BF16)
    lo = (m - hi.astype(F32)).astype(BF16)
    return jnp.concatenate([hi, lo], axis=1)


def _head_halves(t2, in_a):
    zero = jnp.zeros_like(t2)
    return jnp.concatenate([jnp.where(in_a, t2, zero), jnp.where(in_a, zero, t2)], axis=0)


def _stack_blocks(t, nblk, in_a):
    return jnp.concatenate([_head_halves(t[u * ABLK:(u + 1) * ABLK], in_a) for u in range(nblk)], axis=0)


def _interleave(*gens):
    alive = list(gens)
    while alive:
        for g in list(alive):
            if next(g, alive) is alive:
                alive.remove(g)


def _attn_scale():
    scale = 1.0 / math.sqrt(HEAD_DIM)
    assert math.frexp(scale)[0] == 0.5, "a power of two, so that scaling q in bf16 is exact"
    return scale


def _pow2_below(n):
    assert n & (n - 1) == 0
    return [p for p in (64, 32, 16, 8, 4, 2, 1) if p < n]


class _Rider:
    def __init__(self, jobs=(), srcs=(), out_shape=()):
        self.jobs, self.srcs, self.out_shape = list(jobs), list(srcs), list(out_shape)
        self.any = [pl.BlockSpec(memory_space=pl.ANY)]

    def split(self, rest, n_out, n_scratch):
        ni, no = len(self.srcs), len(self.out_shape)
        self.ins, outs = rest[:ni], rest[ni:ni + n_out]
        self.outs = rest[ni + n_out:ni + n_out + no]
        scratch = rest[ni + n_out + no:ni + n_out + no + n_scratch]
        self.sems = rest[ni + n_out + no + n_scratch:]
        return outs, scratch

    def start(self, first):
        self.forwarded = False
        if self.jobs:
            @pl.when(first)
            def _():
                _exchange_start(self.jobs, self.ins, self.outs, *self.sems)

    def forward(self, late):
        self.forwarded = True
        if self.jobs:
            @pl.when(late)
            def _():
                _exchange_forward(self.jobs, self.ins, self.outs, *self.sems)

    def wait(self, last):
        if self.jobs:
            @pl.when(last)
            def _():
                if not self.forwarded:
                    _exchange_forward(self.jobs, self.ins, self.outs, *self.sems)
                _exchange_wait(self.jobs, self.ins, self.outs, *self.sems)

    def in_specs(self):
        return self.any * len(self.srcs)

    def out_specs(self):
        return self.any * len(self.out_shape)

    def scratch(self):
        return _exchange_sems(len(self.jobs)) if self.jobs else []


def _attn_fwd(qkv, tri_fwd, name, rider=None):
    lp = qkv.shape[0]
    n_pairs = (N_HEADS * HEAD_DIM) // 128
    nb = lp // ABLK
    assert nb <= 128 and 2 * HEAD_DIM == 128
    scale = _attn_scale()
    nt = (((1,), (1,)), ((), ()))
    rider = rider or _Rider()

    def body(q_ref, k_ref, v_ref, tri_ref, *rest):
        (o_ref, rs_ref), (r_s, acc_s, rs_s) = rider.split(rest, 2, 3)
        rider.start(pl.program_id(0) == 0)
        lane = lax.broadcasted_iota(jnp.int32, (ABLK, 128), 1)
        row = lax.broadcasted_iota(jnp.int32, (ABLK, 128), 0)
        in_a = lane < HEAD_DIM
        causal = lane < row

        def begin(st, i):
            r_s[st] = jnp.zeros(r_s.shape[1:], F32)
            acc_s[st] = jnp.zeros(acc_s.shape[1:], F32)
            rs_s[st] = jnp.full(rs_s.shape[1:], GONE, F32)
            return q_ref[pl.ds(pl.multiple_of(i * ABLK, ABLK), ABLK), :] * scale

        def live(st):
            least = jnp.min(jnp.minimum(r_s[st, 0], r_s[st, 1]), axis=0, keepdims=True)
            return (least[0, 0] < EXP_ZERO_AT).astype(jnp.int32)

        def step(*args):
            _interleave(stages(*args))

        def stages(st, q2, kb0, nblk, diag):
            k0 = pl.multiple_of(kb0 * ABLK, ABLK)
            kbd = _stack_blocks(k_ref[pl.ds(k0, nblk * ABLK), :], nblk, in_a)
            vbd = _stack_blocks(v_ref[pl.ds(k0, nblk * ABLK), :], nblk, in_a)
            z = lax.dot_general(q2, kbd, nt, preferred_element_type=F32)
            ncol = 2 * nblk
            zt = [z[:, c * 128:(c + 1) * 128] for c in range(ncol)]
            if diag:
                zt = [jnp.where(causal, t, MASKED) if c >= ncol - 2 else t for c, t in enumerate(zt)]
            yield
            bounds = [ncol * j // ATT_SPLIT for j in range(ATT_SPLIT + 1)]
            ce = [None] * ncol
            for c0, c1 in reversed([b for b in zip(bounds[:-1], bounds[1:]) if b[0] < b[1]]):
                parts = [_split_hi_lo(_softplus(zt[c])) for c in range(c0, c1)]
                got = jnp.dot(jnp.concatenate(parts, axis=0), tri_ref[...], preferred_element_type=F32)
                for c in range(c0, c1):
                    ce[c] = got[(c - c0) * 128:(c - c0 + 1) * 128]
                yield
            rr = [r_s[st, 0], r_s[st, 1]]
            rsv = [rs_s[st, :, :128], rs_s[st, :, 128:]]
            ws = [None] * ncol
            for u in reversed(range(nblk)):
                for hh in range(2):
                    c = 2 * u + hh
                    ws[c] = jnp.exp(zt[c] - ce[c][:, :128] - rr[hh]).astype(BF16)
                    rsv[hh] = jnp.where(lane == kb0 + u, rr[hh], rsv[hh])
                    rr[hh] = rr[hh] + ce[c][:, 128:]
            acc_s[st] += jnp.dot(jnp.concatenate(ws, axis=1), vbd, preferred_element_type=F32)
            r_s[st, 0] = rr[0]
            r_s[st, 1] = rr[1]
            rs_s[st, :, :128] = rsv[0]
            rs_s[st, :, 128:] = rsv[1]

        def finish(st, q2, i, n_top):
            i_low = i - n_top
            n_grp = i_low // ATT_GROUP

            def more(c):
                return (c[0] < n_grp) & (c[1] > 0)

            def inner(c):
                step(st, q2, i_low - ATT_GROUP * (c[0] + 1), ATT_GROUP, False)
                return c[0] + 1, live(st)

            _, alive = lax.while_loop(more, inner, (jnp.int32(0), live(st)))
            rem = i_low - ATT_GROUP * n_grp
            for p in _pow2_below(ATT_GROUP):
                def last_steps(p=p):
                    step(st, q2, rem & (p - 1), p, False)
                    return live(st)

                alive = lax.cond(((rem & p) != 0) & (alive > 0), last_steps, lambda alive=alive: alive)

            q0 = pl.multiple_of(i * ABLK, ABLK)
            o_ref[pl.ds(q0, ABLK), :] = acc_s[st].astype(BF16)
            rs_ref[pl.ds(q0, ABLK), :] = rs_s[st]

        def single(i):
            q2 = begin(0, i)
            n_top = jnp.minimum(i, ATT_TOP)
            for t in range(ATT_TOP + 1):
                @pl.when(n_top == t)
                def _():
                    step(0, q2, i - t, t + 1, True)

            finish(0, q2, i, n_top)

        def together(i):
            qs = [begin(st, i + st) for st in range(ATT_FLIGHT)]
            _interleave(*[stages(st, qs[st], i + st - ATT_TOP, ATT_TOP + 1, True) for st in range(ATT_FLIGHT)])
            for st in range(ATT_FLIGHT):
                finish(st, qs[st], i + st, ATT_TOP)

        n_head = min(ATT_TOP, nb)
        n_group = (nb - n_head) // ATT_FLIGHT
        n_single = nb - ATT_FLIGHT * n_group

        def singles(k, carry):
            single(jnp.where(k < n_head, k, nb - n_single + k))
            return carry

        def groups(j, carry):
            together(n_head + ATT_FLIGHT * j)
            return carry

        lax.fori_loop(0, n_single, singles, 0)
        lax.fori_loop(0, n_group, groups, 0)
        rider.wait(pl.program_id(0) == n_pairs - 1)

    col = lambda o: (lambda p: (0, p + o))
    return pl.pallas_call(
        body, out_shape=[SDS((lp, n_pairs * 128), BF16), SDS((lp, n_pairs * 256), F32)] + rider.out_shape,
        grid=(n_pairs,),
        in_specs=[pl.BlockSpec((lp, 128), col(0)), pl.BlockSpec((lp, 128), col(n_pairs)),
                  pl.BlockSpec((lp, 128), col(2 * n_pairs)), pl.BlockSpec((256, 256), lambda p: (0, 0))]
        + rider.in_specs(),
        out_specs=[pl.BlockSpec((lp, 128), col(0)), pl.BlockSpec((lp, 256), col(0))] + rider.out_specs(),
        scratch_shapes=[pltpu.VMEM((ATT_FLIGHT, 2, ABLK, 128), F32), pltpu.VMEM((ATT_FLIGHT, ABLK, 128), F32),
                        pltpu.VMEM((ATT_FLIGHT, ABLK, 256), F32)] + rider.scratch(),
        name=name, compiler_params=_cp())(qkv, qkv, qkv, tri_fwd, *rider.srcs)


def _attn_bwd(qkv, d_out, rsave, tri_after, tri_before, name, rider=None):
    lp = qkv.shape[0]
    n_pairs = (N_HEADS * HEAD_DIM) // 128
    nb = lp // ABLK
    scale = _attn_scale()
    nt = (((1,), (1,)), ((), ()))
    tn = (((0,), (0,)), ((), ()))
    rider = rider or _Rider()

    def body(q_ref, k_ref, v_ref, do_ref, rs_ref, ta_ref, tb_ref, *rest):
        (o_ref,), (dk_s, dv_s, dq_s, pc_s) = rider.split(rest, 1, 4)
        rider.start(pl.program_id(0) == 0)
        lane = lax.broadcasted_iota(jnp.int32, (ABLK, 128), 1)
        row = lax.broadcasted_iota(jnp.int32, (ABLK, 128), 0)
        in_a = lane < HEAD_DIM
        causal = lane < row
        dk_s[...] = jnp.zeros_like(dk_s)
        dv_s[...] = jnp.zeros_like(dv_s)

        def begin(st, i):
            q0 = pl.multiple_of(i * ABLK, ABLK)
            q2 = q_ref[pl.ds(q0, ABLK), :] * scale
            do2 = do_ref[pl.ds(q0, ABLK), :]
            dq_s[st] = jnp.zeros(dq_s.shape[1:], F32)
            pc_s[st] = jnp.zeros(pc_s.shape[1:], F32)
            return dict(q0=q0, q2=q2, do2=do2, q_st=_head_halves(q2, in_a), do_st=_head_halves(do2, in_a))

        def step(*args):
            _interleave(stages(*args))

        def stages(st, blk, kb0, nblk, diag):
            q0, q2, do2 = blk["q0"], blk["q2"], blk["do2"]
            k0 = pl.multiple_of(kb0 * ABLK, ABLK)
            kbd = _stack_blocks(k_ref[pl.ds(k0, nblk * ABLK), :], nblk, in_a)
            vbd = _stack_blocks(v_ref[pl.ds(k0, nblk * ABLK), :], nblk, in_a)
            z = lax.dot_general(q2, kbd, nt, preferred_element_type=F32)
            dw = lax.dot_general(do2, vbd, nt, preferred_element_type=F32)
            ncol = 2 * nblk
            zt = [z[:, c * 128:(c + 1) * 128] for c in range(ncol)]
            if diag:
                zt = [jnp.where(causal, t, MASKED) if c >= ncol - 2 else t for c, t in enumerate(zt)]
            bounds = [ncol * j // ATT_SPLIT for j in range(ATT_SPLIT + 1)]
            batches = [range(c0, c1) for c0, c1 in zip(bounds[:-1], bounds[1:]) if c0 < c1]
            sps, ex, ws, dls, pe = [None] * ncol, [None] * ncol, [None] * ncol, [None] * ncol, [None] * ncol

            def mass(cols):
                for c in cols:
                    sps[c] = _softplus(zt[c])
                got = jnp.dot(jnp.concatenate([_split_hi_lo(sps[c]) for c in cols], axis=0), ta_ref[...],
                              preferred_element_type=F32)
                for j, c in enumerate(cols):
                    ex[c] = got[j * 128:(j + 1) * 128]

            def weights(cols):
                for c in cols:
                    u, hh = c // 2, c % 2
                    r_saved = jnp.sum(jnp.where(lane == kb0 + u, rs_ref[pl.ds(q0, ABLK), hh * 128:(hh + 1) * 128],
                                                0.0), axis=1, keepdims=True)
                    w = jnp.exp(zt[c] - ex[c] - r_saved)
                    ws[c] = w.astype(BF16)
                    dls[c] = dw[:, c * 128:(c + 1) * 128] * w
                got = jnp.dot(jnp.concatenate([_split_hi_lo(dls[c]) for c in cols], axis=0), tb_ref[...],
                              preferred_element_type=F32)
                for j, c in enumerate(cols):
                    pe[c] = got[j * 128:(j + 1) * 128]

            yield
            mass(batches[0])
            yield
            for j in range(len(batches)):
                if j + 1 < len(batches):
                    mass(batches[j + 1])
                    yield
                weights(batches[j])
                yield
            pc = [pc_s[st, 0], pc_s[st, 1]]
            dzs = []
            for c in range(ncol):
                hh = c % 2
                one_minus_beta = jnp.exp(-sps[c])
                dz = dls[c] * one_minus_beta - (pe[c][:, :128] + pc[hh]) * (1.0 - one_minus_beta)
                pc[hh] = pc[hh] + pe[c][:, 128:]
                dzs.append(dz.astype(BF16))
            pc_s[st, 0] = pc[0]
            pc_s[st, 1] = pc[1]
            dq_s[st] += jnp.dot(jnp.concatenate(dzs, axis=1), kbd, preferred_element_type=F32)
            by_head = lambda ts: jnp.concatenate([jnp.concatenate(ts[0::2], axis=1), jnp.concatenate(ts[1::2], axis=1)],
                                                 axis=0)
            rows = pl.ds(k0, nblk * ABLK)
            dk_s[rows, :] += lax.dot_general(by_head(dzs), blk["q_st"], tn, preferred_element_type=F32)
            dv_s[rows, :] += lax.dot_general(by_head(ws), blk["do_st"], tn, preferred_element_type=F32)

        def below(st, blk, i, n_top):
            gone = jnp.min(rs_ref[pl.ds(blk["q0"], ABLK), :], axis=0, keepdims=True) >= EXP_ZERO_AT
            lane1 = lax.broadcasted_iota(jnp.int32, (1, 128), 1)
            first = jnp.sum(jnp.where(gone[:, :128] & gone[:, 128:] & (lane1 < i), 1.0, 0.0)).astype(jnp.int32)
            i_low = i - n_top
            n_grp = i_low // ATT_GROUP
            rem = i_low - ATT_GROUP * n_grp
            for p in reversed(_pow2_below(ATT_GROUP)):
                @pl.when(((rem & p) != 0) & ((rem & (p - 1)) + p > first))
                def _():
                    step(st, blk, rem & (p - 1), p, False)

            def inner(g, c2):
                step(st, blk, rem + ATT_GROUP * g, ATT_GROUP, False)
                return c2

            lax.fori_loop(jnp.maximum(first - rem, 0) // ATT_GROUP, n_grp, inner, 0)

        def done(st, blk):
            o_ref[0, pl.ds(blk["q0"], ABLK), :] = (dq_s[st] * scale).astype(BF16)

        def single(i):
            blk = begin(0, i)
            n_top = jnp.minimum(i, ATT_TOP)
            below(0, blk, i, n_top)
            for t in range(ATT_TOP + 1):
                @pl.when(n_top == t)
                def _():
                    step(0, blk, i - t, t + 1, True)

            done(0, blk)

        def together(i):
            blks = [begin(st, i + st) for st in range(ATT_FLIGHT)]
            for st in range(ATT_FLIGHT):
                below(st, blks[st], i + st, ATT_TOP)
            _interleave(*[stages(st, blks[st], i + st - ATT_TOP, ATT_TOP + 1, True) for st in range(ATT_FLIGHT)])
            for st in range(ATT_FLIGHT):
                done(st, blks[st])

        n_head = min(ATT_TOP, nb)
        n_group = (nb - n_head) // ATT_FLIGHT
        n_single = nb - ATT_FLIGHT * n_group

        def singles(k, carry):
            single(jnp.where(k < n_head, k, nb - n_single + k))
            return carry

        def groups(j, carry):
            together(n_head + ATT_FLIGHT * j)
            return carry

        lax.fori_loop(0, n_single, singles, 0)
        lax.fori_loop(0, n_group, groups, 0)
        o_ref[1] = dk_s[...].astype(BF16)
        o_ref[2] = dv_s[...].astype(BF16)
        rider.wait(pl.program_id(0) == n_pairs - 1)

    col = lambda o: (lambda p: (0, p + o))
    return pl.pallas_call(
        body, out_shape=[SDS((3, lp, n_pairs * 128), BF16)] + rider.out_shape, grid=(n_pairs,),
        in_specs=[pl.BlockSpec((lp, 128), col(0)), pl.BlockSpec((lp, 128), col(n_pairs)),
                  pl.BlockSpec((lp, 128), col(2 * n_pairs)), pl.BlockSpec((lp, 128), col(0)),
                  pl.BlockSpec((lp, 256), col(0)),
                  pl.BlockSpec((256, 128), lambda p: (0, 0)), pl.BlockSpec((256, 256), lambda p: (0, 0))]
        + rider.in_specs(),
        out_specs=[pl.BlockSpec((3, lp, 128), lambda p: (0, 0, p))] + rider.out_specs(),
        scratch_shapes=[pltpu.VMEM((lp, 128), F32), pltpu.VMEM((lp, 128), F32),
                        pltpu.VMEM((ATT_FLIGHT, ABLK, 128), F32), pltpu.VMEM((ATT_FLIGHT, 2, ABLK, 128), F32)]
        + rider.scratch(),
        name=name, compiler_params=_cp())(qkv, qkv, qkv, d_out, rsave, tri_after, tri_before, *rider.srcs)


def _conv_fwd_dw(cacg, w, b, name, rider=None):
    lp = cacg.shape[0]
    c = cacg.shape[1] // 2
    ncb = c // 128
    nchunk = lp // ABLK
    off = CONV_PAD - (CONV_K - 1)
    rider = rider or _Rider()

    def body(a_ref, g_ref, w_ref, b_ref, *rest):
        (y_ref,), (upad,) = rider.split(rest, 1, 1)
        rider.start(pl.program_id(0) == 0)
        upad[0:CONV_PAD, :] = jnp.zeros((CONV_PAD, 128), F32)

        def fill(ch, carry):
            base = pl.multiple_of(ch * ABLK, ABLK)
            upad[pl.ds(base + CONV_PAD, ABLK), :] = a_ref[pl.ds(base, ABLK), :] * _sigmoid(g_ref[pl.ds(base, ABLK), :])
            return carry

        lax.fori_loop(0, nchunk, fill, 0)

        def comp(ch, carry):
            base = pl.multiple_of(ch * ABLK, ABLK)
            acc = jnp.zeros((ABLK, 128), F32)
            for k in range(CONV_K):
                acc = acc + upad[pl.ds(base + (off + k), ABLK), :] * w_ref[k:k + 1, :]
            y_ref[pl.ds(base, ABLK), :] = acc + b_ref[...]
            return carry

        lax.fori_loop(0, nchunk, comp, 0)
        rider.wait(pl.program_id(0) == ncb - 1)

    return pl.pallas_call(
        body, out_shape=[SDS((lp, c), F32)] + rider.out_shape, grid=(ncb,),
        in_specs=[pl.BlockSpec((lp, 128), lambda j: (0, j)), pl.BlockSpec((lp, 128), lambda j: (0, j + ncb)),
                  pl.BlockSpec((CONV_PAD, 128), lambda j: (0, j)), pl.BlockSpec((1, 128), lambda j: (0, j))]
        + rider.in_specs(),
        out_specs=[pl.BlockSpec((lp, 128), lambda j: (0, j))] + rider.out_specs(),
        scratch_shapes=[pltpu.VMEM((lp + CONV_PAD, 128), F32)] + rider.scratch(), name=name,
        compiler_params=_cp())(cacg, cacg, w, b, *rider.srcs)


def _ln_parts(x, g, b):
    mu = jnp.mean(x, axis=-1, keepdims=True)
    xc = x - mu
    rstd = lax.rsqrt(jnp.mean(xc * xc, axis=-1, keepdims=True) + EPS)
    xh = xc * rstd
    return xh, rstd, xh * g + b


def _conv_fwd_ln(yc, g, b, name):
    lp, c = yc.shape
    tm = _tile(lp, EW_ROWS, 16)

    def body(y_ref, g_ref, b_ref, o_ref):
        _, _, ln = _ln_parts(y_ref[...], g_ref[...], b_ref[...])
        o_ref[...] = (ln * _sigmoid(ln)).astype(BF16)

    return pl.pallas_call(
        body, out_shape=SDS((lp, c), BF16), grid=(lp // tm,),
        in_specs=[pl.BlockSpec((tm, c), lambda i: (i, 0)), pl.BlockSpec((1, c), lambda i: (0, 0)),
                  pl.BlockSpec((1, c), lambda i: (0, 0))],
        out_specs=pl.BlockSpec((tm, c), lambda i: (i, 0)), name=name, compiler_params=_cp())(yc, g, b)


def _conv_bwd_ln(yc, dout, g, b, name):
    lp, c = yc.shape
    tm = _tile(lp, EW_ROWS, 16)

    def body(y_ref, d_ref, g_ref, b_ref, o_ref, dg_ref, db_ref):
        @pl.when(pl.program_id(0) == 0)
        def _():
            dg_ref[...] = jnp.zeros_like(dg_ref)
            db_ref[...] = jnp.zeros_like(db_ref)

        xh, rstd, ln = _ln_parts(y_ref[...], g_ref[...], b_ref[...])
        s = _sigmoid(ln)
        dln = d_ref[...] * (s * (1.0 + ln * (1.0 - s)))
        dg_ref[...] += _sum8(dln * xh)
        db_ref[...] += _sum8(dln)
        dxh = dln * g_ref[...]
        o_ref[...] = rstd * (dxh - jnp.mean(dxh, axis=-1, keepdims=True)
                             - xh * jnp.mean(dxh * xh, axis=-1, keepdims=True))

    return pl.pallas_call(
        body, out_shape=[SDS((lp, c), F32), SDS((8, c), F32), SDS((8, c), F32)], grid=(lp // tm,),
        in_specs=[pl.BlockSpec((tm, c), lambda i: (i, 0)), pl.BlockSpec((tm, c), lambda i: (i, 0)),
                  pl.BlockSpec((1, c), lambda i: (0, 0)), pl.BlockSpec((1, c), lambda i: (0, 0))],
        out_specs=[pl.BlockSpec((tm, c), lambda i: (i, 0)), pl.BlockSpec((8, c), lambda i: (0, 0)),
                   pl.BlockSpec((8, c), lambda i: (0, 0))],
        name=name, compiler_params=_cp())(yc, dout, g, b)


def _conv_bwd_dw(dyc, cacg, w, name, rider=None):
    lp, c = dyc.shape
    ncb = c // 128
    nchunk = lp // ABLK
    off = CONV_PAD - (CONV_K - 1)
    rider = rider or _Rider()

    def body(dy_ref, a_ref, g_ref, w_ref, *rest):
        (dcc_ref, dw_ref, db_ref), (upad, dypad, dwacc) = rider.split(rest, 3, 3)
        rider.start(pl.program_id(0) == 0)
        upad[0:CONV_PAD, :] = jnp.zeros((CONV_PAD, 128), F32)
        dypad[lp:lp + CONV_PAD, :] = jnp.zeros((CONV_PAD, 128), F32)
        dwacc[...] = jnp.zeros_like(dwacc)
        db_ref[...] = jnp.zeros_like(db_ref)

        def fill(ch, carry):
            base = pl.multiple_of(ch * ABLK, ABLK)
            upad[pl.ds(base + CONV_PAD, ABLK), :] = a_ref[pl.ds(base, ABLK), :] * _sigmoid(g_ref[pl.ds(base, ABLK), :])
            dypad[pl.ds(base, ABLK), :] = dy_ref[pl.ds(base, ABLK), :]
            return carry

        lax.fori_loop(0, nchunk, fill, 0)

        def comp(ch, carry):
            base = pl.multiple_of(ch * ABLK, ABLK)
            dy = dy_ref[pl.ds(base, ABLK), :]
            du = jnp.zeros((ABLK, 128), F32)
            for k in range(CONV_K):
                du = du + dypad[pl.ds(base + (CONV_K - 1 - k), ABLK), :] * w_ref[k:k + 1, :]
                dwacc[k * 8:(k + 1) * 8, :] += _sum8(dy * upad[pl.ds(base + (off + k), ABLK), :])
            db_ref[...] += _sum8(dy)
            a = a_ref[pl.ds(base, ABLK), :]
            s = _sigmoid(g_ref[pl.ds(base, ABLK), :])
            dcc_ref[0, pl.ds(base, ABLK), :] = (du * s).astype(BF16)
            dcc_ref[1, pl.ds(base, ABLK), :] = (du * a * (s * (1.0 - s))).astype(BF16)
            return carry

        lax.fori_loop(0, nchunk, comp, 0)
        dw_ref[...] = dwacc[...].reshape(CONV_PAD, 8, 128).sum(axis=1)
        rider.wait(pl.program_id(0) == ncb - 1)

    return pl.pallas_call(
        body, out_shape=[SDS((2, lp, c), BF16), SDS((CONV_PAD, c), F32), SDS((8, c), F32)] + rider.out_shape,
        grid=(ncb,),
        in_specs=[pl.BlockSpec((lp, 128), lambda j: (0, j)), pl.BlockSpec((lp, 128), lambda j: (0, j)),
                  pl.BlockSpec((lp, 128), lambda j: (0, j + ncb)), pl.BlockSpec((CONV_PAD, 128), lambda j: (0, j))]
        + rider.in_specs(),
        out_specs=[pl.BlockSpec((2, lp, 128), lambda j: (0, 0, j)),
                   pl.BlockSpec((CONV_PAD, 128), lambda j: (0, j)), pl.BlockSpec((8, 128), lambda j: (0, j))]
        + rider.out_specs(),
        scratch_shapes=[pltpu.VMEM((lp + CONV_PAD, 128), F32), pltpu.VMEM((lp + CONV_PAD, 128), F32),
                        pltpu.VMEM((CONV_PAD * 8, 128), F32)] + rider.scratch(),
        name=name, compiler_params=_cp())(dyc, cacg, cacg, w, *rider.srcs)


def _mesh_pos():
    x, y, c = lax.axis_index("x"), lax.axis_index("y"), lax.axis_index("c")
    return x, y, c


def _peer(pos, r):
    x, y, c = pos
    px = (1 - x) if (r >> 2) & 1 else x
    py = (1 - y) if (r >> 1) & 1 else y
    pc = (1 - c) if r & 1 else c
    return (px, py, pc), 4 * px + 2 * py + pc


SIBLING = 1
OTHER_CHIPS = (2, 4, 6)


class _Job:
    def __init__(self, src, dst, scatter, src_layer=None, dst_layer=None):
        self.src, self.dst, self.scatter, self.src_layer, self.dst_layer = src, dst, scatter, src_layer, dst_layer

    def src_view(self, ins, idx):
        v = ins[self.src] if self.src_layer is None else ins[self.src].at[self.src_layer]
        return v.at[idx] if self.scatter else v

    def dst_view(self, outs, slot):
        v = outs[self.dst] if self.dst_layer is None else outs[self.dst].at[self.dst_layer]
        return v.at[slot]


def _remote(job, j, r, src, dst, to, send, recv):
    return pltpu.make_async_remote_copy(src_ref=src, dst_ref=dst, send_sem=send.at[j, r - 1], recv_sem=recv.at[j, r - 1],
                                        device_id=to, device_id_type=pl.DeviceIdType.MESH)


def _exchange_start(jobs, ins, outs, send, recv, loc):
    pos = _mesh_pos()
    me = 4 * pos[0] + 2 * pos[1] + pos[2]
    for j, job in enumerate(jobs):
        pltpu.make_async_copy(job.src_view(ins, me), job.dst_view(outs, me), loc.at[j]).start()
        for r in (range(1, N_DEV) if job.scatter else (SIBLING,) + OTHER_CHIPS):
            peer, peer_idx = _peer(pos, r)
            _remote(job, j, r, job.src_view(ins, peer_idx), job.dst_view(outs, me), peer, send, recv).start()


def _exchange_forward(jobs, ins, outs, send, recv, loc):
    pos = _mesh_pos()
    sibling, _ = _peer(pos, SIBLING)
    for j, job in enumerate(jobs):
        if job.scatter:
            continue
        for r in OTHER_CHIPS:
            peer, peer_idx = _peer(pos, r)
            slot = job.dst_view(outs, peer_idx)
            _remote(job, j, r, job.src_view(ins, peer_idx), slot, peer, send, recv).wait_recv()
            _remote(job, j, r ^ SIBLING, slot, slot, sibling, send, recv).start()


def _exchange_wait(jobs, ins, outs, send, recv, loc):
    pos = _mesh_pos()
    me = 4 * pos[0] + 2 * pos[1] + pos[2]
    for j, job in enumerate(jobs):
        for r in range(1, N_DEV):
            peer, peer_idx = _peer(pos, r)
            cp = _remote(job, j, r, job.src_view(ins, peer_idx), job.dst_view(outs, peer_idx), peer, send, recv)
            if job.scatter or r not in OTHER_CHIPS:
                cp.wait_recv()
            cp.wait_send()
        pltpu.make_async_copy(job.src_view(ins, me), job.dst_view(outs, me), loc.at[j]).wait()


def _exchange_sems(n_jobs):
    return [pltpu.SemaphoreType.DMA((n_jobs, N_DEV - 1)), pltpu.SemaphoreType.DMA((n_jobs, N_DEV - 1)),
            pltpu.SemaphoreType.DMA((n_jobs,))]


def _exchange(jobs, arrs, out_shape, name):
    n_in, n_out = len(arrs), len(out_shape)
    any_spec = pl.BlockSpec(memory_space=pl.ANY)

    def body(*refs):
        ins, outs, sems = refs[:n_in], refs[n_in:n_in + n_out], refs[n_in + n_out:]
        _exchange_start(jobs, ins, outs, *sems)
        _exchange_forward(jobs, ins, outs, *sems)
        _exchange_wait(jobs, ins, outs, *sems)

    return pl.pallas_call(
        body, out_shape=out_shape, in_specs=[any_spec] * n_in, out_specs=[any_spec] * n_out,
        scratch_shapes=_exchange_sems(len(jobs)),
        name=name, compiler_params=pltpu.CompilerParams(has_side_effects=True))(*arrs)


def _all_reduce_small(placed, cuts, rows, w, name):
    n_in = len(placed)

    def body(*refs):
        in_refs, o_refs = refs[:n_in], refs[n_in:n_in + len(cuts)]
        buf, send, recv = refs[n_in + len(cuts):]
        pos = _mesh_pos()
        me = 4 * pos[0] + 2 * pos[1] + pos[2]
        buf[me] = jnp.zeros((rows, w), F32)
        for ref, (arr, row0, col0, is_partial) in zip(in_refs, placed):
            val = ref[...].sum(axis=0, keepdims=True) if is_partial else ref[...]
            buf[me, row0:row0 + val.shape[0], col0:col0 + val.shape[1]] = val
        for r in range(1, N_DEV):
            peer, _ = _peer(pos, r)
            pltpu.make_async_remote_copy(src_ref=buf.at[me], dst_ref=buf.at[me], send_sem=send.at[r - 1],
                                         recv_sem=recv.at[r - 1], device_id=peer,
                                         device_id_type=pl.DeviceIdType.MESH).start()
        for r in range(1, N_DEV):
            peer, peer_idx = _peer(pos, r)
            cp = pltpu.make_async_remote_copy(src_ref=buf.at[me], dst_ref=buf.at[peer_idx], send_sem=send.at[r - 1],
                                              recv_sem=recv.at[r - 1], device_id=peer,
                                              device_id_type=pl.DeviceIdType.MESH)
            cp.wait_recv()
            cp.wait_send()
        acc = buf[0]
        for dev in range(1, N_DEV):
            acc = acc + buf[dev]
        for o_ref, (_, pieces) in zip(o_refs, cuts):
            for index, row0, nrows, col0, ncols in pieces:
                o_ref[index] = acc[row0:row0 + nrows, col0:col0 + ncols].reshape(o_ref.at[index].shape)

    vmem = pl.BlockSpec(memory_space=pltpu.VMEM)
    return pl.pallas_call(
        body, out_shape=[SDS(shape, F32) for shape, _ in cuts], in_specs=[vmem] * n_in, out_specs=[vmem] * len(cuts),
        scratch_shapes=[pltpu.VMEM((N_DEV, rows, w), F32), pltpu.SemaphoreType.DMA((N_DEV - 1,)),
                        pltpu.SemaphoreType.DMA((N_DEV - 1,))],
        name=name, compiler_params=pltpu.CompilerParams(has_side_effects=True))(*[p[0] for p in placed])


def _adamw_math(w, g, m, v):
    m = ADAM_B1 * m + (1.0 - ADAM_B1) * g
    v = ADAM_B2 * v + (1.0 - ADAM_B2) * (g * g)
    m_hat = m / (1.0 - ADAM_B1 ** ADAM_STEP)
    v_hat = v / (1.0 - ADAM_B2 ** ADAM_STEP)
    delta = -ADAM_LR * (m_hat / (jnp.sqrt(v_hat) + ADAM_EPS) + ADAM_WD * w)
    return delta, m, v


def _adamw_shard(parts, w, m, v, name):
    depth = len(parts)
    _, rr, cc = parts[0].shape
    tr = _tile(rr, 256, 8)
    nt = rr // tr
    part_block, blk = (N_DEV, tr, cc), pl.BlockSpec((None, tr, cc), lambda l, i: (l, i, 0))

    def body(*refs):
        p_refs = refs[:depth]
        w_ref, m_ref, v_ref, g_out, d_out, m_out, v_out = refs[depth:]
        for li in range(depth):
            @pl.when(pl.program_id(0) == li)
            def _(p_ref=p_refs[li]):
                g = p_ref[0].astype(F32)
                for dev in range(1, N_DEV):
                    g = g + p_ref[dev].astype(F32)
                delta, mm, vv = _adamw_math(w_ref[...], g, m_ref[...], v_ref[...])
                g_out[...] = g
                d_out[...] = delta
                m_out[...] = mm
                v_out[...] = vv

    def part_spec(li):
        return pl.BlockSpec(part_block, lambda l, i: (0, jnp.where(l == li, i, jnp.where(l < li, 0, nt - 1)), 0))

    return pl.pallas_call(
        body, out_shape=[SDS(w.shape, F32)] * 4, grid=(depth, nt),
        in_specs=[part_spec(li) for li in range(depth)] + [blk, blk, blk],
        out_specs=[blk] * 4, name=name, compiler_params=_cp())(*parts, w, m, v)


def _adamw_small(gs, ws, ms, vs, name):
    n = len(gs)

    def body(*refs):
        g_refs, w_refs, m_refs, v_refs = (refs[k * n:(k + 1) * n] for k in range(4))
        outs = refs[4 * n:]
        for k in range(n):
            delta, mm, vv = _adamw_math(w_refs[k][...], g_refs[k][...], m_refs[k][...], v_refs[k][...])
            outs[k][...] = delta
            outs[n + k][...] = mm
            outs[2 * n + k][...] = vv

    res = pl.pallas_call(body, out_shape=[SDS(w.shape, F32) for w in ws] * 3, name=name,
                         compiler_params=_cp())(*gs, *ws, *ms, *vs)
    return res[:n], res[n:2 * n], res[2 * n:]


def _from_cols(t):
    return jnp.transpose(t, (1, 0, 2)).reshape(t.shape[1], N_DEV * t.shape[2])


def _swap(t):
    return jnp.swapaxes(t, -1, -2)


def kernel(x, meta_tokens, mix_norm_g, w_in, conv_dw_w, conv_dw_b, conv_ln_g, conv_ln_b, w_out, ffn_norm_g, w_gate, w_up, w_down, final_norm_g, loss_target, m_meta_tokens, m_mix_norm_g, m_w_in, m_conv_dw_w, m_conv_dw_b, m_conv_ln_g, m_conv_ln_b, m_w_out, m_ffn_norm_g, m_w_gate, m_w_up, m_w_down, m_final_norm_g, v_meta_tokens, v_mix_norm_g, v_w_in, v_conv_dw_w, v_conv_dw_b, v_conv_ln_g, v_conv_ln_b, v_w_out, v_ffn_norm_g, v_w_gate, v_w_up, v_w_down, v_final_norm_g):
    depth, d, in_shard = w_in.shape
    seq = x.shape[1]
    sb = N_HEADS * HEAD_DIM
    cc = conv_dw_w.shape[2] * N_DEV
    ff = w_gate.shape[2] * N_DEV
    assert in_shard * N_DEV == 3 * sb + 2 * cc and x.shape[0] == 1
    lr = N_META + seq
    lp = -(-lr // ABLK) * ABLK
    me = 4 * lax.axis_index("x") + 2 * lax.axis_index("y") + lax.axis_index("c")

    big_names = ("w_in", "w_out", "w_gate", "w_up", "w_down")
    transposed = {"w_in": True, "w_out": False, "w_gate": True, "w_up": True, "w_down": False}
    shard = dict(w_in=_swap(w_in).astype(BF16), w_out=w_out.astype(BF16), w_gate=_swap(w_gate).astype(BF16),
                 w_up=_swap(w_up).astype(BF16), w_down=w_down.astype(BF16))

    def gather_of(keys):
        names = sorted({n for n, _ in keys}, key=big_names.index)
        jobs = [_Job(names.index(n), j, False, src_layer=i) for j, (n, i) in enumerate(keys)]
        return jobs, [shard[n] for n in names], [SDS((N_DEV,) + shard[n].shape[1:], BF16) for n, _ in keys]

    first_keys = [("w_in", 0)]

    def riding(keys):
        return _Rider(*gather_of(keys)) if keys else None

    def receive(keys, arrays):
        for (n, li), t in zip(keys, arrays):
            wl[li][n] = t.reshape(-1, d)

    jobs, srcs, out_shape = gather_of(first_keys)
    for extra in (meta_tokens, conv_dw_w):
        jobs.append(_Job(len(srcs), len(out_shape), False))
        srcs.append(extra)
        out_shape.append(SDS((N_DEV,) + extra.shape, F32))
    gathered = _exchange(jobs, srcs, out_shape, "gather_first")
    wl = [dict() for _ in range(depth)]
    receive(first_keys, gathered)
    meta_full = _from_cols(gathered[-2])
    taps = jnp.transpose(gathered[-1], (1, 2, 0, 3)).reshape(depth, CONV_K, cc)
    taps = jnp.pad(taps, ((0, 0), (0, CONV_PAD - CONV_K), (0, 0)))
    tri_fwd, tri_after, tri_before = _tri_consts()

    h = jnp.concatenate([meta_full, x[0], jnp.zeros((lp - lr, d), F32)], axis=0)
    saved = []
    for i in range(depth):
        p = wl[i]
        sv = dict(h_in=h)
        qkv, hn = _rms_mm(h, mix_norm_g[i:i + 1], p["w_in"], 0, 3 * sb, BF16, f"proj_qkv_{i}", True)
        keys = [("w_out", 0)] if i == 0 else []
        cacg, *arrived = _rms_mm(h, mix_norm_g[i:i + 1], p["w_in"], 3 * sb, 2 * cc, F32, f"proj_conv_{i}", False,
                                 riding(keys))
        receive(keys, arrived)
        keys = [("w_gate", i), ("w_up", i)]
        attn, rsave, *arrived = _attn_fwd(qkv, tri_fwd, f"attn_fwd_{i}", riding(keys))
        receive(keys, arrived)
        keys = [("w_down", 0)] if i == 0 else []
        yc, *arrived = _conv_fwd_dw(cacg, taps[i], conv_dw_b[i:i + 1], f"conv_fwd_dw_{i}", riding(keys))
        receive(keys, arrived)
        conv = _conv_fwd_ln(yc, conv_ln_g[i:i + 1], conv_ln_b[i:i + 1], f"conv_fwd_ln_{i}")
        h = _mix_out(attn, conv, p["w_out"], h, f"mix_out_{i}")
        sv.update(qkv=qkv, hn=hn, cacg=cacg, rsave=rsave, yc=yc, attn=attn, conv=conv, h_mid=h)
        nxt = i + 1
        keys = [("w_in", nxt), ("w_out", nxt), ("w_down", nxt)] if nxt < depth else []
        h, hn2, act, gate, up, *arrived = _ffn_fwd(h, ffn_norm_g[i:i + 1], p["w_gate"], p["w_up"], p["w_down"],
                                                   f"ffn_fwd_{i}", riding(keys))
        receive(keys, arrived)
        sv.update(hn2=hn2, act=act, gate=gate, up=up)
        saved.append(sv)

    tpad = jnp.pad(loss_target[0], ((N_META, lp - lr), (0, 0)))
    dh, dg_final, loss_part = _loss_head(h, tpad, final_norm_g.reshape(1, d), seq, "loss_head")
    loss = lax.psum(loss_part[0, 0], MESH_AXES)

    parts = {}

    def sending(items):
        srcs = [t.reshape(N_DEV, t.shape[0] // N_DEV, d) for _, t in items]
        return _Rider([_Job(j, j, True) for j in range(len(items))], srcs, [SDS(t.shape, BF16) for t in srcs])

    def arrive(items, arrays):
        parts.update({key: t for (key, _), t in zip(items, arrays)})

    grads = [None] * depth
    from_above = []
    for i in reversed(range(depth)):
        p, sv = wl[i], saved[i]
        d_g, d_u, *got = _ffn_bwd_act(dh, sv["gate"], sv["up"], p["w_down"], f"ffn_bwd_act_{i}", sending(from_above))
        arrive(from_above, got)
        gw_down = _mm_tn(sv["act"], dh, f"grad_w_down_{i}")
        gw_gate = _mm_tn(d_g, sv["hn2"], f"grad_w_gate_{i}")
        gw_up = _mm_tn(d_u, sv["hn2"], f"grad_w_up_{i}")
        items = [(("w_down", i), gw_down)]
        dh, dg_ffn, *got = _ffn_bwd_in(d_g, d_u, p["w_gate"], p["w_up"], sv["h_mid"], ffn_norm_g[i:i + 1], dh,
                                       f"ffn_bwd_in_{i}", sending(items))
        arrive(items, got)
        gw_out = _grad_w_out(sv["attn"], sv["conv"], dh, f"grad_w_out_{i}")
        d_attn, d_conv = _mix_bwd(dh, p["w_out"], sb, cc, f"mix_bwd_{i}")
        items = [(("w_gate", i), gw_gate)]
        dqkv, *got = _attn_bwd(sv["qkv"], d_attn, sv["rsave"], tri_after, tri_before, f"attn_bwd_{i}", sending(items))
        arrive(items, got)
        dyc, dg_ln, db_ln = _conv_bwd_ln(sv["yc"], d_conv, conv_ln_g[i:i + 1], conv_ln_b[i:i + 1], f"conv_bwd_ln_{i}")
        items = [(("w_up", i), gw_up)]
        dcc, g_taps_i, db_conv, *got = _conv_bwd_dw(dyc, sv["cacg"], taps[i], f"conv_bwd_dw_{i}", sending(items))
        arrive(items, got)
        items = [(("w_out", i), gw_out)]
        gw_in, *got = _grad_w_in_t(dqkv, dcc, sv["hn"], f"grad_w_in_{i}", sending(items))
        arrive(items, got)
        from_above = [(("w_in", i), gw_in)]
        items = from_above if i == 0 else []
        dh, dg_mix, *got = _mix_bwd_in(dqkv, dcc, p["w_in"], sv["h_in"], mix_norm_g[i:i + 1], dh, f"mix_bwd_in_{i}",
                                       sending(items))
        arrive(items, got)
        grads[i] = dict(taps=g_taps_i, dg_mix=dg_mix, dg_ffn=dg_ffn, dg_ln=dg_ln, db_ln=db_ln, db_conv=db_conv)
    grad_x = dh[N_META:lr][None]

    per = d // depth
    assert depth * cc <= d and d % depth == 0 and per % 128 == 0
    placed, cuts, row = [], [], 0
    for key in ("dg_mix", "dg_ffn"):
        placed += [(grads[i][key], row + i, 0, True) for i in range(depth)]
        cuts.append(((depth, d), [(slice(0, depth), row, depth, 0, d)]))
        row += depth
    placed.append((dg_final, row, 0, True))
    cuts.append(((1, d), [(slice(0, 1), row, 1, 0, d)]))
    row += 1
    for key in ("db_conv", "dg_ln", "db_ln"):
        placed += [(grads[i][key], row, i * per, True) for i in range(depth)]
        cuts.append(((depth, cc), [(slice(i, i + 1), row, 1, i * per, cc) for i in range(depth)]))
        row += 1
    placed += [(grads[i]["taps"], row, i * per, False) for i in range(depth)]
    cuts.append(((depth, CONV_PAD, cc), [(i, row, CONV_PAD, i * per, cc) for i in range(depth)]))
    row += CONV_PAD
    placed.append((dh[:N_META], row, 0, False))
    cuts.append(((N_META, d), [(slice(0, N_META), row, N_META, 0, d)]))
    row += N_META
    g_mix, g_ffn, g_final, g_cb, g_lg, g_lb, g_taps_full, g_meta_full = _all_reduce_small(
        placed, cuts, -(-row // 8) * 8, d, "reduce_small")
    csh = cc // N_DEV
    g_taps_own = lax.dynamic_slice_in_dim(g_taps_full[:, :CONV_K], me * csh, csh, axis=2)
    msh = d // N_DEV
    g_meta_own = lax.dynamic_slice_in_dim(g_meta_full, me * msh, msh, axis=1)

    row1 = lambda t: t.reshape(1, d)
    small_g = [g_meta_own, g_mix, g_taps_own, g_cb, g_lg, g_lb, g_ffn, g_final]
    small_w = [meta_tokens, mix_norm_g, conv_dw_w, conv_dw_b, conv_ln_g, conv_ln_b, ffn_norm_g, row1(final_norm_g)]
    small_m = [m_meta_tokens, m_mix_norm_g, m_conv_dw_w, m_conv_dw_b, m_conv_ln_g, m_conv_ln_b, m_ffn_norm_g,
               row1(m_final_norm_g)]
    small_v = [v_meta_tokens, v_mix_norm_g, v_conv_dw_w, v_conv_dw_b, v_conv_ln_g, v_conv_ln_b, v_ffn_norm_g,
               row1(v_final_norm_g)]
    s_delta, s_m, s_v = _adamw_small(small_g, small_w, small_m, small_v, "adamw_small")
    unrow = lambda ts: list(ts[:-1]) + [ts[-1].reshape(d)]
    small_g, s_delta, s_m, s_v = unrow(small_g), unrow(s_delta), unrow(s_m), unrow(s_v)

    big = []
    for n, w, m, v in zip(big_names, (w_in, w_out, w_gate, w_up, w_down), (m_w_in, m_w_out, m_w_gate, m_w_up, m_w_down),
                          (v_w_in, v_w_out, v_w_gate, v_w_up, v_w_down)):
        fix = _swap if transposed[n] else (lambda t: t)
        res = _adamw_shard([parts[(n, i)] for i in range(depth)], fix(w), fix(m), fix(v), f"adamw_{n}")
        big.append([fix(t) for t in res])
    b_in, b_out, b_gate, b_up, b_down = big

    def ordered(k, smalls):
        s_meta, s_mix, s_taps, s_cb, s_lg, s_lb, s_ffn, s_final = smalls
        return [s_meta, s_mix, b_in[k], s_taps, s_cb, s_lg, s_lb, b_out[k], s_ffn, b_gate[k], b_up[k], b_down[k], s_final]

    return (loss, grad_x, *ordered(0, small_g), *ordered(1, s_delta), *ordered(2, s_m), *ordered(3, s_v))
```

```python
import math

import jax
import jax.numpy as jnp
from jax import lax
from jax.experimental import pallas as pl
from jax.experimental.pallas import tpu as pltpu

F32 = jnp.float32
BF16 = jnp.bfloat16
SDS = jax.ShapeDtypeStruct

N_META = 16
N_HEADS = 8
HEAD_DIM = 64
CONV_K = 31
CONV_PAD = 32
ABLK = 128
ATT_GROUP = 4
ATT_SPLIT = 2
ATT_TOP = 2
ATT_FLIGHT = 3
EXP_ZERO_AT = 104.0
GONE = 1e30
MASKED = -1e30
FF_CHUNK = 256
FF_BLOCK = 1408
FF_ROWS = 528
EPS = 1e-6
N_DEV = 8
MESH_AXES = ("x", "y", "c")
ADAM_LR = 0.001
ADAM_B1 = 0.9
ADAM_B2 = 0.999
ADAM_EPS = 1e-08
ADAM_WD = 0.01
ADAM_STEP = 10
MIB = 1 << 20
VMEM_LIMIT_MIB = 48
FFN_FWD_VMEM_MIB = 56
MM_ROWS = 1056
EW_ROWS = 528


def _cp(mib=None):
    return pltpu.CompilerParams(vmem_limit_bytes=(mib or VMEM_LIMIT_MIB) * MIB)


def _tile(n, cap, mult):
    best = None
    for t in range(mult, min(n, cap) + 1, mult):
        if n % t == 0:
            best = t
    assert best is not None, (n, cap, mult)
    return best


def _sum8(v):
    r, c = v.shape
    return v.reshape(r // 8, 8, c).sum(axis=0)


def _sigmoid(x):
    return 1.0 / (1.0 + jnp.exp(-x))


def _ff_block(f):
    fb = _tile(f, FF_BLOCK, 128)
    return fb, [(s0, min(FF_CHUNK, fb - s0)) for s0 in range(0, fb, FF_CHUNK)]


NT = (((1,), (1,)), ((), ()))
TN = (((0,), (0,)), ((), ()))


def _rms_mm(h, g, wt, w_row0, n_cols, out_dtype, name, hn_out, rider=None):
    lp, d = h.shape
    tm = _tile(lp, MM_ROWS, 16)
    tn = _tile(math.gcd(n_cols, w_row0), 768, 128)
    off = w_row0 // tn
    ni, nj = lp // tm, n_cols // tn
    rider = rider or _Rider()

    def body(h_ref, g_ref, w_ref, *rest):
        outs, (hn_s,) = rider.split(rest, 2 if hn_out else 1, 1)
        o_ref = outs[0]
        hn_ref = outs[1] if hn_out else None
        rider.start((pl.program_id(0) == 0) & (pl.program_id(1) == 0))
        if ni > 1:
            rider.forward((pl.program_id(0) == ni - 1) & (pl.program_id(1) == 0))

        @pl.when(pl.program_id(1) == 0)
        def _():
            x = h_ref[...]
            r = lax.rsqrt(jnp.mean(x * x, axis=-1, keepdims=True) + EPS)
            hn = ((x * r) * g_ref[...]).astype(BF16)
            hn_s[...] = hn
            if hn_out:
                hn_ref[...] = hn

        o_ref[...] = lax.dot_general(hn_s[...], w_ref[...], NT, preferred_element_type=F32).astype(out_dtype)
        rider.wait((pl.program_id(0) == ni - 1) & (pl.program_id(1) == nj - 1))

    out_shape = [SDS((lp, n_cols), out_dtype)]
    out_specs = [pl.BlockSpec((tm, tn), lambda i, j: (i, j))]
    if hn_out:
        out_shape.append(SDS((lp, d), BF16))
        out_specs.append(pl.BlockSpec((tm, d), lambda i, j: (i, 0)))
    return pl.pallas_call(
        body, out_shape=out_shape + rider.out_shape, grid=(ni, nj),
        in_specs=[pl.BlockSpec((tm, d), lambda i, j: (i, 0)),
                  pl.BlockSpec((1, d), lambda i, j: (0, 0)),
                  pl.BlockSpec((tn, d), lambda i, j: (j + off, 0))] + rider.in_specs(),
        out_specs=out_specs + rider.out_specs(), scratch_shapes=[pltpu.VMEM((tm, d), BF16)] + rider.scratch(),
        name=name, compiler_params=_cp())(h, g, wt, *rider.srcs)


def _mix_bwd(dh, w, sb, cc, name):
    m, k = dh.shape
    tm = _tile(m, MM_ROWS, 16)

    def body(a_ref, w_ref, attn_ref, conv_ref):
        a = a_ref[...].astype(BF16)
        attn_ref[...] = lax.dot_general(a, w_ref[0:sb, :], NT, preferred_element_type=F32).astype(BF16)
        conv_ref[...] = lax.dot_general(a, w_ref[sb:sb + cc, :], NT, preferred_element_type=F32)

    return pl.pallas_call(
        body, out_shape=[SDS((m, sb), BF16), SDS((m, cc), F32)], grid=(m // tm,),
        in_specs=[pl.BlockSpec((tm, k), lambda i: (i, 0)), pl.BlockSpec((sb + cc, k), lambda i: (0, 0))],
        out_specs=[pl.BlockSpec((tm, sb), lambda i: (i, 0)), pl.BlockSpec((tm, cc), lambda i: (i, 0))],
        name=name, compiler_params=_cp())(dh, w)


def _mix_out(attn, conv, w, res, name):
    m, ka = attn.shape
    kc = conv.shape[1]
    n = w.shape[1]
    assert ka == kc
    tm = _tile(m, MM_ROWS, 16)
    tn = _tile(n, 512, 128)

    def body(a_ref, c_ref, wa_ref, wc_ref, r_ref, o_ref):
        o_ref[...] = (r_ref[...] + jnp.dot(a_ref[...], wa_ref[...], preferred_element_type=F32)
                      + jnp.dot(c_ref[...], wc_ref[...], preferred_element_type=F32))

    return pl.pallas_call(
        body, out_shape=SDS((m, n), F32), grid=(m // tm, n // tn),
        in_specs=[pl.BlockSpec((tm, ka), lambda i, j: (i, 0)), pl.BlockSpec((tm, kc), lambda i, j: (i, 0)),
                  pl.BlockSpec((ka, tn), lambda i, j: (0, j)), pl.BlockSpec((kc, tn), lambda i, j: (1, j)),
                  pl.BlockSpec((tm, tn), lambda i, j: (i, j))],
        out_specs=pl.BlockSpec((tm, tn), lambda i, j: (i, j)), name=name, compiler_params=_cp())(attn, conv, w, w, res)


def _mm_tn(a, b, name):
    l, m = a.shape
    n = b.shape[1]
    tm = _tile(m, 1408, 128)
    tn = _tile(n, 1024, 128)
    tl = _tile(l, 1408, 128)
    nl = l // tl

    def body(a_ref, b_ref, o_ref, acc):
        @pl.when(pl.program_id(2) == 0)
        def _():
            acc[...] = jnp.zeros_like(acc)

        acc[...] += lax.dot_general(a_ref[...].astype(BF16), b_ref[...].astype(BF16), TN, preferred_element_type=F32)

        @pl.when(pl.program_id(2) == nl - 1)
        def _():
            o_ref[...] = acc[...].astype(BF16)

    return pl.pallas_call(
        body, out_shape=SDS((m, n), BF16), grid=(m // tm, n // tn, nl),
        in_specs=[pl.BlockSpec((tl, tm), lambda i, j, s: (s, i)),
                  pl.BlockSpec((tl, tn), lambda i, j, s: (s, j))],
        out_specs=pl.BlockSpec((tm, tn), lambda i, j, s: (i, j)),
        scratch_shapes=[pltpu.VMEM((tm, tn), F32)], name=name, compiler_params=_cp())(a, b)


def _grad_w_out(attn, conv, dh, name):
    l, w = attn.shape
    d = dh.shape[1]
    assert conv.shape == attn.shape
    tn = _tile(d, 1024, 128)
    tl = _tile(l, 1408, 128)
    nl = l // tl

    def body(a_ref, c_ref, b_ref, o_ref, acc):
        p, s = pl.program_id(0), pl.program_id(2)

        @pl.when(s == 0)
        def _():
            acc[...] = jnp.zeros_like(acc)

        @pl.when(p == 0)
        def _():
            acc[...] += lax.dot_general(a_ref[...], b_ref[...].astype(BF16), TN, preferred_element_type=F32)

        @pl.when(p == 1)
        def _():
            acc[...] += lax.dot_general(c_ref[...], b_ref[...].astype(BF16), TN, preferred_element_type=F32)

        @pl.when(s == nl - 1)
        def _():
            o_ref[...] = acc[...].astype(BF16)

    out = pl.pallas_call(
        body, out_shape=SDS((2, w, d), BF16), grid=(2, d // tn, nl),
        in_specs=[pl.BlockSpec((tl, w), lambda p, j, s: (jnp.where(p == 0, s, nl - 1), 0)),
                  pl.BlockSpec((tl, w), lambda p, j, s: (jnp.where(p == 1, s, 0), 0)),
                  pl.BlockSpec((tl, tn), lambda p, j, s: (s, j))],
        out_specs=pl.BlockSpec((None, w, tn), lambda p, j, s: (p, 0, j)),
        scratch_shapes=[pltpu.VMEM((w, tn), F32)], name=name, compiler_params=_cp())(attn, conv, dh)
    return out.reshape(2 * w, d)


def _grad_w_in_t(dqkv, dcc, hn, name, rider=None):
    nq, l, w = dqkv.shape
    nc = dcc.shape[0]
    d = hn.shape[1]
    assert dcc.shape[2] == w
    tn = _tile(d, 1024, 128)
    tl = _tile(l, 1408, 128)
    nl = l // tl
    nj = d // tn
    rider = rider or _Rider()

    def body(q_ref, c_ref, b_ref, *rest):
        (o_ref,), (acc,) = rider.split(rest, 1, 1)
        p, jj, s = pl.program_id(0), pl.program_id(1), pl.program_id(2)
        rider.start((p == 0) & (jj == 0) & (s == 0))

        @pl.when(s == 0)
        def _():
            acc[...] = jnp.zeros_like(acc)

        @pl.when(p < nq)
        def _():
            acc[...] += lax.dot_general(q_ref[...], b_ref[...], TN, preferred_element_type=F32)

        @pl.when(p >= nq)
        def _():
            acc[...] += lax.dot_general(c_ref[...], b_ref[...], TN, preferred_element_type=F32)

        @pl.when(s == nl - 1)
        def _():
            o_ref[...] = acc[...].astype(BF16)

        rider.wait((p == nq + nc - 1) & (jj == nj - 1) & (s == nl - 1))

    out, *arrived = pl.pallas_call(
        body, out_shape=[SDS((nq + nc, w, d), BF16)] + rider.out_shape, grid=(nq + nc, nj, nl),
        in_specs=[pl.BlockSpec((None, tl, w), lambda p, j, s: (jnp.minimum(p, nq - 1), s, 0)),
                  pl.BlockSpec((None, tl, w), lambda p, j, s: (jnp.maximum(p - nq, 0), s, 0)),
                  pl.BlockSpec((tl, tn), lambda p, j, s: (s, j))] + rider.in_specs(),
        out_specs=[pl.BlockSpec((None, w, tn), lambda p, j, s: (p, 0, j))] + rider.out_specs(),
        scratch_shapes=[pltpu.VMEM((w, tn), F32)] + rider.scratch(), name=name,
        compiler_params=_cp())(dqkv, dcc, hn, *rider.srcs)
    return [out.reshape((nq + nc) * w, d)] + arrived


def _rms_bwd_tail(acc, h_ref, g_ref, dres_ref, o_ref, dg_ref):
    x = h_ref[...]
    r = lax.rsqrt(jnp.mean(x * x, axis=-1, keepdims=True) + EPS)
    xr = x * r
    dyv = acc[...]
    gy = dyv * g_ref[...]
    o_ref[...] = dres_ref[...] + r * (gy - xr * jnp.mean(gy * xr, axis=-1, keepdims=True))
    dg_ref[...] += _sum8(dyv * xr)


def _mix_bwd_in(dqkv, dcc, wt, h, g, dres, name, rider=None):
    nq, lp, w = dqkv.shape
    nc = dcc.shape[0]
    d = h.shape[1]
    tm = _tile(lp, FF_ROWS, 16)
    ni = lp // tm
    rider = rider or _Rider()

    def body(q_ref, c_ref, w_ref, h_ref, g_ref, dres_ref, *rest):
        (o_ref, dg_ref), (acc,) = rider.split(rest, 2, 1)
        i = pl.program_id(0)
        rider.start(i == 0)

        @pl.when(i == 0)
        def _():
            dg_ref[...] = jnp.zeros_like(dg_ref)

        pieces = [q_ref[p] for p in range(nq)] + [c_ref[p] for p in range(nc)]
        total = None
        for p, piece in enumerate(pieces):
            part = jnp.dot(piece, w_ref[p * w:(p + 1) * w, :], preferred_element_type=F32)
            total = part if total is None else total + part
        acc[...] = total
        _rms_bwd_tail(acc, h_ref, g_ref, dres_ref, o_ref, dg_ref)
        rider.wait(i == ni - 1)

    return pl.pallas_call(
        body, out_shape=[SDS((lp, d), F32), SDS((8, d), F32)] + rider.out_shape, grid=(ni,),
        in_specs=[pl.BlockSpec((nq, tm, w), lambda i: (0, i, 0)),
                  pl.BlockSpec((nc, tm, w), lambda i: (0, i, 0)),
                  pl.BlockSpec(((nq + nc) * w, d), lambda i: (0, 0)),
                  pl.BlockSpec((tm, d), lambda i: (i, 0)),
                  pl.BlockSpec((1, d), lambda i: (0, 0)),
                  pl.BlockSpec((tm, d), lambda i: (i, 0))] + rider.in_specs(),
        out_specs=[pl.BlockSpec((tm, d), lambda i: (i, 0)), pl.BlockSpec((8, d), lambda i: (0, 0))]
        + rider.out_specs(),
        scratch_shapes=[pltpu.VMEM((tm, d), F32)] + rider.scratch(), name=name,
        compiler_params=_cp())(dqkv, dcc, wt, h, g, dres, *rider.srcs)


def _ffn_bwd_in(d_g, d_u, wg_t, wu_t, h, g, dres, name, rider=None):
    lp, f = d_g.shape
    d = h.shape[1]
    tm = _tile(lp, FF_ROWS, 16)
    tk = _tile(f, FF_BLOCK, 128)
    nk = f // tk
    ni = lp // tm
    rider = rider or _Rider()

    def body(dg_in, du_in, wg_ref, wu_ref, h_ref, g_ref, dres_ref, *rest):
        (o_ref, dg_ref), (acc,) = rider.split(rest, 2, 1)
        i, kk = pl.program_id(0), pl.program_id(1)
        rider.start((i == 0) & (kk == 0))

        @pl.when(kk == 0)
        def _():
            acc[...] = jnp.zeros_like(acc)

        @pl.when((kk == 0) & (i == 0))
        def _():
            dg_ref[...] = jnp.zeros_like(dg_ref)

        acc[...] += (jnp.dot(dg_in[...], wg_ref[...], preferred_element_type=F32)
                     + jnp.dot(du_in[...], wu_ref[...], preferred_element_type=F32))

        @pl.when(kk == nk - 1)
        def _():
            _rms_bwd_tail(acc, h_ref, g_ref, dres_ref, o_ref, dg_ref)

        rider.wait((i == ni - 1) & (kk == nk - 1))

    return pl.pallas_call(
        body, out_shape=[SDS((lp, d), F32), SDS((8, d), F32)] + rider.out_shape, grid=(ni, nk),
        in_specs=[pl.BlockSpec((tm, tk), lambda i, j: (i, j)), pl.BlockSpec((tm, tk), lambda i, j: (i, j)),
                  pl.BlockSpec((tk, d), lambda i, j: (j, 0)), pl.BlockSpec((tk, d), lambda i, j: (j, 0)),
                  pl.BlockSpec((tm, d), lambda i, j: (i, 0)),
                  pl.BlockSpec((1, d), lambda i, j: (0, 0)),
                  pl.BlockSpec((tm, d), lambda i, j: (i, 0))] + rider.in_specs(),
        out_specs=[pl.BlockSpec((tm, d), lambda i, j: (i, 0)), pl.BlockSpec((8, d), lambda i, j: (0, 0))]
        + rider.out_specs(),
        scratch_shapes=[pltpu.VMEM((tm, d), F32)] + rider.scratch(), name=name,
        compiler_params=_cp())(d_g, d_u, wg_t, wu_t, h, g, dres, *rider.srcs)


def _ffn_fwd(h, g, wg_t, wu_t, w_d, name, rider=None):
    lp, d = h.shape
    f = w_d.shape[0]
    tm = _tile(lp, FF_ROWS, 16)
    fb, subs = _ff_block(f)
    nc = f // fb
    ni = lp // tm
    rider = rider or _Rider()

    def body(h_ref, g_ref, wg_ref, wu_ref, wd_ref, *rest):
        (o_ref, hn_ref, act_ref, gate_ref, up_ref), (hn_s, acc) = rider.split(rest, 5, 2)
        c = pl.program_id(1)
        rider.start((pl.program_id(0) == 0) & (c == 0))
        if ni > 1:
            rider.forward((pl.program_id(0) == ni - 1) & (c == 0))

        @pl.when(c == 0)
        def _():
            x = h_ref[...]
            r = lax.rsqrt(jnp.mean(x * x, axis=-1, keepdims=True) + EPS)
            hn = ((x * r) * g_ref[...]).astype(BF16)
            hn_s[...] = hn
            hn_ref[...] = hn
            acc[...] = jnp.zeros_like(acc)

        total = None
        for s0, sw in subs:
            gg = lax.dot_general(hn_s[...], wg_ref[s0:s0 + sw, :], NT, preferred_element_type=F32)
            uu = lax.dot_general(hn_s[...], wu_ref[s0:s0 + sw, :], NT, preferred_element_type=F32)
            gate_ref[:, s0:s0 + sw] = gg
            up_ref[:, s0:s0 + sw] = uu
            act = ((gg * _sigmoid(gg)) * uu).astype(BF16)
            act_ref[:, s0:s0 + sw] = act
            part = jnp.dot(act, wd_ref[s0:s0 + sw, :], preferred_element_type=F32)
            total = part if total is None else total + part
        acc[...] += total

        @pl.when(c == nc - 1)
        def _():
            o_ref[...] = h_ref[...] + acc[...]

        rider.wait((pl.program_id(0) == ni - 1) & (c == nc - 1))

    chunk = pl.BlockSpec((fb, d), lambda i, j: (j, 0))
    wide = pl.BlockSpec((tm, fb), lambda i, j: (i, j))
    return pl.pallas_call(
        body, out_shape=[SDS((lp, d), F32), SDS((lp, d), BF16), SDS((lp, f), BF16), SDS((lp, f), F32),
                         SDS((lp, f), F32)] + rider.out_shape, grid=(ni, nc),
        in_specs=[pl.BlockSpec((tm, d), lambda i, j: (i, 0)), pl.BlockSpec((1, d), lambda i, j: (0, 0)),
                  chunk, chunk, chunk] + rider.in_specs(),
        out_specs=[pl.BlockSpec((tm, d), lambda i, j: (i, 0)), pl.BlockSpec((tm, d), lambda i, j: (i, 0)),
                   wide, wide, wide] + rider.out_specs(),
        scratch_shapes=[pltpu.VMEM((tm, d), BF16), pltpu.VMEM((tm, d), F32)] + rider.scratch(),
        name=name, compiler_params=_cp(FFN_FWD_VMEM_MIB))(h, g, wg_t, wu_t, w_d, *rider.srcs)


def _ffn_bwd_act(dh, gate, up, w_d, name, rider=None):
    lp, d = dh.shape
    f = w_d.shape[0]
    tm = _tile(lp, FF_ROWS, 16)
    fb, subs = _ff_block(f)
    nc = f // fb
    ni = lp // tm
    rider = rider or _Rider()

    def body(dh_ref, gate_ref, up_ref, wd_ref, *rest):
        (dg_out, du_out), (dh_s,) = rider.split(rest, 2, 1)
        rider.start((pl.program_id(0) == 0) & (pl.program_id(1) == 0))

        @pl.when(pl.program_id(1) == 0)
        def _():
            dh_s[...] = dh_ref[...].astype(BF16)

        for s0, sw in subs:
            gg = gate_ref[:, s0:s0 + sw]
            uu = up_ref[:, s0:s0 + sw]
            dact = lax.dot_general(dh_s[...], wd_ref[s0:s0 + sw, :], NT, preferred_element_type=F32)
            s = _sigmoid(gg)
            dg_out[:, s0:s0 + sw] = (dact * uu * (s * (1.0 + gg * (1.0 - s)))).astype(BF16)
            du_out[:, s0:s0 + sw] = (dact * (gg * s)).astype(BF16)
        rider.wait((pl.program_id(0) == ni - 1) & (pl.program_id(1) == nc - 1))

    wide = pl.BlockSpec((tm, fb), lambda i, j: (i, j))
    return pl.pallas_call(
        body, out_shape=[SDS((lp, f), BF16), SDS((lp, f), BF16)] + rider.out_shape, grid=(ni, nc),
        in_specs=[pl.BlockSpec((tm, d), lambda i, j: (i, 0)), wide, wide,
                  pl.BlockSpec((fb, d), lambda i, j: (j, 0))] + rider.in_specs(),
        out_specs=[wide, wide] + rider.out_specs(), scratch_shapes=[pltpu.VMEM((tm, d), BF16)] + rider.scratch(),
        name=name, compiler_params=_cp())(dh, gate, up, w_d, *rider.srcs)


def _loss_head(h, tpad, g, n_real, name):
    lp, d = h.shape
    tm = _tile(lp, EW_ROWS, 16)

    def body(h_ref, t_ref, g_ref, dh_ref, dg_ref, loss_ref):
        i = pl.program_id(0)

        @pl.when(i == 0)
        def _():
            dg_ref[...] = jnp.zeros_like(dg_ref)
            loss_ref[...] = jnp.zeros_like(loss_ref)

        x = h_ref[...]
        r = lax.rsqrt(jnp.mean(x * x, axis=-1, keepdims=True) + EPS)
        xr = x * r
        y = xr * g_ref[...]
        row = i * tm + lax.broadcasted_iota(jnp.int32, (tm, d), 0)
        valid = (row >= N_META) & (row < N_META + n_real)
        diff = jnp.where(valid, y - t_ref[...], 0.0)
        loss_ref[...] += jnp.sum(diff * diff) * (0.5 / d)
        dy = diff * (1.0 / d)
        gy = dy * g_ref[...]
        dh_ref[...] = r * (gy - xr * jnp.mean(gy * xr, axis=-1, keepdims=True))
        dg_ref[...] += _sum8(dy * xr)

    return pl.pallas_call(
        body, out_shape=[SDS((lp, d), F32), SDS((8, d), F32), SDS((8, 128), F32)], grid=(lp // tm,),
        in_specs=[pl.BlockSpec((tm, d), lambda i: (i, 0)),
                  pl.BlockSpec((tm, d), lambda i: (i, 0)),
                  pl.BlockSpec((1, d), lambda i: (0, 0))],
        out_specs=[pl.BlockSpec((tm, d), lambda i: (i, 0)),
                   pl.BlockSpec((8, d), lambda i: (0, 0)),
                   pl.BlockSpec((8, 128), lambda i: (0, 0))],
        name=name, compiler_params=_cp())(h, tpad, g)


def _tri_consts():
    j = lax.broadcasted_iota(jnp.int32, (ABLK, ABLK), 0)
    s = lax.broadcasted_iota(jnp.int32, (ABLK, ABLK), 1)
    after = (j >= s).astype(BF16)
    before = (j < s).astype(BF16)
    ones = jnp.ones((ABLK, ABLK), BF16)
    two = lambda t: jnp.concatenate([t, t], axis=0)
    return (two(jnp.concatenate([after, ones], axis=1)),
            two(after),
            two(jnp.concatenate([before, ones], axis=1)))


def _softplus(z):
    neg_abs = lax.bitcast_convert_type(lax.bitcast_convert_type(z, jnp.uint32) | jnp.uint32(0x80000000), F32)
    return jnp.log(1.0 + jnp.exp(neg_abs)) + jnp.maximum(z, 0.0)


def _split_hi_lo(m):
    hi = m.astype(BF16)
    lo = (m - hi.astype(F32)).astype(BF16)
    return jnp.concatenate([hi, lo], axis=1)


def _head_halves(t2, in_a):
    zero = jnp.zeros_like(t2)
    return jnp.concatenate([jnp.where(in_a, t2, zero), jnp.where(in_a, zero, t2)], axis=0)


def _stack_blocks(t, nblk, in_a):
    return jnp.concatenate([_head_halves(t[u * ABLK:(u + 1) * ABLK], in_a) for u in range(nblk)], axis=0)


def _interleave(*gens):
    alive = list(gens)
    while alive:
        for g in list(alive):
            if next(g, alive) is alive:
                alive.remove(g)


def _attn_scale():
    scale = 1.0 / math.sqrt(HEAD_DIM)
    assert math.frexp(scale)[0] == 0.5, "a power of two, so that scaling q in bf16 is exact"
    return scale


def _pow2_below(n):
    assert n & (n - 1) == 0
    return [p for p in (64, 32, 16, 8, 4, 2, 1) if p < n]


class _Rider:
    def __init__(self, jobs=(), srcs=(), out_shape=()):
        self.jobs, self.srcs, self.out_shape = list(jobs), list(srcs), list(out_shape)
        self.any = [pl.BlockSpec(memory_space=pl.ANY)]

    def split(self, rest, n_out, n_scratch):
        ni, no = len(self.srcs), len(self.out_shape)
        self.ins, outs = rest[:ni], rest[ni:ni + n_out]
        self.outs = rest[ni + n_out:ni + n_out + no]
        scratch = rest[ni + n_out + no:ni + n_out + no + n_scratch]
        self.sems = rest[ni + n_out + no + n_scratch:]
        return outs, scratch

    def start(self, first):
        self.forwarded = False
        if self.jobs:
            @pl.when(first)
            def _():
                _exchange_start(self.jobs, self.ins, self.outs, *self.sems)

    def forward(self, late):
        self.forwarded = True
        if self.jobs:
            @pl.when(late)
            def _():
                _exchange_forward(self.jobs, self.ins, self.outs, *self.sems)

    def wait(self, last):
        if self.jobs:
            @pl.when(last)
            def _():
                if not self.forwarded:
                    _exchange_forward(self.jobs, self.ins, self.outs, *self.sems)
                _exchange_wait(self.jobs, self.ins, self.outs, *self.sems)

    def in_specs(self):
        return self.any * len(self.srcs)

    def out_specs(self):
        return self.any * len(self.out_shape)

    def scratch(self):
        return _exchange_sems(len(self.jobs)) if self.jobs else []


def _attn_fwd(qkv, tri_fwd, name, rider=None):
    lp = qkv.shape[0]
    n_pairs = (N_HEADS * HEAD_DIM) // 128
    nb = lp // ABLK
    assert nb <= 128 and 2 * HEAD_DIM == 128
    scale = _attn_scale()
    nt = (((1,), (1,)), ((), ()))
    rider = rider or _Rider()

    def body(q_ref, k_ref, v_ref, tri_ref, *rest):
        (o_ref, rs_ref), (r_s, acc_s, rs_s) = rider.split(rest, 2, 3)
        rider.start(pl.program_id(0) == 0)
        lane = lax.broadcasted_iota(jnp.int32, (ABLK, 128), 1)
        row = lax.broadcasted_iota(jnp.int32, (ABLK, 128), 0)
        in_a = lane < HEAD_DIM
        causal = lane < row

        def begin(st, i):
            r_s[st] = jnp.zeros(r_s.shape[1:], F32)
            acc_s[st] = jnp.zeros(acc_s.shape[1:], F32)
            rs_s[st] = jnp.full(rs_s.shape[1:], GONE, F32)
            return q_ref[pl.ds(pl.multiple_of(i * ABLK, ABLK), ABLK), :] * scale

        def live(st):
            least = jnp.min(jnp.minimum(r_s[st, 0], r_s[st, 1]), axis=0, keepdims=True)
            return (least[0, 0] < EXP_ZERO_AT).astype(jnp.int32)

        def step(*args):
            _interleave(stages(*args))

        def stages(st, q2, kb0, nblk, diag):
            k0 = pl.multiple_of(kb0 * ABLK, ABLK)
            kbd = _stack_blocks(k_ref[pl.ds(k0, nblk * ABLK), :], nblk, in_a)
            vbd = _stack_blocks(v_ref[pl.ds(k0, nblk * ABLK), :], nblk, in_a)
            z = lax.dot_general(q2, kbd, nt, preferred_element_type=F32)
            ncol = 2 * nblk
            zt = [z[:, c * 128:(c + 1) * 128] for c in range(ncol)]
            if diag:
                zt = [jnp.where(causal, t, MASKED) if c >= ncol - 2 else t for c, t in enumerate(zt)]
            yield
            bounds = [ncol * j // ATT_SPLIT for j in range(ATT_SPLIT + 1)]
            ce = [None] * ncol
            for c0, c1 in reversed([b for b in zip(bounds[:-1], bounds[1:]) if b[0] < b[1]]):
                parts = [_split_hi_lo(_softplus(zt[c])) for c in range(c0, c1)]
                got = jnp.dot(jnp.concatenate(parts, axis=0), tri_ref[...], preferred_element_type=F32)
                for c in range(c0, c1):
                    ce[c] = got[(c - c0) * 128:(c - c0 + 1) * 128]
                yield
            rr = [r_s[st, 0], r_s[st, 1]]
            rsv = [rs_s[st, :, :128], rs_s[st, :, 128:]]
            ws = [None] * ncol
            for u in reversed(range(nblk)):
                for hh in range(2):
                    c = 2 * u + hh
                    ws[c] = jnp.exp(zt[c] - ce[c][:, :128] - rr[hh]).astype(BF16)
                    rsv[hh] = jnp.where(lane == kb0 + u, rr[hh], rsv[hh])
                    rr[hh] = rr[hh] + ce[c][:, 128:]
            acc_s[st] += jnp.dot(jnp.concatenate(ws, axis=1), vbd, preferred_element_type=F32)
            r_s[st, 0] = rr[0]
            r_s[st, 1] = rr[1]
            rs_s[st, :, :128] = rsv[0]
            rs_s[st, :, 128:] = rsv[1]

        def finish(st, q2, i, n_top):
            i_low = i - n_top
            n_grp = i_low // ATT_GROUP

            def more(c):
                return (c[0] < n_grp) & (c[1] > 0)

            def inner(c):
                step(st, q2, i_low - ATT_GROUP * (c[0] + 1), ATT_GROUP, False)
                return c[0] + 1, live(st)

            _, alive = lax.while_loop(more, inner, (jnp.int32(0), live(st)))
            rem = i_low - ATT_GROUP * n_grp
            for p in _pow2_below(ATT_GROUP):
                def last_steps(p=p):
                    step(st, q2, rem & (p - 1), p, False)
                    return live(st)

                alive = lax.cond(((rem & p) != 0) & (alive > 0), last_steps, lambda alive=alive: alive)

            q0 = pl.multiple_of(i * ABLK, ABLK)
            o_ref[pl.ds(q0, ABLK), :] = acc_s[st].astype(BF16)
            rs_ref[pl.ds(q0, ABLK), :] = rs_s[st]

        def single(i):
            q2 = begin(0, i)
            n_top = jnp.minimum(i, ATT_TOP)
            for t in range(ATT_TOP + 1):
                @pl.when(n_top == t)
                def _():
                    step(0, q2, i - t, t + 1, True)

            finish(0, q2, i, n_top)

        def together(i):
            qs = [begin(st, i + st) for st in range(ATT_FLIGHT)]
            _interleave(*[stages(st, qs[st], i + st - ATT_TOP, ATT_TOP + 1, True) for st in range(ATT_FLIGHT)])
            for st in range(ATT_FLIGHT):
                finish(st, qs[st], i + st, ATT_TOP)

        n_head = min(ATT_TOP, nb)
        n_group = (nb - n_head) // ATT_FLIGHT
        n_single = nb - ATT_FLIGHT * n_group

        def singles(k, carry):
            single(jnp.where(k < n_head, k, nb - n_single + k))
            return carry

        def groups(j, carry):
            together(n_head + ATT_FLIGHT * j)
            return carry

        lax.fori_loop(0, n_single, singles, 0)
        lax.fori_loop(0, n_group, groups, 0)
        rider.wait(pl.program_id(0) == n_pairs - 1)

    col = lambda o: (lambda p: (0, p + o))
    return pl.pallas_call(
        body, out_shape=[SDS((lp, n_pairs * 128), BF16), SDS((lp, n_pairs * 256), F32)] + rider.out_shape,
        grid=(n_pairs,),
        in_specs=[pl.BlockSpec((lp, 128), col(0)), pl.BlockSpec((lp, 128), col(n_pairs)),
                  pl.BlockSpec((lp, 128), col(2 * n_pairs)), pl.BlockSpec((256, 256), lambda p: (0, 0))]
        + rider.in_specs(),
        out_specs=[pl.BlockSpec((lp, 128), col(0)), pl.BlockSpec((lp, 256), col(0))] + rider.out_specs(),
        scratch_shapes=[pltpu.VMEM((ATT_FLIGHT, 2, ABLK, 128), F32), pltpu.VMEM((ATT_FLIGHT, ABLK, 128), F32),
                        pltpu.VMEM((ATT_FLIGHT, ABLK, 256), F32)] + rider.scratch(),
        name=name, compiler_params=_cp())(qkv, qkv, qkv, tri_fwd, *rider.srcs)


def _attn_bwd(qkv, d_out, rsave, tri_after, tri_before, name, rider=None):
    lp = qkv.shape[0]
    n_pairs = (N_HEADS * HEAD_DIM) // 128
    nb = lp // ABLK
    scale = _attn_scale()
    nt = (((1,), (1,)), ((), ()))
    tn = (((0,), (0,)), ((), ()))
    rider = rider or _Rider()

    def body(q_ref, k_ref, v_ref, do_ref, rs_ref, ta_ref, tb_ref, *rest):
        (o_ref,), (dk_s, dv_s, dq_s, pc_s) = rider.split(rest, 1, 4)
        rider.start(pl.program_id(0) == 0)
        lane = lax.broadcasted_iota(jnp.int32, (ABLK, 128), 1)
        row = lax.broadcasted_iota(jnp.int32, (ABLK, 128), 0)
        in_a = lane < HEAD_DIM
        causal = lane < row
        dk_s[...] = jnp.zeros_like(dk_s)
        dv_s[...] = jnp.zeros_like(dv_s)

        def begin(st, i):
            q0 = pl.multiple_of(i * ABLK, ABLK)
            q2 = q_ref[pl.ds(q0, ABLK), :] * scale
            do2 = do_ref[pl.ds(q0, ABLK), :]
            dq_s[st] = jnp.zeros(dq_s.shape[1:], F32)
            pc_s[st] = jnp.zeros(pc_s.shape[1:], F32)
            return dict(q0=q0, q2=q2, do2=do2, q_st=_head_halves(q2, in_a), do_st=_head_halves(do2, in_a))

        def step(*args):
            _interleave(stages(*args))

        def stages(st, blk, kb0, nblk, diag):
            q0, q2, do2 = blk["q0"], blk["q2"], blk["do2"]
            k0 = pl.multiple_of(kb0 * ABLK, ABLK)
            kbd = _stack_blocks(k_ref[pl.ds(k0, nblk * ABLK), :], nblk, in_a)
            vbd = _stack_blocks(v_ref[pl.ds(k0, nblk * ABLK), :], nblk, in_a)
            z = lax.dot_general(q2, kbd, nt, preferred_element_type=F32)
            dw = lax.dot_general(do2, vbd, nt, preferred_element_type=F32)
            ncol = 2 * nblk
            zt = [z[:, c * 128:(c + 1) * 128] for c in range(ncol)]
            if diag:
                zt = [jnp.where(causal, t, MASKED) if c >= ncol - 2 else t for c, t in enumerate(zt)]
            bounds = [ncol * j // ATT_SPLIT for j in range(ATT_SPLIT + 1)]
            batches = [range(c0, c1) for c0, c1 in zip(bounds[:-1], bounds[1:]) if c0 < c1]
            sps, ex, ws, dls, pe = [None] * ncol, [None] * ncol, [None] * ncol, [None] * ncol, [None] * ncol

            def mass(cols):
                for c in cols:
                    sps[c] = _softplus(zt[c])
                got = jnp.dot(jnp.concatenate([_split_hi_lo(sps[c]) for c in cols], axis=0), ta_ref[...],
                              preferred_element_type=F32)
                for j, c in enumerate(cols):
                    ex[c] = got[j * 128:(j + 1) * 128]

            def weights(cols):
                for c in cols:
                    u, hh = c // 2, c % 2
                    r_saved = jnp.sum(jnp.where(lane == kb0 + u, rs_ref[pl.ds(q0, ABLK), hh * 128:(hh + 1) * 128],
                                                0.0), axis=1, keepdims=True)
                    w = jnp.exp(zt[c] - ex[c] - r_saved)
                    ws[c] = w.astype(BF16)
                    dls[c] = dw[:, c * 128:(c + 1) * 128] * w
                got = jnp.dot(jnp.concatenate([_split_hi_lo(dls[c]) for c in cols], axis=0), tb_ref[...],
                              preferred_element_type=F32)
                for j, c in enumerate(cols):
                    pe[c] = got[j * 128:(j + 1) * 128]

            yield
            mass(batches[0])
            yield
            for j in range(len(batches)):
                if j + 1 < len(batches):
                    mass(batches[j + 1])
                    yield
                weights(batches[j])
                yield
            pc = [pc_s[st, 0], pc_s[st, 1]]
            dzs = []
            for c in range(ncol):
                hh = c % 2
                one_minus_beta = jnp.exp(-sps[c])
                dz = dls[c] * one_minus_beta - (pe[c][:, :128] + pc[hh]) * (1.0 - one_minus_beta)
                pc[hh] = pc[hh] + pe[c][:, 128:]
                dzs.append(dz.astype(BF16))
            pc_s[st, 0] = pc[0]
            pc_s[st, 1] = pc[1]
            dq_s[st] += jnp.dot(jnp.concatenate(dzs, axis=1), kbd, preferred_element_type=F32)
            by_head = lambda ts: jnp.concatenate([jnp.concatenate(ts[0::2], axis=1), jnp.concatenate(ts[1::2], axis=1)],
                                                 axis=0)
            rows = pl.ds(k0, nblk * ABLK)
            dk_s[rows, :] += lax.dot_general(by_head(dzs), blk["q_st"], tn, preferred_element_type=F32)
            dv_s[rows, :] += lax.dot_general(by_head(ws), blk["do_st"], tn, preferred_element_type=F32)

        def below(st, blk, i, n_top):
            gone = jnp.min(rs_ref[pl.ds(blk["q0"], ABLK), :], axis=0, keepdims=True) >= EXP_ZERO_AT
            lane1 = lax.broadcasted_iota(jnp.int32, (1, 128), 1)
            first = jnp.sum(jnp.where(gone[:, :128] & gone[:, 128:] & (lane1 < i), 1.0, 0.0)).astype(jnp.int32)
            i_low = i - n_top
            n_grp = i_low // ATT_GROUP
            rem = i_low - ATT_GROUP * n_grp
            for p in reversed(_pow2_below(ATT_GROUP)):
                @pl.when(((rem & p) != 0) & ((rem & (p - 1)) + p > first))
                def _():
                    step(st, blk, rem & (p - 1), p, False)

            def inner(g, c2):
                step(st, blk, rem + ATT_GROUP * g, ATT_GROUP, False)
                return c2

            lax.fori_loop(jnp.maximum(first - rem, 0) // ATT_GROUP, n_grp, inner, 0)

        def done(st, blk):
            o_ref[0, pl.ds(blk["q0"], ABLK), :] = (dq_s[st] * scale).astype(BF16)

        def single(i):
            blk = begin(0, i)
            n_top = jnp.minimum(i, ATT_TOP)
            below(0, blk, i, n_top)
            for t in range(ATT_TOP + 1):
                @pl.when(n_top == t)
                def _():
                    step(0, blk, i - t, t + 1, True)

            done(0, blk)

        def together(i):
            blks = [begin(st, i + st) for st in range(ATT_FLIGHT)]
            for st in range(ATT_FLIGHT):
                below(st, blks[st], i + st, ATT_TOP)
            _interleave(*[stages(st, blks[st], i + st - ATT_TOP, ATT_TOP + 1, True) for st in range(ATT_FLIGHT)])
            for st in range(ATT_FLIGHT):
                done(st, blks[st])

        n_head = min(ATT_TOP, nb)
        n_group = (nb - n_head) // ATT_FLIGHT
        n_single = nb - ATT_FLIGHT * n_group

        def singles(k, carry):
            single(jnp.where(k < n_head, k, nb - n_single + k))
            return carry

        def groups(j, carry):
            together(n_head + ATT_FLIGHT * j)
            return carry

        lax.fori_loop(0, n_single, singles, 0)
        lax.fori_loop(0, n_group, groups, 0)
        o_ref[1] = dk_s[...].astype(BF16)
        o_ref[2] = dv_s[...].astype(BF16)
        rider.wait(pl.program_id(0) == n_pairs - 1)

    col = lambda o: (lambda p: (0, p + o))
    return pl.pallas_call(
        body, out_shape=[SDS((3, lp, n_pairs * 128), BF16)] + rider.out_shape, grid=(n_pairs,),
        in_specs=[pl.BlockSpec((lp, 128), col(0)), pl.BlockSpec((lp, 128), col(n_pairs)),
                  pl.BlockSpec((lp, 128), col(2 * n_pairs)), pl.BlockSpec((lp, 128), col(0)),
                  pl.BlockSpec((lp, 256), col(0)),
                  pl.BlockSpec((256, 128), lambda p: (0, 0)), pl.BlockSpec((256, 256), lambda p: (0, 0))]
        + rider.in_specs(),
        out_specs=[pl.BlockSpec((3, lp, 128), lambda p: (0, 0, p))] + rider.out_specs(),
        scratch_shapes=[pltpu.VMEM((lp, 128), F32), pltpu.VMEM((lp, 128), F32),
                        pltpu.VMEM((ATT_FLIGHT, ABLK, 128), F32), pltpu.VMEM((ATT_FLIGHT, 2, ABLK, 128), F32)]
        + rider.scratch(),
        name=name, compiler_params=_cp())(qkv, qkv, qkv, d_out, rsave, tri_after, tri_before, *rider.srcs)


def _conv_fwd_dw(cacg, w, b, name, rider=None):
    lp = cacg.shape[0]
    c = cacg.shape[1] // 2
    ncb = c // 128
    nchunk = lp // ABLK
    off = CONV_PAD - (CONV_K - 1)
    rider = rider or _Rider()

    def body(a_ref, g_ref, w_ref, b_ref, *rest):
        (y_ref,), (upad,) = rider.split(rest, 1, 1)
        rider.start(pl.program_id(0) == 0)
        upad[0:CONV_PAD, :] = jnp.zeros((CONV_PAD, 128), F32)

        def fill(ch, carry):
            base = pl.multiple_of(ch * ABLK, ABLK)
            upad[pl.ds(base + CONV_PAD, ABLK), :] = a_ref[pl.ds(base, ABLK), :] * _sigmoid(g_ref[pl.ds(base, ABLK), :])
            return carry

        lax.fori_loop(0, nchunk, fill, 0)

        def comp(ch, carry):
            base = pl.multiple_of(ch * ABLK, ABLK)
            acc = jnp.zeros((ABLK, 128), F32)
            for k in range(CONV_K):
                acc = acc + upad[pl.ds(base + (off + k), ABLK), :] * w_ref[k:k + 1, :]
            y_ref[pl.ds(base, ABLK), :] = acc + b_ref[...]
            return carry

        lax.fori_loop(0, nchunk, comp, 0)
        rider.wait(pl.program_id(0) == ncb - 1)

    return pl.pallas_call(
        body, out_shape=[SDS((lp, c), F32)] + rider.out_shape, grid=(ncb,),
        in_specs=[pl.BlockSpec((lp, 128), lambda j: (0, j)), pl.BlockSpec((lp, 128), lambda j: (0, j + ncb)),
                  pl.BlockSpec((CONV_PAD, 128), lambda j: (0, j)), pl.BlockSpec((1, 128), lambda j: (0, j))]
        + rider.in_specs(),
        out_specs=[pl.BlockSpec((lp, 128), lambda j: (0, j))] + rider.out_specs(),
        scratch_shapes=[pltpu.VMEM((lp + CONV_PAD, 128), F32)] + rider.scratch(), name=name,
        compiler_params=_cp())(cacg, cacg, w, b, *rider.srcs)


def _ln_parts(x, g, b):
    mu = jnp.mean(x, axis=-1, keepdims=True)
    xc = x - mu
    rstd = lax.rsqrt(jnp.mean(xc * xc, axis=-1, keepdims=True) + EPS)
    xh = xc * rstd
    return xh, rstd, xh * g + b


def _conv_fwd_ln(yc, g, b, name):
    lp, c = yc.shape
    tm = _tile(lp, EW_ROWS, 16)

    def body(y_ref, g_ref, b_ref, o_ref):
        _, _, ln = _ln_parts(y_ref[...], g_ref[...], b_ref[...])
        o_ref[...] = (ln * _sigmoid(ln)).astype(BF16)

    return pl.pallas_call(
        body, out_shape=SDS((lp, c), BF16), grid=(lp // tm,),
        in_specs=[pl.BlockSpec((tm, c), lambda i: (i, 0)), pl.BlockSpec((1, c), lambda i: (0, 0)),
                  pl.BlockSpec((1, c), lambda i: (0, 0))],
        out_specs=pl.BlockSpec((tm, c), lambda i: (i, 0)), name=name, compiler_params=_cp())(yc, g, b)


def _conv_bwd_ln(yc, dout, g, b, name):
    lp, c = yc.shape
    tm = _tile(lp, EW_ROWS, 16)

    def body(y_ref, d_ref, g_ref, b_ref, o_ref, dg_ref, db_ref):
        @pl.when(pl.program_id(0) == 0)
        def _():
            dg_ref[...] = jnp.zeros_like(dg_ref)
            db_ref[...] = jnp.zeros_like(db_ref)

        xh, rstd, ln = _ln_parts(y_ref[...], g_ref[...], b_ref[...])
        s = _sigmoid(ln)
        dln = d_ref[...] * (s * (1.0 + ln * (1.0 - s)))
        dg_ref[...] += _sum8(dln * xh)
        db_ref[...] += _sum8(dln)
        dxh = dln * g_ref[...]
        o_ref[...] = rstd * (dxh - jnp.mean(dxh, axis=-1, keepdims=True)
                             - xh * jnp.mean(dxh * xh, axis=-1, keepdims=True))

    return pl.pallas_call(
        body, out_shape=[SDS((lp, c), F32), SDS((8, c), F32), SDS((8, c), F32)], grid=(lp // tm,),
        in_specs=[pl.BlockSpec((tm, c), lambda i: (i, 0)), pl.BlockSpec((tm, c), lambda i: (i, 0)),
                  pl.BlockSpec((1, c), lambda i: (0, 0)), pl.BlockSpec((1, c), lambda i: (0, 0))],
        out_specs=[pl.BlockSpec((tm, c), lambda i: (i, 0)), pl.BlockSpec((8, c), lambda i: (0, 0)),
                   pl.BlockSpec((8, c), lambda i: (0, 0))],
        name=name, compiler_params=_cp())(yc, dout, g, b)


def _conv_bwd_dw(dyc, cacg, w, name):
    lp, c = dyc.shape
    ncb = c // 128
    nchunk = lp // ABLK
    off = CONV_PAD - (CONV_K - 1)

    def body(dy_ref, a_ref, g_ref, w_ref, dcc_ref, dw_ref, db_ref, upad, dypad, dwacc):
        upad[0:CONV_PAD, :] = jnp.zeros((CONV_PAD, 128), F32)
        dypad[lp:lp + CONV_PAD, :] = jnp.zeros((CONV_PAD, 128), F32)
        dwacc[...] = jnp.zeros_like(dwacc)
        db_ref[...] = jnp.zeros_like(db_ref)

        def fill(ch, carry):
            base = pl.multiple_of(ch * ABLK, ABLK)
            upad[pl.ds(base + CONV_PAD, ABLK), :] = a_ref[pl.ds(base, ABLK), :] * _sigmoid(g_ref[pl.ds(base, ABLK), :])
            dypad[pl.ds(base, ABLK), :] = dy_ref[pl.ds(base, ABLK), :]
            return carry

        lax.fori_loop(0, nchunk, fill, 0)

        def comp(ch, carry):
            base = pl.multiple_of(ch * ABLK, ABLK)
            dy = dy_ref[pl.ds(base, ABLK), :]
            du = jnp.zeros((ABLK, 128), F32)
            for k in range(CONV_K):
                du = du + dypad[pl.ds(base + (CONV_K - 1 - k), ABLK), :] * w_ref[k:k + 1, :]
                dwacc[k * 8:(k + 1) * 8, :] += _sum8(dy * upad[pl.ds(base + (off + k), ABLK), :])
            db_ref[...] += _sum8(dy)
            a = a_ref[pl.ds(base, ABLK), :]
            s = _sigmoid(g_ref[pl.ds(base, ABLK), :])
            dcc_ref[0, pl.ds(base, ABLK), :] = (du * s).astype(BF16)
            dcc_ref[1, pl.ds(base, ABLK), :] = (du * a * (s * (1.0 - s))).astype(BF16)
            return carry

        lax.fori_loop(0, nchunk, comp, 0)
        dw_ref[...] = dwacc[...].reshape(CONV_PAD, 8, 128).sum(axis=1)

    return pl.pallas_call(
        body, out_shape=[SDS((2, lp, c), BF16), SDS((CONV_PAD, c), F32), SDS((8, c), F32)],
        grid=(ncb,),
        in_specs=[pl.BlockSpec((lp, 128), lambda j: (0, j)), pl.BlockSpec((lp, 128), lambda j: (0, j)),
                  pl.BlockSpec((lp, 128), lambda j: (0, j + ncb)), pl.BlockSpec((CONV_PAD, 128), lambda j: (0, j))],
        out_specs=[pl.BlockSpec((2, lp, 128), lambda j: (0, 0, j)),
                   pl.BlockSpec((CONV_PAD, 128), lambda j: (0, j)), pl.BlockSpec((8, 128), lambda j: (0, j))],
        scratch_shapes=[pltpu.VMEM((lp + CONV_PAD, 128), F32), pltpu.VMEM((lp + CONV_PAD, 128), F32),
                        pltpu.VMEM((CONV_PAD * 8, 128), F32)],
        name=name, compiler_params=_cp())(dyc, cacg, cacg, w)


def _mesh_pos():
    x, y, c = lax.axis_index("x"), lax.axis_index("y"), lax.axis_index("c")
    return x, y, c


def _peer(pos, r):
    x, y, c = pos
    px = (1 - x) if (r >> 2) & 1 else x
    py = (1 - y) if (r >> 1) & 1 else y
    pc = (1 - c) if r & 1 else c
    return (px, py, pc), 4 * px + 2 * py + pc


SIBLING = 1
OTHER_CHIPS = (2, 4, 6)


class _Job:
    def __init__(self, src, dst, scatter, src_layer=None, dst_layer=None):
        self.src, self.dst, self.scatter, self.src_layer, self.dst_layer = src, dst, scatter, src_layer, dst_layer

    def src_view(self, ins, idx):
        v = ins[self.src] if self.src_layer is None else ins[self.src].at[self.src_layer]
        return v.at[idx] if self.scatter else v

    def dst_view(self, outs, slot):
        v = outs[self.dst] if self.dst_layer is None else outs[self.dst].at[self.dst_layer]
        return v.at[slot]


def _remote(job, j, r, src, dst, to, send, recv):
    return pltpu.make_async_remote_copy(src_ref=src, dst_ref=dst, send_sem=send.at[j, r - 1], recv_sem=recv.at[j, r - 1],
                                        device_id=to, device_id_type=pl.DeviceIdType.MESH)


def _exchange_start(jobs, ins, outs, send, recv, loc):
    pos = _mesh_pos()
    me = 4 * pos[0] + 2 * pos[1] + pos[2]
    for j, job in enumerate(jobs):
        pltpu.make_async_copy(job.src_view(ins, me), job.dst_view(outs, me), loc.at[j]).start()
        for r in (range(1, N_DEV) if job.scatter else (SIBLING,) + OTHER_CHIPS):
            peer, peer_idx = _peer(pos, r)
            _remote(job, j, r, job.src_view(ins, peer_idx), job.dst_view(outs, me), peer, send, recv).start()


def _exchange_forward(jobs, ins, outs, send, recv, loc):
    pos = _mesh_pos()
    sibling, _ = _peer(pos, SIBLING)
    for j, job in enumerate(jobs):
        if job.scatter:
            continue
        for r in OTHER_CHIPS:
            peer, peer_idx = _peer(pos, r)
            slot = job.dst_view(outs, peer_idx)
            _remote(job, j, r, job.src_view(ins, peer_idx), slot, peer, send, recv).wait_recv()
            _remote(job, j, r ^ SIBLING, slot, slot, sibling, send, recv).start()


def _exchange_wait(jobs, ins, outs, send, recv, loc):
    pos = _mesh_pos()
    me = 4 * pos[0] + 2 * pos[1] + pos[2]
    for j, job in enumerate(jobs):
        for r in range(1, N_DEV):
            peer, peer_idx = _peer(pos, r)
            cp = _remote(job, j, r, job.src_view(ins, peer_idx), job.dst_view(outs, peer_idx), peer, send, recv)
            if job.scatter or r not in OTHER_CHIPS:
                cp.wait_recv()
            cp.wait_send()
        pltpu.make_async_copy(job.src_view(ins, me), job.dst_view(outs, me), loc.at[j]).wait()


def _exchange_sems(n_jobs):
    return [pltpu.SemaphoreType.DMA((n_jobs, N_DEV - 1)), pltpu.SemaphoreType.DMA((n_jobs, N_DEV - 1)),
            pltpu.SemaphoreType.DMA((n_jobs,))]


def _exchange(jobs, arrs, out_shape, name):
    n_in, n_out = len(arrs), len(out_shape)
    any_spec = pl.BlockSpec(memory_space=pl.ANY)

    def body(*refs):
        ins, outs, sems = refs[:n_in], refs[n_in:n_in + n_out], refs[n_in + n_out:]
        _exchange_start(jobs, ins, outs, *sems)
        _exchange_forward(jobs, ins, outs, *sems)
        _exchange_wait(jobs, ins, outs, *sems)

    return pl.pallas_call(
        body, out_shape=out_shape, in_specs=[any_spec] * n_in, out_specs=[any_spec] * n_out,
        scratch_shapes=_exchange_sems(len(jobs)),
        name=name, compiler_params=pltpu.CompilerParams(has_side_effects=True))(*arrs)


def _all_reduce_small(placed, cuts, rows, w, name):
    n_in = len(placed)

    def body(*refs):
        in_refs, o_refs = refs[:n_in], refs[n_in:n_in + len(cuts)]
        buf, send, recv = refs[n_in + len(cuts):]
        pos = _mesh_pos()
        me = 4 * pos[0] + 2 * pos[1] + pos[2]
        buf[me] = jnp.zeros((rows, w), F32)
        for ref, (arr, row0, col0, is_partial) in zip(in_refs, placed):
            val = ref[...].sum(axis=0, keepdims=True) if is_partial else ref[...]
            buf[me, row0:row0 + val.shape[0], col0:col0 + val.shape[1]] = val
        for r in range(1, N_DEV):
            peer, _ = _peer(pos, r)
            pltpu.make_async_remote_copy(src_ref=buf.at[me], dst_ref=buf.at[me], send_sem=send.at[r - 1],
                                         recv_sem=recv.at[r - 1], device_id=peer,
                                         device_id_type=pl.DeviceIdType.MESH).start()
        for r in range(1, N_DEV):
            peer, peer_idx = _peer(pos, r)
            cp = pltpu.make_async_remote_copy(src_ref=buf.at[me], dst_ref=buf.at[peer_idx], send_sem=send.at[r - 1],
                                              recv_sem=recv.at[r - 1], device_id=peer,
                                              device_id_type=pl.DeviceIdType.MESH)
            cp.wait_recv()
            cp.wait_send()
        acc = buf[0]
        for dev in range(1, N_DEV):
            acc = acc + buf[dev]
        for o_ref, (_, pieces) in zip(o_refs, cuts):
            for index, row0, nrows, col0, ncols in pieces:
                o_ref[index] = acc[row0:row0 + nrows, col0:col0 + ncols].reshape(o_ref.at[index].shape)

    vmem = pl.BlockSpec(memory_space=pltpu.VMEM)
    return pl.pallas_call(
        body, out_shape=[SDS(shape, F32) for shape, _ in cuts], in_specs=[vmem] * n_in, out_specs=[vmem] * len(cuts),
        scratch_shapes=[pltpu.VMEM((N_DEV, rows, w), F32), pltpu.SemaphoreType.DMA((N_DEV - 1,)),
                        pltpu.SemaphoreType.DMA((N_DEV - 1,))],
        name=name, compiler_params=pltpu.CompilerParams(has_side_effects=True))(*[p[0] for p in placed])


def _adamw_math(w, g, m, v):
    m = ADAM_B1 * m + (1.0 - ADAM_B1) * g
    v = ADAM_B2 * v + (1.0 - ADAM_B2) * (g * g)
    m_hat = m / (1.0 - ADAM_B1 ** ADAM_STEP)
    v_hat = v / (1.0 - ADAM_B2 ** ADAM_STEP)
    delta = -ADAM_LR * (m_hat / (jnp.sqrt(v_hat) + ADAM_EPS) + ADAM_WD * w)
    return delta, m, v


def _adamw_shard(parts, w, m, v, name):
    depth = len(parts)
    _, rr, cc = parts[0].shape
    tr = _tile(rr, 256, 8)
    nt = rr // tr
    part_block, blk = (N_DEV, tr, cc), pl.BlockSpec((None, tr, cc), lambda l, i: (l, i, 0))

    def body(*refs):
        p_refs = refs[:depth]
        w_ref, m_ref, v_ref, g_out, d_out, m_out, v_out = refs[depth:]
        for li in range(depth):
            @pl.when(pl.program_id(0) == li)
            def _(p_ref=p_refs[li]):
                g = p_ref[0].astype(F32)
                for dev in range(1, N_DEV):
                    g = g + p_ref[dev].astype(F32)
                delta, mm, vv = _adamw_math(w_ref[...], g, m_ref[...], v_ref[...])
                g_out[...] = g
                d_out[...] = delta
                m_out[...] = mm
                v_out[...] = vv

    def part_spec(li):
        return pl.BlockSpec(part_block, lambda l, i: (0, jnp.where(l == li, i, jnp.where(l < li, 0, nt - 1)), 0))

    return pl.pallas_call(
        body, out_shape=[SDS(w.shape, F32)] * 4, grid=(depth, nt),
        in_specs=[part_spec(li) for li in range(depth)] + [blk, blk, blk],
        out_specs=[blk] * 4, name=name, compiler_params=_cp())(*parts, w, m, v)


def _adamw_small(gs, ws, ms, vs, name):
    n = len(gs)

    def body(*refs):
        g_refs, w_refs, m_refs, v_refs = (refs[k * n:(k + 1) * n] for k in range(4))
        outs = refs[4 * n:]
        for k in range(n):
            delta, mm, vv = _adamw_math(w_refs[k][...], g_refs[k][...], m_refs[k][...], v_refs[k][...])
            outs[k][...] = delta
            outs[n + k][...] = mm
            outs[2 * n + k][...] = vv

    res = pl.pallas_call(body, out_shape=[SDS(w.shape, F32) for w in ws] * 3, name=name,
                         compiler_params=_cp())(*gs, *ws, *ms, *vs)
    return res[:n], res[n:2 * n], res[2 * n:]


def _from_cols(t):
    return jnp.transpose(t, (1, 0, 2)).reshape(t.shape[1], N_DEV * t.shape[2])


def _swap(t):
    return jnp.swapaxes(t, -1, -2)


def kernel(x, meta_tokens, mix_norm_g, w_in, conv_dw_w, conv_dw_b, conv_ln_g, conv_ln_b, w_out, ffn_norm_g, w_gate, w_up, w_down, final_norm_g, loss_target, m_meta_tokens, m_mix_norm_g, m_w_in, m_conv_dw_w, m_conv_dw_b, m_conv_ln_g, m_conv_ln_b, m_w_out, m_ffn_norm_g, m_w_gate, m_w_up, m_w_down, m_final_norm_g, v_meta_tokens, v_mix_norm_g, v_w_in, v_conv_dw_w, v_conv_dw_b, v_conv_ln_g, v_conv_ln_b, v_w_out, v_ffn_norm_g, v_w_gate, v_w_up, v_w_down, v_final_norm_g):
    depth, d, in_shard = w_in.shape
    seq = x.shape[1]
    sb = N_HEADS * HEAD_DIM
    cc = conv_dw_w.shape[2] * N_DEV
    ff = w_gate.shape[2] * N_DEV
    assert in_shard * N_DEV == 3 * sb + 2 * cc and x.shape[0] == 1
    lr = N_META + seq
    lp = -(-lr // ABLK) * ABLK
    me = 4 * lax.axis_index("x") + 2 * lax.axis_index("y") + lax.axis_index("c")

    big_names = ("w_in", "w_out", "w_gate", "w_up", "w_down")
    transposed = {"w_in": True, "w_out": False, "w_gate": True, "w_up": True, "w_down": False}
    shard = dict(w_in=_swap(w_in).astype(BF16), w_out=w_out.astype(BF16), w_gate=_swap(w_gate).astype(BF16),
                 w_up=_swap(w_up).astype(BF16), w_down=w_down.astype(BF16))

    def gather_of(keys):
        names = sorted({n for n, _ in keys}, key=big_names.index)
        jobs = [_Job(names.index(n), j, False, src_layer=i) for j, (n, i) in enumerate(keys)]
        return jobs, [shard[n] for n in names], [SDS((N_DEV,) + shard[n].shape[1:], BF16) for n, _ in keys]

    first_keys = [("w_in", 0)]

    def riding(keys):
        return _Rider(*gather_of(keys)) if keys else None

    def receive(keys, arrays):
        for (n, li), t in zip(keys, arrays):
            wl[li][n] = t.reshape(-1, d)

    jobs, srcs, out_shape = gather_of(first_keys)
    for extra in (meta_tokens, conv_dw_w):
        jobs.append(_Job(len(srcs), len(out_shape), False))
        srcs.append(extra)
        out_shape.append(SDS((N_DEV,) + extra.shape, F32))
    gathered = _exchange(jobs, srcs, out_shape, "gather_first")
    wl = [dict() for _ in range(depth)]
    receive(first_keys, gathered)
    meta_full = _from_cols(gathered[-2])
    taps = jnp.transpose(gathered[-1], (1, 2, 0, 3)).reshape(depth, CONV_K, cc)
    taps = jnp.pad(taps, ((0, 0), (0, CONV_PAD - CONV_K), (0, 0)))
    tri_fwd, tri_after, tri_before = _tri_consts()

    h = jnp.concatenate([meta_full, x[0], jnp.zeros((lp - lr, d), F32)], axis=0)
    saved = []
    for i in range(depth):
        p = wl[i]
        sv = dict(h_in=h)
        qkv, hn = _rms_mm(h, mix_norm_g[i:i + 1], p["w_in"], 0, 3 * sb, BF16, f"proj_qkv_{i}", True)
        keys = [("w_out", 0)] if i == 0 else []
        cacg, *arrived = _rms_mm(h, mix_norm_g[i:i + 1], p["w_in"], 3 * sb, 2 * cc, F32, f"proj_conv_{i}", False,
                                 riding(keys))
        receive(keys, arrived)
        keys = [("w_gate", i), ("w_up", i)] + ([("w_down", 0)] if i == 0 else [])
        attn, rsave, *arrived = _attn_fwd(qkv, tri_fwd, f"attn_fwd_{i}", riding(keys))
        receive(keys, arrived)
        yc, = _conv_fwd_dw(cacg, taps[i], conv_dw_b[i:i + 1], f"conv_fwd_dw_{i}")
        conv = _conv_fwd_ln(yc, conv_ln_g[i:i + 1], conv_ln_b[i:i + 1], f"conv_fwd_ln_{i}")
        h = _mix_out(attn, conv, p["w_out"], h, f"mix_out_{i}")
        sv.update(qkv=qkv, hn=hn, cacg=cacg, rsave=rsave, yc=yc, attn=attn, conv=conv, h_mid=h)
        nxt = i + 1
        keys = [("w_in", nxt), ("w_out", nxt), ("w_down", nxt)] if nxt < depth else []
        h, hn2, act, gate, up, *arrived = _ffn_fwd(h, ffn_norm_g[i:i + 1], p["w_gate"], p["w_up"], p["w_down"],
                                                   f"ffn_fwd_{i}", riding(keys))
        receive(keys, arrived)
        sv.update(hn2=hn2, act=act, gate=gate, up=up)
        saved.append(sv)

    tpad = jnp.pad(loss_target[0], ((N_META, lp - lr), (0, 0)))
    dh, dg_final, loss_part = _loss_head(h, tpad, final_norm_g.reshape(1, d), seq, "loss_head")
    loss = lax.psum(loss_part[0, 0], MESH_AXES)

    parts = {}

    def sending(items):
        srcs = [t.reshape(N_DEV, t.shape[0] // N_DEV, d) for _, t in items]
        return _Rider([_Job(j, j, True) for j in range(len(items))], srcs, [SDS(t.shape, BF16) for t in srcs])

    def arrive(items, arrays):
        parts.update({key: t for (key, _), t in zip(items, arrays)})

    grads = [None] * depth
    from_above = []
    for i in reversed(range(depth)):
        p, sv = wl[i], saved[i]
        d_g, d_u, *got = _ffn_bwd_act(dh, sv["gate"], sv["up"], p["w_down"], f"ffn_bwd_act_{i}", sending(from_above))
        arrive(from_above, got)
        gw_down = _mm_tn(sv["act"], dh, f"grad_w_down_{i}")
        gw_gate = _mm_tn(d_g, sv["hn2"], f"grad_w_gate_{i}")
        gw_up = _mm_tn(d_u, sv["hn2"], f"grad_w_up_{i}")
        items = [(("w_down", i), gw_down)]
        dh, dg_ffn, *got = _ffn_bwd_in(d_g, d_u, p["w_gate"], p["w_up"], sv["h_mid"], ffn_norm_g[i:i + 1], dh,
                                       f"ffn_bwd_in_{i}", sending(items))
        arrive(items, got)
        gw_out = _grad_w_out(sv["attn"], sv["conv"], dh, f"grad_w_out_{i}")
        d_attn, d_conv = _mix_bwd(dh, p["w_out"], sb, cc, f"mix_bwd_{i}")
        items = [(("w_gate", i), gw_gate), (("w_up", i), gw_up)]
        dqkv, *got = _attn_bwd(sv["qkv"], d_attn, sv["rsave"], tri_after, tri_before, f"attn_bwd_{i}", sending(items))
        arrive(items, got)
        dyc, dg_ln, db_ln = _conv_bwd_ln(sv["yc"], d_conv, conv_ln_g[i:i + 1], conv_ln_b[i:i + 1], f"conv_bwd_ln_{i}")
        dcc, g_taps_i, db_conv = _conv_bwd_dw(dyc, sv["cacg"], taps[i], f"conv_bwd_dw_{i}")
        items = [(("w_out", i), gw_out)]
        gw_in, *got = _grad_w_in_t(dqkv, dcc, sv["hn"], f"grad_w_in_{i}", sending(items))
        arrive(items, got)
        from_above = [(("w_in", i), gw_in)]
        items = from_above if i == 0 else []
        dh, dg_mix, *got = _mix_bwd_in(dqkv, dcc, p["w_in"], sv["h_in"], mix_norm_g[i:i + 1], dh, f"mix_bwd_in_{i}",
                                       sending(items))
        arrive(items, got)
        grads[i] = dict(taps=g_taps_i, dg_mix=dg_mix, dg_ffn=dg_ffn, dg_ln=dg_ln, db_ln=db_ln, db_conv=db_conv)
    grad_x = dh[N_META:lr][None]

    per = d // depth
    assert depth * cc <= d and d % depth == 0 and per % 128 == 0
    placed, cuts, row = [], [], 0
    for key in ("dg_mix", "dg_ffn"):
        placed += [(grads[i][key], row + i, 0, True) for i in range(depth)]
        cuts.append(((depth, d), [(slice(0, depth), row, depth, 0, d)]))
        row += depth
    placed.append((dg_final, row, 0, True))
    cuts.append(((1, d), [(slice(0, 1), row, 1, 0, d)]))
    row += 1
    for key in ("db_conv", "dg_ln", "db_ln"):
        placed += [(grads[i][key], row, i * per, True) for i in range(depth)]
        cuts.append(((depth, cc), [(slice(i, i + 1), row, 1, i * per, cc) for i in range(depth)]))
        row += 1
    placed += [(grads[i]["taps"], row, i * per, False) for i in range(depth)]
    cuts.append(((depth, CONV_PAD, cc), [(i, row, CONV_PAD, i * per, cc) for i in range(depth)]))
    row += CONV_PAD
    placed.append((dh[:N_META], row, 0, False))
    cuts.append(((N_META, d), [(slice(0, N_META), row, N_META, 0, d)]))
    row += N_META
    g_mix, g_ffn, g_final, g_cb, g_lg, g_lb, g_taps_full, g_meta_full = _all_reduce_small(
        placed, cuts, -(-row // 8) * 8, d, "reduce_small")
    csh = cc // N_DEV
    g_taps_own = lax.dynamic_slice_in_dim(g_taps_full[:, :CONV_K], me * csh, csh, axis=2)
    msh = d // N_DEV
    g_meta_own = lax.dynamic_slice_in_dim(g_meta_full, me * msh, msh, axis=1)

    row1 = lambda t: t.reshape(1, d)
    small_g = [g_meta_own, g_mix, g_taps_own, g_cb, g_lg, g_lb, g_ffn, g_final]
    small_w = [meta_tokens, mix_norm_g, conv_dw_w, conv_dw_b, conv_ln_g, conv_ln_b, ffn_norm_g, row1(final_norm_g)]
    small_m = [m_meta_tokens, m_mix_norm_g, m_conv_dw_w, m_conv_dw_b, m_conv_ln_g, m_conv_ln_b, m_ffn_norm_g,
               row1(m_final_norm_g)]
    small_v = [v_meta_tokens, v_mix_norm_g, v_conv_dw_w, v_conv_dw_b, v_conv_ln_g, v_conv_ln_b, v_ffn_norm_g,
               row1(v_final_norm_g)]
    s_delta, s_m, s_v = _adamw_small(small_g, small_w, small_m, small_v, "adamw_small")
    unrow = lambda ts: list(ts[:-1]) + [ts[-1].reshape(d)]
    small_g, s_delta, s_m, s_v = unrow(small_g), unrow(s_delta), unrow(s_m), unrow(s_v)

    big = []
    for n, w, m, v in zip(big_names, (w_in, w_out, w_gate, w_up, w_down), (m_w_in, m_w_out, m_w_gate, m_w_up, m_w_down),
                          (v_w_in, v_w_out, v_w_gate, v_w_up, v_w_down)):
        fix = _swap if transposed[n] else (lambda t: t)
        res = _adamw_shard([parts[(n, i)] for i in range(depth)], fix(w), fix(m), fix(v), f"adamw_{n}")
        big.append([fix(t) for t in res])
    b_in, b_out, b_gate, b_up, b_down = big

    def ordered(k, smalls):
        s_meta, s_mix, s_taps, s_cb, s_lg, s_lb, s_ffn, s_final = smalls
        return [s_meta, s_mix, b_in[k], s_taps, s_cb, s_lg, s_lb, b_out[k], s_ffn, b_gate[k], b_up[k], b_down[k], s_final]

    return (loss, grad_x, *ordered(0, small_g), *ordered(1, s_delta), *ordered(2, s_m), *ordered(3, s_v))
```

```python
import math

import jax
import jax.numpy as jnp
from jax import lax
from jax.experimental import pallas as pl
from jax.experimental.pallas import tpu as pltpu

F32 = jnp.float32
BF16 = jnp.bfloat16
SDS = jax.ShapeDtypeStruct

N_META = 16
N_HEADS = 8
HEAD_DIM = 64
CONV_K = 31
CONV_PAD = 32
ABLK = 128
ATT_GROUP = 4
ATT_SPLIT = 2
ATT_TOP = 2
ATT_FLIGHT = 3
EXP_ZERO_AT = 104.0
GONE = 1e30
MASKED = -1e30
FF_CHUNK = 256
FF_BLOCK = 1408
FF_ROWS = 528
EPS = 1e-6
N_DEV = 8
ADAM_LR = 0.001
ADAM_B1 = 0.9
ADAM_B2 = 0.999
ADAM_EPS = 1e-08
ADAM_WD = 0.01
ADAM_STEP = 10
MIB = 1 << 20
VMEM_LIMIT_MIB = 48
FFN_FWD_VMEM_MIB = 56
MM_ROWS = 1056
EW_ROWS = 528


def _cp(mib=None):
    return pltpu.CompilerParams(vmem_limit_bytes=(mib or VMEM_LIMIT_MIB) * MIB)


def _tile(n, cap, mult):
    best = None
    for t in range(mult, min(n, cap) + 1, mult):
        if n % t == 0:
            best = t
    assert best is not None, (n, cap, mult)
    return best


def _sum8(v):
    r, c = v.shape
    return v.reshape(r // 8, 8, c).sum(axis=0)


def _sigmoid(x):
    return 1.0 / (1.0 + jnp.exp(-x))


def _ff_block(f):
    fb = _tile(f, FF_BLOCK, 128)
    return fb, [(s0, min(FF_CHUNK, fb - s0)) for s0 in range(0, fb, FF_CHUNK)]


NT = (((1,), (1,)), ((), ()))
TN = (((0,), (0,)), ((), ()))


def _mix_in(h, g, wt, n_qkv, name, rider=None):
    lp, d = h.shape
    n = wt.shape[0]
    tm = _tile(lp, EW_ROWS, 16)
    ni = lp // tm
    rider = rider or _Rider()

    def body(h_ref, g_ref, w_ref, *rest):
        (qkv_ref, cc_ref, hn_ref), _ = rider.split(rest, 3, 0)
        rider.start(pl.program_id(0) == 0)
        if ni > 1:
            rider.forward(pl.program_id(0) == ni - 1)
        x = h_ref[...]
        r = lax.rsqrt(jnp.mean(x * x, axis=-1, keepdims=True) + EPS)
        hn = ((x * r) * g_ref[...]).astype(BF16)
        hn_ref[...] = hn
        qkv_ref[...] = lax.dot_general(hn, w_ref[0:n_qkv, :], NT, preferred_element_type=F32).astype(BF16)
        cc_ref[...] = lax.dot_general(hn, w_ref[n_qkv:n, :], NT, preferred_element_type=F32)
        rider.wait(pl.program_id(0) == ni - 1)

    return pl.pallas_call(
        body, out_shape=[SDS((lp, n_qkv), BF16), SDS((lp, n - n_qkv), F32), SDS((lp, d), BF16)] + rider.out_shape,
        grid=(ni,),
        in_specs=[pl.BlockSpec((tm, d), lambda i: (i, 0)), pl.BlockSpec((1, d), lambda i: (0, 0)),
                  pl.BlockSpec((n, d), lambda i: (0, 0))] + rider.in_specs(),
        out_specs=[pl.BlockSpec((tm, n_qkv), lambda i: (i, 0)), pl.BlockSpec((tm, n - n_qkv), lambda i: (i, 0)),
                   pl.BlockSpec((tm, d), lambda i: (i, 0))] + rider.out_specs(),
        scratch_shapes=rider.scratch(), name=name, compiler_params=_cp())(h, g, wt, *rider.srcs)


def _mix_bwd(dh, w, sb, cc, name):
    m, k = dh.shape
    tm = _tile(m, MM_ROWS, 16)

    def body(a_ref, w_ref, attn_ref, conv_ref):
        a = a_ref[...].astype(BF16)
        attn_ref[...] = lax.dot_general(a, w_ref[0:sb, :], NT, preferred_element_type=F32).astype(BF16)
        conv_ref[...] = lax.dot_general(a, w_ref[sb:sb + cc, :], NT, preferred_element_type=F32)

    return pl.pallas_call(
        body, out_shape=[SDS((m, sb), BF16), SDS((m, cc), F32)], grid=(m // tm,),
        in_specs=[pl.BlockSpec((tm, k), lambda i: (i, 0)), pl.BlockSpec((sb + cc, k), lambda i: (0, 0))],
        out_specs=[pl.BlockSpec((tm, sb), lambda i: (i, 0)), pl.BlockSpec((tm, cc), lambda i: (i, 0))],
        name=name, compiler_params=_cp())(dh, w)


def _mix_out(attn, conv, w, res, name):
    m, ka = attn.shape
    kc = conv.shape[1]
    n = w.shape[1]
    assert ka == kc
    tm = _tile(m, MM_ROWS, 16)
    tn = _tile(n, 512, 128)

    def body(a_ref, c_ref, wa_ref, wc_ref, r_ref, o_ref):
        o_ref[...] = (r_ref[...] + jnp.dot(a_ref[...], wa_ref[...], preferred_element_type=F32)
                      + jnp.dot(c_ref[...], wc_ref[...], preferred_element_type=F32))

    return pl.pallas_call(
        body, out_shape=SDS((m, n), F32), grid=(m // tm, n // tn),
        in_specs=[pl.BlockSpec((tm, ka), lambda i, j: (i, 0)), pl.BlockSpec((tm, kc), lambda i, j: (i, 0)),
                  pl.BlockSpec((ka, tn), lambda i, j: (0, j)), pl.BlockSpec((kc, tn), lambda i, j: (1, j)),
                  pl.BlockSpec((tm, tn), lambda i, j: (i, j))],
        out_specs=pl.BlockSpec((tm, tn), lambda i, j: (i, j)), name=name, compiler_params=_cp())(attn, conv, w, w, res)


def _mm_tn(a, b, name):
    l, m = a.shape
    n = b.shape[1]
    tm = _tile(m, 1408, 128)
    tn = _tile(n, 1024, 128)
    tl = _tile(l, 1408, 128)
    nl = l // tl

    def body(a_ref, b_ref, o_ref, acc):
        @pl.when(pl.program_id(2) == 0)
        def _():
            acc[...] = jnp.zeros_like(acc)

        acc[...] += lax.dot_general(a_ref[...].astype(BF16), b_ref[...].astype(BF16), TN, preferred_element_type=F32)

        @pl.when(pl.program_id(2) == nl - 1)
        def _():
            o_ref[...] = acc[...].astype(BF16)

    return pl.pallas_call(
        body, out_shape=SDS((m, n), BF16), grid=(m // tm, n // tn, nl),
        in_specs=[pl.BlockSpec((tl, tm), lambda i, j, s: (s, i)),
                  pl.BlockSpec((tl, tn), lambda i, j, s: (s, j))],
        out_specs=pl.BlockSpec((tm, tn), lambda i, j, s: (i, j)),
        scratch_shapes=[pltpu.VMEM((tm, tn), F32)], name=name, compiler_params=_cp())(a, b)


def _grad_w_out(attn, conv, dh, name):
    l, w = attn.shape
    d = dh.shape[1]
    assert conv.shape == attn.shape
    tn = _tile(d, 1024, 128)
    tl = _tile(l, 1408, 128)
    nl = l // tl

    def body(a_ref, c_ref, b_ref, o_ref, acc):
        p, s = pl.program_id(0), pl.program_id(2)

        @pl.when(s == 0)
        def _():
            acc[...] = jnp.zeros_like(acc)

        @pl.when(p == 0)
        def _():
            acc[...] += lax.dot_general(a_ref[...], b_ref[...].astype(BF16), TN, preferred_element_type=F32)

        @pl.when(p == 1)
        def _():
            acc[...] += lax.dot_general(c_ref[...], b_ref[...].astype(BF16), TN, preferred_element_type=F32)

        @pl.when(s == nl - 1)
        def _():
            o_ref[...] = acc[...].astype(BF16)

    out = pl.pallas_call(
        body, out_shape=SDS((2, w, d), BF16), grid=(2, d // tn, nl),
        in_specs=[pl.BlockSpec((tl, w), lambda p, j, s: (jnp.where(p == 0, s, nl - 1), 0)),
                  pl.BlockSpec((tl, w), lambda p, j, s: (jnp.where(p == 1, s, 0), 0)),
                  pl.BlockSpec((tl, tn), lambda p, j, s: (s, j))],
        out_specs=pl.BlockSpec((None, w, tn), lambda p, j, s: (p, 0, j)),
        scratch_shapes=[pltpu.VMEM((w, tn), F32)], name=name, compiler_params=_cp())(attn, conv, dh)
    return out.reshape(2 * w, d)


def _grad_w_in_t(dqkv, dcc, hn, name, rider=None):
    nq, l, w = dqkv.shape
    nc = dcc.shape[0]
    d = hn.shape[1]
    assert dcc.shape[2] == w
    tn = _tile(d, 1024, 128)
    tl = _tile(l, 1408, 128)
    nl = l // tl
    nj = d // tn
    rider = rider or _Rider()

    def body(q_ref, c_ref, b_ref, *rest):
        (o_ref,), (acc,) = rider.split(rest, 1, 1)
        p, jj, s = pl.program_id(0), pl.program_id(1), pl.program_id(2)
        rider.start((p == 0) & (jj == 0) & (s == 0))

        @pl.when(s == 0)
        def _():
            acc[...] = jnp.zeros_like(acc)

        @pl.when(p < nq)
        def _():
            acc[...] += lax.dot_general(q_ref[...], b_ref[...], TN, preferred_element_type=F32)

        @pl.when(p >= nq)
        def _():
            acc[...] += lax.dot_general(c_ref[...], b_ref[...], TN, preferred_element_type=F32)

        @pl.when(s == nl - 1)
        def _():
            o_ref[...] = acc[...].astype(BF16)

        rider.wait((p == nq + nc - 1) & (jj == nj - 1) & (s == nl - 1))

    out, *arrived = pl.pallas_call(
        body, out_shape=[SDS((nq + nc, w, d), BF16)] + rider.out_shape, grid=(nq + nc, nj, nl),
        in_specs=[pl.BlockSpec((None, tl, w), lambda p, j, s: (jnp.minimum(p, nq - 1), s, 0)),
                  pl.BlockSpec((None, tl, w), lambda p, j, s: (jnp.maximum(p - nq, 0), s, 0)),
                  pl.BlockSpec((tl, tn), lambda p, j, s: (s, j))] + rider.in_specs(),
        out_specs=[pl.BlockSpec((None, w, tn), lambda p, j, s: (p, 0, j))] + rider.out_specs(),
        scratch_shapes=[pltpu.VMEM((w, tn), F32)] + rider.scratch(), name=name,
        compiler_params=_cp())(dqkv, dcc, hn, *rider.srcs)
    return [out.reshape((nq + nc) * w, d)] + arrived


def _rms_bwd_tail(acc, h_ref, g_ref, dres_ref, o_ref, dg_ref):
    x = h_ref[...]
    r = lax.rsqrt(jnp.mean(x * x, axis=-1, keepdims=True) + EPS)
    xr = x * r
    dyv = acc[...]
    gy = dyv * g_ref[...]
    o_ref[...] = dres_ref[...] + r * (gy - xr * jnp.mean(gy * xr, axis=-1, keepdims=True))
    dg_ref[...] += _sum8(dyv * xr)


def _mix_bwd_in(dqkv, dcc, wt, h, g, dres, name, rider=None):
    nq, lp, w = dqkv.shape
    nc = dcc.shape[0]
    d = h.shape[1]
    tm = _tile(lp, FF_ROWS, 16)
    ni = lp // tm
    rider = rider or _Rider()

    def body(q_ref, c_ref, w_ref, h_ref, g_ref, dres_ref, *rest):
        (o_ref, dg_ref), (acc,) = rider.split(rest, 2, 1)
        i = pl.program_id(0)
        rider.start(i == 0)

        @pl.when(i == 0)
        def _():
            dg_ref[...] = jnp.zeros_like(dg_ref)

        pieces = [q_ref[p] for p in range(nq)] + [c_ref[p] for p in range(nc)]
        total = None
        for p, piece in enumerate(pieces):
            part = jnp.dot(piece, w_ref[p * w:(p + 1) * w, :], preferred_element_type=F32)
            total = part if total is None else total + part
        acc[...] = total
        _rms_bwd_tail(acc, h_ref, g_ref, dres_ref, o_ref, dg_ref)
        rider.wait(i == ni - 1)

    return pl.pallas_call(
        body, out_shape=[SDS((lp, d), F32), SDS((8, d), F32)] + rider.out_shape, grid=(ni,),
        in_specs=[pl.BlockSpec((nq, tm, w), lambda i: (0, i, 0)),
                  pl.BlockSpec((nc, tm, w), lambda i: (0, i, 0)),
                  pl.BlockSpec(((nq + nc) * w, d), lambda i: (0, 0)),
                  pl.BlockSpec((tm, d), lambda i: (i, 0)),
                  pl.BlockSpec((1, d), lambda i: (0, 0)),
                  pl.BlockSpec((tm, d), lambda i: (i, 0))] + rider.in_specs(),
        out_specs=[pl.BlockSpec((tm, d), lambda i: (i, 0)), pl.BlockSpec((8, d), lambda i: (0, 0))]
        + rider.out_specs(),
        scratch_shapes=[pltpu.VMEM((tm, d), F32)] + rider.scratch(), name=name,
        compiler_params=_cp())(dqkv, dcc, wt, h, g, dres, *rider.srcs)


def _ffn_bwd_in(d_g, d_u, wg_t, wu_t, h, g, dres, name, rider=None):
    lp, f = d_g.shape
    d = h.shape[1]
    tm = _tile(lp, FF_ROWS, 16)
    tk = _tile(f, FF_BLOCK, 128)
    nk = f // tk
    ni = lp // tm
    rider = rider or _Rider()

    def body(dg_in, du_in, wg_ref, wu_ref, h_ref, g_ref, dres_ref, *rest):
        (o_ref, dg_ref), (acc,) = rider.split(rest, 2, 1)
        i, kk = pl.program_id(0), pl.program_id(1)
        rider.start((i == 0) & (kk == 0))

        @pl.when(kk == 0)
        def _():
            acc[...] = jnp.zeros_like(acc)

        @pl.when((kk == 0) & (i == 0))
        def _():
            dg_ref[...] = jnp.zeros_like(dg_ref)

        acc[...] += (jnp.dot(dg_in[...], wg_ref[...], preferred_element_type=F32)
                     + jnp.dot(du_in[...], wu_ref[...], preferred_element_type=F32))

        @pl.when(kk == nk - 1)
        def _():
            _rms_bwd_tail(acc, h_ref, g_ref, dres_ref, o_ref, dg_ref)

        rider.wait((i == ni - 1) & (kk == nk - 1))

    return pl.pallas_call(
        body, out_shape=[SDS((lp, d), F32), SDS((8, d), F32)] + rider.out_shape, grid=(ni, nk),
        in_specs=[pl.BlockSpec((tm, tk), lambda i, j: (i, j)), pl.BlockSpec((tm, tk), lambda i, j: (i, j)),
                  pl.BlockSpec((tk, d), lambda i, j: (j, 0)), pl.BlockSpec((tk, d), lambda i, j: (j, 0)),
                  pl.BlockSpec((tm, d), lambda i, j: (i, 0)),
                  pl.BlockSpec((1, d), lambda i, j: (0, 0)),
                  pl.BlockSpec((tm, d), lambda i, j: (i, 0))] + rider.in_specs(),
        out_specs=[pl.BlockSpec((tm, d), lambda i, j: (i, 0)), pl.BlockSpec((8, d), lambda i, j: (0, 0))]
        + rider.out_specs(),
        scratch_shapes=[pltpu.VMEM((tm, d), F32)] + rider.scratch(), name=name,
        compiler_params=_cp())(d_g, d_u, wg_t, wu_t, h, g, dres, *rider.srcs)


def _ffn_fwd(h, g, wg_t, wu_t, w_d, name, rider=None):
    lp, d = h.shape
    f = w_d.shape[0]
    tm = _tile(lp, FF_ROWS, 16)
    fb, subs = _ff_block(f)
    nc = f // fb
    ni = lp // tm
    rider = rider or _Rider()

    def body(h_ref, g_ref, wg_ref, wu_ref, wd_ref, *rest):
        (o_ref, hn_ref, act_ref, gate_ref, up_ref), (hn_s, acc) = rider.split(rest, 5, 2)
        c = pl.program_id(1)
        rider.start((pl.program_id(0) == 0) & (c == 0))
        if ni > 1:
            rider.forward((pl.program_id(0) == ni - 1) & (c == 0))

        @pl.when(c == 0)
        def _():
            x = h_ref[...]
            r = lax.rsqrt(jnp.mean(x * x, axis=-1, keepdims=True) + EPS)
            hn = ((x * r) * g_ref[...]).astype(BF16)
            hn_s[...] = hn
            hn_ref[...] = hn
            acc[...] = jnp.zeros_like(acc)

        total = None
        for s0, sw in subs:
            gg = lax.dot_general(hn_s[...], wg_ref[s0:s0 + sw, :], NT, preferred_element_type=F32)
            uu = lax.dot_general(hn_s[...], wu_ref[s0:s0 + sw, :], NT, preferred_element_type=F32)
            gate_ref[:, s0:s0 + sw] = gg
            up_ref[:, s0:s0 + sw] = uu
            act = ((gg * _sigmoid(gg)) * uu).astype(BF16)
            act_ref[:, s0:s0 + sw] = act
            part = jnp.dot(act, wd_ref[s0:s0 + sw, :], preferred_element_type=F32)
            total = part if total is None else total + part
        acc[...] += total

        @pl.when(c == nc - 1)
        def _():
            o_ref[...] = h_ref[...] + acc[...]

        rider.wait((pl.program_id(0) == ni - 1) & (c == nc - 1))

    chunk = pl.BlockSpec((fb, d), lambda i, j: (j, 0))
    wide = pl.BlockSpec((tm, fb), lambda i, j: (i, j))
    return pl.pallas_call(
        body, out_shape=[SDS((lp, d), F32), SDS((lp, d), BF16), SDS((lp, f), BF16), SDS((lp, f), F32),
                         SDS((lp, f), F32)] + rider.out_shape, grid=(ni, nc),
        in_specs=[pl.BlockSpec((tm, d), lambda i, j: (i, 0)), pl.BlockSpec((1, d), lambda i, j: (0, 0)),
                  chunk, chunk, chunk] + rider.in_specs(),
        out_specs=[pl.BlockSpec((tm, d), lambda i, j: (i, 0)), pl.BlockSpec((tm, d), lambda i, j: (i, 0)),
                   wide, wide, wide] + rider.out_specs(),
        scratch_shapes=[pltpu.VMEM((tm, d), BF16), pltpu.VMEM((tm, d), F32)] + rider.scratch(),
        name=name, compiler_params=_cp(FFN_FWD_VMEM_MIB))(h, g, wg_t, wu_t, w_d, *rider.srcs)


def _ffn_bwd_act(dh, gate, up, w_d, name, rider=None):
    lp, d = dh.shape
    f = w_d.shape[0]
    tm = _tile(lp, FF_ROWS, 16)
    fb, subs = _ff_block(f)
    nc = f // fb
    ni = lp // tm
    rider = rider or _Rider()

    def body(dh_ref, gate_ref, up_ref, wd_ref, *rest):
        (dg_out, du_out), (dh_s,) = rider.split(rest, 2, 1)
        rider.start((pl.program_id(0) == 0) & (pl.program_id(1) == 0))

        @pl.when(pl.program_id(1) == 0)
        def _():
            dh_s[...] = dh_ref[...].astype(BF16)

        for s0, sw in subs:
            gg = gate_ref[:, s0:s0 + sw]
            uu = up_ref[:, s0:s0 + sw]
            dact = lax.dot_general(dh_s[...], wd_ref[s0:s0 + sw, :], NT, preferred_element_type=F32)
            s = _sigmoid(gg)
            dg_out[:, s0:s0 + sw] = (dact * uu * (s * (1.0 + gg * (1.0 - s)))).astype(BF16)
            du_out[:, s0:s0 + sw] = (dact * (gg * s)).astype(BF16)
        rider.wait((pl.program_id(0) == ni - 1) & (pl.program_id(1) == nc - 1))

    wide = pl.BlockSpec((tm, fb), lambda i, j: (i, j))
    return pl.pallas_call(
        body, out_shape=[SDS((lp, f), BF16), SDS((lp, f), BF16)] + rider.out_shape, grid=(ni, nc),
        in_specs=[pl.BlockSpec((tm, d), lambda i, j: (i, 0)), wide, wide,
                  pl.BlockSpec((fb, d), lambda i, j: (j, 0))] + rider.in_specs(),
        out_specs=[wide, wide] + rider.out_specs(), scratch_shapes=[pltpu.VMEM((tm, d), BF16)] + rider.scratch(),
        name=name, compiler_params=_cp())(dh, gate, up, w_d, *rider.srcs)


def _loss_head(h, tpad, g, n_real, name):
    lp, d = h.shape
    tm = _tile(lp, EW_ROWS, 16)

    def body(h_ref, t_ref, g_ref, dh_ref, dg_ref, loss_ref):
        i = pl.program_id(0)

        @pl.when(i == 0)
        def _():
            dg_ref[...] = jnp.zeros_like(dg_ref)
            loss_ref[...] = jnp.zeros_like(loss_ref)

        x = h_ref[...]
        r = lax.rsqrt(jnp.mean(x * x, axis=-1, keepdims=True) + EPS)
        xr = x * r
        y = xr * g_ref[...]
        row = i * tm + lax.broadcasted_iota(jnp.int32, (tm, d), 0)
        valid = (row >= N_META) & (row < N_META + n_real)
        diff = jnp.where(valid, y - t_ref[...], 0.0)
        loss_ref[...] += jnp.sum(diff * diff) * (0.5 / d)
        dy = diff * (1.0 / d)
        gy = dy * g_ref[...]
        dh_ref[...] = r * (gy - xr * jnp.mean(gy * xr, axis=-1, keepdims=True))
        dg_ref[...] += _sum8(dy * xr)

    return pl.pallas_call(
        body, out_shape=[SDS((lp, d), F32), SDS((8, d), F32), SDS((8, 128), F32)], grid=(lp // tm,),
        in_specs=[pl.BlockSpec((tm, d), lambda i: (i, 0)),
                  pl.BlockSpec((tm, d), lambda i: (i, 0)),
                  pl.BlockSpec((1, d), lambda i: (0, 0))],
        out_specs=[pl.BlockSpec((tm, d), lambda i: (i, 0)),
                   pl.BlockSpec((8, d), lambda i: (0, 0)),
                   pl.BlockSpec((8, 128), lambda i: (0, 0))],
        name=name, compiler_params=_cp())(h, tpad, g)


def _tri_consts():
    j = lax.broadcasted_iota(jnp.int32, (ABLK, ABLK), 0)
    s = lax.broadcasted_iota(jnp.int32, (ABLK, ABLK), 1)
    after = (j >= s).astype(BF16)
    before = (j < s).astype(BF16)
    ones = jnp.ones((ABLK, ABLK), BF16)
    two = lambda t: jnp.concatenate([t, t], axis=0)
    return (two(jnp.concatenate([after, ones], axis=1)),
            two(after),
            two(jnp.concatenate([before, ones], axis=1)))


def _softplus(z):
    neg_abs = lax.bitcast_convert_type(lax.bitcast_convert_type(z, jnp.uint32) | jnp.uint32(0x80000000), F32)
    return jnp.log(1.0 + jnp.exp(neg_abs)) + jnp.maximum(z, 0.0)


def _split_hi_lo(m):
    hi = m.astype(BF16)
    lo = (m - hi.astype(F32)).astype(BF16)
    return jnp.concatenate([hi, lo], axis=1)


def _head_halves(t2, in_a):
    zero = jnp.zeros_like(t2)
    return jnp.concatenate([jnp.where(in_a, t2, zero), jnp.where(in_a, zero, t2)], axis=0)


def _stack_blocks(t, nblk, in_a):
    return jnp.concatenate([_head_halves(t[u * ABLK:(u + 1) * ABLK], in_a) for u in range(nblk)], axis=0)


def _interleave(*gens):
    alive = list(gens)
    while alive:
        for g in list(alive):
            if next(g, alive) is alive:
                alive.remove(g)


def _attn_scale():
    scale = 1.0 / math.sqrt(HEAD_DIM)
    assert math.frexp(scale)[0] == 0.5, "a power of two, so that scaling q in bf16 is exact"
    return scale


def _pow2_below(n):
    assert n & (n - 1) == 0
    return [p for p in (64, 32, 16, 8, 4, 2, 1) if p < n]


class _Rider:
    def __init__(self, jobs=(), srcs=(), out_shape=()):
        self.jobs, self.srcs, self.out_shape = list(jobs), list(srcs), list(out_shape)
        self.any = [pl.BlockSpec(memory_space=pl.ANY)]

    def split(self, rest, n_out, n_scratch):
        ni, no = len(self.srcs), len(self.out_shape)
        self.ins, outs = rest[:ni], rest[ni:ni + n_out]
        self.outs = rest[ni + n_out:ni + n_out + no]
        scratch = rest[ni + n_out + no:ni + n_out + no + n_scratch]
        self.sems = rest[ni + n_out + no + n_scratch:]
        return outs, scratch

    def start(self, first):
        self.forwarded = False
        if self.jobs:
            @pl.when(first)
            def _():
                _exchange_start(self.jobs, self.ins, self.outs, *self.sems)

    def forward(self, late):
        self.forwarded = True
        if self.jobs:
            @pl.when(late)
            def _():
                _exchange_forward(self.jobs, self.ins, self.outs, *self.sems)

    def wait(self, last):
        if self.jobs:
            @pl.when(last)
            def _():
                if not self.forwarded:
                    _exchange_forward(self.jobs, self.ins, self.outs, *self.sems)
                _exchange_wait(self.jobs, self.ins, self.outs, *self.sems)

    def in_specs(self):
        return self.any * len(self.srcs)

    def out_specs(self):
        return self.any * len(self.out_shape)

    def scratch(self):
        return _exchange_sems(len(self.jobs)) if self.jobs else []


def _attn_fwd(qkv, tri_fwd, name, rider=None):
    lp = qkv.shape[0]
    n_pairs = (N_HEADS * HEAD_DIM) // 128
    nb = lp // ABLK
    assert nb <= 128 and 2 * HEAD_DIM == 128
    scale = _attn_scale()
    nt = (((1,), (1,)), ((), ()))
    rider = rider or _Rider()

    def body(q_ref, k_ref, v_ref, tri_ref, *rest):
        (o_ref, rs_ref), (r_s, acc_s, rs_s) = rider.split(rest, 2, 3)
        rider.start(pl.program_id(0) == 0)
        lane = lax.broadcasted_iota(jnp.int32, (ABLK, 128), 1)
        row = lax.broadcasted_iota(jnp.int32, (ABLK, 128), 0)
        in_a = lane < HEAD_DIM
        causal = lane < row

        def begin(st, i):
            r_s[st] = jnp.zeros(r_s.shape[1:], F32)
            acc_s[st] = jnp.zeros(acc_s.shape[1:], F32)
            rs_s[st] = jnp.full(rs_s.shape[1:], GONE, F32)
            return q_ref[pl.ds(pl.multiple_of(i * ABLK, ABLK), ABLK), :] * scale

        def live(st):
            least = jnp.min(jnp.minimum(r_s[st, 0], r_s[st, 1]), axis=0, keepdims=True)
            return (least[0, 0] < EXP_ZERO_AT).astype(jnp.int32)

        def step(*args):
            _interleave(stages(*args))

        def stages(st, q2, kb0, nblk, diag):
            k0 = pl.multiple_of(kb0 * ABLK, ABLK)
            kbd = _stack_blocks(k_ref[pl.ds(k0, nblk * ABLK), :], nblk, in_a)
            vbd = _stack_blocks(v_ref[pl.ds(k0, nblk * ABLK), :], nblk, in_a)
            z = lax.dot_general(q2, kbd, nt, preferred_element_type=F32)
            ncol = 2 * nblk
            zt = [z[:, c * 128:(c + 1) * 128] for c in range(ncol)]
            if diag:
                zt = [jnp.where(causal, t, MASKED) if c >= ncol - 2 else t for c, t in enumerate(zt)]
            yield
            bounds = [ncol * j // ATT_SPLIT for j in range(ATT_SPLIT + 1)]
            ce = [None] * ncol
            for c0, c1 in reversed([b for b in zip(bounds[:-1], bounds[1:]) if b[0] < b[1]]):
                parts = [_split_hi_lo(_softplus(zt[c])) for c in range(c0, c1)]
                got = jnp.dot(jnp.concatenate(parts, axis=0), tri_ref[...], preferred_element_type=F32)
                for c in range(c0, c1):
                    ce[c] = got[(c - c0) * 128:(c - c0 + 1) * 128]
                yield
            rr = [r_s[st, 0], r_s[st, 1]]
            rsv = [rs_s[st, :, :128], rs_s[st, :, 128:]]
            ws = [None] * ncol
            for u in reversed(range(nblk)):
                for hh in range(2):
                    c = 2 * u + hh
                    ws[c] = jnp.exp(zt[c] - ce[c][:, :128] - rr[hh]).astype(BF16)
                    rsv[hh] = jnp.where(lane == kb0 + u, rr[hh], rsv[hh])
                    rr[hh] = rr[hh] + ce[c][:, 128:]
            acc_s[st] += jnp.dot(jnp.concatenate(ws, axis=1), vbd, preferred_element_type=F32)
            r_s[st, 0] = rr[0]
            r_s[st, 1] = rr[1]
            rs_s[st, :, :128] = rsv[0]
            rs_s[st, :, 128:] = rsv[1]

        def finish(st, q2, i, n_top):
            i_low = i - n_top
            n_grp = i_low // ATT_GROUP

            def more(c):
                return (c[0] < n_grp) & (c[1] > 0)

            def inner(c):
                step(st, q2, i_low - ATT_GROUP * (c[0] + 1), ATT_GROUP, False)
                return c[0] + 1, live(st)

            _, alive = lax.while_loop(more, inner, (jnp.int32(0), live(st)))
            rem = i_low - ATT_GROUP * n_grp
            for p in _pow2_below(ATT_GROUP):
                def last_steps(p=p):
                    step(st, q2, rem & (p - 1), p, False)
                    return live(st)

                alive = lax.cond(((rem & p) != 0) & (alive > 0), last_steps, lambda alive=alive: alive)

            q0 = pl.multiple_of(i * ABLK, ABLK)
            o_ref[pl.ds(q0, ABLK), :] = acc_s[st].astype(BF16)
            rs_ref[pl.ds(q0, ABLK), :] = rs_s[st]

        def single(i):
            q2 = begin(0, i)
            n_top = jnp.minimum(i, ATT_TOP)
            for t in range(ATT_TOP + 1):
                @pl.when(n_top == t)
                def _():
                    step(0, q2, i - t, t + 1, True)

            finish(0, q2, i, n_top)

        def together(i):
            qs = [begin(st, i + st) for st in range(ATT_FLIGHT)]
            _interleave(*[stages(st, qs[st], i + st - ATT_TOP, ATT_TOP + 1, True) for st in range(ATT_FLIGHT)])
            for st in range(ATT_FLIGHT):
                finish(st, qs[st], i + st, ATT_TOP)

        n_head = min(ATT_TOP, nb)
        n_group = (nb - n_head) // ATT_FLIGHT
        n_single = nb - ATT_FLIGHT * n_group

        def singles(k, carry):
            single(jnp.where(k < n_head, k, nb - n_single + k))
            return carry

        def groups(j, carry):
            together(n_head + ATT_FLIGHT * j)
            return carry

        lax.fori_loop(0, n_single, singles, 0)
        lax.fori_loop(0, n_group, groups, 0)
        rider.wait(pl.program_id(0) == n_pairs - 1)

    col = lambda o: (lambda p: (0, p + o))
    return pl.pallas_call(
        body, out_shape=[SDS((lp, n_pairs * 128), BF16), SDS((lp, n_pairs * 256), F32)] + rider.out_shape,
        grid=(n_pairs,),
        in_specs=[pl.BlockSpec((lp, 128), col(0)), pl.BlockSpec((lp, 128), col(n_pairs)),
                  pl.BlockSpec((lp, 128), col(2 * n_pairs)), pl.BlockSpec((256, 256), lambda p: (0, 0))]
        + rider.in_specs(),
        out_specs=[pl.BlockSpec((lp, 128), col(0)), pl.BlockSpec((lp, 256), col(0))] + rider.out_specs(),
        scratch_shapes=[pltpu.VMEM((ATT_FLIGHT, 2, ABLK, 128), F32), pltpu.VMEM((ATT_FLIGHT, ABLK, 128), F32),
                        pltpu.VMEM((ATT_FLIGHT, ABLK, 256), F32)] + rider.scratch(),
        name=name, compiler_params=_cp())(qkv, qkv, qkv, tri_fwd, *rider.srcs)


def _attn_bwd(qkv, d_out, rsave, tri_after, tri_before, name, rider=None):
    lp = qkv.shape[0]
    n_pairs = (N_HEADS * HEAD_DIM) // 128
    nb = lp // ABLK
    scale = _attn_scale()
    nt = (((1,), (1,)), ((), ()))
    tn = (((0,), (0,)), ((), ()))
    rider = rider or _Rider()

    def body(q_ref, k_ref, v_ref, do_ref, rs_ref, ta_ref, tb_ref, *rest):
        (o_ref,), (dk_s, dv_s, dq_s, pc_s) = rider.split(rest, 1, 4)
        rider.start(pl.program_id(0) == 0)
        lane = lax.broadcasted_iota(jnp.int32, (ABLK, 128), 1)
        row = lax.broadcasted_iota(jnp.int32, (ABLK, 128), 0)
        in_a = lane < HEAD_DIM
        causal = lane < row
        dk_s[...] = jnp.zeros_like(dk_s)
        dv_s[...] = jnp.zeros_like(dv_s)

        def begin(st, i):
            q0 = pl.multiple_of(i * ABLK, ABLK)
            q2 = q_ref[pl.ds(q0, ABLK), :] * scale
            do2 = do_ref[pl.ds(q0, ABLK), :]
            dq_s[st] = jnp.zeros(dq_s.shape[1:], F32)
            pc_s[st] = jnp.zeros(pc_s.shape[1:], F32)
            return dict(q0=q0, q2=q2, do2=do2, q_st=_head_halves(q2, in_a), do_st=_head_halves(do2, in_a))

        def step(*args):
            _interleave(stages(*args))

        def stages(st, blk, kb0, nblk, diag):
            q0, q2, do2 = blk["q0"], blk["q2"], blk["do2"]
            k0 = pl.multiple_of(kb0 * ABLK, ABLK)
            kbd = _stack_blocks(k_ref[pl.ds(k0, nblk * ABLK), :], nblk, in_a)
            vbd = _stack_blocks(v_ref[pl.ds(k0, nblk * ABLK), :], nblk, in_a)
            z = lax.dot_general(q2, kbd, nt, preferred_element_type=F32)
            dw = lax.dot_general(do2, vbd, nt, preferred_element_type=F32)
            ncol = 2 * nblk
            zt = [z[:, c * 128:(c + 1) * 128] for c in range(ncol)]
            if diag:
                zt = [jnp.where(causal, t, MASKED) if c >= ncol - 2 else t for c, t in enumerate(zt)]
            bounds = [ncol * j // ATT_SPLIT for j in range(ATT_SPLIT + 1)]
            batches = [range(c0, c1) for c0, c1 in zip(bounds[:-1], bounds[1:]) if c0 < c1]
            sps, ex, ws, dls, pe = [None] * ncol, [None] * ncol, [None] * ncol, [None] * ncol, [None] * ncol

            def mass(cols):
                for c in cols:
                    sps[c] = _softplus(zt[c])
                got = jnp.dot(jnp.concatenate([_split_hi_lo(sps[c]) for c in cols], axis=0), ta_ref[...],
                              preferred_element_type=F32)
                for j, c in enumerate(cols):
                    ex[c] = got[j * 128:(j + 1) * 128]

            def weights(cols):
                for c in cols:
                    u, hh = c // 2, c % 2
                    r_saved = jnp.sum(jnp.where(lane == kb0 + u, rs_ref[pl.ds(q0, ABLK), hh * 128:(hh + 1) * 128],
                                                0.0), axis=1, keepdims=True)
                    w = jnp.exp(zt[c] - ex[c] - r_saved)
                    ws[c] = w.astype(BF16)
                    dls[c] = dw[:, c * 128:(c + 1) * 128] * w
                got = jnp.dot(jnp.concatenate([_split_hi_lo(dls[c]) for c in cols], axis=0), tb_ref[...],
                              preferred_element_type=F32)
                for j, c in enumerate(cols):
                    pe[c] = got[j * 128:(j + 1) * 128]

            yield
            mass(batches[0])
            yield
            for j in range(len(batches)):
                if j + 1 < len(batches):
                    mass(batches[j + 1])
                    yield
                weights(batches[j])
                yield
            pc = [pc_s[st, 0], pc_s[st, 1]]
            dzs = []
            for c in range(ncol):
                hh = c % 2
                one_minus_beta = jnp.exp(-sps[c])
                dz = dls[c] * one_minus_beta - (pe[c][:, :128] + pc[hh]) * (1.0 - one_minus_beta)
                pc[hh] = pc[hh] + pe[c][:, 128:]
                dzs.append(dz.astype(BF16))
            pc_s[st, 0] = pc[0]
            pc_s[st, 1] = pc[1]
            dq_s[st] += jnp.dot(jnp.concatenate(dzs, axis=1), kbd, preferred_element_type=F32)
            by_head = lambda ts: jnp.concatenate([jnp.concatenate(ts[0::2], axis=1), jnp.concatenate(ts[1::2], axis=1)],
                                                 axis=0)
            rows = pl.ds(k0, nblk * ABLK)
            dk_s[rows, :] += lax.dot_general(by_head(dzs), blk["q_st"], tn, preferred_element_type=F32)
            dv_s[rows, :] += lax.dot_general(by_head(ws), blk["do_st"], tn, preferred_element_type=F32)

        def below(st, blk, i, n_top):
            gone = jnp.min(rs_ref[pl.ds(blk["q0"], ABLK), :], axis=0, keepdims=True) >= EXP_ZERO_AT
            lane1 = lax.broadcasted_iota(jnp.int32, (1, 128), 1)
            first = jnp.sum(jnp.where(gone[:, :128] & gone[:, 128:] & (lane1 < i), 1.0, 0.0)).astype(jnp.int32)
            i_low = i - n_top
            n_grp = i_low // ATT_GROUP
            rem = i_low - ATT_GROUP * n_grp
            for p in reversed(_pow2_below(ATT_GROUP)):
                @pl.when(((rem & p) != 0) & ((rem & (p - 1)) + p > first))
                def _():
                    step(st, blk, rem & (p - 1), p, False)

            def inner(g, c2):
                step(st, blk, rem + ATT_GROUP * g, ATT_GROUP, False)
                return c2

            lax.fori_loop(jnp.maximum(first - rem, 0) // ATT_GROUP, n_grp, inner, 0)

        def done(st, blk):
            o_ref[0, pl.ds(blk["q0"], ABLK), :] = (dq_s[st] * scale).astype(BF16)

        def single(i):
            blk = begin(0, i)
            n_top = jnp.minimum(i, ATT_TOP)
            below(0, blk, i, n_top)
            for t in range(ATT_TOP + 1):
                @pl.when(n_top == t)
                def _():
                    step(0, blk, i - t, t + 1, True)

            done(0, blk)

        def together(i):
            blks = [begin(st, i + st) for st in range(ATT_FLIGHT)]
            for st in range(ATT_FLIGHT):
                below(st, blks[st], i + st, ATT_TOP)
            _interleave(*[stages(st, blks[st], i + st - ATT_TOP, ATT_TOP + 1, True) for st in range(ATT_FLIGHT)])
            for st in range(ATT_FLIGHT):
                done(st, blks[st])

        n_head = min(ATT_TOP, nb)
        n_group = (nb - n_head) // ATT_FLIGHT
        n_single = nb - ATT_FLIGHT * n_group

        def singles(k, carry):
            single(jnp.where(k < n_head, k, nb - n_single + k))
            return carry

        def groups(j, carry):
            together(n_head + ATT_FLIGHT * j)
            return carry

        lax.fori_loop(0, n_single, singles, 0)
        lax.fori_loop(0, n_group, groups, 0)
        o_ref[1] = dk_s[...].astype(BF16)
        o_ref[2] = dv_s[...].astype(BF16)
        rider.wait(pl.program_id(0) == n_pairs - 1)

    col = lambda o: (lambda p: (0, p + o))
    return pl.pallas_call(
        body, out_shape=[SDS((3, lp, n_pairs * 128), BF16)] + rider.out_shape, grid=(n_pairs,),
        in_specs=[pl.BlockSpec((lp, 128), col(0)), pl.BlockSpec((lp, 128), col(n_pairs)),
                  pl.BlockSpec((lp, 128), col(2 * n_pairs)), pl.BlockSpec((lp, 128), col(0)),
                  pl.BlockSpec((lp, 256), col(0)),
                  pl.BlockSpec((256, 128), lambda p: (0, 0)), pl.BlockSpec((256, 256), lambda p: (0, 0))]
        + rider.in_specs(),
        out_specs=[pl.BlockSpec((3, lp, 128), lambda p: (0, 0, p))] + rider.out_specs(),
        scratch_shapes=[pltpu.VMEM((lp, 128), F32), pltpu.VMEM((lp, 128), F32),
                        pltpu.VMEM((ATT_FLIGHT, ABLK, 128), F32), pltpu.VMEM((ATT_FLIGHT, 2, ABLK, 128), F32)]
        + rider.scratch(),
        name=name, compiler_params=_cp())(qkv, qkv, qkv, d_out, rsave, tri_after, tri_before, *rider.srcs)


def _conv_fwd_dw(cacg, w, b, name, rider=None):
    lp = cacg.shape[0]
    c = cacg.shape[1] // 2
    ncb = c // 128
    nchunk = lp // ABLK
    off = CONV_PAD - (CONV_K - 1)
    rider = rider or _Rider()

    def body(a_ref, g_ref, w_ref, b_ref, *rest):
        (y_ref,), (upad,) = rider.split(rest, 1, 1)
        rider.start(pl.program_id(0) == 0)
        upad[0:CONV_PAD, :] = jnp.zeros((CONV_PAD, 128), F32)

        def fill(ch, carry):
            base = pl.multiple_of(ch * ABLK, ABLK)
            upad[pl.ds(base + CONV_PAD, ABLK), :] = a_ref[pl.ds(base, ABLK), :] * _sigmoid(g_ref[pl.ds(base, ABLK), :])
            return carry

        lax.fori_loop(0, nchunk, fill, 0)

        def comp(ch, carry):
            base = pl.multiple_of(ch * ABLK, ABLK)
            acc = jnp.zeros((ABLK, 128), F32)
            for k in range(CONV_K):
                acc = acc + upad[pl.ds(base + (off + k), ABLK), :] * w_ref[k:k + 1, :]
            y_ref[pl.ds(base, ABLK), :] = acc + b_ref[...]
            return carry

        lax.fori_loop(0, nchunk, comp, 0)
        rider.wait(pl.program_id(0) == ncb - 1)

    return pl.pallas_call(
        body, out_shape=[SDS((lp, c), F32)] + rider.out_shape, grid=(ncb,),
        in_specs=[pl.BlockSpec((lp, 128), lambda j: (0, j)), pl.BlockSpec((lp, 128), lambda j: (0, j + ncb)),
                  pl.BlockSpec((CONV_PAD, 128), lambda j: (0, j)), pl.BlockSpec((1, 128), lambda j: (0, j))]
        + rider.in_specs(),
        out_specs=[pl.BlockSpec((lp, 128), lambda j: (0, j))] + rider.out_specs(),
        scratch_shapes=[pltpu.VMEM((lp + CONV_PAD, 128), F32)] + rider.scratch(), name=name,
        compiler_params=_cp())(cacg, cacg, w, b, *rider.srcs)


def _ln_parts(x, g, b):
    mu = jnp.mean(x, axis=-1, keepdims=True)
    xc = x - mu
    rstd = lax.rsqrt(jnp.mean(xc * xc, axis=-1, keepdims=True) + EPS)
    xh = xc * rstd
    return xh, rstd, xh * g + b


def _conv_fwd_ln(yc, g, b, name):
    lp, c = yc.shape
    tm = _tile(lp, EW_ROWS, 16)

    def body(y_ref, g_ref, b_ref, o_ref):
        _, _, ln = _ln_parts(y_ref[...], g_ref[...], b_ref[...])
        o_ref[...] = (ln * _sigmoid(ln)).astype(BF16)

    return pl.pallas_call(
        body, out_shape=SDS((lp, c), BF16), grid=(lp // tm,),
        in_specs=[pl.BlockSpec((tm, c), lambda i: (i, 0)), pl.BlockSpec((1, c), lambda i: (0, 0)),
                  pl.BlockSpec((1, c), lambda i: (0, 0))],
        out_specs=pl.BlockSpec((tm, c), lambda i: (i, 0)), name=name, compiler_params=_cp())(yc, g, b)


def _conv_bwd_ln(yc, dout, g, b, name):
    lp, c = yc.shape
    tm = _tile(lp, EW_ROWS, 16)

    def body(y_ref, d_ref, g_ref, b_ref, o_ref, dg_ref, db_ref):
        @pl.when(pl.program_id(0) == 0)
        def _():
            dg_ref[...] = jnp.zeros_like(dg_ref)
            db_ref[...] = jnp.zeros_like(db_ref)

        xh, rstd, ln = _ln_parts(y_ref[...], g_ref[...], b_ref[...])
        s = _sigmoid(ln)
        dln = d_ref[...] * (s * (1.0 + ln * (1.0 - s)))
        dg_ref[...] += _sum8(dln * xh)
        db_ref[...] += _sum8(dln)
        dxh = dln * g_ref[...]
        o_ref[...] = rstd * (dxh - jnp.mean(dxh, axis=-1, keepdims=True)
                             - xh * jnp.mean(dxh * xh, axis=-1, keepdims=True))

    return pl.pallas_call(
        body, out_shape=[SDS((lp, c), F32), SDS((8, c), F32), SDS((8, c), F32)], grid=(lp // tm,),
        in_specs=[pl.BlockSpec((tm, c), lambda i: (i, 0)), pl.BlockSpec((tm, c), lambda i: (i, 0)),
                  pl.BlockSpec((1, c), lambda i: (0, 0)), pl.BlockSpec((1, c), lambda i: (0, 0))],
        out_specs=[pl.BlockSpec((tm, c), lambda i: (i, 0)), pl.BlockSpec((8, c), lambda i: (0, 0)),
                   pl.BlockSpec((8, c), lambda i: (0, 0))],
        name=name, compiler_params=_cp())(yc, dout, g, b)


def _conv_bwd_dw(dyc, cacg, w, name):
    lp, c = dyc.shape
    ncb = c // 128
    nchunk = lp // ABLK
    off = CONV_PAD - (CONV_K - 1)

    def body(dy_ref, a_ref, g_ref, w_ref, dcc_ref, dw_ref, db_ref, upad, dypad, dwacc):
        upad[0:CONV_PAD, :] = jnp.zeros((CONV_PAD, 128), F32)
        dypad[lp:lp + CONV_PAD, :] = jnp.zeros((CONV_PAD, 128), F32)
        dwacc[...] = jnp.zeros_like(dwacc)
        db_ref[...] = jnp.zeros_like(db_ref)

        def fill(ch, carry):
            base = pl.multiple_of(ch * ABLK, ABLK)
            upad[pl.ds(base + CONV_PAD, ABLK), :] = a_ref[pl.ds(base, ABLK), :] * _sigmoid(g_ref[pl.ds(base, ABLK), :])
            dypad[pl.ds(base, ABLK), :] = dy_ref[pl.ds(base, ABLK), :]
            return carry

        lax.fori_loop(0, nchunk, fill, 0)

        def comp(ch, carry):
            base = pl.multiple_of(ch * ABLK, ABLK)
            dy = dy_ref[pl.ds(base, ABLK), :]
            du = jnp.zeros((ABLK, 128), F32)
            for k in range(CONV_K):
                du = du + dypad[pl.ds(base + (CONV_K - 1 - k), ABLK), :] * w_ref[k:k + 1, :]
                dwacc[k * 8:(k + 1) * 8, :] += _sum8(dy * upad[pl.ds(base + (off + k), ABLK), :])
            db_ref[...] += _sum8(dy)
            a = a_ref[pl.ds(base, ABLK), :]
            s = _sigmoid(g_ref[pl.ds(base, ABLK), :])
            dcc_ref[0, pl.ds(base, ABLK), :] = (du * s).astype(BF16)
            dcc_ref[1, pl.ds(base, ABLK), :] = (du * a * (s * (1.0 - s))).astype(BF16)
            return carry

        lax.fori_loop(0, nchunk, comp, 0)
        dw_ref[...] = dwacc[...].reshape(CONV_PAD, 8, 128).sum(axis=1)

    return pl.pallas_call(
        body, out_shape=[SDS((2, lp, c), BF16), SDS((CONV_PAD, c), F32), SDS((8, c), F32)],
        grid=(ncb,),
        in_specs=[pl.BlockSpec((lp, 128), lambda j: (0, j)), pl.BlockSpec((lp, 128), lambda j: (0, j)),
                  pl.BlockSpec((lp, 128), lambda j: (0, j + ncb)), pl.BlockSpec((CONV_PAD, 128), lambda j: (0, j))],
        out_specs=[pl.BlockSpec((2, lp, 128), lambda j: (0, 0, j)),
                   pl.BlockSpec((CONV_PAD, 128), lambda j: (0, j)), pl.BlockSpec((8, 128), lambda j: (0, j))],
        scratch_shapes=[pltpu.VMEM((lp + CONV_PAD, 128), F32), pltpu.VMEM((lp + CONV_PAD, 128), F32),
                        pltpu.VMEM((CONV_PAD * 8, 128), F32)],
        name=name, compiler_params=_cp())(dyc, cacg, cacg, w)


def _mesh_pos():
    x, y, c = lax.axis_index("x"), lax.axis_index("y"), lax.axis_index("c")
    return x, y, c


def _peer(pos, r):
    x, y, c = pos
    px = (1 - x) if (r >> 2) & 1 else x
    py = (1 - y) if (r >> 1) & 1 else y
    pc = (1 - c) if r & 1 else c
    return (px, py, pc), 4 * px + 2 * py + pc


SIBLING = 1
OTHER_CHIPS = (2, 4, 6)


class _Job:
    def __init__(self, src, dst, scatter, src_layer=None, dst_layer=None):
        self.src, self.dst, self.scatter, self.src_layer, self.dst_layer = src, dst, scatter, src_layer, dst_layer

    def src_view(self, ins, idx):
        v = ins[self.src] if self.src_layer is None else ins[self.src].at[self.src_layer]
        return v.at[idx] if self.scatter else v

    def dst_view(self, outs, slot):
        v = outs[self.dst] if self.dst_layer is None else outs[self.dst].at[self.dst_layer]
        return v.at[slot]


def _remote(job, j, r, src, dst, to, send, recv):
    return pltpu.make_async_remote_copy(src_ref=src, dst_ref=dst, send_sem=send.at[j, r - 1], recv_sem=recv.at[j, r - 1],
                                        device_id=to, device_id_type=pl.DeviceIdType.MESH)


def _exchange_start(jobs, ins, outs, send, recv, loc):
    pos = _mesh_pos()
    me = 4 * pos[0] + 2 * pos[1] + pos[2]
    for j, job in enumerate(jobs):
        pltpu.make_async_copy(job.src_view(ins, me), job.dst_view(outs, me), loc.at[j]).start()
        for r in (range(1, N_DEV) if job.scatter else (SIBLING,) + OTHER_CHIPS):
            peer, peer_idx = _peer(pos, r)
            _remote(job, j, r, job.src_view(ins, peer_idx), job.dst_view(outs, me), peer, send, recv).start()


def _exchange_forward(jobs, ins, outs, send, recv, loc):
    pos = _mesh_pos()
    sibling, _ = _peer(pos, SIBLING)
    for j, job in enumerate(jobs):
        if job.scatter:
            continue
        for r in OTHER_CHIPS:
            peer, peer_idx = _peer(pos, r)
            slot = job.dst_view(outs, peer_idx)
            _remote(job, j, r, job.src_view(ins, peer_idx), slot, peer, send, recv).wait_recv()
            _remote(job, j, r ^ SIBLING, slot, slot, sibling, send, recv).start()


def _exchange_wait(jobs, ins, outs, send, recv, loc):
    pos = _mesh_pos()
    me = 4 * pos[0] + 2 * pos[1] + pos[2]
    for j, job in enumerate(jobs):
        for r in range(1, N_DEV):
            peer, peer_idx = _peer(pos, r)
            cp = _remote(job, j, r, job.src_view(ins, peer_idx), job.dst_view(outs, peer_idx), peer, send, recv)
            if job.scatter or r not in OTHER_CHIPS:
                cp.wait_recv()
            cp.wait_send()
        pltpu.make_async_copy(job.src_view(ins, me), job.dst_view(outs, me), loc.at[j]).wait()


def _exchange_sems(n_jobs):
    return [pltpu.SemaphoreType.DMA((n_jobs, N_DEV - 1)), pltpu.SemaphoreType.DMA((n_jobs, N_DEV - 1)),
            pltpu.SemaphoreType.DMA((n_jobs,))]


def _exchange(jobs, arrs, out_shape, name):
    n_in, n_out = len(arrs), len(out_shape)
    any_spec = pl.BlockSpec(memory_space=pl.ANY)

    def body(*refs):
        ins, outs, sems = refs[:n_in], refs[n_in:n_in + n_out], refs[n_in + n_out:]
        _exchange_start(jobs, ins, outs, *sems)
        _exchange_forward(jobs, ins, outs, *sems)
        _exchange_wait(jobs, ins, outs, *sems)

    return pl.pallas_call(
        body, out_shape=out_shape, in_specs=[any_spec] * n_in, out_specs=[any_spec] * n_out,
        scratch_shapes=_exchange_sems(len(jobs)),
        name=name, compiler_params=pltpu.CompilerParams(has_side_effects=True))(*arrs)


def _all_reduce_small(placed, cuts, rows, w, name):
    n_in = len(placed)

    def body(*refs):
        in_refs, o_refs = refs[:n_in], refs[n_in:n_in + len(cuts)]
        buf, send, recv = refs[n_in + len(cuts):]
        pos = _mesh_pos()
        me = 4 * pos[0] + 2 * pos[1] + pos[2]
        buf[me] = jnp.zeros((rows, w), F32)
        for ref, (arr, row0, col0, is_partial) in zip(in_refs, placed):
            val = ref[...].sum(axis=0, keepdims=True) if is_partial else ref[...]
            buf[me, row0:row0 + val.shape[0], col0:col0 + val.shape[1]] = val
        for r in range(1, N_DEV):
            peer, _ = _peer(pos, r)
            pltpu.make_async_remote_copy(src_ref=buf.at[me], dst_ref=buf.at[me], send_sem=send.at[r - 1],
                                         recv_sem=recv.at[r - 1], device_id=peer,
                                         device_id_type=pl.DeviceIdType.MESH).start()
        for r in range(1, N_DEV):
            peer, peer_idx = _peer(pos, r)
            cp = pltpu.make_async_remote_copy(src_ref=buf.at[me], dst_ref=buf.at[peer_idx], send_sem=send.at[r - 1],
                                              recv_sem=recv.at[r - 1], device_id=peer,
                                              device_id_type=pl.DeviceIdType.MESH)
            cp.wait_recv()
            cp.wait_send()
        acc = buf[0]
        for dev in range(1, N_DEV):
            acc = acc + buf[dev]
        for o_ref, (_, pieces) in zip(o_refs, cuts):
            for index, row0, nrows, col0, ncols in pieces:
                o_ref[index] = acc[row0:row0 + nrows, col0:col0 + ncols].reshape(o_ref.at[index].shape)

    vmem = pl.BlockSpec(memory_space=pltpu.VMEM)
    return pl.pallas_call(
        body, out_shape=[SDS(shape, F32) for shape, _ in cuts], in_specs=[vmem] * n_in, out_specs=[vmem] * len(cuts),
        scratch_shapes=[pltpu.VMEM((N_DEV, rows, w), F32), pltpu.SemaphoreType.DMA((N_DEV - 1,)),
                        pltpu.SemaphoreType.DMA((N_DEV - 1,))],
        name=name, compiler_params=pltpu.CompilerParams(has_side_effects=True))(*[p[0] for p in placed])


def _adamw_math(w, g, m, v):
    m = ADAM_B1 * m + (1.0 - ADAM_B1) * g
    v = ADAM_B2 * v + (1.0 - ADAM_B2) * (g * g)
    m_hat = m / (1.0 - ADAM_B1 ** ADAM_STEP)
    v_hat = v / (1.0 - ADAM_B2 ** ADAM_STEP)
    delta = -ADAM_LR * (m_hat / (jnp.sqrt(v_hat) + ADAM_EPS) + ADAM_WD * w)
    return delta, m, v


def _adamw_shard(parts, w, m, v, name):
    depth = len(parts)
    _, rr, cc = parts[0].shape
    tr = _tile(rr, 256, 8)
    nt = rr // tr
    part_block, blk = (N_DEV, tr, cc), pl.BlockSpec((None, tr, cc), lambda l, i: (l, i, 0))

    def body(*refs):
        p_refs = refs[:depth]
        w_ref, m_ref, v_ref, g_out, d_out, m_out, v_out = refs[depth:]
        for li in range(depth):
            @pl.when(pl.program_id(0) == li)
            def _(p_ref=p_refs[li]):
                g = p_ref[0].astype(F32)
                for dev in range(1, N_DEV):
                    g = g + p_ref[dev].astype(F32)
                delta, mm, vv = _adamw_math(w_ref[...], g, m_ref[...], v_ref[...])
                g_out[...] = g
                d_out[...] = delta
                m_out[...] = mm
                v_out[...] = vv

    def part_spec(li):
        return pl.BlockSpec(part_block, lambda l, i: (0, jnp.where(l == li, i, jnp.where(l < li, 0, nt - 1)), 0))

    return pl.pallas_call(
        body, out_shape=[SDS(w.shape, F32)] * 4, grid=(depth, nt),
        in_specs=[part_spec(li) for li in range(depth)] + [blk, blk, blk],
        out_specs=[blk] * 4, name=name, compiler_params=_cp())(*parts, w, m, v)


def _adamw_small(gs, ws, ms, vs, name):
    n = len(gs)

    def body(*refs):
        g_refs, w_refs, m_refs, v_refs = (refs[k * n:(k + 1) * n] for k in range(4))
        outs = refs[4 * n:]
        for k in range(n):
            delta, mm, vv = _adamw_math(w_refs[k][...], g_refs[k][...], m_refs[k][...], v_refs[k][...])
            outs[k][...] = delta
            outs[n + k][...] = mm
            outs[2 * n + k][...] = vv

    res = pl.pallas_call(body, out_shape=[SDS(w.shape, F32) for w in ws] * 3, name=name,
                         compiler_params=_cp())(*gs, *ws, *ms, *vs)
    return res[:n], res[n:2 * n], res[2 * n:]


def _from_cols(t):
    return jnp.transpose(t, (1, 0, 2)).reshape(t.shape[1], N_DEV * t.shape[2])


def _swap(t):
    return jnp.swapaxes(t, -1, -2)


def kernel(x, meta_tokens, mix_norm_g, w_in, conv_dw_w, conv_dw_b, conv_ln_g, conv_ln_b, w_out, ffn_norm_g, w_gate, w_up, w_down, final_norm_g, loss_target, m_meta_tokens, m_mix_norm_g, m_w_in, m_conv_dw_w, m_conv_dw_b, m_conv_ln_g, m_conv_ln_b, m_w_out, m_ffn_norm_g, m_w_gate, m_w_up, m_w_down, m_final_norm_g, v_meta_tokens, v_mix_norm_g, v_w_in, v_conv_dw_w, v_conv_dw_b, v_conv_ln_g, v_conv_ln_b, v_w_out, v_ffn_norm_g, v_w_gate, v_w_up, v_w_down, v_final_norm_g):
    depth, d, in_shard = w_in.shape
    seq = x.shape[1]
    sb = N_HEADS * HEAD_DIM
    cc = conv_dw_w.shape[2] * N_DEV
    ff = w_gate.shape[2] * N_DEV
    assert in_shard * N_DEV == 3 * sb + 2 * cc and x.shape[0] == 1
    lr = N_META + seq
    lp = -(-lr // ABLK) * ABLK
    me = 4 * lax.axis_index("x") + 2 * lax.axis_index("y") + lax.axis_index("c")

    big_names = ("w_in", "w_out", "w_gate", "w_up", "w_down")
    transposed = {"w_in": True, "w_out": False, "w_gate": True, "w_up": True, "w_down": False}
    shard = dict(w_in=_swap(w_in).astype(BF16), w_out=w_out.astype(BF16), w_gate=_swap(w_gate).astype(BF16),
                 w_up=_swap(w_up).astype(BF16), w_down=w_down.astype(BF16))

    def gather_of(keys):
        names = sorted({n for n, _ in keys}, key=big_names.index)
        jobs = [_Job(names.index(n), j, False, src_layer=i) for j, (n, i) in enumerate(keys)]
        return jobs, [shard[n] for n in names], [SDS((N_DEV,) + shard[n].shape[1:], BF16) for n, _ in keys]

    first_keys = [("w_in", 0)]

    def riding(keys):
        return _Rider(*gather_of(keys)) if keys else None

    def receive(keys, arrays):
        for (n, li), t in zip(keys, arrays):
            wl[li][n] = t.reshape(-1, d)

    jobs, srcs, out_shape = gather_of(first_keys)
    for extra in (meta_tokens, conv_dw_w):
        jobs.append(_Job(len(srcs), len(out_shape), False))
        srcs.append(extra)
        out_shape.append(SDS((N_DEV,) + extra.shape, F32))
    gathered = _exchange(jobs, srcs, out_shape, "gather_first")
    wl = [dict() for _ in range(depth)]
    receive(first_keys, gathered)
    meta_full = _from_cols(gathered[-2])
    taps = jnp.transpose(gathered[-1], (1, 2, 0, 3)).reshape(depth, CONV_K, cc)
    taps = jnp.pad(taps, ((0, 0), (0, CONV_PAD - CONV_K), (0, 0)))
    tri_fwd, tri_after, tri_before = _tri_consts()

    h = jnp.concatenate([meta_full, x[0], jnp.zeros((lp - lr, d), F32)], axis=0)
    saved = []
    for i in range(depth):
        p = wl[i]
        sv = dict(h_in=h)
        keys = [("w_out", 0)] if i == 0 else []
        qkv, cacg, hn, *arrived = _mix_in(h, mix_norm_g[i:i + 1], p["w_in"], 3 * sb, f"mix_in_{i}", riding(keys))
        receive(keys, arrived)
        keys = [("w_gate", i), ("w_up", i)]
        attn, rsave, *arrived = _attn_fwd(qkv, tri_fwd, f"attn_fwd_{i}", riding(keys))
        receive(keys, arrived)
        keys = [("w_down", 0)] if i == 0 else []
        yc, *arrived = _conv_fwd_dw(cacg, taps[i], conv_dw_b[i:i + 1], f"conv_fwd_dw_{i}", riding(keys))
        receive(keys, arrived)
        conv = _conv_fwd_ln(yc, conv_ln_g[i:i + 1], conv_ln_b[i:i + 1], f"conv_fwd_ln_{i}")
        h = _mix_out(attn, conv, p["w_out"], h, f"mix_out_{i}")
        sv.update(qkv=qkv, hn=hn, cacg=cacg, rsave=rsave, yc=yc, attn=attn, conv=conv, h_mid=h)
        nxt = i + 1
        keys = [("w_in", nxt), ("w_out", nxt), ("w_down", nxt)] if nxt < depth else []
        h, hn2, act, gate, up, *arrived = _ffn_fwd(h, ffn_norm_g[i:i + 1], p["w_gate"], p["w_up"], p["w_down"],
                                                   f"ffn_fwd_{i}", riding(keys))
        receive(keys, arrived)
        sv.update(hn2=hn2, act=act, gate=gate, up=up)
        saved.append(sv)

    tpad = jnp.pad(loss_target[0], ((N_META, lp - lr), (0, 0)))
    dh, dg_final, loss_part = _loss_head(h, tpad, final_norm_g.reshape(1, d), seq, "loss_head")

    parts = {}

    def sending(items):
        srcs = [t.reshape(N_DEV, t.shape[0] // N_DEV, d) for _, t in items]
        return _Rider([_Job(j, j, True) for j in range(len(items))], srcs, [SDS(t.shape, BF16) for t in srcs])

    def arrive(items, arrays):
        parts.update({key: t for (key, _), t in zip(items, arrays)})

    grads = [None] * depth
    from_above = []
    for i in reversed(range(depth)):
        p, sv = wl[i], saved[i]
        d_g, d_u, *got = _ffn_bwd_act(dh, sv["gate"], sv["up"], p["w_down"], f"ffn_bwd_act_{i}", sending(from_above))
        arrive(from_above, got)
        gw_down = _mm_tn(sv["act"], dh, f"grad_w_down_{i}")
        gw_gate = _mm_tn(d_g, sv["hn2"], f"grad_w_gate_{i}")
        gw_up = _mm_tn(d_u, sv["hn2"], f"grad_w_up_{i}")
        items = [(("w_down", i), gw_down)]
        dh, dg_ffn, *got = _ffn_bwd_in(d_g, d_u, p["w_gate"], p["w_up"], sv["h_mid"], ffn_norm_g[i:i + 1], dh,
                                       f"ffn_bwd_in_{i}", sending(items))
        arrive(items, got)
        gw_out = _grad_w_out(sv["attn"], sv["conv"], dh, f"grad_w_out_{i}")
        d_attn, d_conv = _mix_bwd(dh, p["w_out"], sb, cc, f"mix_bwd_{i}")
        items = [(("w_gate", i), gw_gate), (("w_up", i), gw_up)]
        dqkv, *got = _attn_bwd(sv["qkv"], d_attn, sv["rsave"], tri_after, tri_before, f"attn_bwd_{i}", sending(items))
        arrive(items, got)
        dyc, dg_ln, db_ln = _conv_bwd_ln(sv["yc"], d_conv, conv_ln_g[i:i + 1], conv_ln_b[i:i + 1], f"conv_bwd_ln_{i}")
        dcc, g_taps_i, db_conv = _conv_bwd_dw(dyc, sv["cacg"], taps[i], f"conv_bwd_dw_{i}")
        items = [(("w_out", i), gw_out)]
        gw_in, *got = _grad_w_in_t(dqkv, dcc, sv["hn"], f"grad_w_in_{i}", sending(items))
        arrive(items, got)
        from_above = [(("w_in", i), gw_in)]
        items = from_above if i == 0 else []
        dh, dg_mix, *got = _mix_bwd_in(dqkv, dcc, p["w_in"], sv["h_in"], mix_norm_g[i:i + 1], dh, f"mix_bwd_in_{i}",
                                       sending(items))
        arrive(items, got)
        grads[i] = dict(taps=g_taps_i, dg_mix=dg_mix, dg_ffn=dg_ffn, dg_ln=dg_ln, db_ln=db_ln, db_conv=db_conv)
    grad_x = dh[N_META:lr][None]

    per = d // depth
    assert depth * cc <= d and d % depth == 0 and per % 128 == 0
    placed, cuts, row = [], [], 0
    for key in ("dg_mix", "dg_ffn"):
        placed += [(grads[i][key], row + i, 0, True) for i in range(depth)]
        cuts.append(((depth, d), [(slice(0, depth), row, depth, 0, d)]))
        row += depth
    placed.append((dg_final, row, 0, True))
    cuts.append(((1, d), [(slice(0, 1), row, 1, 0, d)]))
    row += 1
    for key in ("db_conv", "dg_ln", "db_ln"):
        placed += [(grads[i][key], row, i * per, True) for i in range(depth)]
        cuts.append(((depth, cc), [(slice(i, i + 1), row, 1, i * per, cc) for i in range(depth)]))
        row += 1
    placed += [(grads[i]["taps"], row, i * per, False) for i in range(depth)]
    cuts.append(((depth, CONV_PAD, cc), [(i, row, CONV_PAD, i * per, cc) for i in range(depth)]))
    row += CONV_PAD
    placed.append((dh[:N_META], row, 0, False))
    cuts.append(((N_META, d), [(slice(0, N_META), row, N_META, 0, d)]))
    row += N_META
    placed.append((loss_part[:1], row, 0, False))
    cuts.append(((1, 128), [(slice(0, 1), row, 1, 0, 128)]))
    row += 1
    g_mix, g_ffn, g_final, g_cb, g_lg, g_lb, g_taps_full, g_meta_full, loss_sum = _all_reduce_small(
        placed, cuts, -(-row // 8) * 8, d, "reduce_small")
    loss = loss_sum[0, 0]
    csh = cc // N_DEV
    g_taps_own = lax.dynamic_slice_in_dim(g_taps_full[:, :CONV_K], me * csh, csh, axis=2)
    msh = d // N_DEV
    g_meta_own = lax.dynamic_slice_in_dim(g_meta_full, me * msh, msh, axis=1)

    row1 = lambda t: t.reshape(1, d)
    small_g = [g_meta_own, g_mix, g_taps_own, g_cb, g_lg, g_lb, g_ffn, g_final]
    small_w = [meta_tokens, mix_norm_g, conv_dw_w, conv_dw_b, conv_ln_g, conv_ln_b, ffn_norm_g, row1(final_norm_g)]
    small_m = [m_meta_tokens, m_mix_norm_g, m_conv_dw_w, m_conv_dw_b, m_conv_ln_g, m_conv_ln_b, m_ffn_norm_g,
               row1(m_final_norm_g)]
    small_v = [v_meta_tokens, v_mix_norm_g, v_conv_dw_w, v_conv_dw_b, v_conv_ln_g, v_conv_ln_b, v_ffn_norm_g,
               row1(v_final_norm_g)]
    s_delta, s_m, s_v = _adamw_small(small_g, small_w, small_m, small_v, "adamw_small")
    unrow = lambda ts: list(ts[:-1]) + [ts[-1].reshape(d)]
    small_g, s_delta, s_m, s_v = unrow(small_g), unrow(s_delta), unrow(s_m), unrow(s_v)

    big = []
    for n, w, m, v in zip(big_names, (w_in, w_out, w_gate, w_up, w_down), (m_w_in, m_w_out, m_w_gate, m_w_up, m_w_down),
                          (v_w_in, v_w_out, v_w_gate, v_w_up, v_w_down)):
        fix = _swap if transposed[n] else (lambda t: t)
        res = _adamw_shard([parts[(n, i)] for i in range(depth)], fix(w), fix(m), fix(v), f"adamw_{n}")
        big.append([fix(t) for t in res])
    b_in, b_out, b_gate, b_up, b_down = big

    def ordered(k, smalls):
        s_meta, s_mix, s_taps, s_cb, s_lg, s_lb, s_ffn, s_final = smalls
        return [s_meta, s_mix, b_in[k], s_taps, s_cb, s_lg, s_lb, b_out[k], s_ffn, b_gate[k], b_up[k], b_down[k], s_final]

    return (loss, grad_x, *ordered(0, small_g), *ordered(1, s_delta), *ordered(2, s_m), *ordered(3, s_v))
```

```python
import math

import jax
import jax.numpy as jnp
from jax import lax
from jax.experimental import pallas as pl
from jax.experimental.pallas import tpu as pltpu

F32 = jnp.float32
BF16 = jnp.bfloat16
SDS = jax.ShapeDtypeStruct

N_META = 16
N_HEADS = 8
HEAD_DIM = 64
CONV_K = 31
CONV_PAD = 32
ABLK = 128
ATT_GROUP = 4
ATT_SPLIT = 2
ATT_TOP = 2
ATT_FLIGHT = 3
EXP_ZERO_AT = 104.0
GONE = 1e30
MASKED = -1e30
FF_CHUNK = 256
FF_BLOCK = 1408
FF_ROWS = 528
EPS = 1e-6
N_DEV = 8
ADAM_LR = 0.001
ADAM_B1 = 0.9
ADAM_B2 = 0.999
ADAM_EPS = 1e-08
ADAM_WD = 0.01
ADAM_STEP = 10
MIB = 1 << 20
VMEM_LIMIT_MIB = 48
FFN_FWD_VMEM_MIB = 56
MM_ROWS = 1056
EW_ROWS = 528


def _cp(mib=None):
    return pltpu.CompilerParams(vmem_limit_bytes=(mib or VMEM_LIMIT_MIB) * MIB)


def _tile(n, cap, mult):
    best = None
    for t in range(mult, min(n, cap) + 1, mult):
        if n % t == 0:
            best = t
    assert best is not None, (n, cap, mult)
    return best


def _sum8(v):
    r, c = v.shape
    return v.reshape(r // 8, 8, c).sum(axis=0)


def _sigmoid(x):
    return 1.0 / (1.0 + jnp.exp(-x))


def _ff_block(f):
    fb = _tile(f, FF_BLOCK, 128)
    return fb, [(s0, min(FF_CHUNK, fb - s0)) for s0 in range(0, fb, FF_CHUNK)]


NT = (((1,), (1,)), ((), ()))
TN = (((0,), (0,)), ((), ()))


def _mix_in(h, g, wt, n_qkv, name, rider=None):
    lp, d = h.shape
    n = wt.shape[0]
    tm = _tile(lp, EW_ROWS, 16)
    ni = lp // tm
    rider = rider or _Rider()

    def body(h_ref, g_ref, w_ref, *rest):
        (qkv_ref, cc_ref, hn_ref), _ = rider.split(rest, 3, 0)
        rider.start(pl.program_id(0) == 0)
        if ni > 1:
            rider.forward(pl.program_id(0) == ni - 1)
        x = h_ref[...]
        r = lax.rsqrt(jnp.mean(x * x, axis=-1, keepdims=True) + EPS)
        hn = ((x * r) * g_ref[...]).astype(BF16)
        hn_ref[...] = hn
        qkv_ref[...] = lax.dot_general(hn, w_ref[0:n_qkv, :], NT, preferred_element_type=F32).astype(BF16)
        cc_ref[...] = lax.dot_general(hn, w_ref[n_qkv:n, :], NT, preferred_element_type=F32)
        rider.wait(pl.program_id(0) == ni - 1)

    return pl.pallas_call(
        body, out_shape=[SDS((lp, n_qkv), BF16), SDS((lp, n - n_qkv), F32), SDS((lp, d), BF16)] + rider.out_shape,
        grid=(ni,),
        in_specs=[pl.BlockSpec((tm, d), lambda i: (i, 0)), pl.BlockSpec((1, d), lambda i: (0, 0)),
                  pl.BlockSpec((n, d), lambda i: (0, 0))] + rider.in_specs(),
        out_specs=[pl.BlockSpec((tm, n_qkv), lambda i: (i, 0)), pl.BlockSpec((tm, n - n_qkv), lambda i: (i, 0)),
                   pl.BlockSpec((tm, d), lambda i: (i, 0))] + rider.out_specs(),
        scratch_shapes=rider.scratch(), name=name, compiler_params=_cp())(h, g, wt, *rider.srcs)


def _mix_bwd(dh, w, sb, cc, name):
    m, k = dh.shape
    tm = _tile(m, MM_ROWS, 16)

    def body(a_ref, w_ref, attn_ref, conv_ref):
        a = a_ref[...].astype(BF16)
        attn_ref[...] = lax.dot_general(a, w_ref[0:sb, :], NT, preferred_element_type=F32).astype(BF16)
        conv_ref[...] = lax.dot_general(a, w_ref[sb:sb + cc, :], NT, preferred_element_type=F32)

    return pl.pallas_call(
        body, out_shape=[SDS((m, sb), BF16), SDS((m, cc), F32)], grid=(m // tm,),
        in_specs=[pl.BlockSpec((tm, k), lambda i: (i, 0)), pl.BlockSpec((sb + cc, k), lambda i: (0, 0))],
        out_specs=[pl.BlockSpec((tm, sb), lambda i: (i, 0)), pl.BlockSpec((tm, cc), lambda i: (i, 0))],
        name=name, compiler_params=_cp())(dh, w)


def _mix_out(attn, conv, w, res, name):
    m, ka = attn.shape
    kc = conv.shape[1]
    n = w.shape[1]
    assert ka == kc
    tm = _tile(m, MM_ROWS, 16)
    tn = _tile(n, 512, 128)

    def body(a_ref, c_ref, wa_ref, wc_ref, r_ref, o_ref):
        o_ref[...] = (r_ref[...] + jnp.dot(a_ref[...], wa_ref[...], preferred_element_type=F32)
                      + jnp.dot(c_ref[...], wc_ref[...], preferred_element_type=F32))

    return pl.pallas_call(
        body, out_shape=SDS((m, n), F32), grid=(m // tm, n // tn),
        in_specs=[pl.BlockSpec((tm, ka), lambda i, j: (i, 0)), pl.BlockSpec((tm, kc), lambda i, j: (i, 0)),
                  pl.BlockSpec((ka, tn), lambda i, j: (0, j)), pl.BlockSpec((kc, tn), lambda i, j: (1, j)),
                  pl.BlockSpec((tm, tn), lambda i, j: (i, j))],
        out_specs=pl.BlockSpec((tm, tn), lambda i, j: (i, j)), name=name, compiler_params=_cp())(attn, conv, w, w, res)


def _mm_tn(a, b, name):
    l, m = a.shape
    n = b.shape[1]
    tm = _tile(m, 1408, 128)
    tn = _tile(n, 1024, 128)
    tl = _tile(l, 1408, 128)
    nl = l // tl

    def body(a_ref, b_ref, o_ref, acc):
        @pl.when(pl.program_id(2) == 0)
        def _():
            acc[...] = jnp.zeros_like(acc)

        acc[...] += lax.dot_general(a_ref[...].astype(BF16), b_ref[...].astype(BF16), TN, preferred_element_type=F32)

        @pl.when(pl.program_id(2) == nl - 1)
        def _():
            o_ref[...] = acc[...].astype(BF16)

    return pl.pallas_call(
        body, out_shape=SDS((m, n), BF16), grid=(m // tm, n // tn, nl),
        in_specs=[pl.BlockSpec((tl, tm), lambda i, j, s: (s, i)),
                  pl.BlockSpec((tl, tn), lambda i, j, s: (s, j))],
        out_specs=pl.BlockSpec((tm, tn), lambda i, j, s: (i, j)),
        scratch_shapes=[pltpu.VMEM((tm, tn), F32)], name=name, compiler_params=_cp())(a, b)


def _grad_w_out(attn, conv, dh, name):
    l, w = attn.shape
    d = dh.shape[1]
    assert conv.shape == attn.shape
    tn = _tile(d, 1024, 128)
    tl = _tile(l, 1408, 128)
    nl = l // tl

    def body(a_ref, c_ref, b_ref, o_ref, acc):
        s = pl.program_id(1)

        @pl.when(s == 0)
        def _():
            acc[...] = jnp.zeros_like(acc)

        b = b_ref[...].astype(BF16)
        acc[0] += lax.dot_general(a_ref[...], b, TN, preferred_element_type=F32)
        acc[1] += lax.dot_general(c_ref[...], b, TN, preferred_element_type=F32)

        @pl.when(s == nl - 1)
        def _():
            o_ref[...] = acc[...].astype(BF16)

    out = pl.pallas_call(
        body, out_shape=SDS((2, w, d), BF16), grid=(d // tn, nl),
        in_specs=[pl.BlockSpec((tl, w), lambda j, s: (s, 0)), pl.BlockSpec((tl, w), lambda j, s: (s, 0)),
                  pl.BlockSpec((tl, tn), lambda j, s: (s, j))],
        out_specs=pl.BlockSpec((2, w, tn), lambda j, s: (0, 0, j)),
        scratch_shapes=[pltpu.VMEM((2, w, tn), F32)], name=name, compiler_params=_cp())(attn, conv, dh)
    return out.reshape(2 * w, d)


def _grad_w_in_t(dqkv, dcc, hn, name, rider=None):
    nq, l, w = dqkv.shape
    nc = dcc.shape[0]
    d = hn.shape[1]
    assert dcc.shape[2] == w
    tn = _tile(d, 1024, 128)
    tl = _tile(l, 1408, 128)
    nl = l // tl
    nj = d // tn
    rider = rider or _Rider()

    def body(q_ref, c_ref, b_ref, *rest):
        (o_ref,), (acc,) = rider.split(rest, 1, 1)
        p, jj, s = pl.program_id(0), pl.program_id(1), pl.program_id(2)
        rider.start((p == 0) & (jj == 0) & (s == 0))

        @pl.when(s == 0)
        def _():
            acc[...] = jnp.zeros_like(acc)

        @pl.when(p < nq)
        def _():
            acc[...] += lax.dot_general(q_ref[...], b_ref[...], TN, preferred_element_type=F32)

        @pl.when(p >= nq)
        def _():
            acc[...] += lax.dot_general(c_ref[...], b_ref[...], TN, preferred_element_type=F32)

        @pl.when(s == nl - 1)
        def _():
            o_ref[...] = acc[...].astype(BF16)

        rider.wait((p == nq + nc - 1) & (jj == nj - 1) & (s == nl - 1))

    out, *arrived = pl.pallas_call(
        body, out_shape=[SDS((nq + nc, w, d), BF16)] + rider.out_shape, grid=(nq + nc, nj, nl),
        in_specs=[pl.BlockSpec((None, tl, w), lambda p, j, s: (jnp.minimum(p, nq - 1), s, 0)),
                  pl.BlockSpec((None, tl, w), lambda p, j, s: (jnp.maximum(p - nq, 0), s, 0)),
                  pl.BlockSpec((tl, tn), lambda p, j, s: (s, j))] + rider.in_specs(),
        out_specs=[pl.BlockSpec((None, w, tn), lambda p, j, s: (p, 0, j))] + rider.out_specs(),
        scratch_shapes=[pltpu.VMEM((w, tn), F32)] + rider.scratch(), name=name,
        compiler_params=_cp())(dqkv, dcc, hn, *rider.srcs)
    return [out.reshape((nq + nc) * w, d)] + arrived


def _rms_bwd_tail(acc, h_ref, g_ref, dres_ref, o_ref, dg_ref):
    x = h_ref[...]
    r = lax.rsqrt(jnp.mean(x * x, axis=-1, keepdims=True) + EPS)
    xr = x * r
    dyv = acc[...]
    gy = dyv * g_ref[...]
    o_ref[...] = dres_ref[...] + r * (gy - xr * jnp.mean(gy * xr, axis=-1, keepdims=True))
    dg_ref[...] += _sum8(dyv * xr)


def _mix_bwd_in(dqkv, dcc, wt, h, g, dres, name, rider=None):
    nq, lp, w = dqkv.shape
    nc = dcc.shape[0]
    d = h.shape[1]
    tm = _tile(lp, FF_ROWS, 16)
    ni = lp // tm
    rider = rider or _Rider()

    def body(q_ref, c_ref, w_ref, h_ref, g_ref, dres_ref, *rest):
        (o_ref, dg_ref), (acc,) = rider.split(rest, 2, 1)
        i = pl.program_id(0)
        rider.start(i == 0)

        @pl.when(i == 0)
        def _():
            dg_ref[...] = jnp.zeros_like(dg_ref)

        pieces = [q_ref[p] for p in range(nq)] + [c_ref[p] for p in range(nc)]
        total = None
        for p, piece in enumerate(pieces):
            part = jnp.dot(piece, w_ref[p * w:(p + 1) * w, :], preferred_element_type=F32)
            total = part if total is None else total + part
        acc[...] = total
        _rms_bwd_tail(acc, h_ref, g_ref, dres_ref, o_ref, dg_ref)
        rider.wait(i == ni - 1)

    return pl.pallas_call(
        body, out_shape=[SDS((lp, d), F32), SDS((8, d), F32)] + rider.out_shape, grid=(ni,),
        in_specs=[pl.BlockSpec((nq, tm, w), lambda i: (0, i, 0)),
                  pl.BlockSpec((nc, tm, w), lambda i: (0, i, 0)),
                  pl.BlockSpec(((nq + nc) * w, d), lambda i: (0, 0)),
                  pl.BlockSpec((tm, d), lambda i: (i, 0)),
                  pl.BlockSpec((1, d), lambda i: (0, 0)),
                  pl.BlockSpec((tm, d), lambda i: (i, 0))] + rider.in_specs(),
        out_specs=[pl.BlockSpec((tm, d), lambda i: (i, 0)), pl.BlockSpec((8, d), lambda i: (0, 0))]
        + rider.out_specs(),
        scratch_shapes=[pltpu.VMEM((tm, d), F32)] + rider.scratch(), name=name,
        compiler_params=_cp())(dqkv, dcc, wt, h, g, dres, *rider.srcs)


def _ffn_bwd_in(d_g, d_u, wg_t, wu_t, h, g, dres, name, rider=None):
    lp, f = d_g.shape
    d = h.shape[1]
    tm = _tile(lp, FF_ROWS, 16)
    tk = _tile(f, FF_BLOCK, 128)
    nk = f // tk
    ni = lp // tm
    rider = rider or _Rider()

    def body(dg_in, du_in, wg_ref, wu_ref, h_ref, g_ref, dres_ref, *rest):
        (o_ref, dg_ref), (acc,) = rider.split(rest, 2, 1)
        i, kk = pl.program_id(0), pl.program_id(1)
        rider.start((i == 0) & (kk == 0))

        @pl.when(kk == 0)
        def _():
            acc[...] = jnp.zeros_like(acc)

        @pl.when((kk == 0) & (i == 0))
        def _():
            dg_ref[...] = jnp.zeros_like(dg_ref)

        acc[...] += (jnp.dot(dg_in[...], wg_ref[...], preferred_element_type=F32)
                     + jnp.dot(du_in[...], wu_ref[...], preferred_element_type=F32))

        @pl.when(kk == nk - 1)
        def _():
            _rms_bwd_tail(acc, h_ref, g_ref, dres_ref, o_ref, dg_ref)

        rider.wait((i == ni - 1) & (kk == nk - 1))

    return pl.pallas_call(
        body, out_shape=[SDS((lp, d), F32), SDS((8, d), F32)] + rider.out_shape, grid=(ni, nk),
        in_specs=[pl.BlockSpec((tm, tk), lambda i, j: (i, j)), pl.BlockSpec((tm, tk), lambda i, j: (i, j)),
                  pl.BlockSpec((tk, d), lambda i, j: (j, 0)), pl.BlockSpec((tk, d), lambda i, j: (j, 0)),
                  pl.BlockSpec((tm, d), lambda i, j: (i, 0)),
                  pl.BlockSpec((1, d), lambda i, j: (0, 0)),
                  pl.BlockSpec((tm, d), lambda i, j: (i, 0))] + rider.in_specs(),
        out_specs=[pl.BlockSpec((tm, d), lambda i, j: (i, 0)), pl.BlockSpec((8, d), lambda i, j: (0, 0))]
        + rider.out_specs(),
        scratch_shapes=[pltpu.VMEM((tm, d), F32)] + rider.scratch(), name=name,
        compiler_params=_cp())(d_g, d_u, wg_t, wu_t, h, g, dres, *rider.srcs)


def _ffn_fwd(h, g, wg_t, wu_t, w_d, name, rider=None):
    lp, d = h.shape
    f = w_d.shape[0]
    tm = _tile(lp, FF_ROWS, 16)
    fb, subs = _ff_block(f)
    nc = f // fb
    ni = lp // tm
    rider = rider or _Rider()

    def body(h_ref, g_ref, wg_ref, wu_ref, wd_ref, *rest):
        (o_ref, hn_ref, act_ref, gate_ref, up_ref), (hn_s, acc) = rider.split(rest, 5, 2)
        c = pl.program_id(1)
        rider.start((pl.program_id(0) == 0) & (c == 0))
        if ni > 1:
            rider.forward((pl.program_id(0) == ni - 1) & (c == 0))

        @pl.when(c == 0)
        def _():
            x = h_ref[...]
            r = lax.rsqrt(jnp.mean(x * x, axis=-1, keepdims=True) + EPS)
            hn = ((x * r) * g_ref[...]).astype(BF16)
            hn_s[...] = hn
            hn_ref[...] = hn
            acc[...] = jnp.zeros_like(acc)

        total = None
        for s0, sw in subs:
            gg = lax.dot_general(hn_s[...], wg_ref[s0:s0 + sw, :], NT, preferred_element_type=F32)
            uu = lax.dot_general(hn_s[...], wu_ref[s0:s0 + sw, :], NT, preferred_element_type=F32)
            gate_ref[:, s0:s0 + sw] = gg
            up_ref[:, s0:s0 + sw] = uu
            act = ((gg * _sigmoid(gg)) * uu).astype(BF16)
            act_ref[:, s0:s0 + sw] = act
            part = jnp.dot(act, wd_ref[s0:s0 + sw, :], preferred_element_type=F32)
            total = part if total is None else total + part
        acc[...] += total

        @pl.when(c == nc - 1)
        def _():
            o_ref[...] = h_ref[...] + acc[...]

        rider.wait((pl.program_id(0) == ni - 1) & (c == nc - 1))

    chunk = pl.BlockSpec((fb, d), lambda i, j: (j, 0))
    wide = pl.BlockSpec((tm, fb), lambda i, j: (i, j))
    return pl.pallas_call(
        body, out_shape=[SDS((lp, d), F32), SDS((lp, d), BF16), SDS((lp, f), BF16), SDS((lp, f), F32),
                         SDS((lp, f), F32)] + rider.out_shape, grid=(ni, nc),
        in_specs=[pl.BlockSpec((tm, d), lambda i, j: (i, 0)), pl.BlockSpec((1, d), lambda i, j: (0, 0)),
                  chunk, chunk, chunk] + rider.in_specs(),
        out_specs=[pl.BlockSpec((tm, d), lambda i, j: (i, 0)), pl.BlockSpec((tm, d), lambda i, j: (i, 0)),
                   wide, wide, wide] + rider.out_specs(),
        scratch_shapes=[pltpu.VMEM((tm, d), BF16), pltpu.VMEM((tm, d), F32)] + rider.scratch(),
        name=name, compiler_params=_cp(FFN_FWD_VMEM_MIB))(h, g, wg_t, wu_t, w_d, *rider.srcs)


def _ffn_bwd_act(dh, gate, up, w_d, name, rider=None):
    lp, d = dh.shape
    f = w_d.shape[0]
    tm = _tile(lp, FF_ROWS, 16)
    fb, subs = _ff_block(f)
    nc = f // fb
    ni = lp // tm
    rider = rider or _Rider()

    def body(dh_ref, gate_ref, up_ref, wd_ref, *rest):
        (dg_out, du_out), (dh_s,) = rider.split(rest, 2, 1)
        rider.start((pl.program_id(0) == 0) & (pl.program_id(1) == 0))

        @pl.when(pl.program_id(1) == 0)
        def _():
            dh_s[...] = dh_ref[...].astype(BF16)

        for s0, sw in subs:
            gg = gate_ref[:, s0:s0 + sw]
            uu = up_ref[:, s0:s0 + sw]
            dact = lax.dot_general(dh_s[...], wd_ref[s0:s0 + sw, :], NT, preferred_element_type=F32)
            s = _sigmoid(gg)
            dg_out[:, s0:s0 + sw] = (dact * uu * (s * (1.0 + gg * (1.0 - s)))).astype(BF16)
            du_out[:, s0:s0 + sw] = (dact * (gg * s)).astype(BF16)
        rider.wait((pl.program_id(0) == ni - 1) & (pl.program_id(1) == nc - 1))

    wide = pl.BlockSpec((tm, fb), lambda i, j: (i, j))
    return pl.pallas_call(
        body, out_shape=[SDS((lp, f), BF16), SDS((lp, f), BF16)] + rider.out_shape, grid=(ni, nc),
        in_specs=[pl.BlockSpec((tm, d), lambda i, j: (i, 0)), wide, wide,
                  pl.BlockSpec((fb, d), lambda i, j: (j, 0))] + rider.in_specs(),
        out_specs=[wide, wide] + rider.out_specs(), scratch_shapes=[pltpu.VMEM((tm, d), BF16)] + rider.scratch(),
        name=name, compiler_params=_cp())(dh, gate, up, w_d, *rider.srcs)


def _loss_head(h, tpad, g, n_real, name):
    lp, d = h.shape
    tm = _tile(lp, EW_ROWS, 16)

    def body(h_ref, t_ref, g_ref, dh_ref, dg_ref, loss_ref):
        i = pl.program_id(0)

        @pl.when(i == 0)
        def _():
            dg_ref[...] = jnp.zeros_like(dg_ref)
            loss_ref[...] = jnp.zeros_like(loss_ref)

        x = h_ref[...]
        r = lax.rsqrt(jnp.mean(x * x, axis=-1, keepdims=True) + EPS)
        xr = x * r
        y = xr * g_ref[...]
        row = i * tm + lax.broadcasted_iota(jnp.int32, (tm, d), 0)
        valid = (row >= N_META) & (row < N_META + n_real)
        diff = jnp.where(valid, y - t_ref[...], 0.0)
        loss_ref[...] += jnp.sum(diff * diff) * (0.5 / d)
        dy = diff * (1.0 / d)
        gy = dy * g_ref[...]
        dh_ref[...] = r * (gy - xr * jnp.mean(gy * xr, axis=-1, keepdims=True))
        dg_ref[...] += _sum8(dy * xr)

    return pl.pallas_call(
        body, out_shape=[SDS((lp, d), F32), SDS((8, d), F32), SDS((8, 128), F32)], grid=(lp // tm,),
        in_specs=[pl.BlockSpec((tm, d), lambda i: (i, 0)),
                  pl.BlockSpec((tm, d), lambda i: (i, 0)),
                  pl.BlockSpec((1, d), lambda i: (0, 0))],
        out_specs=[pl.BlockSpec((tm, d), lambda i: (i, 0)),
                   pl.BlockSpec((8, d), lambda i: (0, 0)),
                   pl.BlockSpec((8, 128), lambda i: (0, 0))],
        name=name, compiler_params=_cp())(h, tpad, g)


def _tri_consts():
    j = lax.broadcasted_iota(jnp.int32, (ABLK, ABLK), 0)
    s = lax.broadcasted_iota(jnp.int32, (ABLK, ABLK), 1)
    after = (j >= s).astype(BF16)
    before = (j < s).astype(BF16)
    ones = jnp.ones((ABLK, ABLK), BF16)
    two = lambda t: jnp.concatenate([t, t], axis=0)
    return (two(jnp.concatenate([after, ones], axis=1)),
            two(after),
            two(jnp.concatenate([before, ones], axis=1)))


def _softplus(z):
    neg_abs = lax.bitcast_convert_type(lax.bitcast_convert_type(z, jnp.uint32) | jnp.uint32(0x80000000), F32)
    return jnp.log(1.0 + jnp.exp(neg_abs)) + jnp.maximum(z, 0.0)


def _split_hi_lo(m):
    hi = m.astype(BF16)
    lo = (m - hi.astype(F32)).astype(BF16)
    return jnp.concatenate([hi, lo], axis=1)


def _head_halves(t2, in_a):
    zero = jnp.zeros_like(t2)
    return jnp.concatenate([jnp.where(in_a, t2, zero), jnp.where(in_a, zero, t2)], axis=0)


def _stack_blocks(t, nblk, in_a):
    return jnp.concatenate([_head_halves(t[u * ABLK:(u + 1) * ABLK], in_a) for u in range(nblk)], axis=0)


def _interleave(*gens):
    alive = list(gens)
    while alive:
        for g in list(alive):
            if next(g, alive) is alive:
                alive.remove(g)


def _attn_scale():
    scale = 1.0 / math.sqrt(HEAD_DIM)
    assert math.frexp(scale)[0] == 0.5, "a power of two, so that scaling q in bf16 is exact"
    return scale


def _pow2_below(n):
    assert n & (n - 1) == 0
    return [p for p in (64, 32, 16, 8, 4, 2, 1) if p < n]


class _Rider:
    def __init__(self, jobs=(), srcs=(), out_shape=()):
        self.jobs, self.srcs, self.out_shape = list(jobs), list(srcs), list(out_shape)
        self.any = [pl.BlockSpec(memory_space=pl.ANY)]

    def split(self, rest, n_out, n_scratch):
        ni, no = len(self.srcs), len(self.out_shape)
        self.ins, outs = rest[:ni], rest[ni:ni + n_out]
        self.outs = rest[ni + n_out:ni + n_out + no]
        scratch = rest[ni + n_out + no:ni + n_out + no + n_scratch]
        self.sems = rest[ni + n_out + no + n_scratch:]
        return outs, scratch

    def start(self, first):
        self.forwarded = False
        if self.jobs:
            @pl.when(first)
            def _():
                _exchange_start(self.jobs, self.ins, self.outs, *self.sems)

    def forward(self, late):
        self.forwarded = True
        if self.jobs:
            @pl.when(late)
            def _():
                _exchange_forward(self.jobs, self.ins, self.outs, *self.sems)

    def wait(self, last):
        if self.jobs:
            @pl.when(last)
            def _():
                if not self.forwarded:
                    _exchange_forward(self.jobs, self.ins, self.outs, *self.sems)
                _exchange_wait(self.jobs, self.ins, self.outs, *self.sems)

    def in_specs(self):
        return self.any * len(self.srcs)

    def out_specs(self):
        return self.any * len(self.out_shape)

    def scratch(self):
        return _exchange_sems(len(self.jobs)) if self.jobs else []


def _attn_fwd(qkv, tri_fwd, name, rider=None):
    lp = qkv.shape[0]
    n_pairs = (N_HEADS * HEAD_DIM) // 128
    nb = lp // ABLK
    assert nb <= 128 and 2 * HEAD_DIM == 128
    scale = _attn_scale()
    nt = (((1,), (1,)), ((), ()))
    rider = rider or _Rider()

    def body(q_ref, k_ref, v_ref, tri_ref, *rest):
        (o_ref, rs_ref), (r_s, acc_s, rs_s) = rider.split(rest, 2, 3)
        rider.start(pl.program_id(0) == 0)
        lane = lax.broadcasted_iota(jnp.int32, (ABLK, 128), 1)
        row = lax.broadcasted_iota(jnp.int32, (ABLK, 128), 0)
        in_a = lane < HEAD_DIM
        causal = lane < row

        def begin(st, i):
            r_s[st] = jnp.zeros(r_s.shape[1:], F32)
            acc_s[st] = jnp.zeros(acc_s.shape[1:], F32)
            rs_s[st] = jnp.full(rs_s.shape[1:], GONE, F32)
            return q_ref[pl.ds(pl.multiple_of(i * ABLK, ABLK), ABLK), :] * scale

        def live(st):
            least = jnp.min(jnp.minimum(r_s[st, 0], r_s[st, 1]), axis=0, keepdims=True)
            return (least[0, 0] < EXP_ZERO_AT).astype(jnp.int32)

        def step(*args):
            _interleave(stages(*args))

        def stages(st, q2, kb0, nblk, diag):
            k0 = pl.multiple_of(kb0 * ABLK, ABLK)
            kbd = _stack_blocks(k_ref[pl.ds(k0, nblk * ABLK), :], nblk, in_a)
            vbd = _stack_blocks(v_ref[pl.ds(k0, nblk * ABLK), :], nblk, in_a)
            z = lax.dot_general(q2, kbd, nt, preferred_element_type=F32)
            ncol = 2 * nblk
            zt = [z[:, c * 128:(c + 1) * 128] for c in range(ncol)]
            if diag:
                zt = [jnp.where(causal, t, MASKED) if c >= ncol - 2 else t for c, t in enumerate(zt)]
            yield
            bounds = [ncol * j // ATT_SPLIT for j in range(ATT_SPLIT + 1)]
            ce = [None] * ncol
            for c0, c1 in reversed([b for b in zip(bounds[:-1], bounds[1:]) if b[0] < b[1]]):
                parts = [_split_hi_lo(_softplus(zt[c])) for c in range(c0, c1)]
                got = jnp.dot(jnp.concatenate(parts, axis=0), tri_ref[...], preferred_element_type=F32)
                for c in range(c0, c1):
                    ce[c] = got[(c - c0) * 128:(c - c0 + 1) * 128]
                yield
            rr = [r_s[st, 0], r_s[st, 1]]
            rsv = [rs_s[st, :, :128], rs_s[st, :, 128:]]
            ws = [None] * ncol
            for u in reversed(range(nblk)):
                for hh in range(2):
                    c = 2 * u + hh
                    ws[c] = jnp.exp(zt[c] - ce[c][:, :128] - rr[hh]).astype(BF16)
                    rsv[hh] = jnp.where(lane == kb0 + u, rr[hh], rsv[hh])
                    rr[hh] = rr[hh] + ce[c][:, 128:]
            acc_s[st] += jnp.dot(jnp.concatenate(ws, axis=1), vbd, preferred_element_type=F32)
            r_s[st, 0] = rr[0]
            r_s[st, 1] = rr[1]
            rs_s[st, :, :128] = rsv[0]
            rs_s[st, :, 128:] = rsv[1]

        def finish(st, q2, i, n_top):
            i_low = i - n_top
            n_grp = i_low // ATT_GROUP

            def more(c):
                return (c[0] < n_grp) & (c[1] > 0)

            def inner(c):
                step(st, q2, i_low - ATT_GROUP * (c[0] + 1), ATT_GROUP, False)
                return c[0] + 1, live(st)

            _, alive = lax.while_loop(more, inner, (jnp.int32(0), live(st)))
            rem = i_low - ATT_GROUP * n_grp
            for p in _pow2_below(ATT_GROUP):
                def last_steps(p=p):
                    step(st, q2, rem & (p - 1), p, False)
                    return live(st)

                alive = lax.cond(((rem & p) != 0) & (alive > 0), last_steps, lambda alive=alive: alive)

            q0 = pl.multiple_of(i * ABLK, ABLK)
            o_ref[pl.ds(q0, ABLK), :] = acc_s[st].astype(BF16)
            rs_ref[pl.ds(q0, ABLK), :] = rs_s[st]

        def single(i):
            q2 = begin(0, i)
            n_top = jnp.minimum(i, ATT_TOP)
            for t in range(ATT_TOP + 1):
                @pl.when(n_top == t)
                def _():
                    step(0, q2, i - t, t + 1, True)

            finish(0, q2, i, n_top)

        def together(i):
            qs = [begin(st, i + st) for st in range(ATT_FLIGHT)]
            _interleave(*[stages(st, qs[st], i + st - ATT_TOP, ATT_TOP + 1, True) for st in range(ATT_FLIGHT)])
            for st in range(ATT_FLIGHT):
                finish(st, qs[st], i + st, ATT_TOP)

        n_head = min(ATT_TOP, nb)
        n_group = (nb - n_head) // ATT_FLIGHT
        n_single = nb - ATT_FLIGHT * n_group

        def singles(k, carry):
            single(jnp.where(k < n_head, k, nb - n_single + k))
            return carry

        def groups(j, carry):
            together(n_head + ATT_FLIGHT * j)
            return carry

        lax.fori_loop(0, n_single, singles, 0)
        lax.fori_loop(0, n_group, groups, 0)
        rider.wait(pl.program_id(0) == n_pairs - 1)

    col = lambda o: (lambda p: (0, p + o))
    return pl.pallas_call(
        body, out_shape=[SDS((lp, n_pairs * 128), BF16), SDS((lp, n_pairs * 256), F32)] + rider.out_shape,
        grid=(n_pairs,),
        in_specs=[pl.BlockSpec((lp, 128), col(0)), pl.BlockSpec((lp, 128), col(n_pairs)),
                  pl.BlockSpec((lp, 128), col(2 * n_pairs)), pl.BlockSpec((256, 256), lambda p: (0, 0))]
        + rider.in_specs(),
        out_specs=[pl.BlockSpec((lp, 128), col(0)), pl.BlockSpec((lp, 256), col(0))] + rider.out_specs(),
        scratch_shapes=[pltpu.VMEM((ATT_FLIGHT, 2, ABLK, 128), F32), pltpu.VMEM((ATT_FLIGHT, ABLK, 128), F32),
                        pltpu.VMEM((ATT_FLIGHT, ABLK, 256), F32)] + rider.scratch(),
        name=name, compiler_params=_cp())(qkv, qkv, qkv, tri_fwd, *rider.srcs)


def _attn_bwd(qkv, d_out, rsave, tri_after, tri_before, name, rider=None):
    lp = qkv.shape[0]
    n_pairs = (N_HEADS * HEAD_DIM) // 128
    nb = lp // ABLK
    scale = _attn_scale()
    nt = (((1,), (1,)), ((), ()))
    tn = (((0,), (0,)), ((), ()))
    rider = rider or _Rider()

    def body(q_ref, k_ref, v_ref, do_ref, rs_ref, ta_ref, tb_ref, *rest):
        (o_ref,), (dk_s, dv_s, dq_s, pc_s) = rider.split(rest, 1, 4)
        rider.start(pl.program_id(0) == 0)
        lane = lax.broadcasted_iota(jnp.int32, (ABLK, 128), 1)
        row = lax.broadcasted_iota(jnp.int32, (ABLK, 128), 0)
        in_a = lane < HEAD_DIM
        causal = lane < row
        dk_s[...] = jnp.zeros_like(dk_s)
        dv_s[...] = jnp.zeros_like(dv_s)

        def begin(st, i):
            q0 = pl.multiple_of(i * ABLK, ABLK)
            q2 = q_ref[pl.ds(q0, ABLK), :] * scale
            do2 = do_ref[pl.ds(q0, ABLK), :]
            dq_s[st] = jnp.zeros(dq_s.shape[1:], F32)
            pc_s[st] = jnp.zeros(pc_s.shape[1:], F32)
            return dict(q0=q0, q2=q2, do2=do2, q_st=_head_halves(q2, in_a), do_st=_head_halves(do2, in_a))

        def step(*args):
            _interleave(stages(*args))

        def stages(st, blk, kb0, nblk, diag):
            q0, q2, do2 = blk["q0"], blk["q2"], blk["do2"]
            k0 = pl.multiple_of(kb0 * ABLK, ABLK)
            kbd = _stack_blocks(k_ref[pl.ds(k0, nblk * ABLK), :], nblk, in_a)
            vbd = _stack_blocks(v_ref[pl.ds(k0, nblk * ABLK), :], nblk, in_a)
            z = lax.dot_general(q2, kbd, nt, preferred_element_type=F32)
            dw = lax.dot_general(do2, vbd, nt, preferred_element_type=F32)
            ncol = 2 * nblk
            zt = [z[:, c * 128:(c + 1) * 128] for c in range(ncol)]
            if diag:
                zt = [jnp.where(causal, t, MASKED) if c >= ncol - 2 else t for c, t in enumerate(zt)]
            bounds = [ncol * j // ATT_SPLIT for j in range(ATT_SPLIT + 1)]
            batches = [range(c0, c1) for c0, c1 in zip(bounds[:-1], bounds[1:]) if c0 < c1]
            sps, ex, ws, dls, pe = [None] * ncol, [None] * ncol, [None] * ncol, [None] * ncol, [None] * ncol

            def mass(cols):
                for c in cols:
                    sps[c] = _softplus(zt[c])
                got = jnp.dot(jnp.concatenate([_split_hi_lo(sps[c]) for c in cols], axis=0), ta_ref[...],
                              preferred_element_type=F32)
                for j, c in enumerate(cols):
                    ex[c] = got[j * 128:(j + 1) * 128]

            def weights(cols):
                for c in cols:
                    u, hh = c // 2, c % 2
                    r_saved = jnp.sum(jnp.where(lane == kb0 + u, rs_ref[pl.ds(q0, ABLK), hh * 128:(hh + 1) * 128],
                                                0.0), axis=1, keepdims=True)
                    w = jnp.exp(zt[c] - ex[c] - r_saved)
                    ws[c] = w.astype(BF16)
                    dls[c] = dw[:, c * 128:(c + 1) * 128] * w
                got = jnp.dot(jnp.concatenate([_split_hi_lo(dls[c]) for c in cols], axis=0), tb_ref[...],
                              preferred_element_type=F32)
                for j, c in enumerate(cols):
                    pe[c] = got[j * 128:(j + 1) * 128]

            yield
            mass(batches[0])
            yield
            for j in range(len(batches)):
                if j + 1 < len(batches):
                    mass(batches[j + 1])
                    yield
                weights(batches[j])
                yield
            pc = [pc_s[st, 0], pc_s[st, 1]]
            dzs = []
            for c in range(ncol):
                hh = c % 2
                one_minus_beta = jnp.exp(-sps[c])
                dz = dls[c] * one_minus_beta - (pe[c][:, :128] + pc[hh]) * (1.0 - one_minus_beta)
                pc[hh] = pc[hh] + pe[c][:, 128:]
                dzs.append(dz.astype(BF16))
            pc_s[st, 0] = pc[0]
            pc_s[st, 1] = pc[1]
            dq_s[st] += jnp.dot(jnp.concatenate(dzs, axis=1), kbd, preferred_element_type=F32)
            by_head = lambda ts: jnp.concatenate([jnp.concatenate(ts[0::2], axis=1), jnp.concatenate(ts[1::2], axis=1)],
                                                 axis=0)
            rows = pl.ds(k0, nblk * ABLK)
            dk_s[rows, :] += lax.dot_general(by_head(dzs), blk["q_st"], tn, preferred_element_type=F32)
            dv_s[rows, :] += lax.dot_general(by_head(ws), blk["do_st"], tn, preferred_element_type=F32)

        def below(st, blk, i, n_top):
            gone = jnp.min(rs_ref[pl.ds(blk["q0"], ABLK), :], axis=0, keepdims=True) >= EXP_ZERO_AT
            lane1 = lax.broadcasted_iota(jnp.int32, (1, 128), 1)
            first = jnp.sum(jnp.where(gone[:, :128] & gone[:, 128:] & (lane1 < i), 1.0, 0.0)).astype(jnp.int32)
            i_low = i - n_top
            n_grp = i_low // ATT_GROUP
            rem = i_low - ATT_GROUP * n_grp
            for p in reversed(_pow2_below(ATT_GROUP)):
                @pl.when(((rem & p) != 0) & ((rem & (p - 1)) + p > first))
                def _():
                    step(st, blk, rem & (p - 1), p, False)

            def inner(g, c2):
                step(st, blk, rem + ATT_GROUP * g, ATT_GROUP, False)
                return c2

            lax.fori_loop(jnp.maximum(first - rem, 0) // ATT_GROUP, n_grp, inner, 0)

        def done(st, blk):
            o_ref[0, pl.ds(blk["q0"], ABLK), :] = (dq_s[st] * scale).astype(BF16)

        def single(i):
            blk = begin(0, i)
            n_top = jnp.minimum(i, ATT_TOP)
            below(0, blk, i, n_top)
            for t in range(ATT_TOP + 1):
                @pl.when(n_top == t)
                def _():
                    step(0, blk, i - t, t + 1, True)

            done(0, blk)

        def together(i):
            blks = [begin(st, i + st) for st in range(ATT_FLIGHT)]
            for st in range(ATT_FLIGHT):
                below(st, blks[st], i + st, ATT_TOP)
            _interleave(*[stages(st, blks[st], i + st - ATT_TOP, ATT_TOP + 1, True) for st in range(ATT_FLIGHT)])
            for st in range(ATT_FLIGHT):
                done(st, blks[st])

        n_head = min(ATT_TOP, nb)
        n_group = (nb - n_head) // ATT_FLIGHT
        n_single = nb - ATT_FLIGHT * n_group

        def singles(k, carry):
            single(jnp.where(k < n_head, k, nb - n_single + k))
            return carry

        def groups(j, carry):
            together(n_head + ATT_FLIGHT * j)
            return carry

        lax.fori_loop(0, n_single, singles, 0)
        lax.fori_loop(0, n_group, groups, 0)
        o_ref[1] = dk_s[...].astype(BF16)
        o_ref[2] = dv_s[...].astype(BF16)
        rider.wait(pl.program_id(0) == n_pairs - 1)

    col = lambda o: (lambda p: (0, p + o))
    return pl.pallas_call(
        body, out_shape=[SDS((3, lp, n_pairs * 128), BF16)] + rider.out_shape, grid=(n_pairs,),
        in_specs=[pl.BlockSpec((lp, 128), col(0)), pl.BlockSpec((lp, 128), col(n_pairs)),
                  pl.BlockSpec((lp, 128), col(2 * n_pairs)), pl.BlockSpec((lp, 128), col(0)),
                  pl.BlockSpec((lp, 256), col(0)),
                  pl.BlockSpec((256, 128), lambda p: (0, 0)), pl.BlockSpec((256, 256), lambda p: (0, 0))]
        + rider.in_specs(),
        out_specs=[pl.BlockSpec((3, lp, 128), lambda p: (0, 0, p))] + rider.out_specs(),
        scratch_shapes=[pltpu.VMEM((lp, 128), F32), pltpu.VMEM((lp, 128), F32),
                        pltpu.VMEM((ATT_FLIGHT, ABLK, 128), F32), pltpu.VMEM((ATT_FLIGHT, 2, ABLK, 128), F32)]
        + rider.scratch(),
        name=name, compiler_params=_cp())(qkv, qkv, qkv, d_out, rsave, tri_after, tri_before, *rider.srcs)


def _conv_fwd_dw(cacg, w, b, name, rider=None):
    lp = cacg.shape[0]
    c = cacg.shape[1] // 2
    ncb = c // 128
    nchunk = lp // ABLK
    off = CONV_PAD - (CONV_K - 1)
    rider = rider or _Rider()

    def body(a_ref, g_ref, w_ref, b_ref, *rest):
        (y_ref,), (upad,) = rider.split(rest, 1, 1)
        rider.start(pl.program_id(0) == 0)
        upad[0:CONV_PAD, :] = jnp.zeros((CONV_PAD, 128), F32)

        def fill(ch, carry):
            base = pl.multiple_of(ch * ABLK, ABLK)
            upad[pl.ds(base + CONV_PAD, ABLK), :] = a_ref[pl.ds(base, ABLK), :] * _sigmoid(g_ref[pl.ds(base, ABLK), :])
            return carry

        lax.fori_loop(0, nchunk, fill, 0)

        def comp(ch, carry):
            base = pl.multiple_of(ch * ABLK, ABLK)
            acc = jnp.zeros((ABLK, 128), F32)
            for k in range(CONV_K):
                acc = acc + upad[pl.ds(base + (off + k), ABLK), :] * w_ref[k:k + 1, :]
            y_ref[pl.ds(base, ABLK), :] = acc + b_ref[...]
            return carry

        lax.fori_loop(0, nchunk, comp, 0)
        rider.wait(pl.program_id(0) == ncb - 1)

    return pl.pallas_call(
        body, out_shape=[SDS((lp, c), F32)] + rider.out_shape, grid=(ncb,),
        in_specs=[pl.BlockSpec((lp, 128), lambda j: (0, j)), pl.BlockSpec((lp, 128), lambda j: (0, j + ncb)),
                  pl.BlockSpec((CONV_PAD, 128), lambda j: (0, j)), pl.BlockSpec((1, 128), lambda j: (0, j))]
        + rider.in_specs(),
        out_specs=[pl.BlockSpec((lp, 128), lambda j: (0, j))] + rider.out_specs(),
        scratch_shapes=[pltpu.VMEM((lp + CONV_PAD, 128), F32)] + rider.scratch(), name=name,
        compiler_params=_cp())(cacg, cacg, w, b, *rider.srcs)


def _ln_parts(x, g, b):
    mu = jnp.mean(x, axis=-1, keepdims=True)
    xc = x - mu
    rstd = lax.rsqrt(jnp.mean(xc * xc, axis=-1, keepdims=True) + EPS)
    xh = xc * rstd
    return xh, rstd, xh * g + b


def _conv_fwd_ln(yc, g, b, name):
    lp, c = yc.shape
    tm = _tile(lp, EW_ROWS, 16)

    def body(y_ref, g_ref, b_ref, o_ref):
        _, _, ln = _ln_parts(y_ref[...], g_ref[...], b_ref[...])
        o_ref[...] = (ln * _sigmoid(ln)).astype(BF16)

    return pl.pallas_call(
        body, out_shape=SDS((lp, c), BF16), grid=(lp // tm,),
        in_specs=[pl.BlockSpec((tm, c), lambda i: (i, 0)), pl.BlockSpec((1, c), lambda i: (0, 0)),
                  pl.BlockSpec((1, c), lambda i: (0, 0))],
        out_specs=pl.BlockSpec((tm, c), lambda i: (i, 0)), name=name, compiler_params=_cp())(yc, g, b)


def _conv_bwd_ln(yc, dout, g, b, name):
    lp, c = yc.shape
    tm = _tile(lp, EW_ROWS, 16)

    def body(y_ref, d_ref, g_ref, b_ref, o_ref, dg_ref, db_ref):
        @pl.when(pl.program_id(0) == 0)
        def _():
            dg_ref[...] = jnp.zeros_like(dg_ref)
            db_ref[...] = jnp.zeros_like(db_ref)

        xh, rstd, ln = _ln_parts(y_ref[...], g_ref[...], b_ref[...])
        s = _sigmoid(ln)
        dln = d_ref[...] * (s * (1.0 + ln * (1.0 - s)))
        dg_ref[...] += _sum8(dln * xh)
        db_ref[...] += _sum8(dln)
        dxh = dln * g_ref[...]
        o_ref[...] = rstd * (dxh - jnp.mean(dxh, axis=-1, keepdims=True)
                             - xh * jnp.mean(dxh * xh, axis=-1, keepdims=True))

    return pl.pallas_call(
        body, out_shape=[SDS((lp, c), F32), SDS((8, c), F32), SDS((8, c), F32)], grid=(lp // tm,),
        in_specs=[pl.BlockSpec((tm, c), lambda i: (i, 0)), pl.BlockSpec((tm, c), lambda i: (i, 0)),
                  pl.BlockSpec((1, c), lambda i: (0, 0)), pl.BlockSpec((1, c), lambda i: (0, 0))],
        out_specs=[pl.BlockSpec((tm, c), lambda i: (i, 0)), pl.BlockSpec((8, c), lambda i: (0, 0)),
                   pl.BlockSpec((8, c), lambda i: (0, 0))],
        name=name, compiler_params=_cp())(yc, dout, g, b)


def _conv_bwd_dw(dyc, cacg, w, name):
    lp, c = dyc.shape
    ncb = c // 128
    nchunk = lp // ABLK
    off = CONV_PAD - (CONV_K - 1)

    def body(dy_ref, a_ref, g_ref, w_ref, dcc_ref, dw_ref, db_ref, upad, dypad, dwacc):
        upad[0:CONV_PAD, :] = jnp.zeros((CONV_PAD, 128), F32)
        dypad[lp:lp + CONV_PAD, :] = jnp.zeros((CONV_PAD, 128), F32)
        dwacc[...] = jnp.zeros_like(dwacc)
        db_ref[...] = jnp.zeros_like(db_ref)

        def fill(ch, carry):
            base = pl.multiple_of(ch * ABLK, ABLK)
            upad[pl.ds(base + CONV_PAD, ABLK), :] = a_ref[pl.ds(base, ABLK), :] * _sigmoid(g_ref[pl.ds(base, ABLK), :])
            dypad[pl.ds(base, ABLK), :] = dy_ref[pl.ds(base, ABLK), :]
            return carry

        lax.fori_loop(0, nchunk, fill, 0)

        def comp(ch, carry):
            base = pl.multiple_of(ch * ABLK, ABLK)
            dy = dy_ref[pl.ds(base, ABLK), :]
            du = jnp.zeros((ABLK, 128), F32)
            for k in range(CONV_K):
                du = du + dypad[pl.ds(base + (CONV_K - 1 - k), ABLK), :] * w_ref[k:k + 1, :]
                dwacc[k * 8:(k + 1) * 8, :] += _sum8(dy * upad[pl.ds(base + (off + k), ABLK), :])
            db_ref[...] += _sum8(dy)
            a = a_ref[pl.ds(base, ABLK), :]
            s = _sigmoid(g_ref[pl.ds(base, ABLK), :])
            dcc_ref[0, pl.ds(base, ABLK), :] = (du * s).astype(BF16)
            dcc_ref[1, pl.ds(base, ABLK), :] = (du * a * (s * (1.0 - s))).astype(BF16)
            return carry

        lax.fori_loop(0, nchunk, comp, 0)
        dw_ref[...] = dwacc[...].reshape(CONV_PAD, 8, 128).sum(axis=1)

    return pl.pallas_call(
        body, out_shape=[SDS((2, lp, c), BF16), SDS((CONV_PAD, c), F32), SDS((8, c), F32)],
        grid=(ncb,),
        in_specs=[pl.BlockSpec((lp, 128), lambda j: (0, j)), pl.BlockSpec((lp, 128), lambda j: (0, j)),
                  pl.BlockSpec((lp, 128), lambda j: (0, j + ncb)), pl.BlockSpec((CONV_PAD, 128), lambda j: (0, j))],
        out_specs=[pl.BlockSpec((2, lp, 128), lambda j: (0, 0, j)),
                   pl.BlockSpec((CONV_PAD, 128), lambda j: (0, j)), pl.BlockSpec((8, 128), lambda j: (0, j))],
        scratch_shapes=[pltpu.VMEM((lp + CONV_PAD, 128), F32), pltpu.VMEM((lp + CONV_PAD, 128), F32),
                        pltpu.VMEM((CONV_PAD * 8, 128), F32)],
        name=name, compiler_params=_cp())(dyc, cacg, cacg, w)


def _mesh_pos():
    x, y, c = lax.axis_index("x"), lax.axis_index("y"), lax.axis_index("c")
    return x, y, c


def _peer(pos, r):
    x, y, c = pos
    px = (1 - x) if (r >> 2) & 1 else x
    py = (1 - y) if (r >> 1) & 1 else y
    pc = (1 - c) if r & 1 else c
    return (px, py, pc), 4 * px + 2 * py + pc


SIBLING = 1
OTHER_CHIPS = (2, 4, 6)


class _Job:
    def __init__(self, src, dst, scatter, src_layer=None, dst_layer=None):
        self.src, self.dst, self.scatter, self.src_layer, self.dst_layer = src, dst, scatter, src_layer, dst_layer

    def src_view(self, ins, idx):
        v = ins[self.src] if self.src_layer is None else ins[self.src].at[self.src_layer]
        return v.at[idx] if self.scatter else v

    def dst_view(self, outs, slot):
        v = outs[self.dst] if self.dst_layer is None else outs[self.dst].at[self.dst_layer]
        return v.at[slot]


def _remote(job, j, r, src, dst, to, send, recv):
    return pltpu.make_async_remote_copy(src_ref=src, dst_ref=dst, send_sem=send.at[j, r - 1], recv_sem=recv.at[j, r - 1],
                                        device_id=to, device_id_type=pl.DeviceIdType.MESH)


def _exchange_start(jobs, ins, outs, send, recv, loc):
    pos = _mesh_pos()
    me = 4 * pos[0] + 2 * pos[1] + pos[2]
    for j, job in enumerate(jobs):
        pltpu.make_async_copy(job.src_view(ins, me), job.dst_view(outs, me), loc.at[j]).start()
        for r in (range(1, N_DEV) if job.scatter else (SIBLING,) + OTHER_CHIPS):
            peer, peer_idx = _peer(pos, r)
            _remote(job, j, r, job.src_view(ins, peer_idx), job.dst_view(outs, me), peer, send, recv).start()


def _exchange_forward(jobs, ins, outs, send, recv, loc):
    pos = _mesh_pos()
    sibling, _ = _peer(pos, SIBLING)
    for j, job in enumerate(jobs):
        if job.scatter:
            continue
        for r in OTHER_CHIPS:
            peer, peer_idx = _peer(pos, r)
            slot = job.dst_view(outs, peer_idx)
            _remote(job, j, r, job.src_view(ins, peer_idx), slot, peer, send, recv).wait_recv()
            _remote(job, j, r ^ SIBLING, slot, slot, sibling, send, recv).start()


def _exchange_wait(jobs, ins, outs, send, recv, loc):
    pos = _mesh_pos()
    me = 4 * pos[0] + 2 * pos[1] + pos[2]
    for j, job in enumerate(jobs):
        for r in range(1, N_DEV):
            peer, peer_idx = _peer(pos, r)
            cp = _remote(job, j, r, job.src_view(ins, peer_idx), job.dst_view(outs, peer_idx), peer, send, recv)
            if job.scatter or r not in OTHER_CHIPS:
                cp.wait_recv()
            cp.wait_send()
        pltpu.make_async_copy(job.src_view(ins, me), job.dst_view(outs, me), loc.at[j]).wait()


def _exchange_sems(n_jobs):
    return [pltpu.SemaphoreType.DMA((n_jobs, N_DEV - 1)), pltpu.SemaphoreType.DMA((n_jobs, N_DEV - 1)),
            pltpu.SemaphoreType.DMA((n_jobs,))]


def _exchange(jobs, arrs, out_shape, name):
    n_in, n_out = len(arrs), len(out_shape)
    any_spec = pl.BlockSpec(memory_space=pl.ANY)

    def body(*refs):
        ins, outs, sems = refs[:n_in], refs[n_in:n_in + n_out], refs[n_in + n_out:]
        _exchange_start(jobs, ins, outs, *sems)
        _exchange_forward(jobs, ins, outs, *sems)
        _exchange_wait(jobs, ins, outs, *sems)

    return pl.pallas_call(
        body, out_shape=out_shape, in_specs=[any_spec] * n_in, out_specs=[any_spec] * n_out,
        scratch_shapes=_exchange_sems(len(jobs)),
        name=name, compiler_params=pltpu.CompilerParams(has_side_effects=True))(*arrs)


def _all_reduce_small(placed, cuts, rows, w, name):
    n_in = len(placed)

    def body(*refs):
        in_refs, o_refs = refs[:n_in], refs[n_in:n_in + len(cuts)]
        buf, send, recv = refs[n_in + len(cuts):]
        pos = _mesh_pos()
        me = 4 * pos[0] + 2 * pos[1] + pos[2]
        buf[me] = jnp.zeros((rows, w), F32)
        for ref, (arr, row0, col0, is_partial) in zip(in_refs, placed):
            val = ref[...].sum(axis=0, keepdims=True) if is_partial else ref[...]
            buf[me, row0:row0 + val.shape[0], col0:col0 + val.shape[1]] = val
        for r in range(1, N_DEV):
            peer, _ = _peer(pos, r)
            pltpu.make_async_remote_copy(src_ref=buf.at[me], dst_ref=buf.at[me], send_sem=send.at[r - 1],
                                         recv_sem=recv.at[r - 1], device_id=peer,
                                         device_id_type=pl.DeviceIdType.MESH).start()
        for r in range(1, N_DEV):
            peer, peer_idx = _peer(pos, r)
            cp = pltpu.make_async_remote_copy(src_ref=buf.at[me], dst_ref=buf.at[peer_idx], send_sem=send.at[r - 1],
                                              recv_sem=recv.at[r - 1], device_id=peer,
                                              device_id_type=pl.DeviceIdType.MESH)
            cp.wait_recv()
            cp.wait_send()
        acc = buf[0]
        for dev in range(1, N_DEV):
            acc = acc + buf[dev]
        for o_ref, (_, pieces) in zip(o_refs, cuts):
            for index, row0, nrows, col0, ncols in pieces:
                o_ref[index] = acc[row0:row0 + nrows, col0:col0 + ncols].reshape(o_ref.at[index].shape)

    vmem = pl.BlockSpec(memory_space=pltpu.VMEM)
    return pl.pallas_call(
        body, out_shape=[SDS(shape, F32) for shape, _ in cuts], in_specs=[vmem] * n_in, out_specs=[vmem] * len(cuts),
        scratch_shapes=[pltpu.VMEM((N_DEV, rows, w), F32), pltpu.SemaphoreType.DMA((N_DEV - 1,)),
                        pltpu.SemaphoreType.DMA((N_DEV - 1,))],
        name=name, compiler_params=pltpu.CompilerParams(has_side_effects=True))(*[p[0] for p in placed])


def _adamw_math(w, g, m, v):
    m = ADAM_B1 * m + (1.0 - ADAM_B1) * g
    v = ADAM_B2 * v + (1.0 - ADAM_B2) * (g * g)
    m_hat = m / (1.0 - ADAM_B1 ** ADAM_STEP)
    v_hat = v / (1.0 - ADAM_B2 ** ADAM_STEP)
    delta = -ADAM_LR * (m_hat / (jnp.sqrt(v_hat) + ADAM_EPS) + ADAM_WD * w)
    return delta, m, v


def _adamw_shard(parts, w, m, v, name):
    depth = len(parts)
    _, rr, cc = parts[0].shape
    tr = _tile(rr, 256, 8)
    nt = rr // tr
    part_block, blk = (N_DEV, tr, cc), pl.BlockSpec((None, tr, cc), lambda l, i: (l, i, 0))

    def body(*refs):
        p_refs = refs[:depth]
        w_ref, m_ref, v_ref, g_out, d_out, m_out, v_out = refs[depth:]
        for li in range(depth):
            @pl.when(pl.program_id(0) == li)
            def _(p_ref=p_refs[li]):
                g = p_ref[0].astype(F32)
                for dev in range(1, N_DEV):
                    g = g + p_ref[dev].astype(F32)
                delta, mm, vv = _adamw_math(w_ref[...], g, m_ref[...], v_ref[...])
                g_out[...] = g
                d_out[...] = delta
                m_out[...] = mm
                v_out[...] = vv

    def part_spec(li):
        return pl.BlockSpec(part_block, lambda l, i: (0, jnp.where(l == li, i, jnp.where(l < li, 0, nt - 1)), 0))

    return pl.pallas_call(
        body, out_shape=[SDS(w.shape, F32)] * 4, grid=(depth, nt),
        in_specs=[part_spec(li) for li in range(depth)] + [blk, blk, blk],
        out_specs=[blk] * 4, name=name, compiler_params=_cp())(*parts, w, m, v)


def _adamw_small(gs, ws, ms, vs, name):
    n = len(gs)

    def body(*refs):
        g_refs, w_refs, m_refs, v_refs = (refs[k * n:(k + 1) * n] for k in range(4))
        outs = refs[4 * n:]
        for k in range(n):
            delta, mm, vv = _adamw_math(w_refs[k][...], g_refs[k][...], m_refs[k][...], v_refs[k][...])
            outs[k][...] = delta
            outs[n + k][...] = mm
            outs[2 * n + k][...] = vv

    res = pl.pallas_call(body, out_shape=[SDS(w.shape, F32) for w in ws] * 3, name=name,
                         compiler_params=_cp())(*gs, *ws, *ms, *vs)
    return res[:n], res[n:2 * n], res[2 * n:]


def _from_cols(t):
    return jnp.transpose(t, (1, 0, 2)).reshape(t.shape[1], N_DEV * t.shape[2])


def _swap(t):
    return jnp.swapaxes(t, -1, -2)


def kernel(x, meta_tokens, mix_norm_g, w_in, conv_dw_w, conv_dw_b, conv_ln_g, conv_ln_b, w_out, ffn_norm_g, w_gate, w_up, w_down, final_norm_g, loss_target, m_meta_tokens, m_mix_norm_g, m_w_in, m_conv_dw_w, m_conv_dw_b, m_conv_ln_g, m_conv_ln_b, m_w_out, m_ffn_norm_g, m_w_gate, m_w_up, m_w_down, m_final_norm_g, v_meta_tokens, v_mix_norm_g, v_w_in, v_conv_dw_w, v_conv_dw_b, v_conv_ln_g, v_conv_ln_b, v_w_out, v_ffn_norm_g, v_w_gate, v_w_up, v_w_down, v_final_norm_g):
    depth, d, in_shard = w_in.shape
    seq = x.shape[1]
    sb = N_HEADS * HEAD_DIM
    cc = conv_dw_w.shape[2] * N_DEV
    ff = w_gate.shape[2] * N_DEV
    assert in_shard * N_DEV == 3 * sb + 2 * cc and x.shape[0] == 1
    lr = N_META + seq
    lp = -(-lr // ABLK) * ABLK
    me = 4 * lax.axis_index("x") + 2 * lax.axis_index("y") + lax.axis_index("c")

    big_names = ("w_in", "w_out", "w_gate", "w_up", "w_down")
    transposed = {"w_in": True, "w_out": False, "w_gate": True, "w_up": True, "w_down": False}
    shard = dict(w_in=_swap(w_in).astype(BF16), w_out=w_out.astype(BF16), w_gate=_swap(w_gate).astype(BF16),
                 w_up=_swap(w_up).astype(BF16), w_down=w_down.astype(BF16))

    def gather_of(keys):
        names = sorted({n for n, _ in keys}, key=big_names.index)
        jobs = [_Job(names.index(n), j, False, src_layer=i) for j, (n, i) in enumerate(keys)]
        return jobs, [shard[n] for n in names], [SDS((N_DEV,) + shard[n].shape[1:], BF16) for n, _ in keys]

    first_keys = [("w_in", 0)]

    def riding(keys):
        return _Rider(*gather_of(keys)) if keys else None

    def receive(keys, arrays):
        for (n, li), t in zip(keys, arrays):
            wl[li][n] = t.reshape(-1, d)

    jobs, srcs, out_shape = gather_of(first_keys)
    for extra in (meta_tokens, conv_dw_w):
        jobs.append(_Job(len(srcs), len(out_shape), False))
        srcs.append(extra)
        out_shape.append(SDS((N_DEV,) + extra.shape, F32))
    gathered = _exchange(jobs, srcs, out_shape, "gather_first")
    wl = [dict() for _ in range(depth)]
    receive(first_keys, gathered)
    meta_full = _from_cols(gathered[-2])
    taps = jnp.transpose(gathered[-1], (1, 2, 0, 3)).reshape(depth, CONV_K, cc)
    taps = jnp.pad(taps, ((0, 0), (0, CONV_PAD - CONV_K), (0, 0)))
    tri_fwd, tri_after, tri_before = _tri_consts()

    h = jnp.concatenate([meta_full, x[0], jnp.zeros((lp - lr, d), F32)], axis=0)
    saved = []
    for i in range(depth):
        p = wl[i]
        sv = dict(h_in=h)
        keys = [("w_out", 0)] if i == 0 else []
        qkv, cacg, hn, *arrived = _mix_in(h, mix_norm_g[i:i + 1], p["w_in"], 3 * sb, f"mix_in_{i}", riding(keys))
        receive(keys, arrived)
        keys = [("w_gate", i), ("w_up", i)]
        attn, rsave, *arrived = _attn_fwd(qkv, tri_fwd, f"attn_fwd_{i}", riding(keys))
        receive(keys, arrived)
        keys = [("w_down", 0)] if i == 0 else []
        yc, *arrived = _conv_fwd_dw(cacg, taps[i], conv_dw_b[i:i + 1], f"conv_fwd_dw_{i}", riding(keys))
        receive(keys, arrived)
        conv = _conv_fwd_ln(yc, conv_ln_g[i:i + 1], conv_ln_b[i:i + 1], f"conv_fwd_ln_{i}")
        h = _mix_out(attn, conv, p["w_out"], h, f"mix_out_{i}")
        sv.update(qkv=qkv, hn=hn, cacg=cacg, rsave=rsave, yc=yc, attn=attn, conv=conv, h_mid=h)
        nxt = i + 1
        keys = [("w_in", nxt), ("w_out", nxt), ("w_down", nxt)] if nxt < depth else []
        h, hn2, act, gate, up, *arrived = _ffn_fwd(h, ffn_norm_g[i:i + 1], p["w_gate"], p["w_up"], p["w_down"],
                                                   f"ffn_fwd_{i}", riding(keys))
        receive(keys, arrived)
        sv.update(hn2=hn2, act=act, gate=gate, up=up)
        saved.append(sv)

    tpad = jnp.pad(loss_target[0], ((N_META, lp - lr), (0, 0)))
    dh, dg_final, loss_part = _loss_head(h, tpad, final_norm_g.reshape(1, d), seq, "loss_head")

    parts = {}

    def sending(items):
        srcs = [t.reshape(N_DEV, t.shape[0] // N_DEV, d) for _, t in items]
        return _Rider([_Job(j, j, True) for j in range(len(items))], srcs, [SDS(t.shape, BF16) for t in srcs])

    def arrive(items, arrays):
        parts.update({key: t for (key, _), t in zip(items, arrays)})

    grads = [None] * depth
    from_above = []
    for i in reversed(range(depth)):
        p, sv = wl[i], saved[i]
        d_g, d_u, *got = _ffn_bwd_act(dh, sv["gate"], sv["up"], p["w_down"], f"ffn_bwd_act_{i}", sending(from_above))
        arrive(from_above, got)
        gw_down = _mm_tn(sv["act"], dh, f"grad_w_down_{i}")
        gw_gate = _mm_tn(d_g, sv["hn2"], f"grad_w_gate_{i}")
        gw_up = _mm_tn(d_u, sv["hn2"], f"grad_w_up_{i}")
        items = [(("w_down", i), gw_down)]
        dh, dg_ffn, *got = _ffn_bwd_in(d_g, d_u, p["w_gate"], p["w_up"], sv["h_mid"], ffn_norm_g[i:i + 1], dh,
                                       f"ffn_bwd_in_{i}", sending(items))
        arrive(items, got)
        gw_out = _grad_w_out(sv["attn"], sv["conv"], dh, f"grad_w_out_{i}")
        d_attn, d_conv = _mix_bwd(dh, p["w_out"], sb, cc, f"mix_bwd_{i}")
        items = [(("w_gate", i), gw_gate), (("w_up", i), gw_up)]
        dqkv, *got = _attn_bwd(sv["qkv"], d_attn, sv["rsave"], tri_after, tri_before, f"attn_bwd_{i}", sending(items))
        arrive(items, got)
        dyc, dg_ln, db_ln = _conv_bwd_ln(sv["yc"], d_conv, conv_ln_g[i:i + 1], conv_ln_b[i:i + 1], f"conv_bwd_ln_{i}")
        dcc, g_taps_i, db_conv = _conv_bwd_dw(dyc, sv["cacg"], taps[i], f"conv_bwd_dw_{i}")
        items = [(("w_out", i), gw_out)]
        gw_in, *got = _grad_w_in_t(dqkv, dcc, sv["hn"], f"grad_w_in_{i}", sending(items))
        arrive(items, got)
        from_above = [(("w_in", i), gw_in)]
        items = from_above if i == 0 else []
        dh, dg_mix, *got = _mix_bwd_in(dqkv, dcc, p["w_in"], sv["h_in"], mix_norm_g[i:i + 1], dh, f"mix_bwd_in_{i}",
                                       sending(items))
        arrive(items, got)
        grads[i] = dict(taps=g_taps_i, dg_mix=dg_mix, dg_ffn=dg_ffn, dg_ln=dg_ln, db_ln=db_ln, db_conv=db_conv)
    grad_x = dh[N_META:lr][None]

    per = d // depth
    assert depth * cc <= d and d % depth == 0 and per % 128 == 0
    placed, cuts, row = [], [], 0
    for key in ("dg_mix", "dg_ffn"):
        placed += [(grads[i][key], row + i, 0, True) for i in range(depth)]
        cuts.append(((depth, d), [(slice(0, depth), row, depth, 0, d)]))
        row += depth
    placed.append((dg_final, row, 0, True))
    cuts.append(((1, d), [(slice(0, 1), row, 1, 0, d)]))
    row += 1
    for key in ("db_conv", "dg_ln", "db_ln"):
        placed += [(grads[i][key], row, i * per, True) for i in range(depth)]
        cuts.append(((depth, cc), [(slice(i, i + 1), row, 1, i * per, cc) for i in range(depth)]))
        row += 1
    placed += [(grads[i]["taps"], row, i * per, False) for i in range(depth)]
    cuts.append(((depth, CONV_PAD, cc), [(i, row, CONV_PAD, i * per, cc) for i in range(depth)]))
    row += CONV_PAD
    placed.append((dh[:N_META], row, 0, False))
    cuts.append(((N_META, d), [(slice(0, N_META), row, N_META, 0, d)]))
    row += N_META
    placed.append((loss_part[:1], row, 0, False))
    cuts.append(((1, 128), [(slice(0, 1), row, 1, 0, 128)]))
    row += 1
    g_mix, g_ffn, g_final, g_cb, g_lg, g_lb, g_taps_full, g_meta_full, loss_sum = _all_reduce_small(
        placed, cuts, -(-row // 8) * 8, d, "reduce_small")
    loss = loss_sum[0, 0]
    csh = cc // N_DEV
    g_taps_own = lax.dynamic_slice_in_dim(g_taps_full[:, :CONV_K], me * csh, csh, axis=2)
    msh = d // N_DEV
    g_meta_own = lax.dynamic_slice_in_dim(g_meta_full, me * msh, msh, axis=1)

    row1 = lambda t: t.reshape(1, d)
    small_g = [g_meta_own, g_mix, g_taps_own, g_cb, g_lg, g_lb, g_ffn, g_final]
    small_w = [meta_tokens, mix_norm_g, conv_dw_w, conv_dw_b, conv_ln_g, conv_ln_b, ffn_norm_g, row1(final_norm_g)]
    small_m = [m_meta_tokens, m_mix_norm_g, m_conv_dw_w, m_conv_dw_b, m_conv_ln_g, m_conv_ln_b, m_ffn_norm_g,
               row1(m_final_norm_g)]
    small_v = [v_meta_tokens, v_mix_norm_g, v_conv_dw_w, v_conv_dw_b, v_conv_ln_g, v_conv_ln_b, v_ffn_norm_g,
               row1(v_final_norm_g)]
    s_delta, s_m, s_v = _adamw_small(small_g, small_w, small_m, small_v, "adamw_small")
    unrow = lambda ts: list(ts[:-1]) + [ts[-1].reshape(d)]
    small_g, s_delta, s_m, s_v = unrow(small_g), unrow(s_delta), unrow(s_m), unrow(s_v)

    big = []
    for n, w, m, v in zip(big_names, (w_in, w_out, w_gate, w_up, w_down), (m_w_in, m_w_out, m_w_gate, m_w_up, m_w_down),
                          (v_w_in, v_w_out, v_w_gate, v_w_up, v_w_down)):
        fix = _swap if transposed[n] else (lambda t: t)
        res = _adamw_shard([parts[(n, i)] for i in range(depth)], fix(w), fix(m), fix(v), f"adamw_{n}")
        big.append([fix(t) for t in res])
    b_in, b_out, b_gate, b_up, b_down = big

    def ordered(k, smalls):
        s_meta, s_mix, s_taps, s_cb, s_lg, s_lb, s_ffn, s_final = smalls
        return [s_meta, s_mix, b_in[k], s_taps, s_cb, s_lg, s_lb, b_out[k], s_ffn, b_gate[k], b_up[k], b_down[k], s_final]

    return (loss, grad_x, *ordered(0, small_g), *ordered(1, s_delta), *ordered(2, s_m), *ordered(3, s_v))
```
